```python
import math
import jax, jax.numpy as jnp
from jax import lax
import numpy as np

D_MODEL = 1024
BATCH = 2
SEQ = 8192
DEPTH = 1

D_MIX = D_MODEL
ATTN_HEADS = 8
ATTN_KV_HEADS = 2
ATTN_HEAD_DIM = 64
ATTN_GROUP = ATTN_HEADS // ATTN_KV_HEADS
WINDOW = 128
BLOCK = 128
N_BUCKETS = 32
MAX_DISTANCE = 128
GDN_HEADS = 4
GDN_HEAD_DIM = 128
CONV_K = 5
CHUNK = 64
N_DIR = 2
D_FF = 4 * D_MODEL
EPS = 1e-6

ATTN_Q = ATTN_HEADS * ATTN_HEAD_DIM
ATTN_KV = ATTN_KV_HEADS * ATTN_HEAD_DIM
GDN_W = GDN_HEADS * GDN_HEAD_DIM
D_IN = ATTN_Q + 2 * ATTN_KV + 4 * GDN_W + 2 * N_DIR * GDN_HEADS
SPLIT_POINTS = (ATTN_Q,
                ATTN_Q + ATTN_KV,
                ATTN_Q + 2 * ATTN_KV,
                ATTN_Q + 2 * ATTN_KV + 3 * GDN_W,
                ATTN_Q + 2 * ATTN_KV + 4 * GDN_W,
                ATTN_Q + 2 * ATTN_KV + 4 * GDN_W + N_DIR * GDN_HEADS)

kernel_name = "hymba_swa_gdn_relu2_encoder"


def _rmsnorm(x, w):
    x32 = x.astype(jnp.float32)
    y = x32 * lax.rsqrt(jnp.mean(x32 * x32, axis=-1, keepdims=True) + EPS)
    return (y * w.astype(jnp.float32)).astype(x.dtype)


def _l2norm(t):
    return t * lax.rsqrt(jnp.sum(t * t, axis=-1, keepdims=True) + EPS)


def _t5_buckets(rel):
    nb = N_BUCKETS // 2
    max_exact = nb // 2
    base = jnp.where(rel > 0, nb, 0)
    n = jnp.abs(rel)
    log_ratio = jnp.log(jnp.maximum(n, 1).astype(jnp.float32) / max_exact) / math.log(MAX_DISTANCE / max_exact)
    large = jnp.minimum(max_exact + (log_ratio * (nb - max_exact)).astype(jnp.int32), nb - 1)
    return base + jnp.where(n < max_exact, n, large)


def _band_bias(rel_bias):
    rel = (np.arange(3 * BLOCK)[None, :] - BLOCK) - np.arange(BLOCK)[:, None]
    bucket = _t5_buckets(jnp.asarray(rel, dtype=jnp.int32))
    bias = jnp.transpose(rel_bias[bucket], (2, 0, 1)).astype(jnp.float32)
    return bias, np.abs(rel) <= WINDOW


def _band(t):
    B, S = t.shape[:2]
    nb = S // BLOCK
    tp = jnp.pad(t, ((0, 0), (BLOCK, BLOCK), (0, 0), (0, 0))).reshape(B, nb + 2, BLOCK, t.shape[2], t.shape[3])
    return jnp.concatenate([tp[:, :-2], tp[:, 1:-1], tp[:, 2:]], axis=2)


def _windowed_gqa(q, k, v, bias, in_window, sink):
    B, S = q.shape[:2]
    nb = S // BLOCK
    qb = q.reshape(B, nb, BLOCK, ATTN_KV_HEADS, ATTN_GROUP, ATTN_HEAD_DIM)
    kb, vb = _band(k), _band(v)
    scores = jnp.einsum("bnqkgd,bnskd->bnkgqs", qb, kb).astype(jnp.float32) * (ATTN_HEAD_DIM ** -0.5)
    scores = scores + bias.reshape(ATTN_KV_HEADS, ATTN_GROUP, BLOCK, 3 * BLOCK)[None, None]
    key_pos = np.arange(nb)[:, None] * BLOCK - BLOCK + np.arange(3 * BLOCK)[None, :]
    valid = in_window[None] & ((key_pos >= 0) & (key_pos < S))[:, None, :]
    scores = jnp.where(valid[None, :, None, None], scores, -1e30)
    sink_col = jnp.broadcast_to(
        sink.astype(jnp.float32).reshape(ATTN_KV_HEADS, ATTN_GROUP)[None, None, :, :, None, None],
        scores.shape[:-1] + (1,))
    probs = jax.nn.softmax(jnp.concatenate([scores, sink_col], axis=-1), axis=-1)[..., :-1]
    out = jnp.einsum("bnkgqs,bnskd->bnqkgd", probs.astype(v.dtype), vb)
    return out.reshape(B, S, ATTN_Q)


def _centred_conv(x, w):
    C = x.shape[-1]
    return lax.conv_general_dilated(
        x, w[:, None, :].astype(x.dtype), window_strides=(1,),
        padding=((CONV_K // 2, CONV_K // 2),),
        dimension_numbers=("NWC", "WIO", "NWC"), feature_group_count=C)


def _chunk_gated_delta_rule(q, k, v, g, beta):
    B, S, H, DK = q.shape
    DV = v.shape[-1]
    NC = S // CHUNK

    def to_chunks(t):
        return jnp.moveaxis(t.reshape((B, NC, CHUNK) + t.shape[2:]), 2, 3)

    q = to_chunks(q * (DK ** -0.5))
    k, v, g, beta = to_chunks(k), to_chunks(v), to_chunks(g), to_chunks(beta)
    gc = jnp.cumsum(g, axis=-1)
    causal = np.tril(np.ones((CHUNK, CHUNK), dtype=bool))
    strict = np.tril(np.ones((CHUNK, CHUNK), dtype=bool), -1)
    decay = jnp.exp(jnp.where(causal, gc[..., :, None] - gc[..., None, :], -jnp.inf))
    kk = jnp.einsum("bnhcd,bnhsd->bnhcs", k, k)
    a_mat = jnp.where(strict, beta[..., :, None] * kk * decay, 0.0)
    eye = jnp.eye(CHUNK, dtype=jnp.float32)
    t_mat = lax.linalg.triangular_solve(eye + a_mat, jnp.broadcast_to(eye, a_mat.shape),
                                        left_side=True, lower=True)
    w_c = jnp.einsum("bnhcs,bnhsd->bnhcd", t_mat, beta[..., None] * k * jnp.exp(gc)[..., None])
    u_c = jnp.einsum("bnhcs,bnhse->bnhce", t_mat, beta[..., None] * v)
    qk = jnp.einsum("bnhcd,bnhsd->bnhcs", q, k) * decay
    q_dec = q * jnp.exp(gc)[..., None]
    g_last = gc[..., -1]
    k_dec = k * jnp.exp(g_last[..., None] - gc)[..., None]

    def step(state, xs):
        q_c, k_c, u, w, qk_c, gl = xs
        v_new = u - jnp.einsum("bhcd,bhde->bhce", w, state)
        o = jnp.einsum("bhcd,bhde->bhce", q_c, state) + jnp.einsum("bhcs,bhse->bhce", qk_c, v_new)
        state = state * jnp.exp(gl)[..., None, None] + jnp.einsum("bhcd,bhce->bhde", k_c, v_new)
        return state, o

    xs = tuple(jnp.moveaxis(t, 1, 0) for t in (q_dec, k_dec, u_c, w_c, qk, g_last))
    state0 = jnp.zeros((B, H, DK, DV), dtype=jnp.float32)
    _, o = lax.scan(step, state0, xs)
    return jnp.transpose(o, (1, 0, 3, 2, 4)).reshape(B, S, H, DV)


def _gdn_mixer(qkv, z, a, b, conv_w, a_log, dt_bias, norm_w):
    B, S, _ = qkv.shape
    f32 = jnp.float32
    qkv = jax.nn.silu(_centred_conv(qkv, conv_w)).astype(f32)
    q, k, v = jnp.split(qkv, 3, axis=-1)
    q = _l2norm(q.reshape(B, S, GDN_HEADS, GDN_HEAD_DIM))
    k = _l2norm(k.reshape(B, S, GDN_HEADS, GDN_HEAD_DIM))
    v = v.reshape(B, S, GDN_HEADS, GDN_HEAD_DIM)
    a = a.astype(f32).reshape(B, S, N_DIR, GDN_HEADS)
    b = b.astype(f32).reshape(B, S, N_DIR, GDN_HEADS)
    g = -jnp.exp(a_log.astype(f32)) * jax.nn.softplus(a + dt_bias.astype(f32))
    beta = jax.nn.sigmoid(b)
    o_fwd = _chunk_gated_delta_rule(q, k, v, g[:, :, 0], beta[:, :, 0])
    flip = lambda t: jnp.flip(t, axis=1)
    o_bwd = flip(_chunk_gated_delta_rule(flip(q), flip(k), flip(v), flip(g[:, :, 1]), flip(beta[:, :, 1])))
    o = _rmsnorm(o_fwd + o_bwd, norm_w) * jax.nn.silu(z.astype(f32).reshape(B, S, GDN_HEADS, GDN_HEAD_DIM))
    return o.reshape(B, S, GDN_W).astype(z.dtype)


def setup_inputs(seed: int = 0) -> dict:
    key = jax.random.key(seed)
    ks = jax.random.split(key, 16)
    nrm = lambda k, shape, scale: jax.random.normal(k, shape, dtype=jnp.float32) * scale
    gain = lambda k, shape: 1.0 + 0.02 * jax.random.normal(k, shape, dtype=jnp.float32)
    dt = jnp.exp(jax.random.uniform(ks[7], (DEPTH, N_DIR, GDN_HEADS), minval=math.log(1e-3), maxval=math.log(1e-1)))
    return {
        "x": nrm(ks[0], (BATCH, SEQ, D_MODEL), 1.0),
        "norm_mix_w": gain(ks[1], (DEPTH, D_MODEL)),
        "w_in": nrm(ks[2], (DEPTH, D_MODEL, D_IN), D_MODEL ** -0.5),
        "rel_bias": nrm(ks[3], (N_BUCKETS, ATTN_HEADS), 0.5),
        "attn_sink": nrm(ks[4], (DEPTH, ATTN_HEADS), 1.0),
        "conv_w": nrm(ks[5], (DEPTH, CONV_K, 3 * GDN_W), CONV_K ** -0.5),
        "gdn_a_log": jnp.log(jax.random.uniform(ks[6], (DEPTH, N_DIR, GDN_HEADS), minval=1.0, maxval=16.0)),
        "gdn_dt_bias": dt + jnp.log(-jnp.expm1(-dt)),
        "gdn_norm_w": gain(ks[8], (DEPTH, GDN_HEAD_DIM)),
        "w_out": nrm(ks[9], (DEPTH, D_MIX, D_MODEL), D_MIX ** -0.5),
        "norm_ffn_w": gain(ks[10], (DEPTH, D_MODEL)),
        "w_ffn_in": nrm(ks[11], (DEPTH, D_MODEL, D_FF), D_MODEL ** -0.5),
        "w_ffn_out": nrm(ks[12], (DEPTH, D_FF, D_MODEL), D_FF ** -0.5),
        "norm_final_w": gain(ks[13], (D_MODEL,)),
    }


def reference(x, norm_mix_w, w_in, rel_bias, attn_sink, conv_w, gdn_a_log, gdn_dt_bias,
              gdn_norm_w, w_out, norm_ffn_w, w_ffn_in, w_ffn_out, norm_final_w):
    B, S, _ = x.shape
    band_bias, in_window = _band_bias(rel_bias)
    h = x
    for l in range(DEPTH):
        xn = _rmsnorm(h, norm_mix_w[l])
        proj = xn @ w_in[l]
        q_a, k_a, v_a, qkv_g, z_g, a_g, b_g = jnp.split(proj, SPLIT_POINTS, axis=-1)
        attn = _windowed_gqa(q_a.reshape(B, S, ATTN_HEADS, ATTN_HEAD_DIM),
                             k_a.reshape(B, S, ATTN_KV_HEADS, ATTN_HEAD_DIM),
                             v_a.reshape(B, S, ATTN_KV_HEADS, ATTN_HEAD_DIM),
                             band_bias, in_window, attn_sink[l])
        gdn = _gdn_mixer(qkv_g, z_g, a_g, b_g, conv_w[l], gdn_a_log[l], gdn_dt_bias[l], gdn_norm_w[l])
        h = h + jnp.concatenate([attn, gdn], axis=-1) @ w_out[l]
        hn = _rmsnorm(h, norm_ffn_w[l])
        h = h + jnp.square(jax.nn.relu(hn @ w_ffn_in[l])) @ w_ffn_out[l]
    return _rmsnorm(h, norm_final_w)
```

```python
import functools
import math

import jax
import jax.numpy as jnp
import numpy as np
from jax import lax
from jax.experimental import pallas as pl
from jax.experimental.pallas import tpu as pltpu

F32 = jnp.float32
BF16 = jnp.bfloat16

D_MODEL = 1024
ATTN_HEADS = 8
ATTN_KV_HEADS = 2
ATTN_HEAD_DIM = 64
ATTN_GROUP = ATTN_HEADS // ATTN_KV_HEADS
WINDOW = 128
BLOCK = 128
N_BUCKETS = 32
MAX_DISTANCE = 128
GDN_HEADS = 4
GDN_HEAD_DIM = 128
CONV_K = 5
CHUNK = 64
N_DIR = 2
N_CHAIN = N_DIR * GDN_HEADS
D_FF = 4 * D_MODEL
EPS = 1e-6
ATTN_Q = ATTN_HEADS * ATTN_HEAD_DIM
ATTN_KV = ATTN_KV_HEADS * ATTN_HEAD_DIM
GDN_W = GDN_HEADS * GDN_HEAD_DIM
LANES = 128
SUBLANES = 8
VMEM_LIMIT = 56 * 1024 * 1024

PROJ_TM = 512
PREP_T = 128
FFN_TM = 512


def _dot(a, b):
    return jnp.dot(a.astype(BF16), b.astype(BF16), preferred_element_type=F32)


def _dot_nt(a, b):
    return lax.dot_general(a.astype(BF16), b.astype(BF16), (((1,), (1,)), ((), ())),
                           preferred_element_type=F32)


def _split3(x):
    hi = x.astype(BF16)
    r1 = x - hi.astype(F32)
    mid = r1.astype(BF16)
    lo = (r1 - mid.astype(F32)).astype(BF16)
    return hi, mid, lo


def _dot01_left(m01, x):
    hi, mid, lo = _split3(x)
    d = lambda p: jnp.dot(m01, p, preferred_element_type=F32)
    return d(hi) + d(mid) + d(lo)


def _dot01_right(x, m01):
    hi, mid, lo = _split3(x)
    d = lambda p: jnp.dot(p, m01, preferred_element_type=F32)
    return d(hi) + d(mid) + d(lo)


def _proj_kernel(x_ref, nw_ref, w_ref, qkvg_ref, z_ref, qa_ref, ka_ref, va_ref, ab_ref):
    x = x_ref[...]
    ms = jnp.mean(x * x, axis=-1, keepdims=True)
    xn = (x * lax.rsqrt(ms + EPS) * nw_ref[...]).astype(BF16)
    c0 = 0
    for ref, width, scale in ((qkvg_ref, 3 * GDN_W, None), (z_ref, GDN_W, None),
                              (qa_ref, ATTN_Q, ATTN_HEAD_DIM ** -0.5), (ka_ref, ATTN_KV, None),
                              (va_ref, ATTN_KV, None), (ab_ref, LANES, None)):
        y = jnp.dot(xn, w_ref[:, c0:c0 + width], preferred_element_type=F32)
        if scale is not None:
            y = y * scale
        ref[...] = y.astype(ref.dtype)
        c0 += width


def _proj(x2, norm_w, w_cat):
    n = x2.shape[0]
    d_cat = w_cat.shape[1]
    row = lambda w: pl.BlockSpec((PROJ_TM, w), lambda i: (i, 0))
    return pl.pallas_call(
        _proj_kernel,
        grid=(n // PROJ_TM,),
        in_specs=[row(D_MODEL),
                  pl.BlockSpec((1, D_MODEL), lambda i: (0, 0)),
                  pl.BlockSpec((D_MODEL, d_cat), lambda i: (0, 0))],
        out_specs=[row(3 * GDN_W), row(GDN_W), row(ATTN_Q), row(ATTN_KV), row(ATTN_KV), row(LANES)],
        out_shape=[jax.ShapeDtypeStruct((n, 3 * GDN_W), F32),
                   jax.ShapeDtypeStruct((n, GDN_W), F32),
                   jax.ShapeDtypeStruct((n, ATTN_Q), BF16),
                   jax.ShapeDtypeStruct((n, ATTN_KV), BF16),
                   jax.ShapeDtypeStruct((n, ATTN_KV), BF16),
                   jax.ShapeDtypeStruct((n, LANES), F32)],
        compiler_params=pltpu.CompilerParams(dimension_semantics=("arbitrary",),
                                             vmem_limit_bytes=VMEM_LIMIT),
        name="proj",
    )(x2, norm_w, w_cat)


def _bias_kernel(relb_ref, bucket_ref, o_ref):
    bucket = bucket_ref[...]
    row = lax.broadcasted_iota(jnp.int32, (BLOCK, 3 * BLOCK), 0)
    col = lax.broadcasted_iota(jnp.int32, (BLOCK, 3 * BLOCK), 1)
    in_window = jnp.abs(col - BLOCK - row) <= WINDOW
    for h in range(ATTN_HEADS):
        acc = jnp.zeros((BLOCK, 3 * BLOCK), F32)
        for b in range(N_BUCKETS):
            acc = jnp.where(bucket == b, relb_ref[b, h], acc)
        o_ref[h] = jnp.where(in_window, acc, -1e30)


def _bias_band(rel_bias, bucket):
    return pl.pallas_call(
        _bias_kernel,
        in_specs=[pl.BlockSpec(memory_space=pltpu.SMEM),
                  pl.BlockSpec((BLOCK, 3 * BLOCK), lambda: (0, 0))],
        out_specs=pl.BlockSpec((ATTN_HEADS, BLOCK, 3 * BLOCK), lambda: (0, 0, 0)),
        out_shape=jax.ShapeDtypeStruct((ATTN_HEADS, BLOCK, 3 * BLOCK), F32),
        name="bias_band",
    )(rel_bias, bucket)


def _t5_buckets(rel):
    nb = N_BUCKETS // 2
    max_exact = nb // 2
    base = jnp.where(rel > 0, nb, 0)
    n = jnp.abs(rel)
    log_ratio = jnp.log(jnp.maximum(n, 1).astype(jnp.float32) / max_exact) / math.log(MAX_DISTANCE / max_exact)
    large = jnp.minimum(max_exact + (log_ratio * (nb - max_exact)).astype(jnp.int32), nb - 1)
    return base + jnp.where(n < max_exact, n, large)


def _attn_kernel(sink_ref, q_ref, kp_ref, kc_ref, kn_ref, vp_ref, vc_ref, vn_ref, bias_ref, o_ref):
    n = pl.program_id(1)
    last = pl.num_programs(1) - 1
    kband = jnp.concatenate([kp_ref[0], kc_ref[0], kn_ref[0]], axis=0)
    vband = jnp.concatenate([vp_ref[0], vc_ref[0], vn_ref[0]], axis=0)
    col = lax.broadcasted_iota(jnp.int32, (1, 3 * BLOCK), 1)
    key_bad = ((col < BLOCK) & (n == 0)) | ((col >= 2 * BLOCK) & (n == last))
    q = q_ref[0]
    outs = []
    for h in range(ATTN_HEADS):
        kv = h // ATTN_GROUP
        qh = q[:, h * ATTN_HEAD_DIM:(h + 1) * ATTN_HEAD_DIM]
        kh = kband[:, kv * ATTN_HEAD_DIM:(kv + 1) * ATTN_HEAD_DIM]
        vh = vband[:, kv * ATTN_HEAD_DIM:(kv + 1) * ATTN_HEAD_DIM]
        s = _dot_nt(qh, kh) + bias_ref[h]
        s = jnp.where(key_bad, -1e30, s)
        sink = sink_ref[h]
        m = jnp.maximum(jnp.max(s, axis=-1, keepdims=True), sink)
        p = jnp.exp(s - m)
        den = jnp.sum(p, axis=-1, keepdims=True) + jnp.exp(sink - m)
        outs.append(_dot(p, vh) / den)
    o_ref[0] = jnp.concatenate(outs, axis=-1).astype(o_ref.dtype)


def _attention(q_a, k_a, v_a, band, sink):
    b, s, _ = q_a.shape
    nb = s // BLOCK
    kv_spec = lambda f: pl.BlockSpec((1, BLOCK, ATTN_KV), f)
    prev = lambda bi, n: (bi, jnp.maximum(n - 1, 0), 0)
    cur = lambda bi, n: (bi, n, 0)
    nxt = lambda bi, n: (bi, jnp.minimum(n + 1, nb - 1), 0)
    return pl.pallas_call(
        _attn_kernel,
        grid=(b, nb),
        in_specs=[pl.BlockSpec(memory_space=pltpu.SMEM),
                  pl.BlockSpec((1, BLOCK, ATTN_Q), cur),
                  kv_spec(prev), kv_spec(cur), kv_spec(nxt),
                  kv_spec(prev), kv_spec(cur), kv_spec(nxt),
                  pl.BlockSpec((ATTN_HEADS, BLOCK, 3 * BLOCK), lambda bi, n: (0, 0, 0))],
        out_specs=pl.BlockSpec((1, BLOCK, ATTN_Q), cur),
        out_shape=jax.ShapeDtypeStruct((b, s, ATTN_Q), BF16),
        compiler_params=pltpu.CompilerParams(dimension_semantics=("arbitrary", "arbitrary"),
                                             vmem_limit_bytes=VMEM_LIMIT),
        name="attn",
    )(sink, q_a, k_a, k_a, k_a, v_a, v_a, v_a, band)


def _unit_tri_inverse(a, masks):
    eye = masks["eye"]
    t = eye - jnp.where(masks[1], a, 0.0)
    s = 2
    while s < CHUNK:
        m = jnp.where(masks[s], a, 0.0)
        t = t - _dot(_dot(t, m), t)
        s *= 2
    return t


def _gprep_kernel(xp_ref, x_ref, xn_ref, ab_ref, cw_ref, gp_ref,
                  w_ref, u_ref, qd_ref, kdt_ref, qk_ref, egl_ref):
    i = pl.program_id(1)
    last = pl.num_programs(1) - 1
    t_len = PREP_T
    halo = CONV_K // 2

    x = x_ref[0]
    prev = jnp.where(i > 0, xp_ref[0][SUBLANES - halo:, :], 0.0)
    nxt = jnp.where(i < last, xn_ref[0][:halo, :], 0.0)
    xe = jnp.concatenate([prev, x, nxt], axis=0)
    y = cw_ref[0:1, :] * xe[0:t_len]
    for j in range(1, CONV_K):
        y = y + cw_ref[j:j + 1, :] * xe[j:j + t_len]
    y = y * jax.nn.sigmoid(y)

    def l2n(t):
        return t * lax.rsqrt(jnp.sum(t * t, axis=-1, keepdims=True) + EPS)

    qs, ks, kts, vs = [], [], [], []
    for h in range(GDN_HEADS):
        sl = lambda base: y[:, base + h * GDN_HEAD_DIM: base + (h + 1) * GDN_HEAD_DIM]
        qs.append(l2n(sl(0)) * (GDN_HEAD_DIM ** -0.5))
        kh = l2n(sl(GDN_W))
        ks.append(kh)
        kts.append(kh.T)
        vs.append(sl(2 * GDN_W))

    ab = ab_ref[0]
    sp_in = ab + gp_ref[1:2, :]
    softplus = jnp.maximum(sp_in, 0.0) + jnp.log1p(jnp.exp(-jnp.abs(sp_in)))
    g = -jnp.exp(gp_ref[0:1, :]) * softplus
    beta = jax.nn.sigmoid(ab)

    r_t = lax.broadcasted_iota(jnp.int32, (t_len, t_len), 0)
    c_t = lax.broadcasted_iota(jnp.int32, (t_len, t_len), 1)
    same = (r_t // CHUNK) == (c_t // CHUNK)
    lower = jnp.where(same & (r_t >= c_t), 1.0, 0.0).astype(BF16)
    upper = jnp.where(same & (r_t <= c_t), 1.0, 0.0).astype(BF16)
    g_t = g.T
    cs_col = (_dot01_left(lower, g), _dot01_left(upper, g))
    cs_row = (_dot01_right(g_t, upper), _dot01_right(g_t, lower))

    r_c = lax.broadcasted_iota(jnp.int32, (CHUNK, CHUNK), 0)
    c_c = lax.broadcasted_iota(jnp.int32, (CHUNK, CHUNK), 1)
    masks = {"eye": jnp.where(r_c == c_c, 1.0, 0.0).astype(F32)}
    s = 1
    while s < CHUNK:
        masks[s] = ((r_c // (2 * s)) == (c_c // (2 * s))) & ((r_c // s) != (c_c // s))
        s *= 2
    incl = (r_c >= c_c, r_c <= c_c)
    strict = (r_c > c_c, r_c < c_c)

    for c in range(t_len // CHUNK):
        r0 = c * CHUNK
        egl_rows = []
        for d in range(N_DIR):
            for h in range(GDN_HEADS):
                j = d * GDN_HEADS + h
                gcol = cs_col[d][r0:r0 + CHUNK, j:j + 1]
                grow = cs_row[d][j:j + 1, r0:r0 + CHUNK]
                r_last = r0 + CHUNK - 1 if d == 0 else r0
                glast = cs_col[d][r_last:r_last + 1, j:j + 1]
                bcol = beta[r0:r0 + CHUNK, SUBLANES + j:SUBLANES + j + 1]
                qh = qs[h][r0:r0 + CHUNK]
                kh = ks[h][r0:r0 + CHUNK]
                vh = vs[h][r0:r0 + CHUNK]
                decay = jnp.exp(jnp.where(incl[d], gcol - grow, -jnp.inf))
                kh16 = kh.astype(BF16)
                a_mat = jnp.where(strict[d], bcol * _dot_nt(kh16, kh16) * decay, 0.0)
                t_mat = _unit_tri_inverse(a_mat, masks)
                eg = jnp.exp(gcol)
                kv = jnp.concatenate([kh * (bcol * eg), vh * bcol], axis=1)
                wu = _dot(t_mat, kv)
                w_ref[0, j, r0:r0 + CHUNK, :] = wu[:, :GDN_HEAD_DIM].astype(w_ref.dtype)
                u_ref[0, j, r0:r0 + CHUNK, :] = wu[:, GDN_HEAD_DIM:]
                qd_ref[0, j, r0:r0 + CHUNK, :] = (qh * eg).astype(qd_ref.dtype)
                qk_ref[0, j, r0:r0 + CHUNK, :] = (_dot_nt(qh, kh16) * decay).astype(qk_ref.dtype)
                kdt = kts[h][:, r0:r0 + CHUNK] * jnp.exp(glast - grow)
                kdt_ref[0, j, c] = kdt.astype(kdt_ref.dtype)
                egl_rows.append(jnp.broadcast_to(jnp.exp(glast), (1, LANES)))
        egl_ref[0, c] = jnp.concatenate(egl_rows, axis=0)


def _gdn_prep(qkv_g, ab, conv_w, gate_par):
    b, s, _ = qkv_g.shape
    nt = s // PREP_T
    cpb = PREP_T // CHUNK
    nc = s // CHUNK
    nh8 = PREP_T // SUBLANES
    chain = lambda last: pl.BlockSpec((1, N_CHAIN, PREP_T, last), lambda bi, i: (bi, 0, i, 0))
    return pl.pallas_call(
        _gprep_kernel,
        grid=(b, nt),
        in_specs=[pl.BlockSpec((1, SUBLANES, 3 * GDN_W), lambda bi, i: (bi, jnp.maximum(i * nh8 - 1, 0), 0)),
                  pl.BlockSpec((1, PREP_T, 3 * GDN_W), lambda bi, i: (bi, i, 0)),
                  pl.BlockSpec((1, SUBLANES, 3 * GDN_W),
                               lambda bi, i: (bi, jnp.minimum((i + 1) * nh8, s // SUBLANES - 1), 0)),
                  pl.BlockSpec((1, PREP_T, LANES), lambda bi, i: (bi, i, 0)),
                  pl.BlockSpec((SUBLANES, 3 * GDN_W), lambda bi, i: (0, 0)),
                  pl.BlockSpec((SUBLANES, LANES), lambda bi, i: (0, 0))],
        out_specs=[chain(GDN_HEAD_DIM), chain(GDN_HEAD_DIM), chain(GDN_HEAD_DIM),
                   pl.BlockSpec((1, N_CHAIN, cpb, GDN_HEAD_DIM, CHUNK), lambda bi, i: (bi, 0, i, 0, 0)),
                   chain(CHUNK),
                   pl.BlockSpec((1, cpb, N_CHAIN, LANES), lambda bi, i: (bi, i, 0, 0))],
        out_shape=[jax.ShapeDtypeStruct((b, N_CHAIN, s, GDN_HEAD_DIM), BF16),
                   jax.ShapeDtypeStruct((b, N_CHAIN, s, GDN_HEAD_DIM), F32),
                   jax.ShapeDtypeStruct((b, N_CHAIN, s, GDN_HEAD_DIM), BF16),
                   jax.ShapeDtypeStruct((b, N_CHAIN, nc, GDN_HEAD_DIM, CHUNK), BF16),
                   jax.ShapeDtypeStruct((b, N_CHAIN, s, CHUNK), BF16),
                   jax.ShapeDtypeStruct((b, nc, N_CHAIN, LANES), F32)],
        compiler_params=pltpu.CompilerParams(dimension_semantics=("arbitrary", "arbitrary"),
                                             vmem_limit_bytes=VMEM_LIMIT),
        name="gdn_prep",
    )(qkv_g, qkv_g, qkv_g, ab, conv_w, gate_par)


def _gscan_kernel(egl_ref, wf_ref, wb_ref, uf_ref, ub_ref, qf_ref, qb_ref, kf_ref, kb_ref,
                  pf_ref, pb_ref, of_ref, ob_ref, state_ref):
    bi = pl.program_id(0)
    t = pl.program_id(1)
    nc = pl.num_programs(1)

    @pl.when(t == 0)
    def _():
        state_ref[...] = jnp.zeros_like(state_ref)

    for d, (w_ref, u_ref, q_ref, k_ref, p_ref, o_ref) in enumerate(
            ((wf_ref, uf_ref, qf_ref, kf_ref, pf_ref, of_ref),
             (wb_ref, ub_ref, qb_ref, kb_ref, pb_ref, ob_ref))):
        c = t if d == 0 else nc - 1 - t
        outs = []
        for h in range(GDN_HEADS):
            j = d * GDN_HEADS + h
            st = state_ref[j]
            wq = jnp.concatenate([w_ref[0, h], q_ref[0, h]], axis=0)
            r = _dot(wq, st)
            v_new = u_ref[0, h] - r[:CHUNK]
            outs.append(r[CHUNK:] + _dot(p_ref[0, h], v_new))
            egl = egl_ref[(bi * nc + c) * N_CHAIN + j]
            state_ref[j] = st * egl + _dot(k_ref[0, h, 0], v_new)
        o_ref[0] = jnp.concatenate(outs, axis=-1)


def _gdn_scan(egl, w, u, qd, kdt, qk):
    b, _, s, _ = w.shape
    nc = s // CHUNK
    fwd = lambda last: pl.BlockSpec((1, GDN_HEADS, CHUNK, last), lambda bi, t: (bi, 0, t, 0))
    bwd = lambda last: pl.BlockSpec((1, GDN_HEADS, CHUNK, last), lambda bi, t: (bi, 1, nc - 1 - t, 0))
    kf = pl.BlockSpec((1, GDN_HEADS, 1, GDN_HEAD_DIM, CHUNK), lambda bi, t: (bi, 0, t, 0, 0))
    kb = pl.BlockSpec((1, GDN_HEADS, 1, GDN_HEAD_DIM, CHUNK), lambda bi, t: (bi, 1, nc - 1 - t, 0, 0))
    d = GDN_HEAD_DIM
    return pl.pallas_call(
        _gscan_kernel,
        grid=(b, nc),
        in_specs=[pl.BlockSpec(memory_space=pltpu.SMEM),
                  fwd(d), bwd(d), fwd(d), bwd(d), fwd(d), bwd(d), kf, kb, fwd(CHUNK), bwd(CHUNK)],
        out_specs=[pl.BlockSpec((1, CHUNK, GDN_W), lambda bi, t: (bi, t, 0)),
                   pl.BlockSpec((1, CHUNK, GDN_W), lambda bi, t: (bi, nc - 1 - t, 0))],
        out_shape=[jax.ShapeDtypeStruct((b, s, GDN_W), F32),
                   jax.ShapeDtypeStruct((b, s, GDN_W), F32)],
        scratch_shapes=[pltpu.VMEM((N_CHAIN, GDN_HEAD_DIM, GDN_HEAD_DIM), F32)],
        compiler_params=pltpu.CompilerParams(dimension_semantics=("arbitrary", "arbitrary"),
                                             vmem_limit_bytes=VMEM_LIMIT),
        name="gdn_scan",
    )(egl, w, w, u, u, qd, qd, kdt, kdt, qk, qk)


def _ffn_kernel(x_ref, attn_ref, of_ref, ob_ref, z_ref, gnw_ref, wo_ref, fnw_ref, w1_ref, w2_ref,
                onw_ref, o_ref):
    o = of_ref[...] + ob_ref[...]
    heads = []
    for h in range(GDN_HEADS):
        oh = o[:, h * GDN_HEAD_DIM:(h + 1) * GDN_HEAD_DIM]
        ms = jnp.mean(oh * oh, axis=-1, keepdims=True)
        heads.append(oh * lax.rsqrt(ms + EPS) * gnw_ref[...])
    z = z_ref[...]
    gdn = jnp.concatenate(heads, axis=-1) * (z * jax.nn.sigmoid(z))
    hres = (x_ref[...] + jnp.dot(attn_ref[...], wo_ref[:ATTN_Q, :], preferred_element_type=F32)
            + _dot(gdn, wo_ref[ATTN_Q:, :]))
    ms = jnp.mean(hres * hres, axis=-1, keepdims=True)
    hn = (hres * lax.rsqrt(ms + EPS) * fnw_ref[...]).astype(BF16)
    a = jnp.dot(hn, w1_ref[...], preferred_element_type=F32)
    a = jnp.square(jnp.maximum(a, 0.0)).astype(BF16)
    acc = hres + jnp.dot(a, w2_ref[...], preferred_element_type=F32)
    ms = jnp.mean(acc * acc, axis=-1, keepdims=True)
    o_ref[...] = acc * lax.rsqrt(ms + EPS) * onw_ref[...]


def _out_ffn(x2, attn, o_f, o_b, z, gnw, wo, fnw, w1, w2, onw):
    n = x2.shape[0]
    row = lambda w: pl.BlockSpec((FFN_TM, w), lambda i: (i, 0))
    full = lambda a: pl.BlockSpec(a.shape, lambda i: (0, 0))
    return pl.pallas_call(
        _ffn_kernel,
        grid=(n // FFN_TM,),
        in_specs=[row(D_MODEL), row(ATTN_Q), row(GDN_W), row(GDN_W), row(GDN_W),
                  full(gnw), full(wo), full(fnw), full(w1), full(w2), full(onw)],
        out_specs=row(D_MODEL),
        out_shape=jax.ShapeDtypeStruct((n, D_MODEL), F32),
        compiler_params=pltpu.CompilerParams(dimension_semantics=("arbitrary",),
                                             vmem_limit_bytes=VMEM_LIMIT),
        name="out_ffn",
    )(x2, attn, o_f, o_b, z, gnw, wo, fnw, w1, w2, onw)


def _layer(h, band, norm_mix_w, w_in, attn_sink, conv_w, gdn_a_log, gdn_dt_bias, gdn_norm_w,
           w_out, norm_ffn_w, w_ffn_in, w_ffn_out, out_norm_w):
    b, s, _ = h.shape
    n = b * s
    o_qkvg = ATTN_Q + 2 * ATTN_KV
    o_z = o_qkvg + 3 * GDN_W
    o_ab = o_z + GDN_W
    w_cat = jnp.concatenate(
        [w_in[:, o_qkvg:o_z], w_in[:, o_z:o_ab], w_in[:, :o_qkvg], w_in[:, o_ab:],
         jnp.zeros((D_MODEL, LANES - 2 * N_CHAIN), w_in.dtype)], axis=1).astype(BF16)
    x2 = h.reshape(n, D_MODEL)
    qkv_g, z_g, q_a, k_a, v_a, ab = _proj(x2, norm_mix_w.reshape(1, D_MODEL), w_cat)

    attn = _attention(q_a.reshape(b, s, ATTN_Q), k_a.reshape(b, s, ATTN_KV), v_a.reshape(b, s, ATTN_KV),
                      band, attn_sink)

    conv_pad = jnp.zeros((SUBLANES, 3 * GDN_W), F32).at[:CONV_K].set(conv_w)
    gate_par = jnp.zeros((SUBLANES, LANES), F32)
    gate_par = gate_par.at[0, :N_CHAIN].set(gdn_a_log.reshape(-1)).at[1, :N_CHAIN].set(gdn_dt_bias.reshape(-1))
    w_c, u_c, q_dec, k_dec_t, qk, egl = _gdn_prep(qkv_g.reshape(b, s, 3 * GDN_W), ab.reshape(b, s, LANES),
                                                  conv_pad, gate_par)
    o_f, o_b = _gdn_scan(egl[..., 0].reshape(-1), w_c, u_c, q_dec, k_dec_t, qk)

    out = _out_ffn(x2, attn.reshape(n, ATTN_Q), o_f.reshape(n, GDN_W), o_b.reshape(n, GDN_W), z_g,
                   gdn_norm_w.reshape(1, GDN_HEAD_DIM), w_out.astype(BF16), norm_ffn_w.reshape(1, D_MODEL),
                   w_ffn_in.astype(BF16), w_ffn_out.astype(BF16), out_norm_w.reshape(1, D_MODEL))
    return out.reshape(b, s, D_MODEL)


def kernel(x, norm_mix_w, w_in, rel_bias, attn_sink, conv_w, gdn_a_log, gdn_dt_bias, gdn_norm_w, w_out,
           norm_ffn_w, w_ffn_in, w_ffn_out, norm_final_w):
    depth = w_in.shape[0]
    assert depth == 1, "the fused output kernel applies the final norm after the single trunk layer"
    rel = (np.arange(3 * BLOCK)[None, :] - BLOCK) - np.arange(BLOCK)[:, None]
    bucket = _t5_buckets(jnp.asarray(rel, dtype=jnp.int32))
    band = _bias_band(rel_bias, bucket)
    return _layer(x, band, norm_mix_w[0], w_in[0], attn_sink[0], conv_w[0], gdn_a_log[0], gdn_dt_bias[0],
                  gdn_norm_w[0], w_out[0], norm_ffn_w[0], w_ffn_in[0], w_ffn_out[0], norm_final_w)
```

```python
import functools
import math

import jax
import jax.numpy as jnp
import numpy as np
from jax import lax
from jax.experimental import pallas as pl
from jax.experimental.pallas import tpu as pltpu

F32 = jnp.float32
BF16 = jnp.bfloat16

D_MODEL = 1024
ATTN_HEADS = 8
ATTN_KV_HEADS = 2
ATTN_HEAD_DIM = 64
ATTN_GROUP = ATTN_HEADS // ATTN_KV_HEADS
WINDOW = 128
BLOCK = 128
N_BUCKETS = 32
MAX_DISTANCE = 128
GDN_HEADS = 4
GDN_HEAD_DIM = 128
CONV_K = 5
CHUNK = 64
N_DIR = 2
N_CHAIN = N_DIR * GDN_HEADS
D_FF = 4 * D_MODEL
EPS = 1e-6
ATTN_Q = ATTN_HEADS * ATTN_HEAD_DIM
ATTN_KV = ATTN_KV_HEADS * ATTN_HEAD_DIM
GDN_W = GDN_HEADS * GDN_HEAD_DIM
LANES = 128
SUBLANES = 8
VMEM_LIMIT = 56 * 1024 * 1024

PROJ_TM = 512
PREP_T = 128
FFN_TM = 512


def _dot(a, b):
    return jnp.dot(a.astype(BF16), b.astype(BF16), preferred_element_type=F32)


def _dot_nt(a, b):
    return lax.dot_general(a.astype(BF16), b.astype(BF16), (((1,), (1,)), ((), ())),
                           preferred_element_type=F32)


def _split3(x):
    hi = x.astype(BF16)
    r1 = x - hi.astype(F32)
    mid = r1.astype(BF16)
    lo = (r1 - mid.astype(F32)).astype(BF16)
    return hi, mid, lo


def _dot01_left(m01, x):
    hi, mid, lo = _split3(x)
    d = lambda p: jnp.dot(m01, p, preferred_element_type=F32)
    return d(hi) + d(mid) + d(lo)


def _dot01_right(x, m01):
    hi, mid, lo = _split3(x)
    d = lambda p: jnp.dot(p, m01, preferred_element_type=F32)
    return d(hi) + d(mid) + d(lo)


def _proj_kernel(x_ref, nw_ref, w_ref, qkvg_ref, z_ref, qa_ref, ka_ref, va_ref, ab_ref):
    x = x_ref[...]
    ms = jnp.mean(x * x, axis=-1, keepdims=True)
    xn = (x * lax.rsqrt(ms + EPS) * nw_ref[...]).astype(BF16)
    c0 = 0
    for ref, width, scale in ((qkvg_ref, 3 * GDN_W, None), (z_ref, GDN_W, None),
                              (qa_ref, ATTN_Q, ATTN_HEAD_DIM ** -0.5), (ka_ref, ATTN_KV, None),
                              (va_ref, ATTN_KV, None), (ab_ref, LANES, None)):
        y = jnp.dot(xn, w_ref[:, c0:c0 + width], preferred_element_type=F32)
        if scale is not None:
            y = y * scale
        ref[...] = y.astype(ref.dtype)
        c0 += width


def _proj(x2, norm_w, w_cat):
    n = x2.shape[0]
    d_cat = w_cat.shape[1]
    row = lambda w: pl.BlockSpec((PROJ_TM, w), lambda i: (i, 0))
    return pl.pallas_call(
        _proj_kernel,
        grid=(n // PROJ_TM,),
        in_specs=[row(D_MODEL),
                  pl.BlockSpec((1, D_MODEL), lambda i: (0, 0)),
                  pl.BlockSpec((D_MODEL, d_cat), lambda i: (0, 0))],
        out_specs=[row(3 * GDN_W), row(GDN_W), row(ATTN_Q), row(ATTN_KV), row(ATTN_KV), row(LANES)],
        out_shape=[jax.ShapeDtypeStruct((n, 3 * GDN_W), F32),
                   jax.ShapeDtypeStruct((n, GDN_W), F32),
                   jax.ShapeDtypeStruct((n, ATTN_Q), BF16),
                   jax.ShapeDtypeStruct((n, ATTN_KV), BF16),
                   jax.ShapeDtypeStruct((n, ATTN_KV), BF16),
                   jax.ShapeDtypeStruct((n, LANES), F32)],
        compiler_params=pltpu.CompilerParams(dimension_semantics=("arbitrary",),
                                             vmem_limit_bytes=VMEM_LIMIT),
        name="proj",
    )(x2, norm_w, w_cat)


def _bias_kernel(relb_ref, bucket_ref, o_ref):
    bucket = bucket_ref[...]
    row = lax.broadcasted_iota(jnp.int32, (BLOCK, 3 * BLOCK), 0)
    col = lax.broadcasted_iota(jnp.int32, (BLOCK, 3 * BLOCK), 1)
    in_window = jnp.abs(col - BLOCK - row) <= WINDOW
    for h in range(ATTN_HEADS):
        acc = jnp.zeros((BLOCK, 3 * BLOCK), F32)
        for b in range(N_BUCKETS):
            acc = jnp.where(bucket == b, relb_ref[b, h], acc)
        o_ref[h] = jnp.where(in_window, acc, -1e30)


def _bias_band(rel_bias, bucket):
    return pl.pallas_call(
        _bias_kernel,
        in_specs=[pl.BlockSpec(memory_space=pltpu.SMEM),
                  pl.BlockSpec((BLOCK, 3 * BLOCK), lambda: (0, 0))],
        out_specs=pl.BlockSpec((ATTN_HEADS, BLOCK, 3 * BLOCK), lambda: (0, 0, 0)),
        out_shape=jax.ShapeDtypeStruct((ATTN_HEADS, BLOCK, 3 * BLOCK), F32),
        name="bias_band",
    )(rel_bias, bucket)


def _t5_buckets(rel):
    nb = N_BUCKETS // 2
    max_exact = nb // 2
    base = jnp.where(rel > 0, nb, 0)
    n = jnp.abs(rel)
    log_ratio = jnp.log(jnp.maximum(n, 1).astype(jnp.float32) / max_exact) / math.log(MAX_DISTANCE / max_exact)
    large = jnp.minimum(max_exact + (log_ratio * (nb - max_exact)).astype(jnp.int32), nb - 1)
    return base + jnp.where(n < max_exact, n, large)


def _attn_kernel(sink_ref, q_ref, kp_ref, kc_ref, kn_ref, vp_ref, vc_ref, vn_ref, bias_ref, o_ref):
    n = pl.program_id(1)
    last = pl.num_programs(1) - 1
    kband = jnp.concatenate([kp_ref[0], kc_ref[0], kn_ref[0]], axis=0)
    vband = jnp.concatenate([vp_ref[0], vc_ref[0], vn_ref[0]], axis=0)
    col = lax.broadcasted_iota(jnp.int32, (1, 3 * BLOCK), 1)
    key_bad = ((col < BLOCK) & (n == 0)) | ((col >= 2 * BLOCK) & (n == last))
    q = q_ref[0]
    head = lambda t, i: t[:, i * ATTN_HEAD_DIM:(i + 1) * ATTN_HEAD_DIM]
    scores = [_dot_nt(head(q, h), head(kband, h // ATTN_GROUP)) for h in range(ATTN_HEADS)]
    probs, dens = [], []
    for h in range(ATTN_HEADS):
        s = jnp.where(key_bad, -1e30, scores[h] + bias_ref[h])
        sink = sink_ref[h]
        m = jnp.maximum(jnp.max(s, axis=-1, keepdims=True), sink)
        p = jnp.exp(s - m)
        dens.append(jnp.sum(p, axis=-1, keepdims=True) + jnp.exp(sink - m))
        probs.append(p.astype(BF16))
    outs = [jnp.dot(probs[h], head(vband, h // ATTN_GROUP), preferred_element_type=F32) / dens[h]
            for h in range(ATTN_HEADS)]
    o_ref[0] = jnp.concatenate(outs, axis=-1).astype(o_ref.dtype)


def _attention(q_a, k_a, v_a, band, sink):
    b, s, _ = q_a.shape
    nb = s // BLOCK
    kv_spec = lambda f: pl.BlockSpec((1, BLOCK, ATTN_KV), f)
    prev = lambda bi, n: (bi, jnp.maximum(n - 1, 0), 0)
    cur = lambda bi, n: (bi, n, 0)
    nxt = lambda bi, n: (bi, jnp.minimum(n + 1, nb - 1), 0)
    return pl.pallas_call(
        _attn_kernel,
        grid=(b, nb),
        in_specs=[pl.BlockSpec(memory_space=pltpu.SMEM),
                  pl.BlockSpec((1, BLOCK, ATTN_Q), cur),
                  kv_spec(prev), kv_spec(cur), kv_spec(nxt),
                  kv_spec(prev), kv_spec(cur), kv_spec(nxt),
                  pl.BlockSpec((ATTN_HEADS, BLOCK, 3 * BLOCK), lambda bi, n: (0, 0, 0))],
        out_specs=pl.BlockSpec((1, BLOCK, ATTN_Q), cur),
        out_shape=jax.ShapeDtypeStruct((b, s, ATTN_Q), BF16),
        compiler_params=pltpu.CompilerParams(dimension_semantics=("arbitrary", "arbitrary"),
                                             vmem_limit_bytes=VMEM_LIMIT),
        name="attn",
    )(sink, q_a, k_a, k_a, k_a, v_a, v_a, v_a, band)


def _gprep_kernel(xp_ref, x_ref, xn_ref, ab_ref, cw_ref, gp_ref,
                  w_ref, u_ref, qd_ref, kdt_ref, qk_ref, egl_ref):
    i = pl.program_id(1)
    last = pl.num_programs(1) - 1
    t_len = PREP_T
    halo = CONV_K // 2

    x = x_ref[0]
    prev = jnp.where(i > 0, xp_ref[0][SUBLANES - halo:, :], 0.0)
    nxt = jnp.where(i < last, xn_ref[0][:halo, :], 0.0)
    xe = jnp.concatenate([prev, x, nxt], axis=0)
    y = cw_ref[0:1, :] * xe[0:t_len]
    for j in range(1, CONV_K):
        y = y + cw_ref[j:j + 1, :] * xe[j:j + t_len]
    y = y * jax.nn.sigmoid(y)

    def l2n(t):
        return t * lax.rsqrt(jnp.sum(t * t, axis=-1, keepdims=True) + EPS)

    qs, ks, kts, vs = [], [], [], []
    for h in range(GDN_HEADS):
        sl = lambda base: y[:, base + h * GDN_HEAD_DIM: base + (h + 1) * GDN_HEAD_DIM]
        qs.append(l2n(sl(0)) * (GDN_HEAD_DIM ** -0.5))
        kh = l2n(sl(GDN_W))
        ks.append(kh)
        kts.append(kh.T)
        vs.append(sl(2 * GDN_W))

    ab = ab_ref[0]
    sp_in = ab + gp_ref[1:2, :]
    softplus = jnp.maximum(sp_in, 0.0) + jnp.log1p(jnp.exp(-jnp.abs(sp_in)))
    g = -jnp.exp(gp_ref[0:1, :]) * softplus
    beta = jax.nn.sigmoid(ab)

    r_t = lax.broadcasted_iota(jnp.int32, (t_len, t_len), 0)
    c_t = lax.broadcasted_iota(jnp.int32, (t_len, t_len), 1)
    same = (r_t // CHUNK) == (c_t // CHUNK)
    lower = jnp.where(same & (r_t >= c_t), 1.0, 0.0).astype(BF16)
    upper = jnp.where(same & (r_t <= c_t), 1.0, 0.0).astype(BF16)
    g_t = g.T
    cs_col = (_dot01_left(lower, g), _dot01_left(upper, g))
    cs_row = (_dot01_right(g_t, upper), _dot01_right(g_t, lower))

    r_c = lax.broadcasted_iota(jnp.int32, (CHUNK, CHUNK), 0)
    c_c = lax.broadcasted_iota(jnp.int32, (CHUNK, CHUNK), 1)
    masks = {"eye": jnp.where(r_c == c_c, 1.0, 0.0).astype(F32)}
    s = 1
    while s < CHUNK:
        masks[s] = ((r_c // (2 * s)) == (c_c // (2 * s))) & ((r_c // s) != (c_c // s))
        s *= 2
    incl = (r_c >= c_c, r_c <= c_c)
    strict = (r_c > c_c, r_c < c_c)

    n_chunks = t_len // CHUNK
    rows = lambda c: slice(c * CHUNK, (c + 1) * CHUNK)
    qkk = {}
    for c in range(n_chunks):
        for h in range(GDN_HEADS):
            k16 = ks[h][rows(c)].astype(BF16)
            qk16 = jnp.concatenate([qs[h][rows(c)].astype(BF16), k16], axis=0)
            qkk[c, h] = _dot_nt(qk16, k16)

    inst = [(c, d, h) for c in range(n_chunks) for d in range(N_DIR) for h in range(GDN_HEADS)]
    gcol, grow, glast, bcol, decay, a_mat, t_mat = {}, {}, {}, {}, {}, {}, {}
    for key in inst:
        c, d, h = key
        j = d * GDN_HEADS + h
        r0 = c * CHUNK
        gcol[key] = cs_col[d][rows(c), j:j + 1]
        grow[key] = cs_row[d][j:j + 1, rows(c)]
        r_last = r0 + CHUNK - 1 if d == 0 else r0
        glast[key] = cs_col[d][r_last:r_last + 1, j:j + 1]
        bcol[key] = beta[rows(c), SUBLANES + j:SUBLANES + j + 1]
        decay[key] = jnp.exp(jnp.where(incl[d], gcol[key] - grow[key], -jnp.inf))
        a_mat[key] = jnp.where(strict[d], bcol[key] * qkk[c, h][CHUNK:] * decay[key], 0.0)
        t_mat[key] = masks["eye"] - jnp.where(masks[1], a_mat[key], 0.0)

    s = 2
    while s < CHUNK:
        x_mat = {key: _dot(t_mat[key], jnp.where(masks[s], a_mat[key], 0.0)) for key in inst}
        t_mat = {key: t_mat[key] - _dot(x_mat[key], t_mat[key]) for key in inst}
        s *= 2

    wu = {}
    for key in inst:
        c, d, h = key
        eg = jnp.exp(gcol[key])
        kv = jnp.concatenate([ks[h][rows(c)] * (bcol[key] * eg), vs[h][rows(c)] * bcol[key]], axis=1)
        wu[key] = _dot(t_mat[key], kv)

    for c in range(n_chunks):
        egl_rows = []
        for d in range(N_DIR):
            for h in range(GDN_HEADS):
                key = (c, d, h)
                j = d * GDN_HEADS + h
                w_ref[0, j, rows(c), :] = wu[key][:, :GDN_HEAD_DIM].astype(w_ref.dtype)
                u_ref[0, j, rows(c), :] = wu[key][:, GDN_HEAD_DIM:]
                qd_ref[0, j, rows(c), :] = (qs[h][rows(c)] * jnp.exp(gcol[key])).astype(qd_ref.dtype)
                qk_ref[0, j, rows(c), :] = (qkk[c, h][:CHUNK] * decay[key]).astype(qk_ref.dtype)
                kdt = kts[h][:, rows(c)] * jnp.exp(glast[key] - grow[key])
                kdt_ref[0, j, c] = kdt.astype(kdt_ref.dtype)
                egl_rows.append(jnp.broadcast_to(jnp.exp(glast[key]), (1, LANES)))
        egl_ref[0, c] = jnp.concatenate(egl_rows, axis=0)


def _gdn_prep(qkv_g, ab, conv_w, gate_par):
    b, s, _ = qkv_g.shape
    nt = s // PREP_T
    cpb = PREP_T // CHUNK
    nc = s // CHUNK
    nh8 = PREP_T // SUBLANES
    chain = lambda last: pl.BlockSpec((1, N_CHAIN, PREP_T, last), lambda bi, i: (bi, 0, i, 0))
    return pl.pallas_call(
        _gprep_kernel,
        grid=(b, nt),
        in_specs=[pl.BlockSpec((1, SUBLANES, 3 * GDN_W), lambda bi, i: (bi, jnp.maximum(i * nh8 - 1, 0), 0)),
                  pl.BlockSpec((1, PREP_T, 3 * GDN_W), lambda bi, i: (bi, i, 0)),
                  pl.BlockSpec((1, SUBLANES, 3 * GDN_W),
                               lambda bi, i: (bi, jnp.minimum((i + 1) * nh8, s // SUBLANES - 1), 0)),
                  pl.BlockSpec((1, PREP_T, LANES), lambda bi, i: (bi, i, 0)),
                  pl.BlockSpec((SUBLANES, 3 * GDN_W), lambda bi, i: (0, 0)),
                  pl.BlockSpec((SUBLANES, LANES), lambda bi, i: (0, 0))],
        out_specs=[chain(GDN_HEAD_DIM), chain(GDN_HEAD_DIM), chain(GDN_HEAD_DIM),
                   pl.BlockSpec((1, N_CHAIN, cpb, GDN_HEAD_DIM, CHUNK), lambda bi, i: (bi, 0, i, 0, 0)),
                   chain(CHUNK),
                   pl.BlockSpec((1, cpb, N_CHAIN, LANES), lambda bi, i: (bi, i, 0, 0))],
        out_shape=[jax.ShapeDtypeStruct((b, N_CHAIN, s, GDN_HEAD_DIM), BF16),
                   jax.ShapeDtypeStruct((b, N_CHAIN, s, GDN_HEAD_DIM), F32),
                   jax.ShapeDtypeStruct((b, N_CHAIN, s, GDN_HEAD_DIM), BF16),
                   jax.ShapeDtypeStruct((b, N_CHAIN, nc, GDN_HEAD_DIM, CHUNK), BF16),
                   jax.ShapeDtypeStruct((b, N_CHAIN, s, CHUNK), BF16),
                   jax.ShapeDtypeStruct((b, nc, N_CHAIN, LANES), F32)],
        compiler_params=pltpu.CompilerParams(dimension_semantics=("arbitrary", "arbitrary"),
                                             vmem_limit_bytes=VMEM_LIMIT),
        name="gdn_prep",
    )(qkv_g, qkv_g, qkv_g, ab, conv_w, gate_par)


def _gscan_kernel(egl_ref, wf_ref, wb_ref, uf_ref, ub_ref, qf_ref, qb_ref, kf_ref, kb_ref,
                  pf_ref, pb_ref, of_ref, ob_ref, state_ref):
    t = pl.program_id(0)
    nc = pl.num_programs(0)
    n_batch = wf_ref.shape[0]

    @pl.when(t == 0)
    def _():
        state_ref[...] = jnp.zeros_like(state_ref)

    dirs = ((wf_ref, uf_ref, qf_ref, kf_ref, pf_ref, of_ref), (wb_ref, ub_ref, qb_ref, kb_ref, pb_ref, ob_ref))
    chains = [(bi, d, h) for bi in range(n_batch) for d in range(N_DIR) for h in range(GDN_HEADS)]
    slot = lambda bi, d, h: (bi * N_DIR + d) * GDN_HEADS + h

    st, r = {}, {}
    for key in chains:
        bi, d, h = key
        w_ref, _, q_ref = dirs[d][:3]
        st[key] = state_ref[slot(*key)]
        wq = jnp.concatenate([w_ref[bi, h], q_ref[bi, h]], axis=0)
        r[key] = _dot(wq, st[key])
    v_new, intra = {}, {}
    for key in chains:
        bi, d, h = key
        u_ref, p_ref = dirs[d][1], dirs[d][4]
        v_new[key] = (u_ref[bi, h] - r[key][:CHUNK]).astype(BF16)
        intra[key] = jnp.dot(p_ref[bi, h], v_new[key], preferred_element_type=F32)
    for key in chains:
        bi, d, h = key
        k_ref = dirs[d][3]
        c = t if d == 0 else nc - 1 - t
        egl = egl_ref[(bi * nc + c) * N_CHAIN + d * GDN_HEADS + h]
        state_ref[slot(*key)] = st[key] * egl + jnp.dot(k_ref[bi, h, 0], v_new[key],
                                                        preferred_element_type=F32)
    for bi in range(n_batch):
        for d in range(N_DIR):
            o_ref = dirs[d][5]
            o_ref[bi] = jnp.concatenate(
                [r[bi, d, h][CHUNK:] + intra[bi, d, h] for h in range(GDN_HEADS)], axis=-1)


def _gdn_scan(egl, w, u, qd, kdt, qk):
    b, _, s, _ = w.shape
    nc = s // CHUNK
    fwd = lambda last: pl.BlockSpec((b, GDN_HEADS, CHUNK, last), lambda t: (0, 0, t, 0))
    bwd = lambda last: pl.BlockSpec((b, GDN_HEADS, CHUNK, last), lambda t: (0, 1, nc - 1 - t, 0))
    kf = pl.BlockSpec((b, GDN_HEADS, 1, GDN_HEAD_DIM, CHUNK), lambda t: (0, 0, t, 0, 0))
    kb = pl.BlockSpec((b, GDN_HEADS, 1, GDN_HEAD_DIM, CHUNK), lambda t: (0, 1, nc - 1 - t, 0, 0))
    d = GDN_HEAD_DIM
    return pl.pallas_call(
        _gscan_kernel,
        grid=(nc,),
        in_specs=[pl.BlockSpec(memory_space=pltpu.SMEM),
                  fwd(d), bwd(d), fwd(d), bwd(d), fwd(d), bwd(d), kf, kb, fwd(CHUNK), bwd(CHUNK)],
        out_specs=[pl.BlockSpec((b, CHUNK, GDN_W), lambda t: (0, t, 0)),
                   pl.BlockSpec((b, CHUNK, GDN_W), lambda t: (0, nc - 1 - t, 0))],
        out_shape=[jax.ShapeDtypeStruct((b, s, GDN_W), F32),
                   jax.ShapeDtypeStruct((b, s, GDN_W), F32)],
        scratch_shapes=[pltpu.VMEM((b * N_CHAIN, GDN_HEAD_DIM, GDN_HEAD_DIM), F32)],
        compiler_params=pltpu.CompilerParams(dimension_semantics=("arbitrary",),
                                             vmem_limit_bytes=VMEM_LIMIT),
        name="gdn_scan",
    )(egl, w, w, u, u, qd, qd, kdt, kdt, qk, qk)


def _ffn_kernel(x_ref, attn_ref, of_ref, ob_ref, z_ref, gnw_ref, wo_ref, fnw_ref, w1_ref, w2_ref,
                onw_ref, o_ref):
    o = of_ref[...] + ob_ref[...]
    heads = []
    for h in range(GDN_HEADS):
        oh = o[:, h * GDN_HEAD_DIM:(h + 1) * GDN_HEAD_DIM]
        ms = jnp.mean(oh * oh, axis=-1, keepdims=True)
        heads.append(oh * lax.rsqrt(ms + EPS) * gnw_ref[...])
    z = z_ref[...]
    gdn = jnp.concatenate(heads, axis=-1) * (z * jax.nn.sigmoid(z))
    hres = (x_ref[...] + jnp.dot(attn_ref[...], wo_ref[:ATTN_Q, :], preferred_element_type=F32)
            + _dot(gdn, wo_ref[ATTN_Q:, :]))
    ms = jnp.mean(hres * hres, axis=-1, keepdims=True)
    hn = (hres * lax.rsqrt(ms + EPS) * fnw_ref[...]).astype(BF16)
    a = jnp.dot(hn, w1_ref[...], preferred_element_type=F32)
    a = jnp.square(jnp.maximum(a, 0.0)).astype(BF16)
    acc = hres + jnp.dot(a, w2_ref[...], preferred_element_type=F32)
    ms = jnp.mean(acc * acc, axis=-1, keepdims=True)
    o_ref[...] = acc * lax.rsqrt(ms + EPS) * onw_ref[...]


def _out_ffn(x2, attn, o_f, o_b, z, gnw, wo, fnw, w1, w2, onw):
    n = x2.shape[0]
    row = lambda w: pl.BlockSpec((FFN_TM, w), lambda i: (i, 0))
    full = lambda a: pl.BlockSpec(a.shape, lambda i: (0, 0))
    return pl.pallas_call(
        _ffn_kernel,
        grid=(n // FFN_TM,),
        in_specs=[row(D_MODEL), row(ATTN_Q), row(GDN_W), row(GDN_W), row(GDN_W),
                  full(gnw), full(wo), full(fnw), full(w1), full(w2), full(onw)],
        out_specs=row(D_MODEL),
        out_shape=jax.ShapeDtypeStruct((n, D_MODEL), F32),
        compiler_params=pltpu.CompilerParams(dimension_semantics=("arbitrary",),
                                             vmem_limit_bytes=VMEM_LIMIT),
        name="out_ffn",
    )(x2, attn, o_f, o_b, z, gnw, wo, fnw, w1, w2, onw)


def _layer(h, band, norm_mix_w, w_in, attn_sink, conv_w, gdn_a_log, gdn_dt_bias, gdn_norm_w,
           w_out, norm_ffn_w, w_ffn_in, w_ffn_out, out_norm_w):
    b, s, _ = h.shape
    n = b * s
    o_qkvg = ATTN_Q + 2 * ATTN_KV
    o_z = o_qkvg + 3 * GDN_W
    o_ab = o_z + GDN_W
    w_cat = jnp.concatenate(
        [w_in[:, o_qkvg:o_z], w_in[:, o_z:o_ab], w_in[:, :o_qkvg], w_in[:, o_ab:],
         jnp.zeros((D_MODEL, LANES - 2 * N_CHAIN), w_in.dtype)], axis=1).astype(BF16)
    x2 = h.reshape(n, D_MODEL)
    qkv_g, z_g, q_a, k_a, v_a, ab = _proj(x2, norm_mix_w.reshape(1, D_MODEL), w_cat)

    attn = _attention(q_a.reshape(b, s, ATTN_Q), k_a.reshape(b, s, ATTN_KV), v_a.reshape(b, s, ATTN_KV),
                      band, attn_sink)

    conv_pad = jnp.zeros((SUBLANES, 3 * GDN_W), F32).at[:CONV_K].set(conv_w)
    gate_par = jnp.zeros((SUBLANES, LANES), F32)
    gate_par = gate_par.at[0, :N_CHAIN].set(gdn_a_log.reshape(-1)).at[1, :N_CHAIN].set(gdn_dt_bias.reshape(-1))
    w_c, u_c, q_dec, k_dec_t, qk, egl = _gdn_prep(qkv_g.reshape(b, s, 3 * GDN_W), ab.reshape(b, s, LANES),
                                                  conv_pad, gate_par)
    o_f, o_b = _gdn_scan(egl[..., 0].reshape(-1), w_c, u_c, q_dec, k_dec_t, qk)

    out = _out_ffn(x2, attn.reshape(n, ATTN_Q), o_f.reshape(n, GDN_W), o_b.reshape(n, GDN_W), z_g,
                   gdn_norm_w.reshape(1, GDN_HEAD_DIM), w_out.astype(BF16), norm_ffn_w.reshape(1, D_MODEL),
                   w_ffn_in.astype(BF16), w_ffn_out.astype(BF16), out_norm_w.reshape(1, D_MODEL))
    return out.reshape(b, s, D_MODEL)


def kernel(x, norm_mix_w, w_in, rel_bias, attn_sink, conv_w, gdn_a_log, gdn_dt_bias, gdn_norm_w, w_out,
           norm_ffn_w, w_ffn_in, w_ffn_out, norm_final_w):
    depth = w_in.shape[0]
    assert depth == 1, "the fused output kernel applies the final norm after the single trunk layer"
    rel = (np.arange(3 * BLOCK)[None, :] - BLOCK) - np.arange(BLOCK)[:, None]
    bucket = _t5_buckets(jnp.asarray(rel, dtype=jnp.int32))
    band = _bias_band(rel_bias, bucket)
    return _layer(x, band, norm_mix_w[0], w_in[0], attn_sink[0], conv_w[0], gdn_a_log[0], gdn_dt_bias[0],
                  gdn_norm_w[0], w_out[0], norm_ffn_w[0], w_ffn_in[0], w_ffn_out[0], norm_final_w)
```

```python
import functools
import math

import jax
import jax.numpy as jnp
import numpy as np
from jax import lax
from jax.experimental import pallas as pl
from jax.experimental.pallas import tpu as pltpu

F32 = jnp.float32
BF16 = jnp.bfloat16

D_MODEL = 1024
ATTN_HEADS = 8
ATTN_KV_HEADS = 2
ATTN_HEAD_DIM = 64
ATTN_GROUP = ATTN_HEADS // ATTN_KV_HEADS
WINDOW = 128
BLOCK = 128
N_BUCKETS = 32
MAX_DISTANCE = 128
GDN_HEADS = 4
GDN_HEAD_DIM = 128
CONV_K = 5
CHUNK = 64
N_DIR = 2
N_CHAIN = N_DIR * GDN_HEADS
D_FF = 4 * D_MODEL
EPS = 1e-6
ATTN_Q = ATTN_HEADS * ATTN_HEAD_DIM
ATTN_KV = ATTN_KV_HEADS * ATTN_HEAD_DIM
GDN_W = GDN_HEADS * GDN_HEAD_DIM
LANES = 128
SUBLANES = 8
VMEM_LIMIT = 56 * 1024 * 1024

PROJ_TM = 512
PREP_T = 256
ROW_STRIDE = 4
FFN_TM = 512


def _dot(a, b):
    return jnp.dot(a.astype(BF16), b.astype(BF16), preferred_element_type=F32)


def _dot_nt(a, b):
    return lax.dot_general(a.astype(BF16), b.astype(BF16), (((1,), (1,)), ((), ())),
                           preferred_element_type=F32)


def _split3(x):
    hi = x.astype(BF16)
    r1 = x - hi.astype(F32)
    mid = r1.astype(BF16)
    lo = (r1 - mid.astype(F32)).astype(BF16)
    return hi, mid, lo


def _dot01_left(m01, x):
    hi, mid, lo = _split3(x)
    d = lambda p: jnp.dot(m01, p, preferred_element_type=F32)
    return d(hi) + d(mid) + d(lo)


def _dot01_right(x, m01):
    hi, mid, lo = _split3(x)
    d = lambda p: jnp.dot(p, m01, preferred_element_type=F32)
    return d(hi) + d(mid) + d(lo)


def _proj_kernel(x_ref, nw_ref, w_ref, qkvg_ref, z_ref, qa_ref, ka_ref, va_ref, ab_ref):
    x = x_ref[...]
    ms = jnp.mean(x * x, axis=-1, keepdims=True)
    xn = (x * lax.rsqrt(ms + EPS) * nw_ref[...]).astype(BF16)
    c0 = 0
    for ref, width, scale in ((qkvg_ref, 3 * GDN_W, None), (z_ref, GDN_W, None),
                              (qa_ref, ATTN_Q, ATTN_HEAD_DIM ** -0.5), (ka_ref, ATTN_KV, None),
                              (va_ref, ATTN_KV, None), (ab_ref, LANES, None)):
        y = jnp.dot(xn, w_ref[:, c0:c0 + width], preferred_element_type=F32)
        if scale is not None:
            y = y * scale
        ref[...] = y.astype(ref.dtype)
        c0 += width


def _proj(x2, norm_w, w_cat):
    n = x2.shape[0]
    d_cat = w_cat.shape[1]
    row = lambda w: pl.BlockSpec((PROJ_TM, w), lambda i: (i, 0))
    return pl.pallas_call(
        _proj_kernel,
        grid=(n // PROJ_TM,),
        in_specs=[row(D_MODEL),
                  pl.BlockSpec((1, D_MODEL), lambda i: (0, 0)),
                  pl.BlockSpec((D_MODEL, d_cat), lambda i: (0, 0))],
        out_specs=[row(3 * GDN_W), row(GDN_W), row(ATTN_Q), row(ATTN_KV), row(ATTN_KV), row(LANES)],
        out_shape=[jax.ShapeDtypeStruct((n, 3 * GDN_W), F32),
                   jax.ShapeDtypeStruct((n, GDN_W), F32),
                   jax.ShapeDtypeStruct((n, ATTN_Q), BF16),
                   jax.ShapeDtypeStruct((n, ATTN_KV), BF16),
                   jax.ShapeDtypeStruct((n, ATTN_KV), BF16),
                   jax.ShapeDtypeStruct((n, LANES), F32)],
        compiler_params=pltpu.CompilerParams(dimension_semantics=("arbitrary",),
                                             vmem_limit_bytes=VMEM_LIMIT),
        name="proj",
    )(x2, norm_w, w_cat)


def _bias_kernel(relb_ref, bucket_ref, o_ref):
    bucket = bucket_ref[...]
    row = lax.broadcasted_iota(jnp.int32, (BLOCK, 3 * BLOCK), 0)
    col = lax.broadcasted_iota(jnp.int32, (BLOCK, 3 * BLOCK), 1)
    in_window = jnp.abs(col - BLOCK - row) <= WINDOW
    for h in range(ATTN_HEADS):
        acc = jnp.zeros((BLOCK, 3 * BLOCK), F32)
        for b in range(N_BUCKETS):
            acc = jnp.where(bucket == b, relb_ref[b, h], acc)
        o_ref[h] = jnp.where(in_window, acc, -1e30)


def _bias_band(rel_bias, bucket):
    return pl.pallas_call(
        _bias_kernel,
        in_specs=[pl.BlockSpec(memory_space=pltpu.SMEM),
                  pl.BlockSpec((BLOCK, 3 * BLOCK), lambda: (0, 0))],
        out_specs=pl.BlockSpec((ATTN_HEADS, BLOCK, 3 * BLOCK), lambda: (0, 0, 0)),
        out_shape=jax.ShapeDtypeStruct((ATTN_HEADS, BLOCK, 3 * BLOCK), F32),
        name="bias_band",
    )(rel_bias, bucket)


def _t5_buckets(rel):
    nb = N_BUCKETS // 2
    max_exact = nb // 2
    base = jnp.where(rel > 0, nb, 0)
    n = jnp.abs(rel)
    log_ratio = jnp.log(jnp.maximum(n, 1).astype(jnp.float32) / max_exact) / math.log(MAX_DISTANCE / max_exact)
    large = jnp.minimum(max_exact + (log_ratio * (nb - max_exact)).astype(jnp.int32), nb - 1)
    return base + jnp.where(n < max_exact, n, large)


def _attn_kernel(sink_ref, q_ref, kp_ref, kc_ref, kn_ref, vp_ref, vc_ref, vn_ref, bias_ref, o_ref):
    n = pl.program_id(1)
    last = pl.num_programs(1) - 1
    kband = jnp.concatenate([kp_ref[0], kc_ref[0], kn_ref[0]], axis=0)
    vband = jnp.concatenate([vp_ref[0], vc_ref[0], vn_ref[0]], axis=0)
    col = lax.broadcasted_iota(jnp.int32, (1, 3 * BLOCK), 1)
    key_bad = ((col < BLOCK) & (n == 0)) | ((col >= 2 * BLOCK) & (n == last))
    q = q_ref[0]
    head = lambda t, i: t[:, i * ATTN_HEAD_DIM:(i + 1) * ATTN_HEAD_DIM]
    scores = [_dot_nt(head(q, h), head(kband, h // ATTN_GROUP)) for h in range(ATTN_HEADS)]
    probs, dens = [], []
    for h in range(ATTN_HEADS):
        s = jnp.where(key_bad, -1e30, scores[h] + bias_ref[h])
        sink = sink_ref[h]
        m = jnp.maximum(jnp.max(s, axis=-1, keepdims=True), sink)
        p = jnp.exp(s - m)
        dens.append(jnp.sum(p, axis=-1, keepdims=True) + jnp.exp(sink - m))
        probs.append(p.astype(BF16))
    outs = [jnp.dot(probs[h], head(vband, h // ATTN_GROUP), preferred_element_type=F32) / dens[h]
            for h in range(ATTN_HEADS)]
    o_ref[0] = jnp.concatenate(outs, axis=-1).astype(o_ref.dtype)


def _attention(q_a, k_a, v_a, band, sink):
    b, s, _ = q_a.shape
    nb = s // BLOCK
    kv_spec = lambda f: pl.BlockSpec((1, BLOCK, ATTN_KV), f)
    prev = lambda bi, n: (bi, jnp.maximum(n - 1, 0), 0)
    cur = lambda bi, n: (bi, n, 0)
    nxt = lambda bi, n: (bi, jnp.minimum(n + 1, nb - 1), 0)
    return pl.pallas_call(
        _attn_kernel,
        grid=(b, nb),
        in_specs=[pl.BlockSpec(memory_space=pltpu.SMEM),
                  pl.BlockSpec((1, BLOCK, ATTN_Q), cur),
                  kv_spec(prev), kv_spec(cur), kv_spec(nxt),
                  kv_spec(prev), kv_spec(cur), kv_spec(nxt),
                  pl.BlockSpec((ATTN_HEADS, BLOCK, 3 * BLOCK), lambda bi, n: (0, 0, 0))],
        out_specs=pl.BlockSpec((1, BLOCK, ATTN_Q), cur),
        out_shape=jax.ShapeDtypeStruct((b, s, ATTN_Q), BF16),
        compiler_params=pltpu.CompilerParams(dimension_semantics=("arbitrary", "arbitrary"),
                                             vmem_limit_bytes=VMEM_LIMIT),
        name="attn",
    )(sink, q_a, k_a, k_a, k_a, v_a, v_a, v_a, band)


def _gprep_kernel(xp_ref, x_ref, xn_ref, ab_ref, cw_ref, gp_ref,
                  w_ref, u_ref, qd_ref, kdt_ref, qk_ref, egl_ref, xe_ref, yn_ref):
    i = pl.program_id(1)
    last = pl.num_programs(1) - 1
    t_len = PREP_T
    halo = CONV_K // 2

    n_slab = 3 * GDN_W // LANES
    for sb in range(n_slab):
        lanes = slice(sb * LANES, (sb + 1) * LANES)
        xe_ref[sb, 0:SUBLANES, :] = jnp.where(i > 0, xp_ref[0, :, lanes], 0.0)
        xe_ref[sb, SUBLANES:SUBLANES + t_len, :] = x_ref[0, :, lanes]
        xe_ref[sb, SUBLANES + t_len:, :] = jnp.where(i < last, xn_ref[0, :, lanes], 0.0)

    ab = ab_ref[0]
    sp_in = ab + gp_ref[1:2, :]
    softplus = jnp.maximum(sp_in, 0.0) + jnp.log1p(jnp.exp(-jnp.abs(sp_in)))
    g = -jnp.exp(gp_ref[0:1, :]) * softplus
    beta = jax.nn.sigmoid(ab)

    r_t = lax.broadcasted_iota(jnp.int32, (t_len, t_len), 0)
    c_t = lax.broadcasted_iota(jnp.int32, (t_len, t_len), 1)
    same = (r_t // CHUNK) == (c_t // CHUNK)
    lower = jnp.where(same & (r_t >= c_t), 1.0, 0.0).astype(BF16)
    upper = jnp.where(same & (r_t <= c_t), 1.0, 0.0).astype(BF16)
    g_t = g.T
    cs_col = (_dot01_left(lower, g), _dot01_left(upper, g))
    cs_row = (_dot01_right(g_t, upper), _dot01_right(g_t, lower))

    n_rows = t_len // ROW_STRIDE
    for sb in range(n_slab):
        lanes = slice(sb * LANES, (sb + 1) * LANES)
        for ph in range(ROW_STRIDE):
            acc = None
            for j in range(CONV_K):
                win = xe_ref[sb, pl.ds(SUBLANES - halo + j + ph, n_rows, stride=ROW_STRIDE), :]
                term = cw_ref[j:j + 1, lanes] * win
                acc = term if acc is None else acc + term
            yv = acc * jax.nn.sigmoid(acc)
            if sb < 2 * GDN_HEADS:
                yv = yv * lax.rsqrt(jnp.sum(yv * yv, axis=-1, keepdims=True) + EPS)
            if sb < GDN_HEADS:
                yv = yv * (GDN_HEAD_DIM ** -0.5)
            yn_ref[sb, pl.ds(ph, n_rows, stride=ROW_STRIDE), :] = yv

    qs = [yn_ref[h] for h in range(GDN_HEADS)]
    ks = [yn_ref[GDN_HEADS + h] for h in range(GDN_HEADS)]
    vs = [yn_ref[2 * GDN_HEADS + h] for h in range(GDN_HEADS)]
    kts = [kh.T for kh in ks]

    r_c = lax.broadcasted_iota(jnp.int32, (CHUNK, LANES), 0)
    lane = lax.broadcasted_iota(jnp.int32, (CHUNK, LANES), 1)
    is_fwd = lane < CHUNK
    c_c = lane % CHUNK
    eye = jnp.where(r_c == c_c, 1.0, 0.0).astype(F32)
    incl = (is_fwd & (r_c >= c_c)) | (~is_fwd & (r_c <= c_c))
    strict = (is_fwd & (r_c > c_c)) | (~is_fwd & (r_c < c_c))
    r_d = lax.broadcasted_iota(jnp.int32, (2 * CHUNK, LANES), 0)
    c_d = lax.broadcasted_iota(jnp.int32, (2 * CHUNK, LANES), 1)
    same_dir = (r_d // CHUNK) == (c_d // CHUNK)
    level_mask = lambda s_, r_, c_: ((r_ // (2 * s_)) == (c_ // (2 * s_))) & ((r_ // s_) != (c_ // s_))
    stack2 = lambda t: jnp.concatenate([t, t], axis=0)

    n_chunks = t_len // CHUNK
    rows = lambda c: slice(c * CHUNK, (c + 1) * CHUNK)
    pairs = [(c, h) for c in range(n_chunks) for h in range(GDN_HEADS)]
    qkk = {}
    for c, h in pairs:
        k16 = ks[h][rows(c)].astype(BF16)
        qk16 = jnp.concatenate([qs[h][rows(c)].astype(BF16), k16], axis=0)
        qkk[c, h] = _dot_nt(qk16, stack2(k16))

    bcast = lambda col: jnp.broadcast_to(col, (CHUNK, LANES))
    g_full, b_full, grow, glast, decay, a_mat, t_mat = {}, {}, {}, {}, {}, {}, {}
    for key in pairs:
        c, h = key
        for d in range(N_DIR):
            j = d * GDN_HEADS + h
            r_last = c * CHUNK + (CHUNK - 1 if d == 0 else 0)
            g_full[key, d] = bcast(cs_col[d][rows(c), j:j + 1])
            b_full[key, d] = bcast(beta[rows(c), SUBLANES + j:SUBLANES + j + 1])
            glast[key, d] = cs_col[d][r_last:r_last + 1, j:j + 1]
            grow[key, d] = cs_row[d][j:j + 1, rows(c)]
        gcol2 = jnp.where(is_fwd, g_full[key, 0], g_full[key, 1])
        bcol2 = jnp.where(is_fwd, b_full[key, 0], b_full[key, 1])
        grow2 = jnp.concatenate([grow[key, 0], grow[key, 1]], axis=1)
        decay[key] = jnp.exp(jnp.where(incl, gcol2 - grow2, -jnp.inf))
        a_mat[key] = jnp.where(strict, bcol2 * qkk[key][CHUNK:] * decay[key], 0.0)
        t_mat[key] = eye - jnp.where(level_mask(1, r_c, c_c), a_mat[key], 0.0)

    s = 2
    while s < CHUNK:
        m_mask = same_dir & level_mask(s, r_d % CHUNK, c_d % CHUNK)
        x_mat = {key: _dot(t_mat[key], jnp.where(m_mask, stack2(a_mat[key]), 0.0)) for key in pairs}
        t_mat = {key: t_mat[key] - _dot(x_mat[key], jnp.where(same_dir, stack2(t_mat[key]), 0.0))
                 for key in pairs}
        s *= 2

    wu = {}
    for key in pairs:
        c, h = key
        kv = jnp.concatenate(
            [jnp.concatenate([ks[h][rows(c)] * (b_full[key, d] * jnp.exp(g_full[key, d])),
                              vs[h][rows(c)] * b_full[key, d]], axis=1) for d in range(N_DIR)], axis=0)
        t_sel = jnp.where(same_dir, stack2(t_mat[key]), 0.0)
        wu[key] = _dot(t_sel, kv)

    for c in range(n_chunks):
        egl_rows = []
        for d in range(N_DIR):
            for h in range(GDN_HEADS):
                key = (c, h)
                j = d * GDN_HEADS + h
                wu_d = wu[key][d * CHUNK:(d + 1) * CHUNK]
                w_ref[0, j, rows(c), :] = wu_d[:, :GDN_HEAD_DIM].astype(w_ref.dtype)
                u_ref[0, j, rows(c), :] = wu_d[:, GDN_HEAD_DIM:]
                qd_ref[0, j, rows(c), :] = (qs[h][rows(c)] * jnp.exp(g_full[key, d])).astype(qd_ref.dtype)
                kdt = kts[h][:, rows(c)] * jnp.exp(glast[key, d] - grow[key, d])
                kdt_ref[0, j, c] = kdt.astype(kdt_ref.dtype)
                egl_rows.append(jnp.broadcast_to(jnp.exp(glast[key, d]), (1, LANES)))
        egl_ref[0, c] = jnp.concatenate(egl_rows, axis=0)
        for h in range(GDN_HEADS):
            qk_ref[0, h, rows(c), :] = (qkk[c, h][:CHUNK] * decay[c, h]).astype(qk_ref.dtype)


def _gdn_prep(qkv_g, ab, conv_w, gate_par):
    b, s, _ = qkv_g.shape
    nt = s // PREP_T
    cpb = PREP_T // CHUNK
    nc = s // CHUNK
    nh8 = PREP_T // SUBLANES
    chain = lambda last: pl.BlockSpec((1, N_CHAIN, PREP_T, last), lambda bi, i: (bi, 0, i, 0))
    return pl.pallas_call(
        _gprep_kernel,
        grid=(b, nt),
        in_specs=[pl.BlockSpec((1, SUBLANES, 3 * GDN_W), lambda bi, i: (bi, jnp.maximum(i * nh8 - 1, 0), 0)),
                  pl.BlockSpec((1, PREP_T, 3 * GDN_W), lambda bi, i: (bi, i, 0)),
                  pl.BlockSpec((1, SUBLANES, 3 * GDN_W),
                               lambda bi, i: (bi, jnp.minimum((i + 1) * nh8, s // SUBLANES - 1), 0)),
                  pl.BlockSpec((1, PREP_T, LANES), lambda bi, i: (bi, i, 0)),
                  pl.BlockSpec((SUBLANES, 3 * GDN_W), lambda bi, i: (0, 0)),
                  pl.BlockSpec((SUBLANES, LANES), lambda bi, i: (0, 0))],
        out_specs=[chain(GDN_HEAD_DIM), chain(GDN_HEAD_DIM), chain(GDN_HEAD_DIM),
                   pl.BlockSpec((1, N_CHAIN, cpb, GDN_HEAD_DIM, CHUNK), lambda bi, i: (bi, 0, i, 0, 0)),
                   pl.BlockSpec((1, GDN_HEADS, PREP_T, LANES), lambda bi, i: (bi, 0, i, 0)),
                   pl.BlockSpec((1, cpb, N_CHAIN, LANES), lambda bi, i: (bi, i, 0, 0))],
        out_shape=[jax.ShapeDtypeStruct((b, N_CHAIN, s, GDN_HEAD_DIM), BF16),
                   jax.ShapeDtypeStruct((b, N_CHAIN, s, GDN_HEAD_DIM), F32),
                   jax.ShapeDtypeStruct((b, N_CHAIN, s, GDN_HEAD_DIM), BF16),
                   jax.ShapeDtypeStruct((b, N_CHAIN, nc, GDN_HEAD_DIM, CHUNK), BF16),
                   jax.ShapeDtypeStruct((b, GDN_HEADS, s, LANES), BF16),
                   jax.ShapeDtypeStruct((b, nc, N_CHAIN, LANES), F32)],
        scratch_shapes=[pltpu.VMEM((3 * GDN_W // LANES, PREP_T + 2 * SUBLANES, LANES), F32),
                        pltpu.VMEM((3 * GDN_W // LANES, PREP_T, LANES), F32)],
        compiler_params=pltpu.CompilerParams(dimension_semantics=("arbitrary", "arbitrary"),
                                             vmem_limit_bytes=VMEM_LIMIT),
        name="gdn_prep",
    )(qkv_g, qkv_g, qkv_g, ab, conv_w, gate_par)


def _gscan_kernel(egl_ref, wf_ref, wb_ref, uf_ref, ub_ref, qf_ref, qb_ref, kf_ref, kb_ref,
                  pf_ref, pb_ref, of_ref, ob_ref, state_ref):
    t = pl.program_id(0)
    nc = pl.num_programs(0)
    n_batch = wf_ref.shape[0]

    @pl.when(t == 0)
    def _():
        state_ref[...] = jnp.zeros_like(state_ref)

    dirs = ((wf_ref, uf_ref, qf_ref, kf_ref, pf_ref, of_ref), (wb_ref, ub_ref, qb_ref, kb_ref, pb_ref, ob_ref))
    chains = [(bi, d, h) for bi in range(n_batch) for d in range(N_DIR) for h in range(GDN_HEADS)]
    slot = lambda bi, d, h: (bi * N_DIR + d) * GDN_HEADS + h

    st, r = {}, {}
    for key in chains:
        bi, d, h = key
        w_ref, _, q_ref = dirs[d][:3]
        st[key] = state_ref[slot(*key)]
        wq = jnp.concatenate([w_ref[bi, h], q_ref[bi, h]], axis=0)
        r[key] = _dot(wq, st[key])
    v_new, intra = {}, {}
    for key in chains:
        bi, d, h = key
        u_ref, p_ref = dirs[d][1], dirs[d][4]
        v_new[key] = (u_ref[bi, h] - r[key][:CHUNK]).astype(BF16)
        zeros = jnp.zeros_like(v_new[key])
        v_pad = jnp.concatenate([v_new[key], zeros] if d == 0 else [zeros, v_new[key]], axis=0)
        intra[key] = jnp.dot(p_ref[bi, h], v_pad, preferred_element_type=F32)
    for key in chains:
        bi, d, h = key
        k_ref = dirs[d][3]
        c = t if d == 0 else nc - 1 - t
        egl = egl_ref[(bi * nc + c) * N_CHAIN + d * GDN_HEADS + h]
        state_ref[slot(*key)] = st[key] * egl + jnp.dot(k_ref[bi, h, 0], v_new[key],
                                                        preferred_element_type=F32)
    for bi in range(n_batch):
        for d in range(N_DIR):
            o_ref = dirs[d][5]
            o_ref[bi] = jnp.concatenate(
                [r[bi, d, h][CHUNK:] + intra[bi, d, h] for h in range(GDN_HEADS)], axis=-1)


def _gdn_scan(egl, w, u, qd, kdt, qk):
    b, _, s, _ = w.shape
    nc = s // CHUNK
    fwd = lambda last: pl.BlockSpec((b, GDN_HEADS, CHUNK, last), lambda t: (0, 0, t, 0))
    bwd = lambda last: pl.BlockSpec((b, GDN_HEADS, CHUNK, last), lambda t: (0, 1, nc - 1 - t, 0))
    kf = pl.BlockSpec((b, GDN_HEADS, 1, GDN_HEAD_DIM, CHUNK), lambda t: (0, 0, t, 0, 0))
    kb = pl.BlockSpec((b, GDN_HEADS, 1, GDN_HEAD_DIM, CHUNK), lambda t: (0, 1, nc - 1 - t, 0, 0))
    d = GDN_HEAD_DIM
    return pl.pallas_call(
        _gscan_kernel,
        grid=(nc,),
        in_specs=[pl.BlockSpec(memory_space=pltpu.SMEM),
                  fwd(d), bwd(d), fwd(d), bwd(d), fwd(d), bwd(d), kf, kb,
                  pl.BlockSpec((b, GDN_HEADS, CHUNK, LANES), lambda t: (0, 0, t, 0)),
                  pl.BlockSpec((b, GDN_HEADS, CHUNK, LANES), lambda t: (0, 0, nc - 1 - t, 0))],
        out_specs=[pl.BlockSpec((b, CHUNK, GDN_W), lambda t: (0, t, 0)),
                   pl.BlockSpec((b, CHUNK, GDN_W), lambda t: (0, nc - 1 - t, 0))],
        out_shape=[jax.ShapeDtypeStruct((b, s, GDN_W), F32),
                   jax.ShapeDtypeStruct((b, s, GDN_W), F32)],
        scratch_shapes=[pltpu.VMEM((b * N_CHAIN, GDN_HEAD_DIM, GDN_HEAD_DIM), F32)],
        compiler_params=pltpu.CompilerParams(dimension_semantics=("arbitrary",),
                                             vmem_limit_bytes=VMEM_LIMIT),
        name="gdn_scan",
    )(egl, w, w, u, u, qd, qd, kdt, kdt, qk, qk)


def _ffn_kernel(x_ref, attn_ref, of_ref, ob_ref, z_ref, gnw_ref, wo_ref, fnw_ref, w1_ref, w2_ref,
                onw_ref, o_ref):
    o = of_ref[...] + ob_ref[...]
    heads = []
    for h in range(GDN_HEADS):
        oh = o[:, h * GDN_HEAD_DIM:(h + 1) * GDN_HEAD_DIM]
        ms = jnp.mean(oh * oh, axis=-1, keepdims=True)
        heads.append(oh * lax.rsqrt(ms + EPS) * gnw_ref[...])
    z = z_ref[...]
    gdn = jnp.concatenate(heads, axis=-1) * (z * jax.nn.sigmoid(z))
    hres = (x_ref[...] + jnp.dot(attn_ref[...], wo_ref[:ATTN_Q, :], preferred_element_type=F32)
            + _dot(gdn, wo_ref[ATTN_Q:, :]))
    ms = jnp.mean(hres * hres, axis=-1, keepdims=True)
    hn = (hres * lax.rsqrt(ms + EPS) * fnw_ref[...]).astype(BF16)
    a = jnp.dot(hn, w1_ref[...], preferred_element_type=F32)
    a = jnp.square(jnp.maximum(a, 0.0)).astype(BF16)
    acc = hres + jnp.dot(a, w2_ref[...], preferred_element_type=F32)
    ms = jnp.mean(acc * acc, axis=-1, keepdims=True)
    o_ref[...] = acc * lax.rsqrt(ms + EPS) * onw_ref[...]


def _out_ffn(x2, attn, o_f, o_b, z, gnw, wo, fnw, w1, w2, onw):
    n = x2.shape[0]
    row = lambda w: pl.BlockSpec((FFN_TM, w), lambda i: (i, 0))
    full = lambda a: pl.BlockSpec(a.shape, lambda i: (0, 0))
    return pl.pallas_call(
        _ffn_kernel,
        grid=(n // FFN_TM,),
        in_specs=[row(D_MODEL), row(ATTN_Q), row(GDN_W), row(GDN_W), row(GDN_W),
                  full(gnw), full(wo), full(fnw), full(w1), full(w2), full(onw)],
        out_specs=row(D_MODEL),
        out_shape=jax.ShapeDtypeStruct((n, D_MODEL), F32),
        compiler_params=pltpu.CompilerParams(dimension_semantics=("arbitrary",),
                                             vmem_limit_bytes=VMEM_LIMIT),
        name="out_ffn",
    )(x2, attn, o_f, o_b, z, gnw, wo, fnw, w1, w2, onw)


def _layer(h, band, norm_mix_w, w_in, attn_sink, conv_w, gdn_a_log, gdn_dt_bias, gdn_norm_w,
           w_out, norm_ffn_w, w_ffn_in, w_ffn_out, out_norm_w):
    b, s, _ = h.shape
    n = b * s
    o_qkvg = ATTN_Q + 2 * ATTN_KV
    o_z = o_qkvg + 3 * GDN_W
    o_ab = o_z + GDN_W
    w_cat = jnp.concatenate(
        [w_in[:, o_qkvg:o_z], w_in[:, o_z:o_ab], w_in[:, :o_qkvg], w_in[:, o_ab:],
         jnp.zeros((D_MODEL, LANES - 2 * N_CHAIN), w_in.dtype)], axis=1).astype(BF16)
    x2 = h.reshape(n, D_MODEL)
    qkv_g, z_g, q_a, k_a, v_a, ab = _proj(x2, norm_mix_w.reshape(1, D_MODEL), w_cat)

    attn = _attention(q_a.reshape(b, s, ATTN_Q), k_a.reshape(b, s, ATTN_KV), v_a.reshape(b, s, ATTN_KV),
                      band, attn_sink)

    conv_pad = jnp.zeros((SUBLANES, 3 * GDN_W), F32).at[:CONV_K].set(conv_w)
    gate_par = jnp.zeros((SUBLANES, LANES), F32)
    gate_par = gate_par.at[0, :N_CHAIN].set(gdn_a_log.reshape(-1)).at[1, :N_CHAIN].set(gdn_dt_bias.reshape(-1))
    w_c, u_c, q_dec, k_dec_t, qk, egl = _gdn_prep(qkv_g.reshape(b, s, 3 * GDN_W), ab.reshape(b, s, LANES),
                                                  conv_pad, gate_par)
    o_f, o_b = _gdn_scan(egl[..., 0].reshape(-1), w_c, u_c, q_dec, k_dec_t, qk)

    out = _out_ffn(x2, attn.reshape(n, ATTN_Q), o_f.reshape(n, GDN_W), o_b.reshape(n, GDN_W), z_g,
                   gdn_norm_w.reshape(1, GDN_HEAD_DIM), w_out.astype(BF16), norm_ffn_w.reshape(1, D_MODEL),
                   w_ffn_in.astype(BF16), w_ffn_out.astype(BF16), out_norm_w.reshape(1, D_MODEL))
    return out.reshape(b, s, D_MODEL)


def kernel(x, norm_mix_w, w_in, rel_bias, attn_sink, conv_w, gdn_a_log, gdn_dt_bias, gdn_norm_w, w_out,
           norm_ffn_w, w_ffn_in, w_ffn_out, norm_final_w):
    depth = w_in.shape[0]
    assert depth == 1, "the fused output kernel applies the final norm after the single trunk layer"
    rel = (np.arange(3 * BLOCK)[None, :] - BLOCK) - np.arange(BLOCK)[:, None]
    bucket = _t5_buckets(jnp.asarray(rel, dtype=jnp.int32))
    band = _bias_band(rel_bias, bucket)
    return _layer(x, band, norm_mix_w[0], w_in[0], attn_sink[0], conv_w[0], gdn_a_log[0], gdn_dt_bias[0],
                  gdn_norm_w[0], w_out[0], norm_ffn_w[0], w_ffn_in[0], w_ffn_out[0], norm_final_w)
```

```python
import functools
import math

import jax
import jax.numpy as jnp
import numpy as np
from jax import lax
from jax.experimental import pallas as pl
from jax.experimental.pallas import tpu as pltpu

F32 = jnp.float32
BF16 = jnp.bfloat16

D_MODEL = 1024
ATTN_HEADS = 8
ATTN_KV_HEADS = 2
ATTN_HEAD_DIM = 64
ATTN_GROUP = ATTN_HEADS // ATTN_KV_HEADS
WINDOW = 128
BLOCK = 128
N_BUCKETS = 32
MAX_DISTANCE = 128
GDN_HEADS = 4
GDN_HEAD_DIM = 128
CONV_K = 5
CHUNK = 64
N_DIR = 2
N_CHAIN = N_DIR * GDN_HEADS
D_FF = 4 * D_MODEL
EPS = 1e-6
ATTN_Q = ATTN_HEADS * ATTN_HEAD_DIM
ATTN_KV = ATTN_KV_HEADS * ATTN_HEAD_DIM
GDN_W = GDN_HEADS * GDN_HEAD_DIM
LANES = 128
SUBLANES = 8
VMEM_LIMIT = 56 * 1024 * 1024

PROJ_TM = 512
ATTN_QB = 4
PREP_T = 256
ROW_STRIDE = 4
SCAN_CHUNKS = 4
FFN_TM = 512


def _dot(a, b):
    return jnp.dot(a.astype(BF16), b.astype(BF16), preferred_element_type=F32)


def _dot_nt(a, b):
    return lax.dot_general(a.astype(BF16), b.astype(BF16), (((1,), (1,)), ((), ())),
                           preferred_element_type=F32)


def _split3(x):
    hi = x.astype(BF16)
    r1 = x - hi.astype(F32)
    mid = r1.astype(BF16)
    lo = (r1 - mid.astype(F32)).astype(BF16)
    return hi, mid, lo


def _dot01_right(x, m01):
    hi, mid, lo = _split3(x)
    d = lambda p: jnp.dot(p, m01, preferred_element_type=F32)
    return d(hi) + d(mid) + d(lo)


def _proj_kernel(x_ref, nw_ref, w_ref, qkvg_ref, z_ref, qa_ref, ka_ref, va_ref, ab_ref):
    x = x_ref[...]
    ms = jnp.mean(x * x, axis=-1, keepdims=True)
    xn = (x * lax.rsqrt(ms + EPS) * nw_ref[...]).astype(BF16)
    c0 = 0
    for ref, width, scale in ((qkvg_ref, 3 * GDN_W, None), (z_ref, GDN_W, None),
                              (qa_ref, ATTN_Q, ATTN_HEAD_DIM ** -0.5), (ka_ref, ATTN_KV, None),
                              (va_ref, ATTN_KV, None), (ab_ref, LANES, None)):
        y = jnp.dot(xn, w_ref[:, c0:c0 + width], preferred_element_type=F32)
        if scale is not None:
            y = y * scale
        ref[...] = y.astype(ref.dtype)
        c0 += width


def _proj(x2, norm_w, w_cat):
    n = x2.shape[0]
    d_cat = w_cat.shape[1]
    row = lambda w: pl.BlockSpec((PROJ_TM, w), lambda i: (i, 0))
    return pl.pallas_call(
        _proj_kernel,
        grid=(n // PROJ_TM,),
        in_specs=[row(D_MODEL),
                  pl.BlockSpec((1, D_MODEL), lambda i: (0, 0)),
                  pl.BlockSpec((D_MODEL, d_cat), lambda i: (0, 0))],
        out_specs=[row(3 * GDN_W), row(GDN_W), row(ATTN_Q), row(ATTN_KV), row(ATTN_KV), row(LANES)],
        out_shape=[jax.ShapeDtypeStruct((n, 3 * GDN_W), F32),
                   jax.ShapeDtypeStruct((n, GDN_W), F32),
                   jax.ShapeDtypeStruct((n, ATTN_Q), BF16),
                   jax.ShapeDtypeStruct((n, ATTN_KV), BF16),
                   jax.ShapeDtypeStruct((n, ATTN_KV), BF16),
                   jax.ShapeDtypeStruct((n, LANES), F32)],
        compiler_params=pltpu.CompilerParams(dimension_semantics=("arbitrary",),
                                             vmem_limit_bytes=VMEM_LIMIT),
        name="proj",
    )(x2, norm_w, w_cat)


def _bias_kernel(relb_ref, bucket_ref, o_ref):
    bucket = bucket_ref[...]
    key = lax.broadcasted_iota(jnp.int32, (3 * BLOCK, BLOCK), 0)
    qry = lax.broadcasted_iota(jnp.int32, (3 * BLOCK, BLOCK), 1)
    in_window = jnp.abs(key - BLOCK - qry) <= WINDOW
    for h in range(ATTN_HEADS):
        acc = jnp.zeros((3 * BLOCK, BLOCK), F32)
        for b in range(N_BUCKETS):
            acc = jnp.where(bucket == b, relb_ref[b, h], acc)
        o_ref[h // 2, :, (h % 2) * BLOCK:(h % 2 + 1) * BLOCK] = jnp.where(in_window, acc, -1e30)


def _bias_band(rel_bias, bucket_t):
    shape = (ATTN_HEADS // 2, 3 * BLOCK, 2 * BLOCK)
    return pl.pallas_call(
        _bias_kernel,
        in_specs=[pl.BlockSpec(memory_space=pltpu.SMEM),
                  pl.BlockSpec((3 * BLOCK, BLOCK), lambda: (0, 0))],
        out_specs=pl.BlockSpec(shape, lambda: (0, 0, 0)),
        out_shape=jax.ShapeDtypeStruct(shape, F32),
        name="bias_band",
    )(rel_bias, bucket_t)


def _t5_buckets(rel):
    nb = N_BUCKETS // 2
    max_exact = nb // 2
    base = jnp.where(rel > 0, nb, 0)
    n = jnp.abs(rel)
    log_ratio = jnp.log(jnp.maximum(n, 1).astype(jnp.float32) / max_exact) / math.log(MAX_DISTANCE / max_exact)
    large = jnp.minimum(max_exact + (log_ratio * (nb - max_exact)).astype(jnp.int32), nb - 1)
    return base + jnp.where(n < max_exact, n, large)


def _attn_kernel(sink_ref, q_ref, kp_ref, kc_ref, kn_ref, vp_ref, vc_ref, vn_ref, bias_ref, o_ref):
    n = pl.program_id(1)
    last = pl.num_programs(1) - 1
    kband = jnp.concatenate([kp_ref[0], kc_ref[0], kn_ref[0]], axis=0)
    vband = jnp.concatenate([vp_ref[0], vc_ref[0], vn_ref[0]], axis=0)
    vband_t = vband.astype(F32).T.astype(BF16)
    key = lax.broadcasted_iota(jnp.int32, (3 * BLOCK, 1), 0)
    first_head = lax.broadcasted_iota(jnp.int32, (1, 2 * BLOCK), 1) < BLOCK
    head = lambda t, i: t[:, i * ATTN_HEAD_DIM:(i + 1) * ATTN_HEAD_DIM]
    n_pairs = ATTN_HEADS // 2
    kv_of = lambda pr: (2 * pr) // ATTN_GROUP
    units = [(j, pr) for j in range(ATTN_QB) for pr in range(n_pairs)]
    band_rows = lambda j: slice(j * BLOCK, (j + 3) * BLOCK)

    scores = {}
    for j, pr in units:
        qj = q_ref[0, j * BLOCK:(j + 1) * BLOCK, :]
        q2 = jnp.concatenate([head(qj, 2 * pr), head(qj, 2 * pr + 1)], axis=0)
        scores[j, pr] = _dot_nt(head(kband, kv_of(pr))[band_rows(j)], q2)
    probs, dens = {}, {}
    for j, pr in units:
        s = scores[j, pr] + bias_ref[pr]
        if j == 0:
            s = jnp.where((key < BLOCK) & (n == 0), -1e30, s)
        if j == ATTN_QB - 1:
            s = jnp.where((key >= 2 * BLOCK) & (n == last), -1e30, s)
        sink = jnp.where(first_head, sink_ref[2 * pr], sink_ref[2 * pr + 1])
        m = jnp.maximum(jnp.max(s, axis=0, keepdims=True), sink)
        p = jnp.exp(s - m)
        dens[j, pr] = jnp.sum(p, axis=0, keepdims=True) + jnp.exp(sink - m)
        probs[j, pr] = p.astype(BF16)
    outs_t = {}
    for j, pr in units:
        kv = kv_of(pr)
        v_t = vband_t[kv * ATTN_HEAD_DIM:(kv + 1) * ATTN_HEAD_DIM, band_rows(j)]
        outs_t[j, pr] = jnp.dot(v_t, probs[j, pr], preferred_element_type=F32) / dens[j, pr]
    for j in range(ATTN_QB):
        o_t = jnp.concatenate([outs_t[j, pr][:, half * BLOCK:(half + 1) * BLOCK]
                               for pr in range(n_pairs) for half in range(2)], axis=0)
        o_ref[0, j * BLOCK:(j + 1) * BLOCK, :] = o_t.T.astype(o_ref.dtype)


def _attention(q_a, k_a, v_a, band, sink):
    b, s, _ = q_a.shape
    nb = s // BLOCK
    rows = ATTN_QB * BLOCK
    kv_spec = lambda r, f: pl.BlockSpec((1, r, ATTN_KV), f)
    prev = lambda bi, n: (bi, jnp.maximum(n * ATTN_QB - 1, 0), 0)
    cur = lambda bi, n: (bi, n, 0)
    nxt = lambda bi, n: (bi, jnp.minimum((n + 1) * ATTN_QB, nb - 1), 0)
    return pl.pallas_call(
        _attn_kernel,
        grid=(b, nb // ATTN_QB),
        in_specs=[pl.BlockSpec(memory_space=pltpu.SMEM),
                  pl.BlockSpec((1, rows, ATTN_Q), cur),
                  kv_spec(BLOCK, prev), kv_spec(rows, cur), kv_spec(BLOCK, nxt),
                  kv_spec(BLOCK, prev), kv_spec(rows, cur), kv_spec(BLOCK, nxt),
                  pl.BlockSpec(band.shape, lambda bi, n: (0, 0, 0))],
        out_specs=pl.BlockSpec((1, rows, ATTN_Q), cur),
        out_shape=jax.ShapeDtypeStruct((b, s, ATTN_Q), BF16),
        compiler_params=pltpu.CompilerParams(dimension_semantics=("arbitrary", "arbitrary"),
                                             vmem_limit_bytes=VMEM_LIMIT),
        name="attn",
    )(sink, q_a, k_a, k_a, k_a, v_a, v_a, v_a, band)


def _gprep_kernel(xp_ref, x_ref, xn_ref, ab_ref, cw_ref, gp_ref,
                  w_ref, u_ref, qd_ref, kdt_ref, qk_ref, egl_ref, xe_ref, yn_ref):
    i = pl.program_id(1)
    last = pl.num_programs(1) - 1
    t_len = PREP_T
    halo = CONV_K // 2

    n_slab = 3 * GDN_W // LANES
    for sb in range(n_slab):
        lanes = slice(sb * LANES, (sb + 1) * LANES)
        xe_ref[sb, 0:SUBLANES, :] = jnp.where(i > 0, xp_ref[0, :, lanes], 0.0)
        xe_ref[sb, SUBLANES:SUBLANES + t_len, :] = x_ref[0, :, lanes]
        xe_ref[sb, SUBLANES + t_len:, :] = jnp.where(i < last, xn_ref[0, :, lanes], 0.0)

    ab = ab_ref[0]
    sp_in = ab + gp_ref[1:2, :]
    softplus = jnp.maximum(sp_in, 0.0) + jnp.log1p(jnp.exp(-jnp.abs(sp_in)))
    g = -jnp.exp(gp_ref[0:1, :]) * softplus
    beta = jax.nn.sigmoid(ab)

    r_t = lax.broadcasted_iota(jnp.int32, (t_len, t_len), 0)
    c_t = lax.broadcasted_iota(jnp.int32, (t_len, t_len), 1)
    same = (r_t // CHUNK) == (c_t // CHUNK)
    lower = jnp.where(same & (r_t >= c_t), 1.0, 0.0).astype(BF16)
    upper = jnp.where(same & (r_t <= c_t), 1.0, 0.0).astype(BF16)
    g_t = g.T[:2 * SUBLANES]
    cs_row = (_dot01_right(g_t, upper), _dot01_right(g_t, lower))
    pad_rows = jnp.zeros((LANES - 2 * SUBLANES, t_len), F32)
    cs_col = tuple(jnp.concatenate([r, pad_rows], axis=0).T for r in cs_row)

    n_rows = t_len // ROW_STRIDE
    for sb in range(n_slab):
        lanes = slice(sb * LANES, (sb + 1) * LANES)
        for ph in range(ROW_STRIDE):
            acc = None
            for j in range(CONV_K):
                win = xe_ref[sb, pl.ds(SUBLANES - halo + j + ph, n_rows, stride=ROW_STRIDE), :]
                term = cw_ref[j:j + 1, lanes] * win
                acc = term if acc is None else acc + term
            yv = acc * jax.nn.sigmoid(acc)
            if sb < 2 * GDN_HEADS:
                yv = yv * lax.rsqrt(jnp.sum(yv * yv, axis=-1, keepdims=True) + EPS)
            if sb < GDN_HEADS:
                yv = yv * (GDN_HEAD_DIM ** -0.5)
            yn_ref[sb, pl.ds(ph, n_rows, stride=ROW_STRIDE), :] = yv

    qs = [yn_ref[h] for h in range(GDN_HEADS)]
    ks = [yn_ref[GDN_HEADS + h] for h in range(GDN_HEADS)]
    vs = [yn_ref[2 * GDN_HEADS + h] for h in range(GDN_HEADS)]
    kts = [kh.T for kh in ks]

    r_c = lax.broadcasted_iota(jnp.int32, (CHUNK, LANES), 0)
    lane = lax.broadcasted_iota(jnp.int32, (CHUNK, LANES), 1)
    is_fwd = lane < CHUNK
    c_c = lane % CHUNK
    eye = jnp.where(r_c == c_c, 1.0, 0.0).astype(F32)
    incl = (is_fwd & (r_c >= c_c)) | (~is_fwd & (r_c <= c_c))
    strict = (is_fwd & (r_c > c_c)) | (~is_fwd & (r_c < c_c))
    r_d = lax.broadcasted_iota(jnp.int32, (2 * CHUNK, LANES), 0)
    c_d = lax.broadcasted_iota(jnp.int32, (2 * CHUNK, LANES), 1)
    same_dir = (r_d // CHUNK) == (c_d // CHUNK)
    level_mask = lambda s_, r_, c_: ((r_ // (2 * s_)) == (c_ // (2 * s_))) & ((r_ // s_) != (c_ // s_))
    stack2 = lambda t: jnp.concatenate([t, t], axis=0)

    n_chunks = t_len // CHUNK
    rows = lambda c: slice(c * CHUNK, (c + 1) * CHUNK)
    pairs = [(c, h) for c in range(n_chunks) for h in range(GDN_HEADS)]
    qkk = {}
    for c, h in pairs:
        k16 = ks[h][rows(c)].astype(BF16)
        qk16 = jnp.concatenate([qs[h][rows(c)].astype(BF16), k16], axis=0)
        qkk[c, h] = _dot_nt(qk16, stack2(k16))

    bcast = lambda col: jnp.broadcast_to(col, (CHUNK, LANES))
    g_full, b_full, grow, glast, decay, a_mat, t_mat = {}, {}, {}, {}, {}, {}, {}
    for key in pairs:
        c, h = key
        for d in range(N_DIR):
            j = d * GDN_HEADS + h
            r_last = c * CHUNK + (CHUNK - 1 if d == 0 else 0)
            g_full[key, d] = bcast(cs_col[d][rows(c), j:j + 1])
            b_full[key, d] = bcast(beta[rows(c), SUBLANES + j:SUBLANES + j + 1])
            glast[key, d] = cs_col[d][r_last:r_last + 1, j:j + 1]
            grow[key, d] = cs_row[d][j:j + 1, rows(c)]
        gcol2 = jnp.where(is_fwd, g_full[key, 0], g_full[key, 1])
        bcol2 = jnp.where(is_fwd, b_full[key, 0], b_full[key, 1])
        grow2 = jnp.concatenate([grow[key, 0], grow[key, 1]], axis=1)
        decay[key] = jnp.exp(jnp.where(incl, gcol2 - grow2, -jnp.inf))
        a_mat[key] = jnp.where(strict, bcol2 * qkk[key][CHUNK:] * decay[key], 0.0)
        t_mat[key] = eye - jnp.where(level_mask(1, r_c, c_c), a_mat[key], 0.0)

    s = 2
    while s < CHUNK:
        m_mask = same_dir & level_mask(s, r_d % CHUNK, c_d % CHUNK)
        x_mat = {key: _dot(t_mat[key], jnp.where(m_mask, stack2(a_mat[key]), 0.0)) for key in pairs}
        t_mat = {key: t_mat[key] - _dot(x_mat[key], jnp.where(same_dir, stack2(t_mat[key]), 0.0))
                 for key in pairs}
        s *= 2

    wu = {}
    for key in pairs:
        c, h = key
        kv = jnp.concatenate(
            [jnp.concatenate([ks[h][rows(c)] * (b_full[key, d] * jnp.exp(g_full[key, d])),
                              vs[h][rows(c)] * b_full[key, d]], axis=1) for d in range(N_DIR)], axis=0)
        t_sel = jnp.where(same_dir, stack2(t_mat[key]), 0.0)
        wu[key] = _dot(t_sel, kv)

    for c in range(n_chunks):
        egl_rows = []
        for d in range(N_DIR):
            for h in range(GDN_HEADS):
                key = (c, h)
                j = d * GDN_HEADS + h
                wu_d = wu[key][d * CHUNK:(d + 1) * CHUNK]
                w_ref[0, j, rows(c), :] = wu_d[:, :GDN_HEAD_DIM].astype(w_ref.dtype)
                u_ref[0, j, rows(c), :] = wu_d[:, GDN_HEAD_DIM:]
                qd_ref[0, j, rows(c), :] = (qs[h][rows(c)] * jnp.exp(g_full[key, d])).astype(qd_ref.dtype)
                kdt = kts[h][:, rows(c)] * jnp.exp(glast[key, d] - grow[key, d])
                kdt_ref[0, j, c] = kdt.astype(kdt_ref.dtype)
                egl_rows.append(jnp.broadcast_to(jnp.exp(glast[key, d]), (1, LANES)))
        egl_ref[0, c] = jnp.concatenate(egl_rows, axis=0)
        for h in range(GDN_HEADS):
            qk_ref[0, h, rows(c), :] = (qkk[c, h][:CHUNK] * decay[c, h]).astype(qk_ref.dtype)


def _gdn_prep(qkv_g, ab, conv_w, gate_par):
    b, s, _ = qkv_g.shape
    nt = s // PREP_T
    cpb = PREP_T // CHUNK
    nc = s // CHUNK
    nh8 = PREP_T // SUBLANES
    chain = lambda last: pl.BlockSpec((1, N_CHAIN, PREP_T, last), lambda bi, i: (bi, 0, i, 0))
    return pl.pallas_call(
        _gprep_kernel,
        grid=(b, nt),
        in_specs=[pl.BlockSpec((1, SUBLANES, 3 * GDN_W), lambda bi, i: (bi, jnp.maximum(i * nh8 - 1, 0), 0)),
                  pl.BlockSpec((1, PREP_T, 3 * GDN_W), lambda bi, i: (bi, i, 0)),
                  pl.BlockSpec((1, SUBLANES, 3 * GDN_W),
                               lambda bi, i: (bi, jnp.minimum((i + 1) * nh8, s // SUBLANES - 1), 0)),
                  pl.BlockSpec((1, PREP_T, LANES), lambda bi, i: (bi, i, 0)),
                  pl.BlockSpec((SUBLANES, 3 * GDN_W), lambda bi, i: (0, 0)),
                  pl.BlockSpec((SUBLANES, LANES), lambda bi, i: (0, 0))],
        out_specs=[chain(GDN_HEAD_DIM), chain(GDN_HEAD_DIM), chain(GDN_HEAD_DIM),
                   pl.BlockSpec((1, N_CHAIN, cpb, GDN_HEAD_DIM, CHUNK), lambda bi, i: (bi, 0, i, 0, 0)),
                   pl.BlockSpec((1, GDN_HEADS, PREP_T, LANES), lambda bi, i: (bi, 0, i, 0)),
                   pl.BlockSpec((1, cpb, N_CHAIN, LANES), lambda bi, i: (bi, i, 0, 0))],
        out_shape=[jax.ShapeDtypeStruct((b, N_CHAIN, s, GDN_HEAD_DIM), BF16),
                   jax.ShapeDtypeStruct((b, N_CHAIN, s, GDN_HEAD_DIM), F32),
                   jax.ShapeDtypeStruct((b, N_CHAIN, s, GDN_HEAD_DIM), BF16),
                   jax.ShapeDtypeStruct((b, N_CHAIN, nc, GDN_HEAD_DIM, CHUNK), BF16),
                   jax.ShapeDtypeStruct((b, GDN_HEADS, s, LANES), BF16),
                   jax.ShapeDtypeStruct((b, nc, N_CHAIN, LANES), F32)],
        scratch_shapes=[pltpu.VMEM((3 * GDN_W // LANES, PREP_T + 2 * SUBLANES, LANES), F32),
                        pltpu.VMEM((3 * GDN_W // LANES, PREP_T, LANES), F32)],
        compiler_params=pltpu.CompilerParams(dimension_semantics=("arbitrary", "arbitrary"),
                                             vmem_limit_bytes=VMEM_LIMIT),
        name="gdn_prep",
    )(qkv_g, qkv_g, qkv_g, ab, conv_w, gate_par)


def _gscan_kernel(egl_ref, wf_ref, wb_ref, uf_ref, ub_ref, qf_ref, qb_ref, kf_ref, kb_ref,
                  pf_ref, pb_ref, of_ref, ob_ref, state_ref):
    t = pl.program_id(0)
    n_steps = pl.num_programs(0)
    nc = n_steps * SCAN_CHUNKS
    n_batch = wf_ref.shape[0]

    @pl.when(t == 0)
    def _():
        state_ref[...] = jnp.zeros_like(state_ref)

    dirs = ((wf_ref, uf_ref, qf_ref, kf_ref, pf_ref, of_ref), (wb_ref, ub_ref, qb_ref, kb_ref, pb_ref, ob_ref))
    chains = [(bi, d, h) for bi in range(n_batch) for d in range(N_DIR) for h in range(GDN_HEADS)]
    slot = lambda bi, d, h: (bi * N_DIR + d) * GDN_HEADS + h
    st = {key: state_ref[slot(*key)] for key in chains}

    for sub in range(SCAN_CHUNKS):
        local = (sub, SCAN_CHUNKS - 1 - sub)
        rows = [slice(c * CHUNK, (c + 1) * CHUNK) for c in local]
        chunk = (t * SCAN_CHUNKS + sub, nc - 1 - (t * SCAN_CHUNKS + sub))
        r = {}
        for key in chains:
            bi, d, h = key
            w_ref, _, q_ref = dirs[d][:3]
            wq = jnp.concatenate([w_ref[bi, h, rows[d]], q_ref[bi, h, rows[d]]], axis=0)
            r[key] = _dot(wq, st[key])
        v_new, intra = {}, {}
        for key in chains:
            bi, d, h = key
            u_ref, p_ref = dirs[d][1], dirs[d][4]
            v_new[key] = (u_ref[bi, h, rows[d]] - r[key][:CHUNK]).astype(BF16)
            zeros = jnp.zeros_like(v_new[key])
            v_pad = jnp.concatenate([v_new[key], zeros] if d == 0 else [zeros, v_new[key]], axis=0)
            intra[key] = jnp.dot(p_ref[bi, h, rows[d]], v_pad, preferred_element_type=F32)
        for key in chains:
            bi, d, h = key
            k_ref = dirs[d][3]
            egl = egl_ref[(bi * nc + chunk[d]) * N_CHAIN + d * GDN_HEADS + h]
            st[key] = st[key] * egl + jnp.dot(k_ref[bi, h, local[d]], v_new[key], preferred_element_type=F32)
        for bi in range(n_batch):
            for d in range(N_DIR):
                o_ref = dirs[d][5]
                o_ref[bi, rows[d], :] = jnp.concatenate(
                    [r[bi, d, h][CHUNK:] + intra[bi, d, h] for h in range(GDN_HEADS)], axis=-1)

    for key in chains:
        state_ref[slot(*key)] = st[key]


def _gdn_scan(egl, w, u, qd, kdt, qk):
    b, _, s, _ = w.shape
    rows = SCAN_CHUNKS * CHUNK
    n_steps = s // rows
    fwd_i = lambda t: t
    bwd_i = lambda t: n_steps - 1 - t
    chain = lambda d, at, last: pl.BlockSpec((b, GDN_HEADS, rows, last), lambda t: (0, d, at(t), 0))
    kspec = lambda d, at: pl.BlockSpec((b, GDN_HEADS, SCAN_CHUNKS, GDN_HEAD_DIM, CHUNK),
                                       lambda t: (0, d, at(t), 0, 0))
    dk = GDN_HEAD_DIM
    return pl.pallas_call(
        _gscan_kernel,
        grid=(n_steps,),
        in_specs=[pl.BlockSpec(memory_space=pltpu.SMEM),
                  chain(0, fwd_i, dk), chain(1, bwd_i, dk), chain(0, fwd_i, dk), chain(1, bwd_i, dk),
                  chain(0, fwd_i, dk), chain(1, bwd_i, dk), kspec(0, fwd_i), kspec(1, bwd_i),
                  chain(0, fwd_i, LANES), chain(0, bwd_i, LANES)],
        out_specs=[pl.BlockSpec((b, rows, GDN_W), lambda t: (0, fwd_i(t), 0)),
                   pl.BlockSpec((b, rows, GDN_W), lambda t: (0, bwd_i(t), 0))],
        out_shape=[jax.ShapeDtypeStruct((b, s, GDN_W), F32),
                   jax.ShapeDtypeStruct((b, s, GDN_W), F32)],
        scratch_shapes=[pltpu.VMEM((b * N_CHAIN, GDN_HEAD_DIM, GDN_HEAD_DIM), F32)],
        compiler_params=pltpu.CompilerParams(dimension_semantics=("arbitrary",),
                                             vmem_limit_bytes=VMEM_LIMIT),
        name="gdn_scan",
    )(egl, w, w, u, u, qd, qd, kdt, kdt, qk, qk)


def _ffn_kernel(x_ref, attn_ref, of_ref, ob_ref, z_ref, gnw_ref, wo_ref, fnw_ref, w1_ref, w2_ref,
                onw_ref, o_ref):
    o = of_ref[...] + ob_ref[...]
    heads = []
    for h in range(GDN_HEADS):
        oh = o[:, h * GDN_HEAD_DIM:(h + 1) * GDN_HEAD_DIM]
        ms = jnp.mean(oh * oh, axis=-1, keepdims=True)
        heads.append(oh * lax.rsqrt(ms + EPS) * gnw_ref[...])
    z = z_ref[...]
    gdn = jnp.concatenate(heads, axis=-1) * (z * jax.nn.sigmoid(z))
    hres = (x_ref[...] + jnp.dot(attn_ref[...], wo_ref[:ATTN_Q, :], preferred_element_type=F32)
            + _dot(gdn, wo_ref[ATTN_Q:, :]))
    ms = jnp.mean(hres * hres, axis=-1, keepdims=True)
    hn = (hres * lax.rsqrt(ms + EPS) * fnw_ref[...]).astype(BF16)
    a = jnp.dot(hn, w1_ref[...], preferred_element_type=F32)
    a = jnp.square(jnp.maximum(a, 0.0)).astype(BF16)
    acc = hres + jnp.dot(a, w2_ref[...], preferred_element_type=F32)
    ms = jnp.mean(acc * acc, axis=-1, keepdims=True)
    o_ref[...] = acc * lax.rsqrt(ms + EPS) * onw_ref[...]


def _out_ffn(x2, attn, o_f, o_b, z, gnw, wo, fnw, w1, w2, onw):
    n = x2.shape[0]
    row = lambda w: pl.BlockSpec((FFN_TM, w), lambda i: (i, 0))
    full = lambda a: pl.BlockSpec(a.shape, lambda i: (0, 0))
    return pl.pallas_call(
        _ffn_kernel,
        grid=(n // FFN_TM,),
        in_specs=[row(D_MODEL), row(ATTN_Q), row(GDN_W), row(GDN_W), row(GDN_W),
                  full(gnw), full(wo), full(fnw), full(w1), full(w2), full(onw)],
        out_specs=row(D_MODEL),
        out_shape=jax.ShapeDtypeStruct((n, D_MODEL), F32),
        compiler_params=pltpu.CompilerParams(dimension_semantics=("arbitrary",),
                                             vmem_limit_bytes=VMEM_LIMIT),
        name="out_ffn",
    )(x2, attn, o_f, o_b, z, gnw, wo, fnw, w1, w2, onw)


def _layer(h, band, norm_mix_w, w_in, attn_sink, conv_w, gdn_a_log, gdn_dt_bias, gdn_norm_w,
           w_out, norm_ffn_w, w_ffn_in, w_ffn_out, out_norm_w):
    b, s, _ = h.shape
    n = b * s
    o_qkvg = ATTN_Q + 2 * ATTN_KV
    o_z = o_qkvg + 3 * GDN_W
    o_ab = o_z + GDN_W
    w_cat = jnp.concatenate(
        [w_in[:, o_qkvg:o_z], w_in[:, o_z:o_ab], w_in[:, :o_qkvg], w_in[:, o_ab:],
         jnp.zeros((D_MODEL, LANES - 2 * N_CHAIN), w_in.dtype)], axis=1).astype(BF16)
    x2 = h.reshape(n, D_MODEL)
    qkv_g, z_g, q_a, k_a, v_a, ab = _proj(x2, norm_mix_w.reshape(1, D_MODEL), w_cat)

    attn = _attention(q_a.reshape(b, s, ATTN_Q), k_a.reshape(b, s, ATTN_KV), v_a.reshape(b, s, ATTN_KV),
                      band, attn_sink)

    conv_pad = jnp.zeros((SUBLANES, 3 * GDN_W), F32).at[:CONV_K].set(conv_w)
    gate_par = jnp.zeros((SUBLANES, LANES), F32)
    gate_par = gate_par.at[0, :N_CHAIN].set(gdn_a_log.reshape(-1)).at[1, :N_CHAIN].set(gdn_dt_bias.reshape(-1))
    w_c, u_c, q_dec, k_dec_t, qk, egl = _gdn_prep(qkv_g.reshape(b, s, 3 * GDN_W), ab.reshape(b, s, LANES),
                                                  conv_pad, gate_par)
    o_f, o_b = _gdn_scan(egl[..., 0].reshape(-1), w_c, u_c, q_dec, k_dec_t, qk)

    out = _out_ffn(x2, attn.reshape(n, ATTN_Q), o_f.reshape(n, GDN_W), o_b.reshape(n, GDN_W), z_g,
                   gdn_norm_w.reshape(1, GDN_HEAD_DIM), w_out.astype(BF16), norm_ffn_w.reshape(1, D_MODEL),
                   w_ffn_in.astype(BF16), w_ffn_out.astype(BF16), out_norm_w.reshape(1, D_MODEL))
    return out.reshape(b, s, D_MODEL)


def kernel(x, norm_mix_w, w_in, rel_bias, attn_sink, conv_w, gdn_a_log, gdn_dt_bias, gdn_norm_w, w_out,
           norm_ffn_w, w_ffn_in, w_ffn_out, norm_final_w):
    depth = w_in.shape[0]
    assert depth == 1, "the fused output kernel applies the final norm after the single trunk layer"
    rel = (np.arange(3 * BLOCK)[None, :] - BLOCK) - np.arange(BLOCK)[:, None]
    bucket = _t5_buckets(jnp.asarray(rel, dtype=jnp.int32))
    band = _bias_band(rel_bias, bucket.T)
    return _layer(x, band, norm_mix_w[0], w_in[0], attn_sink[0], conv_w[0], gdn_a_log[0], gdn_dt_bias[0],
                  gdn_norm_w[0], w_out[0], norm_ffn_w[0], w_ffn_in[0], w_ffn_out[0], norm_final_w)
```

```python
import functools
import math

import jax
import jax.numpy as jnp
import numpy as np
from jax import lax
from jax.experimental import pallas as pl
from jax.experimental.pallas import tpu as pltpu

F32 = jnp.float32
BF16 = jnp.bfloat16

D_MODEL = 1024
ATTN_HEADS = 8
ATTN_KV_HEADS = 2
ATTN_HEAD_DIM = 64
ATTN_GROUP = ATTN_HEADS // ATTN_KV_HEADS
WINDOW = 128
BLOCK = 128
N_BUCKETS = 32
MAX_DISTANCE = 128
GDN_HEADS = 4
GDN_HEAD_DIM = 128
CONV_K = 5
CHUNK = 64
N_DIR = 2
N_CHAIN = N_DIR * GDN_HEADS
D_FF = 4 * D_MODEL
EPS = 1e-6
ATTN_Q = ATTN_HEADS * ATTN_HEAD_DIM
ATTN_KV = ATTN_KV_HEADS * ATTN_HEAD_DIM
GDN_W = GDN_HEADS * GDN_HEAD_DIM
LANES = 128
SUBLANES = 8
VMEM_LIMIT = 56 * 1024 * 1024

PROJ_TM = 1024
PROJ_SUB = 512
PROJ_CAST_COLS = 256
ATTN_QB = 4
PREP_T = 512
ROW_STRIDE = 4
SCAN_CHUNKS = 4
FFN_TM = 512
FFN_SUB = 256


def _dot(a, b):
    return jnp.dot(a.astype(BF16), b.astype(BF16), preferred_element_type=F32)


def _dot_nt(a, b):
    return lax.dot_general(a.astype(BF16), b.astype(BF16), (((1,), (1,)), ((), ())),
                           preferred_element_type=F32)


def _split3(x):
    hi = x.astype(BF16)
    r1 = x - hi.astype(F32)
    mid = r1.astype(BF16)
    lo = (r1 - mid.astype(F32)).astype(BF16)
    return hi, mid, lo


def _dot01_right(x, m01):
    hi, mid, lo = _split3(x)
    d = lambda p: jnp.dot(p, m01, preferred_element_type=F32)
    return d(hi) + d(mid) + d(lo)


def _proj_kernel(x_ref, nw_ref, w_ref, qa_ref, ka_ref, va_ref, qkvg_ref, z_ref, ab_ref, wb_ref):
    d_in = w_ref.shape[1]
    d_main = d_in // LANES * LANES

    @pl.when(pl.program_id(0) == 0)
    def _():
        for c0 in range(0, d_main, PROJ_CAST_COLS):
            c1 = min(c0 + PROJ_CAST_COLS, d_main)
            wb_ref[:, c0:c1] = w_ref[:, c0:c1].astype(BF16)
        gate = jnp.concatenate([w_ref[:, d_main:], jnp.zeros((D_MODEL, LANES - (d_in - d_main)), F32)], axis=1)
        wb_ref[:, d_main:] = gate.astype(BF16)

    outs = ((qa_ref, ATTN_Q, ATTN_HEAD_DIM ** -0.5), (ka_ref, ATTN_KV, None), (va_ref, ATTN_KV, None),
            (qkvg_ref, 3 * GDN_W, None), (z_ref, GDN_W, None), (ab_ref, LANES, None))
    for r0 in range(0, PROJ_TM, PROJ_SUB):
        rows = slice(r0, r0 + PROJ_SUB)
        x = x_ref[rows, :]
        ms = jnp.mean(x * x, axis=-1, keepdims=True)
        xn = (x * lax.rsqrt(ms + EPS) * nw_ref[...]).astype(BF16)
        c0 = 0
        for ref, width, scale in outs:
            y = jnp.dot(xn, wb_ref[:, c0:c0 + width], preferred_element_type=F32)
            if scale is not None:
                y = y * scale
            ref[rows, :] = y.astype(ref.dtype)
            c0 += width


def _proj(x2, norm_w, w_in):
    n = x2.shape[0]
    d_in = w_in.shape[1]
    d_pad = d_in // LANES * LANES + LANES
    assert d_pad == ATTN_Q + 2 * ATTN_KV + 4 * GDN_W + LANES and d_in - (d_pad - LANES) == 2 * N_CHAIN
    row = lambda w: pl.BlockSpec((PROJ_TM, w), lambda i: (i, 0))
    return pl.pallas_call(
        _proj_kernel,
        grid=(n // PROJ_TM,),
        in_specs=[row(D_MODEL),
                  pl.BlockSpec((1, D_MODEL), lambda i: (0, 0)),
                  pl.BlockSpec((D_MODEL, d_in), lambda i: (0, 0))],
        out_specs=[row(ATTN_Q), row(ATTN_KV), row(ATTN_KV), row(3 * GDN_W), row(GDN_W), row(LANES)],
        out_shape=[jax.ShapeDtypeStruct((n, ATTN_Q), BF16),
                   jax.ShapeDtypeStruct((n, ATTN_KV), BF16),
                   jax.ShapeDtypeStruct((n, ATTN_KV), BF16),
                   jax.ShapeDtypeStruct((n, 3 * GDN_W), F32),
                   jax.ShapeDtypeStruct((n, GDN_W), F32),
                   jax.ShapeDtypeStruct((n, LANES), F32)],
        scratch_shapes=[pltpu.VMEM((D_MODEL, d_pad), BF16)],
        compiler_params=pltpu.CompilerParams(dimension_semantics=("arbitrary",),
                                             vmem_limit_bytes=VMEM_LIMIT),
        name="proj",
    )(x2, norm_w, w_in)


def _bias_kernel(relb_ref, bucket_ref, o_ref):
    bucket = bucket_ref[...]
    key = lax.broadcasted_iota(jnp.int32, (3 * BLOCK, BLOCK), 0)
    qry = lax.broadcasted_iota(jnp.int32, (3 * BLOCK, BLOCK), 1)
    in_window = jnp.abs(key - BLOCK - qry) <= WINDOW
    for h in range(ATTN_HEADS):
        acc = jnp.zeros((3 * BLOCK, BLOCK), F32)
        for b in range(N_BUCKETS):
            acc = jnp.where(bucket == b, relb_ref[b, h], acc)
        o_ref[h // 2, :, (h % 2) * BLOCK:(h % 2 + 1) * BLOCK] = jnp.where(in_window, acc, -1e30)


def _bias_band(rel_bias, bucket_t):
    shape = (ATTN_HEADS // 2, 3 * BLOCK, 2 * BLOCK)
    return pl.pallas_call(
        _bias_kernel,
        in_specs=[pl.BlockSpec(memory_space=pltpu.SMEM),
                  pl.BlockSpec((3 * BLOCK, BLOCK), lambda: (0, 0))],
        out_specs=pl.BlockSpec(shape, lambda: (0, 0, 0)),
        out_shape=jax.ShapeDtypeStruct(shape, F32),
        name="bias_band",
    )(rel_bias, bucket_t)


def _t5_buckets(rel):
    nb = N_BUCKETS // 2
    max_exact = nb // 2
    base = jnp.where(rel > 0, nb, 0)
    n = jnp.abs(rel)
    log_ratio = jnp.log(jnp.maximum(n, 1).astype(jnp.float32) / max_exact) / math.log(MAX_DISTANCE / max_exact)
    large = jnp.minimum(max_exact + (log_ratio * (nb - max_exact)).astype(jnp.int32), nb - 1)
    return base + jnp.where(n < max_exact, n, large)


def _attn_kernel(sink_ref, q_ref, kp_ref, kc_ref, kn_ref, vp_ref, vc_ref, vn_ref, bias_ref, o_ref):
    n = pl.program_id(1)
    last = pl.num_programs(1) - 1
    kband = jnp.concatenate([kp_ref[0], kc_ref[0], kn_ref[0]], axis=0)
    vband = jnp.concatenate([vp_ref[0], vc_ref[0], vn_ref[0]], axis=0)
    vband_t = vband.astype(F32).T.astype(BF16)
    key = lax.broadcasted_iota(jnp.int32, (3 * BLOCK, 1), 0)
    first_head = lax.broadcasted_iota(jnp.int32, (1, 2 * BLOCK), 1) < BLOCK
    head = lambda t, i: t[:, i * ATTN_HEAD_DIM:(i + 1) * ATTN_HEAD_DIM]
    n_pairs = ATTN_HEADS // 2
    kv_of = lambda pr: (2 * pr) // ATTN_GROUP
    units = [(j, pr) for j in range(ATTN_QB) for pr in range(n_pairs)]
    band_rows = lambda j: slice(j * BLOCK, (j + 3) * BLOCK)

    scores = {}
    for j, pr in units:
        qj = q_ref[0, j * BLOCK:(j + 1) * BLOCK, :]
        q2 = jnp.concatenate([head(qj, 2 * pr), head(qj, 2 * pr + 1)], axis=0)
        scores[j, pr] = _dot_nt(head(kband, kv_of(pr))[band_rows(j)], q2)
    probs, dens = {}, {}
    for j, pr in units:
        s = scores[j, pr] + bias_ref[pr]
        if j == 0:
            s = jnp.where((key < BLOCK) & (n == 0), -1e30, s)
        if j == ATTN_QB - 1:
            s = jnp.where((key >= 2 * BLOCK) & (n == last), -1e30, s)
        sink = jnp.where(first_head, sink_ref[2 * pr], sink_ref[2 * pr + 1])
        m = jnp.maximum(jnp.max(s, axis=0, keepdims=True), sink)
        p = jnp.exp(s - m)
        dens[j, pr] = jnp.sum(p, axis=0, keepdims=True) + jnp.exp(sink - m)
        probs[j, pr] = p.astype(BF16)
    outs_t = {}
    for j, pr in units:
        kv = kv_of(pr)
        v_t = vband_t[kv * ATTN_HEAD_DIM:(kv + 1) * ATTN_HEAD_DIM, band_rows(j)]
        outs_t[j, pr] = jnp.dot(v_t, probs[j, pr], preferred_element_type=F32) / dens[j, pr]
    for j in range(ATTN_QB):
        o_t = jnp.concatenate([outs_t[j, pr][:, half * BLOCK:(half + 1) * BLOCK]
                               for pr in range(n_pairs) for half in range(2)], axis=0)
        o_ref[0, j * BLOCK:(j + 1) * BLOCK, :] = o_t.T.astype(o_ref.dtype)


def _attention(q_a, k_a, v_a, band, sink):
    b, s, _ = q_a.shape
    nb = s // BLOCK
    rows = ATTN_QB * BLOCK
    kv_spec = lambda r, f: pl.BlockSpec((1, r, ATTN_KV), f)
    prev = lambda bi, n: (bi, jnp.maximum(n * ATTN_QB - 1, 0), 0)
    cur = lambda bi, n: (bi, n, 0)
    nxt = lambda bi, n: (bi, jnp.minimum((n + 1) * ATTN_QB, nb - 1), 0)
    return pl.pallas_call(
        _attn_kernel,
        grid=(b, nb // ATTN_QB),
        in_specs=[pl.BlockSpec(memory_space=pltpu.SMEM),
                  pl.BlockSpec((1, rows, ATTN_Q), cur),
                  kv_spec(BLOCK, prev), kv_spec(rows, cur), kv_spec(BLOCK, nxt),
                  kv_spec(BLOCK, prev), kv_spec(rows, cur), kv_spec(BLOCK, nxt),
                  pl.BlockSpec(band.shape, lambda bi, n: (0, 0, 0))],
        out_specs=pl.BlockSpec((1, rows, ATTN_Q), cur),
        out_shape=jax.ShapeDtypeStruct((b, s, ATTN_Q), BF16),
        compiler_params=pltpu.CompilerParams(dimension_semantics=("arbitrary", "arbitrary"),
                                             vmem_limit_bytes=VMEM_LIMIT),
        name="attn",
    )(sink, q_a, k_a, k_a, k_a, v_a, v_a, v_a, band)


def _gprep_kernel(xp_ref, x_ref, xn_ref, ab_ref, cw_ref, gp_ref,
                  w_ref, u_ref, qd_ref, kdt_ref, qk_ref, egl_ref, xe_ref, yn_ref):
    i = pl.program_id(1)
    last = pl.num_programs(1) - 1
    t_len = PREP_T
    halo = CONV_K // 2

    n_slab = 3 * GDN_W // LANES
    for sb in range(n_slab):
        lanes = slice(sb * LANES, (sb + 1) * LANES)
        xe_ref[sb, 0:SUBLANES, :] = jnp.where(i > 0, xp_ref[0, :, lanes], 0.0)
        xe_ref[sb, SUBLANES:SUBLANES + t_len, :] = x_ref[0, :, lanes]
        xe_ref[sb, SUBLANES + t_len:, :] = jnp.where(i < last, xn_ref[0, :, lanes], 0.0)

    ab = ab_ref[0]
    sp_in = ab + gp_ref[1:2, :]
    softplus = jnp.maximum(sp_in, 0.0) + jnp.log1p(jnp.exp(-jnp.abs(sp_in)))
    g = -jnp.exp(gp_ref[0:1, :]) * softplus
    beta = jax.nn.sigmoid(ab)

    r_t = lax.broadcasted_iota(jnp.int32, (t_len, t_len), 0)
    c_t = lax.broadcasted_iota(jnp.int32, (t_len, t_len), 1)
    same = (r_t // CHUNK) == (c_t // CHUNK)
    lower = jnp.where(same & (r_t >= c_t), 1.0, 0.0).astype(BF16)
    upper = jnp.where(same & (r_t <= c_t), 1.0, 0.0).astype(BF16)
    g_t = g.T[:2 * SUBLANES]
    cs_row = (_dot01_right(g_t, upper), _dot01_right(g_t, lower))
    pad_rows = jnp.zeros((LANES - 2 * SUBLANES, t_len), F32)
    cs_col = tuple(jnp.concatenate([r, pad_rows], axis=0).T for r in cs_row)

    n_rows = t_len // ROW_STRIDE
    for sb in range(n_slab):
        lanes = slice(sb * LANES, (sb + 1) * LANES)
        for ph in range(ROW_STRIDE):
            acc = None
            for j in range(CONV_K):
                win = xe_ref[sb, pl.ds(SUBLANES - halo + j + ph, n_rows, stride=ROW_STRIDE), :]
                term = cw_ref[j:j + 1, lanes] * win
                acc = term if acc is None else acc + term
            yv = acc * jax.nn.sigmoid(acc)
            if sb < 2 * GDN_HEADS:
                yv = yv * lax.rsqrt(jnp.sum(yv * yv, axis=-1, keepdims=True) + EPS)
            if sb < GDN_HEADS:
                yv = yv * (GDN_HEAD_DIM ** -0.5)
            yn_ref[sb, pl.ds(ph, n_rows, stride=ROW_STRIDE), :] = yv

    qs = [yn_ref[h] for h in range(GDN_HEADS)]
    ks = [yn_ref[GDN_HEADS + h] for h in range(GDN_HEADS)]
    vs = [yn_ref[2 * GDN_HEADS + h] for h in range(GDN_HEADS)]
    kts = [kh.T for kh in ks]

    r_c = lax.broadcasted_iota(jnp.int32, (CHUNK, LANES), 0)
    lane = lax.broadcasted_iota(jnp.int32, (CHUNK, LANES), 1)
    is_fwd = lane < CHUNK
    c_c = lane % CHUNK
    eye = jnp.where(r_c == c_c, 1.0, 0.0).astype(F32)
    incl = (is_fwd & (r_c >= c_c)) | (~is_fwd & (r_c <= c_c))
    strict = (is_fwd & (r_c > c_c)) | (~is_fwd & (r_c < c_c))
    r_d = lax.broadcasted_iota(jnp.int32, (2 * CHUNK, LANES), 0)
    c_d = lax.broadcasted_iota(jnp.int32, (2 * CHUNK, LANES), 1)
    same_dir = (r_d // CHUNK) == (c_d // CHUNK)
    level_mask = lambda s_, r_, c_: ((r_ // (2 * s_)) == (c_ // (2 * s_))) & ((r_ // s_) != (c_ // s_))
    stack2 = lambda t: jnp.concatenate([t, t], axis=0)

    n_chunks = t_len // CHUNK
    rows = lambda c: slice(c * CHUNK, (c + 1) * CHUNK)
    pairs = [(c, h) for c in range(n_chunks) for h in range(GDN_HEADS)]
    qkk = {}
    for c, h in pairs:
        k16 = ks[h][rows(c)].astype(BF16)
        qk16 = jnp.concatenate([qs[h][rows(c)].astype(BF16), k16], axis=0)
        qkk[c, h] = _dot_nt(qk16, stack2(k16))

    bcast = lambda col: jnp.broadcast_to(col, (CHUNK, LANES))
    g_full, b_full, grow, glast, decay, a_mat, t_mat = {}, {}, {}, {}, {}, {}, {}
    for key in pairs:
        c, h = key
        for d in range(N_DIR):
            j = d * GDN_HEADS + h
            r_last = c * CHUNK + (CHUNK - 1 if d == 0 else 0)
            g_full[key, d] = bcast(cs_col[d][rows(c), j:j + 1])
            b_full[key, d] = bcast(beta[rows(c), SUBLANES + j:SUBLANES + j + 1])
            glast[key, d] = cs_col[d][r_last:r_last + 1, j:j + 1]
            grow[key, d] = cs_row[d][j:j + 1, rows(c)]
        gcol2 = jnp.where(is_fwd, g_full[key, 0], g_full[key, 1])
        bcol2 = jnp.where(is_fwd, b_full[key, 0], b_full[key, 1])
        grow2 = jnp.concatenate([grow[key, 0], grow[key, 1]], axis=1)
        decay[key] = jnp.exp(jnp.where(incl, gcol2 - grow2, -jnp.inf))
        a_mat[key] = jnp.where(strict, bcol2 * qkk[key][CHUNK:] * decay[key], 0.0)
        t_mat[key] = eye - jnp.where(level_mask(1, r_c, c_c), a_mat[key], 0.0)

    same_dir16 = jnp.where(same_dir, 1.0, 0.0).astype(BF16)
    block_diag = lambda t16: stack2(t16) * same_dir16
    a_bd = {key: block_diag(a_mat[key].astype(BF16)) for key in pairs}
    s = 2
    while s < CHUNK:
        lvl = level_mask(s, r_c, c_c)
        t16 = {key: t_mat[key].astype(BF16) for key in pairs}
        x_mat = {key: jnp.dot(t16[key], a_bd[key], preferred_element_type=F32) for key in pairs}
        y_mat = {key: jnp.dot(x_mat[key].astype(BF16), block_diag(t16[key]), preferred_element_type=F32)
                 for key in pairs}
        t_mat = {key: t_mat[key] - jnp.where(lvl, y_mat[key], 0.0) for key in pairs}
        s *= 2

    wu = {}
    for key in pairs:
        c, h = key
        kv = jnp.concatenate(
            [jnp.concatenate([ks[h][rows(c)] * (b_full[key, d] * jnp.exp(g_full[key, d])),
                              vs[h][rows(c)] * b_full[key, d]], axis=1) for d in range(N_DIR)], axis=0)
        t_sel = block_diag(t_mat[key].astype(BF16))
        wu[key] = jnp.dot(t_sel, kv.astype(BF16), preferred_element_type=F32)

    for c in range(n_chunks):
        egl_rows = []
        for d in range(N_DIR):
            for h in range(GDN_HEADS):
                key = (c, h)
                j = d * GDN_HEADS + h
                wu_d = wu[key][d * CHUNK:(d + 1) * CHUNK]
                w_ref[0, j, rows(c), :] = wu_d[:, :GDN_HEAD_DIM].astype(w_ref.dtype)
                u_ref[0, j, rows(c), :] = wu_d[:, GDN_HEAD_DIM:]
                qd_ref[0, j, rows(c), :] = (qs[h][rows(c)] * jnp.exp(g_full[key, d])).astype(qd_ref.dtype)
                kdt = kts[h][:, rows(c)] * jnp.exp(glast[key, d] - grow[key, d])
                kdt_ref[0, j, c] = kdt.astype(kdt_ref.dtype)
                egl_rows.append(jnp.broadcast_to(jnp.exp(glast[key, d]), (1, LANES)))
        egl_ref[0, c] = jnp.concatenate(egl_rows, axis=0)
        for h in range(GDN_HEADS):
            qk_ref[0, h, rows(c), :] = (qkk[c, h][:CHUNK] * decay[c, h]).astype(qk_ref.dtype)


def _gdn_prep(qkv_g, ab, conv_w, gate_par):
    b, s, _ = qkv_g.shape
    nt = s // PREP_T
    cpb = PREP_T // CHUNK
    nc = s // CHUNK
    nh8 = PREP_T // SUBLANES
    chain = lambda last: pl.BlockSpec((1, N_CHAIN, PREP_T, last), lambda bi, i: (bi, 0, i, 0))
    return pl.pallas_call(
        _gprep_kernel,
        grid=(b, nt),
        in_specs=[pl.BlockSpec((1, SUBLANES, 3 * GDN_W), lambda bi, i: (bi, jnp.maximum(i * nh8 - 1, 0), 0)),
                  pl.BlockSpec((1, PREP_T, 3 * GDN_W), lambda bi, i: (bi, i, 0)),
                  pl.BlockSpec((1, SUBLANES, 3 * GDN_W),
                               lambda bi, i: (bi, jnp.minimum((i + 1) * nh8, s // SUBLANES - 1), 0)),
                  pl.BlockSpec((1, PREP_T, LANES), lambda bi, i: (bi, i, 0)),
                  pl.BlockSpec((SUBLANES, 3 * GDN_W), lambda bi, i: (0, 0)),
                  pl.BlockSpec((SUBLANES, LANES), lambda bi, i: (0, 0))],
        out_specs=[chain(GDN_HEAD_DIM), chain(GDN_HEAD_DIM), chain(GDN_HEAD_DIM),
                   pl.BlockSpec((1, N_CHAIN, cpb, GDN_HEAD_DIM, CHUNK), lambda bi, i: (bi, 0, i, 0, 0)),
                   pl.BlockSpec((1, GDN_HEADS, PREP_T, LANES), lambda bi, i: (bi, 0, i, 0)),
                   pl.BlockSpec((1, cpb, N_CHAIN, LANES), lambda bi, i: (bi, i, 0, 0))],
        out_shape=[jax.ShapeDtypeStruct((b, N_CHAIN, s, GDN_HEAD_DIM), BF16),
                   jax.ShapeDtypeStruct((b, N_CHAIN, s, GDN_HEAD_DIM), F32),
                   jax.ShapeDtypeStruct((b, N_CHAIN, s, GDN_HEAD_DIM), BF16),
                   jax.ShapeDtypeStruct((b, N_CHAIN, nc, GDN_HEAD_DIM, CHUNK), BF16),
                   jax.ShapeDtypeStruct((b, GDN_HEADS, s, LANES), BF16),
                   jax.ShapeDtypeStruct((b, nc, N_CHAIN, LANES), F32)],
        scratch_shapes=[pltpu.VMEM((3 * GDN_W // LANES, PREP_T + 2 * SUBLANES, LANES), F32),
                        pltpu.VMEM((3 * GDN_W // LANES, PREP_T, LANES), F32)],
        compiler_params=pltpu.CompilerParams(dimension_semantics=("arbitrary", "arbitrary"),
                                             vmem_limit_bytes=VMEM_LIMIT),
        name="gdn_prep",
    )(qkv_g, qkv_g, qkv_g, ab, conv_w, gate_par)


def _gscan_kernel(egl_ref, wf_ref, wb_ref, uf_ref, ub_ref, qf_ref, qb_ref, kf_ref, kb_ref,
                  pf_ref, pb_ref, of_ref, ob_ref, state_ref):
    t = pl.program_id(0)
    n_steps = pl.num_programs(0)
    nc = n_steps * SCAN_CHUNKS
    n_batch = wf_ref.shape[0]

    @pl.when(t == 0)
    def _():
        state_ref[...] = jnp.zeros_like(state_ref)

    dirs = ((wf_ref, uf_ref, qf_ref, kf_ref, pf_ref, of_ref), (wb_ref, ub_ref, qb_ref, kb_ref, pb_ref, ob_ref))
    chains = [(bi, d, h) for bi in range(n_batch) for d in range(N_DIR) for h in range(GDN_HEADS)]
    slot = lambda bi, d, h: (bi * N_DIR + d) * GDN_HEADS + h
    st = {key: state_ref[slot(*key)] for key in chains}

    for sub in range(SCAN_CHUNKS):
        local = (sub, SCAN_CHUNKS - 1 - sub)
        rows = [slice(c * CHUNK, (c + 1) * CHUNK) for c in local]
        chunk = (t * SCAN_CHUNKS + sub, nc - 1 - (t * SCAN_CHUNKS + sub))
        r = {}
        for key in chains:
            bi, d, h = key
            w_ref, _, q_ref = dirs[d][:3]
            wq = jnp.concatenate([w_ref[bi, h, rows[d]], q_ref[bi, h, rows[d]]], axis=0)
            r[key] = _dot(wq, st[key])
        v_new, intra = {}, {}
        for key in chains:
            bi, d, h = key
            u_ref, p_ref = dirs[d][1], dirs[d][4]
            v_new[key] = (u_ref[bi, h, rows[d]] - r[key][:CHUNK]).astype(BF16)
            zeros = jnp.zeros_like(v_new[key])
            v_pad = jnp.concatenate([v_new[key], zeros] if d == 0 else [zeros, v_new[key]], axis=0)
            intra[key] = jnp.dot(p_ref[bi, h, rows[d]], v_pad, preferred_element_type=F32)
        for key in chains:
            bi, d, h = key
            k_ref = dirs[d][3]
            egl = egl_ref[(bi * nc + chunk[d]) * N_CHAIN + d * GDN_HEADS + h]
            st[key] = st[key] * egl + jnp.dot(k_ref[bi, h, local[d]], v_new[key], preferred_element_type=F32)
        for bi in range(n_batch):
            for d in range(N_DIR):
                o_ref = dirs[d][5]
                o_ref[bi, rows[d], :] = jnp.concatenate(
                    [r[bi, d, h][CHUNK:] + intra[bi, d, h] for h in range(GDN_HEADS)], axis=-1)

    for key in chains:
        state_ref[slot(*key)] = st[key]


def _gdn_scan(egl, w, u, qd, kdt, qk):
    b, _, s, _ = w.shape
    rows = SCAN_CHUNKS * CHUNK
    n_steps = s // rows
    fwd_i = lambda t: t
    bwd_i = lambda t: n_steps - 1 - t
    chain = lambda d, at, last: pl.BlockSpec((b, GDN_HEADS, rows, last), lambda t: (0, d, at(t), 0))
    kspec = lambda d, at: pl.BlockSpec((b, GDN_HEADS, SCAN_CHUNKS, GDN_HEAD_DIM, CHUNK),
                                       lambda t: (0, d, at(t), 0, 0))
    dk = GDN_HEAD_DIM
    return pl.pallas_call(
        _gscan_kernel,
        grid=(n_steps,),
        in_specs=[pl.BlockSpec(memory_space=pltpu.SMEM),
                  chain(0, fwd_i, dk), chain(1, bwd_i, dk), chain(0, fwd_i, dk), chain(1, bwd_i, dk),
                  chain(0, fwd_i, dk), chain(1, bwd_i, dk), kspec(0, fwd_i), kspec(1, bwd_i),
                  chain(0, fwd_i, LANES), chain(0, bwd_i, LANES)],
        out_specs=[pl.BlockSpec((b, rows, GDN_W), lambda t: (0, fwd_i(t), 0)),
                   pl.BlockSpec((b, rows, GDN_W), lambda t: (0, bwd_i(t), 0))],
        out_shape=[jax.ShapeDtypeStruct((b, s, GDN_W), F32),
                   jax.ShapeDtypeStruct((b, s, GDN_W), F32)],
        scratch_shapes=[pltpu.VMEM((b * N_CHAIN, GDN_HEAD_DIM, GDN_HEAD_DIM), F32)],
        compiler_params=pltpu.CompilerParams(dimension_semantics=("arbitrary",),
                                             vmem_limit_bytes=VMEM_LIMIT),
        name="gdn_scan",
    )(egl, w, w, u, u, qd, qd, kdt, kdt, qk, qk)


def _ffn_kernel(x_ref, attn_ref, of_ref, ob_ref, z_ref, gnw_ref, wo_ref, fnw_ref, w1_ref, w2_ref,
                onw_ref, o_ref):
    subs = [slice(r0, r0 + FFN_SUB) for r0 in range(0, FFN_TM, FFN_SUB)]
    rms = lambda t: t * lax.rsqrt(jnp.mean(t * t, axis=-1, keepdims=True) + EPS)
    hres, hn, act, acc = {}, {}, {}, {}
    for r in subs:
        o = of_ref[r, :] + ob_ref[r, :]
        heads = [rms(o[:, h * GDN_HEAD_DIM:(h + 1) * GDN_HEAD_DIM]) * gnw_ref[...] for h in range(GDN_HEADS)]
        z = z_ref[r, :]
        gdn = jnp.concatenate(heads, axis=-1) * (z * jax.nn.sigmoid(z))
        hres[r.start] = (x_ref[r, :] + jnp.dot(attn_ref[r, :], wo_ref[:ATTN_Q, :], preferred_element_type=F32)
                         + _dot(gdn, wo_ref[ATTN_Q:, :]))
    for r in subs:
        hn[r.start] = (rms(hres[r.start]) * fnw_ref[...]).astype(BF16)
        act[r.start] = jnp.dot(hn[r.start], w1_ref[...], preferred_element_type=F32)
    for r in subs:
        a = jnp.square(jnp.maximum(act[r.start], 0.0)).astype(BF16)
        acc[r.start] = hres[r.start] + jnp.dot(a, w2_ref[...], preferred_element_type=F32)
    for r in subs:
        o_ref[r, :] = rms(acc[r.start]) * onw_ref[...]


def _out_ffn(x2, attn, o_f, o_b, z, gnw, wo, fnw, w1, w2, onw):
    n = x2.shape[0]
    row = lambda w: pl.BlockSpec((FFN_TM, w), lambda i: (i, 0))
    full = lambda a: pl.BlockSpec(a.shape, lambda i: (0, 0))
    return pl.pallas_call(
        _ffn_kernel,
        grid=(n // FFN_TM,),
        in_specs=[row(D_MODEL), row(ATTN_Q), row(GDN_W), row(GDN_W), row(GDN_W),
                  full(gnw), full(wo), full(fnw), full(w1), full(w2), full(onw)],
        out_specs=row(D_MODEL),
        out_shape=jax.ShapeDtypeStruct((n, D_MODEL), F32),
        compiler_params=pltpu.CompilerParams(dimension_semantics=("arbitrary",),
                                             vmem_limit_bytes=VMEM_LIMIT),
        name="out_ffn",
    )(x2, attn, o_f, o_b, z, gnw, wo, fnw, w1, w2, onw)


def _layer(h, band, norm_mix_w, w_in, attn_sink, conv_w, gdn_a_log, gdn_dt_bias, gdn_norm_w,
           w_out, norm_ffn_w, w_ffn_in, w_ffn_out, out_norm_w):
    b, s, _ = h.shape
    n = b * s
    x2 = h.reshape(n, D_MODEL)
    q_a, k_a, v_a, qkv_g, z_g, ab = _proj(x2, norm_mix_w.reshape(1, D_MODEL), w_in)

    attn = _attention(q_a.reshape(b, s, ATTN_Q), k_a.reshape(b, s, ATTN_KV), v_a.reshape(b, s, ATTN_KV),
                      band, attn_sink)

    conv_pad = jnp.zeros((SUBLANES, 3 * GDN_W), F32).at[:CONV_K].set(conv_w)
    gate_par = jnp.zeros((SUBLANES, LANES), F32)
    gate_par = gate_par.at[0, :N_CHAIN].set(gdn_a_log.reshape(-1)).at[1, :N_CHAIN].set(gdn_dt_bias.reshape(-1))
    w_c, u_c, q_dec, k_dec_t, qk, egl = _gdn_prep(qkv_g.reshape(b, s, 3 * GDN_W), ab.reshape(b, s, LANES),
                                                  conv_pad, gate_par)
    o_f, o_b = _gdn_scan(egl[..., 0].reshape(-1), w_c, u_c, q_dec, k_dec_t, qk)

    out = _out_ffn(x2, attn.reshape(n, ATTN_Q), o_f.reshape(n, GDN_W), o_b.reshape(n, GDN_W), z_g,
                   gdn_norm_w.reshape(1, GDN_HEAD_DIM), w_out.astype(BF16), norm_ffn_w.reshape(1, D_MODEL),
                   w_ffn_in.astype(BF16), w_ffn_out.astype(BF16), out_norm_w.reshape(1, D_MODEL))
    return out.reshape(b, s, D_MODEL)


def kernel(x, norm_mix_w, w_in, rel_bias, attn_sink, conv_w, gdn_a_log, gdn_dt_bias, gdn_norm_w, w_out,
           norm_ffn_w, w_ffn_in, w_ffn_out, norm_final_w):
    depth = w_in.shape[0]
    assert depth == 1, "the fused output kernel applies the final norm after the single trunk layer"
    rel = (np.arange(3 * BLOCK)[None, :] - BLOCK) - np.arange(BLOCK)[:, None]
    bucket = _t5_buckets(jnp.asarray(rel, dtype=jnp.int32))
    band = _bias_band(rel_bias, bucket.T)
    return _layer(x, band, norm_mix_w[0], w_in[0], attn_sink[0], conv_w[0], gdn_a_log[0], gdn_dt_bias[0],
                  gdn_norm_w[0], w_out[0], norm_ffn_w[0], w_ffn_in[0], w_ffn_out[0], norm_final_w)
```

```python
import functools
import math

import jax
import jax.numpy as jnp
import numpy as np
from jax import lax
from jax.experimental import pallas as pl
from jax.experimental.pallas import tpu as pltpu

F32 = jnp.float32
BF16 = jnp.bfloat16

D_MODEL = 1024
ATTN_HEADS = 8
ATTN_KV_HEADS = 2
ATTN_HEAD_DIM = 64
ATTN_GROUP = ATTN_HEADS // ATTN_KV_HEADS
WINDOW = 128
BLOCK = 128
N_BUCKETS = 32
MAX_DISTANCE = 128
GDN_HEADS = 4
GDN_HEAD_DIM = 128
CONV_K = 5
CHUNK = 64
N_DIR = 2
N_CHAIN = N_DIR * GDN_HEADS
D_FF = 4 * D_MODEL
EPS = 1e-6
ATTN_Q = ATTN_HEADS * ATTN_HEAD_DIM
ATTN_KV = ATTN_KV_HEADS * ATTN_HEAD_DIM
GDN_W = GDN_HEADS * GDN_HEAD_DIM
LANES = 128
SUBLANES = 8
VMEM_LIMIT = 56 * 1024 * 1024

PROJ_TM = 1024
PROJ_SUB = 512
PROJ_CAST_COLS = 256
ATTN_QB = 4
PREP_T = 512
ROW_STRIDE = 4
SCAN_CHUNKS = 4
FFN_TM = 512
FFN_SUB = 256


def _dot(a, b):
    return jnp.dot(a.astype(BF16), b.astype(BF16), preferred_element_type=F32)


def _dot_nt(a, b):
    return lax.dot_general(a.astype(BF16), b.astype(BF16), (((1,), (1,)), ((), ())),
                           preferred_element_type=F32)


def _split3(x):
    hi = x.astype(BF16)
    r1 = x - hi.astype(F32)
    mid = r1.astype(BF16)
    lo = (r1 - mid.astype(F32)).astype(BF16)
    return hi, mid, lo


def _dot01_right(x, m01):
    hi, mid, lo = _split3(x)
    d = lambda p: jnp.dot(p, m01, preferred_element_type=F32)
    return d(hi) + d(mid) + d(lo)


def _proj_kernel(x_ref, nw_ref, w_ref, qa_ref, ka_ref, va_ref, qkvg_ref, z_ref, ab_ref, wb_ref):
    d_in = w_ref.shape[1]
    d_main = d_in // LANES * LANES

    @pl.when(pl.program_id(0) == 0)
    def _():
        for c0 in range(0, d_main, PROJ_CAST_COLS):
            c1 = min(c0 + PROJ_CAST_COLS, d_main)
            wb_ref[:, c0:c1] = w_ref[:, c0:c1].astype(BF16)
        gate = jnp.concatenate([w_ref[:, d_main:], jnp.zeros((D_MODEL, LANES - (d_in - d_main)), F32)], axis=1)
        wb_ref[:, d_main:] = gate.astype(BF16)

    outs = ((qa_ref, ATTN_Q, ATTN_HEAD_DIM ** -0.5), (ka_ref, ATTN_KV, None), (va_ref, ATTN_KV, None),
            (qkvg_ref, 3 * GDN_W, None), (z_ref, GDN_W, None), (ab_ref, LANES, None))
    for r0 in range(0, PROJ_TM, PROJ_SUB):
        rows = slice(r0, r0 + PROJ_SUB)
        x = x_ref[rows, :]
        ms = jnp.mean(x * x, axis=-1, keepdims=True)
        xn = (x * lax.rsqrt(ms + EPS) * nw_ref[...]).astype(BF16)
        c0 = 0
        for ref, width, scale in outs:
            y = jnp.dot(xn, wb_ref[:, c0:c0 + width], preferred_element_type=F32)
            if scale is not None:
                y = y * scale
            ref[rows, :] = y.astype(ref.dtype)
            c0 += width


def _proj(x2, norm_w, w_in):
    n = x2.shape[0]
    d_in = w_in.shape[1]
    d_pad = d_in // LANES * LANES + LANES
    assert d_pad == ATTN_Q + 2 * ATTN_KV + 4 * GDN_W + LANES and d_in - (d_pad - LANES) == 2 * N_CHAIN
    row = lambda w: pl.BlockSpec((PROJ_TM, w), lambda i: (i, 0))
    return pl.pallas_call(
        _proj_kernel,
        grid=(n // PROJ_TM,),
        in_specs=[row(D_MODEL),
                  pl.BlockSpec((1, D_MODEL), lambda i: (0, 0)),
                  pl.BlockSpec((D_MODEL, d_in), lambda i: (0, 0))],
        out_specs=[row(ATTN_Q), row(ATTN_KV), row(ATTN_KV), row(3 * GDN_W), row(GDN_W), row(LANES)],
        out_shape=[jax.ShapeDtypeStruct((n, ATTN_Q), BF16),
                   jax.ShapeDtypeStruct((n, ATTN_KV), BF16),
                   jax.ShapeDtypeStruct((n, ATTN_KV), BF16),
                   jax.ShapeDtypeStruct((n, 3 * GDN_W), F32),
                   jax.ShapeDtypeStruct((n, GDN_W), F32),
                   jax.ShapeDtypeStruct((n, LANES), F32)],
        scratch_shapes=[pltpu.VMEM((D_MODEL, d_pad), BF16)],
        compiler_params=pltpu.CompilerParams(dimension_semantics=("arbitrary",),
                                             vmem_limit_bytes=VMEM_LIMIT),
        name="proj",
    )(x2, norm_w, w_in)


def _bias_kernel(relb_ref, bucket_ref, o_ref):
    bucket = bucket_ref[...]
    key = lax.broadcasted_iota(jnp.int32, (3 * BLOCK, BLOCK), 0)
    qry = lax.broadcasted_iota(jnp.int32, (3 * BLOCK, BLOCK), 1)
    in_window = jnp.abs(key - BLOCK - qry) <= WINDOW
    for h in range(ATTN_HEADS):
        acc = jnp.zeros((3 * BLOCK, BLOCK), F32)
        for b in range(N_BUCKETS):
            acc = jnp.where(bucket == b, relb_ref[b, h], acc)
        o_ref[h // 2, :, (h % 2) * BLOCK:(h % 2 + 1) * BLOCK] = jnp.where(in_window, acc, -1e30)


def _bias_band(rel_bias, bucket_t):
    shape = (ATTN_HEADS // 2, 3 * BLOCK, 2 * BLOCK)
    return pl.pallas_call(
        _bias_kernel,
        in_specs=[pl.BlockSpec(memory_space=pltpu.SMEM),
                  pl.BlockSpec((3 * BLOCK, BLOCK), lambda: (0, 0))],
        out_specs=pl.BlockSpec(shape, lambda: (0, 0, 0)),
        out_shape=jax.ShapeDtypeStruct(shape, F32),
        name="bias_band",
    )(rel_bias, bucket_t)


def _t5_buckets(rel):
    nb = N_BUCKETS // 2
    max_exact = nb // 2
    base = jnp.where(rel > 0, nb, 0)
    n = jnp.abs(rel)
    log_ratio = jnp.log(jnp.maximum(n, 1).astype(jnp.float32) / max_exact) / math.log(MAX_DISTANCE / max_exact)
    large = jnp.minimum(max_exact + (log_ratio * (nb - max_exact)).astype(jnp.int32), nb - 1)
    return base + jnp.where(n < max_exact, n, large)


def _attn_kernel(sink_ref, q_ref, kp_ref, kc_ref, kn_ref, vp_ref, vc_ref, vn_ref, bias_ref, o_ref):
    n = pl.program_id(1)
    last = pl.num_programs(1) - 1
    kband = jnp.concatenate([kp_ref[0], kc_ref[0], kn_ref[0]], axis=0)
    vband = jnp.concatenate([vp_ref[0], vc_ref[0], vn_ref[0]], axis=0)
    vband_t = vband.astype(F32).T.astype(BF16)
    key = lax.broadcasted_iota(jnp.int32, (3 * BLOCK, 1), 0)
    first_head = lax.broadcasted_iota(jnp.int32, (1, 2 * BLOCK), 1) < BLOCK
    head = lambda t, i: t[:, i * ATTN_HEAD_DIM:(i + 1) * ATTN_HEAD_DIM]
    n_pairs = ATTN_HEADS // 2
    kv_of = lambda pr: (2 * pr) // ATTN_GROUP
    units = [(j, pr) for j in range(ATTN_QB) for pr in range(n_pairs)]
    band_rows = lambda j: slice(j * BLOCK, (j + 3) * BLOCK)

    scores = {}
    for j, pr in units:
        qj = q_ref[0, j * BLOCK:(j + 1) * BLOCK, :]
        q2 = jnp.concatenate([head(qj, 2 * pr), head(qj, 2 * pr + 1)], axis=0)
        scores[j, pr] = _dot_nt(head(kband, kv_of(pr))[band_rows(j)], q2)
    probs, dens = {}, {}
    for j, pr in units:
        s = scores[j, pr] + bias_ref[pr]
        if j == 0:
            s = jnp.where((key < BLOCK) & (n == 0), -1e30, s)
        if j == ATTN_QB - 1:
            s = jnp.where((key >= 2 * BLOCK) & (n == last), -1e30, s)
        sink = jnp.where(first_head, sink_ref[2 * pr], sink_ref[2 * pr + 1])
        m = jnp.maximum(jnp.max(s, axis=0, keepdims=True), sink)
        p = jnp.exp(s - m)
        dens[j, pr] = jnp.sum(p, axis=0, keepdims=True) + jnp.exp(sink - m)
        probs[j, pr] = p.astype(BF16)
    outs_t = {}
    for j, pr in units:
        kv = kv_of(pr)
        v_t = vband_t[kv * ATTN_HEAD_DIM:(kv + 1) * ATTN_HEAD_DIM, band_rows(j)]
        outs_t[j, pr] = jnp.dot(v_t, probs[j, pr], preferred_element_type=F32) / dens[j, pr]
    for j in range(ATTN_QB):
        o_t = jnp.concatenate([outs_t[j, pr][:, half * BLOCK:(half + 1) * BLOCK]
                               for pr in range(n_pairs) for half in range(2)], axis=0)
        o_ref[0, j * BLOCK:(j + 1) * BLOCK, :] = o_t.T.astype(o_ref.dtype)


def _attention(q_a, k_a, v_a, band, sink):
    b, s, _ = q_a.shape
    nb = s // BLOCK
    rows = ATTN_QB * BLOCK
    kv_spec = lambda r, f: pl.BlockSpec((1, r, ATTN_KV), f)
    prev = lambda bi, n: (bi, jnp.maximum(n * ATTN_QB - 1, 0), 0)
    cur = lambda bi, n: (bi, n, 0)
    nxt = lambda bi, n: (bi, jnp.minimum((n + 1) * ATTN_QB, nb - 1), 0)
    return pl.pallas_call(
        _attn_kernel,
        grid=(b, nb // ATTN_QB),
        in_specs=[pl.BlockSpec(memory_space=pltpu.SMEM),
                  pl.BlockSpec((1, rows, ATTN_Q), cur),
                  kv_spec(BLOCK, prev), kv_spec(rows, cur), kv_spec(BLOCK, nxt),
                  kv_spec(BLOCK, prev), kv_spec(rows, cur), kv_spec(BLOCK, nxt),
                  pl.BlockSpec(band.shape, lambda bi, n: (0, 0, 0))],
        out_specs=pl.BlockSpec((1, rows, ATTN_Q), cur),
        out_shape=jax.ShapeDtypeStruct((b, s, ATTN_Q), BF16),
        compiler_params=pltpu.CompilerParams(dimension_semantics=("arbitrary", "arbitrary"),
                                             vmem_limit_bytes=VMEM_LIMIT),
        name="attn",
    )(sink, q_a, k_a, k_a, k_a, v_a, v_a, v_a, band)


def _gprep_kernel(xp_ref, x_ref, xn_ref, ab_ref, cw_ref, gp_ref,
                  w_ref, u_ref, qd_ref, kdt_ref, qk_ref, egl_ref, xe_ref, yn_ref):
    i = pl.program_id(1)
    last = pl.num_programs(1) - 1
    t_len = PREP_T
    halo = CONV_K // 2

    n_slab = 3 * GDN_W // LANES
    for sb in range(n_slab):
        lanes = slice(sb * LANES, (sb + 1) * LANES)
        xe_ref[sb, 0:SUBLANES, :] = jnp.where(i > 0, xp_ref[0, :, lanes], 0.0)
        xe_ref[sb, SUBLANES:SUBLANES + t_len, :] = x_ref[0, :, lanes]
        xe_ref[sb, SUBLANES + t_len:, :] = jnp.where(i < last, xn_ref[0, :, lanes], 0.0)

    ab = ab_ref[0]
    sp_in = ab + gp_ref[1:2, :]
    softplus = jnp.maximum(sp_in, 0.0) + jnp.log1p(jnp.exp(-jnp.abs(sp_in)))
    g = -jnp.exp(gp_ref[0:1, :]) * softplus
    beta = jax.nn.sigmoid(ab)

    r_t = lax.broadcasted_iota(jnp.int32, (t_len, t_len), 0)
    c_t = lax.broadcasted_iota(jnp.int32, (t_len, t_len), 1)
    same = (r_t // CHUNK) == (c_t // CHUNK)
    lower = jnp.where(same & (r_t >= c_t), 1.0, 0.0).astype(BF16)
    upper = jnp.where(same & (r_t <= c_t), 1.0, 0.0).astype(BF16)
    g_t = g.T[:2 * SUBLANES]
    cs_row = (_dot01_right(g_t, upper), _dot01_right(g_t, lower))
    pad_rows = jnp.zeros((LANES - 2 * SUBLANES, t_len), F32)
    cs_col = tuple(jnp.concatenate([r, pad_rows], axis=0).T for r in cs_row)

    n_rows = t_len // ROW_STRIDE
    units = [(sb, ph) for sb in range(n_slab) for ph in range(ROW_STRIDE)]
    yv = {}
    for sb, ph in units:
        lanes = slice(sb * LANES, (sb + 1) * LANES)
        acc = None
        for j in range(CONV_K):
            win = xe_ref[sb, pl.ds(SUBLANES - halo + j + ph, n_rows, stride=ROW_STRIDE), :]
            term = cw_ref[j:j + 1, lanes] * win
            acc = term if acc is None else acc + term
        yv[sb, ph] = acc
    for key in units:
        yv[key] = yv[key] * jax.nn.sigmoid(yv[key])
    for sb, ph in units:
        if sb < 2 * GDN_HEADS:
            scale = lax.rsqrt(jnp.sum(yv[sb, ph] * yv[sb, ph], axis=-1, keepdims=True) + EPS)
            if sb < GDN_HEADS:
                scale = scale * (GDN_HEAD_DIM ** -0.5)
            yv[sb, ph] = yv[sb, ph] * scale
    for sb, ph in units:
        yn_ref[sb, pl.ds(ph, n_rows, stride=ROW_STRIDE), :] = yv[sb, ph]

    qs = [yn_ref[h] for h in range(GDN_HEADS)]
    ks = [yn_ref[GDN_HEADS + h] for h in range(GDN_HEADS)]
    vs = [yn_ref[2 * GDN_HEADS + h] for h in range(GDN_HEADS)]
    kts = [kh.T for kh in ks]

    r_c = lax.broadcasted_iota(jnp.int32, (CHUNK, LANES), 0)
    lane = lax.broadcasted_iota(jnp.int32, (CHUNK, LANES), 1)
    is_fwd = lane < CHUNK
    c_c = lane % CHUNK
    eye = jnp.where(r_c == c_c, 1.0, 0.0).astype(F32)
    incl = (is_fwd & (r_c >= c_c)) | (~is_fwd & (r_c <= c_c))
    strict = (is_fwd & (r_c > c_c)) | (~is_fwd & (r_c < c_c))
    r_d = lax.broadcasted_iota(jnp.int32, (2 * CHUNK, LANES), 0)
    c_d = lax.broadcasted_iota(jnp.int32, (2 * CHUNK, LANES), 1)
    same_dir = (r_d // CHUNK) == (c_d // CHUNK)
    level_mask = lambda s_, r_, c_: ((r_ // (2 * s_)) == (c_ // (2 * s_))) & ((r_ // s_) != (c_ // s_))
    stack2 = lambda t: jnp.concatenate([t, t], axis=0)

    n_chunks = t_len // CHUNK
    rows = lambda c: slice(c * CHUNK, (c + 1) * CHUNK)
    pairs = [(c, h) for c in range(n_chunks) for h in range(GDN_HEADS)]
    qkk = {}
    for c, h in pairs:
        k16 = ks[h][rows(c)].astype(BF16)
        qk16 = jnp.concatenate([qs[h][rows(c)].astype(BF16), k16], axis=0)
        qkk[c, h] = _dot_nt(qk16, stack2(k16))

    bcast = lambda col: jnp.broadcast_to(col, (CHUNK, LANES))
    g_full, b_full, grow, glast, decay, a_mat, t_mat = {}, {}, {}, {}, {}, {}, {}
    for key in pairs:
        c, h = key
        for d in range(N_DIR):
            j = d * GDN_HEADS + h
            r_last = c * CHUNK + (CHUNK - 1 if d == 0 else 0)
            g_full[key, d] = bcast(cs_col[d][rows(c), j:j + 1])
            b_full[key, d] = bcast(beta[rows(c), SUBLANES + j:SUBLANES + j + 1])
            glast[key, d] = cs_col[d][r_last:r_last + 1, j:j + 1]
            grow[key, d] = cs_row[d][j:j + 1, rows(c)]
        gcol2 = jnp.where(is_fwd, g_full[key, 0], g_full[key, 1])
        bcol2 = jnp.where(is_fwd, b_full[key, 0], b_full[key, 1])
        grow2 = jnp.concatenate([grow[key, 0], grow[key, 1]], axis=1)
        decay[key] = jnp.exp(jnp.where(incl, gcol2 - grow2, -jnp.inf))
        a_mat[key] = jnp.where(strict, bcol2 * qkk[key][CHUNK:] * decay[key], 0.0)
        t_mat[key] = eye - jnp.where(level_mask(1, r_c, c_c), a_mat[key], 0.0)

    same_dir16 = jnp.where(same_dir, 1.0, 0.0).astype(BF16)
    block_diag = lambda t16: stack2(t16) * same_dir16
    a_bd = {key: block_diag(a_mat[key].astype(BF16)) for key in pairs}
    s = 2
    while s < CHUNK:
        lvl = level_mask(s, r_c, c_c)
        t16 = {key: t_mat[key].astype(BF16) for key in pairs}
        x_mat = {key: jnp.dot(t16[key], a_bd[key], preferred_element_type=F32) for key in pairs}
        y_mat = {key: jnp.dot(x_mat[key].astype(BF16), block_diag(t16[key]), preferred_element_type=F32)
                 for key in pairs}
        t_mat = {key: t_mat[key] - jnp.where(lvl, y_mat[key], 0.0) for key in pairs}
        s *= 2

    wu = {}
    for key in pairs:
        c, h = key
        kv = jnp.concatenate(
            [jnp.concatenate([ks[h][rows(c)] * (b_full[key, d] * jnp.exp(g_full[key, d])),
                              vs[h][rows(c)] * b_full[key, d]], axis=1) for d in range(N_DIR)], axis=0)
        t_sel = block_diag(t_mat[key].astype(BF16))
        wu[key] = jnp.dot(t_sel, kv.astype(BF16), preferred_element_type=F32)

    for c in range(n_chunks):
        egl_rows = []
        for d in range(N_DIR):
            for h in range(GDN_HEADS):
                key = (c, h)
                j = d * GDN_HEADS + h
                wu_d = wu[key][d * CHUNK:(d + 1) * CHUNK]
                w_ref[0, j, rows(c), :] = wu_d[:, :GDN_HEAD_DIM].astype(w_ref.dtype)
                u_ref[0, j, rows(c), :] = wu_d[:, GDN_HEAD_DIM:].astype(u_ref.dtype)
                qd_ref[0, j, rows(c), :] = (qs[h][rows(c)] * jnp.exp(g_full[key, d])).astype(qd_ref.dtype)
                egl_rows.append(jnp.broadcast_to(jnp.exp(glast[key, d]), (1, LANES)))
        egl_ref[0, c] = jnp.concatenate(egl_rows, axis=0)
        for h in range(GDN_HEADS):
            qk_ref[0, h, rows(c), :] = (qkk[c, h][:CHUNK] * decay[c, h]).astype(qk_ref.dtype)
    for cp in range(n_chunks // 2):
        for d in range(N_DIR):
            for h in range(GDN_HEADS):
                fac = jnp.concatenate([jnp.exp(glast[(c, h), d] - grow[(c, h), d]) for c in (2 * cp, 2 * cp + 1)],
                                      axis=1)
                kdt = kts[h][:, 2 * cp * CHUNK:(2 * cp + 2) * CHUNK] * fac
                kdt_ref[0, d * GDN_HEADS + h, cp] = kdt.astype(kdt_ref.dtype)


def _gdn_prep(qkv_g, ab, conv_w, gate_par):
    b, s, _ = qkv_g.shape
    nt = s // PREP_T
    cpb = PREP_T // CHUNK
    nc = s // CHUNK
    nh8 = PREP_T // SUBLANES
    chain = lambda last: pl.BlockSpec((1, N_CHAIN, PREP_T, last), lambda bi, i: (bi, 0, i, 0))
    return pl.pallas_call(
        _gprep_kernel,
        grid=(b, nt),
        in_specs=[pl.BlockSpec((1, SUBLANES, 3 * GDN_W), lambda bi, i: (bi, jnp.maximum(i * nh8 - 1, 0), 0)),
                  pl.BlockSpec((1, PREP_T, 3 * GDN_W), lambda bi, i: (bi, i, 0)),
                  pl.BlockSpec((1, SUBLANES, 3 * GDN_W),
                               lambda bi, i: (bi, jnp.minimum((i + 1) * nh8, s // SUBLANES - 1), 0)),
                  pl.BlockSpec((1, PREP_T, LANES), lambda bi, i: (bi, i, 0)),
                  pl.BlockSpec((SUBLANES, 3 * GDN_W), lambda bi, i: (0, 0)),
                  pl.BlockSpec((SUBLANES, LANES), lambda bi, i: (0, 0))],
        out_specs=[chain(GDN_HEAD_DIM), chain(GDN_HEAD_DIM), chain(GDN_HEAD_DIM),
                   pl.BlockSpec((1, N_CHAIN, cpb // 2, GDN_HEAD_DIM, 2 * CHUNK), lambda bi, i: (bi, 0, i, 0, 0)),
                   pl.BlockSpec((1, GDN_HEADS, PREP_T, LANES), lambda bi, i: (bi, 0, i, 0)),
                   pl.BlockSpec((1, cpb, N_CHAIN, LANES), lambda bi, i: (bi, i, 0, 0))],
        out_shape=[jax.ShapeDtypeStruct((b, N_CHAIN, s, GDN_HEAD_DIM), BF16),
                   jax.ShapeDtypeStruct((b, N_CHAIN, s, GDN_HEAD_DIM), BF16),
                   jax.ShapeDtypeStruct((b, N_CHAIN, s, GDN_HEAD_DIM), BF16),
                   jax.ShapeDtypeStruct((b, N_CHAIN, nc // 2, GDN_HEAD_DIM, 2 * CHUNK), BF16),
                   jax.ShapeDtypeStruct((b, GDN_HEADS, s, LANES), BF16),
                   jax.ShapeDtypeStruct((b, nc, N_CHAIN, LANES), F32)],
        scratch_shapes=[pltpu.VMEM((3 * GDN_W // LANES, PREP_T + 2 * SUBLANES, LANES), F32),
                        pltpu.VMEM((3 * GDN_W // LANES, PREP_T, LANES), F32)],
        compiler_params=pltpu.CompilerParams(dimension_semantics=("arbitrary", "arbitrary"),
                                             vmem_limit_bytes=VMEM_LIMIT),
        name="gdn_prep",
    )(qkv_g, qkv_g, qkv_g, ab, conv_w, gate_par)


def _gscan_kernel(egl_ref, wf_ref, wb_ref, uf_ref, ub_ref, qf_ref, qb_ref, kf_ref, kb_ref,
                  pf_ref, pb_ref, of_ref, ob_ref, state_ref):
    t = pl.program_id(0)
    n_steps = pl.num_programs(0)
    nc = n_steps * SCAN_CHUNKS
    n_batch = wf_ref.shape[0]

    @pl.when(t == 0)
    def _():
        state_ref[...] = jnp.zeros_like(state_ref)

    dirs = ((wf_ref, uf_ref, qf_ref, kf_ref, pf_ref, of_ref), (wb_ref, ub_ref, qb_ref, kb_ref, pb_ref, ob_ref))
    chains = [(bi, d, h) for bi in range(n_batch) for d in range(N_DIR) for h in range(GDN_HEADS)]
    slot = lambda bi, d, h: (bi * N_DIR + d) * GDN_HEADS + h
    st = {key: state_ref[slot(*key)] for key in chains}

    for sub in range(SCAN_CHUNKS):
        local = (sub, SCAN_CHUNKS - 1 - sub)
        rows = [slice(c * CHUNK, (c + 1) * CHUNK) for c in local]
        chunk = (t * SCAN_CHUNKS + sub, nc - 1 - (t * SCAN_CHUNKS + sub))
        r = {}
        for key in chains:
            bi, d, h = key
            w_ref, _, q_ref = dirs[d][:3]
            wq = jnp.concatenate([w_ref[bi, h, rows[d]], q_ref[bi, h, rows[d]]], axis=0)
            r[key] = _dot(wq, st[key])
        v_pad, intra = {}, {}
        for key in chains:
            bi, d, h = key
            u_ref, p_ref = dirs[d][1], dirs[d][4]
            v_new = (u_ref[bi, h, rows[d]].astype(F32) - r[key][:CHUNK]).astype(BF16)
            zeros = jnp.zeros_like(v_new)
            v_pad[key] = (jnp.concatenate([v_new, zeros], axis=0), jnp.concatenate([zeros, v_new], axis=0))
            intra[key] = jnp.dot(p_ref[bi, h, rows[d]], v_pad[key][d], preferred_element_type=F32)
        for key in chains:
            bi, d, h = key
            k_ref = dirs[d][3]
            egl = egl_ref[(bi * nc + chunk[d]) * N_CHAIN + d * GDN_HEADS + h]
            st[key] = st[key] * egl + jnp.dot(k_ref[bi, h, local[d] // 2], v_pad[key][local[d] % 2],
                                              preferred_element_type=F32)
        for bi in range(n_batch):
            for d in range(N_DIR):
                o_ref = dirs[d][5]
                o_ref[bi, rows[d], :] = jnp.concatenate(
                    [r[bi, d, h][CHUNK:] + intra[bi, d, h] for h in range(GDN_HEADS)], axis=-1).astype(o_ref.dtype)

    for key in chains:
        state_ref[slot(*key)] = st[key]


def _gdn_scan(egl, w, u, qd, kdt, qk):
    b, _, s, _ = w.shape
    rows = SCAN_CHUNKS * CHUNK
    n_steps = s // rows
    fwd_i = lambda t: t
    bwd_i = lambda t: n_steps - 1 - t
    chain = lambda d, at, last: pl.BlockSpec((b, GDN_HEADS, rows, last), lambda t: (0, d, at(t), 0))
    kspec = lambda d, at: pl.BlockSpec((b, GDN_HEADS, SCAN_CHUNKS // 2, GDN_HEAD_DIM, 2 * CHUNK),
                                       lambda t: (0, d, at(t), 0, 0))
    dk = GDN_HEAD_DIM
    return pl.pallas_call(
        _gscan_kernel,
        grid=(n_steps,),
        in_specs=[pl.BlockSpec(memory_space=pltpu.SMEM),
                  chain(0, fwd_i, dk), chain(1, bwd_i, dk), chain(0, fwd_i, dk), chain(1, bwd_i, dk),
                  chain(0, fwd_i, dk), chain(1, bwd_i, dk), kspec(0, fwd_i), kspec(1, bwd_i),
                  chain(0, fwd_i, LANES), chain(0, bwd_i, LANES)],
        out_specs=[pl.BlockSpec((b, rows, GDN_W), lambda t: (0, fwd_i(t), 0)),
                   pl.BlockSpec((b, rows, GDN_W), lambda t: (0, bwd_i(t), 0))],
        out_shape=[jax.ShapeDtypeStruct((b, s, GDN_W), BF16),
                   jax.ShapeDtypeStruct((b, s, GDN_W), BF16)],
        scratch_shapes=[pltpu.VMEM((b * N_CHAIN, GDN_HEAD_DIM, GDN_HEAD_DIM), F32)],
        compiler_params=pltpu.CompilerParams(dimension_semantics=("arbitrary",),
                                             vmem_limit_bytes=VMEM_LIMIT),
        name="gdn_scan",
    )(egl, w, w, u, u, qd, qd, kdt, kdt, qk, qk)


def _ffn_kernel(x_ref, attn_ref, of_ref, ob_ref, z_ref, gnw_ref, wo_ref, fnw_ref, w1_ref, w2_ref,
                onw_ref, o_ref):
    subs = [slice(r0, r0 + FFN_SUB) for r0 in range(0, FFN_TM, FFN_SUB)]
    rms = lambda t: t * lax.rsqrt(jnp.mean(t * t, axis=-1, keepdims=True) + EPS)
    hres, hn, act, acc = {}, {}, {}, {}
    for r in subs:
        o = of_ref[r, :].astype(F32) + ob_ref[r, :].astype(F32)
        heads = [rms(o[:, h * GDN_HEAD_DIM:(h + 1) * GDN_HEAD_DIM]) * gnw_ref[...] for h in range(GDN_HEADS)]
        z = z_ref[r, :]
        gdn = jnp.concatenate(heads, axis=-1) * (z * jax.nn.sigmoid(z))
        hres[r.start] = (x_ref[r, :] + jnp.dot(attn_ref[r, :], wo_ref[:ATTN_Q, :], preferred_element_type=F32)
                         + _dot(gdn, wo_ref[ATTN_Q:, :]))
    for r in subs:
        hn[r.start] = (rms(hres[r.start]) * fnw_ref[...]).astype(BF16)
        act[r.start] = jnp.dot(hn[r.start], w1_ref[...], preferred_element_type=F32)
    for r in subs:
        a = jnp.square(jnp.maximum(act[r.start], 0.0)).astype(BF16)
        acc[r.start] = hres[r.start] + jnp.dot(a, w2_ref[...], preferred_element_type=F32)
    for r in subs:
        o_ref[r, :] = rms(acc[r.start]) * onw_ref[...]


def _out_ffn(x2, attn, o_f, o_b, z, gnw, wo, fnw, w1, w2, onw):
    n = x2.shape[0]
    row = lambda w: pl.BlockSpec((FFN_TM, w), lambda i: (i, 0))
    full = lambda a: pl.BlockSpec(a.shape, lambda i: (0, 0))
    return pl.pallas_call(
        _ffn_kernel,
        grid=(n // FFN_TM,),
        in_specs=[row(D_MODEL), row(ATTN_Q), row(GDN_W), row(GDN_W), row(GDN_W),
                  full(gnw), full(wo), full(fnw), full(w1), full(w2), full(onw)],
        out_specs=row(D_MODEL),
        out_shape=jax.ShapeDtypeStruct((n, D_MODEL), F32),
        compiler_params=pltpu.CompilerParams(dimension_semantics=("arbitrary",),
                                             vmem_limit_bytes=VMEM_LIMIT),
        name="out_ffn",
    )(x2, attn, o_f, o_b, z, gnw, wo, fnw, w1, w2, onw)


def _layer(h, band, norm_mix_w, w_in, attn_sink, conv_w, gdn_a_log, gdn_dt_bias, gdn_norm_w,
           w_out, norm_ffn_w, w_ffn_in, w_ffn_out, out_norm_w):
    b, s, _ = h.shape
    n = b * s
    x2 = h.reshape(n, D_MODEL)
    q_a, k_a, v_a, qkv_g, z_g, ab = _proj(x2, norm_mix_w.reshape(1, D_MODEL), w_in)

    attn = _attention(q_a.reshape(b, s, ATTN_Q), k_a.reshape(b, s, ATTN_KV), v_a.reshape(b, s, ATTN_KV),
                      band, attn_sink)

    conv_pad = jnp.zeros((SUBLANES, 3 * GDN_W), F32).at[:CONV_K].set(conv_w)
    gate_par = jnp.zeros((SUBLANES, LANES), F32)
    gate_par = gate_par.at[0, :N_CHAIN].set(gdn_a_log.reshape(-1)).at[1, :N_CHAIN].set(gdn_dt_bias.reshape(-1))
    w_c, u_c, q_dec, k_dec_t, qk, egl = _gdn_prep(qkv_g.reshape(b, s, 3 * GDN_W), ab.reshape(b, s, LANES),
                                                  conv_pad, gate_par)
    o_f, o_b = _gdn_scan(egl[..., 0].reshape(-1), w_c, u_c, q_dec, k_dec_t, qk)

    out = _out_ffn(x2, attn.reshape(n, ATTN_Q), o_f.reshape(n, GDN_W), o_b.reshape(n, GDN_W), z_g,
                   gdn_norm_w.reshape(1, GDN_HEAD_DIM), w_out.astype(BF16), norm_ffn_w.reshape(1, D_MODEL),
                   w_ffn_in.astype(BF16), w_ffn_out.astype(BF16), out_norm_w.reshape(1, D_MODEL))
    return out.reshape(b, s, D_MODEL)


def kernel(x, norm_mix_w, w_in, rel_bias, attn_sink, conv_w, gdn_a_log, gdn_dt_bias, gdn_norm_w, w_out,
           norm_ffn_w, w_ffn_in, w_ffn_out, norm_final_w):
    depth = w_in.shape[0]
    assert depth == 1, "the fused output kernel applies the final norm after the single trunk layer"
    rel = (np.arange(3 * BLOCK)[None, :] - BLOCK) - np.arange(BLOCK)[:, None]
    bucket = _t5_buckets(jnp.asarray(rel, dtype=jnp.int32))
    band = _bias_band(rel_bias, bucket.T)
    return _layer(x, band, norm_mix_w[0], w_in[0], attn_sink[0], conv_w[0], gdn_a_log[0], gdn_dt_bias[0],
                  gdn_norm_w[0], w_out[0], norm_ffn_w[0], w_ffn_in[0], w_ffn_out[0], norm_final_w)
```

```python
import functools
import math

import jax
import jax.numpy as jnp
import numpy as np
from jax import lax
from jax.experimental import pallas as pl
from jax.experimental.pallas import tpu as pltpu

F32 = jnp.float32
BF16 = jnp.bfloat16

D_MODEL = 1024
ATTN_HEADS = 8
ATTN_KV_HEADS = 2
ATTN_HEAD_DIM = 64
ATTN_GROUP = ATTN_HEADS // ATTN_KV_HEADS
WINDOW = 128
BLOCK = 128
N_BUCKETS = 32
MAX_DISTANCE = 128
GDN_HEADS = 4
GDN_HEAD_DIM = 128
CONV_K = 5
CHUNK = 64
N_DIR = 2
N_CHAIN = N_DIR * GDN_HEADS
D_FF = 4 * D_MODEL
EPS = 1e-6
ATTN_Q = ATTN_HEADS * ATTN_HEAD_DIM
ATTN_KV = ATTN_KV_HEADS * ATTN_HEAD_DIM
GDN_W = GDN_HEADS * GDN_HEAD_DIM
LANES = 128
SUBLANES = 8
VMEM_LIMIT = 56 * 1024 * 1024

PROJ_TM = 1024
PROJ_SUB = 512
PROJ_CAST_COLS = 256
ATTN_QB = 4
PREP_T = 512
ROW_STRIDE = 4
SCAN_CHUNKS = 4
FFN_TM = 512
FFN_SUB = 256
FFN_CAST_ROWS_WIDE = 128
FFN_CAST_ROWS_TALL = 512


def _dot(a, b):
    return jnp.dot(a.astype(BF16), b.astype(BF16), preferred_element_type=F32)


def _dot_nt(a, b):
    return lax.dot_general(a.astype(BF16), b.astype(BF16), (((1,), (1,)), ((), ())),
                           preferred_element_type=F32)


def _split3(x):
    hi = x.astype(BF16)
    r1 = x - hi.astype(F32)
    mid = r1.astype(BF16)
    lo = (r1 - mid.astype(F32)).astype(BF16)
    return hi, mid, lo


def _dot01_right(x, m01):
    hi, mid, lo = _split3(x)
    d = lambda p: jnp.dot(p, m01, preferred_element_type=F32)
    return d(hi) + d(mid) + d(lo)


def _proj_kernel(x_ref, nw_ref, w_ref, qa_ref, ka_ref, va_ref, qkvg_ref, z_ref, ab_ref, wb_ref):
    d_in = w_ref.shape[1]
    d_main = d_in // LANES * LANES

    @pl.when(pl.program_id(0) == 0)
    def _():
        for c0 in range(0, d_main, PROJ_CAST_COLS):
            c1 = min(c0 + PROJ_CAST_COLS, d_main)
            wb_ref[:, c0:c1] = w_ref[:, c0:c1].astype(BF16)
        gate = jnp.concatenate([w_ref[:, d_main:], jnp.zeros((D_MODEL, LANES - (d_in - d_main)), F32)], axis=1)
        wb_ref[:, d_main:] = gate.astype(BF16)

    outs = ((qa_ref, ATTN_Q, ATTN_HEAD_DIM ** -0.5), (ka_ref, ATTN_KV, None), (va_ref, ATTN_KV, None),
            (qkvg_ref, 3 * GDN_W, None), (z_ref, GDN_W, None), (ab_ref, LANES, None))
    for r0 in range(0, PROJ_TM, PROJ_SUB):
        rows = slice(r0, r0 + PROJ_SUB)
        x = x_ref[rows, :]
        ms = jnp.mean(x * x, axis=-1, keepdims=True)
        xn = (x * lax.rsqrt(ms + EPS) * nw_ref[...]).astype(BF16)
        c0 = 0
        for ref, width, scale in outs:
            y = jnp.dot(xn, wb_ref[:, c0:c0 + width], preferred_element_type=F32)
            if scale is not None:
                y = y * scale
            ref[rows, :] = y.astype(ref.dtype)
            c0 += width


def _proj(x2, norm_w, w_in):
    n = x2.shape[0]
    d_in = w_in.shape[1]
    d_pad = d_in // LANES * LANES + LANES
    assert d_pad == ATTN_Q + 2 * ATTN_KV + 4 * GDN_W + LANES and d_in - (d_pad - LANES) == 2 * N_CHAIN
    row = lambda w: pl.BlockSpec((PROJ_TM, w), lambda i: (i, 0))
    return pl.pallas_call(
        _proj_kernel,
        grid=(n // PROJ_TM,),
        in_specs=[row(D_MODEL),
                  pl.BlockSpec((1, D_MODEL), lambda i: (0, 0)),
                  pl.BlockSpec((D_MODEL, d_in), lambda i: (0, 0))],
        out_specs=[row(ATTN_Q), row(ATTN_KV), row(ATTN_KV), row(3 * GDN_W), row(GDN_W), row(LANES)],
        out_shape=[jax.ShapeDtypeStruct((n, ATTN_Q), BF16),
                   jax.ShapeDtypeStruct((n, ATTN_KV), BF16),
                   jax.ShapeDtypeStruct((n, ATTN_KV), BF16),
                   jax.ShapeDtypeStruct((n, 3 * GDN_W), F32),
                   jax.ShapeDtypeStruct((n, GDN_W), F32),
                   jax.ShapeDtypeStruct((n, LANES), F32)],
        scratch_shapes=[pltpu.VMEM((D_MODEL, d_pad), BF16)],
        compiler_params=pltpu.CompilerParams(dimension_semantics=("arbitrary",),
                                             vmem_limit_bytes=VMEM_LIMIT),
        name="proj",
    )(x2, norm_w, w_in)


def _bias_kernel(relb_ref, bucket_ref, o_ref):
    bucket = bucket_ref[...]
    key = lax.broadcasted_iota(jnp.int32, (3 * BLOCK, BLOCK), 0)
    qry = lax.broadcasted_iota(jnp.int32, (3 * BLOCK, BLOCK), 1)
    in_window = jnp.abs(key - BLOCK - qry) <= WINDOW
    for h in range(ATTN_HEADS):
        acc = jnp.zeros((3 * BLOCK, BLOCK), F32)
        for b in range(N_BUCKETS):
            acc = jnp.where(bucket == b, relb_ref[b, h], acc)
        o_ref[h // 2, :, (h % 2) * BLOCK:(h % 2 + 1) * BLOCK] = jnp.where(in_window, acc, -1e30)


def _bias_band(rel_bias, bucket_t):
    shape = (ATTN_HEADS // 2, 3 * BLOCK, 2 * BLOCK)
    return pl.pallas_call(
        _bias_kernel,
        in_specs=[pl.BlockSpec(memory_space=pltpu.SMEM),
                  pl.BlockSpec((3 * BLOCK, BLOCK), lambda: (0, 0))],
        out_specs=pl.BlockSpec(shape, lambda: (0, 0, 0)),
        out_shape=jax.ShapeDtypeStruct(shape, F32),
        name="bias_band",
    )(rel_bias, bucket_t)


def _t5_buckets(rel):
    nb = N_BUCKETS // 2
    max_exact = nb // 2
    base = jnp.where(rel > 0, nb, 0)
    n = jnp.abs(rel)
    log_ratio = jnp.log(jnp.maximum(n, 1).astype(jnp.float32) / max_exact) / math.log(MAX_DISTANCE / max_exact)
    large = jnp.minimum(max_exact + (log_ratio * (nb - max_exact)).astype(jnp.int32), nb - 1)
    return base + jnp.where(n < max_exact, n, large)


def _attn_kernel(sink_ref, q_ref, kp_ref, kc_ref, kn_ref, vp_ref, vc_ref, vn_ref, bias_ref, o_ref):
    n = pl.program_id(1)
    last = pl.num_programs(1) - 1
    kband = jnp.concatenate([kp_ref[0], kc_ref[0], kn_ref[0]], axis=0)
    vband = jnp.concatenate([vp_ref[0], vc_ref[0], vn_ref[0]], axis=0)
    vband_t = vband.astype(F32).T.astype(BF16)
    key = lax.broadcasted_iota(jnp.int32, (3 * BLOCK, 1), 0)
    first_head = lax.broadcasted_iota(jnp.int32, (1, 2 * BLOCK), 1) < BLOCK
    head = lambda t, i: t[:, i * ATTN_HEAD_DIM:(i + 1) * ATTN_HEAD_DIM]
    n_pairs = ATTN_HEADS // 2
    kv_of = lambda pr: (2 * pr) // ATTN_GROUP
    units = [(j, pr) for j in range(ATTN_QB) for pr in range(n_pairs)]
    band_rows = lambda j: slice(j * BLOCK, (j + 3) * BLOCK)

    scores = {}
    for j, pr in units:
        qj = q_ref[0, j * BLOCK:(j + 1) * BLOCK, :]
        q2 = jnp.concatenate([head(qj, 2 * pr), head(qj, 2 * pr + 1)], axis=0)
        scores[j, pr] = _dot_nt(head(kband, kv_of(pr))[band_rows(j)], q2)
    probs, dens = {}, {}
    for j, pr in units:
        s = scores[j, pr] + bias_ref[pr]
        if j == 0:
            s = jnp.where((key < BLOCK) & (n == 0), -1e30, s)
        if j == ATTN_QB - 1:
            s = jnp.where((key >= 2 * BLOCK) & (n == last), -1e30, s)
        sink = jnp.where(first_head, sink_ref[2 * pr], sink_ref[2 * pr + 1])
        m = jnp.maximum(jnp.max(s, axis=0, keepdims=True), sink)
        p = jnp.exp(s - m)
        dens[j, pr] = jnp.sum(p, axis=0, keepdims=True) + jnp.exp(sink - m)
        probs[j, pr] = p.astype(BF16)
    outs_t = {}
    for j, pr in units:
        kv = kv_of(pr)
        v_t = vband_t[kv * ATTN_HEAD_DIM:(kv + 1) * ATTN_HEAD_DIM, band_rows(j)]
        outs_t[j, pr] = jnp.dot(v_t, probs[j, pr], preferred_element_type=F32) / dens[j, pr]
    for j in range(ATTN_QB):
        o_t = jnp.concatenate([outs_t[j, pr][:, half * BLOCK:(half + 1) * BLOCK]
                               for pr in range(n_pairs) for half in range(2)], axis=0)
        o_ref[0, j * BLOCK:(j + 1) * BLOCK, :] = o_t.T.astype(o_ref.dtype)


def _attention(q_a, k_a, v_a, band, sink):
    b, s, _ = q_a.shape
    nb = s // BLOCK
    rows = ATTN_QB * BLOCK
    kv_spec = lambda r, f: pl.BlockSpec((1, r, ATTN_KV), f)
    prev = lambda bi, n: (bi, jnp.maximum(n * ATTN_QB - 1, 0), 0)
    cur = lambda bi, n: (bi, n, 0)
    nxt = lambda bi, n: (bi, jnp.minimum((n + 1) * ATTN_QB, nb - 1), 0)
    return pl.pallas_call(
        _attn_kernel,
        grid=(b, nb // ATTN_QB),
        in_specs=[pl.BlockSpec(memory_space=pltpu.SMEM),
                  pl.BlockSpec((1, rows, ATTN_Q), cur),
                  kv_spec(BLOCK, prev), kv_spec(rows, cur), kv_spec(BLOCK, nxt),
                  kv_spec(BLOCK, prev), kv_spec(rows, cur), kv_spec(BLOCK, nxt),
                  pl.BlockSpec(band.shape, lambda bi, n: (0, 0, 0))],
        out_specs=pl.BlockSpec((1, rows, ATTN_Q), cur),
        out_shape=jax.ShapeDtypeStruct((b, s, ATTN_Q), BF16),
        compiler_params=pltpu.CompilerParams(dimension_semantics=("arbitrary", "arbitrary"),
                                             vmem_limit_bytes=VMEM_LIMIT),
        name="attn",
    )(sink, q_a, k_a, k_a, k_a, v_a, v_a, v_a, band)


def _gprep_kernel(xp_ref, x_ref, xn_ref, ab_ref, cw_ref, gp_ref,
                  w_ref, u_ref, qd_ref, kdt_ref, qk_ref, egl_ref, xe_ref, yn_ref):
    i = pl.program_id(1)
    last = pl.num_programs(1) - 1
    t_len = PREP_T
    halo = CONV_K // 2

    n_slab = 3 * GDN_W // LANES
    for sb in range(n_slab):
        lanes = slice(sb * LANES, (sb + 1) * LANES)
        xe_ref[sb, 0:SUBLANES, :] = jnp.where(i > 0, xp_ref[0, :, lanes], 0.0)
        xe_ref[sb, SUBLANES:SUBLANES + t_len, :] = x_ref[0, :, lanes]
        xe_ref[sb, SUBLANES + t_len:, :] = jnp.where(i < last, xn_ref[0, :, lanes], 0.0)

    ab = ab_ref[0]
    sp_in = ab + gp_ref[1:2, :]
    softplus = jnp.maximum(sp_in, 0.0) + jnp.log1p(jnp.exp(-jnp.abs(sp_in)))
    g = -jnp.exp(gp_ref[0:1, :]) * softplus
    beta = jax.nn.sigmoid(ab)

    r_t = lax.broadcasted_iota(jnp.int32, (t_len, t_len), 0)
    c_t = lax.broadcasted_iota(jnp.int32, (t_len, t_len), 1)
    same = (r_t // CHUNK) == (c_t // CHUNK)
    lower = jnp.where(same & (r_t >= c_t), 1.0, 0.0).astype(BF16)
    upper = jnp.where(same & (r_t <= c_t), 1.0, 0.0).astype(BF16)
    g_t = g.T[:2 * SUBLANES]
    cs_row = (_dot01_right(g_t, upper), _dot01_right(g_t, lower))
    pad_rows = jnp.zeros((LANES - 2 * SUBLANES, t_len), F32)
    cs_col = tuple(jnp.concatenate([r, pad_rows], axis=0).T for r in cs_row)

    n_rows = t_len // ROW_STRIDE
    units = [(sb, ph) for sb in range(n_slab) for ph in range(ROW_STRIDE)]
    yv = {}
    for sb, ph in units:
        lanes = slice(sb * LANES, (sb + 1) * LANES)
        acc = None
        for j in range(CONV_K):
            win = xe_ref[sb, pl.ds(SUBLANES - halo + j + ph, n_rows, stride=ROW_STRIDE), :]
            term = cw_ref[j:j + 1, lanes] * win
            acc = term if acc is None else acc + term
        yv[sb, ph] = acc
    for key in units:
        yv[key] = yv[key] * jax.nn.sigmoid(yv[key])
    for sb, ph in units:
        if sb < 2 * GDN_HEADS:
            scale = lax.rsqrt(jnp.sum(yv[sb, ph] * yv[sb, ph], axis=-1, keepdims=True) + EPS)
            if sb < GDN_HEADS:
                scale = scale * (GDN_HEAD_DIM ** -0.5)
            yv[sb, ph] = yv[sb, ph] * scale
    for sb, ph in units:
        yn_ref[sb, pl.ds(ph, n_rows, stride=ROW_STRIDE), :] = yv[sb, ph]

    qs = [yn_ref[h] for h in range(GDN_HEADS)]
    ks = [yn_ref[GDN_HEADS + h] for h in range(GDN_HEADS)]
    vs = [yn_ref[2 * GDN_HEADS + h] for h in range(GDN_HEADS)]
    kts = [kh.T for kh in ks]

    r_c = lax.broadcasted_iota(jnp.int32, (CHUNK, LANES), 0)
    lane = lax.broadcasted_iota(jnp.int32, (CHUNK, LANES), 1)
    is_fwd = lane < CHUNK
    c_c = lane % CHUNK
    eye = jnp.where(r_c == c_c, 1.0, 0.0).astype(F32)
    incl = (is_fwd & (r_c >= c_c)) | (~is_fwd & (r_c <= c_c))
    strict = (is_fwd & (r_c > c_c)) | (~is_fwd & (r_c < c_c))
    r_d = lax.broadcasted_iota(jnp.int32, (2 * CHUNK, LANES), 0)
    c_d = lax.broadcasted_iota(jnp.int32, (2 * CHUNK, LANES), 1)
    same_dir = (r_d // CHUNK) == (c_d // CHUNK)
    level_mask = lambda s_, r_, c_: ((r_ // (2 * s_)) == (c_ // (2 * s_))) & ((r_ // s_) != (c_ // s_))
    stack2 = lambda t: jnp.concatenate([t, t], axis=0)

    n_chunks = t_len // CHUNK
    rows = lambda c: slice(c * CHUNK, (c + 1) * CHUNK)
    pairs = [(c, h) for c in range(n_chunks) for h in range(GDN_HEADS)]
    qkk = {}
    for c, h in pairs:
        k16 = ks[h][rows(c)].astype(BF16)
        qk16 = jnp.concatenate([qs[h][rows(c)].astype(BF16), k16], axis=0)
        qkk[c, h] = _dot_nt(qk16, stack2(k16))

    bcast = lambda col: jnp.broadcast_to(col, (CHUNK, LANES))
    g_full, b_full, grow, glast, decay, a_mat, t_mat = {}, {}, {}, {}, {}, {}, {}
    for key in pairs:
        c, h = key
        for d in range(N_DIR):
            j = d * GDN_HEADS + h
            r_last = c * CHUNK + (CHUNK - 1 if d == 0 else 0)
            g_full[key, d] = bcast(cs_col[d][rows(c), j:j + 1])
            b_full[key, d] = bcast(beta[rows(c), SUBLANES + j:SUBLANES + j + 1])
            glast[key, d] = cs_col[d][r_last:r_last + 1, j:j + 1]
            grow[key, d] = cs_row[d][j:j + 1, rows(c)]
        gcol2 = jnp.where(is_fwd, g_full[key, 0], g_full[key, 1])
        bcol2 = jnp.where(is_fwd, b_full[key, 0], b_full[key, 1])
        grow2 = jnp.concatenate([grow[key, 0], grow[key, 1]], axis=1)
        decay[key] = jnp.exp(jnp.where(incl, gcol2 - grow2, -jnp.inf))
        a_mat[key] = jnp.where(strict, bcol2 * qkk[key][CHUNK:] * decay[key], 0.0)
        t_mat[key] = eye - jnp.where(level_mask(1, r_c, c_c), a_mat[key], 0.0)

    same_dir16 = jnp.where(same_dir, 1.0, 0.0).astype(BF16)
    block_diag = lambda t16: stack2(t16) * same_dir16
    a_bd = {key: block_diag(a_mat[key].astype(BF16)) for key in pairs}
    s = 2
    while s < CHUNK:
        lvl = level_mask(s, r_c, c_c)
        t16 = {key: t_mat[key].astype(BF16) for key in pairs}
        x_mat = {key: jnp.dot(t16[key], a_bd[key], preferred_element_type=F32) for key in pairs}
        y_mat = {key: jnp.dot(x_mat[key].astype(BF16), block_diag(t16[key]), preferred_element_type=F32)
                 for key in pairs}
        t_mat = {key: t_mat[key] - jnp.where(lvl, y_mat[key], 0.0) for key in pairs}
        s *= 2

    wu = {}
    for key in pairs:
        c, h = key
        kv = jnp.concatenate(
            [jnp.concatenate([ks[h][rows(c)] * (b_full[key, d] * jnp.exp(g_full[key, d])),
                              vs[h][rows(c)] * b_full[key, d]], axis=1) for d in range(N_DIR)], axis=0)
        t_sel = block_diag(t_mat[key].astype(BF16))
        wu[key] = jnp.dot(t_sel, kv.astype(BF16), preferred_element_type=F32)

    for c in range(n_chunks):
        egl_rows = []
        for d in range(N_DIR):
            for h in range(GDN_HEADS):
                key = (c, h)
                j = d * GDN_HEADS + h
                wu_d = wu[key][d * CHUNK:(d + 1) * CHUNK]
                w_ref[0, j, rows(c), :] = wu_d[:, :GDN_HEAD_DIM].astype(w_ref.dtype)
                u_ref[0, j, rows(c), :] = wu_d[:, GDN_HEAD_DIM:].astype(u_ref.dtype)
                qd_ref[0, j, rows(c), :] = (qs[h][rows(c)] * jnp.exp(g_full[key, d])).astype(qd_ref.dtype)
                egl_rows.append(jnp.broadcast_to(jnp.exp(glast[key, d]), (1, LANES)))
        egl_ref[0, c] = jnp.concatenate(egl_rows, axis=0)
        for h in range(GDN_HEADS):
            qk_ref[0, h, rows(c), :] = (qkk[c, h][:CHUNK] * decay[c, h]).astype(qk_ref.dtype)
    for cp in range(n_chunks // 2):
        for d in range(N_DIR):
            for h in range(GDN_HEADS):
                fac = jnp.concatenate([jnp.exp(glast[(c, h), d] - grow[(c, h), d]) for c in (2 * cp, 2 * cp + 1)],
                                      axis=1)
                kdt = kts[h][:, 2 * cp * CHUNK:(2 * cp + 2) * CHUNK] * fac
                kdt_ref[0, d * GDN_HEADS + h, cp] = kdt.astype(kdt_ref.dtype)


def _gdn_prep(qkv_g, ab, conv_w, gate_par):
    b, s, _ = qkv_g.shape
    nt = s // PREP_T
    cpb = PREP_T // CHUNK
    nc = s // CHUNK
    nh8 = PREP_T // SUBLANES
    chain = lambda last: pl.BlockSpec((1, N_CHAIN, PREP_T, last), lambda bi, i: (bi, 0, i, 0))
    return pl.pallas_call(
        _gprep_kernel,
        grid=(b, nt),
        in_specs=[pl.BlockSpec((1, SUBLANES, 3 * GDN_W), lambda bi, i: (bi, jnp.maximum(i * nh8 - 1, 0), 0)),
                  pl.BlockSpec((1, PREP_T, 3 * GDN_W), lambda bi, i: (bi, i, 0)),
                  pl.BlockSpec((1, SUBLANES, 3 * GDN_W),
                               lambda bi, i: (bi, jnp.minimum((i + 1) * nh8, s // SUBLANES - 1), 0)),
                  pl.BlockSpec((1, PREP_T, LANES), lambda bi, i: (bi, i, 0)),
                  pl.BlockSpec((SUBLANES, 3 * GDN_W), lambda bi, i: (0, 0)),
                  pl.BlockSpec((SUBLANES, LANES), lambda bi, i: (0, 0))],
        out_specs=[chain(GDN_HEAD_DIM), chain(GDN_HEAD_DIM), chain(GDN_HEAD_DIM),
                   pl.BlockSpec((1, N_CHAIN, cpb // 2, GDN_HEAD_DIM, 2 * CHUNK), lambda bi, i: (bi, 0, i, 0, 0)),
                   pl.BlockSpec((1, GDN_HEADS, PREP_T, LANES), lambda bi, i: (bi, 0, i, 0)),
                   pl.BlockSpec((1, cpb, N_CHAIN, LANES), lambda bi, i: (bi, i, 0, 0))],
        out_shape=[jax.ShapeDtypeStruct((b, N_CHAIN, s, GDN_HEAD_DIM), BF16),
                   jax.ShapeDtypeStruct((b, N_CHAIN, s, GDN_HEAD_DIM), BF16),
                   jax.ShapeDtypeStruct((b, N_CHAIN, s, GDN_HEAD_DIM), BF16),
                   jax.ShapeDtypeStruct((b, N_CHAIN, nc // 2, GDN_HEAD_DIM, 2 * CHUNK), BF16),
                   jax.ShapeDtypeStruct((b, GDN_HEADS, s, LANES), BF16),
                   jax.ShapeDtypeStruct((b, nc, N_CHAIN, LANES), F32)],
        scratch_shapes=[pltpu.VMEM((3 * GDN_W // LANES, PREP_T + 2 * SUBLANES, LANES), F32),
                        pltpu.VMEM((3 * GDN_W // LANES, PREP_T, LANES), F32)],
        compiler_params=pltpu.CompilerParams(dimension_semantics=("arbitrary", "arbitrary"),
                                             vmem_limit_bytes=VMEM_LIMIT),
        name="gdn_prep",
    )(qkv_g, qkv_g, qkv_g, ab, conv_w, gate_par)


def _gscan_kernel(egl_ref, wf_ref, wb_ref, uf_ref, ub_ref, qf_ref, qb_ref, kf_ref, kb_ref,
                  pf_ref, pb_ref, of_ref, ob_ref, state_ref):
    t = pl.program_id(0)
    n_steps = pl.num_programs(0)
    nc = n_steps * SCAN_CHUNKS
    n_batch = wf_ref.shape[0]

    @pl.when(t == 0)
    def _():
        state_ref[...] = jnp.zeros_like(state_ref)

    dirs = ((wf_ref, uf_ref, qf_ref, kf_ref, pf_ref, of_ref), (wb_ref, ub_ref, qb_ref, kb_ref, pb_ref, ob_ref))
    chains = [(bi, d, h) for bi in range(n_batch) for d in range(N_DIR) for h in range(GDN_HEADS)]
    slot = lambda bi, d, h: (bi * N_DIR + d) * GDN_HEADS + h
    st = {key: state_ref[slot(*key)] for key in chains}

    for sub in range(SCAN_CHUNKS):
        local = (sub, SCAN_CHUNKS - 1 - sub)
        rows = [slice(c * CHUNK, (c + 1) * CHUNK) for c in local]
        chunk = (t * SCAN_CHUNKS + sub, nc - 1 - (t * SCAN_CHUNKS + sub))
        r = {}
        for key in chains:
            bi, d, h = key
            w_ref, _, q_ref = dirs[d][:3]
            wq = jnp.concatenate([w_ref[bi, h, rows[d]], q_ref[bi, h, rows[d]]], axis=0)
            r[key] = _dot(wq, st[key])
        v_pad, intra = {}, {}
        for key in chains:
            bi, d, h = key
            u_ref, p_ref = dirs[d][1], dirs[d][4]
            v_new = (u_ref[bi, h, rows[d]].astype(F32) - r[key][:CHUNK]).astype(BF16)
            zeros = jnp.zeros_like(v_new)
            v_pad[key] = (jnp.concatenate([v_new, zeros], axis=0), jnp.concatenate([zeros, v_new], axis=0))
            intra[key] = jnp.dot(p_ref[bi, h, rows[d]], v_pad[key][d], preferred_element_type=F32)
        for key in chains:
            bi, d, h = key
            k_ref = dirs[d][3]
            egl = egl_ref[(bi * nc + chunk[d]) * N_CHAIN + d * GDN_HEADS + h]
            st[key] = st[key] * egl + jnp.dot(k_ref[bi, h, local[d] // 2], v_pad[key][local[d] % 2],
                                              preferred_element_type=F32)
        for bi in range(n_batch):
            for d in range(N_DIR):
                o_ref = dirs[d][5]
                o_ref[bi, rows[d], :] = jnp.concatenate(
                    [r[bi, d, h][CHUNK:] + intra[bi, d, h] for h in range(GDN_HEADS)], axis=-1).astype(o_ref.dtype)

    for key in chains:
        state_ref[slot(*key)] = st[key]


def _gdn_scan(egl, w, u, qd, kdt, qk):
    b, _, s, _ = w.shape
    rows = SCAN_CHUNKS * CHUNK
    n_steps = s // rows
    fwd_i = lambda t: t
    bwd_i = lambda t: n_steps - 1 - t
    chain = lambda d, at, last: pl.BlockSpec((b, GDN_HEADS, rows, last), lambda t: (0, d, at(t), 0))
    kspec = lambda d, at: pl.BlockSpec((b, GDN_HEADS, SCAN_CHUNKS // 2, GDN_HEAD_DIM, 2 * CHUNK),
                                       lambda t: (0, d, at(t), 0, 0))
    dk = GDN_HEAD_DIM
    return pl.pallas_call(
        _gscan_kernel,
        grid=(n_steps,),
        in_specs=[pl.BlockSpec(memory_space=pltpu.SMEM),
                  chain(0, fwd_i, dk), chain(1, bwd_i, dk), chain(0, fwd_i, dk), chain(1, bwd_i, dk),
                  chain(0, fwd_i, dk), chain(1, bwd_i, dk), kspec(0, fwd_i), kspec(1, bwd_i),
                  chain(0, fwd_i, LANES), chain(0, bwd_i, LANES)],
        out_specs=[pl.BlockSpec((b, rows, GDN_W), lambda t: (0, fwd_i(t), 0)),
                   pl.BlockSpec((b, rows, GDN_W), lambda t: (0, bwd_i(t), 0))],
        out_shape=[jax.ShapeDtypeStruct((b, s, GDN_W), BF16),
                   jax.ShapeDtypeStruct((b, s, GDN_W), BF16)],
        scratch_shapes=[pltpu.VMEM((b * N_CHAIN, GDN_HEAD_DIM, GDN_HEAD_DIM), F32)],
        compiler_params=pltpu.CompilerParams(dimension_semantics=("arbitrary",),
                                             vmem_limit_bytes=VMEM_LIMIT),
        name="gdn_scan",
    )(egl, w, w, u, u, qd, qd, kdt, kdt, qk, qk)


def _cast_weights_once(copies):
    jobs, used = [], {}
    for src, dst, stage, sem, chunk_rows in copies:
        for r0 in range(0, src.shape[0], chunk_rows):
            slot = used.get(id(stage), 0) % 2
            used[id(stage)] = used.get(id(stage), 0) + 1
            rows = pl.ds(r0, chunk_rows)
            jobs.append((pltpu.make_async_copy(src.at[rows], stage.at[slot], sem.at[slot]), stage, slot, dst, rows))
    jobs[0][0].start()
    for k, (copy, stage, slot, dst, rows) in enumerate(jobs):
        if k + 1 < len(jobs):
            jobs[k + 1][0].start()
        copy.wait()
        dst[rows, :] = stage[slot].astype(BF16)


def _ffn_kernel(x_ref, attn_ref, of_ref, ob_ref, z_ref, gnw_ref, wo_hbm, fnw_ref, w1_hbm, w2_hbm, onw_ref,
                o_ref, wo_ref, w1_ref, w2_ref, stage_wide, stage_tall, sem_wide, sem_tall):
    @pl.when(pl.program_id(0) == 0)
    def _():
        _cast_weights_once([(w1_hbm, w1_ref, stage_wide, sem_wide, FFN_CAST_ROWS_WIDE),
                            (w2_hbm, w2_ref, stage_tall, sem_tall, FFN_CAST_ROWS_TALL),
                            (wo_hbm, wo_ref, stage_tall, sem_tall, FFN_CAST_ROWS_TALL)])

    subs = [slice(r0, r0 + FFN_SUB) for r0 in range(0, FFN_TM, FFN_SUB)]
    rms = lambda t: t * lax.rsqrt(jnp.mean(t * t, axis=-1, keepdims=True) + EPS)
    hres, hn, act, acc = {}, {}, {}, {}
    for r in subs:
        o = of_ref[r, :].astype(F32) + ob_ref[r, :].astype(F32)
        heads = [rms(o[:, h * GDN_HEAD_DIM:(h + 1) * GDN_HEAD_DIM]) * gnw_ref[...] for h in range(GDN_HEADS)]
        z = z_ref[r, :]
        gdn = jnp.concatenate(heads, axis=-1) * (z * jax.nn.sigmoid(z))
        hres[r.start] = (x_ref[r, :] + jnp.dot(attn_ref[r, :], wo_ref[:ATTN_Q, :], preferred_element_type=F32)
                         + _dot(gdn, wo_ref[ATTN_Q:, :]))
    for r in subs:
        hn[r.start] = (rms(hres[r.start]) * fnw_ref[...]).astype(BF16)
        act[r.start] = jnp.dot(hn[r.start], w1_ref[...], preferred_element_type=F32)
    for r in subs:
        a = jnp.square(jnp.maximum(act[r.start], 0.0)).astype(BF16)
        acc[r.start] = hres[r.start] + jnp.dot(a, w2_ref[...], preferred_element_type=F32)
    for r in subs:
        o_ref[r, :] = rms(acc[r.start]) * onw_ref[...]


def _out_ffn(x2, attn, o_f, o_b, z, gnw, wo, fnw, w1, w2, onw):
    n = x2.shape[0]
    assert wo.shape == (D_MODEL, D_MODEL) and w1.shape == (D_MODEL, D_FF) and w2.shape == (D_FF, D_MODEL)
    row = lambda w: pl.BlockSpec((FFN_TM, w), lambda i: (i, 0))
    full = lambda a: pl.BlockSpec(a.shape, lambda i: (0, 0))
    hbm = pl.BlockSpec(memory_space=pl.ANY)
    return pl.pallas_call(
        _ffn_kernel,
        grid=(n // FFN_TM,),
        in_specs=[row(D_MODEL), row(ATTN_Q), row(GDN_W), row(GDN_W), row(GDN_W),
                  full(gnw), hbm, full(fnw), hbm, hbm, full(onw)],
        out_specs=row(D_MODEL),
        out_shape=jax.ShapeDtypeStruct((n, D_MODEL), F32),
        scratch_shapes=[pltpu.VMEM((D_MODEL, D_MODEL), BF16),
                        pltpu.VMEM((D_MODEL, D_FF), BF16),
                        pltpu.VMEM((D_FF, D_MODEL), BF16),
                        pltpu.VMEM((2, FFN_CAST_ROWS_WIDE, D_FF), F32),
                        pltpu.VMEM((2, FFN_CAST_ROWS_TALL, D_MODEL), F32),
                        pltpu.SemaphoreType.DMA((2,)),
                        pltpu.SemaphoreType.DMA((2,))],
        compiler_params=pltpu.CompilerParams(dimension_semantics=("arbitrary",),
                                             vmem_limit_bytes=VMEM_LIMIT),
        name="out_ffn",
    )(x2, attn, o_f, o_b, z, gnw, wo, fnw, w1, w2, onw)


def _layer(h, band, norm_mix_w, w_in, attn_sink, conv_w, gdn_a_log, gdn_dt_bias, gdn_norm_w,
           w_out, norm_ffn_w, w_ffn_in, w_ffn_out, out_norm_w):
    b, s, _ = h.shape
    n = b * s
    x2 = h.reshape(n, D_MODEL)
    q_a, k_a, v_a, qkv_g, z_g, ab = _proj(x2, norm_mix_w.reshape(1, D_MODEL), w_in)

    attn = _attention(q_a.reshape(b, s, ATTN_Q), k_a.reshape(b, s, ATTN_KV), v_a.reshape(b, s, ATTN_KV),
                      band, attn_sink)

    conv_pad = jnp.zeros((SUBLANES, 3 * GDN_W), F32).at[:CONV_K].set(conv_w)
    gate_par = jnp.zeros((SUBLANES, LANES), F32)
    gate_par = gate_par.at[0, :N_CHAIN].set(gdn_a_log.reshape(-1)).at[1, :N_CHAIN].set(gdn_dt_bias.reshape(-1))
    w_c, u_c, q_dec, k_dec_t, qk, egl = _gdn_prep(qkv_g.reshape(b, s, 3 * GDN_W), ab.reshape(b, s, LANES),
                                                  conv_pad, gate_par)
    o_f, o_b = _gdn_scan(egl[..., 0].reshape(-1), w_c, u_c, q_dec, k_dec_t, qk)

    out = _out_ffn(x2, attn.reshape(n, ATTN_Q), o_f.reshape(n, GDN_W), o_b.reshape(n, GDN_W), z_g,
                   gdn_norm_w.reshape(1, GDN_HEAD_DIM), w_out, norm_ffn_w.reshape(1, D_MODEL),
                   w_ffn_in, w_ffn_out, out_norm_w.reshape(1, D_MODEL))
    return out.reshape(b, s, D_MODEL)


def kernel(x, norm_mix_w, w_in, rel_bias, attn_sink, conv_w, gdn_a_log, gdn_dt_bias, gdn_norm_w, w_out,
           norm_ffn_w, w_ffn_in, w_ffn_out, norm_final_w):
    depth = w_in.shape[0]
    assert depth == 1, "the fused output kernel applies the final norm after the single trunk layer"
    rel = (np.arange(3 * BLOCK)[None, :] - BLOCK) - np.arange(BLOCK)[:, None]
    bucket = _t5_buckets(jnp.asarray(rel, dtype=jnp.int32))
    band = _bias_band(rel_bias, bucket.T)
    return _layer(x, band, norm_mix_w[0], w_in[0], attn_sink[0], conv_w[0], gdn_a_log[0], gdn_dt_bias[0],
                  gdn_norm_w[0], w_out[0], norm_ffn_w[0], w_ffn_in[0], w_ffn_out[0], norm_final_w)
```

```python
import functools
import math

import jax
import jax.numpy as jnp
import numpy as np
from jax import lax
from jax.experimental import pallas as pl
from jax.experimental.pallas import tpu as pltpu

F32 = jnp.float32
BF16 = jnp.bfloat16

D_MODEL = 1024
ATTN_HEADS = 8
ATTN_KV_HEADS = 2
ATTN_HEAD_DIM = 64
ATTN_GROUP = ATTN_HEADS // ATTN_KV_HEADS
WINDOW = 128
BLOCK = 128
N_BUCKETS = 32
MAX_DISTANCE = 128
GDN_HEADS = 4
GDN_HEAD_DIM = 128
CONV_K = 5
CHUNK = 64
N_DIR = 2
N_CHAIN = N_DIR * GDN_HEADS
D_FF = 4 * D_MODEL
EPS = 1e-6
ATTN_Q = ATTN_HEADS * ATTN_HEAD_DIM
ATTN_KV = ATTN_KV_HEADS * ATTN_HEAD_DIM
GDN_W = GDN_HEADS * GDN_HEAD_DIM
LANES = 128
SUBLANES = 8
VMEM_LIMIT = 56 * 1024 * 1024

PROJ_TM = 1024
PROJ_SUB = 512
PROJ_CAST_COLS = 256
ATTN_QB = 4
PREP_T = 512
ROW_STRIDE = 4
SCAN_CHUNKS = 4
FFN_TM = 512
FFN_SUB = 256
FFN_CAST_ROWS_WIDE = 128
FFN_CAST_ROWS_TALL = 512


def _dot(a, b):
    return jnp.dot(a.astype(BF16), b.astype(BF16), preferred_element_type=F32)


def _dot_nt(a, b):
    return lax.dot_general(a.astype(BF16), b.astype(BF16), (((1,), (1,)), ((), ())),
                           preferred_element_type=F32)


def _split3(x):
    hi = x.astype(BF16)
    r1 = x - hi.astype(F32)
    mid = r1.astype(BF16)
    lo = (r1 - mid.astype(F32)).astype(BF16)
    return hi, mid, lo


def _dot01_right(x, m01):
    hi, mid, lo = _split3(x)
    d = lambda p: jnp.dot(p, m01, preferred_element_type=F32)
    return d(hi) + d(mid) + d(lo)


def _proj_kernel(x_ref, nw_ref, w_ref, qa_ref, ka_ref, va_ref, qkvg_ref, z_ref, ab_ref, wb_ref):
    d_in = w_ref.shape[1]
    d_main = d_in // LANES * LANES

    @pl.when(pl.program_id(0) == 0)
    def _():
        for c0 in range(0, d_main, PROJ_CAST_COLS):
            c1 = min(c0 + PROJ_CAST_COLS, d_main)
            wb_ref[:, c0:c1] = w_ref[:, c0:c1].astype(BF16)
        gate = jnp.concatenate([w_ref[:, d_main:], jnp.zeros((D_MODEL, LANES - (d_in - d_main)), F32)], axis=1)
        wb_ref[:, d_main:] = gate.astype(BF16)

    outs = ((qa_ref, ATTN_Q, ATTN_HEAD_DIM ** -0.5), (ka_ref, ATTN_KV, None), (va_ref, ATTN_KV, None),
            (qkvg_ref, 3 * GDN_W, None), (z_ref, GDN_W, None), (ab_ref, LANES, None))
    for r0 in range(0, PROJ_TM, PROJ_SUB):
        rows = slice(r0, r0 + PROJ_SUB)
        x = x_ref[rows, :]
        ms = jnp.mean(x * x, axis=-1, keepdims=True)
        xn = (x * lax.rsqrt(ms + EPS) * nw_ref[...]).astype(BF16)
        c0 = 0
        for ref, width, scale in outs:
            y = jnp.dot(xn, wb_ref[:, c0:c0 + width], preferred_element_type=F32)
            if scale is not None:
                y = y * scale
            ref[rows, :] = y.astype(ref.dtype)
            c0 += width


def _proj(x2, norm_w, w_in):
    n = x2.shape[0]
    d_in = w_in.shape[1]
    d_pad = d_in // LANES * LANES + LANES
    assert d_pad == ATTN_Q + 2 * ATTN_KV + 4 * GDN_W + LANES and d_in - (d_pad - LANES) == 2 * N_CHAIN
    row = lambda w: pl.BlockSpec((PROJ_TM, w), lambda i: (i, 0))
    return pl.pallas_call(
        _proj_kernel,
        grid=(n // PROJ_TM,),
        in_specs=[row(D_MODEL),
                  pl.BlockSpec((1, D_MODEL), lambda i: (0, 0)),
                  pl.BlockSpec((D_MODEL, d_in), lambda i: (0, 0))],
        out_specs=[row(ATTN_Q), row(ATTN_KV), row(ATTN_KV), row(3 * GDN_W), row(GDN_W), row(LANES)],
        out_shape=[jax.ShapeDtypeStruct((n, ATTN_Q), BF16),
                   jax.ShapeDtypeStruct((n, ATTN_KV), BF16),
                   jax.ShapeDtypeStruct((n, ATTN_KV), BF16),
                   jax.ShapeDtypeStruct((n, 3 * GDN_W), F32),
                   jax.ShapeDtypeStruct((n, GDN_W), F32),
                   jax.ShapeDtypeStruct((n, LANES), F32)],
        scratch_shapes=[pltpu.VMEM((D_MODEL, d_pad), BF16)],
        compiler_params=pltpu.CompilerParams(dimension_semantics=("arbitrary",),
                                             vmem_limit_bytes=VMEM_LIMIT),
        name="proj",
    )(x2, norm_w, w_in)


def _bias_kernel(relb_ref, bucket_ref, o_ref):
    bucket = bucket_ref[...]
    key = lax.broadcasted_iota(jnp.int32, (3 * BLOCK, BLOCK), 0)
    qry = lax.broadcasted_iota(jnp.int32, (3 * BLOCK, BLOCK), 1)
    in_window = jnp.abs(key - BLOCK - qry) <= WINDOW
    for h in range(ATTN_HEADS):
        acc = jnp.zeros((3 * BLOCK, BLOCK), F32)
        for b in range(N_BUCKETS):
            acc = jnp.where(bucket == b, relb_ref[b, h], acc)
        o_ref[h // 2, :, (h % 2) * BLOCK:(h % 2 + 1) * BLOCK] = jnp.where(in_window, acc, -1e30)


def _bias_band(rel_bias, bucket_t):
    shape = (ATTN_HEADS // 2, 3 * BLOCK, 2 * BLOCK)
    return pl.pallas_call(
        _bias_kernel,
        in_specs=[pl.BlockSpec(memory_space=pltpu.SMEM),
                  pl.BlockSpec((3 * BLOCK, BLOCK), lambda: (0, 0))],
        out_specs=pl.BlockSpec(shape, lambda: (0, 0, 0)),
        out_shape=jax.ShapeDtypeStruct(shape, F32),
        name="bias_band",
    )(rel_bias, bucket_t)


def _t5_buckets(rel):
    nb = N_BUCKETS // 2
    max_exact = nb // 2
    base = jnp.where(rel > 0, nb, 0)
    n = jnp.abs(rel)
    log_ratio = jnp.log(jnp.maximum(n, 1).astype(jnp.float32) / max_exact) / math.log(MAX_DISTANCE / max_exact)
    large = jnp.minimum(max_exact + (log_ratio * (nb - max_exact)).astype(jnp.int32), nb - 1)
    return base + jnp.where(n < max_exact, n, large)


def _attn_kernel(sink_ref, q_ref, kp_ref, kc_ref, kn_ref, vp_ref, vc_ref, vn_ref, bias_ref, o_ref):
    n = pl.program_id(1)
    last = pl.num_programs(1) - 1
    kband = jnp.concatenate([kp_ref[...], kc_ref[...], kn_ref[...]], axis=0)
    vband = jnp.concatenate([vp_ref[...], vc_ref[...], vn_ref[...]], axis=0)
    vband_t = vband.astype(F32).T.astype(BF16)
    key = lax.broadcasted_iota(jnp.int32, (3 * BLOCK, 1), 0)
    first_head = lax.broadcasted_iota(jnp.int32, (1, 2 * BLOCK), 1) < BLOCK
    head = lambda t, i: t[:, i * ATTN_HEAD_DIM:(i + 1) * ATTN_HEAD_DIM]
    n_pairs = ATTN_HEADS // 2
    kv_of = lambda pr: (2 * pr) // ATTN_GROUP
    units = [(j, pr) for j in range(ATTN_QB) for pr in range(n_pairs)]
    band_rows = lambda j: slice(j * BLOCK, (j + 3) * BLOCK)

    scores = {}
    for j, pr in units:
        qj = q_ref[j * BLOCK:(j + 1) * BLOCK, :]
        q2 = jnp.concatenate([head(qj, 2 * pr), head(qj, 2 * pr + 1)], axis=0)
        scores[j, pr] = _dot_nt(head(kband, kv_of(pr))[band_rows(j)], q2)
    probs, dens = {}, {}
    for j, pr in units:
        s = scores[j, pr] + bias_ref[pr]
        if j == 0:
            s = jnp.where((key < BLOCK) & (n == 0), -1e30, s)
        if j == ATTN_QB - 1:
            s = jnp.where((key >= 2 * BLOCK) & (n == last), -1e30, s)
        sink = jnp.where(first_head, sink_ref[2 * pr], sink_ref[2 * pr + 1])
        m = jnp.maximum(jnp.max(s, axis=0, keepdims=True), sink)
        p = jnp.exp(s - m)
        dens[j, pr] = jnp.sum(p, axis=0, keepdims=True) + jnp.exp(sink - m)
        probs[j, pr] = p.astype(BF16)
    outs_t = {}
    for j, pr in units:
        kv = kv_of(pr)
        v_t = vband_t[kv * ATTN_HEAD_DIM:(kv + 1) * ATTN_HEAD_DIM, band_rows(j)]
        outs_t[j, pr] = jnp.dot(v_t, probs[j, pr], preferred_element_type=F32) / dens[j, pr]
    for j in range(ATTN_QB):
        o_t = jnp.concatenate([outs_t[j, pr][:, half * BLOCK:(half + 1) * BLOCK]
                               for pr in range(n_pairs) for half in range(2)], axis=0)
        o_ref[j * BLOCK:(j + 1) * BLOCK, :] = o_t.T.astype(o_ref.dtype)


def _attention(q_a, k_a, v_a, band, sink, batch):
    n_tok = q_a.shape[0]
    nb = n_tok // batch // BLOCK
    rows = ATTN_QB * BLOCK
    steps = nb // ATTN_QB
    kv_spec = lambda r, f: pl.BlockSpec((r, ATTN_KV), f)
    prev = lambda bi, n: (bi * nb + jnp.maximum(n * ATTN_QB - 1, 0), 0)
    cur = lambda bi, n: (bi * steps + n, 0)
    nxt = lambda bi, n: (bi * nb + jnp.minimum((n + 1) * ATTN_QB, nb - 1), 0)
    return pl.pallas_call(
        _attn_kernel,
        grid=(batch, steps),
        in_specs=[pl.BlockSpec(memory_space=pltpu.SMEM),
                  pl.BlockSpec((rows, ATTN_Q), cur),
                  kv_spec(BLOCK, prev), kv_spec(rows, cur), kv_spec(BLOCK, nxt),
                  kv_spec(BLOCK, prev), kv_spec(rows, cur), kv_spec(BLOCK, nxt),
                  pl.BlockSpec(band.shape, lambda bi, n: (0, 0, 0))],
        out_specs=pl.BlockSpec((rows, ATTN_Q), cur),
        out_shape=jax.ShapeDtypeStruct((n_tok, ATTN_Q), BF16),
        compiler_params=pltpu.CompilerParams(dimension_semantics=("arbitrary", "arbitrary"),
                                             vmem_limit_bytes=VMEM_LIMIT),
        name="attn",
    )(sink, q_a, k_a, k_a, k_a, v_a, v_a, v_a, band)


def _gprep_kernel(xp_ref, x_ref, xn_ref, ab_ref, cw_ref, gp_ref,
                  w_ref, u_ref, qd_ref, kdt_ref, qk_ref, egl_ref, xe_ref, yn_ref):
    i = pl.program_id(1)
    last = pl.num_programs(1) - 1
    t_len = PREP_T
    halo = CONV_K // 2

    n_slab = 3 * GDN_W // LANES
    for sb in range(n_slab):
        lanes = slice(sb * LANES, (sb + 1) * LANES)
        xe_ref[sb, 0:SUBLANES, :] = jnp.where(i > 0, xp_ref[0, :, lanes], 0.0)
        xe_ref[sb, SUBLANES:SUBLANES + t_len, :] = x_ref[0, :, lanes]
        xe_ref[sb, SUBLANES + t_len:, :] = jnp.where(i < last, xn_ref[0, :, lanes], 0.0)

    ab = ab_ref[0]
    sp_in = ab + gp_ref[1:2, :]
    softplus = jnp.maximum(sp_in, 0.0) + jnp.log1p(jnp.exp(-jnp.abs(sp_in)))
    g = -jnp.exp(gp_ref[0:1, :]) * softplus
    beta = jax.nn.sigmoid(ab)

    r_t = lax.broadcasted_iota(jnp.int32, (t_len, t_len), 0)
    c_t = lax.broadcasted_iota(jnp.int32, (t_len, t_len), 1)
    same = (r_t // CHUNK) == (c_t // CHUNK)
    lower = jnp.where(same & (r_t >= c_t), 1.0, 0.0).astype(BF16)
    upper = jnp.where(same & (r_t <= c_t), 1.0, 0.0).astype(BF16)
    g_t = g.T[:2 * SUBLANES]
    cs_row = (_dot01_right(g_t, upper), _dot01_right(g_t, lower))
    pad_rows = jnp.zeros((LANES - 2 * SUBLANES, t_len), F32)
    cs_col = tuple(jnp.concatenate([r, pad_rows], axis=0).T for r in cs_row)

    n_rows = t_len // ROW_STRIDE
    units = [(sb, ph) for sb in range(n_slab) for ph in range(ROW_STRIDE)]
    yv = {}
    for sb, ph in units:
        lanes = slice(sb * LANES, (sb + 1) * LANES)
        acc = None
        for j in range(CONV_K):
            win = xe_ref[sb, pl.ds(SUBLANES - halo + j + ph, n_rows, stride=ROW_STRIDE), :]
            term = cw_ref[j:j + 1, lanes] * win
            acc = term if acc is None else acc + term
        yv[sb, ph] = acc
    for key in units:
        yv[key] = yv[key] * jax.nn.sigmoid(yv[key])
    for sb, ph in units:
        if sb < 2 * GDN_HEADS:
            scale = lax.rsqrt(jnp.sum(yv[sb, ph] * yv[sb, ph], axis=-1, keepdims=True) + EPS)
            if sb < GDN_HEADS:
                scale = scale * (GDN_HEAD_DIM ** -0.5)
            yv[sb, ph] = yv[sb, ph] * scale
    for sb, ph in units:
        yn_ref[sb, pl.ds(ph, n_rows, stride=ROW_STRIDE), :] = yv[sb, ph]

    qs = [yn_ref[h] for h in range(GDN_HEADS)]
    ks = [yn_ref[GDN_HEADS + h] for h in range(GDN_HEADS)]
    vs = [yn_ref[2 * GDN_HEADS + h] for h in range(GDN_HEADS)]
    kts = [kh.T for kh in ks]

    r_c = lax.broadcasted_iota(jnp.int32, (CHUNK, LANES), 0)
    lane = lax.broadcasted_iota(jnp.int32, (CHUNK, LANES), 1)
    is_fwd = lane < CHUNK
    c_c = lane % CHUNK
    eye = jnp.where(r_c == c_c, 1.0, 0.0).astype(F32)
    incl = (is_fwd & (r_c >= c_c)) | (~is_fwd & (r_c <= c_c))
    strict = (is_fwd & (r_c > c_c)) | (~is_fwd & (r_c < c_c))
    r_d = lax.broadcasted_iota(jnp.int32, (2 * CHUNK, LANES), 0)
    c_d = lax.broadcasted_iota(jnp.int32, (2 * CHUNK, LANES), 1)
    same_dir = (r_d // CHUNK) == (c_d // CHUNK)
    level_mask = lambda s_, r_, c_: ((r_ // (2 * s_)) == (c_ // (2 * s_))) & ((r_ // s_) != (c_ // s_))
    stack2 = lambda t: jnp.concatenate([t, t], axis=0)

    n_chunks = t_len // CHUNK
    rows = lambda c: slice(c * CHUNK, (c + 1) * CHUNK)
    pairs = [(c, h) for c in range(n_chunks) for h in range(GDN_HEADS)]
    qkk = {}
    for c, h in pairs:
        k16 = ks[h][rows(c)].astype(BF16)
        qk16 = jnp.concatenate([qs[h][rows(c)].astype(BF16), k16], axis=0)
        qkk[c, h] = _dot_nt(qk16, stack2(k16))

    bcast = lambda col: jnp.broadcast_to(col, (CHUNK, LANES))
    g_full, b_full, grow, glast, decay, a_mat, t_mat = {}, {}, {}, {}, {}, {}, {}
    for key in pairs:
        c, h = key
        for d in range(N_DIR):
            j = d * GDN_HEADS + h
            r_last = c * CHUNK + (CHUNK - 1 if d == 0 else 0)
            g_full[key, d] = bcast(cs_col[d][rows(c), j:j + 1])
            b_full[key, d] = bcast(beta[rows(c), SUBLANES + j:SUBLANES + j + 1])
            glast[key, d] = cs_col[d][r_last:r_last + 1, j:j + 1]
            grow[key, d] = cs_row[d][j:j + 1, rows(c)]
        gcol2 = jnp.where(is_fwd, g_full[key, 0], g_full[key, 1])
        bcol2 = jnp.where(is_fwd, b_full[key, 0], b_full[key, 1])
        grow2 = jnp.concatenate([grow[key, 0], grow[key, 1]], axis=1)
        decay[key] = jnp.exp(jnp.where(incl, gcol2 - grow2, -jnp.inf))
        a_mat[key] = jnp.where(strict, bcol2 * qkk[key][CHUNK:] * decay[key], 0.0)
        t_mat[key] = eye - jnp.where(level_mask(1, r_c, c_c), a_mat[key], 0.0)

    same_dir16 = jnp.where(same_dir, 1.0, 0.0).astype(BF16)
    block_diag = lambda t16: stack2(t16) * same_dir16
    a_bd = {key: block_diag(a_mat[key].astype(BF16)) for key in pairs}
    s = 2
    while s < CHUNK:
        lvl = level_mask(s, r_c, c_c)
        t16 = {key: t_mat[key].astype(BF16) for key in pairs}
        x_mat = {key: jnp.dot(t16[key], a_bd[key], preferred_element_type=F32) for key in pairs}
        y_mat = {key: jnp.dot(x_mat[key].astype(BF16), block_diag(t16[key]), preferred_element_type=F32)
                 for key in pairs}
        t_mat = {key: t_mat[key] - jnp.where(lvl, y_mat[key], 0.0) for key in pairs}
        s *= 2

    wu = {}
    for key in pairs:
        c, h = key
        kv = jnp.concatenate(
            [jnp.concatenate([ks[h][rows(c)] * (b_full[key, d] * jnp.exp(g_full[key, d])),
                              vs[h][rows(c)] * b_full[key, d]], axis=1) for d in range(N_DIR)], axis=0)
        t_sel = block_diag(t_mat[key].astype(BF16))
        wu[key] = jnp.dot(t_sel, kv.astype(BF16), preferred_element_type=F32)

    for c in range(n_chunks):
        egl_rows = []
        for d in range(N_DIR):
            for h in range(GDN_HEADS):
                key = (c, h)
                j = d * GDN_HEADS + h
                wu_d = wu[key][d * CHUNK:(d + 1) * CHUNK]
                w_ref[0, j, rows(c), :] = wu_d[:, :GDN_HEAD_DIM].astype(w_ref.dtype)
                u_ref[0, j, rows(c), :] = wu_d[:, GDN_HEAD_DIM:].astype(u_ref.dtype)
                qd_ref[0, j, rows(c), :] = (qs[h][rows(c)] * jnp.exp(g_full[key, d])).astype(qd_ref.dtype)
                egl_rows.append(jnp.broadcast_to(jnp.exp(glast[key, d]), (1, LANES)))
        egl_ref[0, c] = jnp.concatenate(egl_rows, axis=0)
        for h in range(GDN_HEADS):
            qk_ref[0, h, rows(c), :] = (qkk[c, h][:CHUNK] * decay[c, h]).astype(qk_ref.dtype)
    for cp in range(n_chunks // 2):
        for d in range(N_DIR):
            for h in range(GDN_HEADS):
                fac = jnp.concatenate([jnp.exp(glast[(c, h), d] - grow[(c, h), d]) for c in (2 * cp, 2 * cp + 1)],
                                      axis=1)
                kdt = kts[h][:, 2 * cp * CHUNK:(2 * cp + 2) * CHUNK] * fac
                kdt_ref[0, d * GDN_HEADS + h, cp] = kdt.astype(kdt_ref.dtype)


def _gdn_prep(qkv_g, ab, conv_w, gate_par):
    b, s, _ = qkv_g.shape
    nt = s // PREP_T
    cpb = PREP_T // CHUNK
    nc = s // CHUNK
    nh8 = PREP_T // SUBLANES
    chain = lambda last: pl.BlockSpec((1, N_CHAIN, PREP_T, last), lambda bi, i: (bi, 0, i, 0))
    return pl.pallas_call(
        _gprep_kernel,
        grid=(b, nt),
        in_specs=[pl.BlockSpec((1, SUBLANES, 3 * GDN_W), lambda bi, i: (bi, jnp.maximum(i * nh8 - 1, 0), 0)),
                  pl.BlockSpec((1, PREP_T, 3 * GDN_W), lambda bi, i: (bi, i, 0)),
                  pl.BlockSpec((1, SUBLANES, 3 * GDN_W),
                               lambda bi, i: (bi, jnp.minimum((i + 1) * nh8, s // SUBLANES - 1), 0)),
                  pl.BlockSpec((1, PREP_T, LANES), lambda bi, i: (bi, i, 0)),
                  pl.BlockSpec((SUBLANES, 3 * GDN_W), lambda bi, i: (0, 0)),
                  pl.BlockSpec((SUBLANES, LANES), lambda bi, i: (0, 0))],
        out_specs=[chain(GDN_HEAD_DIM), chain(GDN_HEAD_DIM), chain(GDN_HEAD_DIM),
                   pl.BlockSpec((1, N_CHAIN, cpb // 2, GDN_HEAD_DIM, 2 * CHUNK), lambda bi, i: (bi, 0, i, 0, 0)),
                   pl.BlockSpec((1, GDN_HEADS, PREP_T, LANES), lambda bi, i: (bi, 0, i, 0)),
                   pl.BlockSpec((1, cpb, N_CHAIN, LANES), lambda bi, i: (bi, i, 0, 0))],
        out_shape=[jax.ShapeDtypeStruct((b, N_CHAIN, s, GDN_HEAD_DIM), BF16),
                   jax.ShapeDtypeStruct((b, N_CHAIN, s, GDN_HEAD_DIM), BF16),
                   jax.ShapeDtypeStruct((b, N_CHAIN, s, GDN_HEAD_DIM), BF16),
                   jax.ShapeDtypeStruct((b, N_CHAIN, nc // 2, GDN_HEAD_DIM, 2 * CHUNK), BF16),
                   jax.ShapeDtypeStruct((b, GDN_HEADS, s, LANES), BF16),
                   jax.ShapeDtypeStruct((b, nc, N_CHAIN, LANES), F32)],
        scratch_shapes=[pltpu.VMEM((3 * GDN_W // LANES, PREP_T + 2 * SUBLANES, LANES), F32),
                        pltpu.VMEM((3 * GDN_W // LANES, PREP_T, LANES), F32)],
        compiler_params=pltpu.CompilerParams(dimension_semantics=("arbitrary", "arbitrary"),
                                             vmem_limit_bytes=VMEM_LIMIT),
        name="gdn_prep",
    )(qkv_g, qkv_g, qkv_g, ab, conv_w, gate_par)


def _gscan_kernel(egl_ref, wf_ref, wb_ref, uf_ref, ub_ref, qf_ref, qb_ref, kf_ref, kb_ref,
                  pf_ref, pb_ref, of_ref, ob_ref, state_ref):
    t = pl.program_id(0)
    n_steps = pl.num_programs(0)
    nc = n_steps * SCAN_CHUNKS
    n_batch = wf_ref.shape[0]

    @pl.when(t == 0)
    def _():
        state_ref[...] = jnp.zeros_like(state_ref)

    dirs = ((wf_ref, uf_ref, qf_ref, kf_ref, pf_ref, of_ref), (wb_ref, ub_ref, qb_ref, kb_ref, pb_ref, ob_ref))
    chains = [(bi, d, h) for bi in range(n_batch) for d in range(N_DIR) for h in range(GDN_HEADS)]
    slot = lambda bi, d, h: (bi * N_DIR + d) * GDN_HEADS + h
    st = {key: state_ref[slot(*key)] for key in chains}

    for sub in range(SCAN_CHUNKS):
        local = (sub, SCAN_CHUNKS - 1 - sub)
        rows = [slice(c * CHUNK, (c + 1) * CHUNK) for c in local]
        chunk = (t * SCAN_CHUNKS + sub, nc - 1 - (t * SCAN_CHUNKS + sub))
        r = {}
        for key in chains:
            bi, d, h = key
            w_ref, _, q_ref = dirs[d][:3]
            wq = jnp.concatenate([w_ref[bi, h, rows[d]], q_ref[bi, h, rows[d]]], axis=0)
            r[key] = _dot(wq, st[key])
        v_pad, intra = {}, {}
        for key in chains:
            bi, d, h = key
            u_ref, p_ref = dirs[d][1], dirs[d][4]
            v_new = (u_ref[bi, h, rows[d]].astype(F32) - r[key][:CHUNK]).astype(BF16)
            zeros = jnp.zeros_like(v_new)
            v_pad[key] = (jnp.concatenate([v_new, zeros], axis=0), jnp.concatenate([zeros, v_new], axis=0))
            intra[key] = jnp.dot(p_ref[bi, h, rows[d]], v_pad[key][d], preferred_element_type=F32)
        for key in chains:
            bi, d, h = key
            k_ref = dirs[d][3]
            egl = egl_ref[(bi * nc + chunk[d]) * N_CHAIN + d * GDN_HEADS + h]
            st[key] = st[key] * egl + jnp.dot(k_ref[bi, h, local[d] // 2], v_pad[key][local[d] % 2],
                                              preferred_element_type=F32)
        for bi in range(n_batch):
            for d in range(N_DIR):
                o_ref = dirs[d][5]
                o_ref[bi, rows[d], :] = jnp.concatenate(
                    [r[bi, d, h][CHUNK:] + intra[bi, d, h] for h in range(GDN_HEADS)], axis=-1).astype(o_ref.dtype)

    for key in chains:
        state_ref[slot(*key)] = st[key]


def _gdn_scan(egl, w, u, qd, kdt, qk):
    b, _, s, _ = w.shape
    rows = SCAN_CHUNKS * CHUNK
    n_steps = s // rows
    fwd_i = lambda t: t
    bwd_i = lambda t: n_steps - 1 - t
    chain = lambda d, at, last: pl.BlockSpec((b, GDN_HEADS, rows, last), lambda t: (0, d, at(t), 0))
    kspec = lambda d, at: pl.BlockSpec((b, GDN_HEADS, SCAN_CHUNKS // 2, GDN_HEAD_DIM, 2 * CHUNK),
                                       lambda t: (0, d, at(t), 0, 0))
    dk = GDN_HEAD_DIM
    return pl.pallas_call(
        _gscan_kernel,
        grid=(n_steps,),
        in_specs=[pl.BlockSpec(memory_space=pltpu.SMEM),
                  chain(0, fwd_i, dk), chain(1, bwd_i, dk), chain(0, fwd_i, dk), chain(1, bwd_i, dk),
                  chain(0, fwd_i, dk), chain(1, bwd_i, dk), kspec(0, fwd_i), kspec(1, bwd_i),
                  chain(0, fwd_i, LANES), chain(0, bwd_i, LANES)],
        out_specs=[pl.BlockSpec((b, rows, GDN_W), lambda t: (0, fwd_i(t), 0)),
                   pl.BlockSpec((b, rows, GDN_W), lambda t: (0, bwd_i(t), 0))],
        out_shape=[jax.ShapeDtypeStruct((b, s, GDN_W), BF16),
                   jax.ShapeDtypeStruct((b, s, GDN_W), BF16)],
        scratch_shapes=[pltpu.VMEM((b * N_CHAIN, GDN_HEAD_DIM, GDN_HEAD_DIM), F32)],
        compiler_params=pltpu.CompilerParams(dimension_semantics=("arbitrary",),
                                             vmem_limit_bytes=VMEM_LIMIT),
        name="gdn_scan",
    )(egl, w, w, u, u, qd, qd, kdt, kdt, qk, qk)


def _cast_weights_once(copies):
    jobs, used = [], {}
    for src, dst, stage, sem, chunk_rows in copies:
        for r0 in range(0, src.shape[0], chunk_rows):
            slot = used.get(id(stage), 0) % 2
            used[id(stage)] = used.get(id(stage), 0) + 1
            rows = pl.ds(r0, chunk_rows)
            jobs.append((pltpu.make_async_copy(src.at[rows], stage.at[slot], sem.at[slot]), stage, slot, dst, rows))
    jobs[0][0].start()
    for k, (copy, stage, slot, dst, rows) in enumerate(jobs):
        if k + 1 < len(jobs):
            jobs[k + 1][0].start()
        copy.wait()
        dst[rows, :] = stage[slot].astype(BF16)


def _ffn_kernel(x_ref, attn_ref, of_ref, ob_ref, z_ref, gnw_ref, wo_hbm, fnw_ref, w1_hbm, w2_hbm, onw_ref,
                o_ref, wo_ref, w1_ref, w2_ref, stage_wide, stage_tall, sem_wide, sem_tall):
    @pl.when(pl.program_id(0) == 0)
    def _():
        _cast_weights_once([(w1_hbm, w1_ref, stage_wide, sem_wide, FFN_CAST_ROWS_WIDE),
                            (w2_hbm, w2_ref, stage_tall, sem_tall, FFN_CAST_ROWS_TALL),
                            (wo_hbm, wo_ref, stage_tall, sem_tall, FFN_CAST_ROWS_TALL)])

    subs = [slice(r0, r0 + FFN_SUB) for r0 in range(0, FFN_TM, FFN_SUB)]
    rms = lambda t: t * lax.rsqrt(jnp.mean(t * t, axis=-1, keepdims=True) + EPS)
    hres, hn, act, acc = {}, {}, {}, {}
    for r in subs:
        o = of_ref[r, :].astype(F32) + ob_ref[r, :].astype(F32)
        heads = [rms(o[:, h * GDN_HEAD_DIM:(h + 1) * GDN_HEAD_DIM]) * gnw_ref[...] for h in range(GDN_HEADS)]
        z = z_ref[r, :]
        gdn = jnp.concatenate(heads, axis=-1) * (z * jax.nn.sigmoid(z))
        hres[r.start] = (x_ref[r, :] + jnp.dot(attn_ref[r, :], wo_ref[:ATTN_Q, :], preferred_element_type=F32)
                         + _dot(gdn, wo_ref[ATTN_Q:, :]))
    for r in subs:
        hn[r.start] = (rms(hres[r.start]) * fnw_ref[...]).astype(BF16)
        act[r.start] = jnp.dot(hn[r.start], w1_ref[...], preferred_element_type=F32)
    for r in subs:
        a = jnp.square(jnp.maximum(act[r.start], 0.0)).astype(BF16)
        acc[r.start] = hres[r.start] + jnp.dot(a, w2_ref[...], preferred_element_type=F32)
    for r in subs:
        o_ref[r, :] = rms(acc[r.start]) * onw_ref[...]


def _out_ffn(x2, attn, o_f, o_b, z, gnw, wo, fnw, w1, w2, onw):
    n = x2.shape[0]
    assert wo.shape == (D_MODEL, D_MODEL) and w1.shape == (D_MODEL, D_FF) and w2.shape == (D_FF, D_MODEL)
    row = lambda w: pl.BlockSpec((FFN_TM, w), lambda i: (i, 0))
    full = lambda a: pl.BlockSpec(a.shape, lambda i: (0, 0))
    hbm = pl.BlockSpec(memory_space=pl.ANY)
    return pl.pallas_call(
        _ffn_kernel,
        grid=(n // FFN_TM,),
        in_specs=[row(D_MODEL), row(ATTN_Q), row(GDN_W), row(GDN_W), row(GDN_W),
                  full(gnw), hbm, full(fnw), hbm, hbm, full(onw)],
        out_specs=row(D_MODEL),
        out_shape=jax.ShapeDtypeStruct((n, D_MODEL), F32),
        scratch_shapes=[pltpu.VMEM((D_MODEL, D_MODEL), BF16),
                        pltpu.VMEM((D_MODEL, D_FF), BF16),
                        pltpu.VMEM((D_FF, D_MODEL), BF16),
                        pltpu.VMEM((2, FFN_CAST_ROWS_WIDE, D_FF), F32),
                        pltpu.VMEM((2, FFN_CAST_ROWS_TALL, D_MODEL), F32),
                        pltpu.SemaphoreType.DMA((2,)),
                        pltpu.SemaphoreType.DMA((2,))],
        compiler_params=pltpu.CompilerParams(dimension_semantics=("arbitrary",),
                                             vmem_limit_bytes=VMEM_LIMIT),
        name="out_ffn",
    )(x2, attn, o_f, o_b, z, gnw, wo, fnw, w1, w2, onw)


def _layer(h, band, norm_mix_w, w_in, attn_sink, conv_w, gdn_a_log, gdn_dt_bias, gdn_norm_w,
           w_out, norm_ffn_w, w_ffn_in, w_ffn_out, out_norm_w):
    b, s, _ = h.shape
    n = b * s
    x2 = h.reshape(n, D_MODEL)
    q_a, k_a, v_a, qkv_g, z_g, ab = _proj(x2, norm_mix_w.reshape(1, D_MODEL), w_in)

    attn = _attention(q_a, k_a, v_a, band, attn_sink, b)

    conv_pad = jnp.zeros((SUBLANES, 3 * GDN_W), F32).at[:CONV_K].set(conv_w)
    gate_par = jnp.zeros((SUBLANES, LANES), F32)
    gate_par = gate_par.at[0, :N_CHAIN].set(gdn_a_log.reshape(-1)).at[1, :N_CHAIN].set(gdn_dt_bias.reshape(-1))
    w_c, u_c, q_dec, k_dec_t, qk, egl = _gdn_prep(qkv_g.reshape(b, s, 3 * GDN_W), ab.reshape(b, s, LANES),
                                                  conv_pad, gate_par)
    o_f, o_b = _gdn_scan(egl[..., 0].reshape(-1), w_c, u_c, q_dec, k_dec_t, qk)

    out = _out_ffn(x2, attn, o_f.reshape(n, GDN_W), o_b.reshape(n, GDN_W), z_g,
                   gdn_norm_w.reshape(1, GDN_HEAD_DIM), w_out, norm_ffn_w.reshape(1, D_MODEL),
                   w_ffn_in, w_ffn_out, out_norm_w.reshape(1, D_MODEL))
    return out.reshape(b, s, D_MODEL)


def kernel(x, norm_mix_w, w_in, rel_bias, attn_sink, conv_w, gdn_a_log, gdn_dt_bias, gdn_norm_w, w_out,
           norm_ffn_w, w_ffn_in, w_ffn_out, norm_final_w):
    depth = w_in.shape[0]
    assert depth == 1, "the fused output kernel applies the final norm after the single trunk layer"
    rel = (np.arange(3 * BLOCK)[None, :] - BLOCK) - np.arange(BLOCK)[:, None]
    bucket = _t5_buckets(jnp.asarray(rel, dtype=jnp.int32))
    band = _bias_band(rel_bias, bucket.T)
    return _layer(x, band, norm_mix_w[0], w_in[0], attn_sink[0], conv_w[0], gdn_a_log[0], gdn_dt_bias[0],
                  gdn_norm_w[0], w_out[0], norm_ffn_w[0], w_ffn_in[0], w_ffn_out[0], norm_final_w)
```

```python
import functools
import math

import jax
import jax.numpy as jnp
import numpy as np
from jax import lax
from jax.experimental import pallas as pl
from jax.experimental.pallas import tpu as pltpu

F32 = jnp.float32
BF16 = jnp.bfloat16

D_MODEL = 1024
ATTN_HEADS = 8
ATTN_KV_HEADS = 2
ATTN_HEAD_DIM = 64
ATTN_GROUP = ATTN_HEADS // ATTN_KV_HEADS
WINDOW = 128
BLOCK = 128
N_BUCKETS = 32
MAX_DISTANCE = 128
GDN_HEADS = 4
GDN_HEAD_DIM = 128
CONV_K = 5
CHUNK = 64
N_DIR = 2
N_CHAIN = N_DIR * GDN_HEADS
D_FF = 4 * D_MODEL
EPS = 1e-6
ATTN_Q = ATTN_HEADS * ATTN_HEAD_DIM
ATTN_KV = ATTN_KV_HEADS * ATTN_HEAD_DIM
GDN_W = GDN_HEADS * GDN_HEAD_DIM
LANES = 128
SUBLANES = 8
VMEM_LIMIT = 56 * 1024 * 1024

PROJ_TM = 1024
PROJ_SUB = 512
PROJ_CAST_COLS = 256
ATTN_QB = 4
PREP_T = 512
ROW_STRIDE = 4
SCAN_CHUNKS = 4
FFN_TM = 512
FFN_SUB = 256
FFN_CAST_ROWS_WIDE = 128
FFN_CAST_ROWS_TALL = 512


def _dot(a, b):
    return jnp.dot(a.astype(BF16), b.astype(BF16), preferred_element_type=F32)


def _dot_nt(a, b):
    return lax.dot_general(a.astype(BF16), b.astype(BF16), (((1,), (1,)), ((), ())),
                           preferred_element_type=F32)


def _split3(x):
    hi = x.astype(BF16)
    r1 = x - hi.astype(F32)
    mid = r1.astype(BF16)
    lo = (r1 - mid.astype(F32)).astype(BF16)
    return hi, mid, lo


def _dot01_right(x, m01):
    hi, mid, lo = _split3(x)
    d = lambda p: jnp.dot(p, m01, preferred_element_type=F32)
    return d(hi) + d(mid) + d(lo)


def _proj_kernel(x_ref, nw_ref, w_ref, qa_ref, ka_ref, va_ref, qkvg_ref, z_ref, ab_ref, wb_ref):
    d_in = w_ref.shape[2]
    d_main = d_in // LANES * LANES

    @pl.when(pl.program_id(0) == 0)
    def _():
        for c0 in range(0, d_main, PROJ_CAST_COLS):
            c1 = min(c0 + PROJ_CAST_COLS, d_main)
            wb_ref[:, c0:c1] = w_ref[0, :, c0:c1].astype(BF16)
        gate = jnp.concatenate([w_ref[0, :, d_main:], jnp.zeros((D_MODEL, LANES - (d_in - d_main)), F32)], axis=1)
        wb_ref[:, d_main:] = gate.astype(BF16)

    outs = ((qa_ref, ATTN_Q, ATTN_HEAD_DIM ** -0.5), (ka_ref, ATTN_KV, None), (va_ref, ATTN_KV, None),
            (qkvg_ref, 3 * GDN_W, None), (z_ref, GDN_W, None), (ab_ref, LANES, None))
    for r0 in range(0, PROJ_TM, PROJ_SUB):
        rows = slice(r0, r0 + PROJ_SUB)
        x = x_ref[rows, :]
        ms = jnp.mean(x * x, axis=-1, keepdims=True)
        xn = (x * lax.rsqrt(ms + EPS) * nw_ref[...]).astype(BF16)
        c0 = 0
        for ref, width, scale in outs:
            y = jnp.dot(xn, wb_ref[:, c0:c0 + width], preferred_element_type=F32)
            if scale is not None:
                y = y * scale
            ref[rows, :] = y.astype(ref.dtype)
            c0 += width


def _proj(x2, norm_w, w_in, layer):
    n = x2.shape[0]
    d_in = w_in.shape[2]
    d_pad = d_in // LANES * LANES + LANES
    assert d_pad == ATTN_Q + 2 * ATTN_KV + 4 * GDN_W + LANES and d_in - (d_pad - LANES) == 2 * N_CHAIN
    row = lambda w: pl.BlockSpec((PROJ_TM, w), lambda i: (i, 0))
    return pl.pallas_call(
        _proj_kernel,
        grid=(n // PROJ_TM,),
        in_specs=[row(D_MODEL),
                  pl.BlockSpec((1, D_MODEL), lambda i: (0, 0)),
                  pl.BlockSpec((1, D_MODEL, d_in), lambda i: (layer, 0, 0))],
        out_specs=[row(ATTN_Q), row(ATTN_KV), row(ATTN_KV), row(3 * GDN_W), row(GDN_W), row(LANES)],
        out_shape=[jax.ShapeDtypeStruct((n, ATTN_Q), BF16),
                   jax.ShapeDtypeStruct((n, ATTN_KV), BF16),
                   jax.ShapeDtypeStruct((n, ATTN_KV), BF16),
                   jax.ShapeDtypeStruct((n, 3 * GDN_W), F32),
                   jax.ShapeDtypeStruct((n, GDN_W), F32),
                   jax.ShapeDtypeStruct((n, LANES), F32)],
        scratch_shapes=[pltpu.VMEM((D_MODEL, d_pad), BF16)],
        compiler_params=pltpu.CompilerParams(dimension_semantics=("arbitrary",),
                                             vmem_limit_bytes=VMEM_LIMIT),
        name="proj",
    )(x2, norm_w, w_in)


def _bias_kernel(relb_ref, bucket_ref, o_ref):
    bucket = bucket_ref[...]
    key = lax.broadcasted_iota(jnp.int32, (3 * BLOCK, BLOCK), 0)
    qry = lax.broadcasted_iota(jnp.int32, (3 * BLOCK, BLOCK), 1)
    in_window = jnp.abs(key - BLOCK - qry) <= WINDOW
    for h in range(ATTN_HEADS):
        acc = jnp.zeros((3 * BLOCK, BLOCK), F32)
        for b in range(N_BUCKETS):
            acc = jnp.where(bucket == b, relb_ref[b, h], acc)
        o_ref[h // 2, :, (h % 2) * BLOCK:(h % 2 + 1) * BLOCK] = jnp.where(in_window, acc, -1e30)


def _bias_band(rel_bias, bucket_t):
    shape = (ATTN_HEADS // 2, 3 * BLOCK, 2 * BLOCK)
    return pl.pallas_call(
        _bias_kernel,
        in_specs=[pl.BlockSpec(memory_space=pltpu.SMEM),
                  pl.BlockSpec((3 * BLOCK, BLOCK), lambda: (0, 0))],
        out_specs=pl.BlockSpec(shape, lambda: (0, 0, 0)),
        out_shape=jax.ShapeDtypeStruct(shape, F32),
        name="bias_band",
    )(rel_bias, bucket_t)


def _t5_buckets(rel):
    nb = N_BUCKETS // 2
    max_exact = nb // 2
    base = jnp.where(rel > 0, nb, 0)
    n = jnp.abs(rel)
    log_ratio = jnp.log(jnp.maximum(n, 1).astype(jnp.float32) / max_exact) / math.log(MAX_DISTANCE / max_exact)
    large = jnp.minimum(max_exact + (log_ratio * (nb - max_exact)).astype(jnp.int32), nb - 1)
    return base + jnp.where(n < max_exact, n, large)


def _attn_kernel(sink_ref, q_ref, kp_ref, kc_ref, kn_ref, vp_ref, vc_ref, vn_ref, bias_ref, o_ref):
    n = pl.program_id(1)
    last = pl.num_programs(1) - 1
    kband = jnp.concatenate([kp_ref[...], kc_ref[...], kn_ref[...]], axis=0)
    vband = jnp.concatenate([vp_ref[...], vc_ref[...], vn_ref[...]], axis=0)
    vband_t = vband.astype(F32).T.astype(BF16)
    key = lax.broadcasted_iota(jnp.int32, (3 * BLOCK, 1), 0)
    first_head = lax.broadcasted_iota(jnp.int32, (1, 2 * BLOCK), 1) < BLOCK
    head = lambda t, i: t[:, i * ATTN_HEAD_DIM:(i + 1) * ATTN_HEAD_DIM]
    n_pairs = ATTN_HEADS // 2
    kv_of = lambda pr: (2 * pr) // ATTN_GROUP
    units = [(j, pr) for j in range(ATTN_QB) for pr in range(n_pairs)]
    band_rows = lambda j: slice(j * BLOCK, (j + 3) * BLOCK)

    scores = {}
    for j, pr in units:
        qj = q_ref[j * BLOCK:(j + 1) * BLOCK, :]
        q2 = jnp.concatenate([head(qj, 2 * pr), head(qj, 2 * pr + 1)], axis=0)
        scores[j, pr] = _dot_nt(head(kband, kv_of(pr))[band_rows(j)], q2)
    probs, dens = {}, {}
    for j, pr in units:
        s = scores[j, pr] + bias_ref[pr]
        if j == 0:
            s = jnp.where((key < BLOCK) & (n == 0), -1e30, s)
        if j == ATTN_QB - 1:
            s = jnp.where((key >= 2 * BLOCK) & (n == last), -1e30, s)
        sink = jnp.where(first_head, sink_ref[2 * pr], sink_ref[2 * pr + 1])
        m = jnp.maximum(jnp.max(s, axis=0, keepdims=True), sink)
        p = jnp.exp(s - m)
        dens[j, pr] = jnp.sum(p, axis=0, keepdims=True) + jnp.exp(sink - m)
        probs[j, pr] = p.astype(BF16)
    outs_t = {}
    for j, pr in units:
        kv = kv_of(pr)
        v_t = vband_t[kv * ATTN_HEAD_DIM:(kv + 1) * ATTN_HEAD_DIM, band_rows(j)]
        outs_t[j, pr] = jnp.dot(v_t, probs[j, pr], preferred_element_type=F32) / dens[j, pr]
    for j in range(ATTN_QB):
        o_t = jnp.concatenate([outs_t[j, pr][:, half * BLOCK:(half + 1) * BLOCK]
                               for pr in range(n_pairs) for half in range(2)], axis=0)
        o_ref[j * BLOCK:(j + 1) * BLOCK, :] = o_t.T.astype(o_ref.dtype)


def _attention(q_a, k_a, v_a, band, sink, batch):
    n_tok = q_a.shape[0]
    nb = n_tok // batch // BLOCK
    rows = ATTN_QB * BLOCK
    steps = nb // ATTN_QB
    kv_spec = lambda r, f: pl.BlockSpec((r, ATTN_KV), f)
    prev = lambda bi, n: (bi * nb + jnp.maximum(n * ATTN_QB - 1, 0), 0)
    cur = lambda bi, n: (bi * steps + n, 0)
    nxt = lambda bi, n: (bi * nb + jnp.minimum((n + 1) * ATTN_QB, nb - 1), 0)
    return pl.pallas_call(
        _attn_kernel,
        grid=(batch, steps),
        in_specs=[pl.BlockSpec(memory_space=pltpu.SMEM),
                  pl.BlockSpec((rows, ATTN_Q), cur),
                  kv_spec(BLOCK, prev), kv_spec(rows, cur), kv_spec(BLOCK, nxt),
                  kv_spec(BLOCK, prev), kv_spec(rows, cur), kv_spec(BLOCK, nxt),
                  pl.BlockSpec(band.shape, lambda bi, n: (0, 0, 0))],
        out_specs=pl.BlockSpec((rows, ATTN_Q), cur),
        out_shape=jax.ShapeDtypeStruct((n_tok, ATTN_Q), BF16),
        compiler_params=pltpu.CompilerParams(dimension_semantics=("arbitrary", "arbitrary"),
                                             vmem_limit_bytes=VMEM_LIMIT),
        name="attn",
    )(sink, q_a, k_a, k_a, k_a, v_a, v_a, v_a, band)


def _gprep_kernel(xp_ref, x_ref, xn_ref, ab_ref, cw_ref, gp_ref,
                  w_ref, u_ref, qd_ref, kdt_ref, qk_ref, egl_ref, xe_ref, yn_ref):
    i = pl.program_id(1)
    last = pl.num_programs(1) - 1
    t_len = PREP_T
    halo = CONV_K // 2

    n_slab = 3 * GDN_W // LANES
    for sb in range(n_slab):
        lanes = slice(sb * LANES, (sb + 1) * LANES)
        xe_ref[sb, 0:SUBLANES, :] = jnp.where(i > 0, xp_ref[0, :, lanes], 0.0)
        xe_ref[sb, SUBLANES:SUBLANES + t_len, :] = x_ref[0, :, lanes]
        xe_ref[sb, SUBLANES + t_len:, :] = jnp.where(i < last, xn_ref[0, :, lanes], 0.0)

    ab = ab_ref[0]
    sp_in = ab + gp_ref[1:2, :]
    softplus = jnp.maximum(sp_in, 0.0) + jnp.log1p(jnp.exp(-jnp.abs(sp_in)))
    g = -jnp.exp(gp_ref[0:1, :]) * softplus
    beta = jax.nn.sigmoid(ab)

    r_t = lax.broadcasted_iota(jnp.int32, (t_len, t_len), 0)
    c_t = lax.broadcasted_iota(jnp.int32, (t_len, t_len), 1)
    same = (r_t // CHUNK) == (c_t // CHUNK)
    lower = jnp.where(same & (r_t >= c_t), 1.0, 0.0).astype(BF16)
    upper = jnp.where(same & (r_t <= c_t), 1.0, 0.0).astype(BF16)
    g_t = g.T[:2 * SUBLANES]
    cs_row = (_dot01_right(g_t, upper), _dot01_right(g_t, lower))
    pad_rows = jnp.zeros((LANES - 2 * SUBLANES, t_len), F32)
    cs_col = tuple(jnp.concatenate([r, pad_rows], axis=0).T for r in cs_row)

    n_rows = t_len // ROW_STRIDE
    units = [(sb, ph) for sb in range(n_slab) for ph in range(ROW_STRIDE)]
    yv = {}
    for sb, ph in units:
        lanes = slice(sb * LANES, (sb + 1) * LANES)
        acc = None
        for j in range(CONV_K):
            win = xe_ref[sb, pl.ds(SUBLANES - halo + j + ph, n_rows, stride=ROW_STRIDE), :]
            term = cw_ref[j:j + 1, lanes] * win
            acc = term if acc is None else acc + term
        yv[sb, ph] = acc
    for key in units:
        yv[key] = yv[key] * jax.nn.sigmoid(yv[key])
    for sb, ph in units:
        if sb < 2 * GDN_HEADS:
            scale = lax.rsqrt(jnp.sum(yv[sb, ph] * yv[sb, ph], axis=-1, keepdims=True) + EPS)
            if sb < GDN_HEADS:
                scale = scale * (GDN_HEAD_DIM ** -0.5)
            yv[sb, ph] = yv[sb, ph] * scale
    for sb, ph in units:
        yn_ref[sb, pl.ds(ph, n_rows, stride=ROW_STRIDE), :] = yv[sb, ph]

    qs = [yn_ref[h] for h in range(GDN_HEADS)]
    ks = [yn_ref[GDN_HEADS + h] for h in range(GDN_HEADS)]
    vs = [yn_ref[2 * GDN_HEADS + h] for h in range(GDN_HEADS)]
    kts = [kh.T for kh in ks]

    r_c = lax.broadcasted_iota(jnp.int32, (CHUNK, LANES), 0)
    lane = lax.broadcasted_iota(jnp.int32, (CHUNK, LANES), 1)
    is_fwd = lane < CHUNK
    c_c = lane % CHUNK
    eye = jnp.where(r_c == c_c, 1.0, 0.0).astype(F32)
    incl = (is_fwd & (r_c >= c_c)) | (~is_fwd & (r_c <= c_c))
    strict = (is_fwd & (r_c > c_c)) | (~is_fwd & (r_c < c_c))
    r_d = lax.broadcasted_iota(jnp.int32, (2 * CHUNK, LANES), 0)
    c_d = lax.broadcasted_iota(jnp.int32, (2 * CHUNK, LANES), 1)
    same_dir = (r_d // CHUNK) == (c_d // CHUNK)
    level_mask = lambda s_, r_, c_: ((r_ // (2 * s_)) == (c_ // (2 * s_))) & ((r_ // s_) != (c_ // s_))
    stack2 = lambda t: jnp.concatenate([t, t], axis=0)

    n_chunks = t_len // CHUNK
    rows = lambda c: slice(c * CHUNK, (c + 1) * CHUNK)
    pairs = [(c, h) for c in range(n_chunks) for h in range(GDN_HEADS)]
    qkk = {}
    for c, h in pairs:
        k16 = ks[h][rows(c)].astype(BF16)
        qk16 = jnp.concatenate([qs[h][rows(c)].astype(BF16), k16], axis=0)
        qkk[c, h] = _dot_nt(qk16, stack2(k16))

    bcast = lambda col: jnp.broadcast_to(col, (CHUNK, LANES))
    g_full, b_full, grow, glast, decay, a_mat, t_mat = {}, {}, {}, {}, {}, {}, {}
    for key in pairs:
        c, h = key
        for d in range(N_DIR):
            j = d * GDN_HEADS + h
            r_last = c * CHUNK + (CHUNK - 1 if d == 0 else 0)
            g_full[key, d] = bcast(cs_col[d][rows(c), j:j + 1])
            b_full[key, d] = bcast(beta[rows(c), SUBLANES + j:SUBLANES + j + 1])
            glast[key, d] = cs_col[d][r_last:r_last + 1, j:j + 1]
            grow[key, d] = cs_row[d][j:j + 1, rows(c)]
        gcol2 = jnp.where(is_fwd, g_full[key, 0], g_full[key, 1])
        bcol2 = jnp.where(is_fwd, b_full[key, 0], b_full[key, 1])
        grow2 = jnp.concatenate([grow[key, 0], grow[key, 1]], axis=1)
        decay[key] = jnp.exp(jnp.where(incl, gcol2 - grow2, -jnp.inf))
        a_mat[key] = jnp.where(strict, bcol2 * qkk[key][CHUNK:] * decay[key], 0.0)
        t_mat[key] = eye - jnp.where(level_mask(1, r_c, c_c), a_mat[key], 0.0)

    same_dir16 = jnp.where(same_dir, 1.0, 0.0).astype(BF16)
    block_diag = lambda t16: stack2(t16) * same_dir16
    a_bd = {key: block_diag(a_mat[key].astype(BF16)) for key in pairs}
    s = 2
    while s < CHUNK:
        lvl = level_mask(s, r_c, c_c)
        t16 = {key: t_mat[key].astype(BF16) for key in pairs}
        x_mat = {key: jnp.dot(t16[key], a_bd[key], preferred_element_type=F32) for key in pairs}
        y_mat = {key: jnp.dot(x_mat[key].astype(BF16), block_diag(t16[key]), preferred_element_type=F32)
                 for key in pairs}
        t_mat = {key: t_mat[key] - jnp.where(lvl, y_mat[key], 0.0) for key in pairs}
        s *= 2

    wu = {}
    for key in pairs:
        c, h = key
        kv = jnp.concatenate(
            [jnp.concatenate([ks[h][rows(c)] * (b_full[key, d] * jnp.exp(g_full[key, d])),
                              vs[h][rows(c)] * b_full[key, d]], axis=1) for d in range(N_DIR)], axis=0)
        t_sel = block_diag(t_mat[key].astype(BF16))
        wu[key] = jnp.dot(t_sel, kv.astype(BF16), preferred_element_type=F32)

    for c in range(n_chunks):
        egl_rows = []
        for d in range(N_DIR):
            for h in range(GDN_HEADS):
                key = (c, h)
                j = d * GDN_HEADS + h
                wu_d = wu[key][d * CHUNK:(d + 1) * CHUNK]
                w_ref[0, j, rows(c), :] = wu_d[:, :GDN_HEAD_DIM].astype(w_ref.dtype)
                u_ref[0, j, rows(c), :] = wu_d[:, GDN_HEAD_DIM:].astype(u_ref.dtype)
                qd_ref[0, j, rows(c), :] = (qs[h][rows(c)] * jnp.exp(g_full[key, d])).astype(qd_ref.dtype)
                egl_rows.append(jnp.broadcast_to(jnp.exp(glast[key, d]), (1, LANES)))
        egl_ref[0, c] = jnp.concatenate(egl_rows, axis=0)
        for h in range(GDN_HEADS):
            qk_ref[0, h, rows(c), :] = (qkk[c, h][:CHUNK] * decay[c, h]).astype(qk_ref.dtype)
    for cp in range(n_chunks // 2):
        for d in range(N_DIR):
            for h in range(GDN_HEADS):
                fac = jnp.concatenate([jnp.exp(glast[(c, h), d] - grow[(c, h), d]) for c in (2 * cp, 2 * cp + 1)],
                                      axis=1)
                kdt = kts[h][:, 2 * cp * CHUNK:(2 * cp + 2) * CHUNK] * fac
                kdt_ref[0, d * GDN_HEADS + h, cp] = kdt.astype(kdt_ref.dtype)


def _gdn_prep(qkv_g, ab, conv_w, gate_par):
    b, s, _ = qkv_g.shape
    nt = s // PREP_T
    cpb = PREP_T // CHUNK
    nc = s // CHUNK
    nh8 = PREP_T // SUBLANES
    chain = lambda last: pl.BlockSpec((1, N_CHAIN, PREP_T, last), lambda bi, i: (bi, 0, i, 0))
    return pl.pallas_call(
        _gprep_kernel,
        grid=(b, nt),
        in_specs=[pl.BlockSpec((1, SUBLANES, 3 * GDN_W), lambda bi, i: (bi, jnp.maximum(i * nh8 - 1, 0), 0)),
                  pl.BlockSpec((1, PREP_T, 3 * GDN_W), lambda bi, i: (bi, i, 0)),
                  pl.BlockSpec((1, SUBLANES, 3 * GDN_W),
                               lambda bi, i: (bi, jnp.minimum((i + 1) * nh8, s // SUBLANES - 1), 0)),
                  pl.BlockSpec((1, PREP_T, LANES), lambda bi, i: (bi, i, 0)),
                  pl.BlockSpec((SUBLANES, 3 * GDN_W), lambda bi, i: (0, 0)),
                  pl.BlockSpec((SUBLANES, LANES), lambda bi, i: (0, 0))],
        out_specs=[chain(GDN_HEAD_DIM), chain(GDN_HEAD_DIM), chain(GDN_HEAD_DIM),
                   pl.BlockSpec((1, N_CHAIN, cpb // 2, GDN_HEAD_DIM, 2 * CHUNK), lambda bi, i: (bi, 0, i, 0, 0)),
                   pl.BlockSpec((1, GDN_HEADS, PREP_T, LANES), lambda bi, i: (bi, 0, i, 0)),
                   pl.BlockSpec((1, cpb, N_CHAIN, LANES), lambda bi, i: (bi, i, 0, 0))],
        out_shape=[jax.ShapeDtypeStruct((b, N_CHAIN, s, GDN_HEAD_DIM), BF16),
                   jax.ShapeDtypeStruct((b, N_CHAIN, s, GDN_HEAD_DIM), BF16),
                   jax.ShapeDtypeStruct((b, N_CHAIN, s, GDN_HEAD_DIM), BF16),
                   jax.ShapeDtypeStruct((b, N_CHAIN, nc // 2, GDN_HEAD_DIM, 2 * CHUNK), BF16),
                   jax.ShapeDtypeStruct((b, GDN_HEADS, s, LANES), BF16),
                   jax.ShapeDtypeStruct((b, nc, N_CHAIN, LANES), F32)],
        scratch_shapes=[pltpu.VMEM((3 * GDN_W // LANES, PREP_T + 2 * SUBLANES, LANES), F32),
                        pltpu.VMEM((3 * GDN_W // LANES, PREP_T, LANES), F32)],
        compiler_params=pltpu.CompilerParams(dimension_semantics=("arbitrary", "arbitrary"),
                                             vmem_limit_bytes=VMEM_LIMIT),
        name="gdn_prep",
    )(qkv_g, qkv_g, qkv_g, ab, conv_w, gate_par)


def _gscan_kernel(egl_ref, wf_ref, wb_ref, uf_ref, ub_ref, qf_ref, qb_ref, kf_ref, kb_ref,
                  pf_ref, pb_ref, of_ref, ob_ref, state_ref):
    t = pl.program_id(0)
    n_steps = pl.num_programs(0)
    nc = n_steps * SCAN_CHUNKS
    n_batch = wf_ref.shape[0]

    @pl.when(t == 0)
    def _():
        state_ref[...] = jnp.zeros_like(state_ref)

    dirs = ((wf_ref, uf_ref, qf_ref, kf_ref, pf_ref, of_ref), (wb_ref, ub_ref, qb_ref, kb_ref, pb_ref, ob_ref))
    chains = [(bi, d, h) for bi in range(n_batch) for d in range(N_DIR) for h in range(GDN_HEADS)]
    slot = lambda bi, d, h: (bi * N_DIR + d) * GDN_HEADS + h
    st = {key: state_ref[slot(*key)] for key in chains}

    for sub in range(SCAN_CHUNKS):
        local = (sub, SCAN_CHUNKS - 1 - sub)
        rows = [slice(c * CHUNK, (c + 1) * CHUNK) for c in local]
        chunk = (t * SCAN_CHUNKS + sub, nc - 1 - (t * SCAN_CHUNKS + sub))
        r = {}
        for key in chains:
            bi, d, h = key
            w_ref, _, q_ref = dirs[d][:3]
            wq = jnp.concatenate([w_ref[bi, h, rows[d]], q_ref[bi, h, rows[d]]], axis=0)
            r[key] = _dot(wq, st[key])
        v_pad, intra = {}, {}
        for key in chains:
            bi, d, h = key
            u_ref, p_ref = dirs[d][1], dirs[d][4]
            v_new = (u_ref[bi, h, rows[d]].astype(F32) - r[key][:CHUNK]).astype(BF16)
            zeros = jnp.zeros_like(v_new)
            v_pad[key] = (jnp.concatenate([v_new, zeros], axis=0), jnp.concatenate([zeros, v_new], axis=0))
            intra[key] = jnp.dot(p_ref[bi, h, rows[d]], v_pad[key][d], preferred_element_type=F32)
        for key in chains:
            bi, d, h = key
            k_ref = dirs[d][3]
            egl = egl_ref[(bi * nc + chunk[d]) * N_CHAIN + d * GDN_HEADS + h]
            st[key] = st[key] * egl + jnp.dot(k_ref[bi, h, local[d] // 2], v_pad[key][local[d] % 2],
                                              preferred_element_type=F32)
        for bi in range(n_batch):
            for d in range(N_DIR):
                o_ref = dirs[d][5]
                o_ref[bi, rows[d], :] = jnp.concatenate(
                    [r[bi, d, h][CHUNK:] + intra[bi, d, h] for h in range(GDN_HEADS)], axis=-1).astype(o_ref.dtype)

    for key in chains:
        state_ref[slot(*key)] = st[key]


def _gdn_scan(egl, w, u, qd, kdt, qk):
    b, _, s, _ = w.shape
    rows = SCAN_CHUNKS * CHUNK
    n_steps = s // rows
    fwd_i = lambda t: t
    bwd_i = lambda t: n_steps - 1 - t
    chain = lambda d, at, last: pl.BlockSpec((b, GDN_HEADS, rows, last), lambda t: (0, d, at(t), 0))
    kspec = lambda d, at: pl.BlockSpec((b, GDN_HEADS, SCAN_CHUNKS // 2, GDN_HEAD_DIM, 2 * CHUNK),
                                       lambda t: (0, d, at(t), 0, 0))
    dk = GDN_HEAD_DIM
    return pl.pallas_call(
        _gscan_kernel,
        grid=(n_steps,),
        in_specs=[pl.BlockSpec(memory_space=pltpu.SMEM),
                  chain(0, fwd_i, dk), chain(1, bwd_i, dk), chain(0, fwd_i, dk), chain(1, bwd_i, dk),
                  chain(0, fwd_i, dk), chain(1, bwd_i, dk), kspec(0, fwd_i), kspec(1, bwd_i),
                  chain(0, fwd_i, LANES), chain(0, bwd_i, LANES)],
        out_specs=[pl.BlockSpec((b, rows, GDN_W), lambda t: (0, fwd_i(t), 0)),
                   pl.BlockSpec((b, rows, GDN_W), lambda t: (0, bwd_i(t), 0))],
        out_shape=[jax.ShapeDtypeStruct((b, s, GDN_W), BF16),
                   jax.ShapeDtypeStruct((b, s, GDN_W), BF16)],
        scratch_shapes=[pltpu.VMEM((b * N_CHAIN, GDN_HEAD_DIM, GDN_HEAD_DIM), F32)],
        compiler_params=pltpu.CompilerParams(dimension_semantics=("arbitrary",),
                                             vmem_limit_bytes=VMEM_LIMIT),
        name="gdn_scan",
    )(egl, w, w, u, u, qd, qd, kdt, kdt, qk, qk)


def _cast_weights_once(copies):
    jobs, used = [], {}
    for src, dst, stage, sem, chunk_rows in copies:
        for r0 in range(0, src.shape[0], chunk_rows):
            slot = used.get(id(stage), 0) % 2
            used[id(stage)] = used.get(id(stage), 0) + 1
            rows = pl.ds(r0, chunk_rows)
            jobs.append((pltpu.make_async_copy(src.at[rows], stage.at[slot], sem.at[slot]), stage, slot, dst, rows))
    jobs[0][0].start()
    for k, (copy, stage, slot, dst, rows) in enumerate(jobs):
        if k + 1 < len(jobs):
            jobs[k + 1][0].start()
        copy.wait()
        dst[rows, :] = stage[slot].astype(BF16)


def _ffn_kernel(x_ref, attn_ref, of_ref, ob_ref, z_ref, gnw_ref, wo_hbm, fnw_ref, w1_hbm, w2_hbm, onw_ref,
                o_ref, wo_ref, w1_ref, w2_ref, stage_wide, stage_tall, sem_wide, sem_tall):
    @pl.when(pl.program_id(0) == 0)
    def _():
        _cast_weights_once([(w1_hbm, w1_ref, stage_wide, sem_wide, FFN_CAST_ROWS_WIDE),
                            (w2_hbm, w2_ref, stage_tall, sem_tall, FFN_CAST_ROWS_TALL),
                            (wo_hbm, wo_ref, stage_tall, sem_tall, FFN_CAST_ROWS_TALL)])

    subs = [slice(r0, r0 + FFN_SUB) for r0 in range(0, FFN_TM, FFN_SUB)]
    rms = lambda t: t * lax.rsqrt(jnp.mean(t * t, axis=-1, keepdims=True) + EPS)
    hres, hn, act, acc = {}, {}, {}, {}
    for r in subs:
        o = of_ref[r, :].astype(F32) + ob_ref[r, :].astype(F32)
        heads = [rms(o[:, h * GDN_HEAD_DIM:(h + 1) * GDN_HEAD_DIM]) * gnw_ref[...] for h in range(GDN_HEADS)]
        z = z_ref[r, :]
        gdn = jnp.concatenate(heads, axis=-1) * (z * jax.nn.sigmoid(z))
        hres[r.start] = (x_ref[r, :] + jnp.dot(attn_ref[r, :], wo_ref[:ATTN_Q, :], preferred_element_type=F32)
                         + _dot(gdn, wo_ref[ATTN_Q:, :]))
    for r in subs:
        hn[r.start] = (rms(hres[r.start]) * fnw_ref[...]).astype(BF16)
        act[r.start] = jnp.dot(hn[r.start], w1_ref[...], preferred_element_type=F32)
    for r in subs:
        a = jnp.square(jnp.maximum(act[r.start], 0.0)).astype(BF16)
        acc[r.start] = hres[r.start] + jnp.dot(a, w2_ref[...], preferred_element_type=F32)
    for r in subs:
        o_ref[r, :] = rms(acc[r.start]) * onw_ref[...]


def _out_ffn(x2, attn, o_f, o_b, z, gnw, wo, fnw, w1, w2, onw):
    n = x2.shape[0]
    assert wo.shape == (D_MODEL, D_MODEL) and w1.shape == (D_MODEL, D_FF) and w2.shape == (D_FF, D_MODEL)
    row = lambda w: pl.BlockSpec((FFN_TM, w), lambda i: (i, 0))
    full = lambda a: pl.BlockSpec(a.shape, lambda i: (0, 0))
    hbm = pl.BlockSpec(memory_space=pl.ANY)
    return pl.pallas_call(
        _ffn_kernel,
        grid=(n // FFN_TM,),
        in_specs=[row(D_MODEL), row(ATTN_Q), row(GDN_W), row(GDN_W), row(GDN_W),
                  full(gnw), hbm, full(fnw), hbm, hbm, full(onw)],
        out_specs=row(D_MODEL),
        out_shape=jax.ShapeDtypeStruct((n, D_MODEL), F32),
        scratch_shapes=[pltpu.VMEM((D_MODEL, D_MODEL), BF16),
                        pltpu.VMEM((D_MODEL, D_FF), BF16),
                        pltpu.VMEM((D_FF, D_MODEL), BF16),
                        pltpu.VMEM((2, FFN_CAST_ROWS_WIDE, D_FF), F32),
                        pltpu.VMEM((2, FFN_CAST_ROWS_TALL, D_MODEL), F32),
                        pltpu.SemaphoreType.DMA((2,)),
                        pltpu.SemaphoreType.DMA((2,))],
        compiler_params=pltpu.CompilerParams(dimension_semantics=("arbitrary",),
                                             vmem_limit_bytes=VMEM_LIMIT),
        name="out_ffn",
    )(x2, attn, o_f, o_b, z, gnw, wo, fnw, w1, w2, onw)


def _layer(h, band, norm_mix_w, w_in, layer, attn_sink, conv_w, gdn_a_log, gdn_dt_bias, gdn_norm_w,
           w_out, norm_ffn_w, w_ffn_in, w_ffn_out, out_norm_w):
    b, s, _ = h.shape
    n = b * s
    x2 = h.reshape(n, D_MODEL)
    q_a, k_a, v_a, qkv_g, z_g, ab = _proj(x2, norm_mix_w.reshape(1, D_MODEL), w_in, layer)

    attn = _attention(q_a, k_a, v_a, band, attn_sink, b)

    conv_pad = jnp.zeros((SUBLANES, 3 * GDN_W), F32).at[:CONV_K].set(conv_w)
    gate_par = jnp.zeros((SUBLANES, LANES), F32)
    gate_par = gate_par.at[0, :N_CHAIN].set(gdn_a_log.reshape(-1)).at[1, :N_CHAIN].set(gdn_dt_bias.reshape(-1))
    w_c, u_c, q_dec, k_dec_t, qk, egl = _gdn_prep(qkv_g.reshape(b, s, 3 * GDN_W), ab.reshape(b, s, LANES),
                                                  conv_pad, gate_par)
    o_f, o_b = _gdn_scan(egl[..., 0].reshape(-1), w_c, u_c, q_dec, k_dec_t, qk)

    out = _out_ffn(x2, attn, o_f.reshape(n, GDN_W), o_b.reshape(n, GDN_W), z_g,
                   gdn_norm_w.reshape(1, GDN_HEAD_DIM), w_out, norm_ffn_w.reshape(1, D_MODEL),
                   w_ffn_in, w_ffn_out, out_norm_w.reshape(1, D_MODEL))
    return out.reshape(b, s, D_MODEL)


def kernel(x, norm_mix_w, w_in, rel_bias, attn_sink, conv_w, gdn_a_log, gdn_dt_bias, gdn_norm_w, w_out,
           norm_ffn_w, w_ffn_in, w_ffn_out, norm_final_w):
    depth = w_in.shape[0]
    assert depth == 1, "the fused output kernel applies the final norm after the single trunk layer"
    rel = (np.arange(3 * BLOCK)[None, :] - BLOCK) - np.arange(BLOCK)[:, None]
    bucket = _t5_buckets(jnp.asarray(rel, dtype=jnp.int32))
    band = _bias_band(rel_bias, bucket.T)
    return _layer(x, band, norm_mix_w[0], w_in, 0, attn_sink[0], conv_w[0], gdn_a_log[0], gdn_dt_bias[0],
                  gdn_norm_w[0], w_out[0], norm_ffn_w[0], w_ffn_in[0], w_ffn_out[0], norm_final_w)
```

```python
import functools
import math

import jax
import jax.numpy as jnp
import numpy as np
from jax import lax
from jax.experimental import pallas as pl
from jax.experimental.pallas import tpu as pltpu

F32 = jnp.float32
BF16 = jnp.bfloat16

D_MODEL = 1024
ATTN_HEADS = 8
ATTN_KV_HEADS = 2
ATTN_HEAD_DIM = 64
ATTN_GROUP = ATTN_HEADS // ATTN_KV_HEADS
WINDOW = 128
BLOCK = 128
N_BUCKETS = 32
MAX_DISTANCE = 128
GDN_HEADS = 4
GDN_HEAD_DIM = 128
CONV_K = 5
CHUNK = 64
N_DIR = 2
N_CHAIN = N_DIR * GDN_HEADS
D_FF = 4 * D_MODEL
EPS = 1e-6
ATTN_Q = ATTN_HEADS * ATTN_HEAD_DIM
ATTN_KV = ATTN_KV_HEADS * ATTN_HEAD_DIM
GDN_W = GDN_HEADS * GDN_HEAD_DIM
LANES = 128
SUBLANES = 8
VMEM_LIMIT = 56 * 1024 * 1024

PROJ_TM = 1024
PROJ_SUB = 512
PROJ_CAST_COLS = 256
ATTN_QB = 4
PREP_T = 512
ROW_STRIDE = 4
SCAN_CHUNKS = 4
FFN_TM = 512
FFN_SUB = 256
FFN_CAST_ROWS_WIDE = 128
FFN_CAST_ROWS_TALL = 512


def _dot(a, b):
    return jnp.dot(a.astype(BF16), b.astype(BF16), preferred_element_type=F32)


def _dot_nt(a, b):
    return lax.dot_general(a.astype(BF16), b.astype(BF16), (((1,), (1,)), ((), ())),
                           preferred_element_type=F32)


def _split3(x):
    hi = x.astype(BF16)
    r1 = x - hi.astype(F32)
    mid = r1.astype(BF16)
    lo = (r1 - mid.astype(F32)).astype(BF16)
    return hi, mid, lo


def _dot01_right(x, m01):
    hi, mid, lo = _split3(x)
    d = lambda p: jnp.dot(p, m01, preferred_element_type=F32)
    return d(hi) + d(mid) + d(lo)


def _proj_kernel(x_ref, nw_ref, w_ref, qa_ref, ka_ref, va_ref, qkvg_ref, z_ref, ab_ref, wb_ref):
    d_in = w_ref.shape[2]
    d_main = d_in // LANES * LANES

    @pl.when(pl.program_id(0) == 0)
    def _():
        for c0 in range(0, d_main, PROJ_CAST_COLS):
            c1 = min(c0 + PROJ_CAST_COLS, d_main)
            wb_ref[:, c0:c1] = w_ref[0, :, c0:c1].astype(BF16)
        gate = jnp.concatenate([w_ref[0, :, d_main:], jnp.zeros((D_MODEL, LANES - (d_in - d_main)), F32)], axis=1)
        wb_ref[:, d_main:] = gate.astype(BF16)

    outs = ((qa_ref, ATTN_Q, ATTN_HEAD_DIM ** -0.5), (ka_ref, ATTN_KV, None), (va_ref, ATTN_KV, None),
            (qkvg_ref, 3 * GDN_W, None), (z_ref, GDN_W, None), (ab_ref, LANES, None))
    for r0 in range(0, PROJ_TM, PROJ_SUB):
        rows = slice(r0, r0 + PROJ_SUB)
        x = x_ref[rows, :]
        ms = jnp.mean(x * x, axis=-1, keepdims=True)
        xn = (x * lax.rsqrt(ms + EPS) * nw_ref[...]).astype(BF16)
        c0 = 0
        for ref, width, scale in outs:
            y = jnp.dot(xn, wb_ref[:, c0:c0 + width], preferred_element_type=F32)
            if scale is not None:
                y = y * scale
            ref[rows, :] = y.astype(ref.dtype)
            c0 += width


def _proj(x2, norm_w, w_in, layer):
    n = x2.shape[0]
    d_in = w_in.shape[2]
    d_pad = d_in // LANES * LANES + LANES
    assert d_pad == ATTN_Q + 2 * ATTN_KV + 4 * GDN_W + LANES and d_in - (d_pad - LANES) == 2 * N_CHAIN
    row = lambda w: pl.BlockSpec((PROJ_TM, w), lambda i: (i, 0))
    return pl.pallas_call(
        _proj_kernel,
        grid=(n // PROJ_TM,),
        in_specs=[row(D_MODEL),
                  pl.BlockSpec((1, D_MODEL), lambda i: (0, 0)),
                  pl.BlockSpec((1, D_MODEL, d_in), lambda i: (layer, 0, 0))],
        out_specs=[row(ATTN_Q), row(ATTN_KV), row(ATTN_KV), row(3 * GDN_W), row(GDN_W), row(LANES)],
        out_shape=[jax.ShapeDtypeStruct((n, ATTN_Q), BF16),
                   jax.ShapeDtypeStruct((n, ATTN_KV), BF16),
                   jax.ShapeDtypeStruct((n, ATTN_KV), BF16),
                   jax.ShapeDtypeStruct((n, 3 * GDN_W), F32),
                   jax.ShapeDtypeStruct((n, GDN_W), F32),
                   jax.ShapeDtypeStruct((n, LANES), F32)],
        scratch_shapes=[pltpu.VMEM((D_MODEL, d_pad), BF16)],
        compiler_params=pltpu.CompilerParams(dimension_semantics=("arbitrary",),
                                             vmem_limit_bytes=VMEM_LIMIT),
        name="proj",
    )(x2, norm_w, w_in)


def _bias_kernel(relb_ref, bucket_ref, o_ref):
    bucket = bucket_ref[...]
    key = lax.broadcasted_iota(jnp.int32, (3 * BLOCK, BLOCK), 0)
    qry = lax.broadcasted_iota(jnp.int32, (3 * BLOCK, BLOCK), 1)
    in_window = jnp.abs(key - BLOCK - qry) <= WINDOW
    for h in range(ATTN_HEADS):
        acc = jnp.zeros((3 * BLOCK, BLOCK), F32)
        for b in range(N_BUCKETS):
            acc = jnp.where(bucket == b, relb_ref[b, h], acc)
        o_ref[h // 2, :, (h % 2) * BLOCK:(h % 2 + 1) * BLOCK] = jnp.where(in_window, acc, -1e30)


def _bias_band(rel_bias, bucket_t):
    shape = (ATTN_HEADS // 2, 3 * BLOCK, 2 * BLOCK)
    return pl.pallas_call(
        _bias_kernel,
        in_specs=[pl.BlockSpec(memory_space=pltpu.SMEM),
                  pl.BlockSpec((3 * BLOCK, BLOCK), lambda: (0, 0))],
        out_specs=pl.BlockSpec(shape, lambda: (0, 0, 0)),
        out_shape=jax.ShapeDtypeStruct(shape, F32),
        name="bias_band",
    )(rel_bias, bucket_t)


def _t5_buckets(rel):
    nb = N_BUCKETS // 2
    max_exact = nb // 2
    base = jnp.where(rel > 0, nb, 0)
    n = jnp.abs(rel)
    log_ratio = jnp.log(jnp.maximum(n, 1).astype(jnp.float32) / max_exact) / math.log(MAX_DISTANCE / max_exact)
    large = jnp.minimum(max_exact + (log_ratio * (nb - max_exact)).astype(jnp.int32), nb - 1)
    return base + jnp.where(n < max_exact, n, large)


def _attn_kernel(sink_ref, q_ref, kp_ref, kc_ref, kn_ref, vp_ref, vc_ref, vn_ref, bias_ref, o_ref):
    n = pl.program_id(1)
    last = pl.num_programs(1) - 1
    kband = jnp.concatenate([kp_ref[...], kc_ref[...], kn_ref[...]], axis=0)
    vband = jnp.concatenate([vp_ref[...], vc_ref[...], vn_ref[...]], axis=0)
    vband_t = vband.astype(F32).T.astype(BF16)
    key = lax.broadcasted_iota(jnp.int32, (3 * BLOCK, 1), 0)
    first_head = lax.broadcasted_iota(jnp.int32, (1, 2 * BLOCK), 1) < BLOCK
    head = lambda t, i: t[:, i * ATTN_HEAD_DIM:(i + 1) * ATTN_HEAD_DIM]
    n_pairs = ATTN_HEADS // 2
    kv_of = lambda pr: (2 * pr) // ATTN_GROUP
    units = [(j, pr) for j in range(ATTN_QB) for pr in range(n_pairs)]
    band_rows = lambda j: slice(j * BLOCK, (j + 3) * BLOCK)

    scores = {}
    for j, pr in units:
        qj = q_ref[j * BLOCK:(j + 1) * BLOCK, :]
        q2 = jnp.concatenate([head(qj, 2 * pr), head(qj, 2 * pr + 1)], axis=0)
        scores[j, pr] = _dot_nt(head(kband, kv_of(pr))[band_rows(j)], q2)
    probs, dens = {}, {}
    for j, pr in units:
        s = scores[j, pr] + bias_ref[pr]
        if j == 0:
            s = jnp.where((key < BLOCK) & (n == 0), -1e30, s)
        if j == ATTN_QB - 1:
            s = jnp.where((key >= 2 * BLOCK) & (n == last), -1e30, s)
        sink = jnp.where(first_head, sink_ref[2 * pr], sink_ref[2 * pr + 1])
        m = jnp.maximum(jnp.max(s, axis=0, keepdims=True), sink)
        p = jnp.exp(s - m)
        dens[j, pr] = jnp.sum(p, axis=0, keepdims=True) + jnp.exp(sink - m)
        probs[j, pr] = p.astype(BF16)
    outs_t = {}
    for j, pr in units:
        kv = kv_of(pr)
        v_t = vband_t[kv * ATTN_HEAD_DIM:(kv + 1) * ATTN_HEAD_DIM, band_rows(j)]
        outs_t[j, pr] = jnp.dot(v_t, probs[j, pr], preferred_element_type=F32) / dens[j, pr]
    for j in range(ATTN_QB):
        o_t = jnp.concatenate([outs_t[j, pr][:, half * BLOCK:(half + 1) * BLOCK]
                               for pr in range(n_pairs) for half in range(2)], axis=0)
        o_ref[j * BLOCK:(j + 1) * BLOCK, :] = o_t.T.astype(o_ref.dtype)


def _attention(q_a, k_a, v_a, band, sink, batch):
    n_tok = q_a.shape[0]
    nb = n_tok // batch // BLOCK
    rows = ATTN_QB * BLOCK
    steps = nb // ATTN_QB
    kv_spec = lambda r, f: pl.BlockSpec((r, ATTN_KV), f)
    prev = lambda bi, n: (bi * nb + jnp.maximum(n * ATTN_QB - 1, 0), 0)
    cur = lambda bi, n: (bi * steps + n, 0)
    nxt = lambda bi, n: (bi * nb + jnp.minimum((n + 1) * ATTN_QB, nb - 1), 0)
    return pl.pallas_call(
        _attn_kernel,
        grid=(batch, steps),
        in_specs=[pl.BlockSpec(memory_space=pltpu.SMEM),
                  pl.BlockSpec((rows, ATTN_Q), cur),
                  kv_spec(BLOCK, prev), kv_spec(rows, cur), kv_spec(BLOCK, nxt),
                  kv_spec(BLOCK, prev), kv_spec(rows, cur), kv_spec(BLOCK, nxt),
                  pl.BlockSpec(band.shape, lambda bi, n: (0, 0, 0))],
        out_specs=pl.BlockSpec((rows, ATTN_Q), cur),
        out_shape=jax.ShapeDtypeStruct((n_tok, ATTN_Q), BF16),
        compiler_params=pltpu.CompilerParams(dimension_semantics=("arbitrary", "arbitrary"),
                                             vmem_limit_bytes=VMEM_LIMIT),
        name="attn",
    )(sink, q_a, k_a, k_a, k_a, v_a, v_a, v_a, band)


def _gprep_kernel(xp_ref, x_ref, xn_ref, ab_ref, cw_ref, gp_ref,
                  w_ref, u_ref, qd_ref, kdt_ref, qk_ref, egl_ref, xe_ref, yn_ref):
    i = pl.program_id(1)
    last = pl.num_programs(1) - 1
    t_len = PREP_T
    halo = CONV_K // 2

    n_slab = 3 * GDN_W // LANES
    for sb in range(n_slab):
        lanes = slice(sb * LANES, (sb + 1) * LANES)
        xe_ref[sb, 0:SUBLANES, :] = jnp.where(i > 0, xp_ref[0, :, lanes], 0.0)
        xe_ref[sb, SUBLANES:SUBLANES + t_len, :] = x_ref[0, :, lanes]
        xe_ref[sb, SUBLANES + t_len:, :] = jnp.where(i < last, xn_ref[0, :, lanes], 0.0)

    ab = ab_ref[0]
    sp_in = ab + gp_ref[1:2, :]
    softplus = jnp.maximum(sp_in, 0.0) + jnp.log1p(jnp.exp(-jnp.abs(sp_in)))
    g = -jnp.exp(gp_ref[0:1, :]) * softplus
    beta = jax.nn.sigmoid(ab)

    r_t = lax.broadcasted_iota(jnp.int32, (t_len, t_len), 0)
    c_t = lax.broadcasted_iota(jnp.int32, (t_len, t_len), 1)
    same = (r_t // CHUNK) == (c_t // CHUNK)
    lower = jnp.where(same & (r_t >= c_t), 1.0, 0.0).astype(BF16)
    upper = jnp.where(same & (r_t <= c_t), 1.0, 0.0).astype(BF16)
    g_t = g.T[:2 * SUBLANES]
    cs_row = (_dot01_right(g_t, upper), _dot01_right(g_t, lower))
    pad_rows = jnp.zeros((LANES - 2 * SUBLANES, t_len), F32)
    cs_col = tuple(jnp.concatenate([r, pad_rows], axis=0).T for r in cs_row)

    n_rows = t_len // ROW_STRIDE
    units = [(sb, ph) for sb in range(n_slab) for ph in range(ROW_STRIDE)]
    yv = {}
    for sb, ph in units:
        lanes = slice(sb * LANES, (sb + 1) * LANES)
        acc = None
        for j in range(CONV_K):
            win = xe_ref[sb, pl.ds(SUBLANES - halo + j + ph, n_rows, stride=ROW_STRIDE), :]
            term = cw_ref[j:j + 1, lanes] * win
            acc = term if acc is None else acc + term
        yv[sb, ph] = acc
    for key in units:
        yv[key] = yv[key] * jax.nn.sigmoid(yv[key])
    for sb, ph in units:
        if sb < 2 * GDN_HEADS:
            scale = lax.rsqrt(jnp.sum(yv[sb, ph] * yv[sb, ph], axis=-1, keepdims=True) + EPS)
            if sb < GDN_HEADS:
                scale = scale * (GDN_HEAD_DIM ** -0.5)
            yv[sb, ph] = yv[sb, ph] * scale
    for sb, ph in units:
        yn_ref[sb, pl.ds(ph, n_rows, stride=ROW_STRIDE), :] = yv[sb, ph]

    qs = [yn_ref[h] for h in range(GDN_HEADS)]
    ks = [yn_ref[GDN_HEADS + h] for h in range(GDN_HEADS)]
    vs = [yn_ref[2 * GDN_HEADS + h] for h in range(GDN_HEADS)]
    kts = [kh.T for kh in ks]

    r_c = lax.broadcasted_iota(jnp.int32, (CHUNK, LANES), 0)
    lane = lax.broadcasted_iota(jnp.int32, (CHUNK, LANES), 1)
    is_fwd = lane < CHUNK
    c_c = lane % CHUNK
    eye = jnp.where(r_c == c_c, 1.0, 0.0).astype(F32)
    incl = (is_fwd & (r_c >= c_c)) | (~is_fwd & (r_c <= c_c))
    strict = (is_fwd & (r_c > c_c)) | (~is_fwd & (r_c < c_c))
    r_d = lax.broadcasted_iota(jnp.int32, (2 * CHUNK, LANES), 0)
    c_d = lax.broadcasted_iota(jnp.int32, (2 * CHUNK, LANES), 1)
    same_dir = (r_d // CHUNK) == (c_d // CHUNK)
    level_mask = lambda s_, r_, c_: ((r_ // (2 * s_)) == (c_ // (2 * s_))) & ((r_ // s_) != (c_ // s_))
    stack2 = lambda t: jnp.concatenate([t, t], axis=0)

    n_chunks = t_len // CHUNK
    rows = lambda c: slice(c * CHUNK, (c + 1) * CHUNK)
    pairs = [(c, h) for c in range(n_chunks) for h in range(GDN_HEADS)]
    qkk = {}
    for c, h in pairs:
        k16 = ks[h][rows(c)].astype(BF16)
        qk16 = jnp.concatenate([qs[h][rows(c)].astype(BF16), k16], axis=0)
        qkk[c, h] = _dot_nt(qk16, stack2(k16))

    bcast = lambda col: jnp.broadcast_to(col, (CHUNK, LANES))
    g_full, b_full, grow, glast, decay, a_mat, t_mat = {}, {}, {}, {}, {}, {}, {}
    for key in pairs:
        c, h = key
        for d in range(N_DIR):
            j = d * GDN_HEADS + h
            r_last = c * CHUNK + (CHUNK - 1 if d == 0 else 0)
            g_full[key, d] = bcast(cs_col[d][rows(c), j:j + 1])
            b_full[key, d] = bcast(beta[rows(c), SUBLANES + j:SUBLANES + j + 1])
            glast[key, d] = cs_col[d][r_last:r_last + 1, j:j + 1]
            grow[key, d] = cs_row[d][j:j + 1, rows(c)]
        gcol2 = jnp.where(is_fwd, g_full[key, 0], g_full[key, 1])
        bcol2 = jnp.where(is_fwd, b_full[key, 0], b_full[key, 1])
        grow2 = jnp.concatenate([grow[key, 0], grow[key, 1]], axis=1)
        decay[key] = jnp.exp(jnp.where(incl, gcol2 - grow2, -jnp.inf))
        a_mat[key] = jnp.where(strict, bcol2 * qkk[key][CHUNK:] * decay[key], 0.0)
        t_mat[key] = eye - jnp.where(level_mask(1, r_c, c_c), a_mat[key], 0.0)

    same_dir16 = jnp.where(same_dir, 1.0, 0.0).astype(BF16)
    block_diag = lambda t16: stack2(t16) * same_dir16
    a_bd = {key: block_diag(a_mat[key].astype(BF16)) for key in pairs}
    s = 2
    while s < CHUNK:
        lvl = level_mask(s, r_c, c_c)
        t16 = {key: t_mat[key].astype(BF16) for key in pairs}
        x_mat = {key: jnp.dot(t16[key], a_bd[key], preferred_element_type=F32) for key in pairs}
        y_mat = {key: jnp.dot(x_mat[key].astype(BF16), block_diag(t16[key]), preferred_element_type=F32)
                 for key in pairs}
        t_mat = {key: t_mat[key] - jnp.where(lvl, y_mat[key], 0.0) for key in pairs}
        s *= 2

    wu = {}
    for key in pairs:
        c, h = key
        kv = jnp.concatenate(
            [jnp.concatenate([ks[h][rows(c)] * (b_full[key, d] * jnp.exp(g_full[key, d])),
                              vs[h][rows(c)] * b_full[key, d]], axis=1) for d in range(N_DIR)], axis=0)
        t_sel = block_diag(t_mat[key].astype(BF16))
        wu[key] = jnp.dot(t_sel, kv.astype(BF16), preferred_element_type=F32)

    for c in range(n_chunks):
        egl_rows = []
        for d in range(N_DIR):
            for h in range(GDN_HEADS):
                key = (c, h)
                j = d * GDN_HEADS + h
                wu_d = wu[key][d * CHUNK:(d + 1) * CHUNK]
                w_ref[0, j, rows(c), :] = wu_d[:, :GDN_HEAD_DIM].astype(w_ref.dtype)
                u_ref[0, j, rows(c), :] = wu_d[:, GDN_HEAD_DIM:].astype(u_ref.dtype)
                qd_ref[0, j, rows(c), :] = (qs[h][rows(c)] * jnp.exp(g_full[key, d])).astype(qd_ref.dtype)
                egl_rows.append(jnp.broadcast_to(jnp.exp(glast[key, d]), (1, LANES)))
        egl_ref[0, c] = jnp.concatenate(egl_rows, axis=0)
        for h in range(GDN_HEADS):
            qk_ref[0, h, rows(c), :] = (qkk[c, h][:CHUNK] * decay[c, h]).astype(qk_ref.dtype)
    for cp in range(n_chunks // 2):
        for d in range(N_DIR):
            for h in range(GDN_HEADS):
                fac = jnp.concatenate([jnp.exp(glast[(c, h), d] - grow[(c, h), d]) for c in (2 * cp, 2 * cp + 1)],
                                      axis=1)
                kdt = kts[h][:, 2 * cp * CHUNK:(2 * cp + 2) * CHUNK] * fac
                kdt_ref[0, d * GDN_HEADS + h, cp] = kdt.astype(kdt_ref.dtype)


def _gdn_prep(qkv_g, ab, conv_w, gate_par):
    b, s, _ = qkv_g.shape
    nt = s // PREP_T
    cpb = PREP_T // CHUNK
    nc = s // CHUNK
    nh8 = PREP_T // SUBLANES
    chain = lambda last: pl.BlockSpec((1, N_CHAIN, PREP_T, last), lambda bi, i: (bi, 0, i, 0))
    return pl.pallas_call(
        _gprep_kernel,
        grid=(b, nt),
        in_specs=[pl.BlockSpec((1, SUBLANES, 3 * GDN_W), lambda bi, i: (bi, jnp.maximum(i * nh8 - 1, 0), 0)),
                  pl.BlockSpec((1, PREP_T, 3 * GDN_W), lambda bi, i: (bi, i, 0)),
                  pl.BlockSpec((1, SUBLANES, 3 * GDN_W),
                               lambda bi, i: (bi, jnp.minimum((i + 1) * nh8, s // SUBLANES - 1), 0)),
                  pl.BlockSpec((1, PREP_T, LANES), lambda bi, i: (bi, i, 0)),
                  pl.BlockSpec((SUBLANES, 3 * GDN_W), lambda bi, i: (0, 0)),
                  pl.BlockSpec((SUBLANES, LANES), lambda bi, i: (0, 0))],
        out_specs=[chain(GDN_HEAD_DIM), chain(GDN_HEAD_DIM), chain(GDN_HEAD_DIM),
                   pl.BlockSpec((1, N_CHAIN, cpb // 2, GDN_HEAD_DIM, 2 * CHUNK), lambda bi, i: (bi, 0, i, 0, 0)),
                   pl.BlockSpec((1, GDN_HEADS, PREP_T, LANES), lambda bi, i: (bi, 0, i, 0)),
                   pl.BlockSpec((1, cpb, N_CHAIN, LANES), lambda bi, i: (bi, i, 0, 0))],
        out_shape=[jax.ShapeDtypeStruct((b, N_CHAIN, s, GDN_HEAD_DIM), BF16),
                   jax.ShapeDtypeStruct((b, N_CHAIN, s, GDN_HEAD_DIM), BF16),
                   jax.ShapeDtypeStruct((b, N_CHAIN, s, GDN_HEAD_DIM), BF16),
                   jax.ShapeDtypeStruct((b, N_CHAIN, nc // 2, GDN_HEAD_DIM, 2 * CHUNK), BF16),
                   jax.ShapeDtypeStruct((b, GDN_HEADS, s, LANES), BF16),
                   jax.ShapeDtypeStruct((b, nc, N_CHAIN, LANES), F32)],
        scratch_shapes=[pltpu.VMEM((3 * GDN_W // LANES, PREP_T + 2 * SUBLANES, LANES), F32),
                        pltpu.VMEM((3 * GDN_W // LANES, PREP_T, LANES), F32)],
        compiler_params=pltpu.CompilerParams(dimension_semantics=("arbitrary", "arbitrary"),
                                             vmem_limit_bytes=VMEM_LIMIT),
        name="gdn_prep",
    )(qkv_g, qkv_g, qkv_g, ab, conv_w, gate_par)


def _gscan_kernel(egl_ref, wf_ref, wb_ref, uf_ref, ub_ref, qf_ref, qb_ref, kf_ref, kb_ref,
                  pf_ref, pb_ref, of_ref, ob_ref, state_ref):
    t = pl.program_id(0)
    n_steps = pl.num_programs(0)
    nc = n_steps * SCAN_CHUNKS
    n_batch = wf_ref.shape[0]

    @pl.when(t == 0)
    def _():
        state_ref[...] = jnp.zeros_like(state_ref)

    dirs = ((wf_ref, uf_ref, qf_ref, kf_ref, pf_ref, of_ref), (wb_ref, ub_ref, qb_ref, kb_ref, pb_ref, ob_ref))
    chains = [(bi, d, h) for bi in range(n_batch) for d in range(N_DIR) for h in range(GDN_HEADS)]
    slot = lambda bi, d, h: (bi * N_DIR + d) * GDN_HEADS + h
    st = {key: state_ref[slot(*key)] for key in chains}

    for sub in range(SCAN_CHUNKS):
        local = (sub, SCAN_CHUNKS - 1 - sub)
        rows = [slice(c * CHUNK, (c + 1) * CHUNK) for c in local]
        chunk = (t * SCAN_CHUNKS + sub, nc - 1 - (t * SCAN_CHUNKS + sub))
        r = {}
        for key in chains:
            bi, d, h = key
            w_ref, _, q_ref = dirs[d][:3]
            wq = jnp.concatenate([w_ref[bi, h, rows[d]], q_ref[bi, h, rows[d]]], axis=0)
            r[key] = _dot(wq, st[key])
        v_pad, intra = {}, {}
        for key in chains:
            bi, d, h = key
            u_ref, p_ref = dirs[d][1], dirs[d][4]
            v_new = (u_ref[bi, h, rows[d]].astype(F32) - r[key][:CHUNK]).astype(BF16)
            zeros = jnp.zeros_like(v_new)
            v_pad[key] = (jnp.concatenate([v_new, zeros], axis=0), jnp.concatenate([zeros, v_new], axis=0))
            intra[key] = jnp.dot(p_ref[bi, h, rows[d]], v_pad[key][d], preferred_element_type=F32)
        for key in chains:
            bi, d, h = key
            k_ref = dirs[d][3]
            egl = egl_ref[(bi * nc + chunk[d]) * N_CHAIN + d * GDN_HEADS + h]
            st[key] = st[key] * egl + jnp.dot(k_ref[bi, h, local[d] // 2], v_pad[key][local[d] % 2],
                                              preferred_element_type=F32)
        for bi in range(n_batch):
            for d in range(N_DIR):
                o_ref = dirs[d][5]
                o_ref[bi, rows[d], :] = jnp.concatenate(
                    [r[bi, d, h][CHUNK:] + intra[bi, d, h] for h in range(GDN_HEADS)], axis=-1).astype(o_ref.dtype)

    for key in chains:
        state_ref[slot(*key)] = st[key]


def _gdn_scan(egl, w, u, qd, kdt, qk):
    b, _, s, _ = w.shape
    rows = SCAN_CHUNKS * CHUNK
    n_steps = s // rows
    fwd_i = lambda t: t
    bwd_i = lambda t: n_steps - 1 - t
    chain = lambda d, at, last: pl.BlockSpec((b, GDN_HEADS, rows, last), lambda t: (0, d, at(t), 0))
    kspec = lambda d, at: pl.BlockSpec((b, GDN_HEADS, SCAN_CHUNKS // 2, GDN_HEAD_DIM, 2 * CHUNK),
                                       lambda t: (0, d, at(t), 0, 0))
    dk = GDN_HEAD_DIM
    return pl.pallas_call(
        _gscan_kernel,
        grid=(n_steps,),
        in_specs=[pl.BlockSpec(memory_space=pltpu.SMEM),
                  chain(0, fwd_i, dk), chain(1, bwd_i, dk), chain(0, fwd_i, dk), chain(1, bwd_i, dk),
                  chain(0, fwd_i, dk), chain(1, bwd_i, dk), kspec(0, fwd_i), kspec(1, bwd_i),
                  chain(0, fwd_i, LANES), chain(0, bwd_i, LANES)],
        out_specs=[pl.BlockSpec((b, rows, GDN_W), lambda t: (0, fwd_i(t), 0)),
                   pl.BlockSpec((b, rows, GDN_W), lambda t: (0, bwd_i(t), 0))],
        out_shape=[jax.ShapeDtypeStruct((b, s, GDN_W), BF16),
                   jax.ShapeDtypeStruct((b, s, GDN_W), BF16)],
        scratch_shapes=[pltpu.VMEM((b * N_CHAIN, GDN_HEAD_DIM, GDN_HEAD_DIM), F32)],
        compiler_params=pltpu.CompilerParams(dimension_semantics=("arbitrary",),
                                             vmem_limit_bytes=VMEM_LIMIT),
        name="gdn_scan",
    )(egl, w, w, u, u, qd, qd, kdt, kdt, qk, qk)


def _cast_weights_once(layer, copies):
    jobs, used = [], {}
    for src, dst, stage, sem, chunk_rows in copies:
        for r0 in range(0, src.shape[1], chunk_rows):
            slot = used.get(id(stage), 0) % 2
            used[id(stage)] = used.get(id(stage), 0) + 1
            rows = pl.ds(r0, chunk_rows)
            jobs.append((pltpu.make_async_copy(src.at[layer, rows], stage.at[slot], sem.at[slot]),
                         stage, slot, dst, rows))
    jobs[0][0].start()
    for k, (copy, stage, slot, dst, rows) in enumerate(jobs):
        if k + 1 < len(jobs):
            jobs[k + 1][0].start()
        copy.wait()
        dst[rows, :] = stage[slot].astype(BF16)


def _ffn_kernel(layer, x_ref, attn_ref, of_ref, ob_ref, z_ref, gnw_ref, wo_hbm, fnw_ref, w1_hbm, w2_hbm, onw_ref,
                o_ref, wo_ref, w1_ref, w2_ref, stage_wide, stage_tall, sem_wide, sem_tall):
    @pl.when(pl.program_id(0) == 0)
    def _():
        _cast_weights_once(layer, [(w1_hbm, w1_ref, stage_wide, sem_wide, FFN_CAST_ROWS_WIDE),
                                   (w2_hbm, w2_ref, stage_tall, sem_tall, FFN_CAST_ROWS_TALL),
                                   (wo_hbm, wo_ref, stage_tall, sem_tall, FFN_CAST_ROWS_TALL)])

    subs = [slice(r0, r0 + FFN_SUB) for r0 in range(0, FFN_TM, FFN_SUB)]
    rms = lambda t: t * lax.rsqrt(jnp.mean(t * t, axis=-1, keepdims=True) + EPS)
    hres, hn, act, acc = {}, {}, {}, {}
    for r in subs:
        o = of_ref[r, :].astype(F32) + ob_ref[r, :].astype(F32)
        heads = [rms(o[:, h * GDN_HEAD_DIM:(h + 1) * GDN_HEAD_DIM]) * gnw_ref[...] for h in range(GDN_HEADS)]
        z = z_ref[r, :]
        gdn = jnp.concatenate(heads, axis=-1) * (z * jax.nn.sigmoid(z))
        hres[r.start] = (x_ref[r, :] + jnp.dot(attn_ref[r, :], wo_ref[:ATTN_Q, :], preferred_element_type=F32)
                         + _dot(gdn, wo_ref[ATTN_Q:, :]))
    for r in subs:
        hn[r.start] = (rms(hres[r.start]) * fnw_ref[...]).astype(BF16)
        act[r.start] = jnp.dot(hn[r.start], w1_ref[...], preferred_element_type=F32)
    for r in subs:
        a = jnp.square(jnp.maximum(act[r.start], 0.0)).astype(BF16)
        acc[r.start] = hres[r.start] + jnp.dot(a, w2_ref[...], preferred_element_type=F32)
    for r in subs:
        o_ref[r, :] = rms(acc[r.start]) * onw_ref[...]


def _out_ffn(x2, attn, o_f, o_b, z, gnw, wo, fnw, w1, w2, onw, layer):
    n = x2.shape[0]
    assert wo.shape[1:] == (D_MODEL, D_MODEL) and w1.shape[1:] == (D_MODEL, D_FF) and w2.shape[1:] == (D_FF, D_MODEL)
    row = lambda w: pl.BlockSpec((FFN_TM, w), lambda i: (i, 0))
    full = lambda a: pl.BlockSpec(a.shape, lambda i: (0, 0))
    hbm = pl.BlockSpec(memory_space=pl.ANY)
    return pl.pallas_call(
        functools.partial(_ffn_kernel, layer),
        grid=(n // FFN_TM,),
        in_specs=[row(D_MODEL), row(ATTN_Q), row(GDN_W), row(GDN_W), row(GDN_W),
                  full(gnw), hbm, full(fnw), hbm, hbm, full(onw)],
        out_specs=row(D_MODEL),
        out_shape=jax.ShapeDtypeStruct((n, D_MODEL), F32),
        scratch_shapes=[pltpu.VMEM((D_MODEL, D_MODEL), BF16),
                        pltpu.VMEM((D_MODEL, D_FF), BF16),
                        pltpu.VMEM((D_FF, D_MODEL), BF16),
                        pltpu.VMEM((2, FFN_CAST_ROWS_WIDE, D_FF), F32),
                        pltpu.VMEM((2, FFN_CAST_ROWS_TALL, D_MODEL), F32),
                        pltpu.SemaphoreType.DMA((2,)),
                        pltpu.SemaphoreType.DMA((2,))],
        compiler_params=pltpu.CompilerParams(dimension_semantics=("arbitrary",),
                                             vmem_limit_bytes=VMEM_LIMIT),
        name="out_ffn",
    )(x2, attn, o_f, o_b, z, gnw, wo, fnw, w1, w2, onw)


def _layer(h, band, norm_mix_w, w_in, layer, attn_sink, conv_w, gdn_a_log, gdn_dt_bias, gdn_norm_w,
           w_out, norm_ffn_w, w_ffn_in, w_ffn_out, out_norm_w):
    b, s, _ = h.shape
    n = b * s
    x2 = h.reshape(n, D_MODEL)
    q_a, k_a, v_a, qkv_g, z_g, ab = _proj(x2, norm_mix_w.reshape(1, D_MODEL), w_in, layer)

    attn = _attention(q_a, k_a, v_a, band, attn_sink, b)

    conv_pad = jnp.zeros((SUBLANES, 3 * GDN_W), F32).at[:CONV_K].set(conv_w)
    gate_par = jnp.zeros((SUBLANES, LANES), F32)
    gate_par = gate_par.at[0, :N_CHAIN].set(gdn_a_log.reshape(-1)).at[1, :N_CHAIN].set(gdn_dt_bias.reshape(-1))
    w_c, u_c, q_dec, k_dec_t, qk, egl = _gdn_prep(qkv_g.reshape(b, s, 3 * GDN_W), ab.reshape(b, s, LANES),
                                                  conv_pad, gate_par)
    o_f, o_b = _gdn_scan(egl[..., 0].reshape(-1), w_c, u_c, q_dec, k_dec_t, qk)

    out = _out_ffn(x2, attn, o_f.reshape(n, GDN_W), o_b.reshape(n, GDN_W), z_g,
                   gdn_norm_w.reshape(1, GDN_HEAD_DIM), w_out, norm_ffn_w.reshape(1, D_MODEL),
                   w_ffn_in, w_ffn_out, out_norm_w.reshape(1, D_MODEL), layer)
    return out.reshape(b, s, D_MODEL)


def kernel(x, norm_mix_w, w_in, rel_bias, attn_sink, conv_w, gdn_a_log, gdn_dt_bias, gdn_norm_w, w_out,
           norm_ffn_w, w_ffn_in, w_ffn_out, norm_final_w):
    depth = w_in.shape[0]
    assert depth == 1, "the fused output kernel applies the final norm after the single trunk layer"
    rel = (np.arange(3 * BLOCK)[None, :] - BLOCK) - np.arange(BLOCK)[:, None]
    bucket = _t5_buckets(jnp.asarray(rel, dtype=jnp.int32))
    band = _bias_band(rel_bias, bucket.T)
    return _layer(x, band, norm_mix_w[0], w_in, 0, attn_sink[0], conv_w[0], gdn_a_log[0], gdn_dt_bias[0],
                  gdn_norm_w[0], w_out, norm_ffn_w[0], w_ffn_in, w_ffn_out, norm_final_w)
```

```python
import functools
import math

import jax
import jax.numpy as jnp
import numpy as np
from jax import lax
from jax.experimental import pallas as pl
from jax.experimental.pallas import tpu as pltpu

F32 = jnp.float32
BF16 = jnp.bfloat16

D_MODEL = 1024
ATTN_HEADS = 8
ATTN_KV_HEADS = 2
ATTN_HEAD_DIM = 64
ATTN_GROUP = ATTN_HEADS // ATTN_KV_HEADS
WINDOW = 128
BLOCK = 128
N_BUCKETS = 32
MAX_DISTANCE = 128
GDN_HEADS = 4
GDN_HEAD_DIM = 128
CONV_K = 5
CHUNK = 64
N_DIR = 2
N_CHAIN = N_DIR * GDN_HEADS
D_FF = 4 * D_MODEL
EPS = 1e-6
ATTN_Q = ATTN_HEADS * ATTN_HEAD_DIM
ATTN_KV = ATTN_KV_HEADS * ATTN_HEAD_DIM
GDN_W = GDN_HEADS * GDN_HEAD_DIM
LANES = 128
SUBLANES = 8
VMEM_LIMIT = 56 * 1024 * 1024

PROJ_TM = 1024
PROJ_SUB = 512
PROJ_CAST_ROWS = 128
ATTN_QB = 4
PREP_T = 512
ROW_STRIDE = 4
SCAN_CHUNKS = 4
FFN_TM = 512
FFN_SUB = 256
FFN_CAST_ROWS_WIDE = 128
FFN_CAST_ROWS_TALL = 512


def _dot(a, b):
    return jnp.dot(a.astype(BF16), b.astype(BF16), preferred_element_type=F32)


def _dot_nt(a, b):
    return lax.dot_general(a.astype(BF16), b.astype(BF16), (((1,), (1,)), ((), ())),
                           preferred_element_type=F32)


def _split3(x):
    hi = x.astype(BF16)
    r1 = x - hi.astype(F32)
    mid = r1.astype(BF16)
    lo = (r1 - mid.astype(F32)).astype(BF16)
    return hi, mid, lo


def _dot01_right(x, m01):
    hi, mid, lo = _split3(x)
    d = lambda p: jnp.dot(p, m01, preferred_element_type=F32)
    return d(hi) + d(mid) + d(lo)


def _conv_silu_norm(xe_ref, cw_ref, yn_ref, r0, n_out):
    halo = CONV_K // 2
    n_rows = n_out // ROW_STRIDE
    n_slab = 3 * GDN_W // LANES
    units = [(sb, ph) for sb in range(n_slab) for ph in range(ROW_STRIDE)]
    yv = {}
    for sb, ph in units:
        lanes = slice(sb * LANES, (sb + 1) * LANES)
        acc = None
        for j in range(CONV_K):
            win = xe_ref[sb, pl.ds(SUBLANES + r0 - halo + j + ph, n_rows, stride=ROW_STRIDE), :]
            term = cw_ref[j:j + 1, lanes] * win
            acc = term if acc is None else acc + term
        yv[sb, ph] = acc
    for key in units:
        yv[key] = yv[key] * jax.nn.sigmoid(yv[key])
    for sb, ph in units:
        if sb < 2 * GDN_HEADS:
            scale = lax.rsqrt(jnp.sum(yv[sb, ph] * yv[sb, ph], axis=-1, keepdims=True) + EPS)
            if sb < GDN_HEADS:
                scale = scale * (GDN_HEAD_DIM ** -0.5)
            yv[sb, ph] = yv[sb, ph] * scale
    for sb, ph in units:
        yn_ref[sb, pl.ds(r0 + ph, n_rows, stride=ROW_STRIDE), :] = yv[sb, ph]


def _proj_kernel(layer, tiles_per_seq, x_ref, xp_ref, xn_ref, nw_ref, wt_hbm, cw_ref,
                 qa_ref, ka_ref, va_ref, yn_ref, z_ref, ab_ref, wb_ref, xe_ref, stage_ref, sem):
    d_in = wt_hbm.shape[1]
    d_main = d_in // LANES * LANES
    step = pl.program_id(0)

    @pl.when(step == 0)
    def _():
        starts = list(range(0, d_main, PROJ_CAST_ROWS)) + [d_main]
        sizes = [PROJ_CAST_ROWS] * (len(starts) - 1) + [d_in - d_main]
        copies = [pltpu.make_async_copy(wt_hbm.at[layer, pl.ds(c0, nr)], stage_ref.at[k % 2, pl.ds(0, nr)],
                                        sem.at[k % 2]) for k, (c0, nr) in enumerate(zip(starts, sizes))]
        lane = lax.broadcasted_iota(jnp.int32, (D_MODEL, PROJ_CAST_ROWS), 1)
        copies[0].start()
        for k, (c0, nr) in enumerate(zip(starts, sizes)):
            if k + 1 < len(copies):
                copies[k + 1].start()
            copies[k].wait()
            cols = stage_ref[k % 2].T
            if nr < PROJ_CAST_ROWS:
                cols = jnp.where(lane < nr, cols, 0.0)
            wb_ref[:, c0:c0 + PROJ_CAST_ROWS] = cols.astype(BF16)

    o_q, o_k, o_v, o_g, o_z, o_ab = 0, ATTN_Q, ATTN_Q + ATTN_KV, ATTN_Q + 2 * ATTN_KV, \
        ATTN_Q + 2 * ATTN_KV + 3 * GDN_W, ATTN_Q + 2 * ATTN_KV + 4 * GDN_W
    n_slab = 3 * GDN_W // LANES

    def normed(x):
        ms = jnp.mean(x * x, axis=-1, keepdims=True)
        return (x * lax.rsqrt(ms + EPS) * nw_ref[...]).astype(BF16)

    def to_slabs(y, row0):
        for sb in range(n_slab):
            xe_ref[sb, row0:row0 + y.shape[0], :] = y[:, sb * LANES:(sb + 1) * LANES]

    i = step % tiles_per_seq
    outs = ((qa_ref, o_q, ATTN_Q, ATTN_HEAD_DIM ** -0.5), (ka_ref, o_k, ATTN_KV, None), (va_ref, o_v, ATTN_KV, None),
            (z_ref, o_z, GDN_W, None), (ab_ref, o_ab, LANES, None))
    for r0 in range(0, PROJ_TM, PROJ_SUB):
        rows = slice(r0, r0 + PROJ_SUB)
        if r0 == 0:
            xg = normed(jnp.concatenate([xp_ref[...], xn_ref[...], x_ref[rows, :]], axis=0))
            yg = jnp.dot(xg, wb_ref[:, o_g:o_z], preferred_element_type=F32)
            to_slabs(jnp.where(i > 0, yg[:SUBLANES], 0.0), 0)
            to_slabs(jnp.where(i < tiles_per_seq - 1, yg[SUBLANES:2 * SUBLANES], 0.0), SUBLANES + PROJ_TM)
            to_slabs(yg[2 * SUBLANES:], SUBLANES)
            xn = xg[2 * SUBLANES:]
        else:
            xn = normed(x_ref[rows, :])
            to_slabs(jnp.dot(xn, wb_ref[:, o_g:o_z], preferred_element_type=F32), SUBLANES + r0)
            _conv_silu_norm(xe_ref, cw_ref, yn_ref, r0 - PROJ_SUB, PROJ_SUB)
        for ref, c0, width, scale in outs:
            y = jnp.dot(xn, wb_ref[:, c0:c0 + width], preferred_element_type=F32)
            if scale is not None:
                y = y * scale
            ref[rows, :] = y.astype(ref.dtype)
    _conv_silu_norm(xe_ref, cw_ref, yn_ref, PROJ_TM - PROJ_SUB, PROJ_SUB)


def _proj(x2, norm_w, w_in, conv_w, layer, seq_len):
    n = x2.shape[0]
    d_in = w_in.shape[2]
    d_pad = d_in // LANES * LANES + LANES
    assert d_pad == ATTN_Q + 2 * ATTN_KV + 4 * GDN_W + LANES and d_in - (d_pad - LANES) == 2 * N_CHAIN
    assert seq_len % PROJ_TM == 0
    w_t = jnp.swapaxes(w_in, 1, 2)
    nh8 = PROJ_TM // SUBLANES
    row = lambda w: pl.BlockSpec((PROJ_TM, w), lambda i: (i, 0))
    return pl.pallas_call(
        functools.partial(_proj_kernel, layer, seq_len // PROJ_TM),
        grid=(n // PROJ_TM,),
        in_specs=[row(D_MODEL),
                  pl.BlockSpec((SUBLANES, D_MODEL), lambda i: (jnp.maximum(i * nh8 - 1, 0), 0)),
                  pl.BlockSpec((SUBLANES, D_MODEL), lambda i: (jnp.minimum((i + 1) * nh8, n // SUBLANES - 1), 0)),
                  pl.BlockSpec((1, D_MODEL), lambda i: (0, 0)),
                  pl.BlockSpec(memory_space=pl.ANY),
                  pl.BlockSpec((SUBLANES, 3 * GDN_W), lambda i: (0, 0))],
        out_specs=[row(ATTN_Q), row(ATTN_KV), row(ATTN_KV),
                   pl.BlockSpec((3 * GDN_W // LANES, PROJ_TM, LANES), lambda i: (0, i, 0)),
                   row(GDN_W), row(LANES)],
        out_shape=[jax.ShapeDtypeStruct((n, ATTN_Q), BF16),
                   jax.ShapeDtypeStruct((n, ATTN_KV), BF16),
                   jax.ShapeDtypeStruct((n, ATTN_KV), BF16),
                   jax.ShapeDtypeStruct((3 * GDN_W // LANES, n, LANES), F32),
                   jax.ShapeDtypeStruct((n, GDN_W), F32),
                   jax.ShapeDtypeStruct((n, LANES), F32)],
        scratch_shapes=[pltpu.VMEM((D_MODEL, d_pad), BF16),
                        pltpu.VMEM((3 * GDN_W // LANES, PROJ_TM + 2 * SUBLANES, LANES), F32),
                        pltpu.VMEM((2, PROJ_CAST_ROWS, D_MODEL), F32),
                        pltpu.SemaphoreType.DMA((2,))],
        compiler_params=pltpu.CompilerParams(dimension_semantics=("arbitrary",),
                                             vmem_limit_bytes=VMEM_LIMIT),
        name="proj",
    )(x2, x2, x2, norm_w, w_t, conv_w)


def _bias_kernel(relb_ref, bucket_ref, o_ref):
    bucket = bucket_ref[...]
    key = lax.broadcasted_iota(jnp.int32, (3 * BLOCK, BLOCK), 0)
    qry = lax.broadcasted_iota(jnp.int32, (3 * BLOCK, BLOCK), 1)
    in_window = jnp.abs(key - BLOCK - qry) <= WINDOW
    for h in range(ATTN_HEADS):
        acc = jnp.zeros((3 * BLOCK, BLOCK), F32)
        for b in range(N_BUCKETS):
            acc = jnp.where(bucket == b, relb_ref[b, h], acc)
        o_ref[h // 2, :, (h % 2) * BLOCK:(h % 2 + 1) * BLOCK] = jnp.where(in_window, acc, -1e30)


def _bias_band(rel_bias, bucket_t):
    shape = (ATTN_HEADS // 2, 3 * BLOCK, 2 * BLOCK)
    return pl.pallas_call(
        _bias_kernel,
        in_specs=[pl.BlockSpec(memory_space=pltpu.SMEM),
                  pl.BlockSpec((3 * BLOCK, BLOCK), lambda: (0, 0))],
        out_specs=pl.BlockSpec(shape, lambda: (0, 0, 0)),
        out_shape=jax.ShapeDtypeStruct(shape, F32),
        name="bias_band",
    )(rel_bias, bucket_t)


def _t5_buckets(rel):
    nb = N_BUCKETS // 2
    max_exact = nb // 2
    base = jnp.where(rel > 0, nb, 0)
    n = jnp.abs(rel)
    log_ratio = jnp.log(jnp.maximum(n, 1).astype(jnp.float32) / max_exact) / math.log(MAX_DISTANCE / max_exact)
    large = jnp.minimum(max_exact + (log_ratio * (nb - max_exact)).astype(jnp.int32), nb - 1)
    return base + jnp.where(n < max_exact, n, large)


def _attn_kernel(sink_ref, q_ref, kp_ref, kc_ref, kn_ref, vp_ref, vc_ref, vn_ref, bias_ref, o_ref):
    n = pl.program_id(1)
    last = pl.num_programs(1) - 1
    kband = jnp.concatenate([kp_ref[...], kc_ref[...], kn_ref[...]], axis=0)
    vband = jnp.concatenate([vp_ref[...], vc_ref[...], vn_ref[...]], axis=0)
    vband_t = vband.astype(F32).T.astype(BF16)
    key = lax.broadcasted_iota(jnp.int32, (3 * BLOCK, 1), 0)
    first_head = lax.broadcasted_iota(jnp.int32, (1, 2 * BLOCK), 1) < BLOCK
    head = lambda t, i: t[:, i * ATTN_HEAD_DIM:(i + 1) * ATTN_HEAD_DIM]
    n_pairs = ATTN_HEADS // 2
    kv_of = lambda pr: (2 * pr) // ATTN_GROUP
    units = [(j, pr) for j in range(ATTN_QB) for pr in range(n_pairs)]
    band_rows = lambda j: slice(j * BLOCK, (j + 3) * BLOCK)

    scores = {}
    for j, pr in units:
        qj = q_ref[j * BLOCK:(j + 1) * BLOCK, :]
        q2 = jnp.concatenate([head(qj, 2 * pr), head(qj, 2 * pr + 1)], axis=0)
        scores[j, pr] = _dot_nt(head(kband, kv_of(pr))[band_rows(j)], q2)
    probs, dens = {}, {}
    for j, pr in units:
        s = scores[j, pr] + bias_ref[pr]
        if j == 0:
            s = jnp.where((key < BLOCK) & (n == 0), -1e30, s)
        if j == ATTN_QB - 1:
            s = jnp.where((key >= 2 * BLOCK) & (n == last), -1e30, s)
        sink = jnp.where(first_head, sink_ref[2 * pr], sink_ref[2 * pr + 1])
        m = jnp.maximum(jnp.max(s, axis=0, keepdims=True), sink)
        p = jnp.exp(s - m)
        dens[j, pr] = jnp.sum(p, axis=0, keepdims=True) + jnp.exp(sink - m)
        probs[j, pr] = p.astype(BF16)
    outs_t = {}
    for j, pr in units:
        kv = kv_of(pr)
        v_t = vband_t[kv * ATTN_HEAD_DIM:(kv + 1) * ATTN_HEAD_DIM, band_rows(j)]
        outs_t[j, pr] = jnp.dot(v_t, probs[j, pr], preferred_element_type=F32) / dens[j, pr]
    for j in range(ATTN_QB):
        o_t = jnp.concatenate([outs_t[j, pr][:, half * BLOCK:(half + 1) * BLOCK]
                               for pr in range(n_pairs) for half in range(2)], axis=0)
        o_ref[j * BLOCK:(j + 1) * BLOCK, :] = o_t.T.astype(o_ref.dtype)


def _attention(q_a, k_a, v_a, band, sink, batch):
    n_tok = q_a.shape[0]
    nb = n_tok // batch // BLOCK
    rows = ATTN_QB * BLOCK
    steps = nb // ATTN_QB
    kv_spec = lambda r, f: pl.BlockSpec((r, ATTN_KV), f)
    prev = lambda bi, n: (bi * nb + jnp.maximum(n * ATTN_QB - 1, 0), 0)
    cur = lambda bi, n: (bi * steps + n, 0)
    nxt = lambda bi, n: (bi * nb + jnp.minimum((n + 1) * ATTN_QB, nb - 1), 0)
    return pl.pallas_call(
        _attn_kernel,
        grid=(batch, steps),
        in_specs=[pl.BlockSpec(memory_space=pltpu.SMEM),
                  pl.BlockSpec((rows, ATTN_Q), cur),
                  kv_spec(BLOCK, prev), kv_spec(rows, cur), kv_spec(BLOCK, nxt),
                  kv_spec(BLOCK, prev), kv_spec(rows, cur), kv_spec(BLOCK, nxt),
                  pl.BlockSpec(band.shape, lambda bi, n: (0, 0, 0))],
        out_specs=pl.BlockSpec((rows, ATTN_Q), cur),
        out_shape=jax.ShapeDtypeStruct((n_tok, ATTN_Q), BF16),
        compiler_params=pltpu.CompilerParams(dimension_semantics=("arbitrary", "arbitrary"),
                                             vmem_limit_bytes=VMEM_LIMIT),
        name="attn",
    )(sink, q_a, k_a, k_a, k_a, v_a, v_a, v_a, band)


def _gprep_kernel(yn_ref, ab_ref, gp_ref, w_ref, u_ref, qd_ref, kdt_ref, qk_ref, egl_ref):
    t_len = PREP_T

    ab = ab_ref[0]
    sp_in = ab + gp_ref[1:2, :]
    softplus = jnp.maximum(sp_in, 0.0) + jnp.log1p(jnp.exp(-jnp.abs(sp_in)))
    g = -jnp.exp(gp_ref[0:1, :]) * softplus
    beta = jax.nn.sigmoid(ab)

    r_t = lax.broadcasted_iota(jnp.int32, (t_len, t_len), 0)
    c_t = lax.broadcasted_iota(jnp.int32, (t_len, t_len), 1)
    same = (r_t // CHUNK) == (c_t // CHUNK)
    lower = jnp.where(same & (r_t >= c_t), 1.0, 0.0).astype(BF16)
    upper = jnp.where(same & (r_t <= c_t), 1.0, 0.0).astype(BF16)
    g_t = g.T[:2 * SUBLANES]
    cs_row = (_dot01_right(g_t, upper), _dot01_right(g_t, lower))
    pad_rows = jnp.zeros((LANES - 2 * SUBLANES, t_len), F32)
    cs_col = tuple(jnp.concatenate([r, pad_rows], axis=0).T for r in cs_row)

    qs = [yn_ref[h] for h in range(GDN_HEADS)]
    ks = [yn_ref[GDN_HEADS + h] for h in range(GDN_HEADS)]
    vs = [yn_ref[2 * GDN_HEADS + h] for h in range(GDN_HEADS)]
    kts = [kh.T for kh in ks]

    r_c = lax.broadcasted_iota(jnp.int32, (CHUNK, LANES), 0)
    lane = lax.broadcasted_iota(jnp.int32, (CHUNK, LANES), 1)
    is_fwd = lane < CHUNK
    c_c = lane % CHUNK
    eye = jnp.where(r_c == c_c, 1.0, 0.0).astype(F32)
    incl = (is_fwd & (r_c >= c_c)) | (~is_fwd & (r_c <= c_c))
    strict = (is_fwd & (r_c > c_c)) | (~is_fwd & (r_c < c_c))
    r_d = lax.broadcasted_iota(jnp.int32, (2 * CHUNK, LANES), 0)
    c_d = lax.broadcasted_iota(jnp.int32, (2 * CHUNK, LANES), 1)
    same_dir = (r_d // CHUNK) == (c_d // CHUNK)
    level_mask = lambda s_, r_, c_: ((r_ // (2 * s_)) == (c_ // (2 * s_))) & ((r_ // s_) != (c_ // s_))
    stack2 = lambda t: jnp.concatenate([t, t], axis=0)

    n_chunks = t_len // CHUNK
    rows = lambda c: slice(c * CHUNK, (c + 1) * CHUNK)
    pairs = [(c, h) for c in range(n_chunks) for h in range(GDN_HEADS)]
    qkk = {}
    for c, h in pairs:
        k16 = ks[h][rows(c)].astype(BF16)
        qk16 = jnp.concatenate([qs[h][rows(c)].astype(BF16), k16], axis=0)
        qkk[c, h] = _dot_nt(qk16, stack2(k16))

    bcast = lambda col: jnp.broadcast_to(col, (CHUNK, LANES))
    g_full, b_full, grow, glast, decay, a_mat, t_mat = {}, {}, {}, {}, {}, {}, {}
    for key in pairs:
        c, h = key
        for d in range(N_DIR):
            j = d * GDN_HEADS + h
            r_last = c * CHUNK + (CHUNK - 1 if d == 0 else 0)
            g_full[key, d] = bcast(cs_col[d][rows(c), j:j + 1])
            b_full[key, d] = bcast(beta[rows(c), SUBLANES + j:SUBLANES + j + 1])
            glast[key, d] = cs_col[d][r_last:r_last + 1, j:j + 1]
            grow[key, d] = cs_row[d][j:j + 1, rows(c)]
        gcol2 = jnp.where(is_fwd, g_full[key, 0], g_full[key, 1])
        bcol2 = jnp.where(is_fwd, b_full[key, 0], b_full[key, 1])
        grow2 = jnp.concatenate([grow[key, 0], grow[key, 1]], axis=1)
        decay[key] = jnp.exp(jnp.where(incl, gcol2 - grow2, -jnp.inf))
        a_mat[key] = jnp.where(strict, bcol2 * qkk[key][CHUNK:] * decay[key], 0.0)
        t_mat[key] = eye - jnp.where(level_mask(1, r_c, c_c), a_mat[key], 0.0)

    same_dir16 = jnp.where(same_dir, 1.0, 0.0).astype(BF16)
    block_diag = lambda t16: stack2(t16) * same_dir16
    a_bd = {key: block_diag(a_mat[key].astype(BF16)) for key in pairs}
    s = 2
    while s < CHUNK:
        lvl = level_mask(s, r_c, c_c)
        t16 = {key: t_mat[key].astype(BF16) for key in pairs}
        x_mat = {key: jnp.dot(t16[key], a_bd[key], preferred_element_type=F32) for key in pairs}
        y_mat = {key: jnp.dot(x_mat[key].astype(BF16), block_diag(t16[key]), preferred_element_type=F32)
                 for key in pairs}
        t_mat = {key: t_mat[key] - jnp.where(lvl, y_mat[key], 0.0) for key in pairs}
        s *= 2

    wu = {}
    for key in pairs:
        c, h = key
        kv = jnp.concatenate(
            [jnp.concatenate([ks[h][rows(c)] * (b_full[key, d] * jnp.exp(g_full[key, d])),
                              vs[h][rows(c)] * b_full[key, d]], axis=1) for d in range(N_DIR)], axis=0)
        t_sel = block_diag(t_mat[key].astype(BF16))
        wu[key] = jnp.dot(t_sel, kv.astype(BF16), preferred_element_type=F32)

    for c in range(n_chunks):
        egl_rows = []
        for d in range(N_DIR):
            for h in range(GDN_HEADS):
                key = (c, h)
                j = d * GDN_HEADS + h
                wu_d = wu[key][d * CHUNK:(d + 1) * CHUNK]
                w_ref[0, j, rows(c), :] = wu_d[:, :GDN_HEAD_DIM].astype(w_ref.dtype)
                u_ref[0, j, rows(c), :] = wu_d[:, GDN_HEAD_DIM:].astype(u_ref.dtype)
                qd_ref[0, j, rows(c), :] = (qs[h][rows(c)] * jnp.exp(g_full[key, d])).astype(qd_ref.dtype)
                egl_rows.append(jnp.broadcast_to(jnp.exp(glast[key, d]), (1, LANES)))
        egl_ref[0, c] = jnp.concatenate(egl_rows, axis=0)
        for h in range(GDN_HEADS):
            qk_ref[0, h, rows(c), :] = (qkk[c, h][:CHUNK] * decay[c, h]).astype(qk_ref.dtype)
    for cp in range(n_chunks // 2):
        for d in range(N_DIR):
            for h in range(GDN_HEADS):
                fac = jnp.concatenate([jnp.exp(glast[(c, h), d] - grow[(c, h), d]) for c in (2 * cp, 2 * cp + 1)],
                                      axis=1)
                kdt = kts[h][:, 2 * cp * CHUNK:(2 * cp + 2) * CHUNK] * fac
                kdt_ref[0, d * GDN_HEADS + h, cp] = kdt.astype(kdt_ref.dtype)


def _gdn_prep(yn, ab, gate_par):
    b, s, _ = ab.shape
    nt = s // PREP_T
    cpb = PREP_T // CHUNK
    nc = s // CHUNK
    chain = lambda last: pl.BlockSpec((1, N_CHAIN, PREP_T, last), lambda bi, i: (bi, 0, i, 0))
    return pl.pallas_call(
        _gprep_kernel,
        grid=(b, nt),
        in_specs=[pl.BlockSpec((3 * GDN_W // LANES, PREP_T, LANES), lambda bi, i: (0, bi * nt + i, 0)),
                  pl.BlockSpec((1, PREP_T, LANES), lambda bi, i: (bi, i, 0)),
                  pl.BlockSpec((SUBLANES, LANES), lambda bi, i: (0, 0))],
        out_specs=[chain(GDN_HEAD_DIM), chain(GDN_HEAD_DIM), chain(GDN_HEAD_DIM),
                   pl.BlockSpec((1, N_CHAIN, cpb // 2, GDN_HEAD_DIM, 2 * CHUNK), lambda bi, i: (bi, 0, i, 0, 0)),
                   pl.BlockSpec((1, GDN_HEADS, PREP_T, LANES), lambda bi, i: (bi, 0, i, 0)),
                   pl.BlockSpec((1, cpb, N_CHAIN, LANES), lambda bi, i: (bi, i, 0, 0))],
        out_shape=[jax.ShapeDtypeStruct((b, N_CHAIN, s, GDN_HEAD_DIM), BF16),
                   jax.ShapeDtypeStruct((b, N_CHAIN, s, GDN_HEAD_DIM), BF16),
                   jax.ShapeDtypeStruct((b, N_CHAIN, s, GDN_HEAD_DIM), BF16),
                   jax.ShapeDtypeStruct((b, N_CHAIN, nc // 2, GDN_HEAD_DIM, 2 * CHUNK), BF16),
                   jax.ShapeDtypeStruct((b, GDN_HEADS, s, LANES), BF16),
                   jax.ShapeDtypeStruct((b, nc, N_CHAIN, LANES), F32)],
        compiler_params=pltpu.CompilerParams(dimension_semantics=("arbitrary", "arbitrary"),
                                             vmem_limit_bytes=VMEM_LIMIT),
        name="gdn_prep",
    )(yn, ab, gate_par)


def _gscan_kernel(egl_ref, wf_ref, wb_ref, uf_ref, ub_ref, qf_ref, qb_ref, kf_ref, kb_ref,
                  pf_ref, pb_ref, of_ref, ob_ref, state_ref):
    t = pl.program_id(0)
    n_steps = pl.num_programs(0)
    nc = n_steps * SCAN_CHUNKS
    n_batch = wf_ref.shape[0]

    @pl.when(t == 0)
    def _():
        state_ref[...] = jnp.zeros_like(state_ref)

    dirs = ((wf_ref, uf_ref, qf_ref, kf_ref, pf_ref, of_ref), (wb_ref, ub_ref, qb_ref, kb_ref, pb_ref, ob_ref))
    chains = [(bi, d, h) for bi in range(n_batch) for d in range(N_DIR) for h in range(GDN_HEADS)]
    slot = lambda bi, d, h: (bi * N_DIR + d) * GDN_HEADS + h
    st = {key: state_ref[slot(*key)] for key in chains}

    for sub in range(SCAN_CHUNKS):
        local = (sub, SCAN_CHUNKS - 1 - sub)
        rows = [slice(c * CHUNK, (c + 1) * CHUNK) for c in local]
        chunk = (t * SCAN_CHUNKS + sub, nc - 1 - (t * SCAN_CHUNKS + sub))
        r = {}
        for key in chains:
            bi, d, h = key
            w_ref, _, q_ref = dirs[d][:3]
            wq = jnp.concatenate([w_ref[bi, h, rows[d]], q_ref[bi, h, rows[d]]], axis=0)
            r[key] = _dot(wq, st[key])
        v_pad, intra = {}, {}
        for key in chains:
            bi, d, h = key
            u_ref, p_ref = dirs[d][1], dirs[d][4]
            v_new = (u_ref[bi, h, rows[d]].astype(F32) - r[key][:CHUNK]).astype(BF16)
            zeros = jnp.zeros_like(v_new)
            v_pad[key] = (jnp.concatenate([v_new, zeros], axis=0), jnp.concatenate([zeros, v_new], axis=0))
            intra[key] = jnp.dot(p_ref[bi, h, rows[d]], v_pad[key][d], preferred_element_type=F32)
        for key in chains:
            bi, d, h = key
            k_ref = dirs[d][3]
            egl = egl_ref[(bi * nc + chunk[d]) * N_CHAIN + d * GDN_HEADS + h]
            st[key] = st[key] * egl + jnp.dot(k_ref[bi, h, local[d] // 2], v_pad[key][local[d] % 2],
                                              preferred_element_type=F32)
        for bi in range(n_batch):
            for d in range(N_DIR):
                o_ref = dirs[d][5]
                o_ref[bi, rows[d], :] = jnp.concatenate(
                    [r[bi, d, h][CHUNK:] + intra[bi, d, h] for h in range(GDN_HEADS)], axis=-1).astype(o_ref.dtype)

    for key in chains:
        state_ref[slot(*key)] = st[key]


def _gdn_scan(egl, w, u, qd, kdt, qk):
    b, _, s, _ = w.shape
    rows = SCAN_CHUNKS * CHUNK
    n_steps = s // rows
    fwd_i = lambda t: t
    bwd_i = lambda t: n_steps - 1 - t
    chain = lambda d, at, last: pl.BlockSpec((b, GDN_HEADS, rows, last), lambda t: (0, d, at(t), 0))
    kspec = lambda d, at: pl.BlockSpec((b, GDN_HEADS, SCAN_CHUNKS // 2, GDN_HEAD_DIM, 2 * CHUNK),
                                       lambda t: (0, d, at(t), 0, 0))
    dk = GDN_HEAD_DIM
    return pl.pallas_call(
        _gscan_kernel,
        grid=(n_steps,),
        in_specs=[pl.BlockSpec(memory_space=pltpu.SMEM),
                  chain(0, fwd_i, dk), chain(1, bwd_i, dk), chain(0, fwd_i, dk), chain(1, bwd_i, dk),
                  chain(0, fwd_i, dk), chain(1, bwd_i, dk), kspec(0, fwd_i), kspec(1, bwd_i),
                  chain(0, fwd_i, LANES), chain(0, bwd_i, LANES)],
        out_specs=[pl.BlockSpec((b, rows, GDN_W), lambda t: (0, fwd_i(t), 0)),
                   pl.BlockSpec((b, rows, GDN_W), lambda t: (0, bwd_i(t), 0))],
        out_shape=[jax.ShapeDtypeStruct((b, s, GDN_W), BF16),
                   jax.ShapeDtypeStruct((b, s, GDN_W), BF16)],
        scratch_shapes=[pltpu.VMEM((b * N_CHAIN, GDN_HEAD_DIM, GDN_HEAD_DIM), F32)],
        compiler_params=pltpu.CompilerParams(dimension_semantics=("arbitrary",),
                                             vmem_limit_bytes=VMEM_LIMIT),
        name="gdn_scan",
    )(egl, w, w, u, u, qd, qd, kdt, kdt, qk, qk)


def _cast_weights_once(layer, copies):
    jobs, used = [], {}
    for src, dst, stage, sem, chunk_rows in copies:
        for r0 in range(0, src.shape[1], chunk_rows):
            slot = used.get(id(stage), 0) % 2
            used[id(stage)] = used.get(id(stage), 0) + 1
            rows = pl.ds(r0, chunk_rows)
            jobs.append((pltpu.make_async_copy(src.at[layer, rows], stage.at[slot], sem.at[slot]),
                         stage, slot, dst, rows))
    jobs[0][0].start()
    for k, (copy, stage, slot, dst, rows) in enumerate(jobs):
        if k + 1 < len(jobs):
            jobs[k + 1][0].start()
        copy.wait()
        dst[rows, :] = stage[slot].astype(BF16)


def _ffn_kernel(layer, x_ref, attn_ref, of_ref, ob_ref, z_ref, gnw_ref, wo_hbm, fnw_ref, w1_hbm, w2_hbm, onw_ref,
                o_ref, wo_ref, w1_ref, w2_ref, stage_wide, stage_tall, sem_wide, sem_tall):
    @pl.when(pl.program_id(0) == 0)
    def _():
        _cast_weights_once(layer, [(w1_hbm, w1_ref, stage_wide, sem_wide, FFN_CAST_ROWS_WIDE),
                                   (w2_hbm, w2_ref, stage_tall, sem_tall, FFN_CAST_ROWS_TALL),
                                   (wo_hbm, wo_ref, stage_tall, sem_tall, FFN_CAST_ROWS_TALL)])

    subs = [slice(r0, r0 + FFN_SUB) for r0 in range(0, FFN_TM, FFN_SUB)]
    rms = lambda t: t * lax.rsqrt(jnp.mean(t * t, axis=-1, keepdims=True) + EPS)
    hres, hn, act, acc = {}, {}, {}, {}
    for r in subs:
        o = of_ref[r, :].astype(F32) + ob_ref[r, :].astype(F32)
        heads = [rms(o[:, h * GDN_HEAD_DIM:(h + 1) * GDN_HEAD_DIM]) * gnw_ref[...] for h in range(GDN_HEADS)]
        z = z_ref[r, :]
        gdn = jnp.concatenate(heads, axis=-1) * (z * jax.nn.sigmoid(z))
        hres[r.start] = (x_ref[r, :] + jnp.dot(attn_ref[r, :], wo_ref[:ATTN_Q, :], preferred_element_type=F32)
                         + _dot(gdn, wo_ref[ATTN_Q:, :]))
    for r in subs:
        hn[r.start] = (rms(hres[r.start]) * fnw_ref[...]).astype(BF16)
        act[r.start] = jnp.dot(hn[r.start], w1_ref[...], preferred_element_type=F32)
    for r in subs:
        a = jnp.square(jnp.maximum(act[r.start], 0.0)).astype(BF16)
        acc[r.start] = hres[r.start] + jnp.dot(a, w2_ref[...], preferred_element_type=F32)
    for r in subs:
        o_ref[r, :] = rms(acc[r.start]) * onw_ref[...]


def _out_ffn(x2, attn, o_f, o_b, z, gnw, wo, fnw, w1, w2, onw, layer):
    n = x2.shape[0]
    assert wo.shape[1:] == (D_MODEL, D_MODEL) and w1.shape[1:] == (D_MODEL, D_FF) and w2.shape[1:] == (D_FF, D_MODEL)
    row = lambda w: pl.BlockSpec((FFN_TM, w), lambda i: (i, 0))
    full = lambda a: pl.BlockSpec(a.shape, lambda i: (0, 0))
    hbm = pl.BlockSpec(memory_space=pl.ANY)
    return pl.pallas_call(
        functools.partial(_ffn_kernel, layer),
        grid=(n // FFN_TM,),
        in_specs=[row(D_MODEL), row(ATTN_Q), row(GDN_W), row(GDN_W), row(GDN_W),
                  full(gnw), hbm, full(fnw), hbm, hbm, full(onw)],
        out_specs=row(D_MODEL),
        out_shape=jax.ShapeDtypeStruct((n, D_MODEL), F32),
        scratch_shapes=[pltpu.VMEM((D_MODEL, D_MODEL), BF16),
                        pltpu.VMEM((D_MODEL, D_FF), BF16),
                        pltpu.VMEM((D_FF, D_MODEL), BF16),
                        pltpu.VMEM((2, FFN_CAST_ROWS_WIDE, D_FF), F32),
                        pltpu.VMEM((2, FFN_CAST_ROWS_TALL, D_MODEL), F32),
                        pltpu.SemaphoreType.DMA((2,)),
                        pltpu.SemaphoreType.DMA((2,))],
        compiler_params=pltpu.CompilerParams(dimension_semantics=("arbitrary",),
                                             vmem_limit_bytes=VMEM_LIMIT),
        name="out_ffn",
    )(x2, attn, o_f, o_b, z, gnw, wo, fnw, w1, w2, onw)


def _layer(h, band, norm_mix_w, w_in, layer, attn_sink, conv_w, gdn_a_log, gdn_dt_bias, gdn_norm_w,
           w_out, norm_ffn_w, w_ffn_in, w_ffn_out, out_norm_w):
    b, s, _ = h.shape
    n = b * s
    x2 = h.reshape(n, D_MODEL)
    conv_pad = jnp.zeros((SUBLANES, 3 * GDN_W), F32).at[:CONV_K].set(conv_w)
    q_a, k_a, v_a, yn, z_g, ab = _proj(x2, norm_mix_w.reshape(1, D_MODEL), w_in, conv_pad, layer, s)

    attn = _attention(q_a, k_a, v_a, band, attn_sink, b)

    gate_par = jnp.zeros((SUBLANES, LANES), F32)
    gate_par = gate_par.at[0, :N_CHAIN].set(gdn_a_log.reshape(-1)).at[1, :N_CHAIN].set(gdn_dt_bias.reshape(-1))
    w_c, u_c, q_dec, k_dec_t, qk, egl = _gdn_prep(yn, ab.reshape(b, s, LANES), gate_par)
    o_f, o_b = _gdn_scan(egl[..., 0].reshape(-1), w_c, u_c, q_dec, k_dec_t, qk)

    out = _out_ffn(x2, attn, o_f.reshape(n, GDN_W), o_b.reshape(n, GDN_W), z_g,
                   gdn_norm_w.reshape(1, GDN_HEAD_DIM), w_out, norm_ffn_w.reshape(1, D_MODEL),
                   w_ffn_in, w_ffn_out, out_norm_w.reshape(1, D_MODEL), layer)
    return out.reshape(b, s, D_MODEL)


def kernel(x, norm_mix_w, w_in, rel_bias, attn_sink, conv_w, gdn_a_log, gdn_dt_bias, gdn_norm_w, w_out,
           norm_ffn_w, w_ffn_in, w_ffn_out, norm_final_w):
    depth = w_in.shape[0]
    assert depth == 1, "the fused output kernel applies the final norm after the single trunk layer"
    rel = (np.arange(3 * BLOCK)[None, :] - BLOCK) - np.arange(BLOCK)[:, None]
    bucket = _t5_buckets(jnp.asarray(rel, dtype=jnp.int32))
    band = _bias_band(rel_bias, bucket.T)
    return _layer(x, band, norm_mix_w[0], w_in, 0, attn_sink[0], conv_w[0], gdn_a_log[0], gdn_dt_bias[0],
                  gdn_norm_w[0], w_out, norm_ffn_w[0], w_ffn_in, w_ffn_out, norm_final_w)
```

```python
import functools
import math

import jax
import jax.numpy as jnp
import numpy as np
from jax import lax
from jax.experimental import pallas as pl
from jax.experimental.pallas import tpu as pltpu

F32 = jnp.float32
BF16 = jnp.bfloat16

D_MODEL = 1024
ATTN_HEADS = 8
ATTN_KV_HEADS = 2
ATTN_HEAD_DIM = 64
ATTN_GROUP = ATTN_HEADS // ATTN_KV_HEADS
WINDOW = 128
BLOCK = 128
N_BUCKETS = 32
MAX_DISTANCE = 128
GDN_HEADS = 4
GDN_HEAD_DIM = 128
CONV_K = 5
CHUNK = 64
N_DIR = 2
N_CHAIN = N_DIR * GDN_HEADS
D_FF = 4 * D_MODEL
EPS = 1e-6
ATTN_Q = ATTN_HEADS * ATTN_HEAD_DIM
ATTN_KV = ATTN_KV_HEADS * ATTN_HEAD_DIM
GDN_W = GDN_HEADS * GDN_HEAD_DIM
LANES = 128
SUBLANES = 8
VMEM_LIMIT = 56 * 1024 * 1024

PROJ_TM = 1024
PROJ_SUB = 256
PROJ_CAST_ROWS = 128
ATTN_QB = 4
PREP_T = 512
ROW_STRIDE = 4
SCAN_CHUNKS = 4
FFN_TM = 512
FFN_SUB = 256
FFN_CAST_ROWS_WIDE = 128
FFN_CAST_ROWS_TALL = 512


def _dot(a, b):
    return jnp.dot(a.astype(BF16), b.astype(BF16), preferred_element_type=F32)


def _dot_nt(a, b):
    return lax.dot_general(a.astype(BF16), b.astype(BF16), (((1,), (1,)), ((), ())),
                           preferred_element_type=F32)


def _split3(x):
    hi = x.astype(BF16)
    r1 = x - hi.astype(F32)
    mid = r1.astype(BF16)
    lo = (r1 - mid.astype(F32)).astype(BF16)
    return hi, mid, lo


def _dot01_right(x, m01):
    hi, mid, lo = _split3(x)
    d = lambda p: jnp.dot(p, m01, preferred_element_type=F32)
    return d(hi) + d(mid) + d(lo)


def _conv_silu_norm(xe_ref, cw_ref, yn_ref, r0, n_out):
    halo = CONV_K // 2
    n_rows = n_out // ROW_STRIDE
    n_slab = 3 * GDN_W // LANES
    units = [(sb, ph) for sb in range(n_slab) for ph in range(ROW_STRIDE)]
    yv = {}
    for sb, ph in units:
        lanes = slice(sb * LANES, (sb + 1) * LANES)
        acc = None
        for j in range(CONV_K):
            win = xe_ref[sb, pl.ds(SUBLANES - halo + j + ph, n_rows, stride=ROW_STRIDE), :]
            term = cw_ref[j:j + 1, lanes] * win
            acc = term if acc is None else acc + term
        yv[sb, ph] = acc
    for key in units:
        yv[key] = yv[key] * jax.nn.sigmoid(yv[key])
    for sb, ph in units:
        if sb < 2 * GDN_HEADS:
            scale = lax.rsqrt(jnp.sum(yv[sb, ph] * yv[sb, ph], axis=-1, keepdims=True) + EPS)
            if sb < GDN_HEADS:
                scale = scale * (GDN_HEAD_DIM ** -0.5)
            yv[sb, ph] = yv[sb, ph] * scale
    for sb, ph in units:
        yn_ref[sb, pl.ds(r0 + ph, n_rows, stride=ROW_STRIDE), :] = yv[sb, ph]


def _proj_kernel(layer, tiles_per_seq, x_ref, xp_ref, xn_ref, nw_ref, wt_hbm, cw_ref,
                 qa_ref, ka_ref, va_ref, yn_ref, z_ref, ab_ref, wb_ref, stage_ref, sem, *xe_refs):
    d_in = wt_hbm.shape[1]
    d_main = d_in // LANES * LANES
    step = pl.program_id(0)

    @pl.when(step == 0)
    def _():
        starts = list(range(0, d_main, PROJ_CAST_ROWS)) + [d_main]
        sizes = [PROJ_CAST_ROWS] * (len(starts) - 1) + [d_in - d_main]
        copies = [pltpu.make_async_copy(wt_hbm.at[layer, pl.ds(c0, nr)], stage_ref.at[k % 2, pl.ds(0, nr)],
                                        sem.at[k % 2]) for k, (c0, nr) in enumerate(zip(starts, sizes))]
        lane = lax.broadcasted_iota(jnp.int32, (D_MODEL, PROJ_CAST_ROWS), 1)
        copies[0].start()
        for k, (c0, nr) in enumerate(zip(starts, sizes)):
            if k + 1 < len(copies):
                copies[k + 1].start()
            copies[k].wait()
            cols = stage_ref[k % 2].T
            if nr < PROJ_CAST_ROWS:
                cols = jnp.where(lane < nr, cols, 0.0)
            wb_ref[:, c0:c0 + PROJ_CAST_ROWS] = cols.astype(BF16)

    o_q, o_k, o_v, o_g, o_z, o_ab = 0, ATTN_Q, ATTN_Q + ATTN_KV, ATTN_Q + 2 * ATTN_KV, \
        ATTN_Q + 2 * ATTN_KV + 3 * GDN_W, ATTN_Q + 2 * ATTN_KV + 4 * GDN_W
    n_slab = 3 * GDN_W // LANES

    def normed(x):
        ms = jnp.mean(x * x, axis=-1, keepdims=True)
        return (x * lax.rsqrt(ms + EPS) * nw_ref[...]).astype(BF16)

    n_sub = PROJ_TM // PROJ_SUB

    def to_slabs(xe_ref, y, row0):
        for sb in range(n_slab):
            xe_ref[sb, row0:row0 + y.shape[0], :] = y[:, sb * LANES:(sb + 1) * LANES]

    def put(k, y):
        to_slabs(xe_refs[k], y, SUBLANES)
        if k > 0:
            to_slabs(xe_refs[k - 1], y[:SUBLANES], SUBLANES + PROJ_SUB)
        if k + 1 < n_sub:
            to_slabs(xe_refs[k + 1], y[PROJ_SUB - SUBLANES:], 0)

    i = step % tiles_per_seq
    outs = ((qa_ref, o_q, ATTN_Q, ATTN_HEAD_DIM ** -0.5), (ka_ref, o_k, ATTN_KV, None), (va_ref, o_v, ATTN_KV, None),
            (z_ref, o_z, GDN_W, None), (ab_ref, o_ab, LANES, None))
    for k in range(n_sub):
        rows = slice(k * PROJ_SUB, (k + 1) * PROJ_SUB)
        if k == 0:
            xg = normed(jnp.concatenate([xp_ref[...], xn_ref[...], x_ref[rows, :]], axis=0))
            yg = jnp.dot(xg, wb_ref[:, o_g:o_z], preferred_element_type=F32)
            to_slabs(xe_refs[0], jnp.where(i > 0, yg[:SUBLANES], 0.0), 0)
            to_slabs(xe_refs[n_sub - 1], jnp.where(i < tiles_per_seq - 1, yg[SUBLANES:2 * SUBLANES], 0.0),
                     SUBLANES + PROJ_SUB)
            put(0, yg[2 * SUBLANES:])
            xn = xg[2 * SUBLANES:]
        else:
            xn = normed(x_ref[rows, :])
            put(k, jnp.dot(xn, wb_ref[:, o_g:o_z], preferred_element_type=F32))
            _conv_silu_norm(xe_refs[k - 1], cw_ref, yn_ref, (k - 1) * PROJ_SUB, PROJ_SUB)
        for ref, c0, width, scale in outs:
            y = jnp.dot(xn, wb_ref[:, c0:c0 + width], preferred_element_type=F32)
            if scale is not None:
                y = y * scale
            ref[rows, :] = y.astype(ref.dtype)
    _conv_silu_norm(xe_refs[n_sub - 1], cw_ref, yn_ref, PROJ_TM - PROJ_SUB, PROJ_SUB)


def _proj(x2, norm_w, w_in, conv_w, layer, seq_len):
    n = x2.shape[0]
    d_in = w_in.shape[2]
    d_pad = d_in // LANES * LANES + LANES
    assert d_pad == ATTN_Q + 2 * ATTN_KV + 4 * GDN_W + LANES and d_in - (d_pad - LANES) == 2 * N_CHAIN
    assert seq_len % PROJ_TM == 0
    w_t = jnp.swapaxes(w_in, 1, 2)
    nh8 = PROJ_TM // SUBLANES
    row = lambda w: pl.BlockSpec((PROJ_TM, w), lambda i: (i, 0))
    return pl.pallas_call(
        functools.partial(_proj_kernel, layer, seq_len // PROJ_TM),
        grid=(n // PROJ_TM,),
        in_specs=[row(D_MODEL),
                  pl.BlockSpec((SUBLANES, D_MODEL), lambda i: (jnp.maximum(i * nh8 - 1, 0), 0)),
                  pl.BlockSpec((SUBLANES, D_MODEL), lambda i: (jnp.minimum((i + 1) * nh8, n // SUBLANES - 1), 0)),
                  pl.BlockSpec((1, D_MODEL), lambda i: (0, 0)),
                  pl.BlockSpec(memory_space=pl.ANY),
                  pl.BlockSpec((SUBLANES, 3 * GDN_W), lambda i: (0, 0))],
        out_specs=[row(ATTN_Q), row(ATTN_KV), row(ATTN_KV),
                   pl.BlockSpec((3 * GDN_W // LANES, PROJ_TM, LANES), lambda i: (0, i, 0)),
                   row(GDN_W), row(LANES)],
        out_shape=[jax.ShapeDtypeStruct((n, ATTN_Q), BF16),
                   jax.ShapeDtypeStruct((n, ATTN_KV), BF16),
                   jax.ShapeDtypeStruct((n, ATTN_KV), BF16),
                   jax.ShapeDtypeStruct((3 * GDN_W // LANES, n, LANES), F32),
                   jax.ShapeDtypeStruct((n, GDN_W), F32),
                   jax.ShapeDtypeStruct((n, LANES), F32)],
        scratch_shapes=[pltpu.VMEM((D_MODEL, d_pad), BF16),
                        pltpu.VMEM((2, PROJ_CAST_ROWS, D_MODEL), F32),
                        pltpu.SemaphoreType.DMA((2,))]
        + [pltpu.VMEM((3 * GDN_W // LANES, PROJ_SUB + 2 * SUBLANES, LANES), F32)] * (PROJ_TM // PROJ_SUB),
        compiler_params=pltpu.CompilerParams(dimension_semantics=("arbitrary",),
                                             vmem_limit_bytes=VMEM_LIMIT),
        name="proj",
    )(x2, x2, x2, norm_w, w_t, conv_w)


def _bias_kernel(relb_ref, bucket_ref, o_ref):
    bucket = bucket_ref[...]
    key = lax.broadcasted_iota(jnp.int32, (3 * BLOCK, BLOCK), 0)
    qry = lax.broadcasted_iota(jnp.int32, (3 * BLOCK, BLOCK), 1)
    in_window = jnp.abs(key - BLOCK - qry) <= WINDOW
    for h in range(ATTN_HEADS):
        acc = jnp.zeros((3 * BLOCK, BLOCK), F32)
        for b in range(N_BUCKETS):
            acc = jnp.where(bucket == b, relb_ref[b, h], acc)
        o_ref[h // 2, :, (h % 2) * BLOCK:(h % 2 + 1) * BLOCK] = jnp.where(in_window, acc, -1e30)


def _bias_band(rel_bias, bucket_t):
    shape = (ATTN_HEADS // 2, 3 * BLOCK, 2 * BLOCK)
    return pl.pallas_call(
        _bias_kernel,
        in_specs=[pl.BlockSpec(memory_space=pltpu.SMEM),
                  pl.BlockSpec((3 * BLOCK, BLOCK), lambda: (0, 0))],
        out_specs=pl.BlockSpec(shape, lambda: (0, 0, 0)),
        out_shape=jax.ShapeDtypeStruct(shape, F32),
        name="bias_band",
    )(rel_bias, bucket_t)


def _t5_buckets(rel):
    nb = N_BUCKETS // 2
    max_exact = nb // 2
    base = jnp.where(rel > 0, nb, 0)
    n = jnp.abs(rel)
    log_ratio = jnp.log(jnp.maximum(n, 1).astype(jnp.float32) / max_exact) / math.log(MAX_DISTANCE / max_exact)
    large = jnp.minimum(max_exact + (log_ratio * (nb - max_exact)).astype(jnp.int32), nb - 1)
    return base + jnp.where(n < max_exact, n, large)


def _attn_kernel(sink_ref, q_ref, kp_ref, kc_ref, kn_ref, vp_ref, vc_ref, vn_ref, bias_ref, o_ref):
    n = pl.program_id(1)
    last = pl.num_programs(1) - 1
    kband = jnp.concatenate([kp_ref[...], kc_ref[...], kn_ref[...]], axis=0)
    vband = jnp.concatenate([vp_ref[...], vc_ref[...], vn_ref[...]], axis=0)
    vband_t = vband.astype(F32).T.astype(BF16)
    key = lax.broadcasted_iota(jnp.int32, (3 * BLOCK, 1), 0)
    first_head = lax.broadcasted_iota(jnp.int32, (1, 2 * BLOCK), 1) < BLOCK
    head = lambda t, i: t[:, i * ATTN_HEAD_DIM:(i + 1) * ATTN_HEAD_DIM]
    n_pairs = ATTN_HEADS // 2
    kv_of = lambda pr: (2 * pr) // ATTN_GROUP
    units = [(j, pr) for j in range(ATTN_QB) for pr in range(n_pairs)]
    band_rows = lambda j: slice(j * BLOCK, (j + 3) * BLOCK)

    scores = {}
    for j, pr in units:
        qj = q_ref[j * BLOCK:(j + 1) * BLOCK, :]
        q2 = jnp.concatenate([head(qj, 2 * pr), head(qj, 2 * pr + 1)], axis=0)
        scores[j, pr] = _dot_nt(head(kband, kv_of(pr))[band_rows(j)], q2)
    probs, dens = {}, {}
    for j, pr in units:
        s = scores[j, pr] + bias_ref[pr]
        if j == 0:
            s = jnp.where((key < BLOCK) & (n == 0), -1e30, s)
        if j == ATTN_QB - 1:
            s = jnp.where((key >= 2 * BLOCK) & (n == last), -1e30, s)
        sink = jnp.where(first_head, sink_ref[2 * pr], sink_ref[2 * pr + 1])
        m = jnp.maximum(jnp.max(s, axis=0, keepdims=True), sink)
        p = jnp.exp(s - m)
        dens[j, pr] = jnp.sum(p, axis=0, keepdims=True) + jnp.exp(sink - m)
        probs[j, pr] = p.astype(BF16)
    outs_t = {}
    for j, pr in units:
        kv = kv_of(pr)
        v_t = vband_t[kv * ATTN_HEAD_DIM:(kv + 1) * ATTN_HEAD_DIM, band_rows(j)]
        outs_t[j, pr] = jnp.dot(v_t, probs[j, pr], preferred_element_type=F32) / dens[j, pr]
    for j in range(ATTN_QB):
        o_t = jnp.concatenate([outs_t[j, pr][:, half * BLOCK:(half + 1) * BLOCK]
                               for pr in range(n_pairs) for half in range(2)], axis=0)
        o_ref[j * BLOCK:(j + 1) * BLOCK, :] = o_t.T.astype(o_ref.dtype)


def _attention(q_a, k_a, v_a, band, sink, batch):
    n_tok = q_a.shape[0]
    nb = n_tok // batch // BLOCK
    rows = ATTN_QB * BLOCK
    steps = nb // ATTN_QB
    kv_spec = lambda r, f: pl.BlockSpec((r, ATTN_KV), f)
    prev = lambda bi, n: (bi * nb + jnp.maximum(n * ATTN_QB - 1, 0), 0)
    cur = lambda bi, n: (bi * steps + n, 0)
    nxt = lambda bi, n: (bi * nb + jnp.minimum((n + 1) * ATTN_QB, nb - 1), 0)
    return pl.pallas_call(
        _attn_kernel,
        grid=(batch, steps),
        in_specs=[pl.BlockSpec(memory_space=pltpu.SMEM),
                  pl.BlockSpec((rows, ATTN_Q), cur),
                  kv_spec(BLOCK, prev), kv_spec(rows, cur), kv_spec(BLOCK, nxt),
                  kv_spec(BLOCK, prev), kv_spec(rows, cur), kv_spec(BLOCK, nxt),
                  pl.BlockSpec(band.shape, lambda bi, n: (0, 0, 0))],
        out_specs=pl.BlockSpec((rows, ATTN_Q), cur),
        out_shape=jax.ShapeDtypeStruct((n_tok, ATTN_Q), BF16),
        compiler_params=pltpu.CompilerParams(dimension_semantics=("arbitrary", "arbitrary"),
                                             vmem_limit_bytes=VMEM_LIMIT),
        name="attn",
    )(sink, q_a, k_a, k_a, k_a, v_a, v_a, v_a, band)


def _gprep_kernel(yn_ref, ab_ref, gp_ref, w_ref, u_ref, qd_ref, kdt_ref, qk_ref, egl_ref):
    t_len = PREP_T

    ab = ab_ref[0]
    sp_in = ab + gp_ref[1:2, :]
    softplus = jnp.maximum(sp_in, 0.0) + jnp.log1p(jnp.exp(-jnp.abs(sp_in)))
    g = -jnp.exp(gp_ref[0:1, :]) * softplus
    beta = jax.nn.sigmoid(ab)

    r_t = lax.broadcasted_iota(jnp.int32, (t_len, t_len), 0)
    c_t = lax.broadcasted_iota(jnp.int32, (t_len, t_len), 1)
    same = (r_t // CHUNK) == (c_t // CHUNK)
    lower = jnp.where(same & (r_t >= c_t), 1.0, 0.0).astype(BF16)
    upper = jnp.where(same & (r_t <= c_t), 1.0, 0.0).astype(BF16)
    g_t = g.T[:2 * SUBLANES]
    cs_row = (_dot01_right(g_t, upper), _dot01_right(g_t, lower))
    pad_rows = jnp.zeros((LANES - 2 * SUBLANES, t_len), F32)
    cs_col = tuple(jnp.concatenate([r, pad_rows], axis=0).T for r in cs_row)

    qs = [yn_ref[h] for h in range(GDN_HEADS)]
    ks = [yn_ref[GDN_HEADS + h] for h in range(GDN_HEADS)]
    vs = [yn_ref[2 * GDN_HEADS + h] for h in range(GDN_HEADS)]
    kts = [kh.T for kh in ks]

    r_c = lax.broadcasted_iota(jnp.int32, (CHUNK, LANES), 0)
    lane = lax.broadcasted_iota(jnp.int32, (CHUNK, LANES), 1)
    is_fwd = lane < CHUNK
    c_c = lane % CHUNK
    eye = jnp.where(r_c == c_c, 1.0, 0.0).astype(F32)
    incl = (is_fwd & (r_c >= c_c)) | (~is_fwd & (r_c <= c_c))
    strict = (is_fwd & (r_c > c_c)) | (~is_fwd & (r_c < c_c))
    r_d = lax.broadcasted_iota(jnp.int32, (2 * CHUNK, LANES), 0)
    c_d = lax.broadcasted_iota(jnp.int32, (2 * CHUNK, LANES), 1)
    same_dir = (r_d // CHUNK) == (c_d // CHUNK)
    level_mask = lambda s_, r_, c_: ((r_ // (2 * s_)) == (c_ // (2 * s_))) & ((r_ // s_) != (c_ // s_))
    stack2 = lambda t: jnp.concatenate([t, t], axis=0)

    n_chunks = t_len // CHUNK
    rows = lambda c: slice(c * CHUNK, (c + 1) * CHUNK)
    pairs = [(c, h) for c in range(n_chunks) for h in range(GDN_HEADS)]
    qkk = {}
    for c, h in pairs:
        k16 = ks[h][rows(c)].astype(BF16)
        qk16 = jnp.concatenate([qs[h][rows(c)].astype(BF16), k16], axis=0)
        qkk[c, h] = _dot_nt(qk16, stack2(k16))

    bcast = lambda col: jnp.broadcast_to(col, (CHUNK, LANES))
    g_full, b_full, grow, glast, decay, a_mat, t_mat = {}, {}, {}, {}, {}, {}, {}
    for key in pairs:
        c, h = key
        for d in range(N_DIR):
            j = d * GDN_HEADS + h
            r_last = c * CHUNK + (CHUNK - 1 if d == 0 else 0)
            g_full[key, d] = bcast(cs_col[d][rows(c), j:j + 1])
            b_full[key, d] = bcast(beta[rows(c), SUBLANES + j:SUBLANES + j + 1])
            glast[key, d] = cs_col[d][r_last:r_last + 1, j:j + 1]
            grow[key, d] = cs_row[d][j:j + 1, rows(c)]
        gcol2 = jnp.where(is_fwd, g_full[key, 0], g_full[key, 1])
        bcol2 = jnp.where(is_fwd, b_full[key, 0], b_full[key, 1])
        grow2 = jnp.concatenate([grow[key, 0], grow[key, 1]], axis=1)
        decay[key] = jnp.exp(jnp.where(incl, gcol2 - grow2, -jnp.inf))
        a_mat[key] = jnp.where(strict, bcol2 * qkk[key][CHUNK:] * decay[key], 0.0)
        t_mat[key] = eye - jnp.where(level_mask(1, r_c, c_c), a_mat[key], 0.0)

    same_dir16 = jnp.where(same_dir, 1.0, 0.0).astype(BF16)
    block_diag = lambda t16: stack2(t16) * same_dir16
    a_bd = {key: block_diag(a_mat[key].astype(BF16)) for key in pairs}
    s = 2
    while s < CHUNK:
        lvl = level_mask(s, r_c, c_c)
        t16 = {key: t_mat[key].astype(BF16) for key in pairs}
        x_mat = {key: jnp.dot(t16[key], a_bd[key], preferred_element_type=F32) for key in pairs}
        y_mat = {key: jnp.dot(x_mat[key].astype(BF16), block_diag(t16[key]), preferred_element_type=F32)
                 for key in pairs}
        t_mat = {key: t_mat[key] - jnp.where(lvl, y_mat[key], 0.0) for key in pairs}
        s *= 2

    wu = {}
    for key in pairs:
        c, h = key
        kv = jnp.concatenate(
            [jnp.concatenate([ks[h][rows(c)] * (b_full[key, d] * jnp.exp(g_full[key, d])),
                              vs[h][rows(c)] * b_full[key, d]], axis=1) for d in range(N_DIR)], axis=0)
        t_sel = block_diag(t_mat[key].astype(BF16))
        wu[key] = jnp.dot(t_sel, kv.astype(BF16), preferred_element_type=F32)

    for c in range(n_chunks):
        egl_rows = []
        for d in range(N_DIR):
            for h in range(GDN_HEADS):
                key = (c, h)
                j = d * GDN_HEADS + h
                wu_d = wu[key][d * CHUNK:(d + 1) * CHUNK]
                w_ref[0, j, rows(c), :] = wu_d[:, :GDN_HEAD_DIM].astype(w_ref.dtype)
                u_ref[0, j, rows(c), :] = wu_d[:, GDN_HEAD_DIM:].astype(u_ref.dtype)
                qd_ref[0, j, rows(c), :] = (qs[h][rows(c)] * jnp.exp(g_full[key, d])).astype(qd_ref.dtype)
                egl_rows.append(jnp.broadcast_to(jnp.exp(glast[key, d]), (1, LANES)))
        egl_ref[0, c] = jnp.concatenate(egl_rows, axis=0)
        for h in range(GDN_HEADS):
            qk_ref[0, h, rows(c), :] = (qkk[c, h][:CHUNK] * decay[c, h]).astype(qk_ref.dtype)
    for cp in range(n_chunks // 2):
        for d in range(N_DIR):
            for h in range(GDN_HEADS):
                fac = jnp.concatenate([jnp.exp(glast[(c, h), d] - grow[(c, h), d]) for c in (2 * cp, 2 * cp + 1)],
                                      axis=1)
                kdt = kts[h][:, 2 * cp * CHUNK:(2 * cp + 2) * CHUNK] * fac
                kdt_ref[0, d * GDN_HEADS + h, cp] = kdt.astype(kdt_ref.dtype)


def _gdn_prep(yn, ab, gate_par):
    b, s, _ = ab.shape
    nt = s // PREP_T
    cpb = PREP_T // CHUNK
    nc = s // CHUNK
    chain = lambda last: pl.BlockSpec((1, N_CHAIN, PREP_T, last), lambda bi, i: (bi, 0, i, 0))
    return pl.pallas_call(
        _gprep_kernel,
        grid=(b, nt),
        in_specs=[pl.BlockSpec((3 * GDN_W // LANES, PREP_T, LANES), lambda bi, i: (0, bi * nt + i, 0)),
                  pl.BlockSpec((1, PREP_T, LANES), lambda bi, i: (bi, i, 0)),
                  pl.BlockSpec((SUBLANES, LANES), lambda bi, i: (0, 0))],
        out_specs=[chain(GDN_HEAD_DIM), chain(GDN_HEAD_DIM), chain(GDN_HEAD_DIM),
                   pl.BlockSpec((1, N_CHAIN, cpb // 2, GDN_HEAD_DIM, 2 * CHUNK), lambda bi, i: (bi, 0, i, 0, 0)),
                   pl.BlockSpec((1, GDN_HEADS, PREP_T, LANES), lambda bi, i: (bi, 0, i, 0)),
                   pl.BlockSpec((1, cpb, N_CHAIN, LANES), lambda bi, i: (bi, i, 0, 0))],
        out_shape=[jax.ShapeDtypeStruct((b, N_CHAIN, s, GDN_HEAD_DIM), BF16),
                   jax.ShapeDtypeStruct((b, N_CHAIN, s, GDN_HEAD_DIM), BF16),
                   jax.ShapeDtypeStruct((b, N_CHAIN, s, GDN_HEAD_DIM), BF16),
                   jax.ShapeDtypeStruct((b, N_CHAIN, nc // 2, GDN_HEAD_DIM, 2 * CHUNK), BF16),
                   jax.ShapeDtypeStruct((b, GDN_HEADS, s, LANES), BF16),
                   jax.ShapeDtypeStruct((b, nc, N_CHAIN, LANES), F32)],
        compiler_params=pltpu.CompilerParams(dimension_semantics=("arbitrary", "arbitrary"),
                                             vmem_limit_bytes=VMEM_LIMIT),
        name="gdn_prep",
    )(yn, ab, gate_par)


def _gscan_kernel(egl_ref, wf_ref, wb_ref, uf_ref, ub_ref, qf_ref, qb_ref, kf_ref, kb_ref,
                  pf_ref, pb_ref, of_ref, ob_ref, state_ref):
    t = pl.program_id(0)
    n_steps = pl.num_programs(0)
    nc = n_steps * SCAN_CHUNKS
    n_batch = wf_ref.shape[0]

    @pl.when(t == 0)
    def _():
        state_ref[...] = jnp.zeros_like(state_ref)

    dirs = ((wf_ref, uf_ref, qf_ref, kf_ref, pf_ref, of_ref), (wb_ref, ub_ref, qb_ref, kb_ref, pb_ref, ob_ref))
    chains = [(bi, d, h) for bi in range(n_batch) for d in range(N_DIR) for h in range(GDN_HEADS)]
    slot = lambda bi, d, h: (bi * N_DIR + d) * GDN_HEADS + h
    st = {key: state_ref[slot(*key)] for key in chains}

    for sub in range(SCAN_CHUNKS):
        local = (sub, SCAN_CHUNKS - 1 - sub)
        rows = [slice(c * CHUNK, (c + 1) * CHUNK) for c in local]
        chunk = (t * SCAN_CHUNKS + sub, nc - 1 - (t * SCAN_CHUNKS + sub))
        r = {}
        for key in chains:
            bi, d, h = key
            w_ref, _, q_ref = dirs[d][:3]
            wq = jnp.concatenate([w_ref[bi, h, rows[d]], q_ref[bi, h, rows[d]]], axis=0)
            r[key] = _dot(wq, st[key])
        v_pad, intra = {}, {}
        for key in chains:
            bi, d, h = key
            u_ref, p_ref = dirs[d][1], dirs[d][4]
            v_new = (u_ref[bi, h, rows[d]].astype(F32) - r[key][:CHUNK]).astype(BF16)
            zeros = jnp.zeros_like(v_new)
            v_pad[key] = (jnp.concatenate([v_new, zeros], axis=0), jnp.concatenate([zeros, v_new], axis=0))
            intra[key] = jnp.dot(p_ref[bi, h, rows[d]], v_pad[key][d], preferred_element_type=F32)
        for key in chains:
            bi, d, h = key
            k_ref = dirs[d][3]
            egl = egl_ref[(bi * nc + chunk[d]) * N_CHAIN + d * GDN_HEADS + h]
            st[key] = st[key] * egl + jnp.dot(k_ref[bi, h, local[d] // 2], v_pad[key][local[d] % 2],
                                              preferred_element_type=F32)
        for bi in range(n_batch):
            for d in range(N_DIR):
                o_ref = dirs[d][5]
                o_ref[bi, rows[d], :] = jnp.concatenate(
                    [r[bi, d, h][CHUNK:] + intra[bi, d, h] for h in range(GDN_HEADS)], axis=-1).astype(o_ref.dtype)

    for key in chains:
        state_ref[slot(*key)] = st[key]


def _gdn_scan(egl, w, u, qd, kdt, qk):
    b, _, s, _ = w.shape
    rows = SCAN_CHUNKS * CHUNK
    n_steps = s // rows
    fwd_i = lambda t: t
    bwd_i = lambda t: n_steps - 1 - t
    chain = lambda d, at, last: pl.BlockSpec((b, GDN_HEADS, rows, last), lambda t: (0, d, at(t), 0))
    kspec = lambda d, at: pl.BlockSpec((b, GDN_HEADS, SCAN_CHUNKS // 2, GDN_HEAD_DIM, 2 * CHUNK),
                                       lambda t: (0, d, at(t), 0, 0))
    dk = GDN_HEAD_DIM
    return pl.pallas_call(
        _gscan_kernel,
        grid=(n_steps,),
        in_specs=[pl.BlockSpec(memory_space=pltpu.SMEM),
                  chain(0, fwd_i, dk), chain(1, bwd_i, dk), chain(0, fwd_i, dk), chain(1, bwd_i, dk),
                  chain(0, fwd_i, dk), chain(1, bwd_i, dk), kspec(0, fwd_i), kspec(1, bwd_i),
                  chain(0, fwd_i, LANES), chain(0, bwd_i, LANES)],
        out_specs=[pl.BlockSpec((b, rows, GDN_W), lambda t: (0, fwd_i(t), 0)),
                   pl.BlockSpec((b, rows, GDN_W), lambda t: (0, bwd_i(t), 0))],
        out_shape=[jax.ShapeDtypeStruct((b, s, GDN_W), BF16),
                   jax.ShapeDtypeStruct((b, s, GDN_W), BF16)],
        scratch_shapes=[pltpu.VMEM((b * N_CHAIN, GDN_HEAD_DIM, GDN_HEAD_DIM), F32)],
        compiler_params=pltpu.CompilerParams(dimension_semantics=("arbitrary",),
                                             vmem_limit_bytes=VMEM_LIMIT),
        name="gdn_scan",
    )(egl, w, w, u, u, qd, qd, kdt, kdt, qk, qk)


def _cast_weights_once(layer, copies):
    jobs, used = [], {}
    for src, dst, stage, sem, chunk_rows in copies:
        for r0 in range(0, src.shape[1], chunk_rows):
            slot = used.get(id(stage), 0) % 2
            used[id(stage)] = used.get(id(stage), 0) + 1
            rows = pl.ds(r0, chunk_rows)
            jobs.append((pltpu.make_async_copy(src.at[layer, rows], stage.at[slot], sem.at[slot]),
                         stage, slot, dst, rows))
    jobs[0][0].start()
    for k, (copy, stage, slot, dst, rows) in enumerate(jobs):
        if k + 1 < len(jobs):
            jobs[k + 1][0].start()
        copy.wait()
        dst[rows, :] = stage[slot].astype(BF16)


def _ffn_kernel(layer, x_ref, attn_ref, of_ref, ob_ref, z_ref, gnw_ref, wo_hbm, fnw_ref, w1_hbm, w2_hbm, onw_ref,
                o_ref, wo_ref, w1_ref, w2_ref, stage_wide, stage_tall, sem_wide, sem_tall):
    @pl.when(pl.program_id(0) == 0)
    def _():
        _cast_weights_once(layer, [(w1_hbm, w1_ref, stage_wide, sem_wide, FFN_CAST_ROWS_WIDE),
                                   (w2_hbm, w2_ref, stage_tall, sem_tall, FFN_CAST_ROWS_TALL),
                                   (wo_hbm, wo_ref, stage_tall, sem_tall, FFN_CAST_ROWS_TALL)])

    subs = [slice(r0, r0 + FFN_SUB) for r0 in range(0, FFN_TM, FFN_SUB)]
    rms = lambda t: t * lax.rsqrt(jnp.mean(t * t, axis=-1, keepdims=True) + EPS)
    hres, hn, act, acc = {}, {}, {}, {}
    for r in subs:
        o = of_ref[r, :].astype(F32) + ob_ref[r, :].astype(F32)
        heads = [rms(o[:, h * GDN_HEAD_DIM:(h + 1) * GDN_HEAD_DIM]) * gnw_ref[...] for h in range(GDN_HEADS)]
        z = z_ref[r, :]
        gdn = jnp.concatenate(heads, axis=-1) * (z * jax.nn.sigmoid(z))
        hres[r.start] = (x_ref[r, :] + jnp.dot(attn_ref[r, :], wo_ref[:ATTN_Q, :], preferred_element_type=F32)
                         + _dot(gdn, wo_ref[ATTN_Q:, :]))
    for r in subs:
        hn[r.start] = (rms(hres[r.start]) * fnw_ref[...]).astype(BF16)
        act[r.start] = jnp.dot(hn[r.start], w1_ref[...], preferred_element_type=F32)
    for r in subs:
        a = jnp.square(jnp.maximum(act[r.start], 0.0)).astype(BF16)
        acc[r.start] = hres[r.start] + jnp.dot(a, w2_ref[...], preferred_element_type=F32)
    for r in subs:
        o_ref[r, :] = rms(acc[r.start]) * onw_ref[...]


def _out_ffn(x2, attn, o_f, o_b, z, gnw, wo, fnw, w1, w2, onw, layer):
    n = x2.shape[0]
    assert wo.shape[1:] == (D_MODEL, D_MODEL) and w1.shape[1:] == (D_MODEL, D_FF) and w2.shape[1:] == (D_FF, D_MODEL)
    row = lambda w: pl.BlockSpec((FFN_TM, w), lambda i: (i, 0))
    full = lambda a: pl.BlockSpec(a.shape, lambda i: (0, 0))
    hbm = pl.BlockSpec(memory_space=pl.ANY)
    return pl.pallas_call(
        functools.partial(_ffn_kernel, layer),
        grid=(n // FFN_TM,),
        in_specs=[row(D_MODEL), row(ATTN_Q), row(GDN_W), row(GDN_W), row(GDN_W),
                  full(gnw), hbm, full(fnw), hbm, hbm, full(onw)],
        out_specs=row(D_MODEL),
        out_shape=jax.ShapeDtypeStruct((n, D_MODEL), F32),
        scratch_shapes=[pltpu.VMEM((D_MODEL, D_MODEL), BF16),
                        pltpu.VMEM((D_MODEL, D_FF), BF16),
                        pltpu.VMEM((D_FF, D_MODEL), BF16),
                        pltpu.VMEM((2, FFN_CAST_ROWS_WIDE, D_FF), F32),
                        pltpu.VMEM((2, FFN_CAST_ROWS_TALL, D_MODEL), F32),
                        pltpu.SemaphoreType.DMA((2,)),
                        pltpu.SemaphoreType.DMA((2,))],
        compiler_params=pltpu.CompilerParams(dimension_semantics=("arbitrary",),
                                             vmem_limit_bytes=VMEM_LIMIT),
        name="out_ffn",
    )(x2, attn, o_f, o_b, z, gnw, wo, fnw, w1, w2, onw)


def _layer(h, band, norm_mix_w, w_in, layer, attn_sink, conv_w, gdn_a_log, gdn_dt_bias, gdn_norm_w,
           w_out, norm_ffn_w, w_ffn_in, w_ffn_out, out_norm_w):
    b, s, _ = h.shape
    n = b * s
    x2 = h.reshape(n, D_MODEL)
    conv_pad = jnp.zeros((SUBLANES, 3 * GDN_W), F32).at[:CONV_K].set(conv_w)
    q_a, k_a, v_a, yn, z_g, ab = _proj(x2, norm_mix_w.reshape(1, D_MODEL), w_in, conv_pad, layer, s)

    attn = _attention(q_a, k_a, v_a, band, attn_sink, b)

    gate_par = jnp.zeros((SUBLANES, LANES), F32)
    gate_par = gate_par.at[0, :N_CHAIN].set(gdn_a_log.reshape(-1)).at[1, :N_CHAIN].set(gdn_dt_bias.reshape(-1))
    w_c, u_c, q_dec, k_dec_t, qk, egl = _gdn_prep(yn, ab.reshape(b, s, LANES), gate_par)
    o_f, o_b = _gdn_scan(egl[..., 0].reshape(-1), w_c, u_c, q_dec, k_dec_t, qk)

    out = _out_ffn(x2, attn, o_f.reshape(n, GDN_W), o_b.reshape(n, GDN_W), z_g,
                   gdn_norm_w.reshape(1, GDN_HEAD_DIM), w_out, norm_ffn_w.reshape(1, D_MODEL),
                   w_ffn_in, w_ffn_out, out_norm_w.reshape(1, D_MODEL), layer)
    return out.reshape(b, s, D_MODEL)


def kernel(x, norm_mix_w, w_in, rel_bias, attn_sink, conv_w, gdn_a_log, gdn_dt_bias, gdn_norm_w, w_out,
           norm_ffn_w, w_ffn_in, w_ffn_out, norm_final_w):
    depth = w_in.shape[0]
    assert depth == 1, "the fused output kernel applies the final norm after the single trunk layer"
    rel = (np.arange(3 * BLOCK)[None, :] - BLOCK) - np.arange(BLOCK)[:, None]
    bucket = _t5_buckets(jnp.asarray(rel, dtype=jnp.int32))
    band = _bias_band(rel_bias, bucket.T)
    return _layer(x, band, norm_mix_w[0], w_in, 0, attn_sink[0], conv_w[0], gdn_a_log[0], gdn_dt_bias[0],
                  gdn_norm_w[0], w_out, norm_ffn_w[0], w_ffn_in, w_ffn_out, norm_final_w)
```

```python
import functools
import math

import jax
import jax.numpy as jnp
import numpy as np
from jax import lax
from jax.experimental import pallas as pl
from jax.experimental.pallas import tpu as pltpu

F32 = jnp.float32
BF16 = jnp.bfloat16

D_MODEL = 1024
ATTN_HEADS = 8
ATTN_KV_HEADS = 2
ATTN_HEAD_DIM = 64
ATTN_GROUP = ATTN_HEADS // ATTN_KV_HEADS
WINDOW = 128
BLOCK = 128
N_BUCKETS = 32
MAX_DISTANCE = 128
GDN_HEADS = 4
GDN_HEAD_DIM = 128
CONV_K = 5
CHUNK = 64
N_DIR = 2
N_CHAIN = N_DIR * GDN_HEADS
D_FF = 4 * D_MODEL
EPS = 1e-6
ATTN_Q = ATTN_HEADS * ATTN_HEAD_DIM
ATTN_KV = ATTN_KV_HEADS * ATTN_HEAD_DIM
GDN_W = GDN_HEADS * GDN_HEAD_DIM
LANES = 128
SUBLANES = 8
VMEM_LIMIT = 56 * 1024 * 1024

PROJ_TM = 1024
PROJ_SUB = 256
PROJ_CAST_ROWS = 128
ATTN_QB = 8
PREP_T = 512
ROW_STRIDE = 4
SCAN_CHUNKS = 8
FFN_TM = 512
FFN_SUB = 256
FFN_CAST_ROWS_WIDE = 128
FFN_CAST_ROWS_TALL = 512


def _dot(a, b):
    return jnp.dot(a.astype(BF16), b.astype(BF16), preferred_element_type=F32)


def _dot_nt(a, b):
    return lax.dot_general(a.astype(BF16), b.astype(BF16), (((1,), (1,)), ((), ())),
                           preferred_element_type=F32)


def _split3(x):
    hi = x.astype(BF16)
    r1 = x - hi.astype(F32)
    mid = r1.astype(BF16)
    lo = (r1 - mid.astype(F32)).astype(BF16)
    return hi, mid, lo


def _dot01_right(x, m01):
    hi, mid, lo = _split3(x)
    d = lambda p: jnp.dot(p, m01, preferred_element_type=F32)
    return d(hi) + d(mid) + d(lo)


def _conv_silu_norm(xe_ref, cw_ref, yn_ref, r0, n_out):
    halo = CONV_K // 2
    n_rows = n_out // ROW_STRIDE
    n_slab = 3 * GDN_W // LANES
    units = [(sb, ph) for sb in range(n_slab) for ph in range(ROW_STRIDE)]
    yv = {}
    for sb, ph in units:
        lanes = slice(sb * LANES, (sb + 1) * LANES)
        acc = None
        for j in range(CONV_K):
            win = xe_ref[sb, pl.ds(SUBLANES - halo + j + ph, n_rows, stride=ROW_STRIDE), :]
            term = cw_ref[j:j + 1, lanes] * win
            acc = term if acc is None else acc + term
        yv[sb, ph] = acc
    for key in units:
        yv[key] = yv[key] * jax.nn.sigmoid(yv[key])
    for sb, ph in units:
        if sb < 2 * GDN_HEADS:
            scale = lax.rsqrt(jnp.sum(yv[sb, ph] * yv[sb, ph], axis=-1, keepdims=True) + EPS)
            if sb < GDN_HEADS:
                scale = scale * (GDN_HEAD_DIM ** -0.5)
            yv[sb, ph] = yv[sb, ph] * scale
    for sb, ph in units:
        yn_ref[sb, pl.ds(r0 + ph, n_rows, stride=ROW_STRIDE), :] = yv[sb, ph]


def _proj_kernel(layer, tiles_per_seq, x_ref, xp_ref, xn_ref, nw_ref, wt_hbm, cw_ref,
                 qa_ref, ka_ref, va_ref, yn_ref, z_ref, ab_ref, wb_ref, stage_ref, sem, *xe_refs):
    d_in = wt_hbm.shape[1]
    d_main = d_in // LANES * LANES
    step = pl.program_id(0)

    @pl.when(step == 0)
    def _():
        starts = list(range(0, d_main, PROJ_CAST_ROWS)) + [d_main]
        sizes = [PROJ_CAST_ROWS] * (len(starts) - 1) + [d_in - d_main]
        copies = [pltpu.make_async_copy(wt_hbm.at[layer, pl.ds(c0, nr)], stage_ref.at[k % 2, pl.ds(0, nr)],
                                        sem.at[k % 2]) for k, (c0, nr) in enumerate(zip(starts, sizes))]
        lane = lax.broadcasted_iota(jnp.int32, (D_MODEL, PROJ_CAST_ROWS), 1)
        copies[0].start()
        for k, (c0, nr) in enumerate(zip(starts, sizes)):
            if k + 1 < len(copies):
                copies[k + 1].start()
            copies[k].wait()
            cols = stage_ref[k % 2].T
            if nr < PROJ_CAST_ROWS:
                cols = jnp.where(lane < nr, cols, 0.0)
            wb_ref[:, c0:c0 + PROJ_CAST_ROWS] = cols.astype(BF16)

    o_q, o_k, o_v, o_g, o_z, o_ab = 0, ATTN_Q, ATTN_Q + ATTN_KV, ATTN_Q + 2 * ATTN_KV, \
        ATTN_Q + 2 * ATTN_KV + 3 * GDN_W, ATTN_Q + 2 * ATTN_KV + 4 * GDN_W
    n_slab = 3 * GDN_W // LANES

    def normed(x):
        ms = jnp.mean(x * x, axis=-1, keepdims=True)
        return (x * lax.rsqrt(ms + EPS) * nw_ref[...]).astype(BF16)

    n_sub = PROJ_TM // PROJ_SUB

    def to_slabs(xe_ref, y, row0):
        for sb in range(n_slab):
            xe_ref[sb, row0:row0 + y.shape[0], :] = y[:, sb * LANES:(sb + 1) * LANES]

    def put(k, y):
        to_slabs(xe_refs[k], y, SUBLANES)
        if k > 0:
            to_slabs(xe_refs[k - 1], y[:SUBLANES], SUBLANES + PROJ_SUB)
        if k + 1 < n_sub:
            to_slabs(xe_refs[k + 1], y[PROJ_SUB - SUBLANES:], 0)

    i = step % tiles_per_seq
    outs = ((qa_ref, o_q, ATTN_Q, ATTN_HEAD_DIM ** -0.5), (ka_ref, o_k, ATTN_KV, None), (va_ref, o_v, ATTN_KV, None),
            (z_ref, o_z, GDN_W, None), (ab_ref, o_ab, LANES, None))
    for k in range(n_sub):
        rows = slice(k * PROJ_SUB, (k + 1) * PROJ_SUB)
        if k == 0:
            xg = normed(jnp.concatenate([xp_ref[...], xn_ref[...], x_ref[rows, :]], axis=0))
            yg = jnp.dot(xg, wb_ref[:, o_g:o_z], preferred_element_type=F32)
            to_slabs(xe_refs[0], jnp.where(i > 0, yg[:SUBLANES], 0.0), 0)
            to_slabs(xe_refs[n_sub - 1], jnp.where(i < tiles_per_seq - 1, yg[SUBLANES:2 * SUBLANES], 0.0),
                     SUBLANES + PROJ_SUB)
            put(0, yg[2 * SUBLANES:])
            xn = xg[2 * SUBLANES:]
        else:
            xn = normed(x_ref[rows, :])
            put(k, jnp.dot(xn, wb_ref[:, o_g:o_z], preferred_element_type=F32))
            _conv_silu_norm(xe_refs[k - 1], cw_ref, yn_ref, (k - 1) * PROJ_SUB, PROJ_SUB)
        for ref, c0, width, scale in outs:
            y = jnp.dot(xn, wb_ref[:, c0:c0 + width], preferred_element_type=F32)
            if scale is not None:
                y = y * scale
            ref[rows, :] = y.astype(ref.dtype)
    _conv_silu_norm(xe_refs[n_sub - 1], cw_ref, yn_ref, PROJ_TM - PROJ_SUB, PROJ_SUB)


def _proj(x2, norm_w, w_in, conv_w, layer, seq_len):
    n = x2.shape[0]
    d_in = w_in.shape[2]
    d_pad = d_in // LANES * LANES + LANES
    assert d_pad == ATTN_Q + 2 * ATTN_KV + 4 * GDN_W + LANES and d_in - (d_pad - LANES) == 2 * N_CHAIN
    assert seq_len % PROJ_TM == 0
    w_t = jnp.swapaxes(w_in, 1, 2)
    nh8 = PROJ_TM // SUBLANES
    row = lambda w: pl.BlockSpec((PROJ_TM, w), lambda i: (i, 0))
    return pl.pallas_call(
        functools.partial(_proj_kernel, layer, seq_len // PROJ_TM),
        grid=(n // PROJ_TM,),
        in_specs=[row(D_MODEL),
                  pl.BlockSpec((SUBLANES, D_MODEL), lambda i: (jnp.maximum(i * nh8 - 1, 0), 0)),
                  pl.BlockSpec((SUBLANES, D_MODEL), lambda i: (jnp.minimum((i + 1) * nh8, n // SUBLANES - 1), 0)),
                  pl.BlockSpec((1, D_MODEL), lambda i: (0, 0)),
                  pl.BlockSpec(memory_space=pl.ANY),
                  pl.BlockSpec((SUBLANES, 3 * GDN_W), lambda i: (0, 0))],
        out_specs=[row(ATTN_Q), row(ATTN_KV), row(ATTN_KV),
                   pl.BlockSpec((3 * GDN_W // LANES, PROJ_TM, LANES), lambda i: (0, i, 0)),
                   row(GDN_W), row(LANES)],
        out_shape=[jax.ShapeDtypeStruct((n, ATTN_Q), BF16),
                   jax.ShapeDtypeStruct((n, ATTN_KV), BF16),
                   jax.ShapeDtypeStruct((n, ATTN_KV), BF16),
                   jax.ShapeDtypeStruct((3 * GDN_W // LANES, n, LANES), F32),
                   jax.ShapeDtypeStruct((n, GDN_W), F32),
                   jax.ShapeDtypeStruct((n, LANES), F32)],
        scratch_shapes=[pltpu.VMEM((D_MODEL, d_pad), BF16),
                        pltpu.VMEM((2, PROJ_CAST_ROWS, D_MODEL), F32),
                        pltpu.SemaphoreType.DMA((2,))]
        + [pltpu.VMEM((3 * GDN_W // LANES, PROJ_SUB + 2 * SUBLANES, LANES), F32)] * (PROJ_TM // PROJ_SUB),
        compiler_params=pltpu.CompilerParams(dimension_semantics=("arbitrary",),
                                             vmem_limit_bytes=VMEM_LIMIT),
        name="proj",
    )(x2, x2, x2, norm_w, w_t, conv_w)


def _bias_kernel(relb_ref, bucket_ref, o_ref):
    bucket = bucket_ref[...]
    key = lax.broadcasted_iota(jnp.int32, (3 * BLOCK, BLOCK), 0)
    qry = lax.broadcasted_iota(jnp.int32, (3 * BLOCK, BLOCK), 1)
    in_window = jnp.abs(key - BLOCK - qry) <= WINDOW
    for h in range(ATTN_HEADS):
        acc = jnp.zeros((3 * BLOCK, BLOCK), F32)
        for b in range(N_BUCKETS):
            acc = jnp.where(bucket == b, relb_ref[b, h], acc)
        o_ref[h // 2, :, (h % 2) * BLOCK:(h % 2 + 1) * BLOCK] = jnp.where(in_window, acc, -1e30)


def _bias_band(rel_bias, bucket_t):
    shape = (ATTN_HEADS // 2, 3 * BLOCK, 2 * BLOCK)
    return pl.pallas_call(
        _bias_kernel,
        in_specs=[pl.BlockSpec(memory_space=pltpu.SMEM),
                  pl.BlockSpec((3 * BLOCK, BLOCK), lambda: (0, 0))],
        out_specs=pl.BlockSpec(shape, lambda: (0, 0, 0)),
        out_shape=jax.ShapeDtypeStruct(shape, F32),
        name="bias_band",
    )(rel_bias, bucket_t)


def _t5_buckets(rel):
    nb = N_BUCKETS // 2
    max_exact = nb // 2
    base = jnp.where(rel > 0, nb, 0)
    n = jnp.abs(rel)
    log_ratio = jnp.log(jnp.maximum(n, 1).astype(jnp.float32) / max_exact) / math.log(MAX_DISTANCE / max_exact)
    large = jnp.minimum(max_exact + (log_ratio * (nb - max_exact)).astype(jnp.int32), nb - 1)
    return base + jnp.where(n < max_exact, n, large)


def _attn_kernel(sink_ref, q_ref, kp_ref, kc_ref, kn_ref, vp_ref, vc_ref, vn_ref, bias_ref, o_ref):
    n = pl.program_id(1)
    last = pl.num_programs(1) - 1
    kband = jnp.concatenate([kp_ref[...], kc_ref[...], kn_ref[...]], axis=0)
    vband = jnp.concatenate([vp_ref[...], vc_ref[...], vn_ref[...]], axis=0)
    vband_t = vband.astype(F32).T.astype(BF16)
    key = lax.broadcasted_iota(jnp.int32, (3 * BLOCK, 1), 0)
    first_head = lax.broadcasted_iota(jnp.int32, (1, 2 * BLOCK), 1) < BLOCK
    head = lambda t, i: t[:, i * ATTN_HEAD_DIM:(i + 1) * ATTN_HEAD_DIM]
    n_pairs = ATTN_HEADS // 2
    kv_of = lambda pr: (2 * pr) // ATTN_GROUP
    units = [(j, pr) for j in range(ATTN_QB) for pr in range(n_pairs)]
    band_rows = lambda j: slice(j * BLOCK, (j + 3) * BLOCK)

    scores = {}
    for j, pr in units:
        qj = q_ref[j * BLOCK:(j + 1) * BLOCK, :]
        q2 = jnp.concatenate([head(qj, 2 * pr), head(qj, 2 * pr + 1)], axis=0)
        scores[j, pr] = _dot_nt(head(kband, kv_of(pr))[band_rows(j)], q2)
    probs, dens = {}, {}
    for j, pr in units:
        s = scores[j, pr] + bias_ref[pr]
        if j == 0:
            s = jnp.where((key < BLOCK) & (n == 0), -1e30, s)
        if j == ATTN_QB - 1:
            s = jnp.where((key >= 2 * BLOCK) & (n == last), -1e30, s)
        sink = jnp.where(first_head, sink_ref[2 * pr], sink_ref[2 * pr + 1])
        m = jnp.maximum(jnp.max(s, axis=0, keepdims=True), sink)
        p = jnp.exp(s - m)
        dens[j, pr] = jnp.sum(p, axis=0, keepdims=True) + jnp.exp(sink - m)
        probs[j, pr] = p.astype(BF16)
    outs_t = {}
    for j, pr in units:
        kv = kv_of(pr)
        v_t = vband_t[kv * ATTN_HEAD_DIM:(kv + 1) * ATTN_HEAD_DIM, band_rows(j)]
        outs_t[j, pr] = jnp.dot(v_t, probs[j, pr], preferred_element_type=F32) / dens[j, pr]
    for j in range(ATTN_QB):
        o_t = jnp.concatenate([outs_t[j, pr][:, half * BLOCK:(half + 1) * BLOCK]
                               for pr in range(n_pairs) for half in range(2)], axis=0)
        o_ref[j * BLOCK:(j + 1) * BLOCK, :] = o_t.T.astype(o_ref.dtype)


def _attention(q_a, k_a, v_a, band, sink, batch):
    n_tok = q_a.shape[0]
    nb = n_tok // batch // BLOCK
    rows = ATTN_QB * BLOCK
    steps = nb // ATTN_QB
    kv_spec = lambda r, f: pl.BlockSpec((r, ATTN_KV), f)
    prev = lambda bi, n: (bi * nb + jnp.maximum(n * ATTN_QB - 1, 0), 0)
    cur = lambda bi, n: (bi * steps + n, 0)
    nxt = lambda bi, n: (bi * nb + jnp.minimum((n + 1) * ATTN_QB, nb - 1), 0)
    return pl.pallas_call(
        _attn_kernel,
        grid=(batch, steps),
        in_specs=[pl.BlockSpec(memory_space=pltpu.SMEM),
                  pl.BlockSpec((rows, ATTN_Q), cur),
                  kv_spec(BLOCK, prev), kv_spec(rows, cur), kv_spec(BLOCK, nxt),
                  kv_spec(BLOCK, prev), kv_spec(rows, cur), kv_spec(BLOCK, nxt),
                  pl.BlockSpec(band.shape, lambda bi, n: (0, 0, 0))],
        out_specs=pl.BlockSpec((rows, ATTN_Q), cur),
        out_shape=jax.ShapeDtypeStruct((n_tok, ATTN_Q), BF16),
        compiler_params=pltpu.CompilerParams(dimension_semantics=("arbitrary", "arbitrary"),
                                             vmem_limit_bytes=VMEM_LIMIT),
        name="attn",
    )(sink, q_a, k_a, k_a, k_a, v_a, v_a, v_a, band)


def _gprep_kernel(yn_ref, ab_ref, gp_ref, w_ref, u_ref, qd_ref, kdt_ref, qk_ref, egl_ref):
    t_len = PREP_T

    ab = ab_ref[0]
    sp_in = ab + gp_ref[1:2, :]
    softplus = jnp.maximum(sp_in, 0.0) + jnp.log1p(jnp.exp(-jnp.abs(sp_in)))
    g = -jnp.exp(gp_ref[0:1, :]) * softplus
    beta = jax.nn.sigmoid(ab)

    r_t = lax.broadcasted_iota(jnp.int32, (t_len, t_len), 0)
    c_t = lax.broadcasted_iota(jnp.int32, (t_len, t_len), 1)
    same = (r_t // CHUNK) == (c_t // CHUNK)
    lower = jnp.where(same & (r_t >= c_t), 1.0, 0.0).astype(BF16)
    upper = jnp.where(same & (r_t <= c_t), 1.0, 0.0).astype(BF16)
    g_t = g.T[:2 * SUBLANES]
    cs_row = (_dot01_right(g_t, upper), _dot01_right(g_t, lower))
    pad_rows = jnp.zeros((LANES - 2 * SUBLANES, t_len), F32)
    cs_col = tuple(jnp.concatenate([r, pad_rows], axis=0).T for r in cs_row)

    qs = [yn_ref[h] for h in range(GDN_HEADS)]
    ks = [yn_ref[GDN_HEADS + h] for h in range(GDN_HEADS)]
    vs = [yn_ref[2 * GDN_HEADS + h] for h in range(GDN_HEADS)]
    kts = [kh.T for kh in ks]

    r_c = lax.broadcasted_iota(jnp.int32, (CHUNK, LANES), 0)
    lane = lax.broadcasted_iota(jnp.int32, (CHUNK, LANES), 1)
    is_fwd = lane < CHUNK
    c_c = lane % CHUNK
    eye = jnp.where(r_c == c_c, 1.0, 0.0).astype(F32)
    incl = (is_fwd & (r_c >= c_c)) | (~is_fwd & (r_c <= c_c))
    strict = (is_fwd & (r_c > c_c)) | (~is_fwd & (r_c < c_c))
    r_d = lax.broadcasted_iota(jnp.int32, (2 * CHUNK, LANES), 0)
    c_d = lax.broadcasted_iota(jnp.int32, (2 * CHUNK, LANES), 1)
    same_dir = (r_d // CHUNK) == (c_d // CHUNK)
    level_mask = lambda s_, r_, c_: ((r_ // (2 * s_)) == (c_ // (2 * s_))) & ((r_ // s_) != (c_ // s_))
    stack2 = lambda t: jnp.concatenate([t, t], axis=0)

    n_chunks = t_len // CHUNK
    rows = lambda c: slice(c * CHUNK, (c + 1) * CHUNK)
    pairs = [(c, h) for c in range(n_chunks) for h in range(GDN_HEADS)]
    qkk = {}
    for c, h in pairs:
        k16 = ks[h][rows(c)].astype(BF16)
        qk16 = jnp.concatenate([qs[h][rows(c)].astype(BF16), k16], axis=0)
        qkk[c, h] = _dot_nt(qk16, stack2(k16))

    bcast = lambda col: jnp.broadcast_to(col, (CHUNK, LANES))
    g_full, b_full, grow, glast, decay, a_mat, t_mat = {}, {}, {}, {}, {}, {}, {}
    for key in pairs:
        c, h = key
        for d in range(N_DIR):
            j = d * GDN_HEADS + h
            r_last = c * CHUNK + (CHUNK - 1 if d == 0 else 0)
            g_full[key, d] = bcast(cs_col[d][rows(c), j:j + 1])
            b_full[key, d] = bcast(beta[rows(c), SUBLANES + j:SUBLANES + j + 1])
            glast[key, d] = cs_col[d][r_last:r_last + 1, j:j + 1]
            grow[key, d] = cs_row[d][j:j + 1, rows(c)]
        gcol2 = jnp.where(is_fwd, g_full[key, 0], g_full[key, 1])
        bcol2 = jnp.where(is_fwd, b_full[key, 0], b_full[key, 1])
        grow2 = jnp.concatenate([grow[key, 0], grow[key, 1]], axis=1)
        decay[key] = jnp.exp(jnp.where(incl, gcol2 - grow2, -jnp.inf))
        a_mat[key] = jnp.where(strict, bcol2 * qkk[key][CHUNK:] * decay[key], 0.0)
        t_mat[key] = eye - jnp.where(level_mask(1, r_c, c_c), a_mat[key], 0.0)

    same_dir16 = jnp.where(same_dir, 1.0, 0.0).astype(BF16)
    block_diag = lambda t16: stack2(t16) * same_dir16
    a_bd = {key: block_diag(a_mat[key].astype(BF16)) for key in pairs}
    s = 2
    while s < CHUNK:
        lvl = level_mask(s, r_c, c_c)
        t16 = {key: t_mat[key].astype(BF16) for key in pairs}
        x_mat = {key: jnp.dot(t16[key], a_bd[key], preferred_element_type=F32) for key in pairs}
        y_mat = {key: jnp.dot(x_mat[key].astype(BF16), block_diag(t16[key]), preferred_element_type=F32)
                 for key in pairs}
        t_mat = {key: t_mat[key] - jnp.where(lvl, y_mat[key], 0.0) for key in pairs}
        s *= 2

    wu = {}
    for key in pairs:
        c, h = key
        kv = jnp.concatenate(
            [jnp.concatenate([ks[h][rows(c)] * (b_full[key, d] * jnp.exp(g_full[key, d])),
                              vs[h][rows(c)] * b_full[key, d]], axis=1) for d in range(N_DIR)], axis=0)
        t_sel = block_diag(t_mat[key].astype(BF16))
        wu[key] = jnp.dot(t_sel, kv.astype(BF16), preferred_element_type=F32)

    for c in range(n_chunks):
        egl_rows = []
        for d in range(N_DIR):
            for h in range(GDN_HEADS):
                key = (c, h)
                j = d * GDN_HEADS + h
                wu_d = wu[key][d * CHUNK:(d + 1) * CHUNK]
                w_ref[0, j, rows(c), :] = wu_d[:, :GDN_HEAD_DIM].astype(w_ref.dtype)
                u_ref[0, j, rows(c), :] = wu_d[:, GDN_HEAD_DIM:].astype(u_ref.dtype)
                qd_ref[0, j, rows(c), :] = (qs[h][rows(c)] * jnp.exp(g_full[key, d])).astype(qd_ref.dtype)
                egl_rows.append(jnp.broadcast_to(jnp.exp(glast[key, d]), (1, LANES)))
        egl_ref[0, c] = jnp.concatenate(egl_rows, axis=0)
        for h in range(GDN_HEADS):
            qk_ref[0, h, rows(c), :] = (qkk[c, h][:CHUNK] * decay[c, h]).astype(qk_ref.dtype)
    for cp in range(n_chunks // 2):
        for d in range(N_DIR):
            for h in range(GDN_HEADS):
                fac = jnp.concatenate([jnp.exp(glast[(c, h), d] - grow[(c, h), d]) for c in (2 * cp, 2 * cp + 1)],
                                      axis=1)
                kdt = kts[h][:, 2 * cp * CHUNK:(2 * cp + 2) * CHUNK] * fac
                kdt_ref[0, d * GDN_HEADS + h, cp] = kdt.astype(kdt_ref.dtype)


def _gdn_prep(yn, ab, gate_par):
    b, s, _ = ab.shape
    nt = s // PREP_T
    cpb = PREP_T // CHUNK
    nc = s // CHUNK
    chain = lambda last: pl.BlockSpec((1, N_CHAIN, PREP_T, last), lambda bi, i: (bi, 0, i, 0))
    return pl.pallas_call(
        _gprep_kernel,
        grid=(b, nt),
        in_specs=[pl.BlockSpec((3 * GDN_W // LANES, PREP_T, LANES), lambda bi, i: (0, bi * nt + i, 0)),
                  pl.BlockSpec((1, PREP_T, LANES), lambda bi, i: (bi, i, 0)),
                  pl.BlockSpec((SUBLANES, LANES), lambda bi, i: (0, 0))],
        out_specs=[chain(GDN_HEAD_DIM), chain(GDN_HEAD_DIM), chain(GDN_HEAD_DIM),
                   pl.BlockSpec((1, N_CHAIN, cpb // 2, GDN_HEAD_DIM, 2 * CHUNK), lambda bi, i: (bi, 0, i, 0, 0)),
                   pl.BlockSpec((1, GDN_HEADS, PREP_T, LANES), lambda bi, i: (bi, 0, i, 0)),
                   pl.BlockSpec((1, cpb, N_CHAIN, LANES), lambda bi, i: (bi, i, 0, 0))],
        out_shape=[jax.ShapeDtypeStruct((b, N_CHAIN, s, GDN_HEAD_DIM), BF16),
                   jax.ShapeDtypeStruct((b, N_CHAIN, s, GDN_HEAD_DIM), BF16),
                   jax.ShapeDtypeStruct((b, N_CHAIN, s, GDN_HEAD_DIM), BF16),
                   jax.ShapeDtypeStruct((b, N_CHAIN, nc // 2, GDN_HEAD_DIM, 2 * CHUNK), BF16),
                   jax.ShapeDtypeStruct((b, GDN_HEADS, s, LANES), BF16),
                   jax.ShapeDtypeStruct((b, nc, N_CHAIN, LANES), F32)],
        compiler_params=pltpu.CompilerParams(dimension_semantics=("arbitrary", "arbitrary"),
                                             vmem_limit_bytes=VMEM_LIMIT),
        name="gdn_prep",
    )(yn, ab, gate_par)


def _gscan_kernel(egl_ref, wf_ref, wb_ref, uf_ref, ub_ref, qf_ref, qb_ref, kf_ref, kb_ref,
                  pf_ref, pb_ref, of_ref, ob_ref, state_ref):
    t = pl.program_id(0)
    n_steps = pl.num_programs(0)
    nc = n_steps * SCAN_CHUNKS
    n_batch = wf_ref.shape[0]

    @pl.when(t == 0)
    def _():
        state_ref[...] = jnp.zeros_like(state_ref)

    dirs = ((wf_ref, uf_ref, qf_ref, kf_ref, pf_ref, of_ref), (wb_ref, ub_ref, qb_ref, kb_ref, pb_ref, ob_ref))
    chains = [(bi, d, h) for bi in range(n_batch) for d in range(N_DIR) for h in range(GDN_HEADS)]
    slot = lambda bi, d, h: (bi * N_DIR + d) * GDN_HEADS + h
    st = {key: state_ref[slot(*key)] for key in chains}

    for sub in range(SCAN_CHUNKS):
        local = (sub, SCAN_CHUNKS - 1 - sub)
        rows = [slice(c * CHUNK, (c + 1) * CHUNK) for c in local]
        chunk = (t * SCAN_CHUNKS + sub, nc - 1 - (t * SCAN_CHUNKS + sub))
        r = {}
        for key in chains:
            bi, d, h = key
            w_ref, _, q_ref = dirs[d][:3]
            wq = jnp.concatenate([w_ref[bi, h, rows[d]], q_ref[bi, h, rows[d]]], axis=0)
            r[key] = _dot(wq, st[key])
        v_pad, intra = {}, {}
        for key in chains:
            bi, d, h = key
            u_ref, p_ref = dirs[d][1], dirs[d][4]
            v_new = (u_ref[bi, h, rows[d]].astype(F32) - r[key][:CHUNK]).astype(BF16)
            zeros = jnp.zeros_like(v_new)
            v_pad[key] = (jnp.concatenate([v_new, zeros], axis=0), jnp.concatenate([zeros, v_new], axis=0))
            intra[key] = jnp.dot(p_ref[bi, h, rows[d]], v_pad[key][d], preferred_element_type=F32)
        for key in chains:
            bi, d, h = key
            k_ref = dirs[d][3]
            egl = egl_ref[(bi * nc + chunk[d]) * N_CHAIN + d * GDN_HEADS + h]
            st[key] = st[key] * egl + jnp.dot(k_ref[bi, h, local[d] // 2], v_pad[key][local[d] % 2],
                                              preferred_element_type=F32)
        for bi in range(n_batch):
            for d in range(N_DIR):
                o_ref = dirs[d][5]
                o_ref[bi, rows[d], :] = jnp.concatenate(
                    [r[bi, d, h][CHUNK:] + intra[bi, d, h] for h in range(GDN_HEADS)], axis=-1).astype(o_ref.dtype)

    for key in chains:
        state_ref[slot(*key)] = st[key]


def _gdn_scan(egl, w, u, qd, kdt, qk):
    b, _, s, _ = w.shape
    rows = SCAN_CHUNKS * CHUNK
    n_steps = s // rows
    fwd_i = lambda t: t
    bwd_i = lambda t: n_steps - 1 - t
    chain = lambda d, at, last: pl.BlockSpec((b, GDN_HEADS, rows, last), lambda t: (0, d, at(t), 0))
    kspec = lambda d, at: pl.BlockSpec((b, GDN_HEADS, SCAN_CHUNKS // 2, GDN_HEAD_DIM, 2 * CHUNK),
                                       lambda t: (0, d, at(t), 0, 0))
    dk = GDN_HEAD_DIM
    return pl.pallas_call(
        _gscan_kernel,
        grid=(n_steps,),
        in_specs=[pl.BlockSpec(memory_space=pltpu.SMEM),
                  chain(0, fwd_i, dk), chain(1, bwd_i, dk), chain(0, fwd_i, dk), chain(1, bwd_i, dk),
                  chain(0, fwd_i, dk), chain(1, bwd_i, dk), kspec(0, fwd_i), kspec(1, bwd_i),
                  chain(0, fwd_i, LANES), chain(0, bwd_i, LANES)],
        out_specs=[pl.BlockSpec((b, rows, GDN_W), lambda t: (0, fwd_i(t), 0)),
                   pl.BlockSpec((b, rows, GDN_W), lambda t: (0, bwd_i(t), 0))],
        out_shape=[jax.ShapeDtypeStruct((b, s, GDN_W), BF16),
                   jax.ShapeDtypeStruct((b, s, GDN_W), BF16)],
        scratch_shapes=[pltpu.VMEM((b * N_CHAIN, GDN_HEAD_DIM, GDN_HEAD_DIM), F32)],
        compiler_params=pltpu.CompilerParams(dimension_semantics=("arbitrary",),
                                             vmem_limit_bytes=VMEM_LIMIT),
        name="gdn_scan",
    )(egl, w, w, u, u, qd, qd, kdt, kdt, qk, qk)


def _cast_weights_once(layer, copies):
    jobs, used = [], {}
    for src, dst, stage, sem, chunk_rows in copies:
        for r0 in range(0, src.shape[1], chunk_rows):
            slot = used.get(id(stage), 0) % 2
            used[id(stage)] = used.get(id(stage), 0) + 1
            rows = pl.ds(r0, chunk_rows)
            jobs.append((pltpu.make_async_copy(src.at[layer, rows], stage.at[slot], sem.at[slot]),
                         stage, slot, dst, rows))
    jobs[0][0].start()
    for k, (copy, stage, slot, dst, rows) in enumerate(jobs):
        if k + 1 < len(jobs):
            jobs[k + 1][0].start()
        copy.wait()
        dst[rows, :] = stage[slot].astype(BF16)


def _ffn_kernel(layer, x_ref, attn_ref, of_ref, ob_ref, z_ref, gnw_ref, wo_hbm, fnw_ref, w1_hbm, w2_hbm, onw_ref,
                o_ref, wo_ref, w1_ref, w2_ref, stage_wide, stage_tall, sem_wide, sem_tall):
    @pl.when(pl.program_id(0) == 0)
    def _():
        _cast_weights_once(layer, [(w1_hbm, w1_ref, stage_wide, sem_wide, FFN_CAST_ROWS_WIDE),
                                   (w2_hbm, w2_ref, stage_tall, sem_tall, FFN_CAST_ROWS_TALL),
                                   (wo_hbm, wo_ref, stage_tall, sem_tall, FFN_CAST_ROWS_TALL)])

    subs = [slice(r0, r0 + FFN_SUB) for r0 in range(0, FFN_TM, FFN_SUB)]
    rms = lambda t: t * lax.rsqrt(jnp.mean(t * t, axis=-1, keepdims=True) + EPS)
    hres, hn, act, acc = {}, {}, {}, {}
    for r in subs:
        o = of_ref[r, :].astype(F32) + ob_ref[r, :].astype(F32)
        heads = [rms(o[:, h * GDN_HEAD_DIM:(h + 1) * GDN_HEAD_DIM]) * gnw_ref[...] for h in range(GDN_HEADS)]
        z = z_ref[r, :]
        gdn = jnp.concatenate(heads, axis=-1) * (z * jax.nn.sigmoid(z))
        hres[r.start] = (x_ref[r, :] + jnp.dot(attn_ref[r, :], wo_ref[:ATTN_Q, :], preferred_element_type=F32)
                         + _dot(gdn, wo_ref[ATTN_Q:, :]))
    for r in subs:
        hn[r.start] = (rms(hres[r.start]) * fnw_ref[...]).astype(BF16)
        act[r.start] = jnp.dot(hn[r.start], w1_ref[...], preferred_element_type=F32)
    for r in subs:
        a = jnp.square(jnp.maximum(act[r.start], 0.0)).astype(BF16)
        acc[r.start] = hres[r.start] + jnp.dot(a, w2_ref[...], preferred_element_type=F32)
    for r in subs:
        o_ref[r, :] = rms(acc[r.start]) * onw_ref[...]


def _out_ffn(x2, attn, o_f, o_b, z, gnw, wo, fnw, w1, w2, onw, layer):
    n = x2.shape[0]
    assert wo.shape[1:] == (D_MODEL, D_MODEL) and w1.shape[1:] == (D_MODEL, D_FF) and w2.shape[1:] == (D_FF, D_MODEL)
    row = lambda w: pl.BlockSpec((FFN_TM, w), lambda i: (i, 0))
    full = lambda a: pl.BlockSpec(a.shape, lambda i: (0, 0))
    hbm = pl.BlockSpec(memory_space=pl.ANY)
    return pl.pallas_call(
        functools.partial(_ffn_kernel, layer),
        grid=(n // FFN_TM,),
        in_specs=[row(D_MODEL), row(ATTN_Q), row(GDN_W), row(GDN_W), row(GDN_W),
                  full(gnw), hbm, full(fnw), hbm, hbm, full(onw)],
        out_specs=row(D_MODEL),
        out_shape=jax.ShapeDtypeStruct((n, D_MODEL), F32),
        scratch_shapes=[pltpu.VMEM((D_MODEL, D_MODEL), BF16),
                        pltpu.VMEM((D_MODEL, D_FF), BF16),
                        pltpu.VMEM((D_FF, D_MODEL), BF16),
                        pltpu.VMEM((2, FFN_CAST_ROWS_WIDE, D_FF), F32),
                        pltpu.VMEM((2, FFN_CAST_ROWS_TALL, D_MODEL), F32),
                        pltpu.SemaphoreType.DMA((2,)),
                        pltpu.SemaphoreType.DMA((2,))],
        compiler_params=pltpu.CompilerParams(dimension_semantics=("arbitrary",),
                                             vmem_limit_bytes=VMEM_LIMIT),
        name="out_ffn",
    )(x2, attn, o_f, o_b, z, gnw, wo, fnw, w1, w2, onw)


def _layer(h, band, norm_mix_w, w_in, layer, attn_sink, conv_w, gdn_a_log, gdn_dt_bias, gdn_norm_w,
           w_out, norm_ffn_w, w_ffn_in, w_ffn_out, out_norm_w):
    b, s, _ = h.shape
    n = b * s
    x2 = h.reshape(n, D_MODEL)
    conv_pad = jnp.zeros((SUBLANES, 3 * GDN_W), F32).at[:CONV_K].set(conv_w)
    q_a, k_a, v_a, yn, z_g, ab = _proj(x2, norm_mix_w.reshape(1, D_MODEL), w_in, conv_pad, layer, s)

    attn = _attention(q_a, k_a, v_a, band, attn_sink, b)

    gate_par = jnp.zeros((SUBLANES, LANES), F32)
    gate_par = gate_par.at[0, :N_CHAIN].set(gdn_a_log.reshape(-1)).at[1, :N_CHAIN].set(gdn_dt_bias.reshape(-1))
    w_c, u_c, q_dec, k_dec_t, qk, egl = _gdn_prep(yn, ab.reshape(b, s, LANES), gate_par)
    o_f, o_b = _gdn_scan(egl[..., 0].reshape(-1), w_c, u_c, q_dec, k_dec_t, qk)

    out = _out_ffn(x2, attn, o_f.reshape(n, GDN_W), o_b.reshape(n, GDN_W), z_g,
                   gdn_norm_w.reshape(1, GDN_HEAD_DIM), w_out, norm_ffn_w.reshape(1, D_MODEL),
                   w_ffn_in, w_ffn_out, out_norm_w.reshape(1, D_MODEL), layer)
    return out.reshape(b, s, D_MODEL)


def kernel(x, norm_mix_w, w_in, rel_bias, attn_sink, conv_w, gdn_a_log, gdn_dt_bias, gdn_norm_w, w_out,
           norm_ffn_w, w_ffn_in, w_ffn_out, norm_final_w):
    depth = w_in.shape[0]
    assert depth == 1, "the fused output kernel applies the final norm after the single trunk layer"
    rel = (np.arange(3 * BLOCK)[None, :] - BLOCK) - np.arange(BLOCK)[:, None]
    bucket = _t5_buckets(jnp.asarray(rel, dtype=jnp.int32))
    band = _bias_band(rel_bias, bucket.T)
    return _layer(x, band, norm_mix_w[0], w_in, 0, attn_sink[0], conv_w[0], gdn_a_log[0], gdn_dt_bias[0],
                  gdn_norm_w[0], w_out, norm_ffn_w[0], w_ffn_in, w_ffn_out, norm_final_w)
```

```python
import functools
import math

import jax
import jax.numpy as jnp
import numpy as np
from jax import lax
from jax.experimental import pallas as pl
from jax.experimental.pallas import tpu as pltpu

F32 = jnp.float32
BF16 = jnp.bfloat16

D_MODEL = 1024
ATTN_HEADS = 8
ATTN_KV_HEADS = 2
ATTN_HEAD_DIM = 64
ATTN_GROUP = ATTN_HEADS // ATTN_KV_HEADS
WINDOW = 128
BLOCK = 128
N_BUCKETS = 32
MAX_DISTANCE = 128
GDN_HEADS = 4
GDN_HEAD_DIM = 128
CONV_K = 5
CHUNK = 64
N_DIR = 2
N_CHAIN = N_DIR * GDN_HEADS
D_FF = 4 * D_MODEL
EPS = 1e-6
ATTN_Q = ATTN_HEADS * ATTN_HEAD_DIM
ATTN_KV = ATTN_KV_HEADS * ATTN_HEAD_DIM
GDN_W = GDN_HEADS * GDN_HEAD_DIM
LANES = 128
SUBLANES = 8
VMEM_LIMIT = 56 * 1024 * 1024

PROJ_TM = 1024
PROJ_SUB = 256
PROJ_CAST_ROWS = 128
ATTN_QB = 16
PREP_T = 512
ROW_STRIDE = 4
SCAN_CHUNKS = 16
FFN_TM = 512
FFN_SUB = 256
FFN_CAST_ROWS_WIDE = 128
FFN_CAST_ROWS_TALL = 512


def _dot(a, b):
    return jnp.dot(a.astype(BF16), b.astype(BF16), preferred_element_type=F32)


def _dot_nt(a, b):
    return lax.dot_general(a.astype(BF16), b.astype(BF16), (((1,), (1,)), ((), ())),
                           preferred_element_type=F32)


def _split3(x):
    hi = x.astype(BF16)
    r1 = x - hi.astype(F32)
    mid = r1.astype(BF16)
    lo = (r1 - mid.astype(F32)).astype(BF16)
    return hi, mid, lo


def _dot01_right(x, m01):
    hi, mid, lo = _split3(x)
    d = lambda p: jnp.dot(p, m01, preferred_element_type=F32)
    return d(hi) + d(mid) + d(lo)


def _conv_silu_norm(xe_ref, cw_ref, yn_ref, r0, n_out):
    halo = CONV_K // 2
    n_rows = n_out // ROW_STRIDE
    n_slab = 3 * GDN_W // LANES
    units = [(sb, ph) for sb in range(n_slab) for ph in range(ROW_STRIDE)]
    yv = {}
    for sb, ph in units:
        lanes = slice(sb * LANES, (sb + 1) * LANES)
        acc = None
        for j in range(CONV_K):
            win = xe_ref[sb, pl.ds(SUBLANES - halo + j + ph, n_rows, stride=ROW_STRIDE), :]
            term = cw_ref[j:j + 1, lanes] * win
            acc = term if acc is None else acc + term
        yv[sb, ph] = acc
    for key in units:
        yv[key] = yv[key] * jax.nn.sigmoid(yv[key])
    for sb, ph in units:
        if sb < 2 * GDN_HEADS:
            scale = lax.rsqrt(jnp.sum(yv[sb, ph] * yv[sb, ph], axis=-1, keepdims=True) + EPS)
            if sb < GDN_HEADS:
                scale = scale * (GDN_HEAD_DIM ** -0.5)
            yv[sb, ph] = yv[sb, ph] * scale
    for sb, ph in units:
        yn_ref[sb, pl.ds(r0 + ph, n_rows, stride=ROW_STRIDE), :] = yv[sb, ph]


def _proj_kernel(layer, tiles_per_seq, x_ref, xp_ref, xn_ref, nw_ref, wt_hbm, cw_ref,
                 qa_ref, ka_ref, va_ref, yn_ref, z_ref, ab_ref, wb_ref, stage_ref, sem, *xe_refs):
    d_in = wt_hbm.shape[1]
    d_main = d_in // LANES * LANES
    step = pl.program_id(0)

    @pl.when(step == 0)
    def _():
        starts = list(range(0, d_main, PROJ_CAST_ROWS)) + [d_main]
        sizes = [PROJ_CAST_ROWS] * (len(starts) - 1) + [d_in - d_main]
        copies = [pltpu.make_async_copy(wt_hbm.at[layer, pl.ds(c0, nr)], stage_ref.at[k % 2, pl.ds(0, nr)],
                                        sem.at[k % 2]) for k, (c0, nr) in enumerate(zip(starts, sizes))]
        lane = lax.broadcasted_iota(jnp.int32, (D_MODEL, PROJ_CAST_ROWS), 1)
        copies[0].start()
        for k, (c0, nr) in enumerate(zip(starts, sizes)):
            if k + 1 < len(copies):
                copies[k + 1].start()
            copies[k].wait()
            cols = stage_ref[k % 2].T
            if nr < PROJ_CAST_ROWS:
                cols = jnp.where(lane < nr, cols, 0.0)
            wb_ref[:, c0:c0 + PROJ_CAST_ROWS] = cols.astype(BF16)

    o_q, o_k, o_v, o_g, o_z, o_ab = 0, ATTN_Q, ATTN_Q + ATTN_KV, ATTN_Q + 2 * ATTN_KV, \
        ATTN_Q + 2 * ATTN_KV + 3 * GDN_W, ATTN_Q + 2 * ATTN_KV + 4 * GDN_W
    n_slab = 3 * GDN_W // LANES

    def normed(x):
        ms = jnp.mean(x * x, axis=-1, keepdims=True)
        return (x * lax.rsqrt(ms + EPS) * nw_ref[...]).astype(BF16)

    n_sub = PROJ_TM // PROJ_SUB

    def to_slabs(xe_ref, y, row0):
        for sb in range(n_slab):
            xe_ref[sb, row0:row0 + y.shape[0], :] = y[:, sb * LANES:(sb + 1) * LANES]

    def put(k, y):
        to_slabs(xe_refs[k], y, SUBLANES)
        if k > 0:
            to_slabs(xe_refs[k - 1], y[:SUBLANES], SUBLANES + PROJ_SUB)
        if k + 1 < n_sub:
            to_slabs(xe_refs[k + 1], y[PROJ_SUB - SUBLANES:], 0)

    i = step % tiles_per_seq
    outs = ((qa_ref, o_q, ATTN_Q, ATTN_HEAD_DIM ** -0.5), (ka_ref, o_k, ATTN_KV, None), (va_ref, o_v, ATTN_KV, None),
            (z_ref, o_z, GDN_W, None), (ab_ref, o_ab, LANES, None))
    for k in range(n_sub):
        rows = slice(k * PROJ_SUB, (k + 1) * PROJ_SUB)
        if k == 0:
            xg = normed(jnp.concatenate([xp_ref[...], xn_ref[...], x_ref[rows, :]], axis=0))
            yg = jnp.dot(xg, wb_ref[:, o_g:o_z], preferred_element_type=F32)
            to_slabs(xe_refs[0], jnp.where(i > 0, yg[:SUBLANES], 0.0), 0)
            to_slabs(xe_refs[n_sub - 1], jnp.where(i < tiles_per_seq - 1, yg[SUBLANES:2 * SUBLANES], 0.0),
                     SUBLANES + PROJ_SUB)
            put(0, yg[2 * SUBLANES:])
            xn = xg[2 * SUBLANES:]
        else:
            xn = normed(x_ref[rows, :])
            put(k, jnp.dot(xn, wb_ref[:, o_g:o_z], preferred_element_type=F32))
            _conv_silu_norm(xe_refs[k - 1], cw_ref, yn_ref, (k - 1) * PROJ_SUB, PROJ_SUB)
        for ref, c0, width, scale in outs:
            y = jnp.dot(xn, wb_ref[:, c0:c0 + width], preferred_element_type=F32)
            if scale is not None:
                y = y * scale
            ref[rows, :] = y.astype(ref.dtype)
    _conv_silu_norm(xe_refs[n_sub - 1], cw_ref, yn_ref, PROJ_TM - PROJ_SUB, PROJ_SUB)


def _proj(x2, norm_w, w_in, conv_w, layer, seq_len):
    n = x2.shape[0]
    d_in = w_in.shape[2]
    d_pad = d_in // LANES * LANES + LANES
    assert d_pad == ATTN_Q + 2 * ATTN_KV + 4 * GDN_W + LANES and d_in - (d_pad - LANES) == 2 * N_CHAIN
    assert seq_len % PROJ_TM == 0
    w_t = jnp.swapaxes(w_in, 1, 2)
    nh8 = PROJ_TM // SUBLANES
    row = lambda w: pl.BlockSpec((PROJ_TM, w), lambda i: (i, 0))
    return pl.pallas_call(
        functools.partial(_proj_kernel, layer, seq_len // PROJ_TM),
        grid=(n // PROJ_TM,),
        in_specs=[row(D_MODEL),
                  pl.BlockSpec((SUBLANES, D_MODEL), lambda i: (jnp.maximum(i * nh8 - 1, 0), 0)),
                  pl.BlockSpec((SUBLANES, D_MODEL), lambda i: (jnp.minimum((i + 1) * nh8, n // SUBLANES - 1), 0)),
                  pl.BlockSpec((1, D_MODEL), lambda i: (0, 0)),
                  pl.BlockSpec(memory_space=pl.ANY),
                  pl.BlockSpec((SUBLANES, 3 * GDN_W), lambda i: (0, 0))],
        out_specs=[row(ATTN_Q), row(ATTN_KV), row(ATTN_KV),
                   pl.BlockSpec((3 * GDN_W // LANES, PROJ_TM, LANES), lambda i: (0, i, 0)),
                   row(GDN_W), row(LANES)],
        out_shape=[jax.ShapeDtypeStruct((n, ATTN_Q), BF16),
                   jax.ShapeDtypeStruct((n, ATTN_KV), BF16),
                   jax.ShapeDtypeStruct((n, ATTN_KV), BF16),
                   jax.ShapeDtypeStruct((3 * GDN_W // LANES, n, LANES), F32),
                   jax.ShapeDtypeStruct((n, GDN_W), F32),
                   jax.ShapeDtypeStruct((n, LANES), F32)],
        scratch_shapes=[pltpu.VMEM((D_MODEL, d_pad), BF16),
                        pltpu.VMEM((2, PROJ_CAST_ROWS, D_MODEL), F32),
                        pltpu.SemaphoreType.DMA((2,))]
        + [pltpu.VMEM((3 * GDN_W // LANES, PROJ_SUB + 2 * SUBLANES, LANES), F32)] * (PROJ_TM // PROJ_SUB),
        compiler_params=pltpu.CompilerParams(dimension_semantics=("arbitrary",),
                                             vmem_limit_bytes=VMEM_LIMIT),
        name="proj",
    )(x2, x2, x2, norm_w, w_t, conv_w)


def _bias_kernel(relb_ref, bucket_ref, o_ref):
    bucket = bucket_ref[...]
    key = lax.broadcasted_iota(jnp.int32, (3 * BLOCK, BLOCK), 0)
    qry = lax.broadcasted_iota(jnp.int32, (3 * BLOCK, BLOCK), 1)
    in_window = jnp.abs(key - BLOCK - qry) <= WINDOW
    for h in range(ATTN_HEADS):
        acc = jnp.zeros((3 * BLOCK, BLOCK), F32)
        for b in range(N_BUCKETS):
            acc = jnp.where(bucket == b, relb_ref[b, h], acc)
        o_ref[h // 2, :, (h % 2) * BLOCK:(h % 2 + 1) * BLOCK] = jnp.where(in_window, acc, -1e30)


def _bias_band(rel_bias, bucket_t):
    shape = (ATTN_HEADS // 2, 3 * BLOCK, 2 * BLOCK)
    return pl.pallas_call(
        _bias_kernel,
        in_specs=[pl.BlockSpec(memory_space=pltpu.SMEM),
                  pl.BlockSpec((3 * BLOCK, BLOCK), lambda: (0, 0))],
        out_specs=pl.BlockSpec(shape, lambda: (0, 0, 0)),
        out_shape=jax.ShapeDtypeStruct(shape, F32),
        name="bias_band",
    )(rel_bias, bucket_t)


def _t5_buckets(rel):
    nb = N_BUCKETS // 2
    max_exact = nb // 2
    base = jnp.where(rel > 0, nb, 0)
    n = jnp.abs(rel)
    log_ratio = jnp.log(jnp.maximum(n, 1).astype(jnp.float32) / max_exact) / math.log(MAX_DISTANCE / max_exact)
    large = jnp.minimum(max_exact + (log_ratio * (nb - max_exact)).astype(jnp.int32), nb - 1)
    return base + jnp.where(n < max_exact, n, large)


def _attn_kernel(sink_ref, q_ref, kp_ref, kc_ref, kn_ref, vp_ref, vc_ref, vn_ref, bias_ref, o_ref):
    n = pl.program_id(1)
    last = pl.num_programs(1) - 1
    kband = jnp.concatenate([kp_ref[...], kc_ref[...], kn_ref[...]], axis=0)
    vband = jnp.concatenate([vp_ref[...], vc_ref[...], vn_ref[...]], axis=0)
    vband_t = vband.astype(F32).T.astype(BF16)
    key = lax.broadcasted_iota(jnp.int32, (3 * BLOCK, 1), 0)
    first_head = lax.broadcasted_iota(jnp.int32, (1, 2 * BLOCK), 1) < BLOCK
    head = lambda t, i: t[:, i * ATTN_HEAD_DIM:(i + 1) * ATTN_HEAD_DIM]
    n_pairs = ATTN_HEADS // 2
    kv_of = lambda pr: (2 * pr) // ATTN_GROUP
    units = [(j, pr) for j in range(ATTN_QB) for pr in range(n_pairs)]
    band_rows = lambda j: slice(j * BLOCK, (j + 3) * BLOCK)

    scores = {}
    for j, pr in units:
        qj = q_ref[j * BLOCK:(j + 1) * BLOCK, :]
        q2 = jnp.concatenate([head(qj, 2 * pr), head(qj, 2 * pr + 1)], axis=0)
        scores[j, pr] = _dot_nt(head(kband, kv_of(pr))[band_rows(j)], q2)
    probs, dens = {}, {}
    for j, pr in units:
        s = scores[j, pr] + bias_ref[pr]
        if j == 0:
            s = jnp.where((key < BLOCK) & (n == 0), -1e30, s)
        if j == ATTN_QB - 1:
            s = jnp.where((key >= 2 * BLOCK) & (n == last), -1e30, s)
        sink = jnp.where(first_head, sink_ref[2 * pr], sink_ref[2 * pr + 1])
        m = jnp.maximum(jnp.max(s, axis=0, keepdims=True), sink)
        p = jnp.exp(s - m)
        dens[j, pr] = jnp.sum(p, axis=0, keepdims=True) + jnp.exp(sink - m)
        probs[j, pr] = p.astype(BF16)
    outs_t = {}
    for j, pr in units:
        kv = kv_of(pr)
        v_t = vband_t[kv * ATTN_HEAD_DIM:(kv + 1) * ATTN_HEAD_DIM, band_rows(j)]
        outs_t[j, pr] = jnp.dot(v_t, probs[j, pr], preferred_element_type=F32) / dens[j, pr]
    for j in range(ATTN_QB):
        o_t = jnp.concatenate([outs_t[j, pr][:, half * BLOCK:(half + 1) * BLOCK]
                               for pr in range(n_pairs) for half in range(2)], axis=0)
        o_ref[j * BLOCK:(j + 1) * BLOCK, :] = o_t.T.astype(o_ref.dtype)


def _attention(q_a, k_a, v_a, band, sink, batch):
    n_tok = q_a.shape[0]
    nb = n_tok // batch // BLOCK
    rows = ATTN_QB * BLOCK
    steps = nb // ATTN_QB
    kv_spec = lambda r, f: pl.BlockSpec((r, ATTN_KV), f)
    prev = lambda bi, n: (bi * nb + jnp.maximum(n * ATTN_QB - 1, 0), 0)
    cur = lambda bi, n: (bi * steps + n, 0)
    nxt = lambda bi, n: (bi * nb + jnp.minimum((n + 1) * ATTN_QB, nb - 1), 0)
    return pl.pallas_call(
        _attn_kernel,
        grid=(batch, steps),
        in_specs=[pl.BlockSpec(memory_space=pltpu.SMEM),
                  pl.BlockSpec((rows, ATTN_Q), cur),
                  kv_spec(BLOCK, prev), kv_spec(rows, cur), kv_spec(BLOCK, nxt),
                  kv_spec(BLOCK, prev), kv_spec(rows, cur), kv_spec(BLOCK, nxt),
                  pl.BlockSpec(band.shape, lambda bi, n: (0, 0, 0))],
        out_specs=pl.BlockSpec((rows, ATTN_Q), cur),
        out_shape=jax.ShapeDtypeStruct((n_tok, ATTN_Q), BF16),
        compiler_params=pltpu.CompilerParams(dimension_semantics=("arbitrary", "arbitrary"),
                                             vmem_limit_bytes=VMEM_LIMIT),
        name="attn",
    )(sink, q_a, k_a, k_a, k_a, v_a, v_a, v_a, band)


def _gprep_kernel(yn_ref, ab_ref, gp_ref, w_ref, u_ref, qd_ref, kdt_ref, qk_ref, egl_ref):
    t_len = PREP_T

    ab = ab_ref[0]
    sp_in = ab + gp_ref[1:2, :]
    softplus = jnp.maximum(sp_in, 0.0) + jnp.log1p(jnp.exp(-jnp.abs(sp_in)))
    g = -jnp.exp(gp_ref[0:1, :]) * softplus
    beta = jax.nn.sigmoid(ab)

    r_t = lax.broadcasted_iota(jnp.int32, (t_len, t_len), 0)
    c_t = lax.broadcasted_iota(jnp.int32, (t_len, t_len), 1)
    same = (r_t // CHUNK) == (c_t // CHUNK)
    lower = jnp.where(same & (r_t >= c_t), 1.0, 0.0).astype(BF16)
    upper = jnp.where(same & (r_t <= c_t), 1.0, 0.0).astype(BF16)
    g_t = g.T[:2 * SUBLANES]
    cs_row = (_dot01_right(g_t, upper), _dot01_right(g_t, lower))
    pad_rows = jnp.zeros((LANES - 2 * SUBLANES, t_len), F32)
    cs_col = tuple(jnp.concatenate([r, pad_rows], axis=0).T for r in cs_row)

    qs = [yn_ref[h] for h in range(GDN_HEADS)]
    ks = [yn_ref[GDN_HEADS + h] for h in range(GDN_HEADS)]
    vs = [yn_ref[2 * GDN_HEADS + h] for h in range(GDN_HEADS)]
    kts = [kh.T for kh in ks]

    r_c = lax.broadcasted_iota(jnp.int32, (CHUNK, LANES), 0)
    lane = lax.broadcasted_iota(jnp.int32, (CHUNK, LANES), 1)
    is_fwd = lane < CHUNK
    c_c = lane % CHUNK
    eye = jnp.where(r_c == c_c, 1.0, 0.0).astype(F32)
    incl = (is_fwd & (r_c >= c_c)) | (~is_fwd & (r_c <= c_c))
    strict = (is_fwd & (r_c > c_c)) | (~is_fwd & (r_c < c_c))
    r_d = lax.broadcasted_iota(jnp.int32, (2 * CHUNK, LANES), 0)
    c_d = lax.broadcasted_iota(jnp.int32, (2 * CHUNK, LANES), 1)
    same_dir = (r_d // CHUNK) == (c_d // CHUNK)
    level_mask = lambda s_, r_, c_: ((r_ // (2 * s_)) == (c_ // (2 * s_))) & ((r_ // s_) != (c_ // s_))
    stack2 = lambda t: jnp.concatenate([t, t], axis=0)

    n_chunks = t_len // CHUNK
    rows = lambda c: slice(c * CHUNK, (c + 1) * CHUNK)
    pairs = [(c, h) for c in range(n_chunks) for h in range(GDN_HEADS)]
    qkk = {}
    for c, h in pairs:
        k16 = ks[h][rows(c)].astype(BF16)
        qk16 = jnp.concatenate([qs[h][rows(c)].astype(BF16), k16], axis=0)
        qkk[c, h] = _dot_nt(qk16, stack2(k16))

    bcast = lambda col: jnp.broadcast_to(col, (CHUNK, LANES))
    g_full, b_full, grow, glast, decay, a_mat, t_mat = {}, {}, {}, {}, {}, {}, {}
    for key in pairs:
        c, h = key
        for d in range(N_DIR):
            j = d * GDN_HEADS + h
            r_last = c * CHUNK + (CHUNK - 1 if d == 0 else 0)
            g_full[key, d] = bcast(cs_col[d][rows(c), j:j + 1])
            b_full[key, d] = bcast(beta[rows(c), SUBLANES + j:SUBLANES + j + 1])
            glast[key, d] = cs_col[d][r_last:r_last + 1, j:j + 1]
            grow[key, d] = cs_row[d][j:j + 1, rows(c)]
        gcol2 = jnp.where(is_fwd, g_full[key, 0], g_full[key, 1])
        bcol2 = jnp.where(is_fwd, b_full[key, 0], b_full[key, 1])
        grow2 = jnp.concatenate([grow[key, 0], grow[key, 1]], axis=1)
        decay[key] = jnp.exp(jnp.where(incl, gcol2 - grow2, -jnp.inf))
        a_mat[key] = jnp.where(strict, bcol2 * qkk[key][CHUNK:] * decay[key], 0.0)
        t_mat[key] = eye - jnp.where(level_mask(1, r_c, c_c), a_mat[key], 0.0)

    same_dir16 = jnp.where(same_dir, 1.0, 0.0).astype(BF16)
    block_diag = lambda t16: stack2(t16) * same_dir16
    a_bd = {key: block_diag(a_mat[key].astype(BF16)) for key in pairs}
    s = 2
    while s < CHUNK:
        lvl = level_mask(s, r_c, c_c)
        t16 = {key: t_mat[key].astype(BF16) for key in pairs}
        x_mat = {key: jnp.dot(t16[key], a_bd[key], preferred_element_type=F32) for key in pairs}
        y_mat = {key: jnp.dot(x_mat[key].astype(BF16), block_diag(t16[key]), preferred_element_type=F32)
                 for key in pairs}
        t_mat = {key: t_mat[key] - jnp.where(lvl, y_mat[key], 0.0) for key in pairs}
        s *= 2

    wu = {}
    for key in pairs:
        c, h = key
        kv = jnp.concatenate(
            [jnp.concatenate([ks[h][rows(c)] * (b_full[key, d] * jnp.exp(g_full[key, d])),
                              vs[h][rows(c)] * b_full[key, d]], axis=1) for d in range(N_DIR)], axis=0)
        t_sel = block_diag(t_mat[key].astype(BF16))
        wu[key] = jnp.dot(t_sel, kv.astype(BF16), preferred_element_type=F32)

    for c in range(n_chunks):
        egl_rows = []
        for d in range(N_DIR):
            for h in range(GDN_HEADS):
                key = (c, h)
                j = d * GDN_HEADS + h
                wu_d = wu[key][d * CHUNK:(d + 1) * CHUNK]
                w_ref[0, j, rows(c), :] = wu_d[:, :GDN_HEAD_DIM].astype(w_ref.dtype)
                u_ref[0, j, rows(c), :] = wu_d[:, GDN_HEAD_DIM:].astype(u_ref.dtype)
                qd_ref[0, j, rows(c), :] = (qs[h][rows(c)] * jnp.exp(g_full[key, d])).astype(qd_ref.dtype)
                egl_rows.append(jnp.broadcast_to(jnp.exp(glast[key, d]), (1, LANES)))
        egl_ref[0, c] = jnp.concatenate(egl_rows, axis=0)
        for h in range(GDN_HEADS):
            qk_ref[0, h, rows(c), :] = (qkk[c, h][:CHUNK] * decay[c, h]).astype(qk_ref.dtype)
    for cp in range(n_chunks // 2):
        for d in range(N_DIR):
            for h in range(GDN_HEADS):
                fac = jnp.concatenate([jnp.exp(glast[(c, h), d] - grow[(c, h), d]) for c in (2 * cp, 2 * cp + 1)],
                                      axis=1)
                kdt = kts[h][:, 2 * cp * CHUNK:(2 * cp + 2) * CHUNK] * fac
                kdt_ref[0, d * GDN_HEADS + h, cp] = kdt.astype(kdt_ref.dtype)


def _gdn_prep(yn, ab, gate_par):
    b, s, _ = ab.shape
    nt = s // PREP_T
    cpb = PREP_T // CHUNK
    nc = s // CHUNK
    chain = lambda last: pl.BlockSpec((1, N_CHAIN, PREP_T, last), lambda bi, i: (bi, 0, i, 0))
    return pl.pallas_call(
        _gprep_kernel,
        grid=(b, nt),
        in_specs=[pl.BlockSpec((3 * GDN_W // LANES, PREP_T, LANES), lambda bi, i: (0, bi * nt + i, 0)),
                  pl.BlockSpec((1, PREP_T, LANES), lambda bi, i: (bi, i, 0)),
                  pl.BlockSpec((SUBLANES, LANES), lambda bi, i: (0, 0))],
        out_specs=[chain(GDN_HEAD_DIM), chain(GDN_HEAD_DIM), chain(GDN_HEAD_DIM),
                   pl.BlockSpec((1, N_CHAIN, cpb // 2, GDN_HEAD_DIM, 2 * CHUNK), lambda bi, i: (bi, 0, i, 0, 0)),
                   pl.BlockSpec((1, GDN_HEADS, PREP_T, LANES), lambda bi, i: (bi, 0, i, 0)),
                   pl.BlockSpec((1, cpb, N_CHAIN, LANES), lambda bi, i: (bi, i, 0, 0))],
        out_shape=[jax.ShapeDtypeStruct((b, N_CHAIN, s, GDN_HEAD_DIM), BF16),
                   jax.ShapeDtypeStruct((b, N_CHAIN, s, GDN_HEAD_DIM), BF16),
                   jax.ShapeDtypeStruct((b, N_CHAIN, s, GDN_HEAD_DIM), BF16),
                   jax.ShapeDtypeStruct((b, N_CHAIN, nc // 2, GDN_HEAD_DIM, 2 * CHUNK), BF16),
                   jax.ShapeDtypeStruct((b, GDN_HEADS, s, LANES), BF16),
                   jax.ShapeDtypeStruct((b, nc, N_CHAIN, LANES), F32)],
        compiler_params=pltpu.CompilerParams(dimension_semantics=("arbitrary", "arbitrary"),
                                             vmem_limit_bytes=VMEM_LIMIT),
        name="gdn_prep",
    )(yn, ab, gate_par)


def _gscan_kernel(egl_ref, wf_ref, wb_ref, uf_ref, ub_ref, qf_ref, qb_ref, kf_ref, kb_ref,
                  pf_ref, pb_ref, of_ref, ob_ref, state_ref):
    t = pl.program_id(0)
    n_steps = pl.num_programs(0)
    nc = n_steps * SCAN_CHUNKS
    n_batch = wf_ref.shape[0]

    @pl.when(t == 0)
    def _():
        state_ref[...] = jnp.zeros_like(state_ref)

    dirs = ((wf_ref, uf_ref, qf_ref, kf_ref, pf_ref, of_ref), (wb_ref, ub_ref, qb_ref, kb_ref, pb_ref, ob_ref))
    chains = [(bi, d, h) for bi in range(n_batch) for d in range(N_DIR) for h in range(GDN_HEADS)]
    slot = lambda bi, d, h: (bi * N_DIR + d) * GDN_HEADS + h
    st = {key: state_ref[slot(*key)] for key in chains}

    for sub in range(SCAN_CHUNKS):
        local = (sub, SCAN_CHUNKS - 1 - sub)
        rows = [slice(c * CHUNK, (c + 1) * CHUNK) for c in local]
        chunk = (t * SCAN_CHUNKS + sub, nc - 1 - (t * SCAN_CHUNKS + sub))
        r = {}
        for key in chains:
            bi, d, h = key
            w_ref, _, q_ref = dirs[d][:3]
            wq = jnp.concatenate([w_ref[bi, h, rows[d]], q_ref[bi, h, rows[d]]], axis=0)
            r[key] = _dot(wq, st[key])
        v_pad, intra = {}, {}
        for key in chains:
            bi, d, h = key
            u_ref, p_ref = dirs[d][1], dirs[d][4]
            v_new = (u_ref[bi, h, rows[d]].astype(F32) - r[key][:CHUNK]).astype(BF16)
            zeros = jnp.zeros_like(v_new)
            v_pad[key] = (jnp.concatenate([v_new, zeros], axis=0), jnp.concatenate([zeros, v_new], axis=0))
            intra[key] = jnp.dot(p_ref[bi, h, rows[d]], v_pad[key][d], preferred_element_type=F32)
        for key in chains:
            bi, d, h = key
            k_ref = dirs[d][3]
            egl = egl_ref[(bi * nc + chunk[d]) * N_CHAIN + d * GDN_HEADS + h]
            st[key] = st[key] * egl + jnp.dot(k_ref[bi, h, local[d] // 2], v_pad[key][local[d] % 2],
                                              preferred_element_type=F32)
        for bi in range(n_batch):
            for d in range(N_DIR):
                o_ref = dirs[d][5]
                o_ref[bi, rows[d], :] = jnp.concatenate(
                    [r[bi, d, h][CHUNK:] + intra[bi, d, h] for h in range(GDN_HEADS)], axis=-1).astype(o_ref.dtype)

    for key in chains:
        state_ref[slot(*key)] = st[key]


def _gdn_scan(egl, w, u, qd, kdt, qk):
    b, _, s, _ = w.shape
    rows = SCAN_CHUNKS * CHUNK
    n_steps = s // rows
    fwd_i = lambda t: t
    bwd_i = lambda t: n_steps - 1 - t
    chain = lambda d, at, last: pl.BlockSpec((b, GDN_HEADS, rows, last), lambda t: (0, d, at(t), 0))
    kspec = lambda d, at: pl.BlockSpec((b, GDN_HEADS, SCAN_CHUNKS // 2, GDN_HEAD_DIM, 2 * CHUNK),
                                       lambda t: (0, d, at(t), 0, 0))
    dk = GDN_HEAD_DIM
    return pl.pallas_call(
        _gscan_kernel,
        grid=(n_steps,),
        in_specs=[pl.BlockSpec(memory_space=pltpu.SMEM),
                  chain(0, fwd_i, dk), chain(1, bwd_i, dk), chain(0, fwd_i, dk), chain(1, bwd_i, dk),
                  chain(0, fwd_i, dk), chain(1, bwd_i, dk), kspec(0, fwd_i), kspec(1, bwd_i),
                  chain(0, fwd_i, LANES), chain(0, bwd_i, LANES)],
        out_specs=[pl.BlockSpec((b, rows, GDN_W), lambda t: (0, fwd_i(t), 0)),
                   pl.BlockSpec((b, rows, GDN_W), lambda t: (0, bwd_i(t), 0))],
        out_shape=[jax.ShapeDtypeStruct((b, s, GDN_W), BF16),
                   jax.ShapeDtypeStruct((b, s, GDN_W), BF16)],
        scratch_shapes=[pltpu.VMEM((b * N_CHAIN, GDN_HEAD_DIM, GDN_HEAD_DIM), F32)],
        compiler_params=pltpu.CompilerParams(dimension_semantics=("arbitrary",),
                                             vmem_limit_bytes=VMEM_LIMIT),
        name="gdn_scan",
    )(egl, w, w, u, u, qd, qd, kdt, kdt, qk, qk)


def _cast_weights_once(layer, copies):
    jobs, used = [], {}
    for src, dst, stage, sem, chunk_rows in copies:
        for r0 in range(0, src.shape[1], chunk_rows):
            slot = used.get(id(stage), 0) % 2
            used[id(stage)] = used.get(id(stage), 0) + 1
            rows = pl.ds(r0, chunk_rows)
            jobs.append((pltpu.make_async_copy(src.at[layer, rows], stage.at[slot], sem.at[slot]),
                         stage, slot, dst, rows))
    jobs[0][0].start()
    for k, (copy, stage, slot, dst, rows) in enumerate(jobs):
        if k + 1 < len(jobs):
            jobs[k + 1][0].start()
        copy.wait()
        dst[rows, :] = stage[slot].astype(BF16)


def _ffn_kernel(layer, x_ref, attn_ref, of_ref, ob_ref, z_ref, gnw_ref, wo_hbm, fnw_ref, w1_hbm, w2_hbm, onw_ref,
                o_ref, wo_ref, w1_ref, w2_ref, stage_wide, stage_tall, sem_wide, sem_tall):
    @pl.when(pl.program_id(0) == 0)
    def _():
        _cast_weights_once(layer, [(w1_hbm, w1_ref, stage_wide, sem_wide, FFN_CAST_ROWS_WIDE),
                                   (w2_hbm, w2_ref, stage_tall, sem_tall, FFN_CAST_ROWS_TALL),
                                   (wo_hbm, wo_ref, stage_tall, sem_tall, FFN_CAST_ROWS_TALL)])

    subs = [slice(r0, r0 + FFN_SUB) for r0 in range(0, FFN_TM, FFN_SUB)]
    rms = lambda t: t * lax.rsqrt(jnp.mean(t * t, axis=-1, keepdims=True) + EPS)
    hres, hn, act, acc = {}, {}, {}, {}
    for r in subs:
        o = of_ref[r, :].astype(F32) + ob_ref[r, :].astype(F32)
        heads = [rms(o[:, h * GDN_HEAD_DIM:(h + 1) * GDN_HEAD_DIM]) * gnw_ref[...] for h in range(GDN_HEADS)]
        z = z_ref[r, :]
        gdn = jnp.concatenate(heads, axis=-1) * (z * jax.nn.sigmoid(z))
        hres[r.start] = (x_ref[r, :] + jnp.dot(attn_ref[r, :], wo_ref[:ATTN_Q, :], preferred_element_type=F32)
                         + _dot(gdn, wo_ref[ATTN_Q:, :]))
    for r in subs:
        hn[r.start] = (rms(hres[r.start]) * fnw_ref[...]).astype(BF16)
        act[r.start] = jnp.dot(hn[r.start], w1_ref[...], preferred_element_type=F32)
    for r in subs:
        a = jnp.square(jnp.maximum(act[r.start], 0.0)).astype(BF16)
        acc[r.start] = hres[r.start] + jnp.dot(a, w2_ref[...], preferred_element_type=F32)
    for r in subs:
        o_ref[r, :] = rms(acc[r.start]) * onw_ref[...]


def _out_ffn(x2, attn, o_f, o_b, z, gnw, wo, fnw, w1, w2, onw, layer):
    n = x2.shape[0]
    assert wo.shape[1:] == (D_MODEL, D_MODEL) and w1.shape[1:] == (D_MODEL, D_FF) and w2.shape[1:] == (D_FF, D_MODEL)
    row = lambda w: pl.BlockSpec((FFN_TM, w), lambda i: (i, 0))
    full = lambda a: pl.BlockSpec(a.shape, lambda i: (0, 0))
    hbm = pl.BlockSpec(memory_space=pl.ANY)
    return pl.pallas_call(
        functools.partial(_ffn_kernel, layer),
        grid=(n // FFN_TM,),
        in_specs=[row(D_MODEL), row(ATTN_Q), row(GDN_W), row(GDN_W), row(GDN_W),
                  full(gnw), hbm, full(fnw), hbm, hbm, full(onw)],
        out_specs=row(D_MODEL),
        out_shape=jax.ShapeDtypeStruct((n, D_MODEL), F32),
        scratch_shapes=[pltpu.VMEM((D_MODEL, D_MODEL), BF16),
                        pltpu.VMEM((D_MODEL, D_FF), BF16),
                        pltpu.VMEM((D_FF, D_MODEL), BF16),
                        pltpu.VMEM((2, FFN_CAST_ROWS_WIDE, D_FF), F32),
                        pltpu.VMEM((2, FFN_CAST_ROWS_TALL, D_MODEL), F32),
                        pltpu.SemaphoreType.DMA((2,)),
                        pltpu.SemaphoreType.DMA((2,))],
        compiler_params=pltpu.CompilerParams(dimension_semantics=("arbitrary",),
                                             vmem_limit_bytes=VMEM_LIMIT),
        name="out_ffn",
    )(x2, attn, o_f, o_b, z, gnw, wo, fnw, w1, w2, onw)


def _layer(h, band, norm_mix_w, w_in, layer, attn_sink, conv_w, gdn_a_log, gdn_dt_bias, gdn_norm_w,
           w_out, norm_ffn_w, w_ffn_in, w_ffn_out, out_norm_w):
    b, s, _ = h.shape
    n = b * s
    x2 = h.reshape(n, D_MODEL)
    conv_pad = jnp.zeros((SUBLANES, 3 * GDN_W), F32).at[:CONV_K].set(conv_w)
    q_a, k_a, v_a, yn, z_g, ab = _proj(x2, norm_mix_w.reshape(1, D_MODEL), w_in, conv_pad, layer, s)

    attn = _attention(q_a, k_a, v_a, band, attn_sink, b)

    gate_par = jnp.zeros((SUBLANES, LANES), F32)
    gate_par = gate_par.at[0, :N_CHAIN].set(gdn_a_log.reshape(-1)).at[1, :N_CHAIN].set(gdn_dt_bias.reshape(-1))
    w_c, u_c, q_dec, k_dec_t, qk, egl = _gdn_prep(yn, ab.reshape(b, s, LANES), gate_par)
    o_f, o_b = _gdn_scan(egl[..., 0].reshape(-1), w_c, u_c, q_dec, k_dec_t, qk)

    out = _out_ffn(x2, attn, o_f.reshape(n, GDN_W), o_b.reshape(n, GDN_W), z_g,
                   gdn_norm_w.reshape(1, GDN_HEAD_DIM), w_out, norm_ffn_w.reshape(1, D_MODEL),
                   w_ffn_in, w_ffn_out, out_norm_w.reshape(1, D_MODEL), layer)
    return out.reshape(b, s, D_MODEL)


def kernel(x, norm_mix_w, w_in, rel_bias, attn_sink, conv_w, gdn_a_log, gdn_dt_bias, gdn_norm_w, w_out,
           norm_ffn_w, w_ffn_in, w_ffn_out, norm_final_w):
    depth = w_in.shape[0]
    assert depth == 1, "the fused output kernel applies the final norm after the single trunk layer"
    rel = (np.arange(3 * BLOCK)[None, :] - BLOCK) - np.arange(BLOCK)[:, None]
    bucket = _t5_buckets(jnp.asarray(rel, dtype=jnp.int32))
    band = _bias_band(rel_bias, bucket.T)
    return _layer(x, band, norm_mix_w[0], w_in, 0, attn_sink[0], conv_w[0], gdn_a_log[0], gdn_dt_bias[0],
                  gdn_norm_w[0], w_out, norm_ffn_w[0], w_ffn_in, w_ffn_out, norm_final_w)
```

```python
import functools
import math

import jax
import jax.numpy as jnp
import numpy as np
from jax import lax
from jax.experimental import pallas as pl
from jax.experimental.pallas import tpu as pltpu

F32 = jnp.float32
BF16 = jnp.bfloat16

D_MODEL = 1024
ATTN_HEADS = 8
ATTN_KV_HEADS = 2
ATTN_HEAD_DIM = 64
ATTN_GROUP = ATTN_HEADS // ATTN_KV_HEADS
WINDOW = 128
BLOCK = 128
N_BUCKETS = 32
MAX_DISTANCE = 128
GDN_HEADS = 4
GDN_HEAD_DIM = 128
CONV_K = 5
CHUNK = 64
N_DIR = 2
N_CHAIN = N_DIR * GDN_HEADS
D_FF = 4 * D_MODEL
EPS = 1e-6
ATTN_Q = ATTN_HEADS * ATTN_HEAD_DIM
ATTN_KV = ATTN_KV_HEADS * ATTN_HEAD_DIM
GDN_W = GDN_HEADS * GDN_HEAD_DIM
LANES = 128
SUBLANES = 8
VMEM_LIMIT = 56 * 1024 * 1024

PROJ_TM = 1024
PROJ_SUB = 256
PROJ_CAST_ROWS = 128
ATTN_QB = 8
PREP_T = 1024
ROW_STRIDE = 4
SCAN_CHUNKS = 8
FFN_TM = 512
FFN_SUB = 256
FFN_CAST_ROWS_WIDE = 128
FFN_CAST_ROWS_TALL = 512


def _dot(a, b):
    return jnp.dot(a.astype(BF16), b.astype(BF16), preferred_element_type=F32)


def _dot_nt(a, b):
    return lax.dot_general(a.astype(BF16), b.astype(BF16), (((1,), (1,)), ((), ())),
                           preferred_element_type=F32)


def _split3(x):
    hi = x.astype(BF16)
    r1 = x - hi.astype(F32)
    mid = r1.astype(BF16)
    lo = (r1 - mid.astype(F32)).astype(BF16)
    return hi, mid, lo


def _dot01_right(x, m01):
    hi, mid, lo = _split3(x)
    d = lambda p: jnp.dot(p, m01, preferred_element_type=F32)
    return d(hi) + d(mid) + d(lo)


def _conv_silu_norm(xe_ref, cw_ref, yn_ref, r0, n_out):
    halo = CONV_K // 2
    n_rows = n_out // ROW_STRIDE
    n_slab = 3 * GDN_W // LANES
    units = [(sb, ph) for sb in range(n_slab) for ph in range(ROW_STRIDE)]
    yv = {}
    for sb, ph in units:
        lanes = slice(sb * LANES, (sb + 1) * LANES)
        acc = None
        for j in range(CONV_K):
            win = xe_ref[sb, pl.ds(SUBLANES - halo + j + ph, n_rows, stride=ROW_STRIDE), :]
            term = cw_ref[j:j + 1, lanes] * win
            acc = term if acc is None else acc + term
        yv[sb, ph] = acc
    for key in units:
        yv[key] = yv[key] * jax.nn.sigmoid(yv[key])
    for sb, ph in units:
        if sb < 2 * GDN_HEADS:
            scale = lax.rsqrt(jnp.sum(yv[sb, ph] * yv[sb, ph], axis=-1, keepdims=True) + EPS)
            if sb < GDN_HEADS:
                scale = scale * (GDN_HEAD_DIM ** -0.5)
            yv[sb, ph] = yv[sb, ph] * scale
    for sb, ph in units:
        yn_ref[sb, pl.ds(r0 + ph, n_rows, stride=ROW_STRIDE), :] = yv[sb, ph]


def _proj_kernel(layer, tiles_per_seq, x_ref, xp_ref, xn_ref, nw_ref, wt_hbm, cw_ref,
                 qa_ref, ka_ref, va_ref, yn_ref, z_ref, ab_ref, wb_ref, stage_ref, sem, *xe_refs):
    d_in = wt_hbm.shape[1]
    d_main = d_in // LANES * LANES
    step = pl.program_id(0)

    @pl.when(step == 0)
    def _():
        starts = list(range(0, d_main, PROJ_CAST_ROWS)) + [d_main]
        sizes = [PROJ_CAST_ROWS] * (len(starts) - 1) + [d_in - d_main]
        copies = [pltpu.make_async_copy(wt_hbm.at[layer, pl.ds(c0, nr)], stage_ref.at[k % 2, pl.ds(0, nr)],
                                        sem.at[k % 2]) for k, (c0, nr) in enumerate(zip(starts, sizes))]
        lane = lax.broadcasted_iota(jnp.int32, (D_MODEL, PROJ_CAST_ROWS), 1)
        copies[0].start()
        for k, (c0, nr) in enumerate(zip(starts, sizes)):
            if k + 1 < len(copies):
                copies[k + 1].start()
            copies[k].wait()
            cols = stage_ref[k % 2].T
            if nr < PROJ_CAST_ROWS:
                cols = jnp.where(lane < nr, cols, 0.0)
            wb_ref[:, c0:c0 + PROJ_CAST_ROWS] = cols.astype(BF16)

    o_q, o_k, o_v, o_g, o_z, o_ab = 0, ATTN_Q, ATTN_Q + ATTN_KV, ATTN_Q + 2 * ATTN_KV, \
        ATTN_Q + 2 * ATTN_KV + 3 * GDN_W, ATTN_Q + 2 * ATTN_KV + 4 * GDN_W
    n_slab = 3 * GDN_W // LANES

    def normed(x):
        ms = jnp.mean(x * x, axis=-1, keepdims=True)
        return (x * lax.rsqrt(ms + EPS) * nw_ref[...]).astype(BF16)

    n_sub = PROJ_TM // PROJ_SUB

    def to_slabs(xe_ref, y, row0):
        for sb in range(n_slab):
            xe_ref[sb, row0:row0 + y.shape[0], :] = y[:, sb * LANES:(sb + 1) * LANES]

    def put(k, y):
        to_slabs(xe_refs[k], y, SUBLANES)
        if k > 0:
            to_slabs(xe_refs[k - 1], y[:SUBLANES], SUBLANES + PROJ_SUB)
        if k + 1 < n_sub:
            to_slabs(xe_refs[k + 1], y[PROJ_SUB - SUBLANES:], 0)

    i = step % tiles_per_seq
    outs = ((qa_ref, o_q, ATTN_Q, ATTN_HEAD_DIM ** -0.5), (ka_ref, o_k, ATTN_KV, None), (va_ref, o_v, ATTN_KV, None),
            (z_ref, o_z, GDN_W, None), (ab_ref, o_ab, LANES, None))
    for k in range(n_sub):
        rows = slice(k * PROJ_SUB, (k + 1) * PROJ_SUB)
        if k == 0:
            xg = normed(jnp.concatenate([xp_ref[...], xn_ref[...], x_ref[rows, :]], axis=0))
            yg = jnp.dot(xg, wb_ref[:, o_g:o_z], preferred_element_type=F32)
            to_slabs(xe_refs[0], jnp.where(i > 0, yg[:SUBLANES], 0.0), 0)
            to_slabs(xe_refs[n_sub - 1], jnp.where(i < tiles_per_seq - 1, yg[SUBLANES:2 * SUBLANES], 0.0),
                     SUBLANES + PROJ_SUB)
            put(0, yg[2 * SUBLANES:])
            xn = xg[2 * SUBLANES:]
        else:
            xn = normed(x_ref[rows, :])
            put(k, jnp.dot(xn, wb_ref[:, o_g:o_z], preferred_element_type=F32))
            _conv_silu_norm(xe_refs[k - 1], cw_ref, yn_ref, (k - 1) * PROJ_SUB, PROJ_SUB)
        for ref, c0, width, scale in outs:
            y = jnp.dot(xn, wb_ref[:, c0:c0 + width], preferred_element_type=F32)
            if scale is not None:
                y = y * scale
            ref[rows, :] = y.astype(ref.dtype)
    _conv_silu_norm(xe_refs[n_sub - 1], cw_ref, yn_ref, PROJ_TM - PROJ_SUB, PROJ_SUB)


def _proj(x2, norm_w, w_in, conv_w, layer, seq_len):
    n = x2.shape[0]
    d_in = w_in.shape[2]
    d_pad = d_in // LANES * LANES + LANES
    assert d_pad == ATTN_Q + 2 * ATTN_KV + 4 * GDN_W + LANES and d_in - (d_pad - LANES) == 2 * N_CHAIN
    assert seq_len % PROJ_TM == 0
    w_t = jnp.swapaxes(w_in, 1, 2)
    nh8 = PROJ_TM // SUBLANES
    row = lambda w: pl.BlockSpec((PROJ_TM, w), lambda i: (i, 0))
    return pl.pallas_call(
        functools.partial(_proj_kernel, layer, seq_len // PROJ_TM),
        grid=(n // PROJ_TM,),
        in_specs=[row(D_MODEL),
                  pl.BlockSpec((SUBLANES, D_MODEL), lambda i: (jnp.maximum(i * nh8 - 1, 0), 0)),
                  pl.BlockSpec((SUBLANES, D_MODEL), lambda i: (jnp.minimum((i + 1) * nh8, n // SUBLANES - 1), 0)),
                  pl.BlockSpec((1, D_MODEL), lambda i: (0, 0)),
                  pl.BlockSpec(memory_space=pl.ANY),
                  pl.BlockSpec((SUBLANES, 3 * GDN_W), lambda i: (0, 0))],
        out_specs=[row(ATTN_Q), row(ATTN_KV), row(ATTN_KV),
                   pl.BlockSpec((3 * GDN_W // LANES, PROJ_TM, LANES), lambda i: (0, i, 0)),
                   row(GDN_W), row(LANES)],
        out_shape=[jax.ShapeDtypeStruct((n, ATTN_Q), BF16),
                   jax.ShapeDtypeStruct((n, ATTN_KV), BF16),
                   jax.ShapeDtypeStruct((n, ATTN_KV), BF16),
                   jax.ShapeDtypeStruct((3 * GDN_W // LANES, n, LANES), F32),
                   jax.ShapeDtypeStruct((n, GDN_W), F32),
                   jax.ShapeDtypeStruct((n, LANES), F32)],
        scratch_shapes=[pltpu.VMEM((D_MODEL, d_pad), BF16),
                        pltpu.VMEM((2, PROJ_CAST_ROWS, D_MODEL), F32),
                        pltpu.SemaphoreType.DMA((2,))]
        + [pltpu.VMEM((3 * GDN_W // LANES, PROJ_SUB + 2 * SUBLANES, LANES), F32)] * (PROJ_TM // PROJ_SUB),
        compiler_params=pltpu.CompilerParams(dimension_semantics=("arbitrary",),
                                             vmem_limit_bytes=VMEM_LIMIT),
        name="proj",
    )(x2, x2, x2, norm_w, w_t, conv_w)


def _bias_kernel(relb_ref, bucket_ref, o_ref):
    bucket = bucket_ref[...]
    key = lax.broadcasted_iota(jnp.int32, (3 * BLOCK, BLOCK), 0)
    qry = lax.broadcasted_iota(jnp.int32, (3 * BLOCK, BLOCK), 1)
    in_window = jnp.abs(key - BLOCK - qry) <= WINDOW
    for h in range(ATTN_HEADS):
        acc = jnp.zeros((3 * BLOCK, BLOCK), F32)
        for b in range(N_BUCKETS):
            acc = jnp.where(bucket == b, relb_ref[b, h], acc)
        o_ref[h // 2, :, (h % 2) * BLOCK:(h % 2 + 1) * BLOCK] = jnp.where(in_window, acc, -1e30)


def _bias_band(rel_bias, bucket_t):
    shape = (ATTN_HEADS // 2, 3 * BLOCK, 2 * BLOCK)
    return pl.pallas_call(
        _bias_kernel,
        in_specs=[pl.BlockSpec(memory_space=pltpu.SMEM),
                  pl.BlockSpec((3 * BLOCK, BLOCK), lambda: (0, 0))],
        out_specs=pl.BlockSpec(shape, lambda: (0, 0, 0)),
        out_shape=jax.ShapeDtypeStruct(shape, F32),
        name="bias_band",
    )(rel_bias, bucket_t)


def _t5_buckets(rel):
    nb = N_BUCKETS // 2
    max_exact = nb // 2
    base = jnp.where(rel > 0, nb, 0)
    n = jnp.abs(rel)
    log_ratio = jnp.log(jnp.maximum(n, 1).astype(jnp.float32) / max_exact) / math.log(MAX_DISTANCE / max_exact)
    large = jnp.minimum(max_exact + (log_ratio * (nb - max_exact)).astype(jnp.int32), nb - 1)
    return base + jnp.where(n < max_exact, n, large)


def _attn_kernel(sink_ref, q_ref, kp_ref, kc_ref, kn_ref, vp_ref, vc_ref, vn_ref, bias_ref, o_ref):
    n = pl.program_id(1)
    last = pl.num_programs(1) - 1
    kband = jnp.concatenate([kp_ref[...], kc_ref[...], kn_ref[...]], axis=0)
    vband = jnp.concatenate([vp_ref[...], vc_ref[...], vn_ref[...]], axis=0)
    vband_t = vband.astype(F32).T.astype(BF16)
    key = lax.broadcasted_iota(jnp.int32, (3 * BLOCK, 1), 0)
    first_head = lax.broadcasted_iota(jnp.int32, (1, 2 * BLOCK), 1) < BLOCK
    head = lambda t, i: t[:, i * ATTN_HEAD_DIM:(i + 1) * ATTN_HEAD_DIM]
    n_pairs = ATTN_HEADS // 2
    kv_of = lambda pr: (2 * pr) // ATTN_GROUP
    units = [(j, pr) for j in range(ATTN_QB) for pr in range(n_pairs)]
    band_rows = lambda j: slice(j * BLOCK, (j + 3) * BLOCK)

    scores = {}
    for j, pr in units:
        qj = q_ref[j * BLOCK:(j + 1) * BLOCK, :]
        q2 = jnp.concatenate([head(qj, 2 * pr), head(qj, 2 * pr + 1)], axis=0)
        scores[j, pr] = _dot_nt(head(kband, kv_of(pr))[band_rows(j)], q2)
    probs, dens = {}, {}
    for j, pr in units:
        s = scores[j, pr] + bias_ref[pr]
        if j == 0:
            s = jnp.where((key < BLOCK) & (n == 0), -1e30, s)
        if j == ATTN_QB - 1:
            s = jnp.where((key >= 2 * BLOCK) & (n == last), -1e30, s)
        sink = jnp.where(first_head, sink_ref[2 * pr], sink_ref[2 * pr + 1])
        m = jnp.maximum(jnp.max(s, axis=0, keepdims=True), sink)
        p = jnp.exp(s - m)
        dens[j, pr] = jnp.sum(p, axis=0, keepdims=True) + jnp.exp(sink - m)
        probs[j, pr] = p.astype(BF16)
    outs_t = {}
    for j, pr in units:
        kv = kv_of(pr)
        v_t = vband_t[kv * ATTN_HEAD_DIM:(kv + 1) * ATTN_HEAD_DIM, band_rows(j)]
        outs_t[j, pr] = jnp.dot(v_t, probs[j, pr], preferred_element_type=F32) / dens[j, pr]
    for j in range(ATTN_QB):
        o_t = jnp.concatenate([outs_t[j, pr][:, half * BLOCK:(half + 1) * BLOCK]
                               for pr in range(n_pairs) for half in range(2)], axis=0)
        o_ref[j * BLOCK:(j + 1) * BLOCK, :] = o_t.T.astype(o_ref.dtype)


def _attention(q_a, k_a, v_a, band, sink, batch):
    n_tok = q_a.shape[0]
    nb = n_tok // batch // BLOCK
    rows = ATTN_QB * BLOCK
    steps = nb // ATTN_QB
    kv_spec = lambda r, f: pl.BlockSpec((r, ATTN_KV), f)
    prev = lambda bi, n: (bi * nb + jnp.maximum(n * ATTN_QB - 1, 0), 0)
    cur = lambda bi, n: (bi * steps + n, 0)
    nxt = lambda bi, n: (bi * nb + jnp.minimum((n + 1) * ATTN_QB, nb - 1), 0)
    return pl.pallas_call(
        _attn_kernel,
        grid=(batch, steps),
        in_specs=[pl.BlockSpec(memory_space=pltpu.SMEM),
                  pl.BlockSpec((rows, ATTN_Q), cur),
                  kv_spec(BLOCK, prev), kv_spec(rows, cur), kv_spec(BLOCK, nxt),
                  kv_spec(BLOCK, prev), kv_spec(rows, cur), kv_spec(BLOCK, nxt),
                  pl.BlockSpec(band.shape, lambda bi, n: (0, 0, 0))],
        out_specs=pl.BlockSpec((rows, ATTN_Q), cur),
        out_shape=jax.ShapeDtypeStruct((n_tok, ATTN_Q), BF16),
        compiler_params=pltpu.CompilerParams(dimension_semantics=("arbitrary", "arbitrary"),
                                             vmem_limit_bytes=VMEM_LIMIT),
        name="attn",
    )(sink, q_a, k_a, k_a, k_a, v_a, v_a, v_a, band)


def _gprep_kernel(yn_ref, ab_ref, gp_ref, w_ref, u_ref, qd_ref, kdt_ref, qk_ref, egl_ref):
    t_len = PREP_T

    ab = ab_ref[0]
    sp_in = ab + gp_ref[1:2, :]
    softplus = jnp.maximum(sp_in, 0.0) + jnp.log1p(jnp.exp(-jnp.abs(sp_in)))
    g = -jnp.exp(gp_ref[0:1, :]) * softplus
    beta = jax.nn.sigmoid(ab)

    r_t = lax.broadcasted_iota(jnp.int32, (t_len, t_len), 0)
    c_t = lax.broadcasted_iota(jnp.int32, (t_len, t_len), 1)
    same = (r_t // CHUNK) == (c_t // CHUNK)
    lower = jnp.where(same & (r_t >= c_t), 1.0, 0.0).astype(BF16)
    upper = jnp.where(same & (r_t <= c_t), 1.0, 0.0).astype(BF16)
    g_t = g.T[:2 * SUBLANES]
    cs_row = (_dot01_right(g_t, upper), _dot01_right(g_t, lower))
    pad_rows = jnp.zeros((LANES - 2 * SUBLANES, t_len), F32)
    cs_col = tuple(jnp.concatenate([r, pad_rows], axis=0).T for r in cs_row)

    qs = [yn_ref[h] for h in range(GDN_HEADS)]
    ks = [yn_ref[GDN_HEADS + h] for h in range(GDN_HEADS)]
    vs = [yn_ref[2 * GDN_HEADS + h] for h in range(GDN_HEADS)]
    kts = [kh.T for kh in ks]

    r_c = lax.broadcasted_iota(jnp.int32, (CHUNK, LANES), 0)
    lane = lax.broadcasted_iota(jnp.int32, (CHUNK, LANES), 1)
    is_fwd = lane < CHUNK
    c_c = lane % CHUNK
    eye = jnp.where(r_c == c_c, 1.0, 0.0).astype(F32)
    incl = (is_fwd & (r_c >= c_c)) | (~is_fwd & (r_c <= c_c))
    strict = (is_fwd & (r_c > c_c)) | (~is_fwd & (r_c < c_c))
    r_d = lax.broadcasted_iota(jnp.int32, (2 * CHUNK, LANES), 0)
    c_d = lax.broadcasted_iota(jnp.int32, (2 * CHUNK, LANES), 1)
    same_dir = (r_d // CHUNK) == (c_d // CHUNK)
    level_mask = lambda s_, r_, c_: ((r_ // (2 * s_)) == (c_ // (2 * s_))) & ((r_ // s_) != (c_ // s_))
    stack2 = lambda t: jnp.concatenate([t, t], axis=0)

    n_chunks = t_len // CHUNK
    rows = lambda c: slice(c * CHUNK, (c + 1) * CHUNK)
    pairs = [(c, h) for c in range(n_chunks) for h in range(GDN_HEADS)]
    qkk = {}
    for c, h in pairs:
        k16 = ks[h][rows(c)].astype(BF16)
        qk16 = jnp.concatenate([qs[h][rows(c)].astype(BF16), k16], axis=0)
        qkk[c, h] = _dot_nt(qk16, stack2(k16))

    bcast = lambda col: jnp.broadcast_to(col, (CHUNK, LANES))
    g_full, b_full, grow, glast, decay, a_mat, t_mat = {}, {}, {}, {}, {}, {}, {}
    for key in pairs:
        c, h = key
        for d in range(N_DIR):
            j = d * GDN_HEADS + h
            r_last = c * CHUNK + (CHUNK - 1 if d == 0 else 0)
            g_full[key, d] = bcast(cs_col[d][rows(c), j:j + 1])
            b_full[key, d] = bcast(beta[rows(c), SUBLANES + j:SUBLANES + j + 1])
            glast[key, d] = cs_col[d][r_last:r_last + 1, j:j + 1]
            grow[key, d] = cs_row[d][j:j + 1, rows(c)]
        gcol2 = jnp.where(is_fwd, g_full[key, 0], g_full[key, 1])
        bcol2 = jnp.where(is_fwd, b_full[key, 0], b_full[key, 1])
        grow2 = jnp.concatenate([grow[key, 0], grow[key, 1]], axis=1)
        decay[key] = jnp.exp(jnp.where(incl, gcol2 - grow2, -jnp.inf))
        a_mat[key] = jnp.where(strict, bcol2 * qkk[key][CHUNK:] * decay[key], 0.0)
        t_mat[key] = eye - jnp.where(level_mask(1, r_c, c_c), a_mat[key], 0.0)

    same_dir16 = jnp.where(same_dir, 1.0, 0.0).astype(BF16)
    block_diag = lambda t16: stack2(t16) * same_dir16
    a_bd = {key: block_diag(a_mat[key].astype(BF16)) for key in pairs}
    s = 2
    while s < CHUNK:
        lvl = level_mask(s, r_c, c_c)
        t16 = {key: t_mat[key].astype(BF16) for key in pairs}
        x_mat = {key: jnp.dot(t16[key], a_bd[key], preferred_element_type=F32) for key in pairs}
        y_mat = {key: jnp.dot(x_mat[key].astype(BF16), block_diag(t16[key]), preferred_element_type=F32)
                 for key in pairs}
        t_mat = {key: t_mat[key] - jnp.where(lvl, y_mat[key], 0.0) for key in pairs}
        s *= 2

    wu = {}
    for key in pairs:
        c, h = key
        kv = jnp.concatenate(
            [jnp.concatenate([ks[h][rows(c)] * (b_full[key, d] * jnp.exp(g_full[key, d])),
                              vs[h][rows(c)] * b_full[key, d]], axis=1) for d in range(N_DIR)], axis=0)
        t_sel = block_diag(t_mat[key].astype(BF16))
        wu[key] = jnp.dot(t_sel, kv.astype(BF16), preferred_element_type=F32)

    for c in range(n_chunks):
        egl_rows = []
        for d in range(N_DIR):
            for h in range(GDN_HEADS):
                key = (c, h)
                j = d * GDN_HEADS + h
                wu_d = wu[key][d * CHUNK:(d + 1) * CHUNK]
                w_ref[0, j, rows(c), :] = wu_d[:, :GDN_HEAD_DIM].astype(w_ref.dtype)
                u_ref[0, j, rows(c), :] = wu_d[:, GDN_HEAD_DIM:].astype(u_ref.dtype)
                qd_ref[0, j, rows(c), :] = (qs[h][rows(c)] * jnp.exp(g_full[key, d])).astype(qd_ref.dtype)
                egl_rows.append(jnp.broadcast_to(jnp.exp(glast[key, d]), (1, LANES)))
        egl_ref[0, c] = jnp.concatenate(egl_rows, axis=0)
        for h in range(GDN_HEADS):
            qk_ref[0, h, rows(c), :] = (qkk[c, h][:CHUNK] * decay[c, h]).astype(qk_ref.dtype)
    for cp in range(n_chunks // 2):
        for d in range(N_DIR):
            for h in range(GDN_HEADS):
                fac = jnp.concatenate([jnp.exp(glast[(c, h), d] - grow[(c, h), d]) for c in (2 * cp, 2 * cp + 1)],
                                      axis=1)
                kdt = kts[h][:, 2 * cp * CHUNK:(2 * cp + 2) * CHUNK] * fac
                kdt_ref[0, d * GDN_HEADS + h, cp] = kdt.astype(kdt_ref.dtype)


def _gdn_prep(yn, ab, gate_par):
    b, s, _ = ab.shape
    nt = s // PREP_T
    cpb = PREP_T // CHUNK
    nc = s // CHUNK
    chain = lambda last: pl.BlockSpec((1, N_CHAIN, PREP_T, last), lambda bi, i: (bi, 0, i, 0))
    return pl.pallas_call(
        _gprep_kernel,
        grid=(b, nt),
        in_specs=[pl.BlockSpec((3 * GDN_W // LANES, PREP_T, LANES), lambda bi, i: (0, bi * nt + i, 0)),
                  pl.BlockSpec((1, PREP_T, LANES), lambda bi, i: (bi, i, 0)),
                  pl.BlockSpec((SUBLANES, LANES), lambda bi, i: (0, 0))],
        out_specs=[chain(GDN_HEAD_DIM), chain(GDN_HEAD_DIM), chain(GDN_HEAD_DIM),
                   pl.BlockSpec((1, N_CHAIN, cpb // 2, GDN_HEAD_DIM, 2 * CHUNK), lambda bi, i: (bi, 0, i, 0, 0)),
                   pl.BlockSpec((1, GDN_HEADS, PREP_T, LANES), lambda bi, i: (bi, 0, i, 0)),
                   pl.BlockSpec((1, cpb, N_CHAIN, LANES), lambda bi, i: (bi, i, 0, 0))],
        out_shape=[jax.ShapeDtypeStruct((b, N_CHAIN, s, GDN_HEAD_DIM), BF16),
                   jax.ShapeDtypeStruct((b, N_CHAIN, s, GDN_HEAD_DIM), BF16),
                   jax.ShapeDtypeStruct((b, N_CHAIN, s, GDN_HEAD_DIM), BF16),
                   jax.ShapeDtypeStruct((b, N_CHAIN, nc // 2, GDN_HEAD_DIM, 2 * CHUNK), BF16),
                   jax.ShapeDtypeStruct((b, GDN_HEADS, s, LANES), BF16),
                   jax.ShapeDtypeStruct((b, nc, N_CHAIN, LANES), F32)],
        compiler_params=pltpu.CompilerParams(dimension_semantics=("arbitrary", "arbitrary"),
                                             vmem_limit_bytes=VMEM_LIMIT),
        name="gdn_prep",
    )(yn, ab, gate_par)


def _gscan_kernel(egl_ref, wf_ref, wb_ref, uf_ref, ub_ref, qf_ref, qb_ref, kf_ref, kb_ref,
                  pf_ref, pb_ref, of_ref, ob_ref, state_ref):
    t = pl.program_id(0)
    n_steps = pl.num_programs(0)
    nc = n_steps * SCAN_CHUNKS
    n_batch = wf_ref.shape[0]

    @pl.when(t == 0)
    def _():
        state_ref[...] = jnp.zeros_like(state_ref)

    dirs = ((wf_ref, uf_ref, qf_ref, kf_ref, pf_ref, of_ref), (wb_ref, ub_ref, qb_ref, kb_ref, pb_ref, ob_ref))
    chains = [(bi, d, h) for bi in range(n_batch) for d in range(N_DIR) for h in range(GDN_HEADS)]
    slot = lambda bi, d, h: (bi * N_DIR + d) * GDN_HEADS + h
    st = {key: state_ref[slot(*key)] for key in chains}

    for sub in range(SCAN_CHUNKS):
        local = (sub, SCAN_CHUNKS - 1 - sub)
        rows = [slice(c * CHUNK, (c + 1) * CHUNK) for c in local]
        chunk = (t * SCAN_CHUNKS + sub, nc - 1 - (t * SCAN_CHUNKS + sub))
        r = {}
        for key in chains:
            bi, d, h = key
            w_ref, _, q_ref = dirs[d][:3]
            wq = jnp.concatenate([w_ref[bi, h, rows[d]], q_ref[bi, h, rows[d]]], axis=0)
            r[key] = _dot(wq, st[key])
        v_pad, intra = {}, {}
        for key in chains:
            bi, d, h = key
            u_ref, p_ref = dirs[d][1], dirs[d][4]
            v_new = (u_ref[bi, h, rows[d]].astype(F32) - r[key][:CHUNK]).astype(BF16)
            zeros = jnp.zeros_like(v_new)
            v_pad[key] = (jnp.concatenate([v_new, zeros], axis=0), jnp.concatenate([zeros, v_new], axis=0))
            intra[key] = jnp.dot(p_ref[bi, h, rows[d]], v_pad[key][d], preferred_element_type=F32)
        for key in chains:
            bi, d, h = key
            k_ref = dirs[d][3]
            egl = egl_ref[(bi * nc + chunk[d]) * N_CHAIN + d * GDN_HEADS + h]
            st[key] = st[key] * egl + jnp.dot(k_ref[bi, h, local[d] // 2], v_pad[key][local[d] % 2],
                                              preferred_element_type=F32)
        for bi in range(n_batch):
            for d in range(N_DIR):
                o_ref = dirs[d][5]
                o_ref[bi, rows[d], :] = jnp.concatenate(
                    [r[bi, d, h][CHUNK:] + intra[bi, d, h] for h in range(GDN_HEADS)], axis=-1).astype(o_ref.dtype)

    for key in chains:
        state_ref[slot(*key)] = st[key]


def _gdn_scan(egl, w, u, qd, kdt, qk):
    b, _, s, _ = w.shape
    rows = SCAN_CHUNKS * CHUNK
    n_steps = s // rows
    fwd_i = lambda t: t
    bwd_i = lambda t: n_steps - 1 - t
    chain = lambda d, at, last: pl.BlockSpec((b, GDN_HEADS, rows, last), lambda t: (0, d, at(t), 0))
    kspec = lambda d, at: pl.BlockSpec((b, GDN_HEADS, SCAN_CHUNKS // 2, GDN_HEAD_DIM, 2 * CHUNK),
                                       lambda t: (0, d, at(t), 0, 0))
    dk = GDN_HEAD_DIM
    return pl.pallas_call(
        _gscan_kernel,
        grid=(n_steps,),
        in_specs=[pl.BlockSpec(memory_space=pltpu.SMEM),
                  chain(0, fwd_i, dk), chain(1, bwd_i, dk), chain(0, fwd_i, dk), chain(1, bwd_i, dk),
                  chain(0, fwd_i, dk), chain(1, bwd_i, dk), kspec(0, fwd_i), kspec(1, bwd_i),
                  chain(0, fwd_i, LANES), chain(0, bwd_i, LANES)],
        out_specs=[pl.BlockSpec((b, rows, GDN_W), lambda t: (0, fwd_i(t), 0)),
                   pl.BlockSpec((b, rows, GDN_W), lambda t: (0, bwd_i(t), 0))],
        out_shape=[jax.ShapeDtypeStruct((b, s, GDN_W), BF16),
                   jax.ShapeDtypeStruct((b, s, GDN_W), BF16)],
        scratch_shapes=[pltpu.VMEM((b * N_CHAIN, GDN_HEAD_DIM, GDN_HEAD_DIM), F32)],
        compiler_params=pltpu.CompilerParams(dimension_semantics=("arbitrary",),
                                             vmem_limit_bytes=VMEM_LIMIT),
        name="gdn_scan",
    )(egl, w, w, u, u, qd, qd, kdt, kdt, qk, qk)


def _cast_weights_once(layer, copies):
    jobs, used = [], {}
    for src, dst, stage, sem, chunk_rows in copies:
        for r0 in range(0, src.shape[1], chunk_rows):
            slot = used.get(id(stage), 0) % 2
            used[id(stage)] = used.get(id(stage), 0) + 1
            rows = pl.ds(r0, chunk_rows)
            jobs.append((pltpu.make_async_copy(src.at[layer, rows], stage.at[slot], sem.at[slot]),
                         stage, slot, dst, rows))
    jobs[0][0].start()
    for k, (copy, stage, slot, dst, rows) in enumerate(jobs):
        if k + 1 < len(jobs):
            jobs[k + 1][0].start()
        copy.wait()
        dst[rows, :] = stage[slot].astype(BF16)


def _ffn_kernel(layer, x_ref, attn_ref, of_ref, ob_ref, z_ref, gnw_ref, wo_hbm, fnw_ref, w1_hbm, w2_hbm, onw_ref,
                o_ref, wo_ref, w1_ref, w2_ref, stage_wide, stage_tall, sem_wide, sem_tall):
    @pl.when(pl.program_id(0) == 0)
    def _():
        _cast_weights_once(layer, [(w1_hbm, w1_ref, stage_wide, sem_wide, FFN_CAST_ROWS_WIDE),
                                   (w2_hbm, w2_ref, stage_tall, sem_tall, FFN_CAST_ROWS_TALL),
                                   (wo_hbm, wo_ref, stage_tall, sem_tall, FFN_CAST_ROWS_TALL)])

    subs = [slice(r0, r0 + FFN_SUB) for r0 in range(0, FFN_TM, FFN_SUB)]
    rms = lambda t: t * lax.rsqrt(jnp.mean(t * t, axis=-1, keepdims=True) + EPS)
    hres, hn, act, acc = {}, {}, {}, {}
    for r in subs:
        o = of_ref[r, :].astype(F32) + ob_ref[r, :].astype(F32)
        heads = [rms(o[:, h * GDN_HEAD_DIM:(h + 1) * GDN_HEAD_DIM]) * gnw_ref[...] for h in range(GDN_HEADS)]
        z = z_ref[r, :]
        gdn = jnp.concatenate(heads, axis=-1) * (z * jax.nn.sigmoid(z))
        hres[r.start] = (x_ref[r, :] + jnp.dot(attn_ref[r, :], wo_ref[:ATTN_Q, :], preferred_element_type=F32)
                         + _dot(gdn, wo_ref[ATTN_Q:, :]))
    for r in subs:
        hn[r.start] = (rms(hres[r.start]) * fnw_ref[...]).astype(BF16)
        act[r.start] = jnp.dot(hn[r.start], w1_ref[...], preferred_element_type=F32)
    for r in subs:
        a = jnp.square(jnp.maximum(act[r.start], 0.0)).astype(BF16)
        acc[r.start] = hres[r.start] + jnp.dot(a, w2_ref[...], preferred_element_type=F32)
    for r in subs:
        o_ref[r, :] = rms(acc[r.start]) * onw_ref[...]


def _out_ffn(x2, attn, o_f, o_b, z, gnw, wo, fnw, w1, w2, onw, layer):
    n = x2.shape[0]
    assert wo.shape[1:] == (D_MODEL, D_MODEL) and w1.shape[1:] == (D_MODEL, D_FF) and w2.shape[1:] == (D_FF, D_MODEL)
    row = lambda w: pl.BlockSpec((FFN_TM, w), lambda i: (i, 0))
    full = lambda a: pl.BlockSpec(a.shape, lambda i: (0, 0))
    hbm = pl.BlockSpec(memory_space=pl.ANY)
    return pl.pallas_call(
        functools.partial(_ffn_kernel, layer),
        grid=(n // FFN_TM,),
        in_specs=[row(D_MODEL), row(ATTN_Q), row(GDN_W), row(GDN_W), row(GDN_W),
                  full(gnw), hbm, full(fnw), hbm, hbm, full(onw)],
        out_specs=row(D_MODEL),
        out_shape=jax.ShapeDtypeStruct((n, D_MODEL), F32),
        scratch_shapes=[pltpu.VMEM((D_MODEL, D_MODEL), BF16),
                        pltpu.VMEM((D_MODEL, D_FF), BF16),
                        pltpu.VMEM((D_FF, D_MODEL), BF16),
                        pltpu.VMEM((2, FFN_CAST_ROWS_WIDE, D_FF), F32),
                        pltpu.VMEM((2, FFN_CAST_ROWS_TALL, D_MODEL), F32),
                        pltpu.SemaphoreType.DMA((2,)),
                        pltpu.SemaphoreType.DMA((2,))],
        compiler_params=pltpu.CompilerParams(dimension_semantics=("arbitrary",),
                                             vmem_limit_bytes=VMEM_LIMIT),
        name="out_ffn",
    )(x2, attn, o_f, o_b, z, gnw, wo, fnw, w1, w2, onw)


def _layer(h, band, norm_mix_w, w_in, layer, attn_sink, conv_w, gdn_a_log, gdn_dt_bias, gdn_norm_w,
           w_out, norm_ffn_w, w_ffn_in, w_ffn_out, out_norm_w):
    b, s, _ = h.shape
    n = b * s
    x2 = h.reshape(n, D_MODEL)
    conv_pad = jnp.zeros((SUBLANES, 3 * GDN_W), F32).at[:CONV_K].set(conv_w)
    q_a, k_a, v_a, yn, z_g, ab = _proj(x2, norm_mix_w.reshape(1, D_MODEL), w_in, conv_pad, layer, s)

    attn = _attention(q_a, k_a, v_a, band, attn_sink, b)

    gate_par = jnp.zeros((SUBLANES, LANES), F32)
    gate_par = gate_par.at[0, :N_CHAIN].set(gdn_a_log.reshape(-1)).at[1, :N_CHAIN].set(gdn_dt_bias.reshape(-1))
    w_c, u_c, q_dec, k_dec_t, qk, egl = _gdn_prep(yn, ab.reshape(b, s, LANES), gate_par)
    o_f, o_b = _gdn_scan(egl[..., 0].reshape(-1), w_c, u_c, q_dec, k_dec_t, qk)

    out = _out_ffn(x2, attn, o_f.reshape(n, GDN_W), o_b.reshape(n, GDN_W), z_g,
                   gdn_norm_w.reshape(1, GDN_HEAD_DIM), w_out, norm_ffn_w.reshape(1, D_MODEL),
                   w_ffn_in, w_ffn_out, out_norm_w.reshape(1, D_MODEL), layer)
    return out.reshape(b, s, D_MODEL)


def kernel(x, norm_mix_w, w_in, rel_bias, attn_sink, conv_w, gdn_a_log, gdn_dt_bias, gdn_norm_w, w_out,
           norm_ffn_w, w_ffn_in, w_ffn_out, norm_final_w):
    depth = w_in.shape[0]
    assert depth == 1, "the fused output kernel applies the final norm after the single trunk layer"
    rel = (np.arange(3 * BLOCK)[None, :] - BLOCK) - np.arange(BLOCK)[:, None]
    bucket = _t5_buckets(jnp.asarray(rel, dtype=jnp.int32))
    band = _bias_band(rel_bias, bucket.T)
    return _layer(x, band, norm_mix_w[0], w_in, 0, attn_sink[0], conv_w[0], gdn_a_log[0], gdn_dt_bias[0],
                  gdn_norm_w[0], w_out, norm_ffn_w[0], w_ffn_in, w_ffn_out, norm_final_w)
```

```python
import functools
import math

import jax
import jax.numpy as jnp
import numpy as np
from jax import lax
from jax.experimental import pallas as pl
from jax.experimental.pallas import tpu as pltpu

F32 = jnp.float32
BF16 = jnp.bfloat16

D_MODEL = 1024
ATTN_HEADS = 8
ATTN_KV_HEADS = 2
ATTN_HEAD_DIM = 64
ATTN_GROUP = ATTN_HEADS // ATTN_KV_HEADS
WINDOW = 128
BLOCK = 128
N_BUCKETS = 32
MAX_DISTANCE = 128
GDN_HEADS = 4
GDN_HEAD_DIM = 128
CONV_K = 5
CHUNK = 64
N_DIR = 2
N_CHAIN = N_DIR * GDN_HEADS
D_FF = 4 * D_MODEL
EPS = 1e-6
LOG2E = math.log2(math.e)
ATTN_Q = ATTN_HEADS * ATTN_HEAD_DIM
ATTN_KV = ATTN_KV_HEADS * ATTN_HEAD_DIM
GDN_W = GDN_HEADS * GDN_HEAD_DIM
LANES = 128
SUBLANES = 8
VMEM_LIMIT = 56 * 1024 * 1024

PROJ_TM = 1024
PROJ_SUB = 256
PROJ_CAST_ROWS = 128
ATTN_QB = 8
PREP_T = 1024
ROW_STRIDE = 4
SCAN_CHUNKS = 8
FFN_TM = 512
FFN_SUB = 256
FFN_CAST_ROWS_WIDE = 128
FFN_CAST_ROWS_TALL = 512


def _dot(a, b):
    return jnp.dot(a.astype(BF16), b.astype(BF16), preferred_element_type=F32)


def _dot_nt(a, b):
    return lax.dot_general(a.astype(BF16), b.astype(BF16), (((1,), (1,)), ((), ())),
                           preferred_element_type=F32)


def _split3(x):
    hi = x.astype(BF16)
    r1 = x - hi.astype(F32)
    mid = r1.astype(BF16)
    lo = (r1 - mid.astype(F32)).astype(BF16)
    return hi, mid, lo


def _dot01_right(x, m01):
    hi, mid, lo = _split3(x)
    d = lambda p: jnp.dot(p, m01, preferred_element_type=F32)
    return d(hi) + d(mid) + d(lo)


def _conv_silu_norm(xe_ref, cw_ref, yn_ref, r0, n_out):
    halo = CONV_K // 2
    n_rows = n_out // ROW_STRIDE
    n_slab = 3 * GDN_W // LANES
    units = [(sb, ph) for sb in range(n_slab) for ph in range(ROW_STRIDE)]
    yv = {}
    for sb, ph in units:
        lanes = slice(sb * LANES, (sb + 1) * LANES)
        acc = None
        for j in range(CONV_K):
            win = xe_ref[sb, pl.ds(SUBLANES - halo + j + ph, n_rows, stride=ROW_STRIDE), :]
            term = cw_ref[j:j + 1, lanes] * win
            acc = term if acc is None else acc + term
        yv[sb, ph] = acc
    for key in units:
        yv[key] = yv[key] * jax.nn.sigmoid(yv[key])
    for sb, ph in units:
        if sb < 2 * GDN_HEADS:
            scale = lax.rsqrt(jnp.sum(yv[sb, ph] * yv[sb, ph], axis=-1, keepdims=True) + EPS)
            if sb < GDN_HEADS:
                scale = scale * (GDN_HEAD_DIM ** -0.5)
            yv[sb, ph] = yv[sb, ph] * scale
    for sb, ph in units:
        yn_ref[sb, pl.ds(r0 + ph, n_rows, stride=ROW_STRIDE), :] = yv[sb, ph]


def _proj_kernel(layer, tiles_per_seq, x_ref, xp_ref, xn_ref, nw_ref, wt_hbm, cw_ref,
                 qa_ref, ka_ref, va_ref, yn_ref, z_ref, ab_ref, wb_ref, stage_ref, sem, *xe_refs):
    d_in = wt_hbm.shape[1]
    d_main = d_in // LANES * LANES
    step = pl.program_id(0)

    @pl.when(step == 0)
    def _():
        starts = list(range(0, d_main, PROJ_CAST_ROWS)) + [d_main]
        sizes = [PROJ_CAST_ROWS] * (len(starts) - 1) + [d_in - d_main]
        copies = [pltpu.make_async_copy(wt_hbm.at[layer, pl.ds(c0, nr)], stage_ref.at[k % 2, pl.ds(0, nr)],
                                        sem.at[k % 2]) for k, (c0, nr) in enumerate(zip(starts, sizes))]
        lane = lax.broadcasted_iota(jnp.int32, (D_MODEL, PROJ_CAST_ROWS), 1)
        copies[0].start()
        for k, (c0, nr) in enumerate(zip(starts, sizes)):
            if k + 1 < len(copies):
                copies[k + 1].start()
            copies[k].wait()
            cols = stage_ref[k % 2].T
            if nr < PROJ_CAST_ROWS:
                cols = jnp.where(lane < nr, cols, 0.0)
            wb_ref[:, c0:c0 + PROJ_CAST_ROWS] = cols.astype(BF16)

    o_q, o_k, o_v, o_g, o_z, o_ab = 0, ATTN_Q, ATTN_Q + ATTN_KV, ATTN_Q + 2 * ATTN_KV, \
        ATTN_Q + 2 * ATTN_KV + 3 * GDN_W, ATTN_Q + 2 * ATTN_KV + 4 * GDN_W
    n_slab = 3 * GDN_W // LANES

    def normed(x):
        ms = jnp.mean(x * x, axis=-1, keepdims=True)
        return (x * lax.rsqrt(ms + EPS) * nw_ref[...]).astype(BF16)

    n_sub = PROJ_TM // PROJ_SUB

    def to_slabs(xe_ref, y, row0):
        for sb in range(n_slab):
            xe_ref[sb, row0:row0 + y.shape[0], :] = y[:, sb * LANES:(sb + 1) * LANES]

    def put(k, y):
        to_slabs(xe_refs[k], y, SUBLANES)
        if k > 0:
            to_slabs(xe_refs[k - 1], y[:SUBLANES], SUBLANES + PROJ_SUB)
        if k + 1 < n_sub:
            to_slabs(xe_refs[k + 1], y[PROJ_SUB - SUBLANES:], 0)

    i = step % tiles_per_seq
    outs = ((qa_ref, o_q, ATTN_Q, ATTN_HEAD_DIM ** -0.5 * LOG2E), (ka_ref, o_k, ATTN_KV, None), (va_ref, o_v, ATTN_KV, None),
            (z_ref, o_z, GDN_W, None), (ab_ref, o_ab, LANES, None))
    for k in range(n_sub):
        rows = slice(k * PROJ_SUB, (k + 1) * PROJ_SUB)
        if k == 0:
            xg = normed(jnp.concatenate([xp_ref[...], xn_ref[...], x_ref[rows, :]], axis=0))
            yg = jnp.dot(xg, wb_ref[:, o_g:o_z], preferred_element_type=F32)
            to_slabs(xe_refs[0], jnp.where(i > 0, yg[:SUBLANES], 0.0), 0)
            to_slabs(xe_refs[n_sub - 1], jnp.where(i < tiles_per_seq - 1, yg[SUBLANES:2 * SUBLANES], 0.0),
                     SUBLANES + PROJ_SUB)
            put(0, yg[2 * SUBLANES:])
            xn = xg[2 * SUBLANES:]
        else:
            xn = normed(x_ref[rows, :])
            put(k, jnp.dot(xn, wb_ref[:, o_g:o_z], preferred_element_type=F32))
            _conv_silu_norm(xe_refs[k - 1], cw_ref, yn_ref, (k - 1) * PROJ_SUB, PROJ_SUB)
        for ref, c0, width, scale in outs:
            y = jnp.dot(xn, wb_ref[:, c0:c0 + width], preferred_element_type=F32)
            if scale is not None:
                y = y * scale
            ref[rows, :] = y.astype(ref.dtype)
    _conv_silu_norm(xe_refs[n_sub - 1], cw_ref, yn_ref, PROJ_TM - PROJ_SUB, PROJ_SUB)


def _proj(x2, norm_w, w_in, conv_w, layer, seq_len):
    n = x2.shape[0]
    d_in = w_in.shape[2]
    d_pad = d_in // LANES * LANES + LANES
    assert d_pad == ATTN_Q + 2 * ATTN_KV + 4 * GDN_W + LANES and d_in - (d_pad - LANES) == 2 * N_CHAIN
    assert seq_len % PROJ_TM == 0
    w_t = jnp.swapaxes(w_in, 1, 2)
    nh8 = PROJ_TM // SUBLANES
    row = lambda w: pl.BlockSpec((PROJ_TM, w), lambda i: (i, 0))
    return pl.pallas_call(
        functools.partial(_proj_kernel, layer, seq_len // PROJ_TM),
        grid=(n // PROJ_TM,),
        in_specs=[row(D_MODEL),
                  pl.BlockSpec((SUBLANES, D_MODEL), lambda i: (jnp.maximum(i * nh8 - 1, 0), 0)),
                  pl.BlockSpec((SUBLANES, D_MODEL), lambda i: (jnp.minimum((i + 1) * nh8, n // SUBLANES - 1), 0)),
                  pl.BlockSpec((1, D_MODEL), lambda i: (0, 0)),
                  pl.BlockSpec(memory_space=pl.ANY),
                  pl.BlockSpec((SUBLANES, 3 * GDN_W), lambda i: (0, 0))],
        out_specs=[row(ATTN_Q), row(ATTN_KV), row(ATTN_KV),
                   pl.BlockSpec((3 * GDN_W // LANES, PROJ_TM, LANES), lambda i: (0, i, 0)),
                   row(GDN_W), row(LANES)],
        out_shape=[jax.ShapeDtypeStruct((n, ATTN_Q), BF16),
                   jax.ShapeDtypeStruct((n, ATTN_KV), BF16),
                   jax.ShapeDtypeStruct((n, ATTN_KV), BF16),
                   jax.ShapeDtypeStruct((3 * GDN_W // LANES, n, LANES), F32),
                   jax.ShapeDtypeStruct((n, GDN_W), F32),
                   jax.ShapeDtypeStruct((n, LANES), F32)],
        scratch_shapes=[pltpu.VMEM((D_MODEL, d_pad), BF16),
                        pltpu.VMEM((2, PROJ_CAST_ROWS, D_MODEL), F32),
                        pltpu.SemaphoreType.DMA((2,))]
        + [pltpu.VMEM((3 * GDN_W // LANES, PROJ_SUB + 2 * SUBLANES, LANES), F32)] * (PROJ_TM // PROJ_SUB),
        compiler_params=pltpu.CompilerParams(dimension_semantics=("arbitrary",),
                                             vmem_limit_bytes=VMEM_LIMIT),
        name="proj",
    )(x2, x2, x2, norm_w, w_t, conv_w)


def _bias_kernel(relb_ref, bucket_ref, o_ref):
    bucket = bucket_ref[...]
    key = lax.broadcasted_iota(jnp.int32, (3 * BLOCK, BLOCK), 0)
    qry = lax.broadcasted_iota(jnp.int32, (3 * BLOCK, BLOCK), 1)
    in_window = jnp.abs(key - BLOCK - qry) <= WINDOW
    for h in range(ATTN_HEADS):
        acc = jnp.zeros((3 * BLOCK, BLOCK), F32)
        for b in range(N_BUCKETS):
            acc = jnp.where(bucket == b, relb_ref[b, h], acc)
        o_ref[h // 2, :, (h % 2) * BLOCK:(h % 2 + 1) * BLOCK] = jnp.where(in_window, acc * LOG2E, -1e30)


def _bias_band(rel_bias, bucket_t):
    shape = (ATTN_HEADS // 2, 3 * BLOCK, 2 * BLOCK)
    return pl.pallas_call(
        _bias_kernel,
        in_specs=[pl.BlockSpec(memory_space=pltpu.SMEM),
                  pl.BlockSpec((3 * BLOCK, BLOCK), lambda: (0, 0))],
        out_specs=pl.BlockSpec(shape, lambda: (0, 0, 0)),
        out_shape=jax.ShapeDtypeStruct(shape, F32),
        name="bias_band",
    )(rel_bias, bucket_t)


def _t5_buckets(rel):
    nb = N_BUCKETS // 2
    max_exact = nb // 2
    base = jnp.where(rel > 0, nb, 0)
    n = jnp.abs(rel)
    log_ratio = jnp.log(jnp.maximum(n, 1).astype(jnp.float32) / max_exact) / math.log(MAX_DISTANCE / max_exact)
    large = jnp.minimum(max_exact + (log_ratio * (nb - max_exact)).astype(jnp.int32), nb - 1)
    return base + jnp.where(n < max_exact, n, large)


def _attn_kernel(sink_ref, q_ref, kp_ref, kc_ref, kn_ref, vp_ref, vc_ref, vn_ref, bias_ref, o_ref):
    n = pl.program_id(1)
    last = pl.num_programs(1) - 1
    kband = jnp.concatenate([kp_ref[...], kc_ref[...], kn_ref[...]], axis=0)
    vband = jnp.concatenate([vp_ref[...], vc_ref[...], vn_ref[...]], axis=0)
    vband_t = vband.astype(F32).T.astype(BF16)
    key = lax.broadcasted_iota(jnp.int32, (3 * BLOCK, 1), 0)
    first_head = lax.broadcasted_iota(jnp.int32, (1, 2 * BLOCK), 1) < BLOCK
    head = lambda t, i: t[:, i * ATTN_HEAD_DIM:(i + 1) * ATTN_HEAD_DIM]
    n_pairs = ATTN_HEADS // 2
    kv_of = lambda pr: (2 * pr) // ATTN_GROUP
    units = [(j, pr) for j in range(ATTN_QB) for pr in range(n_pairs)]
    band_rows = lambda j: slice(j * BLOCK, (j + 3) * BLOCK)

    scores = {}
    for j, pr in units:
        qj = q_ref[j * BLOCK:(j + 1) * BLOCK, :]
        q2 = jnp.concatenate([head(qj, 2 * pr), head(qj, 2 * pr + 1)], axis=0)
        scores[j, pr] = _dot_nt(head(kband, kv_of(pr))[band_rows(j)], q2)
    probs, dens = {}, {}
    for j, pr in units:
        s = scores[j, pr] + bias_ref[pr]
        if j == 0:
            s = jnp.where((key < BLOCK) & (n == 0), -1e30, s)
        if j == ATTN_QB - 1:
            s = jnp.where((key >= 2 * BLOCK) & (n == last), -1e30, s)
        sink = jnp.where(first_head, sink_ref[2 * pr], sink_ref[2 * pr + 1]) * LOG2E
        m = jnp.maximum(jnp.max(s, axis=0, keepdims=True), sink)
        p = jnp.exp2(s - m)
        dens[j, pr] = jnp.sum(p, axis=0, keepdims=True) + jnp.exp2(sink - m)
        probs[j, pr] = p.astype(BF16)
    outs_t = {}
    for j, pr in units:
        kv = kv_of(pr)
        v_t = vband_t[kv * ATTN_HEAD_DIM:(kv + 1) * ATTN_HEAD_DIM, band_rows(j)]
        outs_t[j, pr] = jnp.dot(v_t, probs[j, pr], preferred_element_type=F32) / dens[j, pr]
    for j in range(ATTN_QB):
        o_t = jnp.concatenate([outs_t[j, pr][:, half * BLOCK:(half + 1) * BLOCK]
                               for pr in range(n_pairs) for half in range(2)], axis=0)
        o_ref[j * BLOCK:(j + 1) * BLOCK, :] = o_t.T.astype(o_ref.dtype)


def _attention(q_a, k_a, v_a, band, sink, batch):
    n_tok = q_a.shape[0]
    nb = n_tok // batch // BLOCK
    rows = ATTN_QB * BLOCK
    steps = nb // ATTN_QB
    kv_spec = lambda r, f: pl.BlockSpec((r, ATTN_KV), f)
    prev = lambda bi, n: (bi * nb + jnp.maximum(n * ATTN_QB - 1, 0), 0)
    cur = lambda bi, n: (bi * steps + n, 0)
    nxt = lambda bi, n: (bi * nb + jnp.minimum((n + 1) * ATTN_QB, nb - 1), 0)
    return pl.pallas_call(
        _attn_kernel,
        grid=(batch, steps),
        in_specs=[pl.BlockSpec(memory_space=pltpu.SMEM),
                  pl.BlockSpec((rows, ATTN_Q), cur),
                  kv_spec(BLOCK, prev), kv_spec(rows, cur), kv_spec(BLOCK, nxt),
                  kv_spec(BLOCK, prev), kv_spec(rows, cur), kv_spec(BLOCK, nxt),
                  pl.BlockSpec(band.shape, lambda bi, n: (0, 0, 0))],
        out_specs=pl.BlockSpec((rows, ATTN_Q), cur),
        out_shape=jax.ShapeDtypeStruct((n_tok, ATTN_Q), BF16),
        compiler_params=pltpu.CompilerParams(dimension_semantics=("arbitrary", "arbitrary"),
                                             vmem_limit_bytes=VMEM_LIMIT),
        name="attn",
    )(sink, q_a, k_a, k_a, k_a, v_a, v_a, v_a, band)


def _gprep_kernel(yn_ref, ab_ref, gp_ref, w_ref, u_ref, qd_ref, kdt_ref, qk_ref, egl_ref):
    t_len = PREP_T

    ab = ab_ref[0]
    sp_in = ab + gp_ref[1:2, :]
    softplus = jnp.maximum(sp_in, 0.0) + jnp.log1p(jnp.exp(-jnp.abs(sp_in)))
    g = -jnp.exp(gp_ref[0:1, :]) * softplus
    beta = jax.nn.sigmoid(ab)

    r_t = lax.broadcasted_iota(jnp.int32, (t_len, t_len), 0)
    c_t = lax.broadcasted_iota(jnp.int32, (t_len, t_len), 1)
    same = (r_t // CHUNK) == (c_t // CHUNK)
    lower = jnp.where(same & (r_t >= c_t), 1.0, 0.0).astype(BF16)
    upper = jnp.where(same & (r_t <= c_t), 1.0, 0.0).astype(BF16)
    g_t = g.T[:2 * SUBLANES]
    cs_row = (_dot01_right(g_t, upper), _dot01_right(g_t, lower))
    pad_rows = jnp.zeros((LANES - 2 * SUBLANES, t_len), F32)
    cs_col = tuple(jnp.concatenate([r, pad_rows], axis=0).T for r in cs_row)

    qs = [yn_ref[h] for h in range(GDN_HEADS)]
    ks = [yn_ref[GDN_HEADS + h] for h in range(GDN_HEADS)]
    vs = [yn_ref[2 * GDN_HEADS + h] for h in range(GDN_HEADS)]
    kts = [kh.T for kh in ks]

    r_c = lax.broadcasted_iota(jnp.int32, (CHUNK, LANES), 0)
    lane = lax.broadcasted_iota(jnp.int32, (CHUNK, LANES), 1)
    is_fwd = lane < CHUNK
    c_c = lane % CHUNK
    eye = jnp.where(r_c == c_c, 1.0, 0.0).astype(F32)
    incl = (is_fwd & (r_c >= c_c)) | (~is_fwd & (r_c <= c_c))
    strict = (is_fwd & (r_c > c_c)) | (~is_fwd & (r_c < c_c))
    r_d = lax.broadcasted_iota(jnp.int32, (2 * CHUNK, LANES), 0)
    c_d = lax.broadcasted_iota(jnp.int32, (2 * CHUNK, LANES), 1)
    same_dir = (r_d // CHUNK) == (c_d // CHUNK)
    level_mask = lambda s_, r_, c_: ((r_ // (2 * s_)) == (c_ // (2 * s_))) & ((r_ // s_) != (c_ // s_))
    stack2 = lambda t: jnp.concatenate([t, t], axis=0)

    n_chunks = t_len // CHUNK
    rows = lambda c: slice(c * CHUNK, (c + 1) * CHUNK)
    pairs = [(c, h) for c in range(n_chunks) for h in range(GDN_HEADS)]
    qkk = {}
    for c, h in pairs:
        k16 = ks[h][rows(c)].astype(BF16)
        qk16 = jnp.concatenate([qs[h][rows(c)].astype(BF16), k16], axis=0)
        qkk[c, h] = _dot_nt(qk16, stack2(k16))

    bcast = lambda col: jnp.broadcast_to(col, (CHUNK, LANES))
    g_full, b_full, grow, glast, decay, a_mat, t_mat = {}, {}, {}, {}, {}, {}, {}
    for key in pairs:
        c, h = key
        for d in range(N_DIR):
            j = d * GDN_HEADS + h
            r_last = c * CHUNK + (CHUNK - 1 if d == 0 else 0)
            g_full[key, d] = bcast(cs_col[d][rows(c), j:j + 1])
            b_full[key, d] = bcast(beta[rows(c), SUBLANES + j:SUBLANES + j + 1])
            glast[key, d] = cs_col[d][r_last:r_last + 1, j:j + 1]
            grow[key, d] = cs_row[d][j:j + 1, rows(c)]
        gcol2 = jnp.where(is_fwd, g_full[key, 0], g_full[key, 1])
        bcol2 = jnp.where(is_fwd, b_full[key, 0], b_full[key, 1])
        grow2 = jnp.concatenate([grow[key, 0], grow[key, 1]], axis=1)
        decay[key] = jnp.exp(jnp.where(incl, gcol2 - grow2, -jnp.inf))
        a_mat[key] = jnp.where(strict, bcol2 * qkk[key][CHUNK:] * decay[key], 0.0)
        t_mat[key] = eye - jnp.where(level_mask(1, r_c, c_c), a_mat[key], 0.0)

    same_dir16 = jnp.where(same_dir, 1.0, 0.0).astype(BF16)
    block_diag = lambda t16: stack2(t16) * same_dir16
    a_bd = {key: block_diag(a_mat[key].astype(BF16)) for key in pairs}
    s = 2
    while s < CHUNK:
        lvl = level_mask(s, r_c, c_c)
        t16 = {key: t_mat[key].astype(BF16) for key in pairs}
        x_mat = {key: jnp.dot(t16[key], a_bd[key], preferred_element_type=F32) for key in pairs}
        y_mat = {key: jnp.dot(x_mat[key].astype(BF16), block_diag(t16[key]), preferred_element_type=F32)
                 for key in pairs}
        t_mat = {key: t_mat[key] - jnp.where(lvl, y_mat[key], 0.0) for key in pairs}
        s *= 2

    wu = {}
    for key in pairs:
        c, h = key
        kv = jnp.concatenate(
            [jnp.concatenate([ks[h][rows(c)] * (b_full[key, d] * jnp.exp(g_full[key, d])),
                              vs[h][rows(c)] * b_full[key, d]], axis=1) for d in range(N_DIR)], axis=0)
        t_sel = block_diag(t_mat[key].astype(BF16))
        wu[key] = jnp.dot(t_sel, kv.astype(BF16), preferred_element_type=F32)

    for c in range(n_chunks):
        egl_rows = []
        for d in range(N_DIR):
            for h in range(GDN_HEADS):
                key = (c, h)
                j = d * GDN_HEADS + h
                wu_d = wu[key][d * CHUNK:(d + 1) * CHUNK]
                w_ref[0, j, rows(c), :] = wu_d[:, :GDN_HEAD_DIM].astype(w_ref.dtype)
                u_ref[0, j, rows(c), :] = wu_d[:, GDN_HEAD_DIM:].astype(u_ref.dtype)
                qd_ref[0, j, rows(c), :] = (qs[h][rows(c)] * jnp.exp(g_full[key, d])).astype(qd_ref.dtype)
                egl_rows.append(jnp.broadcast_to(jnp.exp(glast[key, d]), (1, LANES)))
        egl_ref[0, c] = jnp.concatenate(egl_rows, axis=0)
        for h in range(GDN_HEADS):
            qk_ref[0, h, rows(c), :] = (qkk[c, h][:CHUNK] * decay[c, h]).astype(qk_ref.dtype)
    for cp in range(n_chunks // 2):
        for d in range(N_DIR):
            for h in range(GDN_HEADS):
                fac = jnp.concatenate([jnp.exp(glast[(c, h), d] - grow[(c, h), d]) for c in (2 * cp, 2 * cp + 1)],
                                      axis=1)
                kdt = kts[h][:, 2 * cp * CHUNK:(2 * cp + 2) * CHUNK] * fac
                kdt_ref[0, d * GDN_HEADS + h, cp] = kdt.astype(kdt_ref.dtype)


def _gdn_prep(yn, ab, gate_par):
    b, s, _ = ab.shape
    nt = s // PREP_T
    cpb = PREP_T // CHUNK
    nc = s // CHUNK
    chain = lambda last: pl.BlockSpec((1, N_CHAIN, PREP_T, last), lambda bi, i: (bi, 0, i, 0))
    return pl.pallas_call(
        _gprep_kernel,
        grid=(b, nt),
        in_specs=[pl.BlockSpec((3 * GDN_W // LANES, PREP_T, LANES), lambda bi, i: (0, bi * nt + i, 0)),
                  pl.BlockSpec((1, PREP_T, LANES), lambda bi, i: (bi, i, 0)),
                  pl.BlockSpec((SUBLANES, LANES), lambda bi, i: (0, 0))],
        out_specs=[chain(GDN_HEAD_DIM), chain(GDN_HEAD_DIM), chain(GDN_HEAD_DIM),
                   pl.BlockSpec((1, N_CHAIN, cpb // 2, GDN_HEAD_DIM, 2 * CHUNK), lambda bi, i: (bi, 0, i, 0, 0)),
                   pl.BlockSpec((1, GDN_HEADS, PREP_T, LANES), lambda bi, i: (bi, 0, i, 0)),
                   pl.BlockSpec((1, cpb, N_CHAIN, LANES), lambda bi, i: (bi, i, 0, 0))],
        out_shape=[jax.ShapeDtypeStruct((b, N_CHAIN, s, GDN_HEAD_DIM), BF16),
                   jax.ShapeDtypeStruct((b, N_CHAIN, s, GDN_HEAD_DIM), BF16),
                   jax.ShapeDtypeStruct((b, N_CHAIN, s, GDN_HEAD_DIM), BF16),
                   jax.ShapeDtypeStruct((b, N_CHAIN, nc // 2, GDN_HEAD_DIM, 2 * CHUNK), BF16),
                   jax.ShapeDtypeStruct((b, GDN_HEADS, s, LANES), BF16),
                   jax.ShapeDtypeStruct((b, nc, N_CHAIN, LANES), F32)],
        compiler_params=pltpu.CompilerParams(dimension_semantics=("arbitrary", "arbitrary"),
                                             vmem_limit_bytes=VMEM_LIMIT),
        name="gdn_prep",
    )(yn, ab, gate_par)


def _gscan_kernel(egl_ref, wf_ref, wb_ref, uf_ref, ub_ref, qf_ref, qb_ref, kf_ref, kb_ref,
                  pf_ref, pb_ref, of_ref, ob_ref, state_ref):
    t = pl.program_id(0)
    n_steps = pl.num_programs(0)
    nc = n_steps * SCAN_CHUNKS
    n_batch = wf_ref.shape[0]

    @pl.when(t == 0)
    def _():
        state_ref[...] = jnp.zeros_like(state_ref)

    dirs = ((wf_ref, uf_ref, qf_ref, kf_ref, pf_ref, of_ref), (wb_ref, ub_ref, qb_ref, kb_ref, pb_ref, ob_ref))
    chains = [(bi, d, h) for bi in range(n_batch) for d in range(N_DIR) for h in range(GDN_HEADS)]
    slot = lambda bi, d, h: (bi * N_DIR + d) * GDN_HEADS + h
    st = {key: state_ref[slot(*key)] for key in chains}

    for sub in range(SCAN_CHUNKS):
        local = (sub, SCAN_CHUNKS - 1 - sub)
        rows = [slice(c * CHUNK, (c + 1) * CHUNK) for c in local]
        chunk = (t * SCAN_CHUNKS + sub, nc - 1 - (t * SCAN_CHUNKS + sub))
        r = {}
        for key in chains:
            bi, d, h = key
            w_ref, _, q_ref = dirs[d][:3]
            wq = jnp.concatenate([w_ref[bi, h, rows[d]], q_ref[bi, h, rows[d]]], axis=0)
            r[key] = _dot(wq, st[key])
        v_pad, intra = {}, {}
        for key in chains:
            bi, d, h = key
            u_ref, p_ref = dirs[d][1], dirs[d][4]
            v_new = (u_ref[bi, h, rows[d]].astype(F32) - r[key][:CHUNK]).astype(BF16)
            zeros = jnp.zeros_like(v_new)
            v_pad[key] = (jnp.concatenate([v_new, zeros], axis=0), jnp.concatenate([zeros, v_new], axis=0))
            intra[key] = jnp.dot(p_ref[bi, h, rows[d]], v_pad[key][d], preferred_element_type=F32)
        for key in chains:
            bi, d, h = key
            k_ref = dirs[d][3]
            egl = egl_ref[(bi * nc + chunk[d]) * N_CHAIN + d * GDN_HEADS + h]
            st[key] = st[key] * egl + jnp.dot(k_ref[bi, h, local[d] // 2], v_pad[key][local[d] % 2],
                                              preferred_element_type=F32)
        for bi in range(n_batch):
            for d in range(N_DIR):
                o_ref = dirs[d][5]
                o_ref[bi, rows[d], :] = jnp.concatenate(
                    [r[bi, d, h][CHUNK:] + intra[bi, d, h] for h in range(GDN_HEADS)], axis=-1).astype(o_ref.dtype)

    for key in chains:
        state_ref[slot(*key)] = st[key]


def _gdn_scan(egl, w, u, qd, kdt, qk):
    b, _, s, _ = w.shape
    rows = SCAN_CHUNKS * CHUNK
    n_steps = s // rows
    fwd_i = lambda t: t
    bwd_i = lambda t: n_steps - 1 - t
    chain = lambda d, at, last: pl.BlockSpec((b, GDN_HEADS, rows, last), lambda t: (0, d, at(t), 0))
    kspec = lambda d, at: pl.BlockSpec((b, GDN_HEADS, SCAN_CHUNKS // 2, GDN_HEAD_DIM, 2 * CHUNK),
                                       lambda t: (0, d, at(t), 0, 0))
    dk = GDN_HEAD_DIM
    return pl.pallas_call(
        _gscan_kernel,
        grid=(n_steps,),
        in_specs=[pl.BlockSpec(memory_space=pltpu.SMEM),
                  chain(0, fwd_i, dk), chain(1, bwd_i, dk), chain(0, fwd_i, dk), chain(1, bwd_i, dk),
                  chain(0, fwd_i, dk), chain(1, bwd_i, dk), kspec(0, fwd_i), kspec(1, bwd_i),
                  chain(0, fwd_i, LANES), chain(0, bwd_i, LANES)],
        out_specs=[pl.BlockSpec((b, rows, GDN_W), lambda t: (0, fwd_i(t), 0)),
                   pl.BlockSpec((b, rows, GDN_W), lambda t: (0, bwd_i(t), 0))],
        out_shape=[jax.ShapeDtypeStruct((b, s, GDN_W), BF16),
                   jax.ShapeDtypeStruct((b, s, GDN_W), BF16)],
        scratch_shapes=[pltpu.VMEM((b * N_CHAIN, GDN_HEAD_DIM, GDN_HEAD_DIM), F32)],
        compiler_params=pltpu.CompilerParams(dimension_semantics=("arbitrary",),
                                             vmem_limit_bytes=VMEM_LIMIT),
        name="gdn_scan",
    )(egl, w, w, u, u, qd, qd, kdt, kdt, qk, qk)


def _cast_weights_once(layer, copies):
    jobs, used = [], {}
    for src, dst, stage, sem, chunk_rows in copies:
        for r0 in range(0, src.shape[1], chunk_rows):
            slot = used.get(id(stage), 0) % 2
            used[id(stage)] = used.get(id(stage), 0) + 1
            rows = pl.ds(r0, chunk_rows)
            jobs.append((pltpu.make_async_copy(src.at[layer, rows], stage.at[slot], sem.at[slot]),
                         stage, slot, dst, rows))
    jobs[0][0].start()
    for k, (copy, stage, slot, dst, rows) in enumerate(jobs):
        if k + 1 < len(jobs):
            jobs[k + 1][0].start()
        copy.wait()
        dst[rows, :] = stage[slot].astype(BF16)


def _ffn_kernel(layer, x_ref, attn_ref, of_ref, ob_ref, z_ref, gnw_ref, wo_hbm, fnw_ref, w1_hbm, w2_hbm, onw_ref,
                o_ref, wo_ref, w1_ref, w2_ref, stage_wide, stage_tall, sem_wide, sem_tall):
    @pl.when(pl.program_id(0) == 0)
    def _():
        _cast_weights_once(layer, [(w1_hbm, w1_ref, stage_wide, sem_wide, FFN_CAST_ROWS_WIDE),
                                   (w2_hbm, w2_ref, stage_tall, sem_tall, FFN_CAST_ROWS_TALL),
                                   (wo_hbm, wo_ref, stage_tall, sem_tall, FFN_CAST_ROWS_TALL)])

    subs = [slice(r0, r0 + FFN_SUB) for r0 in range(0, FFN_TM, FFN_SUB)]
    rms = lambda t: t * lax.rsqrt(jnp.mean(t * t, axis=-1, keepdims=True) + EPS)
    hres, hn, act, acc = {}, {}, {}, {}
    for r in subs:
        o = of_ref[r, :].astype(F32) + ob_ref[r, :].astype(F32)
        heads = [rms(o[:, h * GDN_HEAD_DIM:(h + 1) * GDN_HEAD_DIM]) * gnw_ref[...] for h in range(GDN_HEADS)]
        z = z_ref[r, :]
        gdn = jnp.concatenate(heads, axis=-1) * (z * jax.nn.sigmoid(z))
        hres[r.start] = (x_ref[r, :] + jnp.dot(attn_ref[r, :], wo_ref[:ATTN_Q, :], preferred_element_type=F32)
                         + _dot(gdn, wo_ref[ATTN_Q:, :]))
    for r in subs:
        hn[r.start] = (rms(hres[r.start]) * fnw_ref[...]).astype(BF16)
        act[r.start] = jnp.dot(hn[r.start], w1_ref[...], preferred_element_type=F32)
    for r in subs:
        a = jnp.square(jnp.maximum(act[r.start], 0.0)).astype(BF16)
        acc[r.start] = hres[r.start] + jnp.dot(a, w2_ref[...], preferred_element_type=F32)
    for r in subs:
        o_ref[r, :] = rms(acc[r.start]) * onw_ref[...]


def _out_ffn(x2, attn, o_f, o_b, z, gnw, wo, fnw, w1, w2, onw, layer):
    n = x2.shape[0]
    assert wo.shape[1:] == (D_MODEL, D_MODEL) and w1.shape[1:] == (D_MODEL, D_FF) and w2.shape[1:] == (D_FF, D_MODEL)
    row = lambda w: pl.BlockSpec((FFN_TM, w), lambda i: (i, 0))
    full = lambda a: pl.BlockSpec(a.shape, lambda i: (0, 0))
    hbm = pl.BlockSpec(memory_space=pl.ANY)
    return pl.pallas_call(
        functools.partial(_ffn_kernel, layer),
        grid=(n // FFN_TM,),
        in_specs=[row(D_MODEL), row(ATTN_Q), row(GDN_W), row(GDN_W), row(GDN_W),
                  full(gnw), hbm, full(fnw), hbm, hbm, full(onw)],
        out_specs=row(D_MODEL),
        out_shape=jax.ShapeDtypeStruct((n, D_MODEL), F32),
        scratch_shapes=[pltpu.VMEM((D_MODEL, D_MODEL), BF16),
                        pltpu.VMEM((D_MODEL, D_FF), BF16),
                        pltpu.VMEM((D_FF, D_MODEL), BF16),
                        pltpu.VMEM((2, FFN_CAST_ROWS_WIDE, D_FF), F32),
                        pltpu.VMEM((2, FFN_CAST_ROWS_TALL, D_MODEL), F32),
                        pltpu.SemaphoreType.DMA((2,)),
                        pltpu.SemaphoreType.DMA((2,))],
        compiler_params=pltpu.CompilerParams(dimension_semantics=("arbitrary",),
                                             vmem_limit_bytes=VMEM_LIMIT),
        name="out_ffn",
    )(x2, attn, o_f, o_b, z, gnw, wo, fnw, w1, w2, onw)


def _layer(h, band, norm_mix_w, w_in, layer, attn_sink, conv_w, gdn_a_log, gdn_dt_bias, gdn_norm_w,
           w_out, norm_ffn_w, w_ffn_in, w_ffn_out, out_norm_w):
    b, s, _ = h.shape
    n = b * s
    x2 = h.reshape(n, D_MODEL)
    conv_pad = jnp.zeros((SUBLANES, 3 * GDN_W), F32).at[:CONV_K].set(conv_w)
    q_a, k_a, v_a, yn, z_g, ab = _proj(x2, norm_mix_w.reshape(1, D_MODEL), w_in, conv_pad, layer, s)

    attn = _attention(q_a, k_a, v_a, band, attn_sink, b)

    gate_par = jnp.zeros((SUBLANES, LANES), F32)
    gate_par = gate_par.at[0, :N_CHAIN].set(gdn_a_log.reshape(-1)).at[1, :N_CHAIN].set(gdn_dt_bias.reshape(-1))
    w_c, u_c, q_dec, k_dec_t, qk, egl = _gdn_prep(yn, ab.reshape(b, s, LANES), gate_par)
    o_f, o_b = _gdn_scan(egl[..., 0].reshape(-1), w_c, u_c, q_dec, k_dec_t, qk)

    out = _out_ffn(x2, attn, o_f.reshape(n, GDN_W), o_b.reshape(n, GDN_W), z_g,
                   gdn_norm_w.reshape(1, GDN_HEAD_DIM), w_out, norm_ffn_w.reshape(1, D_MODEL),
                   w_ffn_in, w_ffn_out, out_norm_w.reshape(1, D_MODEL), layer)
    return out.reshape(b, s, D_MODEL)


def kernel(x, norm_mix_w, w_in, rel_bias, attn_sink, conv_w, gdn_a_log, gdn_dt_bias, gdn_norm_w, w_out,
           norm_ffn_w, w_ffn_in, w_ffn_out, norm_final_w):
    depth = w_in.shape[0]
    assert depth == 1, "the fused output kernel applies the final norm after the single trunk layer"
    rel = (np.arange(3 * BLOCK)[None, :] - BLOCK) - np.arange(BLOCK)[:, None]
    bucket = _t5_buckets(jnp.asarray(rel, dtype=jnp.int32))
    band = _bias_band(rel_bias, bucket.T)
    return _layer(x, band, norm_mix_w[0], w_in, 0, attn_sink[0], conv_w[0], gdn_a_log[0], gdn_dt_bias[0],
                  gdn_norm_w[0], w_out, norm_ffn_w[0], w_ffn_in, w_ffn_out, norm_final_w)
```

```python
import functools
import math

import jax
import jax.numpy as jnp
import numpy as np
from jax import lax
from jax.experimental import pallas as pl
from jax.experimental.pallas import tpu as pltpu

F32 = jnp.float32
BF16 = jnp.bfloat16

D_MODEL = 1024
ATTN_HEADS = 8
ATTN_KV_HEADS = 2
ATTN_HEAD_DIM = 64
ATTN_GROUP = ATTN_HEADS // ATTN_KV_HEADS
WINDOW = 128
BLOCK = 128
N_BUCKETS = 32
MAX_DISTANCE = 128
GDN_HEADS = 4
GDN_HEAD_DIM = 128
CONV_K = 5
CHUNK = 64
N_DIR = 2
N_CHAIN = N_DIR * GDN_HEADS
D_FF = 4 * D_MODEL
EPS = 1e-6
LOG2E = math.log2(math.e)
ATTN_Q = ATTN_HEADS * ATTN_HEAD_DIM
ATTN_KV = ATTN_KV_HEADS * ATTN_HEAD_DIM
GDN_W = GDN_HEADS * GDN_HEAD_DIM
LANES = 128
SUBLANES = 8
VMEM_LIMIT = 56 * 1024 * 1024

PROJ_TM = 1024
PROJ_SUB = 256
PROJ_CAST_ROWS = 128
ATTN_QB = 8
PREP_T = 1024
ROW_STRIDE = 4
SCAN_CHUNKS = 8
FFN_TM = 512
FFN_SUB = 256
FFN_CAST_ROWS_WIDE = 128
FFN_CAST_ROWS_TALL = 512


def _dot(a, b):
    return jnp.dot(a.astype(BF16), b.astype(BF16), preferred_element_type=F32)


def _dot_nt(a, b):
    return lax.dot_general(a.astype(BF16), b.astype(BF16), (((1,), (1,)), ((), ())),
                           preferred_element_type=F32)


def _split3(x):
    hi = x.astype(BF16)
    r1 = x - hi.astype(F32)
    mid = r1.astype(BF16)
    lo = (r1 - mid.astype(F32)).astype(BF16)
    return hi, mid, lo


def _dot01_right(x, m01):
    hi, mid, lo = _split3(x)
    d = lambda p: jnp.dot(p, m01, preferred_element_type=F32)
    return d(hi) + d(mid) + d(lo)


def _conv_silu_norm(xe_ref, cw_ref, yn_ref, r0, n_out):
    halo = CONV_K // 2
    n_rows = n_out // ROW_STRIDE
    n_slab = 3 * GDN_W // LANES
    units = [(sb, ph) for sb in range(n_slab) for ph in range(ROW_STRIDE)]
    half_taps = [cw_ref[j:j + 1, :] * 0.5 for j in range(CONV_K)]
    yv = {}
    for sb, ph in units:
        lanes = slice(sb * LANES, (sb + 1) * LANES)
        acc = None
        for j in range(CONV_K):
            win = xe_ref[sb, pl.ds(SUBLANES - halo + j + ph, n_rows, stride=ROW_STRIDE), :]
            term = half_taps[j][:, lanes] * win
            acc = term if acc is None else acc + term
        yv[sb, ph] = acc
    for key in units:
        yv[key] = yv[key] + yv[key] * jnp.tanh(yv[key])
    for sb, ph in units:
        if sb < 2 * GDN_HEADS:
            scale = lax.rsqrt(jnp.sum(yv[sb, ph] * yv[sb, ph], axis=-1, keepdims=True) + EPS)
            if sb < GDN_HEADS:
                scale = scale * (GDN_HEAD_DIM ** -0.5)
            yv[sb, ph] = yv[sb, ph] * scale
    for sb, ph in units:
        yn_ref[sb, pl.ds(r0 + ph, n_rows, stride=ROW_STRIDE), :] = yv[sb, ph]


def _proj_kernel(layer, tiles_per_seq, x_ref, xp_ref, xn_ref, nw_ref, wt_hbm, cw_ref,
                 qa_ref, ka_ref, va_ref, yn_ref, z_ref, ab_ref, wb_ref, stage_ref, sem, *xe_refs):
    d_in = wt_hbm.shape[1]
    d_main = d_in // LANES * LANES
    step = pl.program_id(0)

    @pl.when(step == 0)
    def _():
        starts = list(range(0, d_main, PROJ_CAST_ROWS)) + [d_main]
        sizes = [PROJ_CAST_ROWS] * (len(starts) - 1) + [d_in - d_main]
        copies = [pltpu.make_async_copy(wt_hbm.at[layer, pl.ds(c0, nr)], stage_ref.at[k % 2, pl.ds(0, nr)],
                                        sem.at[k % 2]) for k, (c0, nr) in enumerate(zip(starts, sizes))]
        lane = lax.broadcasted_iota(jnp.int32, (D_MODEL, PROJ_CAST_ROWS), 1)
        copies[0].start()
        for k, (c0, nr) in enumerate(zip(starts, sizes)):
            if k + 1 < len(copies):
                copies[k + 1].start()
            copies[k].wait()
            cols = stage_ref[k % 2].T
            if nr < PROJ_CAST_ROWS:
                cols = jnp.where(lane < nr, cols, 0.0)
            wb_ref[:, c0:c0 + PROJ_CAST_ROWS] = cols.astype(BF16)

    o_q, o_k, o_v, o_g, o_z, o_ab = 0, ATTN_Q, ATTN_Q + ATTN_KV, ATTN_Q + 2 * ATTN_KV, \
        ATTN_Q + 2 * ATTN_KV + 3 * GDN_W, ATTN_Q + 2 * ATTN_KV + 4 * GDN_W
    n_slab = 3 * GDN_W // LANES

    def normed(x):
        ms = jnp.mean(x * x, axis=-1, keepdims=True)
        return (x * lax.rsqrt(ms + EPS) * nw_ref[...]).astype(BF16)

    n_sub = PROJ_TM // PROJ_SUB

    def to_slabs(xe_ref, y, row0):
        for sb in range(n_slab):
            xe_ref[sb, row0:row0 + y.shape[0], :] = y[:, sb * LANES:(sb + 1) * LANES]

    def put(k, y):
        to_slabs(xe_refs[k], y, SUBLANES)
        if k > 0:
            to_slabs(xe_refs[k - 1], y[:SUBLANES], SUBLANES + PROJ_SUB)
        if k + 1 < n_sub:
            to_slabs(xe_refs[k + 1], y[PROJ_SUB - SUBLANES:], 0)

    i = step % tiles_per_seq
    outs = ((qa_ref, o_q, ATTN_Q, ATTN_HEAD_DIM ** -0.5 * LOG2E), (ka_ref, o_k, ATTN_KV, None), (va_ref, o_v, ATTN_KV, None),
            (z_ref, o_z, GDN_W, None), (ab_ref, o_ab, LANES, None))
    for k in range(n_sub):
        rows = slice(k * PROJ_SUB, (k + 1) * PROJ_SUB)
        if k == 0:
            xg = normed(jnp.concatenate([xp_ref[...], xn_ref[...], x_ref[rows, :]], axis=0))
            yg = jnp.dot(xg, wb_ref[:, o_g:o_z], preferred_element_type=F32)
            to_slabs(xe_refs[0], jnp.where(i > 0, yg[:SUBLANES], 0.0), 0)
            to_slabs(xe_refs[n_sub - 1], jnp.where(i < tiles_per_seq - 1, yg[SUBLANES:2 * SUBLANES], 0.0),
                     SUBLANES + PROJ_SUB)
            put(0, yg[2 * SUBLANES:])
            xn = xg[2 * SUBLANES:]
        else:
            xn = normed(x_ref[rows, :])
            put(k, jnp.dot(xn, wb_ref[:, o_g:o_z], preferred_element_type=F32))
            _conv_silu_norm(xe_refs[k - 1], cw_ref, yn_ref, (k - 1) * PROJ_SUB, PROJ_SUB)
        for ref, c0, width, scale in outs:
            y = jnp.dot(xn, wb_ref[:, c0:c0 + width], preferred_element_type=F32)
            if scale is not None:
                y = y * scale
            ref[rows, :] = y.astype(ref.dtype)
    _conv_silu_norm(xe_refs[n_sub - 1], cw_ref, yn_ref, PROJ_TM - PROJ_SUB, PROJ_SUB)


def _proj(x2, norm_w, w_in, conv_w, layer, seq_len):
    n = x2.shape[0]
    d_in = w_in.shape[2]
    d_pad = d_in // LANES * LANES + LANES
    assert d_pad == ATTN_Q + 2 * ATTN_KV + 4 * GDN_W + LANES and d_in - (d_pad - LANES) == 2 * N_CHAIN
    assert seq_len % PROJ_TM == 0
    w_t = jnp.swapaxes(w_in, 1, 2)
    nh8 = PROJ_TM // SUBLANES
    row = lambda w: pl.BlockSpec((PROJ_TM, w), lambda i: (i, 0))
    return pl.pallas_call(
        functools.partial(_proj_kernel, layer, seq_len // PROJ_TM),
        grid=(n // PROJ_TM,),
        in_specs=[row(D_MODEL),
                  pl.BlockSpec((SUBLANES, D_MODEL), lambda i: (jnp.maximum(i * nh8 - 1, 0), 0)),
                  pl.BlockSpec((SUBLANES, D_MODEL), lambda i: (jnp.minimum((i + 1) * nh8, n // SUBLANES - 1), 0)),
                  pl.BlockSpec((1, D_MODEL), lambda i: (0, 0)),
                  pl.BlockSpec(memory_space=pl.ANY),
                  pl.BlockSpec((SUBLANES, 3 * GDN_W), lambda i: (0, 0))],
        out_specs=[row(ATTN_Q), row(ATTN_KV), row(ATTN_KV),
                   pl.BlockSpec((3 * GDN_W // LANES, PROJ_TM, LANES), lambda i: (0, i, 0)),
                   row(GDN_W), row(LANES)],
        out_shape=[jax.ShapeDtypeStruct((n, ATTN_Q), BF16),
                   jax.ShapeDtypeStruct((n, ATTN_KV), BF16),
                   jax.ShapeDtypeStruct((n, ATTN_KV), BF16),
                   jax.ShapeDtypeStruct((3 * GDN_W // LANES, n, LANES), F32),
                   jax.ShapeDtypeStruct((n, GDN_W), F32),
                   jax.ShapeDtypeStruct((n, LANES), F32)],
        scratch_shapes=[pltpu.VMEM((D_MODEL, d_pad), BF16),
                        pltpu.VMEM((2, PROJ_CAST_ROWS, D_MODEL), F32),
                        pltpu.SemaphoreType.DMA((2,))]
        + [pltpu.VMEM((3 * GDN_W // LANES, PROJ_SUB + 2 * SUBLANES, LANES), F32)] * (PROJ_TM // PROJ_SUB),
        compiler_params=pltpu.CompilerParams(dimension_semantics=("arbitrary",),
                                             vmem_limit_bytes=VMEM_LIMIT),
        name="proj",
    )(x2, x2, x2, norm_w, w_t, conv_w)


def _bias_kernel(relb_ref, bucket_ref, o_ref):
    bucket = bucket_ref[...]
    key = lax.broadcasted_iota(jnp.int32, (3 * BLOCK, BLOCK), 0)
    qry = lax.broadcasted_iota(jnp.int32, (3 * BLOCK, BLOCK), 1)
    in_window = jnp.abs(key - BLOCK - qry) <= WINDOW
    for h in range(ATTN_HEADS):
        acc = jnp.zeros((3 * BLOCK, BLOCK), F32)
        for b in range(N_BUCKETS):
            acc = jnp.where(bucket == b, relb_ref[b, h], acc)
        o_ref[h // 2, :, (h % 2) * BLOCK:(h % 2 + 1) * BLOCK] = jnp.where(in_window, acc * LOG2E, -1e30)


def _bias_band(rel_bias, bucket_t):
    shape = (ATTN_HEADS // 2, 3 * BLOCK, 2 * BLOCK)
    return pl.pallas_call(
        _bias_kernel,
        in_specs=[pl.BlockSpec(memory_space=pltpu.SMEM),
                  pl.BlockSpec((3 * BLOCK, BLOCK), lambda: (0, 0))],
        out_specs=pl.BlockSpec(shape, lambda: (0, 0, 0)),
        out_shape=jax.ShapeDtypeStruct(shape, F32),
        name="bias_band",
    )(rel_bias, bucket_t)


def _t5_buckets(rel):
    nb = N_BUCKETS // 2
    max_exact = nb // 2
    base = jnp.where(rel > 0, nb, 0)
    n = jnp.abs(rel)
    log_ratio = jnp.log(jnp.maximum(n, 1).astype(jnp.float32) / max_exact) / math.log(MAX_DISTANCE / max_exact)
    large = jnp.minimum(max_exact + (log_ratio * (nb - max_exact)).astype(jnp.int32), nb - 1)
    return base + jnp.where(n < max_exact, n, large)


def _attn_kernel(sink_ref, q_ref, kp_ref, kc_ref, kn_ref, vp_ref, vc_ref, vn_ref, bias_ref, o_ref):
    n = pl.program_id(1)
    last = pl.num_programs(1) - 1
    kband = jnp.concatenate([kp_ref[...], kc_ref[...], kn_ref[...]], axis=0)
    vband = jnp.concatenate([vp_ref[...], vc_ref[...], vn_ref[...]], axis=0)
    vband_t = vband.astype(F32).T.astype(BF16)
    key = lax.broadcasted_iota(jnp.int32, (3 * BLOCK, 1), 0)
    first_head = lax.broadcasted_iota(jnp.int32, (1, 2 * BLOCK), 1) < BLOCK
    head = lambda t, i: t[:, i * ATTN_HEAD_DIM:(i + 1) * ATTN_HEAD_DIM]
    n_pairs = ATTN_HEADS // 2
    kv_of = lambda pr: (2 * pr) // ATTN_GROUP
    units = [(j, pr) for j in range(ATTN_QB) for pr in range(n_pairs)]
    band_rows = lambda j: slice(j * BLOCK, (j + 3) * BLOCK)

    scores = {}
    for j, pr in units:
        qj = q_ref[j * BLOCK:(j + 1) * BLOCK, :]
        q2 = jnp.concatenate([head(qj, 2 * pr), head(qj, 2 * pr + 1)], axis=0)
        scores[j, pr] = _dot_nt(head(kband, kv_of(pr))[band_rows(j)], q2)
    probs, dens = {}, {}
    for j, pr in units:
        s = scores[j, pr] + bias_ref[pr]
        if j == 0:
            s = jnp.where((key < BLOCK) & (n == 0), -1e30, s)
        if j == ATTN_QB - 1:
            s = jnp.where((key >= 2 * BLOCK) & (n == last), -1e30, s)
        sink = jnp.where(first_head, sink_ref[2 * pr], sink_ref[2 * pr + 1]) * LOG2E
        m = jnp.maximum(jnp.max(s, axis=0, keepdims=True), sink)
        p = jnp.exp2(s - m)
        dens[j, pr] = jnp.sum(p, axis=0, keepdims=True) + jnp.exp2(sink - m)
        probs[j, pr] = p.astype(BF16)
    outs_t = {}
    for j, pr in units:
        kv = kv_of(pr)
        v_t = vband_t[kv * ATTN_HEAD_DIM:(kv + 1) * ATTN_HEAD_DIM, band_rows(j)]
        outs_t[j, pr] = jnp.dot(v_t, probs[j, pr], preferred_element_type=F32) / dens[j, pr]
    for j in range(ATTN_QB):
        o_t = jnp.concatenate([outs_t[j, pr][:, half * BLOCK:(half + 1) * BLOCK]
                               for pr in range(n_pairs) for half in range(2)], axis=0)
        o_ref[j * BLOCK:(j + 1) * BLOCK, :] = o_t.T.astype(o_ref.dtype)


def _attention(q_a, k_a, v_a, band, sink, batch):
    n_tok = q_a.shape[0]
    nb = n_tok // batch // BLOCK
    rows = ATTN_QB * BLOCK
    steps = nb // ATTN_QB
    kv_spec = lambda r, f: pl.BlockSpec((r, ATTN_KV), f)
    prev = lambda bi, n: (bi * nb + jnp.maximum(n * ATTN_QB - 1, 0), 0)
    cur = lambda bi, n: (bi * steps + n, 0)
    nxt = lambda bi, n: (bi * nb + jnp.minimum((n + 1) * ATTN_QB, nb - 1), 0)
    return pl.pallas_call(
        _attn_kernel,
        grid=(batch, steps),
        in_specs=[pl.BlockSpec(memory_space=pltpu.SMEM),
                  pl.BlockSpec((rows, ATTN_Q), cur),
                  kv_spec(BLOCK, prev), kv_spec(rows, cur), kv_spec(BLOCK, nxt),
                  kv_spec(BLOCK, prev), kv_spec(rows, cur), kv_spec(BLOCK, nxt),
                  pl.BlockSpec(band.shape, lambda bi, n: (0, 0, 0))],
        out_specs=pl.BlockSpec((rows, ATTN_Q), cur),
        out_shape=jax.ShapeDtypeStruct((n_tok, ATTN_Q), BF16),
        compiler_params=pltpu.CompilerParams(dimension_semantics=("arbitrary", "arbitrary"),
                                             vmem_limit_bytes=VMEM_LIMIT),
        name="attn",
    )(sink, q_a, k_a, k_a, k_a, v_a, v_a, v_a, band)


def _gprep_kernel(yn_ref, ab_ref, gp_ref, w_ref, u_ref, qd_ref, kdt_ref, qk_ref, egl_ref):
    t_len = PREP_T

    ab = ab_ref[0]
    sp_in = ab + gp_ref[1:2, :]
    softplus = jnp.maximum(sp_in, 0.0) + jnp.log1p(jnp.exp(-jnp.abs(sp_in)))
    g = (-jnp.exp(gp_ref[0:1, :]) * LOG2E) * softplus
    beta = jax.nn.sigmoid(ab)

    r_t = lax.broadcasted_iota(jnp.int32, (t_len, t_len), 0)
    c_t = lax.broadcasted_iota(jnp.int32, (t_len, t_len), 1)
    same = (r_t // CHUNK) == (c_t // CHUNK)
    lower = jnp.where(same & (r_t >= c_t), 1.0, 0.0).astype(BF16)
    upper = jnp.where(same & (r_t <= c_t), 1.0, 0.0).astype(BF16)
    g_t = g.T[:2 * SUBLANES]
    cs_row = (_dot01_right(g_t, upper), _dot01_right(g_t, lower))
    pad_rows = jnp.zeros((LANES - 2 * SUBLANES, t_len), F32)
    cs_col = tuple(jnp.concatenate([r, pad_rows], axis=0).T for r in cs_row)

    qs = [yn_ref[h] for h in range(GDN_HEADS)]
    ks = [yn_ref[GDN_HEADS + h] for h in range(GDN_HEADS)]
    vs = [yn_ref[2 * GDN_HEADS + h] for h in range(GDN_HEADS)]
    kts = [kh.T for kh in ks]

    r_c = lax.broadcasted_iota(jnp.int32, (CHUNK, LANES), 0)
    lane = lax.broadcasted_iota(jnp.int32, (CHUNK, LANES), 1)
    is_fwd = lane < CHUNK
    c_c = lane % CHUNK
    eye = jnp.where(r_c == c_c, 1.0, 0.0).astype(F32)
    incl = (is_fwd & (r_c >= c_c)) | (~is_fwd & (r_c <= c_c))
    strict = (is_fwd & (r_c > c_c)) | (~is_fwd & (r_c < c_c))
    r_d = lax.broadcasted_iota(jnp.int32, (2 * CHUNK, LANES), 0)
    c_d = lax.broadcasted_iota(jnp.int32, (2 * CHUNK, LANES), 1)
    same_dir = (r_d // CHUNK) == (c_d // CHUNK)
    level_mask = lambda s_, r_, c_: ((r_ // (2 * s_)) == (c_ // (2 * s_))) & ((r_ // s_) != (c_ // s_))
    stack2 = lambda t: jnp.concatenate([t, t], axis=0)

    n_chunks = t_len // CHUNK
    rows = lambda c: slice(c * CHUNK, (c + 1) * CHUNK)
    pairs = [(c, h) for c in range(n_chunks) for h in range(GDN_HEADS)]
    qkk = {}
    for c, h in pairs:
        k16 = ks[h][rows(c)].astype(BF16)
        qk16 = jnp.concatenate([qs[h][rows(c)].astype(BF16), k16], axis=0)
        qkk[c, h] = _dot_nt(qk16, stack2(k16))

    bcast = lambda col: jnp.broadcast_to(col, (CHUNK, LANES))
    g_full, b_full, grow, glast, decay, a_mat, t_mat = {}, {}, {}, {}, {}, {}, {}
    for key in pairs:
        c, h = key
        for d in range(N_DIR):
            j = d * GDN_HEADS + h
            r_last = c * CHUNK + (CHUNK - 1 if d == 0 else 0)
            g_full[key, d] = bcast(cs_col[d][rows(c), j:j + 1])
            b_full[key, d] = bcast(beta[rows(c), SUBLANES + j:SUBLANES + j + 1])
            glast[key, d] = cs_col[d][r_last:r_last + 1, j:j + 1]
            grow[key, d] = cs_row[d][j:j + 1, rows(c)]
        gcol2 = jnp.where(is_fwd, g_full[key, 0], g_full[key, 1])
        bcol2 = jnp.where(is_fwd, b_full[key, 0], b_full[key, 1])
        grow2 = jnp.concatenate([grow[key, 0], grow[key, 1]], axis=1)
        decay[key] = jnp.exp2(jnp.where(incl, gcol2 - grow2, -jnp.inf))
        a_mat[key] = jnp.where(strict, bcol2 * qkk[key][CHUNK:] * decay[key], 0.0)
        t_mat[key] = eye - jnp.where(level_mask(1, r_c, c_c), a_mat[key], 0.0)

    same_dir16 = jnp.where(same_dir, 1.0, 0.0).astype(BF16)
    block_diag = lambda t16: stack2(t16) * same_dir16
    a_bd = {key: block_diag(a_mat[key].astype(BF16)) for key in pairs}
    s = 2
    while s < CHUNK:
        lvl = level_mask(s, r_c, c_c)
        t16 = {key: t_mat[key].astype(BF16) for key in pairs}
        x_mat = {key: jnp.dot(t16[key], a_bd[key], preferred_element_type=F32) for key in pairs}
        y_mat = {key: jnp.dot(x_mat[key].astype(BF16), block_diag(t16[key]), preferred_element_type=F32)
                 for key in pairs}
        t_mat = {key: t_mat[key] - jnp.where(lvl, y_mat[key], 0.0) for key in pairs}
        s *= 2

    wu = {}
    for key in pairs:
        c, h = key
        kv = jnp.concatenate(
            [jnp.concatenate([ks[h][rows(c)] * (b_full[key, d] * jnp.exp2(g_full[key, d])),
                              vs[h][rows(c)] * b_full[key, d]], axis=1) for d in range(N_DIR)], axis=0)
        t_sel = block_diag(t_mat[key].astype(BF16))
        wu[key] = jnp.dot(t_sel, kv.astype(BF16), preferred_element_type=F32)

    for c in range(n_chunks):
        egl_rows = []
        for d in range(N_DIR):
            for h in range(GDN_HEADS):
                key = (c, h)
                j = d * GDN_HEADS + h
                wu_d = wu[key][d * CHUNK:(d + 1) * CHUNK]
                w_ref[0, j, rows(c), :] = wu_d[:, :GDN_HEAD_DIM].astype(w_ref.dtype)
                u_ref[0, j, rows(c), :] = wu_d[:, GDN_HEAD_DIM:].astype(u_ref.dtype)
                qd_ref[0, j, rows(c), :] = (qs[h][rows(c)] * jnp.exp2(g_full[key, d])).astype(qd_ref.dtype)
                egl_rows.append(jnp.broadcast_to(jnp.exp2(glast[key, d]), (1, LANES)))
        egl_ref[0, c] = jnp.concatenate(egl_rows, axis=0)
        for h in range(GDN_HEADS):
            qk_ref[0, h, rows(c), :] = (qkk[c, h][:CHUNK] * decay[c, h]).astype(qk_ref.dtype)
    for cp in range(n_chunks // 2):
        for d in range(N_DIR):
            for h in range(GDN_HEADS):
                fac = jnp.concatenate([jnp.exp2(glast[(c, h), d] - grow[(c, h), d]) for c in (2 * cp, 2 * cp + 1)],
                                      axis=1)
                kdt = kts[h][:, 2 * cp * CHUNK:(2 * cp + 2) * CHUNK] * fac
                kdt_ref[0, d * GDN_HEADS + h, cp] = kdt.astype(kdt_ref.dtype)


def _gdn_prep(yn, ab, gate_par):
    b, s, _ = ab.shape
    nt = s // PREP_T
    cpb = PREP_T // CHUNK
    nc = s // CHUNK
    chain = lambda last: pl.BlockSpec((1, N_CHAIN, PREP_T, last), lambda bi, i: (bi, 0, i, 0))
    return pl.pallas_call(
        _gprep_kernel,
        grid=(b, nt),
        in_specs=[pl.BlockSpec((3 * GDN_W // LANES, PREP_T, LANES), lambda bi, i: (0, bi * nt + i, 0)),
                  pl.BlockSpec((1, PREP_T, LANES), lambda bi, i: (bi, i, 0)),
                  pl.BlockSpec((SUBLANES, LANES), lambda bi, i: (0, 0))],
        out_specs=[chain(GDN_HEAD_DIM), chain(GDN_HEAD_DIM), chain(GDN_HEAD_DIM),
                   pl.BlockSpec((1, N_CHAIN, cpb // 2, GDN_HEAD_DIM, 2 * CHUNK), lambda bi, i: (bi, 0, i, 0, 0)),
                   pl.BlockSpec((1, GDN_HEADS, PREP_T, LANES), lambda bi, i: (bi, 0, i, 0)),
                   pl.BlockSpec((1, cpb, N_CHAIN, LANES), lambda bi, i: (bi, i, 0, 0))],
        out_shape=[jax.ShapeDtypeStruct((b, N_CHAIN, s, GDN_HEAD_DIM), BF16),
                   jax.ShapeDtypeStruct((b, N_CHAIN, s, GDN_HEAD_DIM), BF16),
                   jax.ShapeDtypeStruct((b, N_CHAIN, s, GDN_HEAD_DIM), BF16),
                   jax.ShapeDtypeStruct((b, N_CHAIN, nc // 2, GDN_HEAD_DIM, 2 * CHUNK), BF16),
                   jax.ShapeDtypeStruct((b, GDN_HEADS, s, LANES), BF16),
                   jax.ShapeDtypeStruct((b, nc, N_CHAIN, LANES), F32)],
        compiler_params=pltpu.CompilerParams(dimension_semantics=("arbitrary", "arbitrary"),
                                             vmem_limit_bytes=VMEM_LIMIT),
        name="gdn_prep",
    )(yn, ab, gate_par)


def _gscan_kernel(egl_ref, wf_ref, wb_ref, uf_ref, ub_ref, qf_ref, qb_ref, kf_ref, kb_ref,
                  pf_ref, pb_ref, of_ref, ob_ref, state_ref):
    t = pl.program_id(0)
    n_steps = pl.num_programs(0)
    nc = n_steps * SCAN_CHUNKS
    n_batch = wf_ref.shape[0]

    @pl.when(t == 0)
    def _():
        state_ref[...] = jnp.zeros_like(state_ref)

    dirs = ((wf_ref, uf_ref, qf_ref, kf_ref, pf_ref, of_ref), (wb_ref, ub_ref, qb_ref, kb_ref, pb_ref, ob_ref))
    chains = [(bi, d, h) for bi in range(n_batch) for d in range(N_DIR) for h in range(GDN_HEADS)]
    slot = lambda bi, d, h: (bi * N_DIR + d) * GDN_HEADS + h
    st = {key: state_ref[slot(*key)] for key in chains}

    for sub in range(SCAN_CHUNKS):
        local = (sub, SCAN_CHUNKS - 1 - sub)
        rows = [slice(c * CHUNK, (c + 1) * CHUNK) for c in local]
        chunk = (t * SCAN_CHUNKS + sub, nc - 1 - (t * SCAN_CHUNKS + sub))
        r = {}
        for key in chains:
            bi, d, h = key
            w_ref, _, q_ref = dirs[d][:3]
            wq = jnp.concatenate([w_ref[bi, h, rows[d]], q_ref[bi, h, rows[d]]], axis=0)
            r[key] = _dot(wq, st[key])
        v_pad, intra = {}, {}
        for key in chains:
            bi, d, h = key
            u_ref, p_ref = dirs[d][1], dirs[d][4]
            v_new = (u_ref[bi, h, rows[d]].astype(F32) - r[key][:CHUNK]).astype(BF16)
            zeros = jnp.zeros_like(v_new)
            v_pad[key] = (jnp.concatenate([v_new, zeros], axis=0), jnp.concatenate([zeros, v_new], axis=0))
            intra[key] = jnp.dot(p_ref[bi, h, rows[d]], v_pad[key][d], preferred_element_type=F32)
        for key in chains:
            bi, d, h = key
            k_ref = dirs[d][3]
            egl = egl_ref[(bi * nc + chunk[d]) * N_CHAIN + d * GDN_HEADS + h]
            st[key] = st[key] * egl + jnp.dot(k_ref[bi, h, local[d] // 2], v_pad[key][local[d] % 2],
                                              preferred_element_type=F32)
        for bi in range(n_batch):
            for d in range(N_DIR):
                o_ref = dirs[d][5]
                o_ref[bi, rows[d], :] = jnp.concatenate(
                    [r[bi, d, h][CHUNK:] + intra[bi, d, h] for h in range(GDN_HEADS)], axis=-1).astype(o_ref.dtype)

    for key in chains:
        state_ref[slot(*key)] = st[key]


def _gdn_scan(egl, w, u, qd, kdt, qk):
    b, _, s, _ = w.shape
    rows = SCAN_CHUNKS * CHUNK
    n_steps = s // rows
    fwd_i = lambda t: t
    bwd_i = lambda t: n_steps - 1 - t
    chain = lambda d, at, last: pl.BlockSpec((b, GDN_HEADS, rows, last), lambda t: (0, d, at(t), 0))
    kspec = lambda d, at: pl.BlockSpec((b, GDN_HEADS, SCAN_CHUNKS // 2, GDN_HEAD_DIM, 2 * CHUNK),
                                       lambda t: (0, d, at(t), 0, 0))
    dk = GDN_HEAD_DIM
    return pl.pallas_call(
        _gscan_kernel,
        grid=(n_steps,),
        in_specs=[pl.BlockSpec(memory_space=pltpu.SMEM),
                  chain(0, fwd_i, dk), chain(1, bwd_i, dk), chain(0, fwd_i, dk), chain(1, bwd_i, dk),
                  chain(0, fwd_i, dk), chain(1, bwd_i, dk), kspec(0, fwd_i), kspec(1, bwd_i),
                  chain(0, fwd_i, LANES), chain(0, bwd_i, LANES)],
        out_specs=[pl.BlockSpec((b, rows, GDN_W), lambda t: (0, fwd_i(t), 0)),
                   pl.BlockSpec((b, rows, GDN_W), lambda t: (0, bwd_i(t), 0))],
        out_shape=[jax.ShapeDtypeStruct((b, s, GDN_W), BF16),
                   jax.ShapeDtypeStruct((b, s, GDN_W), BF16)],
        scratch_shapes=[pltpu.VMEM((b * N_CHAIN, GDN_HEAD_DIM, GDN_HEAD_DIM), F32)],
        compiler_params=pltpu.CompilerParams(dimension_semantics=("arbitrary",),
                                             vmem_limit_bytes=VMEM_LIMIT),
        name="gdn_scan",
    )(egl, w, w, u, u, qd, qd, kdt, kdt, qk, qk)


def _cast_weights_once(layer, copies):
    jobs, used = [], {}
    for src, dst, stage, sem, chunk_rows in copies:
        for r0 in range(0, src.shape[1], chunk_rows):
            slot = used.get(id(stage), 0) % 2
            used[id(stage)] = used.get(id(stage), 0) + 1
            rows = pl.ds(r0, chunk_rows)
            jobs.append((pltpu.make_async_copy(src.at[layer, rows], stage.at[slot], sem.at[slot]),
                         stage, slot, dst, rows))
    jobs[0][0].start()
    for k, (copy, stage, slot, dst, rows) in enumerate(jobs):
        if k + 1 < len(jobs):
            jobs[k + 1][0].start()
        copy.wait()
        dst[rows, :] = stage[slot].astype(BF16)


def _ffn_kernel(layer, x_ref, attn_ref, of_ref, ob_ref, z_ref, gnw_ref, wo_hbm, fnw_ref, w1_hbm, w2_hbm, onw_ref,
                o_ref, wo_ref, w1_ref, w2_ref, stage_wide, stage_tall, sem_wide, sem_tall):
    @pl.when(pl.program_id(0) == 0)
    def _():
        _cast_weights_once(layer, [(w1_hbm, w1_ref, stage_wide, sem_wide, FFN_CAST_ROWS_WIDE),
                                   (w2_hbm, w2_ref, stage_tall, sem_tall, FFN_CAST_ROWS_TALL),
                                   (wo_hbm, wo_ref, stage_tall, sem_tall, FFN_CAST_ROWS_TALL)])

    subs = [slice(r0, r0 + FFN_SUB) for r0 in range(0, FFN_TM, FFN_SUB)]
    rms = lambda t: t * lax.rsqrt(jnp.mean(t * t, axis=-1, keepdims=True) + EPS)
    hres, hn, act, acc = {}, {}, {}, {}
    for r in subs:
        o = of_ref[r, :].astype(F32) + ob_ref[r, :].astype(F32)
        heads = [rms(o[:, h * GDN_HEAD_DIM:(h + 1) * GDN_HEAD_DIM]) * gnw_ref[...] for h in range(GDN_HEADS)]
        z = z_ref[r, :]
        gdn = jnp.concatenate(heads, axis=-1) * (z * jax.nn.sigmoid(z))
        hres[r.start] = (x_ref[r, :] + jnp.dot(attn_ref[r, :], wo_ref[:ATTN_Q, :], preferred_element_type=F32)
                         + _dot(gdn, wo_ref[ATTN_Q:, :]))
    for r in subs:
        hn[r.start] = (rms(hres[r.start]) * fnw_ref[...]).astype(BF16)
        act[r.start] = jnp.dot(hn[r.start], w1_ref[...], preferred_element_type=F32)
    for r in subs:
        a = jnp.square(jnp.maximum(act[r.start], 0.0)).astype(BF16)
        acc[r.start] = hres[r.start] + jnp.dot(a, w2_ref[...], preferred_element_type=F32)
    for r in subs:
        o_ref[r, :] = rms(acc[r.start]) * onw_ref[...]


def _out_ffn(x2, attn, o_f, o_b, z, gnw, wo, fnw, w1, w2, onw, layer):
    n = x2.shape[0]
    assert wo.shape[1:] == (D_MODEL, D_MODEL) and w1.shape[1:] == (D_MODEL, D_FF) and w2.shape[1:] == (D_FF, D_MODEL)
    row = lambda w: pl.BlockSpec((FFN_TM, w), lambda i: (i, 0))
    full = lambda a: pl.BlockSpec(a.shape, lambda i: (0, 0))
    hbm = pl.BlockSpec(memory_space=pl.ANY)
    return pl.pallas_call(
        functools.partial(_ffn_kernel, layer),
        grid=(n // FFN_TM,),
        in_specs=[row(D_MODEL), row(ATTN_Q), row(GDN_W), row(GDN_W), row(GDN_W),
                  full(gnw), hbm, full(fnw), hbm, hbm, full(onw)],
        out_specs=row(D_MODEL),
        out_shape=jax.ShapeDtypeStruct((n, D_MODEL), F32),
        scratch_shapes=[pltpu.VMEM((D_MODEL, D_MODEL), BF16),
                        pltpu.VMEM((D_MODEL, D_FF), BF16),
                        pltpu.VMEM((D_FF, D_MODEL), BF16),
                        pltpu.VMEM((2, FFN_CAST_ROWS_WIDE, D_FF), F32),
                        pltpu.VMEM((2, FFN_CAST_ROWS_TALL, D_MODEL), F32),
                        pltpu.SemaphoreType.DMA((2,)),
                        pltpu.SemaphoreType.DMA((2,))],
        compiler_params=pltpu.CompilerParams(dimension_semantics=("arbitrary",),
                                             vmem_limit_bytes=VMEM_LIMIT),
        name="out_ffn",
    )(x2, attn, o_f, o_b, z, gnw, wo, fnw, w1, w2, onw)


def _layer(h, band, norm_mix_w, w_in, layer, attn_sink, conv_w, gdn_a_log, gdn_dt_bias, gdn_norm_w,
           w_out, norm_ffn_w, w_ffn_in, w_ffn_out, out_norm_w):
    b, s, _ = h.shape
    n = b * s
    x2 = h.reshape(n, D_MODEL)
    conv_pad = jnp.zeros((SUBLANES, 3 * GDN_W), F32).at[:CONV_K].set(conv_w)
    q_a, k_a, v_a, yn, z_g, ab = _proj(x2, norm_mix_w.reshape(1, D_MODEL), w_in, conv_pad, layer, s)

    attn = _attention(q_a, k_a, v_a, band, attn_sink, b)

    gate_par = jnp.zeros((SUBLANES, LANES), F32)
    gate_par = gate_par.at[0, :N_CHAIN].set(gdn_a_log.reshape(-1)).at[1, :N_CHAIN].set(gdn_dt_bias.reshape(-1))
    w_c, u_c, q_dec, k_dec_t, qk, egl = _gdn_prep(yn, ab.reshape(b, s, LANES), gate_par)
    o_f, o_b = _gdn_scan(egl[..., 0].reshape(-1), w_c, u_c, q_dec, k_dec_t, qk)

    out = _out_ffn(x2, attn, o_f.reshape(n, GDN_W), o_b.reshape(n, GDN_W), z_g,
                   gdn_norm_w.reshape(1, GDN_HEAD_DIM), w_out, norm_ffn_w.reshape(1, D_MODEL),
                   w_ffn_in, w_ffn_out, out_norm_w.reshape(1, D_MODEL), layer)
    return out.reshape(b, s, D_MODEL)


def kernel(x, norm_mix_w, w_in, rel_bias, attn_sink, conv_w, gdn_a_log, gdn_dt_bias, gdn_norm_w, w_out,
           norm_ffn_w, w_ffn_in, w_ffn_out, norm_final_w):
    depth = w_in.shape[0]
    assert depth == 1, "the fused output kernel applies the final norm after the single trunk layer"
    rel = (np.arange(3 * BLOCK)[None, :] - BLOCK) - np.arange(BLOCK)[:, None]
    bucket = _t5_buckets(jnp.asarray(rel, dtype=jnp.int32))
    band = _bias_band(rel_bias, bucket.T)
    return _layer(x, band, norm_mix_w[0], w_in, 0, attn_sink[0], conv_w[0], gdn_a_log[0], gdn_dt_bias[0],
                  gdn_norm_w[0], w_out, norm_ffn_w[0], w_ffn_in, w_ffn_out, norm_final_w)
```

```python
import functools
import math

import jax
import jax.numpy as jnp
import numpy as np
from jax import lax
from jax.experimental import pallas as pl
from jax.experimental.pallas import tpu as pltpu

F32 = jnp.float32
BF16 = jnp.bfloat16

D_MODEL = 1024
ATTN_HEADS = 8
ATTN_KV_HEADS = 2
ATTN_HEAD_DIM = 64
ATTN_GROUP = ATTN_HEADS // ATTN_KV_HEADS
WINDOW = 128
BLOCK = 128
N_BUCKETS = 32
MAX_DISTANCE = 128
GDN_HEADS = 4
GDN_HEAD_DIM = 128
CONV_K = 5
CHUNK = 64
N_DIR = 2
N_CHAIN = N_DIR * GDN_HEADS
D_FF = 4 * D_MODEL
EPS = 1e-6
LOG2E = math.log2(math.e)
ATTN_Q = ATTN_HEADS * ATTN_HEAD_DIM
ATTN_KV = ATTN_KV_HEADS * ATTN_HEAD_DIM
GDN_W = GDN_HEADS * GDN_HEAD_DIM
LANES = 128
SUBLANES = 8
VMEM_LIMIT = 56 * 1024 * 1024

PROJ_TM = 1024
PROJ_SUB = 256
PROJ_CAST_ROWS = 128
ATTN_QB = 8
PREP_T = 1024
ROW_STRIDE = 4
SCAN_CHUNKS = 8
FFN_TM = 512
FFN_SUB = 256
FFN_CAST_ROWS_WIDE = 128
FFN_CAST_ROWS_TALL = 512


def _dot(a, b):
    return jnp.dot(a.astype(BF16), b.astype(BF16), preferred_element_type=F32)


def _dot_nt(a, b):
    return lax.dot_general(a.astype(BF16), b.astype(BF16), (((1,), (1,)), ((), ())),
                           preferred_element_type=F32)


def _split3(x):
    hi = x.astype(BF16)
    r1 = x - hi.astype(F32)
    mid = r1.astype(BF16)
    lo = (r1 - mid.astype(F32)).astype(BF16)
    return hi, mid, lo


def _dot01_right(x, m01):
    hi, mid, lo = _split3(x)
    d = lambda p: jnp.dot(p, m01, preferred_element_type=F32)
    return d(hi) + d(mid) + d(lo)


def _conv_silu_norm(xe_ref, cw_ref, yn_ref, r0, n_out):
    halo = CONV_K // 2
    n_rows = n_out // ROW_STRIDE
    n_slab = 3 * GDN_W // LANES
    units = [(sb, ph) for sb in range(n_slab) for ph in range(ROW_STRIDE)]
    half_taps = [cw_ref[j:j + 1, :] * 0.5 for j in range(CONV_K)]
    yv = {}
    for sb, ph in units:
        lanes = slice(sb * LANES, (sb + 1) * LANES)
        acc = None
        for j in range(CONV_K):
            win = xe_ref[sb, pl.ds(SUBLANES - halo + j + ph, n_rows, stride=ROW_STRIDE), :]
            term = half_taps[j][:, lanes] * win
            acc = term if acc is None else acc + term
        yv[sb, ph] = acc
    for key in units:
        yv[key] = yv[key] + yv[key] * jnp.tanh(yv[key])
    for sb, ph in units:
        if sb < 2 * GDN_HEADS:
            scale = lax.rsqrt(jnp.sum(yv[sb, ph] * yv[sb, ph], axis=-1, keepdims=True) + EPS)
            if sb < GDN_HEADS:
                scale = scale * (GDN_HEAD_DIM ** -0.5)
            yv[sb, ph] = yv[sb, ph] * scale
    for sb, ph in units:
        yn_ref[sb, pl.ds(r0 + ph, n_rows, stride=ROW_STRIDE), :] = yv[sb, ph]


def _proj_kernel(layer, tiles_per_seq, x_ref, xp_ref, xn_ref, nw_ref, wt_hbm, cw_ref,
                 qa_ref, ka_ref, va_ref, yn_ref, z_ref, ab_ref, wb_ref, stage_ref, sem, *xe_refs):
    d_in = wt_hbm.shape[1]
    d_main = d_in // LANES * LANES
    step = pl.program_id(0)

    @pl.when(step == 0)
    def _():
        starts = list(range(0, d_main, PROJ_CAST_ROWS)) + [d_main]
        sizes = [PROJ_CAST_ROWS] * (len(starts) - 1) + [d_in - d_main]
        copies = [pltpu.make_async_copy(wt_hbm.at[layer, pl.ds(c0, nr)], stage_ref.at[k % 2, pl.ds(0, nr)],
                                        sem.at[k % 2]) for k, (c0, nr) in enumerate(zip(starts, sizes))]
        lane = lax.broadcasted_iota(jnp.int32, (D_MODEL, PROJ_CAST_ROWS), 1)
        copies[0].start()
        for k, (c0, nr) in enumerate(zip(starts, sizes)):
            if k + 1 < len(copies):
                copies[k + 1].start()
            copies[k].wait()
            cols = stage_ref[k % 2].T
            if nr < PROJ_CAST_ROWS:
                cols = jnp.where(lane < nr, cols, 0.0)
            wb_ref[:, c0:c0 + PROJ_CAST_ROWS] = cols.astype(BF16)

    o_q, o_k, o_v, o_g, o_z, o_ab = 0, ATTN_Q, ATTN_Q + ATTN_KV, ATTN_Q + 2 * ATTN_KV, \
        ATTN_Q + 2 * ATTN_KV + 3 * GDN_W, ATTN_Q + 2 * ATTN_KV + 4 * GDN_W
    n_slab = 3 * GDN_W // LANES

    def normed(x):
        ms = jnp.mean(x * x, axis=-1, keepdims=True)
        return (x * lax.rsqrt(ms + EPS) * nw_ref[...]).astype(BF16)

    n_sub = PROJ_TM // PROJ_SUB

    def to_slabs(xe_ref, y, row0):
        for sb in range(n_slab):
            xe_ref[sb, row0:row0 + y.shape[0], :] = y[:, sb * LANES:(sb + 1) * LANES]

    def put(k, y):
        to_slabs(xe_refs[k], y, SUBLANES)
        if k > 0:
            to_slabs(xe_refs[k - 1], y[:SUBLANES], SUBLANES + PROJ_SUB)
        if k + 1 < n_sub:
            to_slabs(xe_refs[k + 1], y[PROJ_SUB - SUBLANES:], 0)

    i = step % tiles_per_seq
    outs = ((qa_ref, o_q, ATTN_Q, ATTN_HEAD_DIM ** -0.5 * LOG2E), (ka_ref, o_k, ATTN_KV, None), (va_ref, o_v, ATTN_KV, None),
            (z_ref, o_z, GDN_W, None), (ab_ref, o_ab, LANES, None))
    for k in range(n_sub):
        rows = slice(k * PROJ_SUB, (k + 1) * PROJ_SUB)
        if k == 0:
            xg = normed(jnp.concatenate([xp_ref[...], xn_ref[...], x_ref[rows, :]], axis=0))
            yg = jnp.dot(xg, wb_ref[:, o_g:o_z], preferred_element_type=F32)
            to_slabs(xe_refs[0], jnp.where(i > 0, yg[:SUBLANES], 0.0), 0)
            to_slabs(xe_refs[n_sub - 1], jnp.where(i < tiles_per_seq - 1, yg[SUBLANES:2 * SUBLANES], 0.0),
                     SUBLANES + PROJ_SUB)
            put(0, yg[2 * SUBLANES:])
            xn = xg[2 * SUBLANES:]
        else:
            xn = normed(x_ref[rows, :])
            put(k, jnp.dot(xn, wb_ref[:, o_g:o_z], preferred_element_type=F32))
            _conv_silu_norm(xe_refs[k - 1], cw_ref, yn_ref, (k - 1) * PROJ_SUB, PROJ_SUB)
        for ref, c0, width, scale in outs:
            y = jnp.dot(xn, wb_ref[:, c0:c0 + width], preferred_element_type=F32)
            if scale is not None:
                y = y * scale
            ref[rows, :] = y.astype(ref.dtype)
    _conv_silu_norm(xe_refs[n_sub - 1], cw_ref, yn_ref, PROJ_TM - PROJ_SUB, PROJ_SUB)


def _proj(x2, norm_w, w_in, conv_w, layer, seq_len):
    n = x2.shape[0]
    d_in = w_in.shape[2]
    d_pad = d_in // LANES * LANES + LANES
    assert d_pad == ATTN_Q + 2 * ATTN_KV + 4 * GDN_W + LANES and d_in - (d_pad - LANES) == 2 * N_CHAIN
    assert seq_len % PROJ_TM == 0
    w_t = jnp.swapaxes(w_in, 1, 2)
    nh8 = PROJ_TM // SUBLANES
    row = lambda w: pl.BlockSpec((PROJ_TM, w), lambda i: (i, 0))
    return pl.pallas_call(
        functools.partial(_proj_kernel, layer, seq_len // PROJ_TM),
        grid=(n // PROJ_TM,),
        in_specs=[row(D_MODEL),
                  pl.BlockSpec((SUBLANES, D_MODEL), lambda i: (jnp.maximum(i * nh8 - 1, 0), 0)),
                  pl.BlockSpec((SUBLANES, D_MODEL), lambda i: (jnp.minimum((i + 1) * nh8, n // SUBLANES - 1), 0)),
                  pl.BlockSpec((1, D_MODEL), lambda i: (0, 0)),
                  pl.BlockSpec(memory_space=pl.ANY),
                  pl.BlockSpec((SUBLANES, 3 * GDN_W), lambda i: (0, 0))],
        out_specs=[row(ATTN_Q), row(ATTN_KV), row(ATTN_KV),
                   pl.BlockSpec((3 * GDN_W // LANES, PROJ_TM, LANES), lambda i: (0, i, 0)),
                   row(GDN_W), row(LANES)],
        out_shape=[jax.ShapeDtypeStruct((n, ATTN_Q), BF16),
                   jax.ShapeDtypeStruct((n, ATTN_KV), BF16),
                   jax.ShapeDtypeStruct((n, ATTN_KV), BF16),
                   jax.ShapeDtypeStruct((3 * GDN_W // LANES, n, LANES), F32),
                   jax.ShapeDtypeStruct((n, GDN_W), F32),
                   jax.ShapeDtypeStruct((n, LANES), F32)],
        scratch_shapes=[pltpu.VMEM((D_MODEL, d_pad), BF16),
                        pltpu.VMEM((2, PROJ_CAST_ROWS, D_MODEL), F32),
                        pltpu.SemaphoreType.DMA((2,))]
        + [pltpu.VMEM((3 * GDN_W // LANES, PROJ_SUB + 2 * SUBLANES, LANES), F32)] * (PROJ_TM // PROJ_SUB),
        compiler_params=pltpu.CompilerParams(dimension_semantics=("arbitrary",),
                                             vmem_limit_bytes=VMEM_LIMIT),
        name="proj",
    )(x2, x2, x2, norm_w, w_t, conv_w)


def _bias_kernel(relb_ref, bucket_ref, o_ref):
    bucket = bucket_ref[...]
    key = lax.broadcasted_iota(jnp.int32, (3 * BLOCK, BLOCK), 0)
    qry = lax.broadcasted_iota(jnp.int32, (3 * BLOCK, BLOCK), 1)
    in_window = jnp.abs(key - BLOCK - qry) <= WINDOW
    for h in range(ATTN_HEADS):
        acc = jnp.zeros((3 * BLOCK, BLOCK), F32)
        for b in range(N_BUCKETS):
            acc = jnp.where(bucket == b, relb_ref[b, h], acc)
        o_ref[h // 2, :, (h % 2) * BLOCK:(h % 2 + 1) * BLOCK] = jnp.where(in_window, acc * LOG2E, -1e30)


def _bias_band(rel_bias, bucket_t):
    shape = (ATTN_HEADS // 2, 3 * BLOCK, 2 * BLOCK)
    return pl.pallas_call(
        _bias_kernel,
        in_specs=[pl.BlockSpec(memory_space=pltpu.SMEM),
                  pl.BlockSpec((3 * BLOCK, BLOCK), lambda: (0, 0))],
        out_specs=pl.BlockSpec(shape, lambda: (0, 0, 0)),
        out_shape=jax.ShapeDtypeStruct(shape, F32),
        name="bias_band",
    )(rel_bias, bucket_t)


def _t5_buckets(rel):
    nb = N_BUCKETS // 2
    max_exact = nb // 2
    base = jnp.where(rel > 0, nb, 0)
    n = jnp.abs(rel)
    log_ratio = jnp.log(jnp.maximum(n, 1).astype(jnp.float32) / max_exact) / math.log(MAX_DISTANCE / max_exact)
    large = jnp.minimum(max_exact + (log_ratio * (nb - max_exact)).astype(jnp.int32), nb - 1)
    return base + jnp.where(n < max_exact, n, large)


def _attn_kernel(sink_ref, q_ref, kp_ref, kc_ref, kn_ref, vp_ref, vc_ref, vn_ref, bias_ref, o_ref):
    n = pl.program_id(1)
    last = pl.num_programs(1) - 1
    kband = jnp.concatenate([kp_ref[...], kc_ref[...], kn_ref[...]], axis=0)
    vband = jnp.concatenate([vp_ref[...], vc_ref[...], vn_ref[...]], axis=0)
    vband_t = vband.astype(F32).T.astype(BF16)
    key = lax.broadcasted_iota(jnp.int32, (3 * BLOCK, 1), 0)
    first_head = lax.broadcasted_iota(jnp.int32, (1, 2 * BLOCK), 1) < BLOCK
    head = lambda t, i: t[:, i * ATTN_HEAD_DIM:(i + 1) * ATTN_HEAD_DIM]
    n_pairs = ATTN_HEADS // 2
    kv_of = lambda pr: (2 * pr) // ATTN_GROUP
    units = [(j, pr) for j in range(ATTN_QB) for pr in range(n_pairs)]
    band_rows = lambda j: slice(j * BLOCK, (j + 3) * BLOCK)

    scores = {}
    for j, pr in units:
        qj = q_ref[j * BLOCK:(j + 1) * BLOCK, :]
        q2 = jnp.concatenate([head(qj, 2 * pr), head(qj, 2 * pr + 1)], axis=0)
        scores[j, pr] = _dot_nt(head(kband, kv_of(pr))[band_rows(j)], q2)
    probs, dens = {}, {}
    for j, pr in units:
        s = scores[j, pr] + bias_ref[pr]
        if j == 0:
            s = jnp.where((key < BLOCK) & (n == 0), -1e30, s)
        if j == ATTN_QB - 1:
            s = jnp.where((key >= 2 * BLOCK) & (n == last), -1e30, s)
        sink = jnp.where(first_head, sink_ref[2 * pr], sink_ref[2 * pr + 1]) * LOG2E
        m = jnp.maximum(jnp.max(s, axis=0, keepdims=True), sink)
        p = jnp.exp2(s - m)
        dens[j, pr] = jnp.sum(p, axis=0, keepdims=True) + jnp.exp2(sink - m)
        probs[j, pr] = p.astype(BF16)
    outs_t = {}
    for j, pr in units:
        kv = kv_of(pr)
        v_t = vband_t[kv * ATTN_HEAD_DIM:(kv + 1) * ATTN_HEAD_DIM, band_rows(j)]
        outs_t[j, pr] = jnp.dot(v_t, probs[j, pr], preferred_element_type=F32) / dens[j, pr]
    for j in range(ATTN_QB):
        o_t = jnp.concatenate([outs_t[j, pr][:, half * BLOCK:(half + 1) * BLOCK]
                               for pr in range(n_pairs) for half in range(2)], axis=0)
        o_ref[j * BLOCK:(j + 1) * BLOCK, :] = o_t.T.astype(o_ref.dtype)


def _attention(q_a, k_a, v_a, band, sink, batch):
    n_tok = q_a.shape[0]
    nb = n_tok // batch // BLOCK
    rows = ATTN_QB * BLOCK
    steps = nb // ATTN_QB
    kv_spec = lambda r, f: pl.BlockSpec((r, ATTN_KV), f)
    prev = lambda bi, n: (bi * nb + jnp.maximum(n * ATTN_QB - 1, 0), 0)
    cur = lambda bi, n: (bi * steps + n, 0)
    nxt = lambda bi, n: (bi * nb + jnp.minimum((n + 1) * ATTN_QB, nb - 1), 0)
    return pl.pallas_call(
        _attn_kernel,
        grid=(batch, steps),
        in_specs=[pl.BlockSpec(memory_space=pltpu.SMEM),
                  pl.BlockSpec((rows, ATTN_Q), cur),
                  kv_spec(BLOCK, prev), kv_spec(rows, cur), kv_spec(BLOCK, nxt),
                  kv_spec(BLOCK, prev), kv_spec(rows, cur), kv_spec(BLOCK, nxt),
                  pl.BlockSpec(band.shape, lambda bi, n: (0, 0, 0))],
        out_specs=pl.BlockSpec((rows, ATTN_Q), cur),
        out_shape=jax.ShapeDtypeStruct((n_tok, ATTN_Q), BF16),
        compiler_params=pltpu.CompilerParams(dimension_semantics=("arbitrary", "arbitrary"),
                                             vmem_limit_bytes=VMEM_LIMIT),
        name="attn",
    )(sink, q_a, k_a, k_a, k_a, v_a, v_a, v_a, band)


def _gprep_kernel(yn_ref, ab_ref, gp_ref, w_ref, u_ref, qd_ref, kdt_ref, qk_ref, egl_ref):
    t_len = PREP_T

    ab = ab_ref[0]
    sp_in = ab + gp_ref[1:2, :]
    softplus = jnp.maximum(sp_in, 0.0) + jnp.log1p(jnp.exp(-jnp.abs(sp_in)))
    g = (-jnp.exp(gp_ref[0:1, :]) * LOG2E) * softplus
    beta = jax.nn.sigmoid(ab)

    r_t = lax.broadcasted_iota(jnp.int32, (t_len, t_len), 0)
    c_t = lax.broadcasted_iota(jnp.int32, (t_len, t_len), 1)
    same = (r_t // CHUNK) == (c_t // CHUNK)
    lower = jnp.where(same & (r_t >= c_t), 1.0, 0.0).astype(BF16)
    upper = jnp.where(same & (r_t <= c_t), 1.0, 0.0).astype(BF16)
    g_t = g.T[:2 * SUBLANES]
    cs_row = (_dot01_right(g_t, upper), _dot01_right(g_t, lower))
    pad_rows = jnp.zeros((LANES - 2 * SUBLANES, t_len), F32)
    cs_col = tuple(jnp.concatenate([r, pad_rows], axis=0).T for r in cs_row)
    beta_t = beta.T[:3 * SUBLANES]

    qs = [yn_ref[h] for h in range(GDN_HEADS)]
    ks = [yn_ref[GDN_HEADS + h] for h in range(GDN_HEADS)]
    vs = [yn_ref[2 * GDN_HEADS + h] for h in range(GDN_HEADS)]
    kts = [kh.T for kh in ks]

    r_c = lax.broadcasted_iota(jnp.int32, (CHUNK, LANES), 0)
    lane = lax.broadcasted_iota(jnp.int32, (CHUNK, LANES), 1)
    is_fwd = lane < CHUNK
    c_c = lane % CHUNK
    eye = jnp.where(r_c == c_c, 1.0, 0.0).astype(F32)
    incl = (is_fwd & (r_c >= c_c)) | (~is_fwd & (r_c <= c_c))
    strict = (is_fwd & (r_c > c_c)) | (~is_fwd & (r_c < c_c))
    r_d = lax.broadcasted_iota(jnp.int32, (2 * CHUNK, LANES), 0)
    c_d = lax.broadcasted_iota(jnp.int32, (2 * CHUNK, LANES), 1)
    same_dir = (r_d // CHUNK) == (c_d // CHUNK)
    level_mask = lambda s_, r_, c_: ((r_ // (2 * s_)) == (c_ // (2 * s_))) & ((r_ // s_) != (c_ // s_))
    stack2 = lambda t: jnp.concatenate([t, t], axis=0)

    n_chunks = t_len // CHUNK
    rows = lambda c: slice(c * CHUNK, (c + 1) * CHUNK)
    pairs = [(c, h) for c in range(n_chunks) for h in range(GDN_HEADS)]
    qkk = {}
    for c, h in pairs:
        k16 = ks[h][rows(c)].astype(BF16)
        qk16 = jnp.concatenate([qs[h][rows(c)].astype(BF16), k16], axis=0)
        qkk[c, h] = _dot_nt(qk16, stack2(k16))

    bcast = lambda col: jnp.broadcast_to(col, (CHUNK, LANES))
    g_full, brow, grow, glast, decay, a_mat, t_mat = {}, {}, {}, {}, {}, {}, {}
    for key in pairs:
        c, h = key
        for d in range(N_DIR):
            j = d * GDN_HEADS + h
            r_last = c * CHUNK + (CHUNK - 1 if d == 0 else 0)
            g_full[key, d] = bcast(cs_col[d][rows(c), j:j + 1])
            brow[key, d] = beta_t[SUBLANES + j:SUBLANES + j + 1, rows(c)]
            glast[key, d] = cs_col[d][r_last:r_last + 1, j:j + 1]
            grow[key, d] = cs_row[d][j:j + 1, rows(c)]
        gcol2 = jnp.where(is_fwd, g_full[key, 0], g_full[key, 1])
        grow2 = jnp.concatenate([grow[key, 0], grow[key, 1]], axis=1)
        brow[key] = jnp.concatenate([brow[key, 0], brow[key, 1]], axis=1)
        decay[key] = jnp.exp2(jnp.where(incl, gcol2 - grow2, -jnp.inf))
        a_mat[key] = jnp.where(strict, qkk[key][CHUNK:] * decay[key], 0.0) * brow[key]
        t_mat[key] = eye - jnp.where(level_mask(1, r_c, c_c), a_mat[key], 0.0)

    same_dir16 = jnp.where(same_dir, 1.0, 0.0).astype(BF16)
    block_diag = lambda t16: stack2(t16) * same_dir16
    a_bd = {key: block_diag(a_mat[key].astype(BF16)) for key in pairs}
    s = 2
    while s < CHUNK:
        lvl = level_mask(s, r_c, c_c)
        t16 = {key: t_mat[key].astype(BF16) for key in pairs}
        x_mat = {key: jnp.dot(t16[key], a_bd[key], preferred_element_type=F32) for key in pairs}
        y_mat = {key: jnp.dot(x_mat[key].astype(BF16), block_diag(t16[key]), preferred_element_type=F32)
                 for key in pairs}
        t_mat = {key: t_mat[key] - jnp.where(lvl, y_mat[key], 0.0) for key in pairs}
        s *= 2

    wu = {}
    for key in pairs:
        c, h = key
        kv = jnp.concatenate(
            [jnp.concatenate([ks[h][rows(c)] * jnp.exp2(g_full[key, d]), vs[h][rows(c)]], axis=1)
             for d in range(N_DIR)], axis=0)
        t_sel = block_diag(t_mat[key].astype(BF16))
        wu[key] = jnp.dot(t_sel, kv.astype(BF16), preferred_element_type=F32)

    for c in range(n_chunks):
        egl_rows = []
        for d in range(N_DIR):
            for h in range(GDN_HEADS):
                key = (c, h)
                j = d * GDN_HEADS + h
                wu_d = wu[key][d * CHUNK:(d + 1) * CHUNK]
                w_ref[0, j, rows(c), :] = wu_d[:, :GDN_HEAD_DIM].astype(w_ref.dtype)
                u_ref[0, j, rows(c), :] = wu_d[:, GDN_HEAD_DIM:].astype(u_ref.dtype)
                qd_ref[0, j, rows(c), :] = (qs[h][rows(c)] * jnp.exp2(g_full[key, d])).astype(qd_ref.dtype)
                egl_rows.append(jnp.broadcast_to(jnp.exp2(glast[key, d]), (1, LANES)))
        egl_ref[0, c] = jnp.concatenate(egl_rows, axis=0)
        for h in range(GDN_HEADS):
            qk_ref[0, h, rows(c), :] = (qkk[c, h][:CHUNK] * decay[c, h] * brow[c, h]).astype(qk_ref.dtype)
    for cp in range(n_chunks // 2):
        for d in range(N_DIR):
            for h in range(GDN_HEADS):
                fac = jnp.concatenate([brow[(c, h), d] * jnp.exp2(glast[(c, h), d] - grow[(c, h), d])
                                       for c in (2 * cp, 2 * cp + 1)], axis=1)
                kdt = kts[h][:, 2 * cp * CHUNK:(2 * cp + 2) * CHUNK] * fac
                kdt_ref[0, d * GDN_HEADS + h, cp] = kdt.astype(kdt_ref.dtype)


def _gdn_prep(yn, ab, gate_par):
    b, s, _ = ab.shape
    nt = s // PREP_T
    cpb = PREP_T // CHUNK
    nc = s // CHUNK
    chain = lambda last: pl.BlockSpec((1, N_CHAIN, PREP_T, last), lambda bi, i: (bi, 0, i, 0))
    return pl.pallas_call(
        _gprep_kernel,
        grid=(b, nt),
        in_specs=[pl.BlockSpec((3 * GDN_W // LANES, PREP_T, LANES), lambda bi, i: (0, bi * nt + i, 0)),
                  pl.BlockSpec((1, PREP_T, LANES), lambda bi, i: (bi, i, 0)),
                  pl.BlockSpec((SUBLANES, LANES), lambda bi, i: (0, 0))],
        out_specs=[chain(GDN_HEAD_DIM), chain(GDN_HEAD_DIM), chain(GDN_HEAD_DIM),
                   pl.BlockSpec((1, N_CHAIN, cpb // 2, GDN_HEAD_DIM, 2 * CHUNK), lambda bi, i: (bi, 0, i, 0, 0)),
                   pl.BlockSpec((1, GDN_HEADS, PREP_T, LANES), lambda bi, i: (bi, 0, i, 0)),
                   pl.BlockSpec((1, cpb, N_CHAIN, LANES), lambda bi, i: (bi, i, 0, 0))],
        out_shape=[jax.ShapeDtypeStruct((b, N_CHAIN, s, GDN_HEAD_DIM), BF16),
                   jax.ShapeDtypeStruct((b, N_CHAIN, s, GDN_HEAD_DIM), BF16),
                   jax.ShapeDtypeStruct((b, N_CHAIN, s, GDN_HEAD_DIM), BF16),
                   jax.ShapeDtypeStruct((b, N_CHAIN, nc // 2, GDN_HEAD_DIM, 2 * CHUNK), BF16),
                   jax.ShapeDtypeStruct((b, GDN_HEADS, s, LANES), BF16),
                   jax.ShapeDtypeStruct((b, nc, N_CHAIN, LANES), F32)],
        compiler_params=pltpu.CompilerParams(dimension_semantics=("arbitrary", "arbitrary"),
                                             vmem_limit_bytes=VMEM_LIMIT),
        name="gdn_prep",
    )(yn, ab, gate_par)


def _gscan_kernel(egl_ref, wf_ref, wb_ref, uf_ref, ub_ref, qf_ref, qb_ref, kf_ref, kb_ref,
                  pf_ref, pb_ref, of_ref, ob_ref, state_ref):
    t = pl.program_id(0)
    n_steps = pl.num_programs(0)
    nc = n_steps * SCAN_CHUNKS
    n_batch = wf_ref.shape[0]

    @pl.when(t == 0)
    def _():
        state_ref[...] = jnp.zeros_like(state_ref)

    dirs = ((wf_ref, uf_ref, qf_ref, kf_ref, pf_ref, of_ref), (wb_ref, ub_ref, qb_ref, kb_ref, pb_ref, ob_ref))
    chains = [(bi, d, h) for bi in range(n_batch) for d in range(N_DIR) for h in range(GDN_HEADS)]
    slot = lambda bi, d, h: (bi * N_DIR + d) * GDN_HEADS + h
    st = {key: state_ref[slot(*key)] for key in chains}

    for sub in range(SCAN_CHUNKS):
        local = (sub, SCAN_CHUNKS - 1 - sub)
        rows = [slice(c * CHUNK, (c + 1) * CHUNK) for c in local]
        chunk = (t * SCAN_CHUNKS + sub, nc - 1 - (t * SCAN_CHUNKS + sub))
        r = {}
        for key in chains:
            bi, d, h = key
            w_ref, _, q_ref = dirs[d][:3]
            wq = jnp.concatenate([w_ref[bi, h, rows[d]], q_ref[bi, h, rows[d]]], axis=0)
            r[key] = _dot(wq, st[key])
        v_pad, intra = {}, {}
        for key in chains:
            bi, d, h = key
            u_ref, p_ref = dirs[d][1], dirs[d][4]
            v_new = (u_ref[bi, h, rows[d]].astype(F32) - r[key][:CHUNK]).astype(BF16)
            zeros = jnp.zeros_like(v_new)
            v_pad[key] = (jnp.concatenate([v_new, zeros], axis=0), jnp.concatenate([zeros, v_new], axis=0))
            intra[key] = jnp.dot(p_ref[bi, h, rows[d]], v_pad[key][d], preferred_element_type=F32)
        for key in chains:
            bi, d, h = key
            k_ref = dirs[d][3]
            egl = egl_ref[(bi * nc + chunk[d]) * N_CHAIN + d * GDN_HEADS + h]
            st[key] = st[key] * egl + jnp.dot(k_ref[bi, h, local[d] // 2], v_pad[key][local[d] % 2],
                                              preferred_element_type=F32)
        for bi in range(n_batch):
            for d in range(N_DIR):
                o_ref = dirs[d][5]
                o_ref[bi, rows[d], :] = jnp.concatenate(
                    [r[bi, d, h][CHUNK:] + intra[bi, d, h] for h in range(GDN_HEADS)], axis=-1).astype(o_ref.dtype)

    for key in chains:
        state_ref[slot(*key)] = st[key]


def _gdn_scan(egl, w, u, qd, kdt, qk):
    b, _, s, _ = w.shape
    rows = SCAN_CHUNKS * CHUNK
    n_steps = s // rows
    fwd_i = lambda t: t
    bwd_i = lambda t: n_steps - 1 - t
    chain = lambda d, at, last: pl.BlockSpec((b, GDN_HEADS, rows, last), lambda t: (0, d, at(t), 0))
    kspec = lambda d, at: pl.BlockSpec((b, GDN_HEADS, SCAN_CHUNKS // 2, GDN_HEAD_DIM, 2 * CHUNK),
                                       lambda t: (0, d, at(t), 0, 0))
    dk = GDN_HEAD_DIM
    return pl.pallas_call(
        _gscan_kernel,
        grid=(n_steps,),
        in_specs=[pl.BlockSpec(memory_space=pltpu.SMEM),
                  chain(0, fwd_i, dk), chain(1, bwd_i, dk), chain(0, fwd_i, dk), chain(1, bwd_i, dk),
                  chain(0, fwd_i, dk), chain(1, bwd_i, dk), kspec(0, fwd_i), kspec(1, bwd_i),
                  chain(0, fwd_i, LANES), chain(0, bwd_i, LANES)],
        out_specs=[pl.BlockSpec((b, rows, GDN_W), lambda t: (0, fwd_i(t), 0)),
                   pl.BlockSpec((b, rows, GDN_W), lambda t: (0, bwd_i(t), 0))],
        out_shape=[jax.ShapeDtypeStruct((b, s, GDN_W), BF16),
                   jax.ShapeDtypeStruct((b, s, GDN_W), BF16)],
        scratch_shapes=[pltpu.VMEM((b * N_CHAIN, GDN_HEAD_DIM, GDN_HEAD_DIM), F32)],
        compiler_params=pltpu.CompilerParams(dimension_semantics=("arbitrary",),
                                             vmem_limit_bytes=VMEM_LIMIT),
        name="gdn_scan",
    )(egl, w, w, u, u, qd, qd, kdt, kdt, qk, qk)


def _cast_weights_once(layer, copies):
    jobs, used = [], {}
    for src, dst, stage, sem, chunk_rows in copies:
        for r0 in range(0, src.shape[1], chunk_rows):
            slot = used.get(id(stage), 0) % 2
            used[id(stage)] = used.get(id(stage), 0) + 1
            rows = pl.ds(r0, chunk_rows)
            jobs.append((pltpu.make_async_copy(src.at[layer, rows], stage.at[slot], sem.at[slot]),
                         stage, slot, dst, rows))
    jobs[0][0].start()
    for k, (copy, stage, slot, dst, rows) in enumerate(jobs):
        if k + 1 < len(jobs):
            jobs[k + 1][0].start()
        copy.wait()
        dst[rows, :] = stage[slot].astype(BF16)


def _ffn_kernel(layer, x_ref, attn_ref, of_ref, ob_ref, z_ref, gnw_ref, wo_hbm, fnw_ref, w1_hbm, w2_hbm, onw_ref,
                o_ref, wo_ref, w1_ref, w2_ref, stage_wide, stage_tall, sem_wide, sem_tall):
    @pl.when(pl.program_id(0) == 0)
    def _():
        _cast_weights_once(layer, [(w1_hbm, w1_ref, stage_wide, sem_wide, FFN_CAST_ROWS_WIDE),
                                   (w2_hbm, w2_ref, stage_tall, sem_tall, FFN_CAST_ROWS_TALL),
                                   (wo_hbm, wo_ref, stage_tall, sem_tall, FFN_CAST_ROWS_TALL)])

    subs = [slice(r0, r0 + FFN_SUB) for r0 in range(0, FFN_TM, FFN_SUB)]
    rms = lambda t: t * lax.rsqrt(jnp.mean(t * t, axis=-1, keepdims=True) + EPS)
    hres, hn, act, acc = {}, {}, {}, {}
    for r in subs:
        o = of_ref[r, :].astype(F32) + ob_ref[r, :].astype(F32)
        heads = [rms(o[:, h * GDN_HEAD_DIM:(h + 1) * GDN_HEAD_DIM]) * gnw_ref[...] for h in range(GDN_HEADS)]
        z = z_ref[r, :]
        gdn = jnp.concatenate(heads, axis=-1) * (z * jax.nn.sigmoid(z))
        hres[r.start] = (x_ref[r, :] + jnp.dot(attn_ref[r, :], wo_ref[:ATTN_Q, :], preferred_element_type=F32)
                         + _dot(gdn, wo_ref[ATTN_Q:, :]))
    for r in subs:
        hn[r.start] = (rms(hres[r.start]) * fnw_ref[...]).astype(BF16)
        act[r.start] = jnp.dot(hn[r.start], w1_ref[...], preferred_element_type=F32)
    for r in subs:
        a = jnp.square(jnp.maximum(act[r.start], 0.0)).astype(BF16)
        acc[r.start] = hres[r.start] + jnp.dot(a, w2_ref[...], preferred_element_type=F32)
    for r in subs:
        o_ref[r, :] = rms(acc[r.start]) * onw_ref[...]


def _out_ffn(x2, attn, o_f, o_b, z, gnw, wo, fnw, w1, w2, onw, layer):
    n = x2.shape[0]
    assert wo.shape[1:] == (D_MODEL, D_MODEL) and w1.shape[1:] == (D_MODEL, D_FF) and w2.shape[1:] == (D_FF, D_MODEL)
    row = lambda w: pl.BlockSpec((FFN_TM, w), lambda i: (i, 0))
    full = lambda a: pl.BlockSpec(a.shape, lambda i: (0, 0))
    hbm = pl.BlockSpec(memory_space=pl.ANY)
    return pl.pallas_call(
        functools.partial(_ffn_kernel, layer),
        grid=(n // FFN_TM,),
        in_specs=[row(D_MODEL), row(ATTN_Q), row(GDN_W), row(GDN_W), row(GDN_W),
                  full(gnw), hbm, full(fnw), hbm, hbm, full(onw)],
        out_specs=row(D_MODEL),
        out_shape=jax.ShapeDtypeStruct((n, D_MODEL), F32),
        scratch_shapes=[pltpu.VMEM((D_MODEL, D_MODEL), BF16),
                        pltpu.VMEM((D_MODEL, D_FF), BF16),
                        pltpu.VMEM((D_FF, D_MODEL), BF16),
                        pltpu.VMEM((2, FFN_CAST_ROWS_WIDE, D_FF), F32),
                        pltpu.VMEM((2, FFN_CAST_ROWS_TALL, D_MODEL), F32),
                        pltpu.SemaphoreType.DMA((2,)),
                        pltpu.SemaphoreType.DMA((2,))],
        compiler_params=pltpu.CompilerParams(dimension_semantics=("arbitrary",),
                                             vmem_limit_bytes=VMEM_LIMIT),
        name="out_ffn",
    )(x2, attn, o_f, o_b, z, gnw, wo, fnw, w1, w2, onw)


def _layer(h, band, norm_mix_w, w_in, layer, attn_sink, conv_w, gdn_a_log, gdn_dt_bias, gdn_norm_w,
           w_out, norm_ffn_w, w_ffn_in, w_ffn_out, out_norm_w):
    b, s, _ = h.shape
    n = b * s
    x2 = h.reshape(n, D_MODEL)
    conv_pad = jnp.zeros((SUBLANES, 3 * GDN_W), F32).at[:CONV_K].set(conv_w)
    q_a, k_a, v_a, yn, z_g, ab = _proj(x2, norm_mix_w.reshape(1, D_MODEL), w_in, conv_pad, layer, s)

    attn = _attention(q_a, k_a, v_a, band, attn_sink, b)

    gate_par = jnp.zeros((SUBLANES, LANES), F32)
    gate_par = gate_par.at[0, :N_CHAIN].set(gdn_a_log.reshape(-1)).at[1, :N_CHAIN].set(gdn_dt_bias.reshape(-1))
    w_c, u_c, q_dec, k_dec_t, qk, egl = _gdn_prep(yn, ab.reshape(b, s, LANES), gate_par)
    o_f, o_b = _gdn_scan(egl[..., 0].reshape(-1), w_c, u_c, q_dec, k_dec_t, qk)

    out = _out_ffn(x2, attn, o_f.reshape(n, GDN_W), o_b.reshape(n, GDN_W), z_g,
                   gdn_norm_w.reshape(1, GDN_HEAD_DIM), w_out, norm_ffn_w.reshape(1, D_MODEL),
                   w_ffn_in, w_ffn_out, out_norm_w.reshape(1, D_MODEL), layer)
    return out.reshape(b, s, D_MODEL)


def kernel(x, norm_mix_w, w_in, rel_bias, attn_sink, conv_w, gdn_a_log, gdn_dt_bias, gdn_norm_w, w_out,
           norm_ffn_w, w_ffn_in, w_ffn_out, norm_final_w):
    depth = w_in.shape[0]
    assert depth == 1, "the fused output kernel applies the final norm after the single trunk layer"
    rel = (np.arange(3 * BLOCK)[None, :] - BLOCK) - np.arange(BLOCK)[:, None]
    bucket = _t5_buckets(jnp.asarray(rel, dtype=jnp.int32))
    band = _bias_band(rel_bias, bucket.T)
    return _layer(x, band, norm_mix_w[0], w_in, 0, attn_sink[0], conv_w[0], gdn_a_log[0], gdn_dt_bias[0],
                  gdn_norm_w[0], w_out, norm_ffn_w[0], w_ffn_in, w_ffn_out, norm_final_w)
```

```python
import functools
import math

import jax
import jax.numpy as jnp
import numpy as np
from jax import lax
from jax.experimental import pallas as pl
from jax.experimental.pallas import tpu as pltpu

F32 = jnp.float32
BF16 = jnp.bfloat16

D_MODEL = 1024
ATTN_HEADS = 8
ATTN_KV_HEADS = 2
ATTN_HEAD_DIM = 64
ATTN_GROUP = ATTN_HEADS // ATTN_KV_HEADS
WINDOW = 128
BLOCK = 128
N_BUCKETS = 32
MAX_DISTANCE = 128
GDN_HEADS = 4
GDN_HEAD_DIM = 128
CONV_K = 5
CHUNK = 64
N_DIR = 2
N_CHAIN = N_DIR * GDN_HEADS
D_FF = 4 * D_MODEL
EPS = 1e-6
LOG2E = math.log2(math.e)
ATTN_Q = ATTN_HEADS * ATTN_HEAD_DIM
ATTN_KV = ATTN_KV_HEADS * ATTN_HEAD_DIM
GDN_W = GDN_HEADS * GDN_HEAD_DIM
LANES = 128
SUBLANES = 8
VMEM_LIMIT = 56 * 1024 * 1024

PROJ_TM = 1024
PROJ_SUB = 256
PROJ_CAST_ROWS = 128
ATTN_QB = 8
PREP_T = 1024
ROW_STRIDE = 4
SCAN_CHUNKS = 8
FFN_TM = 512
FFN_SUB = 256
FFN_CAST_ROWS_WIDE = 128
FFN_CAST_ROWS_TALL = 512


def _dot(a, b):
    return jnp.dot(a.astype(BF16), b.astype(BF16), preferred_element_type=F32)


def _dot_nt(a, b):
    return lax.dot_general(a.astype(BF16), b.astype(BF16), (((1,), (1,)), ((), ())),
                           preferred_element_type=F32)


def _split3(x):
    hi = x.astype(BF16)
    r1 = x - hi.astype(F32)
    mid = r1.astype(BF16)
    lo = (r1 - mid.astype(F32)).astype(BF16)
    return hi, mid, lo


def _dot01_right(x, m01):
    hi, mid, lo = _split3(x)
    d = lambda p: jnp.dot(p, m01, preferred_element_type=F32)
    return d(hi) + d(mid) + d(lo)


def _conv_silu_norm(xe_ref, cw_ref, yn_ref, r0, n_out):
    halo = CONV_K // 2
    n_rows = n_out // ROW_STRIDE
    n_slab = 3 * GDN_W // LANES
    units = [(sb, ph) for sb in range(n_slab) for ph in range(ROW_STRIDE)]
    half_taps = [cw_ref[j:j + 1, :] * 0.5 for j in range(CONV_K)]
    yv = {}
    for sb, ph in units:
        lanes = slice(sb * LANES, (sb + 1) * LANES)
        acc = None
        for j in range(CONV_K):
            win = xe_ref[sb, pl.ds(SUBLANES - halo + j + ph, n_rows, stride=ROW_STRIDE), :]
            term = half_taps[j][:, lanes] * win
            acc = term if acc is None else acc + term
        yv[sb, ph] = acc
    for key in units:
        yv[key] = yv[key] + yv[key] * jnp.tanh(yv[key])
    for sb, ph in units:
        if sb < 2 * GDN_HEADS:
            scale = lax.rsqrt(jnp.sum(yv[sb, ph] * yv[sb, ph], axis=-1, keepdims=True) + EPS)
            if sb < GDN_HEADS:
                scale = scale * (GDN_HEAD_DIM ** -0.5)
            yv[sb, ph] = yv[sb, ph] * scale
    for sb, ph in units:
        yn_ref[sb, pl.ds(r0 + ph, n_rows, stride=ROW_STRIDE), :] = yv[sb, ph]


def _proj_kernel(layer, tiles_per_seq, x_ref, xp_ref, xn_ref, nw_ref, wt_hbm, cw_ref,
                 qa_ref, ka_ref, va_ref, yn_ref, z_ref, ab_ref, wb_ref, stage_ref, sem, *xe_refs):
    d_in = wt_hbm.shape[1]
    d_main = d_in // LANES * LANES
    step = pl.program_id(0)

    @pl.when(step == 0)
    def _():
        starts = list(range(0, d_main, PROJ_CAST_ROWS)) + [d_main]
        sizes = [PROJ_CAST_ROWS] * (len(starts) - 1) + [d_in - d_main]
        copies = [pltpu.make_async_copy(wt_hbm.at[layer, pl.ds(c0, nr)], stage_ref.at[k % 2, pl.ds(0, nr)],
                                        sem.at[k % 2]) for k, (c0, nr) in enumerate(zip(starts, sizes))]
        lane = lax.broadcasted_iota(jnp.int32, (D_MODEL, PROJ_CAST_ROWS), 1)
        copies[0].start()
        for k, (c0, nr) in enumerate(zip(starts, sizes)):
            if k + 1 < len(copies):
                copies[k + 1].start()
            copies[k].wait()
            cols = stage_ref[k % 2].T
            if nr < PROJ_CAST_ROWS:
                cols = jnp.where(lane < nr, cols, 0.0)
            wb_ref[:, c0:c0 + PROJ_CAST_ROWS] = cols.astype(BF16)

    o_q, o_k, o_v, o_g, o_z, o_ab = 0, ATTN_Q, ATTN_Q + ATTN_KV, ATTN_Q + 2 * ATTN_KV, \
        ATTN_Q + 2 * ATTN_KV + 3 * GDN_W, ATTN_Q + 2 * ATTN_KV + 4 * GDN_W
    n_slab = 3 * GDN_W // LANES

    def normed(x):
        ms = jnp.mean(x * x, axis=-1, keepdims=True)
        return (x * lax.rsqrt(ms + EPS) * nw_ref[...]).astype(BF16)

    n_sub = PROJ_TM // PROJ_SUB

    def to_slabs(xe_ref, y, row0):
        for sb in range(n_slab):
            xe_ref[sb, row0:row0 + y.shape[0], :] = y[:, sb * LANES:(sb + 1) * LANES]

    def put(k, y):
        to_slabs(xe_refs[k], y, SUBLANES)
        if k > 0:
            to_slabs(xe_refs[k - 1], y[:SUBLANES], SUBLANES + PROJ_SUB)
        if k + 1 < n_sub:
            to_slabs(xe_refs[k + 1], y[PROJ_SUB - SUBLANES:], 0)

    i = step % tiles_per_seq
    outs = ((qa_ref, o_q, ATTN_Q, ATTN_HEAD_DIM ** -0.5 * LOG2E), (ka_ref, o_k, ATTN_KV, None), (va_ref, o_v, ATTN_KV, None),
            (z_ref, o_z, GDN_W, None), (ab_ref, o_ab, LANES, None))
    for k in range(n_sub):
        rows = slice(k * PROJ_SUB, (k + 1) * PROJ_SUB)
        if k == 0:
            xg = normed(jnp.concatenate([xp_ref[...], xn_ref[...], x_ref[rows, :]], axis=0))
            yg = jnp.dot(xg, wb_ref[:, o_g:o_z], preferred_element_type=F32)
            to_slabs(xe_refs[0], jnp.where(i > 0, yg[:SUBLANES], 0.0), 0)
            to_slabs(xe_refs[n_sub - 1], jnp.where(i < tiles_per_seq - 1, yg[SUBLANES:2 * SUBLANES], 0.0),
                     SUBLANES + PROJ_SUB)
            put(0, yg[2 * SUBLANES:])
            xn = xg[2 * SUBLANES:]
        else:
            xn = normed(x_ref[rows, :])
            put(k, jnp.dot(xn, wb_ref[:, o_g:o_z], preferred_element_type=F32))
            _conv_silu_norm(xe_refs[k - 1], cw_ref, yn_ref, (k - 1) * PROJ_SUB, PROJ_SUB)
        for ref, c0, width, scale in outs:
            y = jnp.dot(xn, wb_ref[:, c0:c0 + width], preferred_element_type=F32)
            if scale is not None:
                y = y * scale
            ref[rows, :] = y.astype(ref.dtype)
    _conv_silu_norm(xe_refs[n_sub - 1], cw_ref, yn_ref, PROJ_TM - PROJ_SUB, PROJ_SUB)


def _proj(x2, norm_w, w_in, conv_w, layer, seq_len):
    n = x2.shape[0]
    d_in = w_in.shape[2]
    d_pad = d_in // LANES * LANES + LANES
    assert d_pad == ATTN_Q + 2 * ATTN_KV + 4 * GDN_W + LANES and d_in - (d_pad - LANES) == 2 * N_CHAIN
    assert seq_len % PROJ_TM == 0
    w_t = jnp.swapaxes(w_in, 1, 2)
    nh8 = PROJ_TM // SUBLANES
    row = lambda w: pl.BlockSpec((PROJ_TM, w), lambda i: (i, 0))
    return pl.pallas_call(
        functools.partial(_proj_kernel, layer, seq_len // PROJ_TM),
        grid=(n // PROJ_TM,),
        in_specs=[row(D_MODEL),
                  pl.BlockSpec((SUBLANES, D_MODEL), lambda i: (jnp.maximum(i * nh8 - 1, 0), 0)),
                  pl.BlockSpec((SUBLANES, D_MODEL), lambda i: (jnp.minimum((i + 1) * nh8, n // SUBLANES - 1), 0)),
                  pl.BlockSpec((1, D_MODEL), lambda i: (0, 0)),
                  pl.BlockSpec(memory_space=pl.ANY),
                  pl.BlockSpec((SUBLANES, 3 * GDN_W), lambda i: (0, 0))],
        out_specs=[row(ATTN_Q), row(ATTN_KV), row(ATTN_KV),
                   pl.BlockSpec((3 * GDN_W // LANES, PROJ_TM, LANES), lambda i: (0, i, 0)),
                   row(GDN_W), row(LANES)],
        out_shape=[jax.ShapeDtypeStruct((n, ATTN_Q), BF16),
                   jax.ShapeDtypeStruct((n, ATTN_KV), BF16),
                   jax.ShapeDtypeStruct((n, ATTN_KV), BF16),
                   jax.ShapeDtypeStruct((3 * GDN_W // LANES, n, LANES), F32),
                   jax.ShapeDtypeStruct((n, GDN_W), F32),
                   jax.ShapeDtypeStruct((n, LANES), F32)],
        scratch_shapes=[pltpu.VMEM((D_MODEL, d_pad), BF16),
                        pltpu.VMEM((2, PROJ_CAST_ROWS, D_MODEL), F32),
                        pltpu.SemaphoreType.DMA((2,))]
        + [pltpu.VMEM((3 * GDN_W // LANES, PROJ_SUB + 2 * SUBLANES, LANES), F32)] * (PROJ_TM // PROJ_SUB),
        compiler_params=pltpu.CompilerParams(dimension_semantics=("arbitrary",),
                                             vmem_limit_bytes=VMEM_LIMIT),
        name="proj",
    )(x2, x2, x2, norm_w, w_t, conv_w)


def _bias_kernel(relb_ref, bucket_ref, o_ref):
    bucket = bucket_ref[...]
    key = lax.broadcasted_iota(jnp.int32, (3 * BLOCK, BLOCK), 0)
    qry = lax.broadcasted_iota(jnp.int32, (3 * BLOCK, BLOCK), 1)
    in_window = jnp.abs(key - BLOCK - qry) <= WINDOW
    for h in range(ATTN_HEADS):
        acc = jnp.zeros((3 * BLOCK, BLOCK), F32)
        for b in range(N_BUCKETS):
            acc = jnp.where(bucket == b, relb_ref[b, h], acc)
        o_ref[h // 2, :, (h % 2) * BLOCK:(h % 2 + 1) * BLOCK] = jnp.where(in_window, acc * LOG2E, -1e30)


def _bias_band(rel_bias, bucket_t):
    shape = (ATTN_HEADS // 2, 3 * BLOCK, 2 * BLOCK)
    return pl.pallas_call(
        _bias_kernel,
        in_specs=[pl.BlockSpec(memory_space=pltpu.SMEM),
                  pl.BlockSpec((3 * BLOCK, BLOCK), lambda: (0, 0))],
        out_specs=pl.BlockSpec(shape, lambda: (0, 0, 0)),
        out_shape=jax.ShapeDtypeStruct(shape, F32),
        name="bias_band",
    )(rel_bias, bucket_t)


def _t5_buckets(rel):
    nb = N_BUCKETS // 2
    max_exact = nb // 2
    base = jnp.where(rel > 0, nb, 0)
    n = jnp.abs(rel)
    log_ratio = jnp.log(jnp.maximum(n, 1).astype(jnp.float32) / max_exact) / math.log(MAX_DISTANCE / max_exact)
    large = jnp.minimum(max_exact + (log_ratio * (nb - max_exact)).astype(jnp.int32), nb - 1)
    return base + jnp.where(n < max_exact, n, large)


def _attn_kernel(sink_ref, q_ref, kp_ref, kc_ref, kn_ref, vp_ref, vc_ref, vn_ref, bias_ref, o_ref):
    n = pl.program_id(1)
    last = pl.num_programs(1) - 1
    kband = jnp.concatenate([kp_ref[...], kc_ref[...], kn_ref[...]], axis=0)
    vband = jnp.concatenate([vp_ref[...], vc_ref[...], vn_ref[...]], axis=0)
    vband_t = vband.astype(F32).T.astype(BF16)
    key = lax.broadcasted_iota(jnp.int32, (3 * BLOCK, 1), 0)
    first_head = lax.broadcasted_iota(jnp.int32, (1, 2 * BLOCK), 1) < BLOCK
    head = lambda t, i: t[:, i * ATTN_HEAD_DIM:(i + 1) * ATTN_HEAD_DIM]
    n_pairs = ATTN_HEADS // 2
    kv_of = lambda pr: (2 * pr) // ATTN_GROUP
    units = [(j, pr) for j in range(ATTN_QB) for pr in range(n_pairs)]
    band_rows = lambda j: slice(j * BLOCK, (j + 3) * BLOCK)

    scores = {}
    for j, pr in units:
        qj = q_ref[j * BLOCK:(j + 1) * BLOCK, :]
        q2 = jnp.concatenate([head(qj, 2 * pr), head(qj, 2 * pr + 1)], axis=0)
        scores[j, pr] = _dot_nt(head(kband, kv_of(pr))[band_rows(j)], q2)
    probs, dens = {}, {}
    for j, pr in units:
        s = scores[j, pr] + bias_ref[pr]
        if j == 0:
            s = jnp.where((key < BLOCK) & (n == 0), -1e30, s)
        if j == ATTN_QB - 1:
            s = jnp.where((key >= 2 * BLOCK) & (n == last), -1e30, s)
        sink = jnp.where(first_head, sink_ref[2 * pr], sink_ref[2 * pr + 1]) * LOG2E
        m = jnp.maximum(jnp.max(s, axis=0, keepdims=True), sink)
        probs[j, pr] = jnp.exp2(s - m).astype(BF16)
        dens[j, pr] = jnp.exp2(sink - m)
    outs_t = {}
    ones_rows = jnp.ones((2 * SUBLANES, 3 * BLOCK), BF16)
    for j, pr in units:
        kv = kv_of(pr)
        v_t = vband_t[kv * ATTN_HEAD_DIM:(kv + 1) * ATTN_HEAD_DIM, band_rows(j)]
        pv = jnp.dot(jnp.concatenate([v_t, ones_rows], axis=0), probs[j, pr], preferred_element_type=F32)
        outs_t[j, pr] = pv[:ATTN_HEAD_DIM] / (pv[ATTN_HEAD_DIM:ATTN_HEAD_DIM + 1] + dens[j, pr])
    for j in range(ATTN_QB):
        o_t = jnp.concatenate([outs_t[j, pr][:, half * BLOCK:(half + 1) * BLOCK]
                               for pr in range(n_pairs) for half in range(2)], axis=0)
        o_ref[j * BLOCK:(j + 1) * BLOCK, :] = o_t.T.astype(o_ref.dtype)


def _attention(q_a, k_a, v_a, band, sink, batch):
    n_tok = q_a.shape[0]
    nb = n_tok // batch // BLOCK
    rows = ATTN_QB * BLOCK
    steps = nb // ATTN_QB
    kv_spec = lambda r, f: pl.BlockSpec((r, ATTN_KV), f)
    prev = lambda bi, n: (bi * nb + jnp.maximum(n * ATTN_QB - 1, 0), 0)
    cur = lambda bi, n: (bi * steps + n, 0)
    nxt = lambda bi, n: (bi * nb + jnp.minimum((n + 1) * ATTN_QB, nb - 1), 0)
    return pl.pallas_call(
        _attn_kernel,
        grid=(batch, steps),
        in_specs=[pl.BlockSpec(memory_space=pltpu.SMEM),
                  pl.BlockSpec((rows, ATTN_Q), cur),
                  kv_spec(BLOCK, prev), kv_spec(rows, cur), kv_spec(BLOCK, nxt),
                  kv_spec(BLOCK, prev), kv_spec(rows, cur), kv_spec(BLOCK, nxt),
                  pl.BlockSpec(band.shape, lambda bi, n: (0, 0, 0))],
        out_specs=pl.BlockSpec((rows, ATTN_Q), cur),
        out_shape=jax.ShapeDtypeStruct((n_tok, ATTN_Q), BF16),
        compiler_params=pltpu.CompilerParams(dimension_semantics=("arbitrary", "arbitrary"),
                                             vmem_limit_bytes=VMEM_LIMIT),
        name="attn",
    )(sink, q_a, k_a, k_a, k_a, v_a, v_a, v_a, band)


def _gprep_kernel(yn_ref, ab_ref, gp_ref, w_ref, u_ref, qd_ref, kdt_ref, qk_ref, egl_ref):
    t_len = PREP_T

    ab = ab_ref[0]
    sp_in = ab + gp_ref[1:2, :]
    softplus = jnp.maximum(sp_in, 0.0) + jnp.log1p(jnp.exp(-jnp.abs(sp_in)))
    g = (-jnp.exp(gp_ref[0:1, :]) * LOG2E) * softplus
    beta = jax.nn.sigmoid(ab)

    r_t = lax.broadcasted_iota(jnp.int32, (t_len, t_len), 0)
    c_t = lax.broadcasted_iota(jnp.int32, (t_len, t_len), 1)
    same = (r_t // CHUNK) == (c_t // CHUNK)
    lower = jnp.where(same & (r_t >= c_t), 1.0, 0.0).astype(BF16)
    upper = jnp.where(same & (r_t <= c_t), 1.0, 0.0).astype(BF16)
    g_t = g.T[:2 * SUBLANES]
    cs_row = (_dot01_right(g_t, upper), _dot01_right(g_t, lower))
    pad_rows = jnp.zeros((LANES - 2 * SUBLANES, t_len), F32)
    cs_col = tuple(jnp.concatenate([r, pad_rows], axis=0).T for r in cs_row)
    beta_t = beta.T[:3 * SUBLANES]

    qs = [yn_ref[h] for h in range(GDN_HEADS)]
    ks = [yn_ref[GDN_HEADS + h] for h in range(GDN_HEADS)]
    vs = [yn_ref[2 * GDN_HEADS + h] for h in range(GDN_HEADS)]
    kts = [kh.T for kh in ks]

    r_c = lax.broadcasted_iota(jnp.int32, (CHUNK, LANES), 0)
    lane = lax.broadcasted_iota(jnp.int32, (CHUNK, LANES), 1)
    is_fwd = lane < CHUNK
    c_c = lane % CHUNK
    eye = jnp.where(r_c == c_c, 1.0, 0.0).astype(F32)
    incl = (is_fwd & (r_c >= c_c)) | (~is_fwd & (r_c <= c_c))
    strict = (is_fwd & (r_c > c_c)) | (~is_fwd & (r_c < c_c))
    r_d = lax.broadcasted_iota(jnp.int32, (2 * CHUNK, LANES), 0)
    c_d = lax.broadcasted_iota(jnp.int32, (2 * CHUNK, LANES), 1)
    same_dir = (r_d // CHUNK) == (c_d // CHUNK)
    level_mask = lambda s_, r_, c_: ((r_ // (2 * s_)) == (c_ // (2 * s_))) & ((r_ // s_) != (c_ // s_))
    stack2 = lambda t: jnp.concatenate([t, t], axis=0)

    n_chunks = t_len // CHUNK
    rows = lambda c: slice(c * CHUNK, (c + 1) * CHUNK)
    pairs = [(c, h) for c in range(n_chunks) for h in range(GDN_HEADS)]
    qkk = {}
    for c, h in pairs:
        k16 = ks[h][rows(c)].astype(BF16)
        qk16 = jnp.concatenate([qs[h][rows(c)].astype(BF16), k16], axis=0)
        qkk[c, h] = _dot_nt(qk16, stack2(k16))

    bcast = lambda col: jnp.broadcast_to(col, (CHUNK, LANES))
    brow, grow, glast, a_mat, t_mat, kv16 = {}, {}, {}, {}, {}, {}
    for key in pairs:
        c, h = key
        g_full = {}
        for d in range(N_DIR):
            j = d * GDN_HEADS + h
            r_last = c * CHUNK + (CHUNK - 1 if d == 0 else 0)
            g_full[d] = bcast(cs_col[d][rows(c), j:j + 1])
            brow[key, d] = beta_t[SUBLANES + j:SUBLANES + j + 1, rows(c)]
            glast[key, d] = cs_col[d][r_last:r_last + 1, j:j + 1]
            grow[key, d] = cs_row[d][j:j + 1, rows(c)]
        gcol2 = jnp.where(is_fwd, g_full[0], g_full[1])
        grow2 = jnp.concatenate([grow[key, 0], grow[key, 1]], axis=1)
        brow2 = jnp.concatenate([brow[key, 0], brow[key, 1]], axis=1)
        decay = jnp.exp2(jnp.where(incl, gcol2 - grow2, -jnp.inf))
        a_mat[key] = jnp.where(strict, qkk[key][CHUNK:] * decay, 0.0) * brow2
        t_mat[key] = eye - jnp.where(level_mask(1, r_c, c_c), a_mat[key], 0.0)
        qk_ref[0, h, rows(c), :] = (qkk[key][:CHUNK] * decay * brow2).astype(qk_ref.dtype)
        eg = [jnp.exp2(g_full[d]) for d in range(N_DIR)]
        for d in range(N_DIR):
            qd_ref[0, d * GDN_HEADS + h, rows(c), :] = (qs[h][rows(c)] * eg[d]).astype(qd_ref.dtype)
        kv16[key] = jnp.concatenate(
            [jnp.concatenate([ks[h][rows(c)] * eg[d], vs[h][rows(c)]], axis=1) for d in range(N_DIR)],
            axis=0).astype(BF16)

    same_dir16 = jnp.where(same_dir, 1.0, 0.0).astype(BF16)
    block_diag = lambda t16: stack2(t16) * same_dir16
    a_bd = {key: block_diag(a_mat[key].astype(BF16)) for key in pairs}
    s = 2
    while s < CHUNK:
        lvl = level_mask(s, r_c, c_c)
        t16 = {key: t_mat[key].astype(BF16) for key in pairs}
        x_mat = {key: jnp.dot(t16[key], a_bd[key], preferred_element_type=F32) for key in pairs}
        y_mat = {key: jnp.dot(x_mat[key].astype(BF16), block_diag(t16[key]), preferred_element_type=F32)
                 for key in pairs}
        t_mat = {key: t_mat[key] - jnp.where(lvl, y_mat[key], 0.0) for key in pairs}
        s *= 2

    wu = {}
    for key in pairs:
        c, h = key
        t_sel = block_diag(t_mat[key].astype(BF16))
        wu[key] = jnp.dot(t_sel, kv16[key], preferred_element_type=F32)

    for c in range(n_chunks):
        egl_rows = []
        for d in range(N_DIR):
            for h in range(GDN_HEADS):
                key = (c, h)
                j = d * GDN_HEADS + h
                wu_d = wu[key][d * CHUNK:(d + 1) * CHUNK]
                w_ref[0, j, rows(c), :] = wu_d[:, :GDN_HEAD_DIM].astype(w_ref.dtype)
                u_ref[0, j, rows(c), :] = wu_d[:, GDN_HEAD_DIM:].astype(u_ref.dtype)
                egl_rows.append(jnp.broadcast_to(jnp.exp2(glast[key, d]), (1, LANES)))
        egl_ref[0, c] = jnp.concatenate(egl_rows, axis=0)
    for cp in range(n_chunks // 2):
        for d in range(N_DIR):
            for h in range(GDN_HEADS):
                fac = jnp.concatenate([brow[(c, h), d] * jnp.exp2(glast[(c, h), d] - grow[(c, h), d])
                                       for c in (2 * cp, 2 * cp + 1)], axis=1)
                kdt = kts[h][:, 2 * cp * CHUNK:(2 * cp + 2) * CHUNK] * fac
                kdt_ref[0, d * GDN_HEADS + h, cp] = kdt.astype(kdt_ref.dtype)


def _gdn_prep(yn, ab, gate_par):
    b, s, _ = ab.shape
    nt = s // PREP_T
    cpb = PREP_T // CHUNK
    nc = s // CHUNK
    chain = lambda last: pl.BlockSpec((1, N_CHAIN, PREP_T, last), lambda bi, i: (bi, 0, i, 0))
    return pl.pallas_call(
        _gprep_kernel,
        grid=(b, nt),
        in_specs=[pl.BlockSpec((3 * GDN_W // LANES, PREP_T, LANES), lambda bi, i: (0, bi * nt + i, 0)),
                  pl.BlockSpec((1, PREP_T, LANES), lambda bi, i: (bi, i, 0)),
                  pl.BlockSpec((SUBLANES, LANES), lambda bi, i: (0, 0))],
        out_specs=[chain(GDN_HEAD_DIM), chain(GDN_HEAD_DIM), chain(GDN_HEAD_DIM),
                   pl.BlockSpec((1, N_CHAIN, cpb // 2, GDN_HEAD_DIM, 2 * CHUNK), lambda bi, i: (bi, 0, i, 0, 0)),
                   pl.BlockSpec((1, GDN_HEADS, PREP_T, LANES), lambda bi, i: (bi, 0, i, 0)),
                   pl.BlockSpec((1, cpb, N_CHAIN, LANES), lambda bi, i: (bi, i, 0, 0))],
        out_shape=[jax.ShapeDtypeStruct((b, N_CHAIN, s, GDN_HEAD_DIM), BF16),
                   jax.ShapeDtypeStruct((b, N_CHAIN, s, GDN_HEAD_DIM), BF16),
                   jax.ShapeDtypeStruct((b, N_CHAIN, s, GDN_HEAD_DIM), BF16),
                   jax.ShapeDtypeStruct((b, N_CHAIN, nc // 2, GDN_HEAD_DIM, 2 * CHUNK), BF16),
                   jax.ShapeDtypeStruct((b, GDN_HEADS, s, LANES), BF16),
                   jax.ShapeDtypeStruct((b, nc, N_CHAIN, LANES), F32)],
        compiler_params=pltpu.CompilerParams(dimension_semantics=("arbitrary", "arbitrary"),
                                             vmem_limit_bytes=VMEM_LIMIT),
        name="gdn_prep",
    )(yn, ab, gate_par)


def _gscan_kernel(egl_ref, wf_ref, wb_ref, uf_ref, ub_ref, qf_ref, qb_ref, kf_ref, kb_ref,
                  pf_ref, pb_ref, of_ref, ob_ref, state_ref):
    t = pl.program_id(0)
    n_steps = pl.num_programs(0)
    nc = n_steps * SCAN_CHUNKS
    n_batch = wf_ref.shape[0]

    @pl.when(t == 0)
    def _():
        state_ref[...] = jnp.zeros_like(state_ref)

    dirs = ((wf_ref, uf_ref, qf_ref, kf_ref, pf_ref, of_ref), (wb_ref, ub_ref, qb_ref, kb_ref, pb_ref, ob_ref))
    chains = [(bi, d, h) for bi in range(n_batch) for d in range(N_DIR) for h in range(GDN_HEADS)]
    slot = lambda bi, d, h: (bi * N_DIR + d) * GDN_HEADS + h
    st = {key: state_ref[slot(*key)] for key in chains}

    for sub in range(SCAN_CHUNKS):
        local = (sub, SCAN_CHUNKS - 1 - sub)
        rows = [slice(c * CHUNK, (c + 1) * CHUNK) for c in local]
        chunk = (t * SCAN_CHUNKS + sub, nc - 1 - (t * SCAN_CHUNKS + sub))
        r = {}
        for key in chains:
            bi, d, h = key
            w_ref, _, q_ref = dirs[d][:3]
            wq = jnp.concatenate([w_ref[bi, h, rows[d]], q_ref[bi, h, rows[d]]], axis=0)
            r[key] = _dot(wq, st[key])
        v_pad, intra = {}, {}
        for key in chains:
            bi, d, h = key
            u_ref, p_ref = dirs[d][1], dirs[d][4]
            v_new = (u_ref[bi, h, rows[d]].astype(F32) - r[key][:CHUNK]).astype(BF16)
            zeros = jnp.zeros_like(v_new)
            v_pad[key] = (jnp.concatenate([v_new, zeros], axis=0), jnp.concatenate([zeros, v_new], axis=0))
            intra[key] = jnp.dot(p_ref[bi, h, rows[d]], v_pad[key][d], preferred_element_type=F32)
        for key in chains:
            bi, d, h = key
            k_ref = dirs[d][3]
            egl = egl_ref[(bi * nc + chunk[d]) * N_CHAIN + d * GDN_HEADS + h]
            st[key] = st[key] * egl + jnp.dot(k_ref[bi, h, local[d] // 2], v_pad[key][local[d] % 2],
                                              preferred_element_type=F32)
        for bi in range(n_batch):
            for d in range(N_DIR):
                o_ref = dirs[d][5]
                o_ref[bi, rows[d], :] = jnp.concatenate(
                    [r[bi, d, h][CHUNK:] + intra[bi, d, h] for h in range(GDN_HEADS)], axis=-1).astype(o_ref.dtype)

    for key in chains:
        state_ref[slot(*key)] = st[key]


def _gdn_scan(egl, w, u, qd, kdt, qk):
    b, _, s, _ = w.shape
    rows = SCAN_CHUNKS * CHUNK
    n_steps = s // rows
    fwd_i = lambda t: t
    bwd_i = lambda t: n_steps - 1 - t
    chain = lambda d, at, last: pl.BlockSpec((b, GDN_HEADS, rows, last), lambda t: (0, d, at(t), 0))
    kspec = lambda d, at: pl.BlockSpec((b, GDN_HEADS, SCAN_CHUNKS // 2, GDN_HEAD_DIM, 2 * CHUNK),
                                       lambda t: (0, d, at(t), 0, 0))
    dk = GDN_HEAD_DIM
    return pl.pallas_call(
        _gscan_kernel,
        grid=(n_steps,),
        in_specs=[pl.BlockSpec(memory_space=pltpu.SMEM),
                  chain(0, fwd_i, dk), chain(1, bwd_i, dk), chain(0, fwd_i, dk), chain(1, bwd_i, dk),
                  chain(0, fwd_i, dk), chain(1, bwd_i, dk), kspec(0, fwd_i), kspec(1, bwd_i),
                  chain(0, fwd_i, LANES), chain(0, bwd_i, LANES)],
        out_specs=[pl.BlockSpec((b, rows, GDN_W), lambda t: (0, fwd_i(t), 0)),
                   pl.BlockSpec((b, rows, GDN_W), lambda t: (0, bwd_i(t), 0))],
        out_shape=[jax.ShapeDtypeStruct((b, s, GDN_W), BF16),
                   jax.ShapeDtypeStruct((b, s, GDN_W), BF16)],
        scratch_shapes=[pltpu.VMEM((b * N_CHAIN, GDN_HEAD_DIM, GDN_HEAD_DIM), F32)],
        compiler_params=pltpu.CompilerParams(dimension_semantics=("arbitrary",),
                                             vmem_limit_bytes=VMEM_LIMIT),
        name="gdn_scan",
    )(egl, w, w, u, u, qd, qd, kdt, kdt, qk, qk)


def _cast_weights_once(layer, copies):
    jobs, used = [], {}
    for src, dst, stage, sem, chunk_rows in copies:
        for r0 in range(0, src.shape[1], chunk_rows):
            slot = used.get(id(stage), 0) % 2
            used[id(stage)] = used.get(id(stage), 0) + 1
            rows = pl.ds(r0, chunk_rows)
            jobs.append((pltpu.make_async_copy(src.at[layer, rows], stage.at[slot], sem.at[slot]),
                         stage, slot, dst, rows))
    jobs[0][0].start()
    for k, (copy, stage, slot, dst, rows) in enumerate(jobs):
        if k + 1 < len(jobs):
            jobs[k + 1][0].start()
        copy.wait()
        dst[rows, :] = stage[slot].astype(BF16)


def _ffn_kernel(layer, x_ref, attn_ref, of_ref, ob_ref, z_ref, gnw_ref, wo_hbm, fnw_ref, w1_hbm, w2_hbm, onw_ref,
                o_ref, wo_ref, w1_ref, w2_ref, stage_wide, stage_tall, sem_wide, sem_tall):
    @pl.when(pl.program_id(0) == 0)
    def _():
        _cast_weights_once(layer, [(w1_hbm, w1_ref, stage_wide, sem_wide, FFN_CAST_ROWS_WIDE),
                                   (w2_hbm, w2_ref, stage_tall, sem_tall, FFN_CAST_ROWS_TALL),
                                   (wo_hbm, wo_ref, stage_tall, sem_tall, FFN_CAST_ROWS_TALL)])

    subs = [slice(r0, r0 + FFN_SUB) for r0 in range(0, FFN_TM, FFN_SUB)]
    rms = lambda t: t * lax.rsqrt(jnp.mean(t * t, axis=-1, keepdims=True) + EPS)
    hres, hn, act, acc = {}, {}, {}, {}
    for r in subs:
        o = of_ref[r, :].astype(F32) + ob_ref[r, :].astype(F32)
        heads = [rms(o[:, h * GDN_HEAD_DIM:(h + 1) * GDN_HEAD_DIM]) * gnw_ref[...] for h in range(GDN_HEADS)]
        z = z_ref[r, :]
        gdn = jnp.concatenate(heads, axis=-1) * (z * jax.nn.sigmoid(z))
        hres[r.start] = (x_ref[r, :] + jnp.dot(attn_ref[r, :], wo_ref[:ATTN_Q, :], preferred_element_type=F32)
                         + _dot(gdn, wo_ref[ATTN_Q:, :]))
    for r in subs:
        hn[r.start] = (rms(hres[r.start]) * fnw_ref[...]).astype(BF16)
        act[r.start] = jnp.dot(hn[r.start], w1_ref[...], preferred_element_type=F32)
    for r in subs:
        a = jnp.square(jnp.maximum(act[r.start], 0.0)).astype(BF16)
        acc[r.start] = hres[r.start] + jnp.dot(a, w2_ref[...], preferred_element_type=F32)
    for r in subs:
        o_ref[r, :] = rms(acc[r.start]) * onw_ref[...]


def _out_ffn(x2, attn, o_f, o_b, z, gnw, wo, fnw, w1, w2, onw, layer):
    n = x2.shape[0]
    assert wo.shape[1:] == (D_MODEL, D_MODEL) and w1.shape[1:] == (D_MODEL, D_FF) and w2.shape[1:] == (D_FF, D_MODEL)
    row = lambda w: pl.BlockSpec((FFN_TM, w), lambda i: (i, 0))
    full = lambda a: pl.BlockSpec(a.shape, lambda i: (0, 0))
    hbm = pl.BlockSpec(memory_space=pl.ANY)
    return pl.pallas_call(
        functools.partial(_ffn_kernel, layer),
        grid=(n // FFN_TM,),
        in_specs=[row(D_MODEL), row(ATTN_Q), row(GDN_W), row(GDN_W), row(GDN_W),
                  full(gnw), hbm, full(fnw), hbm, hbm, full(onw)],
        out_specs=row(D_MODEL),
        out_shape=jax.ShapeDtypeStruct((n, D_MODEL), F32),
        scratch_shapes=[pltpu.VMEM((D_MODEL, D_MODEL), BF16),
                        pltpu.VMEM((D_MODEL, D_FF), BF16),
                        pltpu.VMEM((D_FF, D_MODEL), BF16),
                        pltpu.VMEM((2, FFN_CAST_ROWS_WIDE, D_FF), F32),
                        pltpu.VMEM((2, FFN_CAST_ROWS_TALL, D_MODEL), F32),
                        pltpu.SemaphoreType.DMA((2,)),
                        pltpu.SemaphoreType.DMA((2,))],
        compiler_params=pltpu.CompilerParams(dimension_semantics=("arbitrary",),
                                             vmem_limit_bytes=VMEM_LIMIT),
        name="out_ffn",
    )(x2, attn, o_f, o_b, z, gnw, wo, fnw, w1, w2, onw)


def _layer(h, band, norm_mix_w, w_in, layer, attn_sink, conv_w, gdn_a_log, gdn_dt_bias, gdn_norm_w,
           w_out, norm_ffn_w, w_ffn_in, w_ffn_out, out_norm_w):
    b, s, _ = h.shape
    n = b * s
    x2 = h.reshape(n, D_MODEL)
    conv_pad = jnp.zeros((SUBLANES, 3 * GDN_W), F32).at[:CONV_K].set(conv_w)
    q_a, k_a, v_a, yn, z_g, ab = _proj(x2, norm_mix_w.reshape(1, D_MODEL), w_in, conv_pad, layer, s)

    attn = _attention(q_a, k_a, v_a, band, attn_sink, b)

    gate_par = jnp.zeros((SUBLANES, LANES), F32)
    gate_par = gate_par.at[0, :N_CHAIN].set(gdn_a_log.reshape(-1)).at[1, :N_CHAIN].set(gdn_dt_bias.reshape(-1))
    w_c, u_c, q_dec, k_dec_t, qk, egl = _gdn_prep(yn, ab.reshape(b, s, LANES), gate_par)
    o_f, o_b = _gdn_scan(egl[..., 0].reshape(-1), w_c, u_c, q_dec, k_dec_t, qk)

    out = _out_ffn(x2, attn, o_f.reshape(n, GDN_W), o_b.reshape(n, GDN_W), z_g,
                   gdn_norm_w.reshape(1, GDN_HEAD_DIM), w_out, norm_ffn_w.reshape(1, D_MODEL),
                   w_ffn_in, w_ffn_out, out_norm_w.reshape(1, D_MODEL), layer)
    return out.reshape(b, s, D_MODEL)


def kernel(x, norm_mix_w, w_in, rel_bias, attn_sink, conv_w, gdn_a_log, gdn_dt_bias, gdn_norm_w, w_out,
           norm_ffn_w, w_ffn_in, w_ffn_out, norm_final_w):
    depth = w_in.shape[0]
    assert depth == 1, "the fused output kernel applies the final norm after the single trunk layer"
    rel = (np.arange(3 * BLOCK)[None, :] - BLOCK) - np.arange(BLOCK)[:, None]
    bucket = _t5_buckets(jnp.asarray(rel, dtype=jnp.int32))
    band = _bias_band(rel_bias, bucket.T)
    return _layer(x, band, norm_mix_w[0], w_in, 0, attn_sink[0], conv_w[0], gdn_a_log[0], gdn_dt_bias[0],
                  gdn_norm_w[0], w_out, norm_ffn_w[0], w_ffn_in, w_ffn_out, norm_final_w)
```

```python
import functools
import math

import jax
import jax.numpy as jnp
import numpy as np
from jax import lax
from jax.experimental import pallas as pl
from jax.experimental.pallas import tpu as pltpu

F32 = jnp.float32
BF16 = jnp.bfloat16

D_MODEL = 1024
ATTN_HEADS = 8
ATTN_KV_HEADS = 2
ATTN_HEAD_DIM = 64
ATTN_GROUP = ATTN_HEADS // ATTN_KV_HEADS
WINDOW = 128
BLOCK = 128
N_BUCKETS = 32
MAX_DISTANCE = 128
GDN_HEADS = 4
GDN_HEAD_DIM = 128
CONV_K = 5
CHUNK = 64
N_DIR = 2
N_CHAIN = N_DIR * GDN_HEADS
D_FF = 4 * D_MODEL
EPS = 1e-6
LOG2E = math.log2(math.e)
ATTN_Q = ATTN_HEADS * ATTN_HEAD_DIM
ATTN_KV = ATTN_KV_HEADS * ATTN_HEAD_DIM
GDN_W = GDN_HEADS * GDN_HEAD_DIM
LANES = 128
SUBLANES = 8
VMEM_LIMIT = 56 * 1024 * 1024

PROJ_TM = 1024
PROJ_SUB = 256
PROJ_CAST_ROWS = 256
PROJ_CAST_SLOTS = 4
ATTN_QB = 8
PREP_T = 1024
ROW_STRIDE = 4
SCAN_CHUNKS = 8
FFN_TM = 512
FFN_SUB = 256
FFN_CAST_ROWS_WIDE = 128
FFN_CAST_ROWS_TALL = 512


def _dot(a, b):
    return jnp.dot(a.astype(BF16), b.astype(BF16), preferred_element_type=F32)


def _dot_nt(a, b):
    return lax.dot_general(a.astype(BF16), b.astype(BF16), (((1,), (1,)), ((), ())),
                           preferred_element_type=F32)


def _split3(x):
    hi = x.astype(BF16)
    r1 = x - hi.astype(F32)
    mid = r1.astype(BF16)
    lo = (r1 - mid.astype(F32)).astype(BF16)
    return hi, mid, lo


def _dot01_right(x, m01):
    hi, mid, lo = _split3(x)
    d = lambda p: jnp.dot(p, m01, preferred_element_type=F32)
    return d(hi) + d(mid) + d(lo)


def _conv_silu_norm(xe_ref, cw_ref, yn_ref, r0, n_out):
    halo = CONV_K // 2
    n_rows = n_out // ROW_STRIDE
    n_slab = 3 * GDN_W // LANES
    units = [(sb, ph) for sb in range(n_slab) for ph in range(ROW_STRIDE)]
    half_taps = [cw_ref[j:j + 1, :] * 0.5 for j in range(CONV_K)]
    yv = {}
    for sb, ph in units:
        lanes = slice(sb * LANES, (sb + 1) * LANES)
        acc = None
        for j in range(CONV_K):
            win = xe_ref[sb, pl.ds(SUBLANES - halo + j + ph, n_rows, stride=ROW_STRIDE), :]
            term = half_taps[j][:, lanes] * win
            acc = term if acc is None else acc + term
        yv[sb, ph] = acc
    for key in units:
        yv[key] = yv[key] + yv[key] * jnp.tanh(yv[key])
    for sb, ph in units:
        if sb < 2 * GDN_HEADS:
            scale = lax.rsqrt(jnp.sum(yv[sb, ph] * yv[sb, ph], axis=-1, keepdims=True) + EPS)
            if sb < GDN_HEADS:
                scale = scale * (GDN_HEAD_DIM ** -0.5)
            yv[sb, ph] = yv[sb, ph] * scale
    for sb, ph in units:
        yn_ref[sb, pl.ds(r0 + ph, n_rows, stride=ROW_STRIDE), :] = yv[sb, ph]


def _proj_kernel(layer, tiles_per_seq, x_ref, xp_ref, xn_ref, nw_ref, wt_hbm, cw_ref,
                 qa_ref, ka_ref, va_ref, yn_ref, z_ref, ab_ref, wb_ref, stage_ref, sem, *xe_refs):
    d_in = wt_hbm.shape[1]
    d_main = d_in // LANES * LANES
    step = pl.program_id(0)

    @pl.when(step == 0)
    def _():
        starts = list(range(0, d_main, PROJ_CAST_ROWS)) + [d_main]
        sizes = [min(PROJ_CAST_ROWS, d_main - c0) for c0 in starts[:-1]] + [d_in - d_main]
        slot = lambda k: k % PROJ_CAST_SLOTS
        copies = [pltpu.make_async_copy(wt_hbm.at[layer, pl.ds(c0, nr)], stage_ref.at[slot(k), pl.ds(0, nr)],
                                        sem.at[slot(k)]) for k, (c0, nr) in enumerate(zip(starts, sizes))]
        lane = lax.broadcasted_iota(jnp.int32, (D_MODEL, PROJ_CAST_ROWS), 1)
        for k in range(min(PROJ_CAST_SLOTS - 1, len(copies))):
            copies[k].start()
        for k, (c0, nr) in enumerate(zip(starts, sizes)):
            if k + PROJ_CAST_SLOTS - 1 < len(copies):
                copies[k + PROJ_CAST_SLOTS - 1].start()
            copies[k].wait()
            cols = stage_ref[slot(k)].T
            if nr < PROJ_CAST_ROWS:
                cols = jnp.where(lane < nr, cols, 0.0)
            width = min(PROJ_CAST_ROWS, wb_ref.shape[1] - c0)
            wb_ref[:, c0:c0 + width] = cols[:, :width].astype(BF16)

    o_q, o_k, o_v, o_g, o_z, o_ab = 0, ATTN_Q, ATTN_Q + ATTN_KV, ATTN_Q + 2 * ATTN_KV, \
        ATTN_Q + 2 * ATTN_KV + 3 * GDN_W, ATTN_Q + 2 * ATTN_KV + 4 * GDN_W
    n_slab = 3 * GDN_W // LANES

    def normed(x):
        ms = jnp.mean(x * x, axis=-1, keepdims=True)
        return (x * lax.rsqrt(ms + EPS) * nw_ref[...]).astype(BF16)

    n_sub = PROJ_TM // PROJ_SUB

    def to_slabs(xe_ref, y, row0):
        for sb in range(n_slab):
            xe_ref[sb, row0:row0 + y.shape[0], :] = y[:, sb * LANES:(sb + 1) * LANES]

    def put(k, y):
        to_slabs(xe_refs[k], y, SUBLANES)
        if k > 0:
            to_slabs(xe_refs[k - 1], y[:SUBLANES], SUBLANES + PROJ_SUB)
        if k + 1 < n_sub:
            to_slabs(xe_refs[k + 1], y[PROJ_SUB - SUBLANES:], 0)

    i = step % tiles_per_seq
    outs = ((qa_ref, o_q, ATTN_Q, ATTN_HEAD_DIM ** -0.5 * LOG2E), (ka_ref, o_k, ATTN_KV, None), (va_ref, o_v, ATTN_KV, None),
            (z_ref, o_z, GDN_W, None), (ab_ref, o_ab, LANES, None))
    for k in range(n_sub):
        rows = slice(k * PROJ_SUB, (k + 1) * PROJ_SUB)
        if k == 0:
            xg = normed(jnp.concatenate([xp_ref[...], xn_ref[...], x_ref[rows, :]], axis=0))
            yg = jnp.dot(xg, wb_ref[:, o_g:o_z], preferred_element_type=F32)
            to_slabs(xe_refs[0], jnp.where(i > 0, yg[:SUBLANES], 0.0), 0)
            to_slabs(xe_refs[n_sub - 1], jnp.where(i < tiles_per_seq - 1, yg[SUBLANES:2 * SUBLANES], 0.0),
                     SUBLANES + PROJ_SUB)
            put(0, yg[2 * SUBLANES:])
            xn = xg[2 * SUBLANES:]
        else:
            xn = normed(x_ref[rows, :])
            put(k, jnp.dot(xn, wb_ref[:, o_g:o_z], preferred_element_type=F32))
            _conv_silu_norm(xe_refs[k - 1], cw_ref, yn_ref, (k - 1) * PROJ_SUB, PROJ_SUB)
        for ref, c0, width, scale in outs:
            y = jnp.dot(xn, wb_ref[:, c0:c0 + width], preferred_element_type=F32)
            if scale is not None:
                y = y * scale
            ref[rows, :] = y.astype(ref.dtype)
    _conv_silu_norm(xe_refs[n_sub - 1], cw_ref, yn_ref, PROJ_TM - PROJ_SUB, PROJ_SUB)


def _proj(x2, norm_w, w_in, conv_w, layer, seq_len):
    n = x2.shape[0]
    d_in = w_in.shape[2]
    d_pad = d_in // LANES * LANES + LANES
    assert d_pad == ATTN_Q + 2 * ATTN_KV + 4 * GDN_W + LANES and d_in - (d_pad - LANES) == 2 * N_CHAIN
    assert seq_len % PROJ_TM == 0
    w_t = jnp.swapaxes(w_in, 1, 2)
    nh8 = PROJ_TM // SUBLANES
    row = lambda w: pl.BlockSpec((PROJ_TM, w), lambda i: (i, 0))
    return pl.pallas_call(
        functools.partial(_proj_kernel, layer, seq_len // PROJ_TM),
        grid=(n // PROJ_TM,),
        in_specs=[row(D_MODEL),
                  pl.BlockSpec((SUBLANES, D_MODEL), lambda i: (jnp.maximum(i * nh8 - 1, 0), 0)),
                  pl.BlockSpec((SUBLANES, D_MODEL), lambda i: (jnp.minimum((i + 1) * nh8, n // SUBLANES - 1), 0)),
                  pl.BlockSpec((1, D_MODEL), lambda i: (0, 0)),
                  pl.BlockSpec(memory_space=pl.ANY),
                  pl.BlockSpec((SUBLANES, 3 * GDN_W), lambda i: (0, 0))],
        out_specs=[row(ATTN_Q), row(ATTN_KV), row(ATTN_KV),
                   pl.BlockSpec((3 * GDN_W // LANES, PROJ_TM, LANES), lambda i: (0, i, 0)),
                   row(GDN_W), row(LANES)],
        out_shape=[jax.ShapeDtypeStruct((n, ATTN_Q), BF16),
                   jax.ShapeDtypeStruct((n, ATTN_KV), BF16),
                   jax.ShapeDtypeStruct((n, ATTN_KV), BF16),
                   jax.ShapeDtypeStruct((3 * GDN_W // LANES, n, LANES), F32),
                   jax.ShapeDtypeStruct((n, GDN_W), F32),
                   jax.ShapeDtypeStruct((n, LANES), F32)],
        scratch_shapes=[pltpu.VMEM((D_MODEL, d_pad), BF16),
                        pltpu.VMEM((PROJ_CAST_SLOTS, PROJ_CAST_ROWS, D_MODEL), F32),
                        pltpu.SemaphoreType.DMA((PROJ_CAST_SLOTS,))]
        + [pltpu.VMEM((3 * GDN_W // LANES, PROJ_SUB + 2 * SUBLANES, LANES), F32)] * (PROJ_TM // PROJ_SUB),
        compiler_params=pltpu.CompilerParams(dimension_semantics=("arbitrary",),
                                             vmem_limit_bytes=VMEM_LIMIT),
        name="proj",
    )(x2, x2, x2, norm_w, w_t, conv_w)


def _bias_kernel(relb_ref, bucket_ref, o_ref):
    bucket = bucket_ref[...]
    key = lax.broadcasted_iota(jnp.int32, (3 * BLOCK, BLOCK), 0)
    qry = lax.broadcasted_iota(jnp.int32, (3 * BLOCK, BLOCK), 1)
    in_window = jnp.abs(key - BLOCK - qry) <= WINDOW
    acc = [jnp.zeros((3 * BLOCK, BLOCK), F32) for _ in range(ATTN_HEADS)]
    for b in range(N_BUCKETS):
        hit = bucket == b
        acc = [jnp.where(hit, relb_ref[b, h], acc[h]) for h in range(ATTN_HEADS)]
    for h in range(ATTN_HEADS):
        o_ref[h // 2, :, (h % 2) * BLOCK:(h % 2 + 1) * BLOCK] = jnp.where(in_window, acc[h] * LOG2E, -1e30)


def _bias_band(rel_bias, bucket_t):
    shape = (ATTN_HEADS // 2, 3 * BLOCK, 2 * BLOCK)
    return pl.pallas_call(
        _bias_kernel,
        in_specs=[pl.BlockSpec(memory_space=pltpu.SMEM),
                  pl.BlockSpec((3 * BLOCK, BLOCK), lambda: (0, 0))],
        out_specs=pl.BlockSpec(shape, lambda: (0, 0, 0)),
        out_shape=jax.ShapeDtypeStruct(shape, F32),
        name="bias_band",
    )(rel_bias, bucket_t)


def _t5_buckets(rel):
    nb = N_BUCKETS // 2
    max_exact = nb // 2
    base = jnp.where(rel > 0, nb, 0)
    n = jnp.abs(rel)
    log_ratio = jnp.log(jnp.maximum(n, 1).astype(jnp.float32) / max_exact) / math.log(MAX_DISTANCE / max_exact)
    large = jnp.minimum(max_exact + (log_ratio * (nb - max_exact)).astype(jnp.int32), nb - 1)
    return base + jnp.where(n < max_exact, n, large)


def _attn_kernel(sink_ref, q_ref, kp_ref, kc_ref, kn_ref, vp_ref, vc_ref, vn_ref, bias_ref, o_ref):
    n = pl.program_id(1)
    last = pl.num_programs(1) - 1
    kband = jnp.concatenate([kp_ref[...], kc_ref[...], kn_ref[...]], axis=0)
    vband = jnp.concatenate([vp_ref[...], vc_ref[...], vn_ref[...]], axis=0)
    vband_t = vband.astype(F32).T.astype(BF16)
    key = lax.broadcasted_iota(jnp.int32, (3 * BLOCK, 1), 0)
    first_head = lax.broadcasted_iota(jnp.int32, (1, 2 * BLOCK), 1) < BLOCK
    head = lambda t, i: t[:, i * ATTN_HEAD_DIM:(i + 1) * ATTN_HEAD_DIM]
    n_pairs = ATTN_HEADS // 2
    kv_of = lambda pr: (2 * pr) // ATTN_GROUP
    units = [(j, pr) for j in range(ATTN_QB) for pr in range(n_pairs)]
    band_rows = lambda j: slice(j * BLOCK, (j + 3) * BLOCK)

    scores = {}
    for j, pr in units:
        qj = q_ref[j * BLOCK:(j + 1) * BLOCK, :]
        q2 = jnp.concatenate([head(qj, 2 * pr), head(qj, 2 * pr + 1)], axis=0)
        scores[j, pr] = _dot_nt(head(kband, kv_of(pr))[band_rows(j)], q2)
    probs, dens = {}, {}
    for j, pr in units:
        s = scores[j, pr] + bias_ref[pr]
        if j == 0:
            s = jnp.where((key < BLOCK) & (n == 0), -1e30, s)
        if j == ATTN_QB - 1:
            s = jnp.where((key >= 2 * BLOCK) & (n == last), -1e30, s)
        sink = jnp.where(first_head, sink_ref[2 * pr], sink_ref[2 * pr + 1]) * LOG2E
        m = jnp.maximum(jnp.max(s, axis=0, keepdims=True), sink)
        probs[j, pr] = jnp.exp2(s - m).astype(BF16)
        dens[j, pr] = jnp.exp2(sink - m)
    outs_t = {}
    ones_rows = jnp.ones((2 * SUBLANES, 3 * BLOCK), BF16)
    for j, pr in units:
        kv = kv_of(pr)
        v_t = vband_t[kv * ATTN_HEAD_DIM:(kv + 1) * ATTN_HEAD_DIM, band_rows(j)]
        pv = jnp.dot(jnp.concatenate([v_t, ones_rows], axis=0), probs[j, pr], preferred_element_type=F32)
        outs_t[j, pr] = pv[:ATTN_HEAD_DIM] / (pv[ATTN_HEAD_DIM:ATTN_HEAD_DIM + 1] + dens[j, pr])
    for j in range(ATTN_QB):
        o_t = jnp.concatenate([outs_t[j, pr][:, half * BLOCK:(half + 1) * BLOCK]
                               for pr in range(n_pairs) for half in range(2)], axis=0)
        o_ref[j * BLOCK:(j + 1) * BLOCK, :] = o_t.T.astype(o_ref.dtype)


def _attention(q_a, k_a, v_a, band, sink, batch):
    n_tok = q_a.shape[0]
    nb = n_tok // batch // BLOCK
    rows = ATTN_QB * BLOCK
    steps = nb // ATTN_QB
    kv_spec = lambda r, f: pl.BlockSpec((r, ATTN_KV), f)
    prev = lambda bi, n: (bi * nb + jnp.maximum(n * ATTN_QB - 1, 0), 0)
    cur = lambda bi, n: (bi * steps + n, 0)
    nxt = lambda bi, n: (bi * nb + jnp.minimum((n + 1) * ATTN_QB, nb - 1), 0)
    return pl.pallas_call(
        _attn_kernel,
        grid=(batch, steps),
        in_specs=[pl.BlockSpec(memory_space=pltpu.SMEM),
                  pl.BlockSpec((rows, ATTN_Q), cur),
                  kv_spec(BLOCK, prev), kv_spec(rows, cur), kv_spec(BLOCK, nxt),
                  kv_spec(BLOCK, prev), kv_spec(rows, cur), kv_spec(BLOCK, nxt),
                  pl.BlockSpec(band.shape, lambda bi, n: (0, 0, 0))],
        out_specs=pl.BlockSpec((rows, ATTN_Q), cur),
        out_shape=jax.ShapeDtypeStruct((n_tok, ATTN_Q), BF16),
        compiler_params=pltpu.CompilerParams(dimension_semantics=("arbitrary", "arbitrary"),
                                             vmem_limit_bytes=VMEM_LIMIT),
        name="attn",
    )(sink, q_a, k_a, k_a, k_a, v_a, v_a, v_a, band)


def _gprep_kernel(yn_ref, ab_ref, gp_ref, w_ref, u_ref, qd_ref, kdt_ref, qk_ref, egl_ref):
    t_len = PREP_T

    ab = ab_ref[0]
    sp_in = ab + gp_ref[1:2, :]
    softplus = jnp.maximum(sp_in, 0.0) + jnp.log1p(jnp.exp(-jnp.abs(sp_in)))
    g = (-jnp.exp(gp_ref[0:1, :]) * LOG2E) * softplus
    beta = jax.nn.sigmoid(ab)

    r_t = lax.broadcasted_iota(jnp.int32, (t_len, t_len), 0)
    c_t = lax.broadcasted_iota(jnp.int32, (t_len, t_len), 1)
    same = (r_t // CHUNK) == (c_t // CHUNK)
    lower = jnp.where(same & (r_t >= c_t), 1.0, 0.0).astype(BF16)
    upper = jnp.where(same & (r_t <= c_t), 1.0, 0.0).astype(BF16)
    g_t = g.T[:2 * SUBLANES]
    cs_row = (_dot01_right(g_t, upper), _dot01_right(g_t, lower))
    pad_rows = jnp.zeros((LANES - 2 * SUBLANES, t_len), F32)
    cs_col = tuple(jnp.concatenate([r, pad_rows], axis=0).T for r in cs_row)
    beta_t = beta.T[:3 * SUBLANES]

    qs = [yn_ref[h] for h in range(GDN_HEADS)]
    ks = [yn_ref[GDN_HEADS + h] for h in range(GDN_HEADS)]
    vs = [yn_ref[2 * GDN_HEADS + h] for h in range(GDN_HEADS)]
    kts = [kh.T for kh in ks]

    r_c = lax.broadcasted_iota(jnp.int32, (CHUNK, LANES), 0)
    lane = lax.broadcasted_iota(jnp.int32, (CHUNK, LANES), 1)
    is_fwd = lane < CHUNK
    c_c = lane % CHUNK
    eye = jnp.where(r_c == c_c, 1.0, 0.0).astype(F32)
    incl = (is_fwd & (r_c >= c_c)) | (~is_fwd & (r_c <= c_c))
    strict = (is_fwd & (r_c > c_c)) | (~is_fwd & (r_c < c_c))
    r_d = lax.broadcasted_iota(jnp.int32, (2 * CHUNK, LANES), 0)
    c_d = lax.broadcasted_iota(jnp.int32, (2 * CHUNK, LANES), 1)
    same_dir = (r_d // CHUNK) == (c_d // CHUNK)
    level_mask = lambda s_, r_, c_: ((r_ // (2 * s_)) == (c_ // (2 * s_))) & ((r_ // s_) != (c_ // s_))
    stack2 = lambda t: jnp.concatenate([t, t], axis=0)

    n_chunks = t_len // CHUNK
    rows = lambda c: slice(c * CHUNK, (c + 1) * CHUNK)
    pairs = [(c, h) for c in range(n_chunks) for h in range(GDN_HEADS)]
    qkk = {}
    for c, h in pairs:
        k16 = ks[h][rows(c)].astype(BF16)
        qk16 = jnp.concatenate([qs[h][rows(c)].astype(BF16), k16], axis=0)
        qkk[c, h] = _dot_nt(qk16, stack2(k16))

    bcast = lambda col: jnp.broadcast_to(col, (CHUNK, LANES))
    brow, grow, glast, a_mat, t_mat, kv16 = {}, {}, {}, {}, {}, {}
    for key in pairs:
        c, h = key
        g_full = {}
        for d in range(N_DIR):
            j = d * GDN_HEADS + h
            r_last = c * CHUNK + (CHUNK - 1 if d == 0 else 0)
            g_full[d] = bcast(cs_col[d][rows(c), j:j + 1])
            brow[key, d] = beta_t[SUBLANES + j:SUBLANES + j + 1, rows(c)]
            glast[key, d] = cs_col[d][r_last:r_last + 1, j:j + 1]
            grow[key, d] = cs_row[d][j:j + 1, rows(c)]
        gcol2 = jnp.where(is_fwd, g_full[0], g_full[1])
        grow2 = jnp.concatenate([grow[key, 0], grow[key, 1]], axis=1)
        brow2 = jnp.concatenate([brow[key, 0], brow[key, 1]], axis=1)
        decay = jnp.exp2(jnp.where(incl, gcol2 - grow2, -jnp.inf))
        a_mat[key] = jnp.where(strict, qkk[key][CHUNK:] * decay, 0.0) * brow2
        t_mat[key] = eye - jnp.where(level_mask(1, r_c, c_c), a_mat[key], 0.0)
        qk_ref[0, h, rows(c), :] = (qkk[key][:CHUNK] * decay * brow2).astype(qk_ref.dtype)
        eg = [jnp.exp2(g_full[d]) for d in range(N_DIR)]
        for d in range(N_DIR):
            qd_ref[0, d * GDN_HEADS + h, rows(c), :] = (qs[h][rows(c)] * eg[d]).astype(qd_ref.dtype)
        kv16[key] = jnp.concatenate(
            [jnp.concatenate([ks[h][rows(c)] * eg[d], vs[h][rows(c)]], axis=1) for d in range(N_DIR)],
            axis=0).astype(BF16)

    same_dir16 = jnp.where(same_dir, 1.0, 0.0).astype(BF16)
    block_diag = lambda t16: stack2(t16) * same_dir16
    a_bd = {key: block_diag(a_mat[key].astype(BF16)) for key in pairs}
    s = 2
    while s < CHUNK:
        lvl = level_mask(s, r_c, c_c)
        t16 = {key: t_mat[key].astype(BF16) for key in pairs}
        x_mat = {key: jnp.dot(t16[key], a_bd[key], preferred_element_type=F32) for key in pairs}
        y_mat = {key: jnp.dot(x_mat[key].astype(BF16), block_diag(t16[key]), preferred_element_type=F32)
                 for key in pairs}
        t_mat = {key: t_mat[key] - jnp.where(lvl, y_mat[key], 0.0) for key in pairs}
        s *= 2

    wu = {}
    for key in pairs:
        c, h = key
        t_sel = block_diag(t_mat[key].astype(BF16))
        wu[key] = jnp.dot(t_sel, kv16[key], preferred_element_type=F32)

    for c in range(n_chunks):
        egl_rows = []
        for d in range(N_DIR):
            for h in range(GDN_HEADS):
                key = (c, h)
                j = d * GDN_HEADS + h
                wu_d = wu[key][d * CHUNK:(d + 1) * CHUNK]
                w_ref[0, j, rows(c), :] = wu_d[:, :GDN_HEAD_DIM].astype(w_ref.dtype)
                u_ref[0, j, rows(c), :] = wu_d[:, GDN_HEAD_DIM:].astype(u_ref.dtype)
                egl_rows.append(jnp.broadcast_to(jnp.exp2(glast[key, d]), (1, LANES)))
        egl_ref[0, c] = jnp.concatenate(egl_rows, axis=0)
    for cp in range(n_chunks // 2):
        for d in range(N_DIR):
            for h in range(GDN_HEADS):
                fac = jnp.concatenate([brow[(c, h), d] * jnp.exp2(glast[(c, h), d] - grow[(c, h), d])
                                       for c in (2 * cp, 2 * cp + 1)], axis=1)
                kdt = kts[h][:, 2 * cp * CHUNK:(2 * cp + 2) * CHUNK] * fac
                kdt_ref[0, d * GDN_HEADS + h, cp] = kdt.astype(kdt_ref.dtype)


def _gdn_prep(yn, ab, gate_par):
    b, s, _ = ab.shape
    nt = s // PREP_T
    cpb = PREP_T // CHUNK
    nc = s // CHUNK
    chain = lambda last: pl.BlockSpec((1, N_CHAIN, PREP_T, last), lambda bi, i: (bi, 0, i, 0))
    return pl.pallas_call(
        _gprep_kernel,
        grid=(b, nt),
        in_specs=[pl.BlockSpec((3 * GDN_W // LANES, PREP_T, LANES), lambda bi, i: (0, bi * nt + i, 0)),
                  pl.BlockSpec((1, PREP_T, LANES), lambda bi, i: (bi, i, 0)),
                  pl.BlockSpec((SUBLANES, LANES), lambda bi, i: (0, 0))],
        out_specs=[chain(GDN_HEAD_DIM), chain(GDN_HEAD_DIM), chain(GDN_HEAD_DIM),
                   pl.BlockSpec((1, N_CHAIN, cpb // 2, GDN_HEAD_DIM, 2 * CHUNK), lambda bi, i: (bi, 0, i, 0, 0)),
                   pl.BlockSpec((1, GDN_HEADS, PREP_T, LANES), lambda bi, i: (bi, 0, i, 0)),
                   pl.BlockSpec((1, cpb, N_CHAIN, LANES), lambda bi, i: (bi, i, 0, 0))],
        out_shape=[jax.ShapeDtypeStruct((b, N_CHAIN, s, GDN_HEAD_DIM), BF16),
                   jax.ShapeDtypeStruct((b, N_CHAIN, s, GDN_HEAD_DIM), BF16),
                   jax.ShapeDtypeStruct((b, N_CHAIN, s, GDN_HEAD_DIM), BF16),
                   jax.ShapeDtypeStruct((b, N_CHAIN, nc // 2, GDN_HEAD_DIM, 2 * CHUNK), BF16),
                   jax.ShapeDtypeStruct((b, GDN_HEADS, s, LANES), BF16),
                   jax.ShapeDtypeStruct((b, nc, N_CHAIN, LANES), F32)],
        compiler_params=pltpu.CompilerParams(dimension_semantics=("arbitrary", "arbitrary"),
                                             vmem_limit_bytes=VMEM_LIMIT),
        name="gdn_prep",
    )(yn, ab, gate_par)


def _gscan_kernel(egl_ref, wf_ref, wb_ref, uf_ref, ub_ref, qf_ref, qb_ref, kf_ref, kb_ref,
                  pf_ref, pb_ref, of_ref, ob_ref, state_ref):
    t = pl.program_id(0)
    n_steps = pl.num_programs(0)
    nc = n_steps * SCAN_CHUNKS
    n_batch = wf_ref.shape[0]

    @pl.when(t == 0)
    def _():
        state_ref[...] = jnp.zeros_like(state_ref)

    dirs = ((wf_ref, uf_ref, qf_ref, kf_ref, pf_ref, of_ref), (wb_ref, ub_ref, qb_ref, kb_ref, pb_ref, ob_ref))
    chains = [(bi, d, h) for bi in range(n_batch) for d in range(N_DIR) for h in range(GDN_HEADS)]
    slot = lambda bi, d, h: (bi * N_DIR + d) * GDN_HEADS + h
    st = {key: state_ref[slot(*key)] for key in chains}

    for sub in range(SCAN_CHUNKS):
        local = (sub, SCAN_CHUNKS - 1 - sub)
        rows = [slice(c * CHUNK, (c + 1) * CHUNK) for c in local]
        chunk = (t * SCAN_CHUNKS + sub, nc - 1 - (t * SCAN_CHUNKS + sub))
        r = {}
        for key in chains:
            bi, d, h = key
            w_ref, _, q_ref = dirs[d][:3]
            wq = jnp.concatenate([w_ref[bi, h, rows[d]], q_ref[bi, h, rows[d]]], axis=0)
            r[key] = _dot(wq, st[key])
        v_pad, intra = {}, {}
        for key in chains:
            bi, d, h = key
            u_ref, p_ref = dirs[d][1], dirs[d][4]
            v_new = (u_ref[bi, h, rows[d]].astype(F32) - r[key][:CHUNK]).astype(BF16)
            zeros = jnp.zeros_like(v_new)
            v_pad[key] = (jnp.concatenate([v_new, zeros], axis=0), jnp.concatenate([zeros, v_new], axis=0))
            intra[key] = jnp.dot(p_ref[bi, h, rows[d]], v_pad[key][d], preferred_element_type=F32)
        for key in chains:
            bi, d, h = key
            k_ref = dirs[d][3]
            egl = egl_ref[(bi * nc + chunk[d]) * N_CHAIN + d * GDN_HEADS + h]
            st[key] = st[key] * egl + jnp.dot(k_ref[bi, h, local[d] // 2], v_pad[key][local[d] % 2],
                                              preferred_element_type=F32)
        for bi in range(n_batch):
            for d in range(N_DIR):
                o_ref = dirs[d][5]
                o_ref[bi, rows[d], :] = jnp.concatenate(
                    [r[bi, d, h][CHUNK:] + intra[bi, d, h] for h in range(GDN_HEADS)], axis=-1).astype(o_ref.dtype)

    for key in chains:
        state_ref[slot(*key)] = st[key]


def _gdn_scan(egl, w, u, qd, kdt, qk):
    b, _, s, _ = w.shape
    rows = SCAN_CHUNKS * CHUNK
    n_steps = s // rows
    fwd_i = lambda t: t
    bwd_i = lambda t: n_steps - 1 - t
    chain = lambda d, at, last: pl.BlockSpec((b, GDN_HEADS, rows, last), lambda t: (0, d, at(t), 0))
    kspec = lambda d, at: pl.BlockSpec((b, GDN_HEADS, SCAN_CHUNKS // 2, GDN_HEAD_DIM, 2 * CHUNK),
                                       lambda t: (0, d, at(t), 0, 0))
    dk = GDN_HEAD_DIM
    return pl.pallas_call(
        _gscan_kernel,
        grid=(n_steps,),
        in_specs=[pl.BlockSpec(memory_space=pltpu.SMEM),
                  chain(0, fwd_i, dk), chain(1, bwd_i, dk), chain(0, fwd_i, dk), chain(1, bwd_i, dk),
                  chain(0, fwd_i, dk), chain(1, bwd_i, dk), kspec(0, fwd_i), kspec(1, bwd_i),
                  chain(0, fwd_i, LANES), chain(0, bwd_i, LANES)],
        out_specs=[pl.BlockSpec((b, rows, GDN_W), lambda t: (0, fwd_i(t), 0)),
                   pl.BlockSpec((b, rows, GDN_W), lambda t: (0, bwd_i(t), 0))],
        out_shape=[jax.ShapeDtypeStruct((b, s, GDN_W), BF16),
                   jax.ShapeDtypeStruct((b, s, GDN_W), BF16)],
        scratch_shapes=[pltpu.VMEM((b * N_CHAIN, GDN_HEAD_DIM, GDN_HEAD_DIM), F32)],
        compiler_params=pltpu.CompilerParams(dimension_semantics=("arbitrary",),
                                             vmem_limit_bytes=VMEM_LIMIT),
        name="gdn_scan",
    )(egl, w, w, u, u, qd, qd, kdt, kdt, qk, qk)


def _cast_weights_once(layer, copies):
    jobs, used = [], {}
    for src, dst, stage, sem, chunk_rows in copies:
        for r0 in range(0, src.shape[1], chunk_rows):
            slot = used.get(id(stage), 0) % 2
            used[id(stage)] = used.get(id(stage), 0) + 1
            rows = pl.ds(r0, chunk_rows)
            jobs.append((pltpu.make_async_copy(src.at[layer, rows], stage.at[slot], sem.at[slot]),
                         stage, slot, dst, rows))
    jobs[0][0].start()
    for k, (copy, stage, slot, dst, rows) in enumerate(jobs):
        if k + 1 < len(jobs):
            jobs[k + 1][0].start()
        copy.wait()
        dst[rows, :] = stage[slot].astype(BF16)


def _ffn_kernel(layer, x_ref, attn_ref, of_ref, ob_ref, z_ref, gnw_ref, wo_hbm, fnw_ref, w1_hbm, w2_hbm, onw_ref,
                o_ref, wo_ref, w1_ref, w2_ref, stage_wide, stage_tall, sem_wide, sem_tall):
    @pl.when(pl.program_id(0) == 0)
    def _():
        _cast_weights_once(layer, [(w1_hbm, w1_ref, stage_wide, sem_wide, FFN_CAST_ROWS_WIDE),
                                   (w2_hbm, w2_ref, stage_tall, sem_tall, FFN_CAST_ROWS_TALL),
                                   (wo_hbm, wo_ref, stage_tall, sem_tall, FFN_CAST_ROWS_TALL)])

    subs = [slice(r0, r0 + FFN_SUB) for r0 in range(0, FFN_TM, FFN_SUB)]
    rms = lambda t: t * lax.rsqrt(jnp.mean(t * t, axis=-1, keepdims=True) + EPS)
    hres, hn, act, acc = {}, {}, {}, {}
    for r in subs:
        o = of_ref[r, :].astype(F32) + ob_ref[r, :].astype(F32)
        heads = [rms(o[:, h * GDN_HEAD_DIM:(h + 1) * GDN_HEAD_DIM]) * gnw_ref[...] for h in range(GDN_HEADS)]
        z = z_ref[r, :]
        gdn = jnp.concatenate(heads, axis=-1) * (z * jax.nn.sigmoid(z))
        hres[r.start] = (x_ref[r, :] + jnp.dot(attn_ref[r, :], wo_ref[:ATTN_Q, :], preferred_element_type=F32)
                         + _dot(gdn, wo_ref[ATTN_Q:, :]))
    for r in subs:
        hn[r.start] = (rms(hres[r.start]) * fnw_ref[...]).astype(BF16)
        act[r.start] = jnp.dot(hn[r.start], w1_ref[...], preferred_element_type=F32)
    for r in subs:
        a = jnp.square(jnp.maximum(act[r.start], 0.0)).astype(BF16)
        acc[r.start] = hres[r.start] + jnp.dot(a, w2_ref[...], preferred_element_type=F32)
    for r in subs:
        o_ref[r, :] = rms(acc[r.start]) * onw_ref[...]


def _out_ffn(x2, attn, o_f, o_b, z, gnw, wo, fnw, w1, w2, onw, layer):
    n = x2.shape[0]
    assert wo.shape[1:] == (D_MODEL, D_MODEL) and w1.shape[1:] == (D_MODEL, D_FF) and w2.shape[1:] == (D_FF, D_MODEL)
    row = lambda w: pl.BlockSpec((FFN_TM, w), lambda i: (i, 0))
    full = lambda a: pl.BlockSpec(a.shape, lambda i: (0, 0))
    hbm = pl.BlockSpec(memory_space=pl.ANY)
    return pl.pallas_call(
        functools.partial(_ffn_kernel, layer),
        grid=(n // FFN_TM,),
        in_specs=[row(D_MODEL), row(ATTN_Q), row(GDN_W), row(GDN_W), row(GDN_W),
                  full(gnw), hbm, full(fnw), hbm, hbm, full(onw)],
        out_specs=row(D_MODEL),
        out_shape=jax.ShapeDtypeStruct((n, D_MODEL), F32),
        scratch_shapes=[pltpu.VMEM((D_MODEL, D_MODEL), BF16),
                        pltpu.VMEM((D_MODEL, D_FF), BF16),
                        pltpu.VMEM((D_FF, D_MODEL), BF16),
                        pltpu.VMEM((2, FFN_CAST_ROWS_WIDE, D_FF), F32),
                        pltpu.VMEM((2, FFN_CAST_ROWS_TALL, D_MODEL), F32),
                        pltpu.SemaphoreType.DMA((2,)),
                        pltpu.SemaphoreType.DMA((2,))],
        compiler_params=pltpu.CompilerParams(dimension_semantics=("arbitrary",),
                                             vmem_limit_bytes=VMEM_LIMIT),
        name="out_ffn",
    )(x2, attn, o_f, o_b, z, gnw, wo, fnw, w1, w2, onw)


def _layer(h, band, norm_mix_w, w_in, layer, attn_sink, conv_w, gdn_a_log, gdn_dt_bias, gdn_norm_w,
           w_out, norm_ffn_w, w_ffn_in, w_ffn_out, out_norm_w):
    b, s, _ = h.shape
    n = b * s
    x2 = h.reshape(n, D_MODEL)
    conv_pad = jnp.zeros((SUBLANES, 3 * GDN_W), F32).at[:CONV_K].set(conv_w)
    q_a, k_a, v_a, yn, z_g, ab = _proj(x2, norm_mix_w.reshape(1, D_MODEL), w_in, conv_pad, layer, s)

    attn = _attention(q_a, k_a, v_a, band, attn_sink, b)

    gate_par = jnp.zeros((SUBLANES, LANES), F32)
    gate_par = gate_par.at[0, :N_CHAIN].set(gdn_a_log.reshape(-1)).at[1, :N_CHAIN].set(gdn_dt_bias.reshape(-1))
    w_c, u_c, q_dec, k_dec_t, qk, egl = _gdn_prep(yn, ab.reshape(b, s, LANES), gate_par)
    o_f, o_b = _gdn_scan(egl[..., 0].reshape(-1), w_c, u_c, q_dec, k_dec_t, qk)

    out = _out_ffn(x2, attn, o_f.reshape(n, GDN_W), o_b.reshape(n, GDN_W), z_g,
                   gdn_norm_w.reshape(1, GDN_HEAD_DIM), w_out, norm_ffn_w.reshape(1, D_MODEL),
                   w_ffn_in, w_ffn_out, out_norm_w.reshape(1, D_MODEL), layer)
    return out.reshape(b, s, D_MODEL)


def kernel(x, norm_mix_w, w_in, rel_bias, attn_sink, conv_w, gdn_a_log, gdn_dt_bias, gdn_norm_w, w_out,
           norm_ffn_w, w_ffn_in, w_ffn_out, norm_final_w):
    depth = w_in.shape[0]
    assert depth == 1, "the fused output kernel applies the final norm after the single trunk layer"
    rel = (np.arange(3 * BLOCK)[None, :] - BLOCK) - np.arange(BLOCK)[:, None]
    bucket = _t5_buckets(jnp.asarray(rel, dtype=jnp.int32))
    band = _bias_band(rel_bias, bucket.T)
    return _layer(x, band, norm_mix_w[0], w_in, 0, attn_sink[0], conv_w[0], gdn_a_log[0], gdn_dt_bias[0],
                  gdn_norm_w[0], w_out, norm_ffn_w[0], w_ffn_in, w_ffn_out, norm_final_w)
```

```python
import functools
import math

import jax
import jax.numpy as jnp
import numpy as np
from jax import lax
from jax.experimental import pallas as pl
from jax.experimental.pallas import tpu as pltpu

F32 = jnp.float32
BF16 = jnp.bfloat16

D_MODEL = 1024
ATTN_HEADS = 8
ATTN_KV_HEADS = 2
ATTN_HEAD_DIM = 64
ATTN_GROUP = ATTN_HEADS // ATTN_KV_HEADS
WINDOW = 128
BLOCK = 128
N_BUCKETS = 32
MAX_DISTANCE = 128
GDN_HEADS = 4
GDN_HEAD_DIM = 128
CONV_K = 5
CHUNK = 64
N_DIR = 2
N_CHAIN = N_DIR * GDN_HEADS
D_FF = 4 * D_MODEL
EPS = 1e-6
LOG2E = math.log2(math.e)
ATTN_Q = ATTN_HEADS * ATTN_HEAD_DIM
ATTN_KV = ATTN_KV_HEADS * ATTN_HEAD_DIM
GDN_W = GDN_HEADS * GDN_HEAD_DIM
LANES = 128
SUBLANES = 8
BF16_SUBLANES = 16
VMEM_LIMIT = 56 * 1024 * 1024

PROJ_TM = 1024
PROJ_SUB = 256
PROJ_CAST_ROWS = 256
PROJ_CAST_SLOTS = 4
ATTN_QB = 8
PREP_T = 1024
ROW_STRIDE = 4
SCAN_CHUNKS = 8
FFN_TM = 512
FFN_SUB = 256


def _dot(a, b):
    return jnp.dot(a.astype(BF16), b.astype(BF16), preferred_element_type=F32)


def _dot_nt(a, b):
    return lax.dot_general(a.astype(BF16), b.astype(BF16), (((1,), (1,)), ((), ())),
                           preferred_element_type=F32)


def _split3(x):
    hi = x.astype(BF16)
    r1 = x - hi.astype(F32)
    mid = r1.astype(BF16)
    lo = (r1 - mid.astype(F32)).astype(BF16)
    return hi, mid, lo


def _dot01_right(x, m01):
    hi, mid, lo = _split3(x)
    d = lambda p: jnp.dot(p, m01, preferred_element_type=F32)
    return d(hi) + d(mid) + d(lo)


def _conv_silu_norm(xe_ref, cw_ref, yn_ref, r0, n_out):
    halo = CONV_K // 2
    n_rows = n_out // ROW_STRIDE
    n_slab = 3 * GDN_W // LANES
    units = [(sb, ph) for sb in range(n_slab) for ph in range(ROW_STRIDE)]
    half_taps = [cw_ref[j:j + 1, :] * 0.5 for j in range(CONV_K)]
    yv = {}
    for sb, ph in units:
        lanes = slice(sb * LANES, (sb + 1) * LANES)
        acc = None
        for j in range(CONV_K):
            win = xe_ref[sb, pl.ds(SUBLANES - halo + j + ph, n_rows, stride=ROW_STRIDE), :]
            term = half_taps[j][:, lanes] * win
            acc = term if acc is None else acc + term
        yv[sb, ph] = acc
    for key in units:
        yv[key] = yv[key] + yv[key] * jnp.tanh(yv[key])
    for sb, ph in units:
        if sb < 2 * GDN_HEADS:
            scale = lax.rsqrt(jnp.sum(yv[sb, ph] * yv[sb, ph], axis=-1, keepdims=True) + EPS)
            if sb < GDN_HEADS:
                scale = scale * (GDN_HEAD_DIM ** -0.5)
            yv[sb, ph] = yv[sb, ph] * scale
    for sb, ph in units:
        yn_ref[sb, pl.ds(r0 + ph, n_rows, stride=ROW_STRIDE), :] = yv[sb, ph]


def _proj_kernel(layer, tiles_per_seq, x_ref, xp_ref, xn_ref, nw_ref, wt_hbm, cw_ref,
                 qa_ref, ka_ref, va_ref, yn_ref, z_ref, ab_ref, wb_ref, stage_ref, sem, *xe_refs):
    d_in = wt_hbm.shape[1]
    d_main = d_in // LANES * LANES
    step = pl.program_id(0)

    @pl.when(step == 0)
    def _():
        starts = list(range(0, d_main, PROJ_CAST_ROWS)) + [d_main]
        sizes = [min(PROJ_CAST_ROWS, d_main - c0) for c0 in starts[:-1]] + [d_in - d_main]
        slot = lambda k: k % PROJ_CAST_SLOTS
        copies = [pltpu.make_async_copy(wt_hbm.at[layer, pl.ds(c0, nr)], stage_ref.at[slot(k), pl.ds(0, nr)],
                                        sem.at[slot(k)]) for k, (c0, nr) in enumerate(zip(starts, sizes))]
        lane = lax.broadcasted_iota(jnp.int32, (D_MODEL, PROJ_CAST_ROWS), 1)
        for k in range(min(PROJ_CAST_SLOTS - 1, len(copies))):
            copies[k].start()
        for k, (c0, nr) in enumerate(zip(starts, sizes)):
            if k + PROJ_CAST_SLOTS - 1 < len(copies):
                copies[k + PROJ_CAST_SLOTS - 1].start()
            copies[k].wait()
            cols = stage_ref[slot(k)].T
            if nr < PROJ_CAST_ROWS:
                cols = jnp.where(lane < nr, cols, 0.0)
            width = min(PROJ_CAST_ROWS, wb_ref.shape[1] - c0)
            wb_ref[:, c0:c0 + width] = cols[:, :width].astype(BF16)

    o_q, o_k, o_v, o_g, o_z, o_ab = 0, ATTN_Q, ATTN_Q + ATTN_KV, ATTN_Q + 2 * ATTN_KV, \
        ATTN_Q + 2 * ATTN_KV + 3 * GDN_W, ATTN_Q + 2 * ATTN_KV + 4 * GDN_W
    n_slab = 3 * GDN_W // LANES

    def normed(x):
        ms = jnp.mean(x * x, axis=-1, keepdims=True)
        return (x * lax.rsqrt(ms + EPS) * nw_ref[...]).astype(BF16)

    n_sub = PROJ_TM // PROJ_SUB

    def to_slabs(xe_ref, y, row0):
        for sb in range(n_slab):
            xe_ref[sb, row0:row0 + y.shape[0], :] = y[:, sb * LANES:(sb + 1) * LANES]

    def put(k, y):
        to_slabs(xe_refs[k], y, SUBLANES)
        if k > 0:
            to_slabs(xe_refs[k - 1], y[:SUBLANES], SUBLANES + PROJ_SUB)
        if k + 1 < n_sub:
            to_slabs(xe_refs[k + 1], y[PROJ_SUB - SUBLANES:], 0)

    i = step % tiles_per_seq
    outs = ((qa_ref, o_q, ATTN_Q, ATTN_HEAD_DIM ** -0.5 * LOG2E), (ka_ref, o_k, ATTN_KV, None), (va_ref, o_v, ATTN_KV, None),
            (z_ref, o_z, GDN_W, None), (ab_ref, o_ab, LANES, None))
    for k in range(n_sub):
        rows = slice(k * PROJ_SUB, (k + 1) * PROJ_SUB)
        if k == 0:
            xg = normed(jnp.concatenate([xp_ref[...], xn_ref[...], x_ref[rows, :]], axis=0))
            yg = jnp.dot(xg, wb_ref[:, o_g:o_z], preferred_element_type=F32)
            to_slabs(xe_refs[0], jnp.where(i > 0, yg[:SUBLANES], 0.0), 0)
            to_slabs(xe_refs[n_sub - 1], jnp.where(i < tiles_per_seq - 1, yg[SUBLANES:2 * SUBLANES], 0.0),
                     SUBLANES + PROJ_SUB)
            put(0, yg[2 * SUBLANES:])
            xn = xg[2 * SUBLANES:]
        else:
            xn = normed(x_ref[rows, :])
            put(k, jnp.dot(xn, wb_ref[:, o_g:o_z], preferred_element_type=F32))
            _conv_silu_norm(xe_refs[k - 1], cw_ref, yn_ref, (k - 1) * PROJ_SUB, PROJ_SUB)
        for ref, c0, width, scale in outs:
            y = jnp.dot(xn, wb_ref[:, c0:c0 + width], preferred_element_type=F32)
            if scale is not None:
                y = y * scale
            ref[rows, :] = y.astype(ref.dtype)
    _conv_silu_norm(xe_refs[n_sub - 1], cw_ref, yn_ref, PROJ_TM - PROJ_SUB, PROJ_SUB)


def _proj(x2, norm_w, w_in, conv_w, layer, seq_len):
    n = x2.shape[0]
    d_in = w_in.shape[2]
    d_pad = d_in // LANES * LANES + LANES
    assert d_pad == ATTN_Q + 2 * ATTN_KV + 4 * GDN_W + LANES and d_in - (d_pad - LANES) == 2 * N_CHAIN
    assert seq_len % PROJ_TM == 0
    w_t = jnp.swapaxes(w_in, 1, 2)
    nh8 = PROJ_TM // SUBLANES
    row = lambda w: pl.BlockSpec((PROJ_TM, w), lambda i: (i, 0))
    return pl.pallas_call(
        functools.partial(_proj_kernel, layer, seq_len // PROJ_TM),
        grid=(n // PROJ_TM,),
        in_specs=[row(D_MODEL),
                  pl.BlockSpec((SUBLANES, D_MODEL), lambda i: (jnp.maximum(i * nh8 - 1, 0), 0)),
                  pl.BlockSpec((SUBLANES, D_MODEL), lambda i: (jnp.minimum((i + 1) * nh8, n // SUBLANES - 1), 0)),
                  pl.BlockSpec((1, D_MODEL), lambda i: (0, 0)),
                  pl.BlockSpec(memory_space=pl.ANY),
                  pl.BlockSpec((SUBLANES, 3 * GDN_W), lambda i: (0, 0))],
        out_specs=[row(ATTN_Q), row(ATTN_KV), row(ATTN_KV),
                   pl.BlockSpec((3 * GDN_W // LANES, PROJ_TM, LANES), lambda i: (0, i, 0)),
                   row(GDN_W), row(LANES)],
        out_shape=[jax.ShapeDtypeStruct((n, ATTN_Q), BF16),
                   jax.ShapeDtypeStruct((n, ATTN_KV), BF16),
                   jax.ShapeDtypeStruct((n, ATTN_KV), BF16),
                   jax.ShapeDtypeStruct((3 * GDN_W // LANES, n, LANES), F32),
                   jax.ShapeDtypeStruct((n, GDN_W), F32),
                   jax.ShapeDtypeStruct((n, LANES), F32)],
        scratch_shapes=[pltpu.VMEM((D_MODEL, d_pad), BF16),
                        pltpu.VMEM((PROJ_CAST_SLOTS, PROJ_CAST_ROWS, D_MODEL), F32),
                        pltpu.SemaphoreType.DMA((PROJ_CAST_SLOTS,))]
        + [pltpu.VMEM((3 * GDN_W // LANES, PROJ_SUB + 2 * SUBLANES, LANES), F32)] * (PROJ_TM // PROJ_SUB),
        compiler_params=pltpu.CompilerParams(dimension_semantics=("arbitrary",),
                                             vmem_limit_bytes=VMEM_LIMIT),
        name="proj",
    )(x2, x2, x2, norm_w, w_t, conv_w)


def _bias_kernel(relb_ref, bucket_ref, o_ref):
    bucket = bucket_ref[...]
    key = lax.broadcasted_iota(jnp.int32, (3 * BLOCK, BLOCK), 0)
    qry = lax.broadcasted_iota(jnp.int32, (3 * BLOCK, BLOCK), 1)
    in_window = jnp.abs(key - BLOCK - qry) <= WINDOW
    acc = [jnp.zeros((3 * BLOCK, BLOCK), F32) for _ in range(ATTN_HEADS)]
    for b in range(N_BUCKETS):
        hit = bucket == b
        acc = [jnp.where(hit, relb_ref[b, h], acc[h]) for h in range(ATTN_HEADS)]
    for h in range(ATTN_HEADS):
        o_ref[h // 2, :, (h % 2) * BLOCK:(h % 2 + 1) * BLOCK] = jnp.where(in_window, acc[h] * LOG2E, -1e30)


def _bias_band(rel_bias, bucket_t):
    shape = (ATTN_HEADS // 2, 3 * BLOCK, 2 * BLOCK)
    return pl.pallas_call(
        _bias_kernel,
        in_specs=[pl.BlockSpec(memory_space=pltpu.SMEM),
                  pl.BlockSpec((3 * BLOCK, BLOCK), lambda: (0, 0))],
        out_specs=pl.BlockSpec(shape, lambda: (0, 0, 0)),
        out_shape=jax.ShapeDtypeStruct(shape, F32),
        name="bias_band",
    )(rel_bias, bucket_t)


def _t5_buckets(rel):
    nb = N_BUCKETS // 2
    max_exact = nb // 2
    base = jnp.where(rel > 0, nb, 0)
    n = jnp.abs(rel)
    log_ratio = jnp.log(jnp.maximum(n, 1).astype(jnp.float32) / max_exact) / math.log(MAX_DISTANCE / max_exact)
    large = jnp.minimum(max_exact + (log_ratio * (nb - max_exact)).astype(jnp.int32), nb - 1)
    return base + jnp.where(n < max_exact, n, large)


def _attn_kernel(sink_ref, q_ref, kp_ref, kc_ref, kn_ref, vp_ref, vc_ref, vn_ref, bias_ref, wo_ref, w1_ref, w2_ref,
                 o_ref, wo_bf_ref, w1_bf_ref, w2_bf_ref):
    for src, dst in ((wo_ref, wo_bf_ref), (w1_ref, w1_bf_ref), (w2_ref, w2_bf_ref)):
        dst[...] = src[0].astype(dst.dtype)
    n = pl.program_id(1)
    last = pl.num_programs(1) - 1
    kband = jnp.concatenate([kp_ref[...], kc_ref[...], kn_ref[...]], axis=0)
    vband = jnp.concatenate([vp_ref[...], vc_ref[...], vn_ref[...]], axis=0)
    vband_t = vband.astype(F32).T.astype(BF16)
    key = lax.broadcasted_iota(jnp.int32, (3 * BLOCK, 1), 0)
    first_head = lax.broadcasted_iota(jnp.int32, (1, 2 * BLOCK), 1) < BLOCK
    head = lambda t, i: t[:, i * ATTN_HEAD_DIM:(i + 1) * ATTN_HEAD_DIM]
    n_pairs = ATTN_HEADS // 2
    kv_of = lambda pr: (2 * pr) // ATTN_GROUP
    units = [(j, pr) for j in range(ATTN_QB) for pr in range(n_pairs)]
    band_rows = lambda j: slice(j * BLOCK, (j + 3) * BLOCK)

    scores = {}
    for j, pr in units:
        qj = q_ref[j * BLOCK:(j + 1) * BLOCK, :]
        q2 = jnp.concatenate([head(qj, 2 * pr), head(qj, 2 * pr + 1)], axis=0)
        scores[j, pr] = _dot_nt(head(kband, kv_of(pr))[band_rows(j)], q2)
    probs, dens = {}, {}
    for j, pr in units:
        s = scores[j, pr] + bias_ref[pr]
        if j == 0:
            s = jnp.where((key < BLOCK) & (n == 0), -1e30, s)
        if j == ATTN_QB - 1:
            s = jnp.where((key >= 2 * BLOCK) & (n == last), -1e30, s)
        sink = jnp.where(first_head, sink_ref[2 * pr], sink_ref[2 * pr + 1]) * LOG2E
        m = jnp.maximum(jnp.max(s, axis=0, keepdims=True), sink)
        probs[j, pr] = jnp.exp2(s - m).astype(BF16)
        dens[j, pr] = jnp.exp2(sink - m)
    outs_t = {}
    ones_rows = jnp.ones((2 * SUBLANES, 3 * BLOCK), BF16)
    for j, pr in units:
        kv = kv_of(pr)
        v_t = vband_t[kv * ATTN_HEAD_DIM:(kv + 1) * ATTN_HEAD_DIM, band_rows(j)]
        pv = jnp.dot(jnp.concatenate([v_t, ones_rows], axis=0), probs[j, pr], preferred_element_type=F32)
        outs_t[j, pr] = pv[:ATTN_HEAD_DIM] / (pv[ATTN_HEAD_DIM:ATTN_HEAD_DIM + 1] + dens[j, pr])
    for j in range(ATTN_QB):
        o_t = jnp.concatenate([outs_t[j, pr][:, half * BLOCK:(half + 1) * BLOCK]
                               for pr in range(n_pairs) for half in range(2)], axis=0)
        o_ref[j * BLOCK:(j + 1) * BLOCK, :] = o_t.T.astype(o_ref.dtype)


def _attention(q_a, k_a, v_a, band, sink, batch, layer, weights):
    n_tok = q_a.shape[0]
    nb = n_tok // batch // BLOCK
    rows = ATTN_QB * BLOCK
    steps = nb // ATTN_QB
    kv_spec = lambda r, f: pl.BlockSpec((r, ATTN_KV), f)
    prev = lambda bi, n: (bi * nb + jnp.maximum(n * ATTN_QB - 1, 0), 0)
    cur = lambda bi, n: (bi * steps + n, 0)
    nxt = lambda bi, n: (bi * nb + jnp.minimum((n + 1) * ATTN_QB, nb - 1), 0)
    n_steps = batch * steps
    assert all(w.shape[1] % (n_steps * BF16_SUBLANES) == 0 for w in weights)
    w_rows = [w.shape[1] // n_steps for w in weights]
    w_in_specs = [pl.BlockSpec((1, r, w.shape[2]), lambda bi, n: (layer, bi * steps + n, 0))
                  for r, w in zip(w_rows, weights)]
    w_out_specs = [pl.BlockSpec((r, w.shape[2]), lambda bi, n: (bi * steps + n, 0)) for r, w in zip(w_rows, weights)]
    return pl.pallas_call(
        _attn_kernel,
        grid=(batch, steps),
        in_specs=[pl.BlockSpec(memory_space=pltpu.SMEM),
                  pl.BlockSpec((rows, ATTN_Q), cur),
                  kv_spec(BLOCK, prev), kv_spec(rows, cur), kv_spec(BLOCK, nxt),
                  kv_spec(BLOCK, prev), kv_spec(rows, cur), kv_spec(BLOCK, nxt),
                  pl.BlockSpec(band.shape, lambda bi, n: (0, 0, 0))] + w_in_specs,
        out_specs=[pl.BlockSpec((rows, ATTN_Q), cur)] + w_out_specs,
        out_shape=[jax.ShapeDtypeStruct((n_tok, ATTN_Q), BF16)]
                  + [jax.ShapeDtypeStruct(w.shape[1:], BF16) for w in weights],
        compiler_params=pltpu.CompilerParams(dimension_semantics=("arbitrary", "arbitrary"),
                                             vmem_limit_bytes=VMEM_LIMIT),
        name="attn",
    )(sink, q_a, k_a, k_a, k_a, v_a, v_a, v_a, band, *weights)


def _gprep_kernel(yn_ref, ab_ref, gp_ref, w_ref, u_ref, qd_ref, kdt_ref, qk_ref, egl_ref):
    t_len = PREP_T

    ab = ab_ref[0]
    sp_in = ab + gp_ref[1:2, :]
    softplus = jnp.maximum(sp_in, 0.0) + jnp.log1p(jnp.exp(-jnp.abs(sp_in)))
    g = (-jnp.exp(gp_ref[0:1, :]) * LOG2E) * softplus
    beta = jax.nn.sigmoid(ab)

    r_t = lax.broadcasted_iota(jnp.int32, (t_len, t_len), 0)
    c_t = lax.broadcasted_iota(jnp.int32, (t_len, t_len), 1)
    same = (r_t // CHUNK) == (c_t // CHUNK)
    lower = jnp.where(same & (r_t >= c_t), 1.0, 0.0).astype(BF16)
    upper = jnp.where(same & (r_t <= c_t), 1.0, 0.0).astype(BF16)
    g_t = g.T[:2 * SUBLANES]
    cs_row = (_dot01_right(g_t, upper), _dot01_right(g_t, lower))
    pad_rows = jnp.zeros((LANES - 2 * SUBLANES, t_len), F32)
    cs_col = tuple(jnp.concatenate([r, pad_rows], axis=0).T for r in cs_row)
    beta_t = beta.T[:3 * SUBLANES]

    qs = [yn_ref[h] for h in range(GDN_HEADS)]
    ks = [yn_ref[GDN_HEADS + h] for h in range(GDN_HEADS)]
    vs = [yn_ref[2 * GDN_HEADS + h] for h in range(GDN_HEADS)]
    kts = [kh.T for kh in ks]

    r_c = lax.broadcasted_iota(jnp.int32, (CHUNK, LANES), 0)
    lane = lax.broadcasted_iota(jnp.int32, (CHUNK, LANES), 1)
    is_fwd = lane < CHUNK
    c_c = lane % CHUNK
    eye = jnp.where(r_c == c_c, 1.0, 0.0).astype(F32)
    incl = (is_fwd & (r_c >= c_c)) | (~is_fwd & (r_c <= c_c))
    strict = (is_fwd & (r_c > c_c)) | (~is_fwd & (r_c < c_c))
    r_d = lax.broadcasted_iota(jnp.int32, (2 * CHUNK, LANES), 0)
    c_d = lax.broadcasted_iota(jnp.int32, (2 * CHUNK, LANES), 1)
    same_dir = (r_d // CHUNK) == (c_d // CHUNK)
    level_mask = lambda s_, r_, c_: ((r_ // (2 * s_)) == (c_ // (2 * s_))) & ((r_ // s_) != (c_ // s_))
    stack2 = lambda t: jnp.concatenate([t, t], axis=0)

    n_chunks = t_len // CHUNK
    rows = lambda c: slice(c * CHUNK, (c + 1) * CHUNK)
    pairs = [(c, h) for c in range(n_chunks) for h in range(GDN_HEADS)]
    qkk = {}
    for c, h in pairs:
        k16 = ks[h][rows(c)].astype(BF16)
        qk16 = jnp.concatenate([qs[h][rows(c)].astype(BF16), k16], axis=0)
        qkk[c, h] = _dot_nt(qk16, stack2(k16))

    bcast = lambda col: jnp.broadcast_to(col, (CHUNK, LANES))
    brow, grow, glast, a_mat, t_mat, kv16 = {}, {}, {}, {}, {}, {}
    for key in pairs:
        c, h = key
        g_full = {}
        for d in range(N_DIR):
            j = d * GDN_HEADS + h
            r_last = c * CHUNK + (CHUNK - 1 if d == 0 else 0)
            g_full[d] = bcast(cs_col[d][rows(c), j:j + 1])
            brow[key, d] = beta_t[SUBLANES + j:SUBLANES + j + 1, rows(c)]
            glast[key, d] = cs_col[d][r_last:r_last + 1, j:j + 1]
            grow[key, d] = cs_row[d][j:j + 1, rows(c)]
        gcol2 = jnp.where(is_fwd, g_full[0], g_full[1])
        grow2 = jnp.concatenate([grow[key, 0], grow[key, 1]], axis=1)
        brow2 = jnp.concatenate([brow[key, 0], brow[key, 1]], axis=1)
        decay = jnp.exp2(jnp.where(incl, gcol2 - grow2, -jnp.inf))
        a_mat[key] = jnp.where(strict, qkk[key][CHUNK:] * decay, 0.0) * brow2
        t_mat[key] = eye - jnp.where(level_mask(1, r_c, c_c), a_mat[key], 0.0)
        qk_ref[0, h, rows(c), :] = (qkk[key][:CHUNK] * decay * brow2).astype(qk_ref.dtype)
        eg = [jnp.exp2(g_full[d]) for d in range(N_DIR)]
        for d in range(N_DIR):
            qd_ref[0, d * GDN_HEADS + h, rows(c), :] = (qs[h][rows(c)] * eg[d]).astype(qd_ref.dtype)
        kv16[key] = jnp.concatenate(
            [jnp.concatenate([ks[h][rows(c)] * eg[d], vs[h][rows(c)]], axis=1) for d in range(N_DIR)],
            axis=0).astype(BF16)

    same_dir16 = jnp.where(same_dir, 1.0, 0.0).astype(BF16)
    block_diag = lambda t16: stack2(t16) * same_dir16
    a_bd = {key: block_diag(a_mat[key].astype(BF16)) for key in pairs}
    s = 2
    while s < CHUNK:
        lvl = level_mask(s, r_c, c_c)
        t16 = {key: t_mat[key].astype(BF16) for key in pairs}
        x_mat = {key: jnp.dot(t16[key], a_bd[key], preferred_element_type=F32) for key in pairs}
        y_mat = {key: jnp.dot(x_mat[key].astype(BF16), block_diag(t16[key]), preferred_element_type=F32)
                 for key in pairs}
        t_mat = {key: t_mat[key] - jnp.where(lvl, y_mat[key], 0.0) for key in pairs}
        s *= 2

    wu = {}
    for key in pairs:
        c, h = key
        t_sel = block_diag(t_mat[key].astype(BF16))
        wu[key] = jnp.dot(t_sel, kv16[key], preferred_element_type=F32)

    for c in range(n_chunks):
        egl_rows = []
        for d in range(N_DIR):
            for h in range(GDN_HEADS):
                key = (c, h)
                j = d * GDN_HEADS + h
                wu_d = wu[key][d * CHUNK:(d + 1) * CHUNK]
                w_ref[0, j, rows(c), :] = wu_d[:, :GDN_HEAD_DIM].astype(w_ref.dtype)
                u_ref[0, j, rows(c), :] = wu_d[:, GDN_HEAD_DIM:].astype(u_ref.dtype)
                egl_rows.append(jnp.broadcast_to(jnp.exp2(glast[key, d]), (1, LANES)))
        egl_ref[0, c] = jnp.concatenate(egl_rows, axis=0)
    for cp in range(n_chunks // 2):
        for d in range(N_DIR):
            for h in range(GDN_HEADS):
                fac = jnp.concatenate([brow[(c, h), d] * jnp.exp2(glast[(c, h), d] - grow[(c, h), d])
                                       for c in (2 * cp, 2 * cp + 1)], axis=1)
                kdt = kts[h][:, 2 * cp * CHUNK:(2 * cp + 2) * CHUNK] * fac
                kdt_ref[0, d * GDN_HEADS + h, cp] = kdt.astype(kdt_ref.dtype)


def _gdn_prep(yn, ab, gate_par):
    b, s, _ = ab.shape
    nt = s // PREP_T
    cpb = PREP_T // CHUNK
    nc = s // CHUNK
    chain = lambda last: pl.BlockSpec((1, N_CHAIN, PREP_T, last), lambda bi, i: (bi, 0, i, 0))
    return pl.pallas_call(
        _gprep_kernel,
        grid=(b, nt),
        in_specs=[pl.BlockSpec((3 * GDN_W // LANES, PREP_T, LANES), lambda bi, i: (0, bi * nt + i, 0)),
                  pl.BlockSpec((1, PREP_T, LANES), lambda bi, i: (bi, i, 0)),
                  pl.BlockSpec((SUBLANES, LANES), lambda bi, i: (0, 0))],
        out_specs=[chain(GDN_HEAD_DIM), chain(GDN_HEAD_DIM), chain(GDN_HEAD_DIM),
                   pl.BlockSpec((1, N_CHAIN, cpb // 2, GDN_HEAD_DIM, 2 * CHUNK), lambda bi, i: (bi, 0, i, 0, 0)),
                   pl.BlockSpec((1, GDN_HEADS, PREP_T, LANES), lambda bi, i: (bi, 0, i, 0)),
                   pl.BlockSpec((1, cpb, N_CHAIN, LANES), lambda bi, i: (bi, i, 0, 0))],
        out_shape=[jax.ShapeDtypeStruct((b, N_CHAIN, s, GDN_HEAD_DIM), BF16),
                   jax.ShapeDtypeStruct((b, N_CHAIN, s, GDN_HEAD_DIM), BF16),
                   jax.ShapeDtypeStruct((b, N_CHAIN, s, GDN_HEAD_DIM), BF16),
                   jax.ShapeDtypeStruct((b, N_CHAIN, nc // 2, GDN_HEAD_DIM, 2 * CHUNK), BF16),
                   jax.ShapeDtypeStruct((b, GDN_HEADS, s, LANES), BF16),
                   jax.ShapeDtypeStruct((b, nc, N_CHAIN, LANES), F32)],
        compiler_params=pltpu.CompilerParams(dimension_semantics=("arbitrary", "arbitrary"),
                                             vmem_limit_bytes=VMEM_LIMIT),
        name="gdn_prep",
    )(yn, ab, gate_par)


def _gscan_kernel(egl_ref, wf_ref, wb_ref, uf_ref, ub_ref, qf_ref, qb_ref, kf_ref, kb_ref,
                  pf_ref, pb_ref, of_ref, ob_ref, state_ref):
    t = pl.program_id(0)
    n_steps = pl.num_programs(0)
    nc = n_steps * SCAN_CHUNKS
    n_batch = wf_ref.shape[0]

    @pl.when(t == 0)
    def _():
        state_ref[...] = jnp.zeros_like(state_ref)

    dirs = ((wf_ref, uf_ref, qf_ref, kf_ref, pf_ref, of_ref), (wb_ref, ub_ref, qb_ref, kb_ref, pb_ref, ob_ref))
    chains = [(bi, d, h) for bi in range(n_batch) for d in range(N_DIR) for h in range(GDN_HEADS)]
    slot = lambda bi, d, h: (bi * N_DIR + d) * GDN_HEADS + h
    st = {key: state_ref[slot(*key)] for key in chains}

    for sub in range(SCAN_CHUNKS):
        local = (sub, SCAN_CHUNKS - 1 - sub)
        rows = [slice(c * CHUNK, (c + 1) * CHUNK) for c in local]
        chunk = (t * SCAN_CHUNKS + sub, nc - 1 - (t * SCAN_CHUNKS + sub))
        r = {}
        for key in chains:
            bi, d, h = key
            w_ref, _, q_ref = dirs[d][:3]
            wq = jnp.concatenate([w_ref[bi, h, rows[d]], q_ref[bi, h, rows[d]]], axis=0)
            r[key] = _dot(wq, st[key])
        v_pad, intra = {}, {}
        for key in chains:
            bi, d, h = key
            u_ref, p_ref = dirs[d][1], dirs[d][4]
            v_new = (u_ref[bi, h, rows[d]].astype(F32) - r[key][:CHUNK]).astype(BF16)
            zeros = jnp.zeros_like(v_new)
            v_pad[key] = (jnp.concatenate([v_new, zeros], axis=0), jnp.concatenate([zeros, v_new], axis=0))
            intra[key] = jnp.dot(p_ref[bi, h, rows[d]], v_pad[key][d], preferred_element_type=F32)
        for key in chains:
            bi, d, h = key
            k_ref = dirs[d][3]
            egl = egl_ref[(bi * nc + chunk[d]) * N_CHAIN + d * GDN_HEADS + h]
            st[key] = st[key] * egl + jnp.dot(k_ref[bi, h, local[d] // 2], v_pad[key][local[d] % 2],
                                              preferred_element_type=F32)
        for bi in range(n_batch):
            for d in range(N_DIR):
                o_ref = dirs[d][5]
                o_ref[bi, rows[d], :] = jnp.concatenate(
                    [r[bi, d, h][CHUNK:] + intra[bi, d, h] for h in range(GDN_HEADS)], axis=-1).astype(o_ref.dtype)

    for key in chains:
        state_ref[slot(*key)] = st[key]


def _gdn_scan(egl, w, u, qd, kdt, qk):
    b, _, s, _ = w.shape
    rows = SCAN_CHUNKS * CHUNK
    n_steps = s // rows
    fwd_i = lambda t: t
    bwd_i = lambda t: n_steps - 1 - t
    chain = lambda d, at, last: pl.BlockSpec((b, GDN_HEADS, rows, last), lambda t: (0, d, at(t), 0))
    kspec = lambda d, at: pl.BlockSpec((b, GDN_HEADS, SCAN_CHUNKS // 2, GDN_HEAD_DIM, 2 * CHUNK),
                                       lambda t: (0, d, at(t), 0, 0))
    dk = GDN_HEAD_DIM
    return pl.pallas_call(
        _gscan_kernel,
        grid=(n_steps,),
        in_specs=[pl.BlockSpec(memory_space=pltpu.SMEM),
                  chain(0, fwd_i, dk), chain(1, bwd_i, dk), chain(0, fwd_i, dk), chain(1, bwd_i, dk),
                  chain(0, fwd_i, dk), chain(1, bwd_i, dk), kspec(0, fwd_i), kspec(1, bwd_i),
                  chain(0, fwd_i, LANES), chain(0, bwd_i, LANES)],
        out_specs=[pl.BlockSpec((b, rows, GDN_W), lambda t: (0, fwd_i(t), 0)),
                   pl.BlockSpec((b, rows, GDN_W), lambda t: (0, bwd_i(t), 0))],
        out_shape=[jax.ShapeDtypeStruct((b, s, GDN_W), BF16),
                   jax.ShapeDtypeStruct((b, s, GDN_W), BF16)],
        scratch_shapes=[pltpu.VMEM((b * N_CHAIN, GDN_HEAD_DIM, GDN_HEAD_DIM), F32)],
        compiler_params=pltpu.CompilerParams(dimension_semantics=("arbitrary",),
                                             vmem_limit_bytes=VMEM_LIMIT),
        name="gdn_scan",
    )(egl, w, w, u, u, qd, qd, kdt, kdt, qk, qk)


def _ffn_kernel(x_ref, attn_ref, of_ref, ob_ref, z_ref, gnw_ref, wo_hbm, fnw_ref, w1_hbm, w2_hbm, onw_ref,
                o_ref, wo_ref, w1_ref, w2_ref, sem):
    @pl.when(pl.program_id(0) == 0)
    def _():
        copies = [pltpu.make_async_copy(src, dst, sem.at[k])
                  for k, (src, dst) in enumerate(((wo_hbm, wo_ref), (w1_hbm, w1_ref), (w2_hbm, w2_ref)))]
        for copy in copies:
            copy.start()
        for copy in copies:
            copy.wait()

    subs = [slice(r0, r0 + FFN_SUB) for r0 in range(0, FFN_TM, FFN_SUB)]
    rms = lambda t: t * lax.rsqrt(jnp.mean(t * t, axis=-1, keepdims=True) + EPS)
    hres, hn, act, acc = {}, {}, {}, {}
    for r in subs:
        o = of_ref[r, :].astype(F32) + ob_ref[r, :].astype(F32)
        heads = [rms(o[:, h * GDN_HEAD_DIM:(h + 1) * GDN_HEAD_DIM]) * gnw_ref[...] for h in range(GDN_HEADS)]
        z = z_ref[r, :]
        gdn = jnp.concatenate(heads, axis=-1) * (z * jax.nn.sigmoid(z))
        hres[r.start] = (x_ref[r, :] + jnp.dot(attn_ref[r, :], wo_ref[:ATTN_Q, :], preferred_element_type=F32)
                         + _dot(gdn, wo_ref[ATTN_Q:, :]))
    for r in subs:
        hn[r.start] = (rms(hres[r.start]) * fnw_ref[...]).astype(BF16)
        act[r.start] = jnp.dot(hn[r.start], w1_ref[...], preferred_element_type=F32)
    for r in subs:
        a = jnp.square(jnp.maximum(act[r.start], 0.0)).astype(BF16)
        acc[r.start] = hres[r.start] + jnp.dot(a, w2_ref[...], preferred_element_type=F32)
    for r in subs:
        o_ref[r, :] = rms(acc[r.start]) * onw_ref[...]


def _out_ffn(x2, attn, o_f, o_b, z, gnw, wo, fnw, w1, w2, onw):
    n = x2.shape[0]
    assert wo.shape == (D_MODEL, D_MODEL) and w1.shape == (D_MODEL, D_FF) and w2.shape == (D_FF, D_MODEL)
    row = lambda w: pl.BlockSpec((FFN_TM, w), lambda i: (i, 0))
    full = lambda a: pl.BlockSpec(a.shape, lambda i: (0, 0))
    hbm = pl.BlockSpec(memory_space=pl.ANY)
    return pl.pallas_call(
        _ffn_kernel,
        grid=(n // FFN_TM,),
        in_specs=[row(D_MODEL), row(ATTN_Q), row(GDN_W), row(GDN_W), row(GDN_W),
                  full(gnw), hbm, full(fnw), hbm, hbm, full(onw)],
        out_specs=row(D_MODEL),
        out_shape=jax.ShapeDtypeStruct((n, D_MODEL), F32),
        scratch_shapes=[pltpu.VMEM((D_MODEL, D_MODEL), BF16),
                        pltpu.VMEM((D_MODEL, D_FF), BF16),
                        pltpu.VMEM((D_FF, D_MODEL), BF16),
                        pltpu.SemaphoreType.DMA((3,))],
        compiler_params=pltpu.CompilerParams(dimension_semantics=("arbitrary",),
                                             vmem_limit_bytes=VMEM_LIMIT),
        name="out_ffn",
    )(x2, attn, o_f, o_b, z, gnw, wo, fnw, w1, w2, onw)


def _layer(h, band, norm_mix_w, w_in, layer, attn_sink, conv_w, gdn_a_log, gdn_dt_bias, gdn_norm_w,
           w_out, norm_ffn_w, w_ffn_in, w_ffn_out, out_norm_w):
    b, s, _ = h.shape
    n = b * s
    x2 = h.reshape(n, D_MODEL)
    conv_pad = jnp.zeros((SUBLANES, 3 * GDN_W), F32).at[:CONV_K].set(conv_w)
    q_a, k_a, v_a, yn, z_g, ab = _proj(x2, norm_mix_w.reshape(1, D_MODEL), w_in, conv_pad, layer, s)

    attn, wo_bf, w1_bf, w2_bf = _attention(q_a, k_a, v_a, band, attn_sink, b, layer, (w_out, w_ffn_in, w_ffn_out))

    gate_par = jnp.zeros((SUBLANES, LANES), F32)
    gate_par = gate_par.at[0, :N_CHAIN].set(gdn_a_log.reshape(-1)).at[1, :N_CHAIN].set(gdn_dt_bias.reshape(-1))
    w_c, u_c, q_dec, k_dec_t, qk, egl = _gdn_prep(yn, ab.reshape(b, s, LANES), gate_par)
    o_f, o_b = _gdn_scan(egl[..., 0].reshape(-1), w_c, u_c, q_dec, k_dec_t, qk)

    out = _out_ffn(x2, attn, o_f.reshape(n, GDN_W), o_b.reshape(n, GDN_W), z_g,
                   gdn_norm_w.reshape(1, GDN_HEAD_DIM), wo_bf, norm_ffn_w.reshape(1, D_MODEL),
                   w1_bf, w2_bf, out_norm_w.reshape(1, D_MODEL))
    return out.reshape(b, s, D_MODEL)


def kernel(x, norm_mix_w, w_in, rel_bias, attn_sink, conv_w, gdn_a_log, gdn_dt_bias, gdn_norm_w, w_out,
           norm_ffn_w, w_ffn_in, w_ffn_out, norm_final_w):
    depth = w_in.shape[0]
    assert depth == 1, "the fused output kernel applies the final norm after the single trunk layer"
    rel = (np.arange(3 * BLOCK)[None, :] - BLOCK) - np.arange(BLOCK)[:, None]
    bucket = _t5_buckets(jnp.asarray(rel, dtype=jnp.int32))
    band = _bias_band(rel_bias, bucket.T)
    return _layer(x, band, norm_mix_w[0], w_in, 0, attn_sink[0], conv_w[0], gdn_a_log[0], gdn_dt_bias[0],
                  gdn_norm_w[0], w_out, norm_ffn_w[0], w_ffn_in, w_ffn_out, norm_final_w)
```

```python
import functools
import math

import jax
import jax.numpy as jnp
import numpy as np
from jax import lax
from jax.experimental import pallas as pl
from jax.experimental.pallas import tpu as pltpu

F32 = jnp.float32
BF16 = jnp.bfloat16

D_MODEL = 1024
ATTN_HEADS = 8
ATTN_KV_HEADS = 2
ATTN_HEAD_DIM = 64
ATTN_GROUP = ATTN_HEADS // ATTN_KV_HEADS
WINDOW = 128
BLOCK = 128
N_BUCKETS = 32
MAX_DISTANCE = 128
GDN_HEADS = 4
GDN_HEAD_DIM = 128
CONV_K = 5
CHUNK = 64
N_DIR = 2
N_CHAIN = N_DIR * GDN_HEADS
D_FF = 4 * D_MODEL
EPS = 1e-6
LOG2E = math.log2(math.e)
ATTN_Q = ATTN_HEADS * ATTN_HEAD_DIM
ATTN_KV = ATTN_KV_HEADS * ATTN_HEAD_DIM
GDN_W = GDN_HEADS * GDN_HEAD_DIM
LANES = 128
SUBLANES = 8
BF16_SUBLANES = 16
VMEM_LIMIT = 56 * 1024 * 1024

PROJ_TM = 1024
PROJ_SUB = 256
PROJ_CAST_ROWS = 256
PROJ_CAST_SLOTS = 4
ATTN_QB = 8
PREP_T = 1024
ROW_STRIDE = 4
SCAN_CHUNKS = 8
FFN_TM = 512
FFN_SUB = 256


def _dot(a, b):
    return jnp.dot(a.astype(BF16), b.astype(BF16), preferred_element_type=F32)


def _dot_nt(a, b):
    return lax.dot_general(a.astype(BF16), b.astype(BF16), (((1,), (1,)), ((), ())),
                           preferred_element_type=F32)


def _split3(x):
    hi = x.astype(BF16)
    r1 = x - hi.astype(F32)
    mid = r1.astype(BF16)
    lo = (r1 - mid.astype(F32)).astype(BF16)
    return hi, mid, lo


def _dot01_right(x, m01):
    hi, mid, lo = _split3(x)
    d = lambda p: jnp.dot(p, m01, preferred_element_type=F32)
    return d(hi) + d(mid) + d(lo)


def _conv_silu_norm(xe_ref, cw_ref, yn_ref, r0, n_out):
    halo = CONV_K // 2
    n_rows = n_out // ROW_STRIDE
    n_slab = 3 * GDN_W // LANES
    units = [(sb, ph) for sb in range(n_slab) for ph in range(ROW_STRIDE)]
    half_taps = [cw_ref[j:j + 1, :] * 0.5 for j in range(CONV_K)]
    yv = {}
    for sb, ph in units:
        lanes = slice(sb * LANES, (sb + 1) * LANES)
        acc = None
        for j in range(CONV_K):
            win = xe_ref[sb, pl.ds(SUBLANES - halo + j + ph, n_rows, stride=ROW_STRIDE), :]
            term = half_taps[j][:, lanes] * win
            acc = term if acc is None else acc + term
        yv[sb, ph] = acc
    for key in units:
        yv[key] = yv[key] + yv[key] * jnp.tanh(yv[key])
    for sb, ph in units:
        if sb < 2 * GDN_HEADS:
            scale = lax.rsqrt(jnp.sum(yv[sb, ph] * yv[sb, ph], axis=-1, keepdims=True) + EPS)
            if sb < GDN_HEADS:
                scale = scale * (GDN_HEAD_DIM ** -0.5)
            yv[sb, ph] = yv[sb, ph] * scale
    for sb, ph in units:
        yn_ref[sb, pl.ds(r0 + ph, n_rows, stride=ROW_STRIDE), :] = yv[sb, ph]


def _proj_kernel(layer, tiles_per_seq, x_ref, xp_ref, xn_ref, nw_ref, wt_hbm, cw_ref,
                 qa_ref, ka_ref, va_ref, yn_ref, z_ref, ab_ref, wb_ref, stage_ref, sem, *xe_refs):
    d_in = wt_hbm.shape[1]
    d_main = d_in // LANES * LANES
    step = pl.program_id(0)

    @pl.when(step == 0)
    def _():
        starts = list(range(0, d_main, PROJ_CAST_ROWS)) + [d_main]
        sizes = [min(PROJ_CAST_ROWS, d_main - c0) for c0 in starts[:-1]] + [d_in - d_main]
        slot = lambda k: k % PROJ_CAST_SLOTS
        copies = [pltpu.make_async_copy(wt_hbm.at[layer, pl.ds(c0, nr)], stage_ref.at[slot(k), pl.ds(0, nr)],
                                        sem.at[slot(k)]) for k, (c0, nr) in enumerate(zip(starts, sizes))]
        lane = lax.broadcasted_iota(jnp.int32, (D_MODEL, PROJ_CAST_ROWS), 1)
        for k in range(min(PROJ_CAST_SLOTS - 1, len(copies))):
            copies[k].start()
        for k, (c0, nr) in enumerate(zip(starts, sizes)):
            if k + PROJ_CAST_SLOTS - 1 < len(copies):
                copies[k + PROJ_CAST_SLOTS - 1].start()
            copies[k].wait()
            cols = stage_ref[slot(k)].T
            if nr < PROJ_CAST_ROWS:
                cols = jnp.where(lane < nr, cols, 0.0)
            width = min(PROJ_CAST_ROWS, wb_ref.shape[1] - c0)
            wb_ref[:, c0:c0 + width] = cols[:, :width].astype(BF16)

    o_q, o_k, o_v, o_g, o_z, o_ab = 0, ATTN_Q, ATTN_Q + ATTN_KV, ATTN_Q + 2 * ATTN_KV, \
        ATTN_Q + 2 * ATTN_KV + 3 * GDN_W, ATTN_Q + 2 * ATTN_KV + 4 * GDN_W
    n_slab = 3 * GDN_W // LANES

    def normed(x):
        ms = jnp.mean(x * x, axis=-1, keepdims=True)
        return (x * lax.rsqrt(ms + EPS) * nw_ref[...]).astype(BF16)

    n_sub = PROJ_TM // PROJ_SUB

    def to_slabs(xe_ref, y, row0):
        for sb in range(n_slab):
            xe_ref[sb, row0:row0 + y.shape[0], :] = y[:, sb * LANES:(sb + 1) * LANES]

    def put(k, y):
        to_slabs(xe_refs[k], y, SUBLANES)
        if k > 0:
            to_slabs(xe_refs[k - 1], y[:SUBLANES], SUBLANES + PROJ_SUB)
        if k + 1 < n_sub:
            to_slabs(xe_refs[k + 1], y[PROJ_SUB - SUBLANES:], 0)

    i = step % tiles_per_seq
    outs = ((qa_ref, o_q, ATTN_Q, ATTN_HEAD_DIM ** -0.5 * LOG2E), (ka_ref, o_k, ATTN_KV, None), (va_ref, o_v, ATTN_KV, None),
            (z_ref, o_z, GDN_W, None), (ab_ref, o_ab, LANES, None))
    for k in range(n_sub):
        rows = slice(k * PROJ_SUB, (k + 1) * PROJ_SUB)
        if k == 0:
            xg = normed(jnp.concatenate([xp_ref[...], xn_ref[...], x_ref[rows, :]], axis=0))
            yg = jnp.dot(xg, wb_ref[:, o_g:o_z], preferred_element_type=F32)
            to_slabs(xe_refs[0], jnp.where(i > 0, yg[:SUBLANES], 0.0), 0)
            to_slabs(xe_refs[n_sub - 1], jnp.where(i < tiles_per_seq - 1, yg[SUBLANES:2 * SUBLANES], 0.0),
                     SUBLANES + PROJ_SUB)
            put(0, yg[2 * SUBLANES:])
            xn = xg[2 * SUBLANES:]
        else:
            xn = normed(x_ref[rows, :])
            put(k, jnp.dot(xn, wb_ref[:, o_g:o_z], preferred_element_type=F32))
            _conv_silu_norm(xe_refs[k - 1], cw_ref, yn_ref, (k - 1) * PROJ_SUB, PROJ_SUB)
        for ref, c0, width, scale in outs:
            y = jnp.dot(xn, wb_ref[:, c0:c0 + width], preferred_element_type=F32)
            if scale is not None:
                y = y * scale
            ref[rows, :] = y.astype(ref.dtype)
    _conv_silu_norm(xe_refs[n_sub - 1], cw_ref, yn_ref, PROJ_TM - PROJ_SUB, PROJ_SUB)


def _proj(x2, norm_w, w_in, conv_w, layer, seq_len):
    n = x2.shape[0]
    d_in = w_in.shape[2]
    d_pad = d_in // LANES * LANES + LANES
    assert d_pad == ATTN_Q + 2 * ATTN_KV + 4 * GDN_W + LANES and d_in - (d_pad - LANES) == 2 * N_CHAIN
    assert seq_len % PROJ_TM == 0
    w_t = jnp.swapaxes(w_in, 1, 2)
    nh8 = PROJ_TM // SUBLANES
    row = lambda w: pl.BlockSpec((PROJ_TM, w), lambda i: (i, 0))
    return pl.pallas_call(
        functools.partial(_proj_kernel, layer, seq_len // PROJ_TM),
        grid=(n // PROJ_TM,),
        in_specs=[row(D_MODEL),
                  pl.BlockSpec((SUBLANES, D_MODEL), lambda i: (jnp.maximum(i * nh8 - 1, 0), 0)),
                  pl.BlockSpec((SUBLANES, D_MODEL), lambda i: (jnp.minimum((i + 1) * nh8, n // SUBLANES - 1), 0)),
                  pl.BlockSpec((1, D_MODEL), lambda i: (0, 0)),
                  pl.BlockSpec(memory_space=pl.ANY),
                  pl.BlockSpec((SUBLANES, 3 * GDN_W), lambda i: (0, 0))],
        out_specs=[row(ATTN_Q), row(ATTN_KV), row(ATTN_KV),
                   pl.BlockSpec((3 * GDN_W // LANES, PROJ_TM, LANES), lambda i: (0, i, 0)),
                   row(GDN_W), row(LANES)],
        out_shape=[jax.ShapeDtypeStruct((n, ATTN_Q), BF16),
                   jax.ShapeDtypeStruct((n, ATTN_KV), BF16),
                   jax.ShapeDtypeStruct((n, ATTN_KV), BF16),
                   jax.ShapeDtypeStruct((3 * GDN_W // LANES, n, LANES), F32),
                   jax.ShapeDtypeStruct((n, GDN_W), F32),
                   jax.ShapeDtypeStruct((n, LANES), F32)],
        scratch_shapes=[pltpu.VMEM((D_MODEL, d_pad), BF16),
                        pltpu.VMEM((PROJ_CAST_SLOTS, PROJ_CAST_ROWS, D_MODEL), F32),
                        pltpu.SemaphoreType.DMA((PROJ_CAST_SLOTS,))]
        + [pltpu.VMEM((3 * GDN_W // LANES, PROJ_SUB + 2 * SUBLANES, LANES), F32)] * (PROJ_TM // PROJ_SUB),
        compiler_params=pltpu.CompilerParams(dimension_semantics=("arbitrary",),
                                             vmem_limit_bytes=VMEM_LIMIT),
        name="proj",
    )(x2, x2, x2, norm_w, w_t, conv_w)


def _bias_kernel(relb_ref, bucket_ref, o_ref):
    bucket = bucket_ref[...]
    key = lax.broadcasted_iota(jnp.int32, (3 * BLOCK, BLOCK), 0)
    qry = lax.broadcasted_iota(jnp.int32, (3 * BLOCK, BLOCK), 1)
    in_window = jnp.abs(key - BLOCK - qry) <= WINDOW
    acc = [jnp.zeros((3 * BLOCK, BLOCK), F32) for _ in range(ATTN_HEADS)]
    for b in range(N_BUCKETS):
        hit = bucket == b
        acc = [jnp.where(hit, relb_ref[b, h], acc[h]) for h in range(ATTN_HEADS)]
    for h in range(ATTN_HEADS):
        o_ref[h // 2, :, (h % 2) * BLOCK:(h % 2 + 1) * BLOCK] = jnp.where(in_window, acc[h] * LOG2E, -1e30)


def _bias_band(rel_bias, bucket_t):
    shape = (ATTN_HEADS // 2, 3 * BLOCK, 2 * BLOCK)
    return pl.pallas_call(
        _bias_kernel,
        in_specs=[pl.BlockSpec(memory_space=pltpu.SMEM),
                  pl.BlockSpec((3 * BLOCK, BLOCK), lambda: (0, 0))],
        out_specs=pl.BlockSpec(shape, lambda: (0, 0, 0)),
        out_shape=jax.ShapeDtypeStruct(shape, F32),
        name="bias_band",
    )(rel_bias, bucket_t)


def _t5_buckets(rel):
    nb = N_BUCKETS // 2
    max_exact = nb // 2
    base = jnp.where(rel > 0, nb, 0)
    n = jnp.abs(rel)
    log_ratio = jnp.log(jnp.maximum(n, 1).astype(jnp.float32) / max_exact) / math.log(MAX_DISTANCE / max_exact)
    large = jnp.minimum(max_exact + (log_ratio * (nb - max_exact)).astype(jnp.int32), nb - 1)
    return base + jnp.where(n < max_exact, n, large)


def _attn_kernel(sink_ref, q_ref, kp_ref, kc_ref, kn_ref, vp_ref, vc_ref, vn_ref, bias_ref, wo_ref, w1_ref, w2_ref,
                 o_ref, wo_bf_ref, w1_bf_ref, w2_bf_ref):
    for src, dst in ((wo_ref, wo_bf_ref), (w1_ref, w1_bf_ref), (w2_ref, w2_bf_ref)):
        dst[...] = src[0].astype(dst.dtype)
    n = pl.program_id(1)
    last = pl.num_programs(1) - 1
    kband = jnp.concatenate([kp_ref[...], kc_ref[...], kn_ref[...]], axis=0)
    vband = jnp.concatenate([vp_ref[...], vc_ref[...], vn_ref[...]], axis=0)
    vband_t = vband.astype(F32).T.astype(BF16)
    key = lax.broadcasted_iota(jnp.int32, (3 * BLOCK, 1), 0)
    first_head = lax.broadcasted_iota(jnp.int32, (1, 2 * BLOCK), 1) < BLOCK
    head = lambda t, i: t[:, i * ATTN_HEAD_DIM:(i + 1) * ATTN_HEAD_DIM]
    n_pairs = ATTN_HEADS // 2
    kv_of = lambda pr: (2 * pr) // ATTN_GROUP
    units = [(j, pr) for j in range(ATTN_QB) for pr in range(n_pairs)]
    band_rows = lambda j: slice(j * BLOCK, (j + 3) * BLOCK)

    scores = {}
    for j, pr in units:
        qj = q_ref[j * BLOCK:(j + 1) * BLOCK, :]
        q2 = jnp.concatenate([head(qj, 2 * pr), head(qj, 2 * pr + 1)], axis=0)
        scores[j, pr] = _dot_nt(head(kband, kv_of(pr))[band_rows(j)], q2)
    probs, dens = {}, {}
    for j, pr in units:
        s = scores[j, pr] + bias_ref[pr]
        if j == 0:
            s = jnp.where((key < BLOCK) & (n == 0), -1e30, s)
        if j == ATTN_QB - 1:
            s = jnp.where((key >= 2 * BLOCK) & (n == last), -1e30, s)
        sink = jnp.where(first_head, sink_ref[2 * pr], sink_ref[2 * pr + 1]) * LOG2E
        m = jnp.maximum(jnp.max(s, axis=0, keepdims=True), sink)
        probs[j, pr] = jnp.exp2(s - m).astype(BF16)
        dens[j, pr] = jnp.exp2(sink - m)
    outs_t = {}
    ones_rows = jnp.ones((2 * SUBLANES, 3 * BLOCK), BF16)
    for j, pr in units:
        kv = kv_of(pr)
        v_t = vband_t[kv * ATTN_HEAD_DIM:(kv + 1) * ATTN_HEAD_DIM, band_rows(j)]
        pv = jnp.dot(jnp.concatenate([v_t, ones_rows], axis=0), probs[j, pr], preferred_element_type=F32)
        outs_t[j, pr] = pv[:ATTN_HEAD_DIM] / (pv[ATTN_HEAD_DIM:ATTN_HEAD_DIM + 1] + dens[j, pr])
    for j in range(ATTN_QB):
        o_t = jnp.concatenate([outs_t[j, pr][:, half * BLOCK:(half + 1) * BLOCK]
                               for pr in range(n_pairs) for half in range(2)], axis=0)
        o_ref[j * BLOCK:(j + 1) * BLOCK, :] = o_t.T.astype(o_ref.dtype)


def _attention(q_a, k_a, v_a, band, sink, batch, layer, weights):
    n_tok = q_a.shape[0]
    nb = n_tok // batch // BLOCK
    rows = ATTN_QB * BLOCK
    steps = nb // ATTN_QB
    kv_spec = lambda r, f: pl.BlockSpec((r, ATTN_KV), f)
    prev = lambda bi, n: (bi * nb + jnp.maximum(n * ATTN_QB - 1, 0), 0)
    cur = lambda bi, n: (bi * steps + n, 0)
    nxt = lambda bi, n: (bi * nb + jnp.minimum((n + 1) * ATTN_QB, nb - 1), 0)
    n_steps = batch * steps
    assert all(w.shape[1] % (n_steps * BF16_SUBLANES) == 0 for w in weights)
    w_rows = [w.shape[1] // n_steps for w in weights]
    w_in_specs = [pl.BlockSpec((1, r, w.shape[2]), lambda bi, n: (layer, bi * steps + n, 0))
                  for r, w in zip(w_rows, weights)]
    w_out_specs = [pl.BlockSpec((r, w.shape[2]), lambda bi, n: (bi * steps + n, 0)) for r, w in zip(w_rows, weights)]
    return pl.pallas_call(
        _attn_kernel,
        grid=(batch, steps),
        in_specs=[pl.BlockSpec(memory_space=pltpu.SMEM),
                  pl.BlockSpec((rows, ATTN_Q), cur),
                  kv_spec(BLOCK, prev), kv_spec(rows, cur), kv_spec(BLOCK, nxt),
                  kv_spec(BLOCK, prev), kv_spec(rows, cur), kv_spec(BLOCK, nxt),
                  pl.BlockSpec(band.shape, lambda bi, n: (0, 0, 0))] + w_in_specs,
        out_specs=[pl.BlockSpec((rows, ATTN_Q), cur)] + w_out_specs,
        out_shape=[jax.ShapeDtypeStruct((n_tok, ATTN_Q), BF16)]
                  + [jax.ShapeDtypeStruct(w.shape[1:], BF16) for w in weights],
        compiler_params=pltpu.CompilerParams(dimension_semantics=("arbitrary", "arbitrary"),
                                             vmem_limit_bytes=VMEM_LIMIT),
        name="attn",
    )(sink, q_a, k_a, k_a, k_a, v_a, v_a, v_a, band, *weights)


def _gprep_kernel(yn_ref, ab_ref, gp_ref, w_ref, u_ref, qd_ref, kdt_ref, qkf_ref, qkb_ref, egl_ref):
    t_len = PREP_T

    ab = ab_ref[0]
    sp_in = ab + gp_ref[1:2, :]
    softplus = jnp.maximum(sp_in, 0.0) + jnp.log1p(jnp.exp(-jnp.abs(sp_in)))
    g = (-jnp.exp(gp_ref[0:1, :]) * LOG2E) * softplus
    beta = jax.nn.sigmoid(ab)

    r_t = lax.broadcasted_iota(jnp.int32, (t_len, t_len), 0)
    c_t = lax.broadcasted_iota(jnp.int32, (t_len, t_len), 1)
    same = (r_t // CHUNK) == (c_t // CHUNK)
    lower = jnp.where(same & (r_t >= c_t), 1.0, 0.0).astype(BF16)
    upper = jnp.where(same & (r_t <= c_t), 1.0, 0.0).astype(BF16)
    g_t = g.T[:2 * SUBLANES]
    cs_row = (_dot01_right(g_t, upper), _dot01_right(g_t, lower))
    pad_rows = jnp.zeros((LANES - 2 * SUBLANES, t_len), F32)
    cs_col = tuple(jnp.concatenate([r, pad_rows], axis=0).T for r in cs_row)
    beta_t = beta.T[:3 * SUBLANES]

    qs = [yn_ref[h] for h in range(GDN_HEADS)]
    ks = [yn_ref[GDN_HEADS + h] for h in range(GDN_HEADS)]
    vs = [yn_ref[2 * GDN_HEADS + h] for h in range(GDN_HEADS)]
    kts = [kh.T for kh in ks]

    r_c = lax.broadcasted_iota(jnp.int32, (CHUNK, LANES), 0)
    lane = lax.broadcasted_iota(jnp.int32, (CHUNK, LANES), 1)
    is_fwd = lane < CHUNK
    c_c = lane % CHUNK
    eye = jnp.where(r_c == c_c, 1.0, 0.0).astype(F32)
    incl = (is_fwd & (r_c >= c_c)) | (~is_fwd & (r_c <= c_c))
    strict = (is_fwd & (r_c > c_c)) | (~is_fwd & (r_c < c_c))
    r_d = lax.broadcasted_iota(jnp.int32, (2 * CHUNK, LANES), 0)
    c_d = lax.broadcasted_iota(jnp.int32, (2 * CHUNK, LANES), 1)
    same_dir = (r_d // CHUNK) == (c_d // CHUNK)
    level_mask = lambda s_, r_, c_: ((r_ // (2 * s_)) == (c_ // (2 * s_))) & ((r_ // s_) != (c_ // s_))
    stack2 = lambda t: jnp.concatenate([t, t], axis=0)

    n_chunks = t_len // CHUNK
    rows = lambda c: slice(c * CHUNK, (c + 1) * CHUNK)
    pairs = [(c, h) for c in range(n_chunks) for h in range(GDN_HEADS)]
    qkk = {}
    for c, h in pairs:
        k16 = ks[h][rows(c)].astype(BF16)
        qk16 = jnp.concatenate([qs[h][rows(c)].astype(BF16), k16], axis=0)
        qkk[c, h] = _dot_nt(qk16, stack2(k16))

    bcast = lambda col: jnp.broadcast_to(col, (CHUNK, LANES))
    brow, grow, glast, a_mat, t_mat, kv16, qk_even = {}, {}, {}, {}, {}, {}, {}
    for key in pairs:
        c, h = key
        g_full = {}
        for d in range(N_DIR):
            j = d * GDN_HEADS + h
            r_last = c * CHUNK + (CHUNK - 1 if d == 0 else 0)
            g_full[d] = bcast(cs_col[d][rows(c), j:j + 1])
            brow[key, d] = beta_t[SUBLANES + j:SUBLANES + j + 1, rows(c)]
            glast[key, d] = cs_col[d][r_last:r_last + 1, j:j + 1]
            grow[key, d] = cs_row[d][j:j + 1, rows(c)]
        gcol2 = jnp.where(is_fwd, g_full[0], g_full[1])
        grow2 = jnp.concatenate([grow[key, 0], grow[key, 1]], axis=1)
        brow2 = jnp.concatenate([brow[key, 0], brow[key, 1]], axis=1)
        decay = jnp.exp2(jnp.where(incl, gcol2 - grow2, -jnp.inf))
        a_mat[key] = jnp.where(strict, qkk[key][CHUNK:] * decay, 0.0) * brow2
        t_mat[key] = eye - jnp.where(level_mask(1, r_c, c_c), a_mat[key], 0.0)
        qk = qkk[key][:CHUNK] * decay * brow2
        if c % 2 == 0:
            qk_even[h] = qk
        else:
            pair_rows = rows(c // 2)
            qkf_ref[0, h, pair_rows, :] = jnp.where(is_fwd, qk_even[h], pltpu.roll(qk, CHUNK, 1)).astype(qkf_ref.dtype)
            qkb_ref[0, h, pair_rows, :] = jnp.where(is_fwd, pltpu.roll(qk_even[h], CHUNK, 1), qk).astype(qkb_ref.dtype)
        eg = [jnp.exp2(g_full[d]) for d in range(N_DIR)]
        for d in range(N_DIR):
            qd_ref[0, d * GDN_HEADS + h, rows(c), :] = (qs[h][rows(c)] * eg[d]).astype(qd_ref.dtype)
        kv16[key] = jnp.concatenate(
            [jnp.concatenate([ks[h][rows(c)] * eg[d], vs[h][rows(c)]], axis=1) for d in range(N_DIR)],
            axis=0).astype(BF16)

    same_dir16 = jnp.where(same_dir, 1.0, 0.0).astype(BF16)
    block_diag = lambda t16: stack2(t16) * same_dir16
    a_bd = {key: block_diag(a_mat[key].astype(BF16)) for key in pairs}
    s = 2
    while s < CHUNK:
        lvl = level_mask(s, r_c, c_c)
        t16 = {key: t_mat[key].astype(BF16) for key in pairs}
        x_mat = {key: jnp.dot(t16[key], a_bd[key], preferred_element_type=F32) for key in pairs}
        y_mat = {key: jnp.dot(x_mat[key].astype(BF16), block_diag(t16[key]), preferred_element_type=F32)
                 for key in pairs}
        t_mat = {key: t_mat[key] - jnp.where(lvl, y_mat[key], 0.0) for key in pairs}
        s *= 2

    wu = {}
    for key in pairs:
        c, h = key
        t_sel = block_diag(t_mat[key].astype(BF16))
        wu[key] = jnp.dot(t_sel, kv16[key], preferred_element_type=F32)

    for c in range(n_chunks):
        egl_rows = []
        for d in range(N_DIR):
            for h in range(GDN_HEADS):
                key = (c, h)
                j = d * GDN_HEADS + h
                wu_d = wu[key][d * CHUNK:(d + 1) * CHUNK]
                w_ref[0, j, rows(c), :] = wu_d[:, :GDN_HEAD_DIM].astype(w_ref.dtype)
                u_ref[0, j, rows(c), :] = wu_d[:, GDN_HEAD_DIM:].astype(u_ref.dtype)
                egl_rows.append(jnp.broadcast_to(jnp.exp2(glast[key, d]), (1, LANES)))
        egl_ref[0, c] = jnp.concatenate(egl_rows, axis=0)
    for cp in range(n_chunks // 2):
        for d in range(N_DIR):
            for h in range(GDN_HEADS):
                fac = jnp.concatenate([brow[(c, h), d] * jnp.exp2(glast[(c, h), d] - grow[(c, h), d])
                                       for c in (2 * cp, 2 * cp + 1)], axis=1)
                kdt = kts[h][:, 2 * cp * CHUNK:(2 * cp + 2) * CHUNK] * fac
                kdt_ref[0, d * GDN_HEADS + h, cp] = kdt.astype(kdt_ref.dtype)


def _gdn_prep(yn, ab, gate_par):
    b, s, _ = ab.shape
    nt = s // PREP_T
    cpb = PREP_T // CHUNK
    nc = s // CHUNK
    chain = lambda last: pl.BlockSpec((1, N_CHAIN, PREP_T, last), lambda bi, i: (bi, 0, i, 0))
    return pl.pallas_call(
        _gprep_kernel,
        grid=(b, nt),
        in_specs=[pl.BlockSpec((3 * GDN_W // LANES, PREP_T, LANES), lambda bi, i: (0, bi * nt + i, 0)),
                  pl.BlockSpec((1, PREP_T, LANES), lambda bi, i: (bi, i, 0)),
                  pl.BlockSpec((SUBLANES, LANES), lambda bi, i: (0, 0))],
        out_specs=[chain(GDN_HEAD_DIM), chain(GDN_HEAD_DIM), chain(GDN_HEAD_DIM),
                   pl.BlockSpec((1, N_CHAIN, cpb // 2, GDN_HEAD_DIM, 2 * CHUNK), lambda bi, i: (bi, 0, i, 0, 0)),
                   pl.BlockSpec((1, GDN_HEADS, PREP_T // 2, LANES), lambda bi, i: (bi, 0, i, 0)),
                   pl.BlockSpec((1, GDN_HEADS, PREP_T // 2, LANES), lambda bi, i: (bi, 0, i, 0)),
                   pl.BlockSpec((1, cpb, N_CHAIN, LANES), lambda bi, i: (bi, i, 0, 0))],
        out_shape=[jax.ShapeDtypeStruct((b, N_CHAIN, s, GDN_HEAD_DIM), BF16),
                   jax.ShapeDtypeStruct((b, N_CHAIN, s, GDN_HEAD_DIM), BF16),
                   jax.ShapeDtypeStruct((b, N_CHAIN, s, GDN_HEAD_DIM), BF16),
                   jax.ShapeDtypeStruct((b, N_CHAIN, nc // 2, GDN_HEAD_DIM, 2 * CHUNK), BF16),
                   jax.ShapeDtypeStruct((b, GDN_HEADS, s // 2, LANES), BF16),
                   jax.ShapeDtypeStruct((b, GDN_HEADS, s // 2, LANES), BF16),
                   jax.ShapeDtypeStruct((b, nc, N_CHAIN, LANES), F32)],
        compiler_params=pltpu.CompilerParams(dimension_semantics=("arbitrary", "arbitrary"),
                                             vmem_limit_bytes=VMEM_LIMIT),
        name="gdn_prep",
    )(yn, ab, gate_par)


def _gscan_kernel(egl_ref, wf_ref, wb_ref, uf_ref, ub_ref, qf_ref, qb_ref, kf_ref, kb_ref,
                  pf_ref, pb_ref, of_ref, ob_ref, state_ref):
    t = pl.program_id(0)
    n_steps = pl.num_programs(0)
    nc = n_steps * SCAN_CHUNKS
    n_batch = wf_ref.shape[0]

    @pl.when(t == 0)
    def _():
        state_ref[...] = jnp.zeros_like(state_ref)

    dirs = ((wf_ref, uf_ref, qf_ref, kf_ref, pf_ref, of_ref), (wb_ref, ub_ref, qb_ref, kb_ref, pb_ref, ob_ref))
    chains = [(bi, d, h) for bi in range(n_batch) for d in range(N_DIR) for h in range(GDN_HEADS)]
    slot = lambda bi, d, h: (bi * N_DIR + d) * GDN_HEADS + h
    st = {key: state_ref[slot(*key)] for key in chains}

    for sub in range(SCAN_CHUNKS):
        local = (sub, SCAN_CHUNKS - 1 - sub)
        rows = [slice(c * CHUNK, (c + 1) * CHUNK) for c in local]
        chunk = (t * SCAN_CHUNKS + sub, nc - 1 - (t * SCAN_CHUNKS + sub))
        r = {}
        for key in chains:
            bi, d, h = key
            w_ref, _, q_ref = dirs[d][:3]
            wq = jnp.concatenate([w_ref[bi, h, rows[d]], q_ref[bi, h, rows[d]]], axis=0)
            r[key] = _dot(wq, st[key])
        v_pad, intra = {}, {}
        for key in chains:
            bi, d, h = key
            u_ref, p_ref = dirs[d][1], dirs[d][4]
            v_new = (u_ref[bi, h, rows[d]].astype(F32) - r[key][:CHUNK]).astype(BF16)
            zeros = jnp.zeros_like(v_new)
            v_pad[key] = (jnp.concatenate([v_new, zeros], axis=0), jnp.concatenate([zeros, v_new], axis=0))
            pair_rows = slice(local[d] // 2 * CHUNK, (local[d] // 2 + 1) * CHUNK)
            intra[key] = jnp.dot(p_ref[bi, h, pair_rows], v_pad[key][local[d] % 2], preferred_element_type=F32)
        for key in chains:
            bi, d, h = key
            k_ref = dirs[d][3]
            egl = egl_ref[(bi * nc + chunk[d]) * N_CHAIN + d * GDN_HEADS + h]
            st[key] = st[key] * egl + jnp.dot(k_ref[bi, h, local[d] // 2], v_pad[key][local[d] % 2],
                                              preferred_element_type=F32)
        for bi in range(n_batch):
            for d in range(N_DIR):
                o_ref = dirs[d][5]
                o_ref[bi, rows[d], :] = jnp.concatenate(
                    [r[bi, d, h][CHUNK:] + intra[bi, d, h] for h in range(GDN_HEADS)], axis=-1).astype(o_ref.dtype)

    for key in chains:
        state_ref[slot(*key)] = st[key]


def _gdn_scan(egl, w, u, qd, kdt, qk_f, qk_b):
    b, _, s, _ = w.shape
    rows = SCAN_CHUNKS * CHUNK
    n_steps = s // rows
    fwd_i = lambda t: t
    bwd_i = lambda t: n_steps - 1 - t
    chain = lambda d, at, last: pl.BlockSpec((b, GDN_HEADS, rows, last), lambda t: (0, d, at(t), 0))
    kspec = lambda d, at: pl.BlockSpec((b, GDN_HEADS, SCAN_CHUNKS // 2, GDN_HEAD_DIM, 2 * CHUNK),
                                       lambda t: (0, d, at(t), 0, 0))
    pspec = lambda at: pl.BlockSpec((b, GDN_HEADS, rows // 2, LANES), lambda t: (0, 0, at(t), 0))
    dk = GDN_HEAD_DIM
    return pl.pallas_call(
        _gscan_kernel,
        grid=(n_steps,),
        in_specs=[pl.BlockSpec(memory_space=pltpu.SMEM),
                  chain(0, fwd_i, dk), chain(1, bwd_i, dk), chain(0, fwd_i, dk), chain(1, bwd_i, dk),
                  chain(0, fwd_i, dk), chain(1, bwd_i, dk), kspec(0, fwd_i), kspec(1, bwd_i),
                  pspec(fwd_i), pspec(bwd_i)],
        out_specs=[pl.BlockSpec((b, rows, GDN_W), lambda t: (0, fwd_i(t), 0)),
                   pl.BlockSpec((b, rows, GDN_W), lambda t: (0, bwd_i(t), 0))],
        out_shape=[jax.ShapeDtypeStruct((b, s, GDN_W), BF16),
                   jax.ShapeDtypeStruct((b, s, GDN_W), BF16)],
        scratch_shapes=[pltpu.VMEM((b * N_CHAIN, GDN_HEAD_DIM, GDN_HEAD_DIM), F32)],
        compiler_params=pltpu.CompilerParams(dimension_semantics=("arbitrary",),
                                             vmem_limit_bytes=VMEM_LIMIT),
        name="gdn_scan",
    )(egl, w, w, u, u, qd, qd, kdt, kdt, qk_f, qk_b)


def _ffn_kernel(x_ref, attn_ref, of_ref, ob_ref, z_ref, gnw_ref, wo_hbm, fnw_ref, w1_hbm, w2_hbm, onw_ref,
                o_ref, wo_ref, w1_ref, w2_ref, sem):
    @pl.when(pl.program_id(0) == 0)
    def _():
        copies = [pltpu.make_async_copy(src, dst, sem.at[k])
                  for k, (src, dst) in enumerate(((wo_hbm, wo_ref), (w1_hbm, w1_ref), (w2_hbm, w2_ref)))]
        for copy in copies:
            copy.start()
        for copy in copies:
            copy.wait()

    subs = [slice(r0, r0 + FFN_SUB) for r0 in range(0, FFN_TM, FFN_SUB)]
    rms = lambda t: t * lax.rsqrt(jnp.mean(t * t, axis=-1, keepdims=True) + EPS)
    hres, hn, act, acc = {}, {}, {}, {}
    for r in subs:
        o = of_ref[r, :].astype(F32) + ob_ref[r, :].astype(F32)
        heads = [rms(o[:, h * GDN_HEAD_DIM:(h + 1) * GDN_HEAD_DIM]) * gnw_ref[...] for h in range(GDN_HEADS)]
        z = z_ref[r, :]
        gdn = jnp.concatenate(heads, axis=-1) * (z * jax.nn.sigmoid(z))
        hres[r.start] = (x_ref[r, :] + jnp.dot(attn_ref[r, :], wo_ref[:ATTN_Q, :], preferred_element_type=F32)
                         + _dot(gdn, wo_ref[ATTN_Q:, :]))
    for r in subs:
        hn[r.start] = (rms(hres[r.start]) * fnw_ref[...]).astype(BF16)
        act[r.start] = jnp.dot(hn[r.start], w1_ref[...], preferred_element_type=F32)
    for r in subs:
        a = jnp.square(jnp.maximum(act[r.start], 0.0)).astype(BF16)
        acc[r.start] = hres[r.start] + jnp.dot(a, w2_ref[...], preferred_element_type=F32)
    for r in subs:
        o_ref[r, :] = rms(acc[r.start]) * onw_ref[...]


def _out_ffn(x2, attn, o_f, o_b, z, gnw, wo, fnw, w1, w2, onw):
    n = x2.shape[0]
    assert wo.shape == (D_MODEL, D_MODEL) and w1.shape == (D_MODEL, D_FF) and w2.shape == (D_FF, D_MODEL)
    row = lambda w: pl.BlockSpec((FFN_TM, w), lambda i: (i, 0))
    full = lambda a: pl.BlockSpec(a.shape, lambda i: (0, 0))
    hbm = pl.BlockSpec(memory_space=pl.ANY)
    return pl.pallas_call(
        _ffn_kernel,
        grid=(n // FFN_TM,),
        in_specs=[row(D_MODEL), row(ATTN_Q), row(GDN_W), row(GDN_W), row(GDN_W),
                  full(gnw), hbm, full(fnw), hbm, hbm, full(onw)],
        out_specs=row(D_MODEL),
        out_shape=jax.ShapeDtypeStruct((n, D_MODEL), F32),
        scratch_shapes=[pltpu.VMEM((D_MODEL, D_MODEL), BF16),
                        pltpu.VMEM((D_MODEL, D_FF), BF16),
                        pltpu.VMEM((D_FF, D_MODEL), BF16),
                        pltpu.SemaphoreType.DMA((3,))],
        compiler_params=pltpu.CompilerParams(dimension_semantics=("arbitrary",),
                                             vmem_limit_bytes=VMEM_LIMIT),
        name="out_ffn",
    )(x2, attn, o_f, o_b, z, gnw, wo, fnw, w1, w2, onw)


def _layer(h, band, norm_mix_w, w_in, layer, attn_sink, conv_w, gdn_a_log, gdn_dt_bias, gdn_norm_w,
           w_out, norm_ffn_w, w_ffn_in, w_ffn_out, out_norm_w):
    b, s, _ = h.shape
    n = b * s
    x2 = h.reshape(n, D_MODEL)
    conv_pad = jnp.zeros((SUBLANES, 3 * GDN_W), F32).at[:CONV_K].set(conv_w)
    q_a, k_a, v_a, yn, z_g, ab = _proj(x2, norm_mix_w.reshape(1, D_MODEL), w_in, conv_pad, layer, s)

    attn, wo_bf, w1_bf, w2_bf = _attention(q_a, k_a, v_a, band, attn_sink, b, layer, (w_out, w_ffn_in, w_ffn_out))

    gate_par = jnp.zeros((SUBLANES, LANES), F32)
    gate_par = gate_par.at[0, :N_CHAIN].set(gdn_a_log.reshape(-1)).at[1, :N_CHAIN].set(gdn_dt_bias.reshape(-1))
    w_c, u_c, q_dec, k_dec_t, qk_f, qk_b, egl = _gdn_prep(yn, ab.reshape(b, s, LANES), gate_par)
    o_f, o_b = _gdn_scan(egl[..., 0].reshape(-1), w_c, u_c, q_dec, k_dec_t, qk_f, qk_b)

    out = _out_ffn(x2, attn, o_f.reshape(n, GDN_W), o_b.reshape(n, GDN_W), z_g,
                   gdn_norm_w.reshape(1, GDN_HEAD_DIM), wo_bf, norm_ffn_w.reshape(1, D_MODEL),
                   w1_bf, w2_bf, out_norm_w.reshape(1, D_MODEL))
    return out.reshape(b, s, D_MODEL)


def kernel(x, norm_mix_w, w_in, rel_bias, attn_sink, conv_w, gdn_a_log, gdn_dt_bias, gdn_norm_w, w_out,
           norm_ffn_w, w_ffn_in, w_ffn_out, norm_final_w):
    depth = w_in.shape[0]
    assert depth == 1, "the fused output kernel applies the final norm after the single trunk layer"
    rel = (np.arange(3 * BLOCK)[None, :] - BLOCK) - np.arange(BLOCK)[:, None]
    bucket = _t5_buckets(jnp.asarray(rel, dtype=jnp.int32))
    band = _bias_band(rel_bias, bucket.T)
    return _layer(x, band, norm_mix_w[0], w_in, 0, attn_sink[0], conv_w[0], gdn_a_log[0], gdn_dt_bias[0],
                  gdn_norm_w[0], w_out, norm_ffn_w[0], w_ffn_in, w_ffn_out, norm_final_w)
```

```python
import functools
import math

import jax
import jax.numpy as jnp
import numpy as np
from jax import lax
from jax.experimental import pallas as pl
from jax.experimental.pallas import tpu as pltpu

F32 = jnp.float32
BF16 = jnp.bfloat16

D_MODEL = 1024
ATTN_HEADS = 8
ATTN_KV_HEADS = 2
ATTN_HEAD_DIM = 64
ATTN_GROUP = ATTN_HEADS // ATTN_KV_HEADS
WINDOW = 128
BLOCK = 128
N_BUCKETS = 32
MAX_DISTANCE = 128
GDN_HEADS = 4
GDN_HEAD_DIM = 128
CONV_K = 5
CHUNK = 64
N_DIR = 2
N_CHAIN = N_DIR * GDN_HEADS
D_FF = 4 * D_MODEL
EPS = 1e-6
LOG2E = math.log2(math.e)
ATTN_Q = ATTN_HEADS * ATTN_HEAD_DIM
ATTN_KV = ATTN_KV_HEADS * ATTN_HEAD_DIM
GDN_W = GDN_HEADS * GDN_HEAD_DIM
LANES = 128
SUBLANES = 8
BF16_SUBLANES = 16
VMEM_LIMIT = 56 * 1024 * 1024

PROJ_TM = 1024
PROJ_SUB = 256
PROJ_CAST_ROWS = 256
PROJ_CAST_SLOTS = 4
ATTN_QB = 8
PREP_T = 1024
ROW_STRIDE = 4
SCAN_CHUNKS = 8
FFN_TM = 512
FFN_SUB = 256


def _dot(a, b):
    return jnp.dot(a.astype(BF16), b.astype(BF16), preferred_element_type=F32)


def _dot_nt(a, b):
    return lax.dot_general(a.astype(BF16), b.astype(BF16), (((1,), (1,)), ((), ())),
                           preferred_element_type=F32)


def _split3(x):
    hi = x.astype(BF16)
    r1 = x - hi.astype(F32)
    mid = r1.astype(BF16)
    lo = (r1 - mid.astype(F32)).astype(BF16)
    return hi, mid, lo


def _dot01_right(x, m01):
    hi, mid, lo = _split3(x)
    d = lambda p: jnp.dot(p, m01, preferred_element_type=F32)
    return d(hi) + d(mid) + d(lo)


def _conv_silu_norm(xe_ref, cw_ref, yn_ref, r0, n_out):
    halo = CONV_K // 2
    n_rows = n_out // ROW_STRIDE
    n_slab = 3 * GDN_W // LANES
    units = [(sb, ph) for sb in range(n_slab) for ph in range(ROW_STRIDE)]
    half_taps = [cw_ref[j:j + 1, :] * 0.5 for j in range(CONV_K)]
    yv = {}
    for sb, ph in units:
        lanes = slice(sb * LANES, (sb + 1) * LANES)
        acc = None
        for j in range(CONV_K):
            win = xe_ref[sb, pl.ds(SUBLANES - halo + j + ph, n_rows, stride=ROW_STRIDE), :]
            term = half_taps[j][:, lanes] * win
            acc = term if acc is None else acc + term
        yv[sb, ph] = acc
    for key in units:
        yv[key] = yv[key] + yv[key] * jnp.tanh(yv[key])
    for sb, ph in units:
        if sb < 2 * GDN_HEADS:
            scale = lax.rsqrt(jnp.sum(yv[sb, ph] * yv[sb, ph], axis=-1, keepdims=True) + EPS)
            if sb < GDN_HEADS:
                scale = scale * (GDN_HEAD_DIM ** -0.5)
            yv[sb, ph] = yv[sb, ph] * scale
    for sb, ph in units:
        yn_ref[sb, pl.ds(r0 + ph, n_rows, stride=ROW_STRIDE), :] = yv[sb, ph]


def _proj_kernel(layer, tiles_per_seq, x_ref, xp_ref, xn_ref, nw_ref, wt_hbm, cw_ref,
                 qa_ref, ka_ref, va_ref, yn_ref, z_ref, ab_ref, wb_ref, stage_ref, sem, *xe_refs):
    d_in = wt_hbm.shape[1]
    d_main = d_in // LANES * LANES
    step = pl.program_id(0)

    @pl.when(step == 0)
    def _():
        starts = list(range(0, d_main, PROJ_CAST_ROWS)) + [d_main]
        sizes = [min(PROJ_CAST_ROWS, d_main - c0) for c0 in starts[:-1]] + [d_in - d_main]
        slot = lambda k: k % PROJ_CAST_SLOTS
        copies = [pltpu.make_async_copy(wt_hbm.at[layer, pl.ds(c0, nr)], stage_ref.at[slot(k), pl.ds(0, nr)],
                                        sem.at[slot(k)]) for k, (c0, nr) in enumerate(zip(starts, sizes))]
        lane = lax.broadcasted_iota(jnp.int32, (D_MODEL, PROJ_CAST_ROWS), 1)
        for k in range(min(PROJ_CAST_SLOTS - 1, len(copies))):
            copies[k].start()
        for k, (c0, nr) in enumerate(zip(starts, sizes)):
            if k + PROJ_CAST_SLOTS - 1 < len(copies):
                copies[k + PROJ_CAST_SLOTS - 1].start()
            copies[k].wait()
            cols = stage_ref[slot(k)].T
            if nr < PROJ_CAST_ROWS:
                cols = jnp.where(lane < nr, cols, 0.0)
            width = min(PROJ_CAST_ROWS, wb_ref.shape[1] - c0)
            wb_ref[:, c0:c0 + width] = cols[:, :width].astype(BF16)

    o_q, o_k, o_v, o_g, o_z, o_ab = 0, ATTN_Q, ATTN_Q + ATTN_KV, ATTN_Q + 2 * ATTN_KV, \
        ATTN_Q + 2 * ATTN_KV + 3 * GDN_W, ATTN_Q + 2 * ATTN_KV + 4 * GDN_W
    n_slab = 3 * GDN_W // LANES

    def normed(x):
        ms = jnp.mean(x * x, axis=-1, keepdims=True)
        return (x * lax.rsqrt(ms + EPS) * nw_ref[...]).astype(BF16)

    n_sub = PROJ_TM // PROJ_SUB

    def to_slabs(xe_ref, y, row0):
        for sb in range(n_slab):
            xe_ref[sb, row0:row0 + y.shape[0], :] = y[:, sb * LANES:(sb + 1) * LANES]

    def put(k, y):
        to_slabs(xe_refs[k], y, SUBLANES)
        if k > 0:
            to_slabs(xe_refs[k - 1], y[:SUBLANES], SUBLANES + PROJ_SUB)
        if k + 1 < n_sub:
            to_slabs(xe_refs[k + 1], y[PROJ_SUB - SUBLANES:], 0)

    i = step % tiles_per_seq
    outs = ((qa_ref, o_q, ATTN_Q, ATTN_HEAD_DIM ** -0.5 * LOG2E), (ka_ref, o_k, ATTN_KV, None), (va_ref, o_v, ATTN_KV, None),
            (z_ref, o_z, GDN_W, None), (ab_ref, o_ab, LANES, None))
    for k in range(n_sub):
        rows = slice(k * PROJ_SUB, (k + 1) * PROJ_SUB)
        if k == 0:
            xg = normed(jnp.concatenate([xp_ref[...], xn_ref[...], x_ref[rows, :]], axis=0))
            yg = jnp.dot(xg, wb_ref[:, o_g:o_z], preferred_element_type=F32)
            to_slabs(xe_refs[0], jnp.where(i > 0, yg[:SUBLANES], 0.0), 0)
            to_slabs(xe_refs[n_sub - 1], jnp.where(i < tiles_per_seq - 1, yg[SUBLANES:2 * SUBLANES], 0.0),
                     SUBLANES + PROJ_SUB)
            put(0, yg[2 * SUBLANES:])
            xn = xg[2 * SUBLANES:]
        else:
            xn = normed(x_ref[rows, :])
            put(k, jnp.dot(xn, wb_ref[:, o_g:o_z], preferred_element_type=F32))
            _conv_silu_norm(xe_refs[k - 1], cw_ref, yn_ref, (k - 1) * PROJ_SUB, PROJ_SUB)
        for ref, c0, width, scale in outs:
            y = jnp.dot(xn, wb_ref[:, c0:c0 + width], preferred_element_type=F32)
            if scale is not None:
                y = y * scale
            ref[rows, :] = y.astype(ref.dtype)
    _conv_silu_norm(xe_refs[n_sub - 1], cw_ref, yn_ref, PROJ_TM - PROJ_SUB, PROJ_SUB)


def _proj(x2, norm_w, w_in, conv_w, layer, seq_len):
    n = x2.shape[0]
    d_in = w_in.shape[2]
    d_pad = d_in // LANES * LANES + LANES
    assert d_pad == ATTN_Q + 2 * ATTN_KV + 4 * GDN_W + LANES and d_in - (d_pad - LANES) == 2 * N_CHAIN
    assert seq_len % PROJ_TM == 0
    w_t = jnp.swapaxes(w_in, 1, 2)
    nh8 = PROJ_TM // SUBLANES
    row = lambda w: pl.BlockSpec((PROJ_TM, w), lambda i: (i, 0))
    return pl.pallas_call(
        functools.partial(_proj_kernel, layer, seq_len // PROJ_TM),
        grid=(n // PROJ_TM,),
        in_specs=[row(D_MODEL),
                  pl.BlockSpec((SUBLANES, D_MODEL), lambda i: (jnp.maximum(i * nh8 - 1, 0), 0)),
                  pl.BlockSpec((SUBLANES, D_MODEL), lambda i: (jnp.minimum((i + 1) * nh8, n // SUBLANES - 1), 0)),
                  pl.BlockSpec((1, D_MODEL), lambda i: (0, 0)),
                  pl.BlockSpec(memory_space=pl.ANY),
                  pl.BlockSpec((SUBLANES, 3 * GDN_W), lambda i: (0, 0))],
        out_specs=[row(ATTN_Q), row(ATTN_KV), row(ATTN_KV),
                   pl.BlockSpec((3 * GDN_W // LANES, PROJ_TM, LANES), lambda i: (0, i, 0)),
                   row(GDN_W), row(LANES)],
        out_shape=[jax.ShapeDtypeStruct((n, ATTN_Q), BF16),
                   jax.ShapeDtypeStruct((n, ATTN_KV), BF16),
                   jax.ShapeDtypeStruct((n, ATTN_KV), BF16),
                   jax.ShapeDtypeStruct((3 * GDN_W // LANES, n, LANES), F32),
                   jax.ShapeDtypeStruct((n, GDN_W), F32),
                   jax.ShapeDtypeStruct((n, LANES), F32)],
        scratch_shapes=[pltpu.VMEM((D_MODEL, d_pad), BF16),
                        pltpu.VMEM((PROJ_CAST_SLOTS, PROJ_CAST_ROWS, D_MODEL), F32),
                        pltpu.SemaphoreType.DMA((PROJ_CAST_SLOTS,))]
        + [pltpu.VMEM((3 * GDN_W // LANES, PROJ_SUB + 2 * SUBLANES, LANES), F32)] * (PROJ_TM // PROJ_SUB),
        compiler_params=pltpu.CompilerParams(dimension_semantics=("arbitrary",),
                                             vmem_limit_bytes=VMEM_LIMIT),
        name="proj",
    )(x2, x2, x2, norm_w, w_t, conv_w)


def _bias_kernel(relb_ref, bucket_ref, o_ref):
    bucket = bucket_ref[...]
    key = lax.broadcasted_iota(jnp.int32, (3 * BLOCK, BLOCK), 0)
    qry = lax.broadcasted_iota(jnp.int32, (3 * BLOCK, BLOCK), 1)
    in_window = jnp.abs(key - BLOCK - qry) <= WINDOW
    acc = [jnp.zeros((3 * BLOCK, BLOCK), F32) for _ in range(ATTN_HEADS)]
    for b in range(N_BUCKETS):
        hit = bucket == b
        acc = [jnp.where(hit, relb_ref[b, h], acc[h]) for h in range(ATTN_HEADS)]
    for h in range(ATTN_HEADS):
        o_ref[h // 2, :, (h % 2) * BLOCK:(h % 2 + 1) * BLOCK] = jnp.where(in_window, acc[h] * LOG2E, -1e30)


def _bias_band(rel_bias, bucket_t):
    shape = (ATTN_HEADS // 2, 3 * BLOCK, 2 * BLOCK)
    return pl.pallas_call(
        _bias_kernel,
        in_specs=[pl.BlockSpec(memory_space=pltpu.SMEM),
                  pl.BlockSpec((3 * BLOCK, BLOCK), lambda: (0, 0))],
        out_specs=pl.BlockSpec(shape, lambda: (0, 0, 0)),
        out_shape=jax.ShapeDtypeStruct(shape, F32),
        name="bias_band",
    )(rel_bias, bucket_t)


def _t5_buckets(rel):
    nb = N_BUCKETS // 2
    max_exact = nb // 2
    base = jnp.where(rel > 0, nb, 0)
    n = jnp.abs(rel)
    log_ratio = jnp.log(jnp.maximum(n, 1).astype(jnp.float32) / max_exact) / math.log(MAX_DISTANCE / max_exact)
    large = jnp.minimum(max_exact + (log_ratio * (nb - max_exact)).astype(jnp.int32), nb - 1)
    return base + jnp.where(n < max_exact, n, large)


def _attn_kernel(sink_ref, q_ref, kp_ref, kc_ref, kn_ref, vp_ref, vc_ref, vn_ref, bias_ref, wo_ref, w1_ref, w2_ref,
                 o_ref, wo_bf_ref, w1_bf_ref, w2_bf_ref):
    for src, dst in ((wo_ref, wo_bf_ref), (w1_ref, w1_bf_ref), (w2_ref, w2_bf_ref)):
        dst[...] = src[0].astype(dst.dtype)
    n = pl.program_id(1)
    last = pl.num_programs(1) - 1
    kband = jnp.concatenate([kp_ref[...], kc_ref[...], kn_ref[...]], axis=0)
    vband = jnp.concatenate([vp_ref[...], vc_ref[...], vn_ref[...]], axis=0)
    vband_t = vband.astype(F32).T.astype(BF16)
    key = lax.broadcasted_iota(jnp.int32, (3 * BLOCK, 1), 0)
    first_head = lax.broadcasted_iota(jnp.int32, (1, 2 * BLOCK), 1) < BLOCK
    head = lambda t, i: t[:, i * ATTN_HEAD_DIM:(i + 1) * ATTN_HEAD_DIM]
    n_pairs = ATTN_HEADS // 2
    kv_of = lambda pr: (2 * pr) // ATTN_GROUP
    units = [(j, pr) for j in range(ATTN_QB) for pr in range(n_pairs)]
    band_rows = lambda j: slice(j * BLOCK, (j + 3) * BLOCK)

    scores = {}
    for j, pr in units:
        qj = q_ref[j * BLOCK:(j + 1) * BLOCK, :]
        q2 = jnp.concatenate([head(qj, 2 * pr), head(qj, 2 * pr + 1)], axis=0)
        scores[j, pr] = _dot_nt(head(kband, kv_of(pr))[band_rows(j)], q2)
    probs, dens = {}, {}
    for j, pr in units:
        s = scores[j, pr] + bias_ref[pr]
        if j == 0:
            s = jnp.where((key < BLOCK) & (n == 0), -1e30, s)
        if j == ATTN_QB - 1:
            s = jnp.where((key >= 2 * BLOCK) & (n == last), -1e30, s)
        sink = jnp.where(first_head, sink_ref[2 * pr], sink_ref[2 * pr + 1]) * LOG2E
        m = jnp.maximum(jnp.max(s, axis=0, keepdims=True), sink)
        probs[j, pr] = jnp.exp2(s - m).astype(BF16)
        dens[j, pr] = jnp.exp2(sink - m)
    outs_t = {}
    ones_rows = jnp.ones((2 * SUBLANES, 3 * BLOCK), BF16)
    for j, pr in units:
        kv = kv_of(pr)
        v_t = vband_t[kv * ATTN_HEAD_DIM:(kv + 1) * ATTN_HEAD_DIM, band_rows(j)]
        pv = jnp.dot(jnp.concatenate([v_t, ones_rows], axis=0), probs[j, pr], preferred_element_type=F32)
        outs_t[j, pr] = pv[:ATTN_HEAD_DIM] / (pv[ATTN_HEAD_DIM:ATTN_HEAD_DIM + 1] + dens[j, pr])
    for j in range(ATTN_QB):
        o_t = jnp.concatenate([outs_t[j, pr][:, half * BLOCK:(half + 1) * BLOCK]
                               for pr in range(n_pairs) for half in range(2)], axis=0)
        o_ref[j * BLOCK:(j + 1) * BLOCK, :] = o_t.T.astype(o_ref.dtype)


def _attention(q_a, k_a, v_a, band, sink, batch, layer, weights):
    n_tok = q_a.shape[0]
    nb = n_tok // batch // BLOCK
    rows = ATTN_QB * BLOCK
    steps = nb // ATTN_QB
    kv_spec = lambda r, f: pl.BlockSpec((r, ATTN_KV), f)
    prev = lambda bi, n: (bi * nb + jnp.maximum(n * ATTN_QB - 1, 0), 0)
    cur = lambda bi, n: (bi * steps + n, 0)
    nxt = lambda bi, n: (bi * nb + jnp.minimum((n + 1) * ATTN_QB, nb - 1), 0)
    n_steps = batch * steps
    assert all(w.shape[1] % (n_steps * BF16_SUBLANES) == 0 for w in weights)
    w_rows = [w.shape[1] // n_steps for w in weights]
    w_in_specs = [pl.BlockSpec((1, r, w.shape[2]), lambda bi, n: (layer, bi * steps + n, 0))
                  for r, w in zip(w_rows, weights)]
    w_out_specs = [pl.BlockSpec((r, w.shape[2]), lambda bi, n: (bi * steps + n, 0)) for r, w in zip(w_rows, weights)]
    return pl.pallas_call(
        _attn_kernel,
        grid=(batch, steps),
        in_specs=[pl.BlockSpec(memory_space=pltpu.SMEM),
                  pl.BlockSpec((rows, ATTN_Q), cur),
                  kv_spec(BLOCK, prev), kv_spec(rows, cur), kv_spec(BLOCK, nxt),
                  kv_spec(BLOCK, prev), kv_spec(rows, cur), kv_spec(BLOCK, nxt),
                  pl.BlockSpec(band.shape, lambda bi, n: (0, 0, 0))] + w_in_specs,
        out_specs=[pl.BlockSpec((rows, ATTN_Q), cur)] + w_out_specs,
        out_shape=[jax.ShapeDtypeStruct((n_tok, ATTN_Q), BF16)]
                  + [jax.ShapeDtypeStruct(w.shape[1:], BF16) for w in weights],
        compiler_params=pltpu.CompilerParams(dimension_semantics=("arbitrary", "arbitrary"),
                                             vmem_limit_bytes=VMEM_LIMIT),
        name="attn",
    )(sink, q_a, k_a, k_a, k_a, v_a, v_a, v_a, band, *weights)


def _gprep_kernel(yn_ref, ab_ref, gp_ref, w_ref, u_ref, qd_ref, kdt_ref, qkf_ref, qkb_ref, egl_ref):
    t_len = PREP_T

    ab = ab_ref[0]
    sp_in = ab + gp_ref[1:2, :]
    softplus = jnp.maximum(sp_in, 0.0) + jnp.log1p(jnp.exp(-jnp.abs(sp_in)))
    g = (-jnp.exp(gp_ref[0:1, :]) * LOG2E) * softplus
    beta = jax.nn.sigmoid(ab)

    r_t = lax.broadcasted_iota(jnp.int32, (t_len, t_len), 0)
    c_t = lax.broadcasted_iota(jnp.int32, (t_len, t_len), 1)
    same = (r_t // CHUNK) == (c_t // CHUNK)
    lower = jnp.where(same & (r_t >= c_t), 1.0, 0.0).astype(BF16)
    upper = jnp.where(same & (r_t <= c_t), 1.0, 0.0).astype(BF16)
    g_t = g.T[:2 * SUBLANES]
    cs_row = (_dot01_right(g_t, upper), _dot01_right(g_t, lower))
    pad_rows = jnp.zeros((LANES - 2 * SUBLANES, t_len), F32)
    cs_col = tuple(jnp.concatenate([r, pad_rows], axis=0).T for r in cs_row)
    beta_t = beta.T[:3 * SUBLANES]

    qs = [yn_ref[h] for h in range(GDN_HEADS)]
    ks = [yn_ref[GDN_HEADS + h] for h in range(GDN_HEADS)]
    vs = [yn_ref[2 * GDN_HEADS + h] for h in range(GDN_HEADS)]
    kts = [kh.T for kh in ks]

    r_c = lax.broadcasted_iota(jnp.int32, (CHUNK, LANES), 0)
    lane = lax.broadcasted_iota(jnp.int32, (CHUNK, LANES), 1)
    is_fwd = lane < CHUNK
    c_c = lane % CHUNK
    eye = jnp.where(r_c == c_c, 1.0, 0.0).astype(F32)
    incl = (is_fwd & (r_c >= c_c)) | (~is_fwd & (r_c <= c_c))
    strict = (is_fwd & (r_c > c_c)) | (~is_fwd & (r_c < c_c))
    r_d = lax.broadcasted_iota(jnp.int32, (2 * CHUNK, LANES), 0)
    c_d = lax.broadcasted_iota(jnp.int32, (2 * CHUNK, LANES), 1)
    same_dir = (r_d // CHUNK) == (c_d // CHUNK)
    level_mask = lambda s_, r_, c_: ((r_ // (2 * s_)) == (c_ // (2 * s_))) & ((r_ // s_) != (c_ // s_))
    stack2 = lambda t: jnp.concatenate([t, t], axis=0)

    n_chunks = t_len // CHUNK
    rows = lambda c: slice(c * CHUNK, (c + 1) * CHUNK)
    pairs = [(c, h) for c in range(n_chunks) for h in range(GDN_HEADS)]
    qkk = {}
    for c, h in pairs:
        k16 = ks[h][rows(c)].astype(BF16)
        qk16 = jnp.concatenate([qs[h][rows(c)].astype(BF16), k16], axis=0)
        qkk[c, h] = _dot_nt(qk16, stack2(k16))

    bcast = lambda col: jnp.broadcast_to(col, (CHUNK, LANES))
    brow, grow, glast, a_mat, t_mat, kv16, qk_even = {}, {}, {}, {}, {}, {}, {}
    for key in pairs:
        c, h = key
        g_full = {}
        for d in range(N_DIR):
            j = d * GDN_HEADS + h
            r_last = c * CHUNK + (CHUNK - 1 if d == 0 else 0)
            g_full[d] = bcast(cs_col[d][rows(c), j:j + 1])
            brow[key, d] = beta_t[SUBLANES + j:SUBLANES + j + 1, rows(c)]
            glast[key, d] = cs_col[d][r_last:r_last + 1, j:j + 1]
            grow[key, d] = cs_row[d][j:j + 1, rows(c)]
        gcol2 = jnp.where(is_fwd, g_full[0], g_full[1])
        grow2 = jnp.concatenate([grow[key, 0], grow[key, 1]], axis=1)
        brow2 = jnp.concatenate([brow[key, 0], brow[key, 1]], axis=1)
        decay = jnp.exp2(jnp.where(incl, gcol2 - grow2, -jnp.inf))
        a_mat[key] = jnp.where(strict, qkk[key][CHUNK:] * decay, 0.0) * brow2
        t_mat[key] = eye - jnp.where(level_mask(1, r_c, c_c), a_mat[key], 0.0)
        qk = qkk[key][:CHUNK] * decay * brow2
        if c % 2 == 0:
            qk_even[h] = qk
        else:
            pair_rows = rows(c // 2)
            qkf_ref[0, h, pair_rows, :] = jnp.where(is_fwd, qk_even[h], pltpu.roll(qk, CHUNK, 1)).astype(qkf_ref.dtype)
            qkb_ref[0, h, pair_rows, :] = jnp.where(is_fwd, pltpu.roll(qk_even[h], CHUNK, 1), qk).astype(qkb_ref.dtype)
        eg = [jnp.exp2(g_full[d]) for d in range(N_DIR)]
        for d in range(N_DIR):
            qd_ref[0, d * GDN_HEADS + h, rows(c), :] = (qs[h][rows(c)] * eg[d]).astype(qd_ref.dtype)
        kv16[key] = jnp.concatenate(
            [jnp.concatenate([ks[h][rows(c)] * eg[d], vs[h][rows(c)]], axis=1) for d in range(N_DIR)],
            axis=0).astype(BF16)

    same_dir16 = jnp.where(same_dir, 1.0, 0.0).astype(BF16)
    block_diag = lambda t16: stack2(t16) * same_dir16
    a_bd = {key: block_diag(a_mat[key].astype(BF16)) for key in pairs}
    s = 2
    while s < CHUNK:
        lvl = level_mask(s, r_c, c_c)
        t16 = {key: t_mat[key].astype(BF16) for key in pairs}
        x_mat = {key: jnp.dot(t16[key], a_bd[key], preferred_element_type=F32) for key in pairs}
        y_mat = {key: jnp.dot(x_mat[key].astype(BF16), block_diag(t16[key]), preferred_element_type=F32)
                 for key in pairs}
        t_mat = {key: t_mat[key] - jnp.where(lvl, y_mat[key], 0.0) for key in pairs}
        s *= 2

    wu = {}
    for key in pairs:
        c, h = key
        t_sel = block_diag(t_mat[key].astype(BF16))
        wu[key] = jnp.dot(t_sel, kv16[key], preferred_element_type=F32)

    for c in range(n_chunks):
        egl_rows = []
        for d in range(N_DIR):
            for h in range(GDN_HEADS):
                key = (c, h)
                j = d * GDN_HEADS + h
                wu_d = wu[key][d * CHUNK:(d + 1) * CHUNK]
                w_ref[0, j, rows(c), :] = wu_d[:, :GDN_HEAD_DIM].astype(w_ref.dtype)
                u_ref[0, j, rows(c), :] = wu_d[:, GDN_HEAD_DIM:].astype(u_ref.dtype)
                egl_rows.append(jnp.broadcast_to(jnp.exp2(glast[key, d]), (1, LANES)))
        egl_ref[0, c] = jnp.concatenate(egl_rows, axis=0)
    for cp in range(n_chunks // 2):
        for d in range(N_DIR):
            for h in range(GDN_HEADS):
                fac = jnp.concatenate([brow[(c, h), d] * jnp.exp2(glast[(c, h), d] - grow[(c, h), d])
                                       for c in (2 * cp, 2 * cp + 1)], axis=1)
                kdt = kts[h][:, 2 * cp * CHUNK:(2 * cp + 2) * CHUNK] * fac
                kdt_ref[0, d * GDN_HEADS + h, cp] = kdt.astype(kdt_ref.dtype)


def _gdn_prep(yn, ab, gate_par):
    b, s, _ = ab.shape
    nt = s // PREP_T
    cpb = PREP_T // CHUNK
    nc = s // CHUNK
    chain = lambda last: pl.BlockSpec((1, N_CHAIN, PREP_T, last), lambda bi, i: (bi, 0, i, 0))
    return pl.pallas_call(
        _gprep_kernel,
        grid=(b, nt),
        in_specs=[pl.BlockSpec((3 * GDN_W // LANES, PREP_T, LANES), lambda bi, i: (0, bi * nt + i, 0)),
                  pl.BlockSpec((1, PREP_T, LANES), lambda bi, i: (bi, i, 0)),
                  pl.BlockSpec((SUBLANES, LANES), lambda bi, i: (0, 0))],
        out_specs=[chain(GDN_HEAD_DIM), chain(GDN_HEAD_DIM), chain(GDN_HEAD_DIM),
                   pl.BlockSpec((1, N_CHAIN, cpb // 2, GDN_HEAD_DIM, 2 * CHUNK), lambda bi, i: (bi, 0, i, 0, 0)),
                   pl.BlockSpec((1, GDN_HEADS, PREP_T // 2, LANES), lambda bi, i: (bi, 0, i, 0)),
                   pl.BlockSpec((1, GDN_HEADS, PREP_T // 2, LANES), lambda bi, i: (bi, 0, i, 0)),
                   pl.BlockSpec((1, cpb, N_CHAIN, LANES), lambda bi, i: (bi, i, 0, 0))],
        out_shape=[jax.ShapeDtypeStruct((b, N_CHAIN, s, GDN_HEAD_DIM), BF16),
                   jax.ShapeDtypeStruct((b, N_CHAIN, s, GDN_HEAD_DIM), BF16),
                   jax.ShapeDtypeStruct((b, N_CHAIN, s, GDN_HEAD_DIM), BF16),
                   jax.ShapeDtypeStruct((b, N_CHAIN, nc // 2, GDN_HEAD_DIM, 2 * CHUNK), BF16),
                   jax.ShapeDtypeStruct((b, GDN_HEADS, s // 2, LANES), BF16),
                   jax.ShapeDtypeStruct((b, GDN_HEADS, s // 2, LANES), BF16),
                   jax.ShapeDtypeStruct((b, nc, N_CHAIN, LANES), F32)],
        compiler_params=pltpu.CompilerParams(dimension_semantics=("arbitrary", "arbitrary"),
                                             vmem_limit_bytes=VMEM_LIMIT),
        name="gdn_prep",
    )(yn, ab, gate_par)


def _gscan_kernel(egl_ref, wf_ref, wb_ref, uf_ref, ub_ref, qf_ref, qb_ref, kf_ref, kb_ref,
                  pf_ref, pb_ref, of_ref, ob_ref, state_ref):
    t = pl.program_id(0)
    n_steps = pl.num_programs(0)
    nc = n_steps * SCAN_CHUNKS
    n_batch = wf_ref.shape[0]

    @pl.when(t == 0)
    def _():
        state_ref[...] = jnp.zeros_like(state_ref)

    dirs = ((wf_ref, uf_ref, qf_ref, kf_ref, pf_ref, of_ref), (wb_ref, ub_ref, qb_ref, kb_ref, pb_ref, ob_ref))
    chains = [(bi, d, h) for bi in range(n_batch) for d in range(N_DIR) for h in range(GDN_HEADS)]
    slot = lambda bi, d, h: (bi * N_DIR + d) * GDN_HEADS + h
    st = {key: state_ref[slot(*key)] for key in chains}

    for sub in range(SCAN_CHUNKS):
        local = (sub, SCAN_CHUNKS - 1 - sub)
        rows = [slice(c * CHUNK, (c + 1) * CHUNK) for c in local]
        chunk = (t * SCAN_CHUNKS + sub, nc - 1 - (t * SCAN_CHUNKS + sub))
        r = {}
        for key in chains:
            bi, d, h = key
            w_ref, _, q_ref = dirs[d][:3]
            wq = jnp.concatenate([w_ref[bi, h, rows[d]], q_ref[bi, h, rows[d]]], axis=0)
            r[key] = _dot(wq, st[key])
        intra = {}
        for key in chains:
            bi, d, h = key
            u_ref, k_ref, p_ref = dirs[d][1], dirs[d][3], dirs[d][4]
            v_new = (u_ref[bi, h, rows[d]].astype(F32) - r[key][:CHUNK]).astype(BF16)
            zeros = jnp.zeros_like(v_new)
            v_pad = jnp.concatenate([v_new, zeros] if local[d] % 2 == 0 else [zeros, v_new], axis=0)
            pair_rows = slice(local[d] // 2 * CHUNK, (local[d] // 2 + 1) * CHUNK)
            pk = jnp.concatenate([p_ref[bi, h, pair_rows], k_ref[bi, h, local[d] // 2]], axis=0)
            res = jnp.dot(pk, v_pad, preferred_element_type=F32)
            intra[key] = res[:CHUNK]
            egl = egl_ref[(bi * nc + chunk[d]) * N_CHAIN + d * GDN_HEADS + h]
            st[key] = st[key] * egl + res[CHUNK:]
        for bi in range(n_batch):
            for d in range(N_DIR):
                o_ref = dirs[d][5]
                o_ref[bi, rows[d], :] = jnp.concatenate(
                    [r[bi, d, h][CHUNK:] + intra[bi, d, h] for h in range(GDN_HEADS)], axis=-1).astype(o_ref.dtype)

    for key in chains:
        state_ref[slot(*key)] = st[key]


def _gdn_scan(egl, w, u, qd, kdt, qk_f, qk_b):
    b, _, s, _ = w.shape
    rows = SCAN_CHUNKS * CHUNK
    n_steps = s // rows
    fwd_i = lambda t: t
    bwd_i = lambda t: n_steps - 1 - t
    chain = lambda d, at, last: pl.BlockSpec((b, GDN_HEADS, rows, last), lambda t: (0, d, at(t), 0))
    kspec = lambda d, at: pl.BlockSpec((b, GDN_HEADS, SCAN_CHUNKS // 2, GDN_HEAD_DIM, 2 * CHUNK),
                                       lambda t: (0, d, at(t), 0, 0))
    pspec = lambda at: pl.BlockSpec((b, GDN_HEADS, rows // 2, LANES), lambda t: (0, 0, at(t), 0))
    dk = GDN_HEAD_DIM
    return pl.pallas_call(
        _gscan_kernel,
        grid=(n_steps,),
        in_specs=[pl.BlockSpec(memory_space=pltpu.SMEM),
                  chain(0, fwd_i, dk), chain(1, bwd_i, dk), chain(0, fwd_i, dk), chain(1, bwd_i, dk),
                  chain(0, fwd_i, dk), chain(1, bwd_i, dk), kspec(0, fwd_i), kspec(1, bwd_i),
                  pspec(fwd_i), pspec(bwd_i)],
        out_specs=[pl.BlockSpec((b, rows, GDN_W), lambda t: (0, fwd_i(t), 0)),
                   pl.BlockSpec((b, rows, GDN_W), lambda t: (0, bwd_i(t), 0))],
        out_shape=[jax.ShapeDtypeStruct((b, s, GDN_W), BF16),
                   jax.ShapeDtypeStruct((b, s, GDN_W), BF16)],
        scratch_shapes=[pltpu.VMEM((b * N_CHAIN, GDN_HEAD_DIM, GDN_HEAD_DIM), F32)],
        compiler_params=pltpu.CompilerParams(dimension_semantics=("arbitrary",),
                                             vmem_limit_bytes=VMEM_LIMIT),
        name="gdn_scan",
    )(egl, w, w, u, u, qd, qd, kdt, kdt, qk_f, qk_b)


def _ffn_kernel(x_ref, attn_ref, of_ref, ob_ref, z_ref, gnw_ref, wo_hbm, fnw_ref, w1_hbm, w2_hbm, onw_ref,
                o_ref, wo_ref, w1_ref, w2_ref, sem):
    @pl.when(pl.program_id(0) == 0)
    def _():
        copies = [pltpu.make_async_copy(src, dst, sem.at[k])
                  for k, (src, dst) in enumerate(((wo_hbm, wo_ref), (w1_hbm, w1_ref), (w2_hbm, w2_ref)))]
        for copy in copies:
            copy.start()
        for copy in copies:
            copy.wait()

    subs = [slice(r0, r0 + FFN_SUB) for r0 in range(0, FFN_TM, FFN_SUB)]
    rms = lambda t: t * lax.rsqrt(jnp.mean(t * t, axis=-1, keepdims=True) + EPS)
    hres, hn, act, acc = {}, {}, {}, {}
    for r in subs:
        o = of_ref[r, :].astype(F32) + ob_ref[r, :].astype(F32)
        heads = [rms(o[:, h * GDN_HEAD_DIM:(h + 1) * GDN_HEAD_DIM]) * gnw_ref[...] for h in range(GDN_HEADS)]
        z = z_ref[r, :]
        gdn = jnp.concatenate(heads, axis=-1) * (z * jax.nn.sigmoid(z))
        hres[r.start] = (x_ref[r, :] + jnp.dot(attn_ref[r, :], wo_ref[:ATTN_Q, :], preferred_element_type=F32)
                         + _dot(gdn, wo_ref[ATTN_Q:, :]))
    for r in subs:
        hn[r.start] = (rms(hres[r.start]) * fnw_ref[...]).astype(BF16)
        act[r.start] = jnp.dot(hn[r.start], w1_ref[...], preferred_element_type=F32)
    for r in subs:
        a = jnp.square(jnp.maximum(act[r.start], 0.0)).astype(BF16)
        acc[r.start] = hres[r.start] + jnp.dot(a, w2_ref[...], preferred_element_type=F32)
    for r in subs:
        o_ref[r, :] = rms(acc[r.start]) * onw_ref[...]


def _out_ffn(x2, attn, o_f, o_b, z, gnw, wo, fnw, w1, w2, onw):
    n = x2.shape[0]
    assert wo.shape == (D_MODEL, D_MODEL) and w1.shape == (D_MODEL, D_FF) and w2.shape == (D_FF, D_MODEL)
    row = lambda w: pl.BlockSpec((FFN_TM, w), lambda i: (i, 0))
    full = lambda a: pl.BlockSpec(a.shape, lambda i: (0, 0))
    hbm = pl.BlockSpec(memory_space=pl.ANY)
    return pl.pallas_call(
        _ffn_kernel,
        grid=(n // FFN_TM,),
        in_specs=[row(D_MODEL), row(ATTN_Q), row(GDN_W), row(GDN_W), row(GDN_W),
                  full(gnw), hbm, full(fnw), hbm, hbm, full(onw)],
        out_specs=row(D_MODEL),
        out_shape=jax.ShapeDtypeStruct((n, D_MODEL), F32),
        scratch_shapes=[pltpu.VMEM((D_MODEL, D_MODEL), BF16),
                        pltpu.VMEM((D_MODEL, D_FF), BF16),
                        pltpu.VMEM((D_FF, D_MODEL), BF16),
                        pltpu.SemaphoreType.DMA((3,))],
        compiler_params=pltpu.CompilerParams(dimension_semantics=("arbitrary",),
                                             vmem_limit_bytes=VMEM_LIMIT),
        name="out_ffn",
    )(x2, attn, o_f, o_b, z, gnw, wo, fnw, w1, w2, onw)


def _layer(h, band, norm_mix_w, w_in, layer, attn_sink, conv_w, gdn_a_log, gdn_dt_bias, gdn_norm_w,
           w_out, norm_ffn_w, w_ffn_in, w_ffn_out, out_norm_w):
    b, s, _ = h.shape
    n = b * s
    x2 = h.reshape(n, D_MODEL)
    conv_pad = jnp.zeros((SUBLANES, 3 * GDN_W), F32).at[:CONV_K].set(conv_w)
    q_a, k_a, v_a, yn, z_g, ab = _proj(x2, norm_mix_w.reshape(1, D_MODEL), w_in, conv_pad, layer, s)

    attn, wo_bf, w1_bf, w2_bf = _attention(q_a, k_a, v_a, band, attn_sink, b, layer, (w_out, w_ffn_in, w_ffn_out))

    gate_par = jnp.zeros((SUBLANES, LANES), F32)
    gate_par = gate_par.at[0, :N_CHAIN].set(gdn_a_log.reshape(-1)).at[1, :N_CHAIN].set(gdn_dt_bias.reshape(-1))
    w_c, u_c, q_dec, k_dec_t, qk_f, qk_b, egl = _gdn_prep(yn, ab.reshape(b, s, LANES), gate_par)
    o_f, o_b = _gdn_scan(egl[..., 0].reshape(-1), w_c, u_c, q_dec, k_dec_t, qk_f, qk_b)

    out = _out_ffn(x2, attn, o_f.reshape(n, GDN_W), o_b.reshape(n, GDN_W), z_g,
                   gdn_norm_w.reshape(1, GDN_HEAD_DIM), wo_bf, norm_ffn_w.reshape(1, D_MODEL),
                   w1_bf, w2_bf, out_norm_w.reshape(1, D_MODEL))
    return out.reshape(b, s, D_MODEL)


def kernel(x, norm_mix_w, w_in, rel_bias, attn_sink, conv_w, gdn_a_log, gdn_dt_bias, gdn_norm_w, w_out,
           norm_ffn_w, w_ffn_in, w_ffn_out, norm_final_w):
    depth = w_in.shape[0]
    assert depth == 1, "the fused output kernel applies the final norm after the single trunk layer"
    rel = (np.arange(3 * BLOCK)[None, :] - BLOCK) - np.arange(BLOCK)[:, None]
    bucket = _t5_buckets(jnp.asarray(rel, dtype=jnp.int32))
    band = _bias_band(rel_bias, bucket.T)
    return _layer(x, band, norm_mix_w[0], w_in, 0, attn_sink[0], conv_w[0], gdn_a_log[0], gdn_dt_bias[0],
                  gdn_norm_w[0], w_out, norm_ffn_w[0], w_ffn_in, w_ffn_out, norm_final_w)
```

```python
import functools
import math

import jax
import jax.numpy as jnp
import numpy as np
from jax import lax
from jax.experimental import pallas as pl
from jax.experimental.pallas import tpu as pltpu

F32 = jnp.float32
BF16 = jnp.bfloat16

D_MODEL = 1024
ATTN_HEADS = 8
ATTN_KV_HEADS = 2
ATTN_HEAD_DIM = 64
ATTN_GROUP = ATTN_HEADS // ATTN_KV_HEADS
WINDOW = 128
BLOCK = 128
N_BUCKETS = 32
MAX_DISTANCE = 128
GDN_HEADS = 4
GDN_HEAD_DIM = 128
CONV_K = 5
CHUNK = 64
N_DIR = 2
N_CHAIN = N_DIR * GDN_HEADS
D_FF = 4 * D_MODEL
EPS = 1e-6
LOG2E = math.log2(math.e)
ATTN_Q = ATTN_HEADS * ATTN_HEAD_DIM
ATTN_KV = ATTN_KV_HEADS * ATTN_HEAD_DIM
GDN_W = GDN_HEADS * GDN_HEAD_DIM
LANES = 128
SUBLANES = 8
BF16_SUBLANES = 16
VMEM_LIMIT = 56 * 1024 * 1024

PROJ_TM = 1024
PROJ_SUB = 256
PROJ_CAST_ROWS = 256
PROJ_CAST_SLOTS = 4
ATTN_QB = 8
PREP_T = 1024
ROW_STRIDE = 4
SCAN_CHUNKS = 8
FFN_TM = 512
FFN_SUB = 256


def _dot(a, b):
    return jnp.dot(a.astype(BF16), b.astype(BF16), preferred_element_type=F32)


def _dot_nt(a, b):
    return lax.dot_general(a.astype(BF16), b.astype(BF16), (((1,), (1,)), ((), ())),
                           preferred_element_type=F32)


def _split3(x):
    hi = x.astype(BF16)
    r1 = x - hi.astype(F32)
    mid = r1.astype(BF16)
    lo = (r1 - mid.astype(F32)).astype(BF16)
    return hi, mid, lo


def _dot01_right(x, m01):
    hi, mid, lo = _split3(x)
    d = lambda p: jnp.dot(p, m01, preferred_element_type=F32)
    return d(hi) + d(mid) + d(lo)


def _conv_silu_norm(xe_ref, cw_ref, yn_ref, r0, n_out):
    halo = CONV_K // 2
    n_rows = n_out // ROW_STRIDE
    n_slab = 3 * GDN_W // LANES
    units = [(sb, ph) for sb in range(n_slab) for ph in range(ROW_STRIDE)]
    half_taps = [cw_ref[j:j + 1, :] * 0.5 for j in range(CONV_K)]
    yv = {}
    for sb, ph in units:
        lanes = slice(sb * LANES, (sb + 1) * LANES)
        acc = None
        for j in range(CONV_K):
            win = xe_ref[sb, pl.ds(SUBLANES - halo + j + ph, n_rows, stride=ROW_STRIDE), :]
            term = half_taps[j][:, lanes] * win
            acc = term if acc is None else acc + term
        yv[sb, ph] = acc
    for key in units:
        yv[key] = yv[key] + yv[key] * jnp.tanh(yv[key])
    for sb, ph in units:
        if sb < 2 * GDN_HEADS:
            scale = lax.rsqrt(jnp.sum(yv[sb, ph] * yv[sb, ph], axis=-1, keepdims=True) + EPS)
            if sb < GDN_HEADS:
                scale = scale * (GDN_HEAD_DIM ** -0.5)
            yv[sb, ph] = yv[sb, ph] * scale
    for sb, ph in units:
        yn_ref[sb, pl.ds(r0 + ph, n_rows, stride=ROW_STRIDE), :] = yv[sb, ph]


def _proj_kernel(layer, tiles_per_seq, x_ref, xp_ref, xn_ref, nw_ref, wt_hbm, cw_ref,
                 qa_ref, ka_ref, va_ref, yn_ref, z_ref, ab_ref, wb_ref, stage_ref, sem, *xe_refs):
    d_in = wt_hbm.shape[1]
    d_main = d_in // LANES * LANES
    step = pl.program_id(0)

    @pl.when(step == 0)
    def _():
        starts = list(range(0, d_main, PROJ_CAST_ROWS)) + [d_main]
        sizes = [min(PROJ_CAST_ROWS, d_main - c0) for c0 in starts[:-1]] + [d_in - d_main]
        slot = lambda k: k % PROJ_CAST_SLOTS
        copies = [pltpu.make_async_copy(wt_hbm.at[layer, pl.ds(c0, nr)], stage_ref.at[slot(k), pl.ds(0, nr)],
                                        sem.at[slot(k)]) for k, (c0, nr) in enumerate(zip(starts, sizes))]
        lane = lax.broadcasted_iota(jnp.int32, (D_MODEL, PROJ_CAST_ROWS), 1)
        for k in range(min(PROJ_CAST_SLOTS - 1, len(copies))):
            copies[k].start()
        for k, (c0, nr) in enumerate(zip(starts, sizes)):
            if k + PROJ_CAST_SLOTS - 1 < len(copies):
                copies[k + PROJ_CAST_SLOTS - 1].start()
            copies[k].wait()
            cols = stage_ref[slot(k)].T
            if nr < PROJ_CAST_ROWS:
                cols = jnp.where(lane < nr, cols, 0.0)
            width = min(PROJ_CAST_ROWS, wb_ref.shape[1] - c0)
            wb_ref[:, c0:c0 + width] = cols[:, :width].astype(BF16)

    o_q, o_k, o_v, o_g, o_z, o_ab = 0, ATTN_Q, ATTN_Q + ATTN_KV, ATTN_Q + 2 * ATTN_KV, \
        ATTN_Q + 2 * ATTN_KV + 3 * GDN_W, ATTN_Q + 2 * ATTN_KV + 4 * GDN_W
    n_slab = 3 * GDN_W // LANES

    def normed(x):
        ms = jnp.mean(x * x, axis=-1, keepdims=True)
        return (x * lax.rsqrt(ms + EPS) * nw_ref[...]).astype(BF16)

    n_sub = PROJ_TM // PROJ_SUB

    def to_slabs(xe_ref, y, row0):
        for sb in range(n_slab):
            xe_ref[sb, row0:row0 + y.shape[0], :] = y[:, sb * LANES:(sb + 1) * LANES]

    def put(k, y):
        to_slabs(xe_refs[k], y, SUBLANES)
        if k > 0:
            to_slabs(xe_refs[k - 1], y[:SUBLANES], SUBLANES + PROJ_SUB)
        if k + 1 < n_sub:
            to_slabs(xe_refs[k + 1], y[PROJ_SUB - SUBLANES:], 0)

    i = step % tiles_per_seq
    outs = ((qa_ref, o_q, ATTN_Q, ATTN_HEAD_DIM ** -0.5 * LOG2E), (ka_ref, o_k, ATTN_KV, None), (va_ref, o_v, ATTN_KV, None),
            (z_ref, o_z, GDN_W, None), (ab_ref, o_ab, LANES, None))
    for k in range(n_sub):
        rows = slice(k * PROJ_SUB, (k + 1) * PROJ_SUB)
        if k == 0:
            xg = normed(jnp.concatenate([xp_ref[...], xn_ref[...], x_ref[rows, :]], axis=0))
            yg = jnp.dot(xg, wb_ref[:, o_g:o_z], preferred_element_type=F32)
            to_slabs(xe_refs[0], jnp.where(i > 0, yg[:SUBLANES], 0.0), 0)
            to_slabs(xe_refs[n_sub - 1], jnp.where(i < tiles_per_seq - 1, yg[SUBLANES:2 * SUBLANES], 0.0),
                     SUBLANES + PROJ_SUB)
            put(0, yg[2 * SUBLANES:])
            xn = xg[2 * SUBLANES:]
        else:
            xn = normed(x_ref[rows, :])
            put(k, jnp.dot(xn, wb_ref[:, o_g:o_z], preferred_element_type=F32))
            _conv_silu_norm(xe_refs[k - 1], cw_ref, yn_ref, (k - 1) * PROJ_SUB, PROJ_SUB)
        for ref, c0, width, scale in outs:
            y = jnp.dot(xn, wb_ref[:, c0:c0 + width], preferred_element_type=F32)
            if scale is not None:
                y = y * scale
            ref[rows, :] = y.astype(ref.dtype)
    _conv_silu_norm(xe_refs[n_sub - 1], cw_ref, yn_ref, PROJ_TM - PROJ_SUB, PROJ_SUB)


def _proj(x2, norm_w, w_in, conv_w, layer, seq_len):
    n = x2.shape[0]
    d_in = w_in.shape[2]
    d_pad = d_in // LANES * LANES + LANES
    assert d_pad == ATTN_Q + 2 * ATTN_KV + 4 * GDN_W + LANES and d_in - (d_pad - LANES) == 2 * N_CHAIN
    assert seq_len % PROJ_TM == 0
    w_t = jnp.swapaxes(w_in, 1, 2)
    nh8 = PROJ_TM // SUBLANES
    row = lambda w: pl.BlockSpec((PROJ_TM, w), lambda i: (i, 0))
    return pl.pallas_call(
        functools.partial(_proj_kernel, layer, seq_len // PROJ_TM),
        grid=(n // PROJ_TM,),
        in_specs=[row(D_MODEL),
                  pl.BlockSpec((SUBLANES, D_MODEL), lambda i: (jnp.maximum(i * nh8 - 1, 0), 0)),
                  pl.BlockSpec((SUBLANES, D_MODEL), lambda i: (jnp.minimum((i + 1) * nh8, n // SUBLANES - 1), 0)),
                  pl.BlockSpec((1, D_MODEL), lambda i: (0, 0)),
                  pl.BlockSpec(memory_space=pl.ANY),
                  pl.BlockSpec((SUBLANES, 3 * GDN_W), lambda i: (0, 0))],
        out_specs=[row(ATTN_Q), row(ATTN_KV), row(ATTN_KV),
                   pl.BlockSpec((3 * GDN_W // LANES, PROJ_TM, LANES), lambda i: (0, i, 0)),
                   row(GDN_W), row(LANES)],
        out_shape=[jax.ShapeDtypeStruct((n, ATTN_Q), BF16),
                   jax.ShapeDtypeStruct((n, ATTN_KV), BF16),
                   jax.ShapeDtypeStruct((n, ATTN_KV), BF16),
                   jax.ShapeDtypeStruct((3 * GDN_W // LANES, n, LANES), F32),
                   jax.ShapeDtypeStruct((n, GDN_W), F32),
                   jax.ShapeDtypeStruct((n, LANES), F32)],
        scratch_shapes=[pltpu.VMEM((D_MODEL, d_pad), BF16),
                        pltpu.VMEM((PROJ_CAST_SLOTS, PROJ_CAST_ROWS, D_MODEL), F32),
                        pltpu.SemaphoreType.DMA((PROJ_CAST_SLOTS,))]
        + [pltpu.VMEM((3 * GDN_W // LANES, PROJ_SUB + 2 * SUBLANES, LANES), F32)] * (PROJ_TM // PROJ_SUB),
        compiler_params=pltpu.CompilerParams(dimension_semantics=("arbitrary",),
                                             vmem_limit_bytes=VMEM_LIMIT),
        name="proj",
    )(x2, x2, x2, norm_w, w_t, conv_w)


def _bias_kernel(relb_ref, bucket_ref, o_ref):
    bucket = bucket_ref[...]
    key = lax.broadcasted_iota(jnp.int32, (3 * BLOCK, BLOCK), 0)
    qry = lax.broadcasted_iota(jnp.int32, (3 * BLOCK, BLOCK), 1)
    in_window = jnp.abs(key - BLOCK - qry) <= WINDOW
    acc = [jnp.zeros((3 * BLOCK, BLOCK), F32) for _ in range(ATTN_HEADS)]
    for b in range(N_BUCKETS):
        hit = bucket == b
        acc = [jnp.where(hit, relb_ref[b, h], acc[h]) for h in range(ATTN_HEADS)]
    for h in range(ATTN_HEADS):
        o_ref[h // 2, :, (h % 2) * BLOCK:(h % 2 + 1) * BLOCK] = jnp.where(in_window, acc[h] * LOG2E, -1e30)


def _bias_band(rel_bias, bucket_t):
    shape = (ATTN_HEADS // 2, 3 * BLOCK, 2 * BLOCK)
    return pl.pallas_call(
        _bias_kernel,
        in_specs=[pl.BlockSpec(memory_space=pltpu.SMEM),
                  pl.BlockSpec((3 * BLOCK, BLOCK), lambda: (0, 0))],
        out_specs=pl.BlockSpec(shape, lambda: (0, 0, 0)),
        out_shape=jax.ShapeDtypeStruct(shape, F32),
        name="bias_band",
    )(rel_bias, bucket_t)


def _t5_buckets(rel):
    nb = N_BUCKETS // 2
    max_exact = nb // 2
    base = jnp.where(rel > 0, nb, 0)
    n = jnp.abs(rel)
    log_ratio = jnp.log(jnp.maximum(n, 1).astype(jnp.float32) / max_exact) / math.log(MAX_DISTANCE / max_exact)
    large = jnp.minimum(max_exact + (log_ratio * (nb - max_exact)).astype(jnp.int32), nb - 1)
    return base + jnp.where(n < max_exact, n, large)


def _attn_kernel(sink_ref, q_ref, kp_ref, kc_ref, kn_ref, vp_ref, vc_ref, vn_ref, bias_ref, wo_ref, w1_ref, w2_ref,
                 o_ref, wo_bf_ref, w1_bf_ref, w2_bf_ref):
    for src, dst in ((wo_ref, wo_bf_ref), (w1_ref, w1_bf_ref), (w2_ref, w2_bf_ref)):
        dst[...] = src[0].astype(dst.dtype)
    n = pl.program_id(1)
    last = pl.num_programs(1) - 1
    kband = jnp.concatenate([kp_ref[...], kc_ref[...], kn_ref[...]], axis=0)
    vband = jnp.concatenate([vp_ref[...], vc_ref[...], vn_ref[...]], axis=0)
    vband_t = vband.astype(F32).T.astype(BF16)
    key = lax.broadcasted_iota(jnp.int32, (3 * BLOCK, 1), 0)
    first_head = lax.broadcasted_iota(jnp.int32, (1, 2 * BLOCK), 1) < BLOCK
    head = lambda t, i: t[:, i * ATTN_HEAD_DIM:(i + 1) * ATTN_HEAD_DIM]
    n_pairs = ATTN_HEADS // 2
    kv_of = lambda pr: (2 * pr) // ATTN_GROUP
    units = [(j, pr) for j in range(ATTN_QB) for pr in range(n_pairs)]
    band_rows = lambda j: slice(j * BLOCK, (j + 3) * BLOCK)

    scores = {}
    for j, pr in units:
        qj = q_ref[j * BLOCK:(j + 1) * BLOCK, :]
        q2 = jnp.concatenate([head(qj, 2 * pr), head(qj, 2 * pr + 1)], axis=0)
        scores[j, pr] = _dot_nt(head(kband, kv_of(pr))[band_rows(j)], q2)
    probs, dens = {}, {}
    for j, pr in units:
        s = scores[j, pr] + bias_ref[pr]
        if j == 0:
            s = jnp.where((key < BLOCK) & (n == 0), -1e30, s)
        if j == ATTN_QB - 1:
            s = jnp.where((key >= 2 * BLOCK) & (n == last), -1e30, s)
        sink = jnp.where(first_head, sink_ref[2 * pr], sink_ref[2 * pr + 1]) * LOG2E
        m = jnp.maximum(jnp.max(s, axis=0, keepdims=True), sink)
        probs[j, pr] = jnp.exp2(s - m).astype(BF16)
        dens[j, pr] = jnp.exp2(sink - m)
    outs_t = {}
    ones_rows = jnp.ones((2 * SUBLANES, 3 * BLOCK), BF16)
    for j, pr in units:
        kv = kv_of(pr)
        v_t = vband_t[kv * ATTN_HEAD_DIM:(kv + 1) * ATTN_HEAD_DIM, band_rows(j)]
        pv = jnp.dot(jnp.concatenate([v_t, ones_rows], axis=0), probs[j, pr], preferred_element_type=F32)
        outs_t[j, pr] = pv[:ATTN_HEAD_DIM] / (pv[ATTN_HEAD_DIM:ATTN_HEAD_DIM + 1] + dens[j, pr])
    for j in range(ATTN_QB):
        o_t = jnp.concatenate([outs_t[j, pr][:, half * BLOCK:(half + 1) * BLOCK]
                               for pr in range(n_pairs) for half in range(2)], axis=0)
        o_ref[j * BLOCK:(j + 1) * BLOCK, :] = o_t.T.astype(o_ref.dtype)


def _attention(q_a, k_a, v_a, band, sink, batch, layer, weights):
    n_tok = q_a.shape[0]
    nb = n_tok // batch // BLOCK
    rows = ATTN_QB * BLOCK
    steps = nb // ATTN_QB
    kv_spec = lambda r, f: pl.BlockSpec((r, ATTN_KV), f)
    prev = lambda bi, n: (bi * nb + jnp.maximum(n * ATTN_QB - 1, 0), 0)
    cur = lambda bi, n: (bi * steps + n, 0)
    nxt = lambda bi, n: (bi * nb + jnp.minimum((n + 1) * ATTN_QB, nb - 1), 0)
    n_steps = batch * steps
    assert all(w.shape[1] % (n_steps * BF16_SUBLANES) == 0 for w in weights)
    w_rows = [w.shape[1] // n_steps for w in weights]
    w_in_specs = [pl.BlockSpec((1, r, w.shape[2]), lambda bi, n: (layer, bi * steps + n, 0))
                  for r, w in zip(w_rows, weights)]
    w_out_specs = [pl.BlockSpec((r, w.shape[2]), lambda bi, n: (bi * steps + n, 0)) for r, w in zip(w_rows, weights)]
    return pl.pallas_call(
        _attn_kernel,
        grid=(batch, steps),
        in_specs=[pl.BlockSpec(memory_space=pltpu.SMEM),
                  pl.BlockSpec((rows, ATTN_Q), cur),
                  kv_spec(BLOCK, prev), kv_spec(rows, cur), kv_spec(BLOCK, nxt),
                  kv_spec(BLOCK, prev), kv_spec(rows, cur), kv_spec(BLOCK, nxt),
                  pl.BlockSpec(band.shape, lambda bi, n: (0, 0, 0))] + w_in_specs,
        out_specs=[pl.BlockSpec((rows, ATTN_Q), cur)] + w_out_specs,
        out_shape=[jax.ShapeDtypeStruct((n_tok, ATTN_Q), BF16)]
                  + [jax.ShapeDtypeStruct(w.shape[1:], BF16) for w in weights],
        compiler_params=pltpu.CompilerParams(dimension_semantics=("arbitrary", "arbitrary"),
                                             vmem_limit_bytes=VMEM_LIMIT),
        name="attn",
    )(sink, q_a, k_a, k_a, k_a, v_a, v_a, v_a, band, *weights)


def _gprep_kernel(yn_ref, ab_ref, gp_ref, w_ref, u_ref, qd_ref, kdt_ref, qkf_ref, qkb_ref, egl_ref):
    t_len = PREP_T

    ab = ab_ref[0]
    sp_in = ab + gp_ref[1:2, :]
    softplus = jnp.maximum(sp_in, 0.0) + jnp.log1p(jnp.exp(-jnp.abs(sp_in)))
    g = (-jnp.exp(gp_ref[0:1, :]) * LOG2E) * softplus
    beta = jax.nn.sigmoid(ab)

    r_t = lax.broadcasted_iota(jnp.int32, (t_len, t_len), 0)
    c_t = lax.broadcasted_iota(jnp.int32, (t_len, t_len), 1)
    same = (r_t // CHUNK) == (c_t // CHUNK)
    lower = jnp.where(same & (r_t >= c_t), 1.0, 0.0).astype(BF16)
    upper = jnp.where(same & (r_t <= c_t), 1.0, 0.0).astype(BF16)
    g_t = g.T[:2 * SUBLANES]
    cs_row = (_dot01_right(g_t, upper), _dot01_right(g_t, lower))
    pad_rows = jnp.zeros((LANES - 2 * SUBLANES, t_len), F32)
    cs_col = tuple(jnp.concatenate([r, pad_rows], axis=0).T for r in cs_row)
    beta_t = beta.T[:3 * SUBLANES]

    qs = [yn_ref[h] for h in range(GDN_HEADS)]
    ks = [yn_ref[GDN_HEADS + h] for h in range(GDN_HEADS)]
    vs = [yn_ref[2 * GDN_HEADS + h] for h in range(GDN_HEADS)]
    kts = [kh.T for kh in ks]

    r_c = lax.broadcasted_iota(jnp.int32, (CHUNK, LANES), 0)
    lane = lax.broadcasted_iota(jnp.int32, (CHUNK, LANES), 1)
    is_fwd = lane < CHUNK
    c_c = lane % CHUNK
    eye = jnp.where(r_c == c_c, 1.0, 0.0).astype(F32)
    incl = (is_fwd & (r_c >= c_c)) | (~is_fwd & (r_c <= c_c))
    strict = (is_fwd & (r_c > c_c)) | (~is_fwd & (r_c < c_c))
    r_d = lax.broadcasted_iota(jnp.int32, (2 * CHUNK, LANES), 0)
    c_d = lax.broadcasted_iota(jnp.int32, (2 * CHUNK, LANES), 1)
    same_dir = (r_d // CHUNK) == (c_d // CHUNK)
    level_mask = lambda s_, r_, c_: ((r_ // (2 * s_)) == (c_ // (2 * s_))) & ((r_ // s_) != (c_ // s_))
    stack2 = lambda t: jnp.concatenate([t, t], axis=0)

    n_chunks = t_len // CHUNK
    rows = lambda c: slice(c * CHUNK, (c + 1) * CHUNK)
    pairs = [(c, h) for c in range(n_chunks) for h in range(GDN_HEADS)]
    qkk = {}
    for c, h in pairs:
        k16 = ks[h][rows(c)].astype(BF16)
        qk16 = jnp.concatenate([qs[h][rows(c)].astype(BF16), k16], axis=0)
        qkk[c, h] = _dot_nt(qk16, stack2(k16))

    bcast = lambda col: jnp.broadcast_to(col, (CHUNK, LANES))
    brow, grow, glast, a_mat, t_mat, kv16, qk_even = {}, {}, {}, {}, {}, {}, {}
    for key in pairs:
        c, h = key
        g_full = {}
        for d in range(N_DIR):
            j = d * GDN_HEADS + h
            r_last = c * CHUNK + (CHUNK - 1 if d == 0 else 0)
            g_full[d] = bcast(cs_col[d][rows(c), j:j + 1])
            brow[key, d] = beta_t[SUBLANES + j:SUBLANES + j + 1, rows(c)]
            glast[key, d] = cs_col[d][r_last:r_last + 1, j:j + 1]
            grow[key, d] = cs_row[d][j:j + 1, rows(c)]
        gcol2 = jnp.where(is_fwd, g_full[0], g_full[1])
        grow2 = jnp.concatenate([grow[key, 0], grow[key, 1]], axis=1)
        brow2 = jnp.concatenate([brow[key, 0], brow[key, 1]], axis=1)
        decay = jnp.exp2(jnp.where(incl, gcol2 - grow2, -jnp.inf))
        a_mat[key] = jnp.where(strict, qkk[key][CHUNK:] * decay, 0.0) * brow2
        t_mat[key] = eye - jnp.where(level_mask(1, r_c, c_c), a_mat[key], 0.0)
        qk = qkk[key][:CHUNK] * decay * brow2
        if c % 2 == 0:
            qk_even[h] = qk
        else:
            pair_rows = rows(c // 2)
            qkf_ref[0, h, pair_rows, :] = jnp.where(is_fwd, qk_even[h], pltpu.roll(qk, CHUNK, 1)).astype(qkf_ref.dtype)
            qkb_ref[0, h, pair_rows, :] = jnp.where(is_fwd, pltpu.roll(qk_even[h], CHUNK, 1), qk).astype(qkb_ref.dtype)
        eg = [jnp.exp2(g_full[d]) for d in range(N_DIR)]
        for d in range(N_DIR):
            qd_ref[0, d * GDN_HEADS + h, rows(c), :] = (qs[h][rows(c)] * eg[d]).astype(qd_ref.dtype)
        kv16[key] = jnp.concatenate(
            [jnp.concatenate([ks[h][rows(c)] * eg[d], vs[h][rows(c)]], axis=1) for d in range(N_DIR)],
            axis=0).astype(BF16)

    same_dir16 = jnp.where(same_dir, 1.0, 0.0).astype(BF16)
    block_diag = lambda t16: stack2(t16) * same_dir16
    a_bd = {key: block_diag(a_mat[key].astype(BF16)) for key in pairs}
    s = 2
    while s < CHUNK:
        lvl = level_mask(s, r_c, c_c)
        t16 = {key: t_mat[key].astype(BF16) for key in pairs}
        x_mat = {key: jnp.dot(t16[key], a_bd[key], preferred_element_type=F32) for key in pairs}
        y_mat = {key: jnp.dot(x_mat[key].astype(BF16), block_diag(t16[key]), preferred_element_type=F32)
                 for key in pairs}
        t_mat = {key: t_mat[key] - jnp.where(lvl, y_mat[key], 0.0) for key in pairs}
        s *= 2

    wu = {}
    for key in pairs:
        c, h = key
        t_sel = block_diag(t_mat[key].astype(BF16))
        wu[key] = jnp.dot(t_sel, kv16[key], preferred_element_type=F32)

    for c in range(n_chunks):
        egl_rows = []
        for d in range(N_DIR):
            for h in range(GDN_HEADS):
                key = (c, h)
                j = d * GDN_HEADS + h
                wu_d = wu[key][d * CHUNK:(d + 1) * CHUNK]
                w_ref[0, j, rows(c), :] = wu_d[:, :GDN_HEAD_DIM].astype(w_ref.dtype)
                u_ref[0, j, rows(c), :] = wu_d[:, GDN_HEAD_DIM:].astype(u_ref.dtype)
                egl_rows.append(jnp.broadcast_to(jnp.exp2(glast[key, d]), (1, LANES)))
        egl_ref[0, c] = jnp.concatenate(egl_rows, axis=0)
    for cp in range(n_chunks // 2):
        for d in range(N_DIR):
            for h in range(GDN_HEADS):
                fac = jnp.concatenate([brow[(c, h), d] * jnp.exp2(glast[(c, h), d] - grow[(c, h), d])
                                       for c in (2 * cp, 2 * cp + 1)], axis=1)
                kdt = kts[h][:, 2 * cp * CHUNK:(2 * cp + 2) * CHUNK] * fac
                kdt_ref[0, d * GDN_HEADS + h, cp] = kdt.astype(kdt_ref.dtype)


def _gdn_prep(yn, ab, gate_par):
    b, s, _ = ab.shape
    nt = s // PREP_T
    cpb = PREP_T // CHUNK
    nc = s // CHUNK
    chain = lambda last: pl.BlockSpec((1, N_CHAIN, PREP_T, last), lambda bi, i: (bi, 0, i, 0))
    return pl.pallas_call(
        _gprep_kernel,
        grid=(b, nt),
        in_specs=[pl.BlockSpec((3 * GDN_W // LANES, PREP_T, LANES), lambda bi, i: (0, bi * nt + i, 0)),
                  pl.BlockSpec((1, PREP_T, LANES), lambda bi, i: (bi, i, 0)),
                  pl.BlockSpec((SUBLANES, LANES), lambda bi, i: (0, 0))],
        out_specs=[chain(GDN_HEAD_DIM), chain(GDN_HEAD_DIM), chain(GDN_HEAD_DIM),
                   pl.BlockSpec((1, N_CHAIN, cpb // 2, GDN_HEAD_DIM, 2 * CHUNK), lambda bi, i: (bi, 0, i, 0, 0)),
                   pl.BlockSpec((1, GDN_HEADS, PREP_T // 2, LANES), lambda bi, i: (bi, 0, i, 0)),
                   pl.BlockSpec((1, GDN_HEADS, PREP_T // 2, LANES), lambda bi, i: (bi, 0, i, 0)),
                   pl.BlockSpec((1, cpb, N_CHAIN, LANES), lambda bi, i: (bi, i, 0, 0))],
        out_shape=[jax.ShapeDtypeStruct((b, N_CHAIN, s, GDN_HEAD_DIM), BF16),
                   jax.ShapeDtypeStruct((b, N_CHAIN, s, GDN_HEAD_DIM), BF16),
                   jax.ShapeDtypeStruct((b, N_CHAIN, s, GDN_HEAD_DIM), BF16),
                   jax.ShapeDtypeStruct((b, N_CHAIN, nc // 2, GDN_HEAD_DIM, 2 * CHUNK), BF16),
                   jax.ShapeDtypeStruct((b, GDN_HEADS, s // 2, LANES), BF16),
                   jax.ShapeDtypeStruct((b, GDN_HEADS, s // 2, LANES), BF16),
                   jax.ShapeDtypeStruct((b, nc, N_CHAIN, LANES), F32)],
        compiler_params=pltpu.CompilerParams(dimension_semantics=("arbitrary", "arbitrary"),
                                             vmem_limit_bytes=VMEM_LIMIT),
        name="gdn_prep",
    )(yn, ab, gate_par)


def _gscan_kernel(ef_ref, eb_ref, wf_ref, wb_ref, uf_ref, ub_ref, qf_ref, qb_ref, kf_ref, kb_ref,
                  pf_ref, pb_ref, of_ref, ob_ref, state_ref):
    n_batch = wf_ref.shape[0]

    @pl.when(pl.program_id(0) == 0)
    def _():
        state_ref[...] = jnp.zeros_like(state_ref)

    dirs = ((wf_ref, uf_ref, qf_ref, kf_ref, pf_ref, of_ref), (wb_ref, ub_ref, qb_ref, kb_ref, pb_ref, ob_ref))
    chains = [(bi, d, h) for bi in range(n_batch) for d in range(N_DIR) for h in range(GDN_HEADS)]
    slot = lambda bi, d, h: (bi * N_DIR + d) * GDN_HEADS + h
    st = {key: state_ref[slot(*key)] for key in chains}

    for sub in range(SCAN_CHUNKS):
        local = (sub, SCAN_CHUNKS - 1 - sub)
        rows = [slice(c * CHUNK, (c + 1) * CHUNK) for c in local]
        r = {}
        for key in chains:
            bi, d, h = key
            w_ref, _, q_ref = dirs[d][:3]
            wq = jnp.concatenate([w_ref[bi, h, rows[d]], q_ref[bi, h, rows[d]]], axis=0)
            r[key] = _dot(wq, st[key])
        intra = {}
        for key in chains:
            bi, d, h = key
            u_ref, k_ref, p_ref = dirs[d][1], dirs[d][3], dirs[d][4]
            v_new = (u_ref[bi, h, rows[d]].astype(F32) - r[key][:CHUNK]).astype(BF16)
            zeros = jnp.zeros_like(v_new)
            v_pad = jnp.concatenate([v_new, zeros] if local[d] % 2 == 0 else [zeros, v_new], axis=0)
            pair_rows = slice(local[d] // 2 * CHUNK, (local[d] // 2 + 1) * CHUNK)
            pk = jnp.concatenate([p_ref[bi, h, pair_rows], k_ref[bi, h, local[d] // 2]], axis=0)
            res = jnp.dot(pk, v_pad, preferred_element_type=F32)
            intra[key] = res[:CHUNK]
            j = d * GDN_HEADS + h
            egl = (ef_ref, eb_ref)[d][bi, local[d], j:j + 1, :]
            st[key] = st[key] * egl + res[CHUNK:]
        for bi in range(n_batch):
            for d in range(N_DIR):
                o_ref = dirs[d][5]
                o_ref[bi, rows[d], :] = jnp.concatenate(
                    [r[bi, d, h][CHUNK:] + intra[bi, d, h] for h in range(GDN_HEADS)], axis=-1).astype(o_ref.dtype)

    for key in chains:
        state_ref[slot(*key)] = st[key]


def _gdn_scan(egl, w, u, qd, kdt, qk_f, qk_b):
    b, _, s, _ = w.shape
    rows = SCAN_CHUNKS * CHUNK
    n_steps = s // rows
    fwd_i = lambda t: t
    bwd_i = lambda t: n_steps - 1 - t
    chain = lambda d, at, last: pl.BlockSpec((b, GDN_HEADS, rows, last), lambda t: (0, d, at(t), 0))
    kspec = lambda d, at: pl.BlockSpec((b, GDN_HEADS, SCAN_CHUNKS // 2, GDN_HEAD_DIM, 2 * CHUNK),
                                       lambda t: (0, d, at(t), 0, 0))
    pspec = lambda at: pl.BlockSpec((b, GDN_HEADS, rows // 2, LANES), lambda t: (0, 0, at(t), 0))
    espec = lambda at: pl.BlockSpec((b, SCAN_CHUNKS, N_CHAIN, LANES), lambda t: (0, at(t), 0, 0))
    dk = GDN_HEAD_DIM
    return pl.pallas_call(
        _gscan_kernel,
        grid=(n_steps,),
        in_specs=[espec(fwd_i), espec(bwd_i),
                  chain(0, fwd_i, dk), chain(1, bwd_i, dk), chain(0, fwd_i, dk), chain(1, bwd_i, dk),
                  chain(0, fwd_i, dk), chain(1, bwd_i, dk), kspec(0, fwd_i), kspec(1, bwd_i),
                  pspec(fwd_i), pspec(bwd_i)],
        out_specs=[pl.BlockSpec((b, rows, GDN_W), lambda t: (0, fwd_i(t), 0)),
                   pl.BlockSpec((b, rows, GDN_W), lambda t: (0, bwd_i(t), 0))],
        out_shape=[jax.ShapeDtypeStruct((b, s, GDN_W), BF16),
                   jax.ShapeDtypeStruct((b, s, GDN_W), BF16)],
        scratch_shapes=[pltpu.VMEM((b * N_CHAIN, GDN_HEAD_DIM, GDN_HEAD_DIM), F32)],
        compiler_params=pltpu.CompilerParams(dimension_semantics=("arbitrary",),
                                             vmem_limit_bytes=VMEM_LIMIT),
        name="gdn_scan",
    )(egl, egl, w, w, u, u, qd, qd, kdt, kdt, qk_f, qk_b)


def _ffn_kernel(x_ref, attn_ref, of_ref, ob_ref, z_ref, gnw_ref, wo_hbm, fnw_ref, w1_hbm, w2_hbm, onw_ref,
                o_ref, wo_ref, w1_ref, w2_ref, sem):
    @pl.when(pl.program_id(0) == 0)
    def _():
        copies = [pltpu.make_async_copy(src, dst, sem.at[k])
                  for k, (src, dst) in enumerate(((wo_hbm, wo_ref), (w1_hbm, w1_ref), (w2_hbm, w2_ref)))]
        for copy in copies:
            copy.start()
        for copy in copies:
            copy.wait()

    subs = [slice(r0, r0 + FFN_SUB) for r0 in range(0, FFN_TM, FFN_SUB)]
    rms = lambda t: t * lax.rsqrt(jnp.mean(t * t, axis=-1, keepdims=True) + EPS)
    hres, hn, act, acc = {}, {}, {}, {}
    for r in subs:
        o = of_ref[r, :].astype(F32) + ob_ref[r, :].astype(F32)
        heads = [rms(o[:, h * GDN_HEAD_DIM:(h + 1) * GDN_HEAD_DIM]) * gnw_ref[...] for h in range(GDN_HEADS)]
        z = z_ref[r, :]
        gdn = jnp.concatenate(heads, axis=-1) * (z * jax.nn.sigmoid(z))
        hres[r.start] = (x_ref[r, :] + jnp.dot(attn_ref[r, :], wo_ref[:ATTN_Q, :], preferred_element_type=F32)
                         + _dot(gdn, wo_ref[ATTN_Q:, :]))
    for r in subs:
        hn[r.start] = (rms(hres[r.start]) * fnw_ref[...]).astype(BF16)
        act[r.start] = jnp.dot(hn[r.start], w1_ref[...], preferred_element_type=F32)
    for r in subs:
        a = jnp.square(jnp.maximum(act[r.start], 0.0)).astype(BF16)
        acc[r.start] = hres[r.start] + jnp.dot(a, w2_ref[...], preferred_element_type=F32)
    for r in subs:
        o_ref[r, :] = rms(acc[r.start]) * onw_ref[...]


def _out_ffn(x2, attn, o_f, o_b, z, gnw, wo, fnw, w1, w2, onw):
    n = x2.shape[0]
    assert wo.shape == (D_MODEL, D_MODEL) and w1.shape == (D_MODEL, D_FF) and w2.shape == (D_FF, D_MODEL)
    row = lambda w: pl.BlockSpec((FFN_TM, w), lambda i: (i, 0))
    full = lambda a: pl.BlockSpec(a.shape, lambda i: (0, 0))
    hbm = pl.BlockSpec(memory_space=pl.ANY)
    return pl.pallas_call(
        _ffn_kernel,
        grid=(n // FFN_TM,),
        in_specs=[row(D_MODEL), row(ATTN_Q), row(GDN_W), row(GDN_W), row(GDN_W),
                  full(gnw), hbm, full(fnw), hbm, hbm, full(onw)],
        out_specs=row(D_MODEL),
        out_shape=jax.ShapeDtypeStruct((n, D_MODEL), F32),
        scratch_shapes=[pltpu.VMEM((D_MODEL, D_MODEL), BF16),
                        pltpu.VMEM((D_MODEL, D_FF), BF16),
                        pltpu.VMEM((D_FF, D_MODEL), BF16),
                        pltpu.SemaphoreType.DMA((3,))],
        compiler_params=pltpu.CompilerParams(dimension_semantics=("arbitrary",),
                                             vmem_limit_bytes=VMEM_LIMIT),
        name="out_ffn",
    )(x2, attn, o_f, o_b, z, gnw, wo, fnw, w1, w2, onw)


def _layer(h, band, norm_mix_w, w_in, layer, attn_sink, conv_w, gdn_a_log, gdn_dt_bias, gdn_norm_w,
           w_out, norm_ffn_w, w_ffn_in, w_ffn_out, out_norm_w):
    b, s, _ = h.shape
    n = b * s
    x2 = h.reshape(n, D_MODEL)
    conv_pad = jnp.zeros((SUBLANES, 3 * GDN_W), F32).at[:CONV_K].set(conv_w)
    q_a, k_a, v_a, yn, z_g, ab = _proj(x2, norm_mix_w.reshape(1, D_MODEL), w_in, conv_pad, layer, s)

    attn, wo_bf, w1_bf, w2_bf = _attention(q_a, k_a, v_a, band, attn_sink, b, layer, (w_out, w_ffn_in, w_ffn_out))

    gate_par = jnp.zeros((SUBLANES, LANES), F32)
    gate_par = gate_par.at[0, :N_CHAIN].set(gdn_a_log.reshape(-1)).at[1, :N_CHAIN].set(gdn_dt_bias.reshape(-1))
    w_c, u_c, q_dec, k_dec_t, qk_f, qk_b, egl = _gdn_prep(yn, ab.reshape(b, s, LANES), gate_par)
    o_f, o_b = _gdn_scan(egl, w_c, u_c, q_dec, k_dec_t, qk_f, qk_b)

    out = _out_ffn(x2, attn, o_f.reshape(n, GDN_W), o_b.reshape(n, GDN_W), z_g,
                   gdn_norm_w.reshape(1, GDN_HEAD_DIM), wo_bf, norm_ffn_w.reshape(1, D_MODEL),
                   w1_bf, w2_bf, out_norm_w.reshape(1, D_MODEL))
    return out.reshape(b, s, D_MODEL)


def kernel(x, norm_mix_w, w_in, rel_bias, attn_sink, conv_w, gdn_a_log, gdn_dt_bias, gdn_norm_w, w_out,
           norm_ffn_w, w_ffn_in, w_ffn_out, norm_final_w):
    depth = w_in.shape[0]
    assert depth == 1, "the fused output kernel applies the final norm after the single trunk layer"
    rel = (np.arange(3 * BLOCK)[None, :] - BLOCK) - np.arange(BLOCK)[:, None]
    bucket = _t5_buckets(jnp.asarray(rel, dtype=jnp.int32))
    band = _bias_band(rel_bias, bucket.T)
    return _layer(x, band, norm_mix_w[0], w_in, 0, attn_sink[0], conv_w[0], gdn_a_log[0], gdn_dt_bias[0],
                  gdn_norm_w[0], w_out, norm_ffn_w[0], w_ffn_in, w_ffn_out, norm_final_w)
```

```python
import functools
import math

import jax
import jax.numpy as jnp
import numpy as np
from jax import lax
from jax.experimental import pallas as pl
from jax.experimental.pallas import tpu as pltpu

F32 = jnp.float32
BF16 = jnp.bfloat16

D_MODEL = 1024
ATTN_HEADS = 8
ATTN_KV_HEADS = 2
ATTN_HEAD_DIM = 64
ATTN_GROUP = ATTN_HEADS // ATTN_KV_HEADS
WINDOW = 128
BLOCK = 128
N_BUCKETS = 32
MAX_DISTANCE = 128
GDN_HEADS = 4
GDN_HEAD_DIM = 128
CONV_K = 5
CHUNK = 64
N_DIR = 2
N_CHAIN = N_DIR * GDN_HEADS
D_FF = 4 * D_MODEL
EPS = 1e-6
LOG2E = math.log2(math.e)
ATTN_Q = ATTN_HEADS * ATTN_HEAD_DIM
ATTN_KV = ATTN_KV_HEADS * ATTN_HEAD_DIM
GDN_W = GDN_HEADS * GDN_HEAD_DIM
LANES = 128
SUBLANES = 8
BF16_SUBLANES = 16
VMEM_LIMIT = 56 * 1024 * 1024

PROJ_TM = 1024
PROJ_SUB = 256
PROJ_CAST_ROWS = 256
PROJ_CAST_SLOTS = 4
ATTN_QB = 8
PREP_T = 1024
ROW_STRIDE = 4
SCAN_CHUNKS = 8
FFN_TM = 512
FFN_SUB = 256


def _dot(a, b):
    return jnp.dot(a.astype(BF16), b.astype(BF16), preferred_element_type=F32)


def _dot_nt(a, b):
    return lax.dot_general(a.astype(BF16), b.astype(BF16), (((1,), (1,)), ((), ())),
                           preferred_element_type=F32)


def _split3(x):
    hi = x.astype(BF16)
    r1 = x - hi.astype(F32)
    mid = r1.astype(BF16)
    lo = (r1 - mid.astype(F32)).astype(BF16)
    return hi, mid, lo


def _dot01_right(x, m01):
    hi, mid, lo = _split3(x)
    d = lambda p: jnp.dot(p, m01, preferred_element_type=F32)
    return d(hi) + d(mid) + d(lo)


def _conv_silu_norm(xe_ref, cw_ref, yn_ref, r0, n_out):
    halo = CONV_K // 2
    n_rows = n_out // ROW_STRIDE
    n_slab = 3 * GDN_W // LANES
    units = [(sb, ph) for sb in range(n_slab) for ph in range(ROW_STRIDE)]
    half_taps = [cw_ref[j] * 0.5 for j in range(CONV_K)]
    yv = {}
    for sb, ph in units:
        lanes = slice(sb * LANES, (sb + 1) * LANES)
        acc = None
        for j in range(CONV_K):
            win = xe_ref[sb, pl.ds(SUBLANES - halo + j + ph, n_rows, stride=ROW_STRIDE), :]
            term = half_taps[j][:, lanes] * win
            acc = term if acc is None else acc + term
        yv[sb, ph] = acc
    for key in units:
        yv[key] = yv[key] + yv[key] * jnp.tanh(yv[key])
    for sb, ph in units:
        if sb < 2 * GDN_HEADS:
            scale = lax.rsqrt(jnp.sum(yv[sb, ph] * yv[sb, ph], axis=-1, keepdims=True) + EPS)
            if sb < GDN_HEADS:
                scale = scale * (GDN_HEAD_DIM ** -0.5)
            yv[sb, ph] = yv[sb, ph] * scale
    for sb, ph in units:
        yn_ref[sb, pl.ds(r0 + ph, n_rows, stride=ROW_STRIDE), :] = yv[sb, ph]


def _proj_kernel(layer, tiles_per_seq, x_ref, xp_ref, xn_ref, nw_ref, wt_hbm, cw_ref,
                 qa_ref, ka_ref, va_ref, yn_ref, z_ref, ab_ref, wb_ref, stage_ref, sem, *xe_refs):
    d_in = wt_hbm.shape[1]
    d_main = d_in // LANES * LANES
    step = pl.program_id(0)

    @pl.when(step == 0)
    def _():
        starts = list(range(0, d_main, PROJ_CAST_ROWS)) + [d_main]
        sizes = [min(PROJ_CAST_ROWS, d_main - c0) for c0 in starts[:-1]] + [d_in - d_main]
        slot = lambda k: k % PROJ_CAST_SLOTS
        copies = [pltpu.make_async_copy(wt_hbm.at[layer, pl.ds(c0, nr)], stage_ref.at[slot(k), pl.ds(0, nr)],
                                        sem.at[slot(k)]) for k, (c0, nr) in enumerate(zip(starts, sizes))]
        lane = lax.broadcasted_iota(jnp.int32, (D_MODEL, PROJ_CAST_ROWS), 1)
        for k in range(min(PROJ_CAST_SLOTS - 1, len(copies))):
            copies[k].start()
        for k, (c0, nr) in enumerate(zip(starts, sizes)):
            if k + PROJ_CAST_SLOTS - 1 < len(copies):
                copies[k + PROJ_CAST_SLOTS - 1].start()
            copies[k].wait()
            cols = stage_ref[slot(k)].T
            if nr < PROJ_CAST_ROWS:
                cols = jnp.where(lane < nr, cols, 0.0)
            width = min(PROJ_CAST_ROWS, wb_ref.shape[1] - c0)
            wb_ref[:, c0:c0 + width] = cols[:, :width].astype(BF16)

    o_q, o_k, o_v, o_g, o_z, o_ab = 0, ATTN_Q, ATTN_Q + ATTN_KV, ATTN_Q + 2 * ATTN_KV, \
        ATTN_Q + 2 * ATTN_KV + 3 * GDN_W, ATTN_Q + 2 * ATTN_KV + 4 * GDN_W
    n_slab = 3 * GDN_W // LANES

    def normed(x):
        ms = jnp.mean(x * x, axis=-1, keepdims=True)
        return (x * lax.rsqrt(ms + EPS) * nw_ref[...]).astype(BF16)

    n_sub = PROJ_TM // PROJ_SUB

    def to_slabs(xe_ref, y, row0):
        for sb in range(n_slab):
            xe_ref[sb, row0:row0 + y.shape[0], :] = y[:, sb * LANES:(sb + 1) * LANES]

    def put(k, y):
        to_slabs(xe_refs[k], y, SUBLANES)
        if k > 0:
            to_slabs(xe_refs[k - 1], y[:SUBLANES], SUBLANES + PROJ_SUB)
        if k + 1 < n_sub:
            to_slabs(xe_refs[k + 1], y[PROJ_SUB - SUBLANES:], 0)

    i = step % tiles_per_seq
    outs = ((qa_ref, o_q, ATTN_Q, ATTN_HEAD_DIM ** -0.5 * LOG2E), (ka_ref, o_k, ATTN_KV, None), (va_ref, o_v, ATTN_KV, None),
            (z_ref, o_z, GDN_W, None), (ab_ref, o_ab, LANES, None))
    for k in range(n_sub):
        rows = slice(k * PROJ_SUB, (k + 1) * PROJ_SUB)
        if k == 0:
            xg = normed(jnp.concatenate([xp_ref[...], xn_ref[...], x_ref[rows, :]], axis=0))
            yg = jnp.dot(xg, wb_ref[:, o_g:o_z], preferred_element_type=F32)
            to_slabs(xe_refs[0], jnp.where(i > 0, yg[:SUBLANES], 0.0), 0)
            to_slabs(xe_refs[n_sub - 1], jnp.where(i < tiles_per_seq - 1, yg[SUBLANES:2 * SUBLANES], 0.0),
                     SUBLANES + PROJ_SUB)
            put(0, yg[2 * SUBLANES:])
            xn = xg[2 * SUBLANES:]
        else:
            xn = normed(x_ref[rows, :])
            put(k, jnp.dot(xn, wb_ref[:, o_g:o_z], preferred_element_type=F32))
            _conv_silu_norm(xe_refs[k - 1], cw_ref, yn_ref, (k - 1) * PROJ_SUB, PROJ_SUB)
        for ref, c0, width, scale in outs:
            y = jnp.dot(xn, wb_ref[:, c0:c0 + width], preferred_element_type=F32)
            if scale is not None:
                y = y * scale
            ref[rows, :] = y.astype(ref.dtype)
    _conv_silu_norm(xe_refs[n_sub - 1], cw_ref, yn_ref, PROJ_TM - PROJ_SUB, PROJ_SUB)


def _proj(x2, norm_w, w_in, conv_w, layer, seq_len):
    n = x2.shape[0]
    d_in = w_in.shape[2]
    d_pad = d_in // LANES * LANES + LANES
    assert d_pad == ATTN_Q + 2 * ATTN_KV + 4 * GDN_W + LANES and d_in - (d_pad - LANES) == 2 * N_CHAIN
    assert seq_len % PROJ_TM == 0
    w_t = jnp.swapaxes(w_in, 1, 2)
    nh8 = PROJ_TM // SUBLANES
    row = lambda w: pl.BlockSpec((PROJ_TM, w), lambda i: (i, 0))
    return pl.pallas_call(
        functools.partial(_proj_kernel, layer, seq_len // PROJ_TM),
        grid=(n // PROJ_TM,),
        in_specs=[row(D_MODEL),
                  pl.BlockSpec((SUBLANES, D_MODEL), lambda i: (jnp.maximum(i * nh8 - 1, 0), 0)),
                  pl.BlockSpec((SUBLANES, D_MODEL), lambda i: (jnp.minimum((i + 1) * nh8, n // SUBLANES - 1), 0)),
                  pl.BlockSpec((1, D_MODEL), lambda i: (0, 0)),
                  pl.BlockSpec(memory_space=pl.ANY),
                  pl.BlockSpec((CONV_K, 1, 3 * GDN_W), lambda i: (0, layer, 0))],
        out_specs=[row(ATTN_Q), row(ATTN_KV), row(ATTN_KV),
                   pl.BlockSpec((3 * GDN_W // LANES, PROJ_TM, LANES), lambda i: (0, i, 0)),
                   row(GDN_W), row(LANES)],
        out_shape=[jax.ShapeDtypeStruct((n, ATTN_Q), BF16),
                   jax.ShapeDtypeStruct((n, ATTN_KV), BF16),
                   jax.ShapeDtypeStruct((n, ATTN_KV), BF16),
                   jax.ShapeDtypeStruct((3 * GDN_W // LANES, n, LANES), F32),
                   jax.ShapeDtypeStruct((n, GDN_W), F32),
                   jax.ShapeDtypeStruct((n, LANES), F32)],
        scratch_shapes=[pltpu.VMEM((D_MODEL, d_pad), BF16),
                        pltpu.VMEM((PROJ_CAST_SLOTS, PROJ_CAST_ROWS, D_MODEL), F32),
                        pltpu.SemaphoreType.DMA((PROJ_CAST_SLOTS,))]
        + [pltpu.VMEM((3 * GDN_W // LANES, PROJ_SUB + 2 * SUBLANES, LANES), F32)] * (PROJ_TM // PROJ_SUB),
        compiler_params=pltpu.CompilerParams(dimension_semantics=("arbitrary",),
                                             vmem_limit_bytes=VMEM_LIMIT),
        name="proj",
    )(x2, x2, x2, norm_w, w_t, jnp.swapaxes(conv_w, 0, 1))


def _bias_kernel(relb_ref, bucket_ref, o_ref):
    bucket = bucket_ref[...]
    key = lax.broadcasted_iota(jnp.int32, (3 * BLOCK, BLOCK), 0)
    qry = lax.broadcasted_iota(jnp.int32, (3 * BLOCK, BLOCK), 1)
    in_window = jnp.abs(key - BLOCK - qry) <= WINDOW
    acc = [jnp.zeros((3 * BLOCK, BLOCK), F32) for _ in range(ATTN_HEADS)]
    for b in range(N_BUCKETS):
        hit = bucket == b
        acc = [jnp.where(hit, relb_ref[h, b], acc[h]) for h in range(ATTN_HEADS)]
    for h in range(ATTN_HEADS):
        o_ref[h // 2, :, (h % 2) * BLOCK:(h % 2 + 1) * BLOCK] = jnp.where(in_window, acc[h] * LOG2E, -1e30)


def _bias_band(rel_bias_t, bucket_t):
    shape = (ATTN_HEADS // 2, 3 * BLOCK, 2 * BLOCK)
    return pl.pallas_call(
        _bias_kernel,
        in_specs=[pl.BlockSpec(memory_space=pltpu.SMEM),
                  pl.BlockSpec((3 * BLOCK, BLOCK), lambda: (0, 0))],
        out_specs=pl.BlockSpec(shape, lambda: (0, 0, 0)),
        out_shape=jax.ShapeDtypeStruct(shape, F32),
        name="bias_band",
    )(rel_bias_t, bucket_t)


def _t5_buckets(rel):
    nb = N_BUCKETS // 2
    max_exact = nb // 2
    base = jnp.where(rel > 0, nb, 0)
    n = jnp.abs(rel)
    log_ratio = jnp.log(jnp.maximum(n, 1).astype(jnp.float32) / max_exact) / math.log(MAX_DISTANCE / max_exact)
    large = jnp.minimum(max_exact + (log_ratio * (nb - max_exact)).astype(jnp.int32), nb - 1)
    return base + jnp.where(n < max_exact, n, large)


def _attn_kernel(sink_ref, q_ref, kp_ref, kc_ref, kn_ref, vp_ref, vc_ref, vn_ref, bias_ref, wo_ref, w1_ref, w2_ref,
                 o_ref, wo_bf_ref, w1_bf_ref, w2_bf_ref):
    for src, dst in ((wo_ref, wo_bf_ref), (w1_ref, w1_bf_ref), (w2_ref, w2_bf_ref)):
        dst[...] = src[0].astype(dst.dtype)
    n = pl.program_id(1)
    last = pl.num_programs(1) - 1
    kband = jnp.concatenate([kp_ref[...], kc_ref[...], kn_ref[...]], axis=0)
    vband = jnp.concatenate([vp_ref[...], vc_ref[...], vn_ref[...]], axis=0)
    vband_t = vband.astype(F32).T.astype(BF16)
    key = lax.broadcasted_iota(jnp.int32, (3 * BLOCK, 1), 0)
    first_head = lax.broadcasted_iota(jnp.int32, (1, 2 * BLOCK), 1) < BLOCK
    head = lambda t, i: t[:, i * ATTN_HEAD_DIM:(i + 1) * ATTN_HEAD_DIM]
    n_pairs = ATTN_HEADS // 2
    kv_of = lambda pr: (2 * pr) // ATTN_GROUP
    units = [(j, pr) for j in range(ATTN_QB) for pr in range(n_pairs)]
    band_rows = lambda j: slice(j * BLOCK, (j + 3) * BLOCK)

    scores = {}
    for j, pr in units:
        qj = q_ref[j * BLOCK:(j + 1) * BLOCK, :]
        q2 = jnp.concatenate([head(qj, 2 * pr), head(qj, 2 * pr + 1)], axis=0)
        scores[j, pr] = _dot_nt(head(kband, kv_of(pr))[band_rows(j)], q2)
    probs, dens = {}, {}
    for j, pr in units:
        s = scores[j, pr] + bias_ref[pr]
        if j == 0:
            s = jnp.where((key < BLOCK) & (n == 0), -1e30, s)
        if j == ATTN_QB - 1:
            s = jnp.where((key >= 2 * BLOCK) & (n == last), -1e30, s)
        sink = jnp.where(first_head, sink_ref[2 * pr], sink_ref[2 * pr + 1]) * LOG2E
        m = jnp.maximum(jnp.max(s, axis=0, keepdims=True), sink)
        probs[j, pr] = jnp.exp2(s - m).astype(BF16)
        dens[j, pr] = jnp.exp2(sink - m)
    outs_t = {}
    ones_rows = jnp.ones((2 * SUBLANES, 3 * BLOCK), BF16)
    for j, pr in units:
        kv = kv_of(pr)
        v_t = vband_t[kv * ATTN_HEAD_DIM:(kv + 1) * ATTN_HEAD_DIM, band_rows(j)]
        pv = jnp.dot(jnp.concatenate([v_t, ones_rows], axis=0), probs[j, pr], preferred_element_type=F32)
        outs_t[j, pr] = pv[:ATTN_HEAD_DIM] / (pv[ATTN_HEAD_DIM:ATTN_HEAD_DIM + 1] + dens[j, pr])
    for j in range(ATTN_QB):
        o_t = jnp.concatenate([outs_t[j, pr][:, half * BLOCK:(half + 1) * BLOCK]
                               for pr in range(n_pairs) for half in range(2)], axis=0)
        o_ref[j * BLOCK:(j + 1) * BLOCK, :] = o_t.T.astype(o_ref.dtype)


def _attention(q_a, k_a, v_a, band, sink, batch, layer, weights):
    n_tok = q_a.shape[0]
    nb = n_tok // batch // BLOCK
    rows = ATTN_QB * BLOCK
    steps = nb // ATTN_QB
    kv_spec = lambda r, f: pl.BlockSpec((r, ATTN_KV), f)
    prev = lambda bi, n: (bi * nb + jnp.maximum(n * ATTN_QB - 1, 0), 0)
    cur = lambda bi, n: (bi * steps + n, 0)
    nxt = lambda bi, n: (bi * nb + jnp.minimum((n + 1) * ATTN_QB, nb - 1), 0)
    n_steps = batch * steps
    assert all(w.shape[1] % (n_steps * BF16_SUBLANES) == 0 for w in weights)
    w_rows = [w.shape[1] // n_steps for w in weights]
    w_in_specs = [pl.BlockSpec((1, r, w.shape[2]), lambda bi, n: (layer, bi * steps + n, 0))
                  for r, w in zip(w_rows, weights)]
    w_out_specs = [pl.BlockSpec((r, w.shape[2]), lambda bi, n: (bi * steps + n, 0)) for r, w in zip(w_rows, weights)]
    return pl.pallas_call(
        _attn_kernel,
        grid=(batch, steps),
        in_specs=[pl.BlockSpec(memory_space=pltpu.SMEM),
                  pl.BlockSpec((rows, ATTN_Q), cur),
                  kv_spec(BLOCK, prev), kv_spec(rows, cur), kv_spec(BLOCK, nxt),
                  kv_spec(BLOCK, prev), kv_spec(rows, cur), kv_spec(BLOCK, nxt),
                  pl.BlockSpec(band.shape, lambda bi, n: (0, 0, 0))] + w_in_specs,
        out_specs=[pl.BlockSpec((rows, ATTN_Q), cur)] + w_out_specs,
        out_shape=[jax.ShapeDtypeStruct((n_tok, ATTN_Q), BF16)]
                  + [jax.ShapeDtypeStruct(w.shape[1:], BF16) for w in weights],
        compiler_params=pltpu.CompilerParams(dimension_semantics=("arbitrary", "arbitrary"),
                                             vmem_limit_bytes=VMEM_LIMIT),
        name="attn",
    )(sink, q_a, k_a, k_a, k_a, v_a, v_a, v_a, band, *weights)


def _gprep_kernel(yn_ref, ab_ref, gp_ref, w_ref, u_ref, qd_ref, kdt_ref, qkf_ref, qkb_ref, egl_ref):
    t_len = PREP_T

    ab = ab_ref[0]
    sp_in = ab + gp_ref[1:2, :]
    softplus = jnp.maximum(sp_in, 0.0) + jnp.log1p(jnp.exp(-jnp.abs(sp_in)))
    g = (-jnp.exp(gp_ref[0:1, :]) * LOG2E) * softplus
    beta = jax.nn.sigmoid(ab)

    r_t = lax.broadcasted_iota(jnp.int32, (t_len, t_len), 0)
    c_t = lax.broadcasted_iota(jnp.int32, (t_len, t_len), 1)
    same = (r_t // CHUNK) == (c_t // CHUNK)
    lower = jnp.where(same & (r_t >= c_t), 1.0, 0.0).astype(BF16)
    upper = jnp.where(same & (r_t <= c_t), 1.0, 0.0).astype(BF16)
    g_t = g.T[:2 * SUBLANES]
    cs_row = (_dot01_right(g_t, upper), _dot01_right(g_t, lower))
    pad_rows = jnp.zeros((LANES - 2 * SUBLANES, t_len), F32)
    cs_col = tuple(jnp.concatenate([r, pad_rows], axis=0).T for r in cs_row)
    beta_t = beta.T[:3 * SUBLANES]

    qs = [yn_ref[h] for h in range(GDN_HEADS)]
    ks = [yn_ref[GDN_HEADS + h] for h in range(GDN_HEADS)]
    vs = [yn_ref[2 * GDN_HEADS + h] for h in range(GDN_HEADS)]
    kts = [kh.T for kh in ks]

    r_c = lax.broadcasted_iota(jnp.int32, (CHUNK, LANES), 0)
    lane = lax.broadcasted_iota(jnp.int32, (CHUNK, LANES), 1)
    is_fwd = lane < CHUNK
    c_c = lane % CHUNK
    eye = jnp.where(r_c == c_c, 1.0, 0.0).astype(F32)
    incl = (is_fwd & (r_c >= c_c)) | (~is_fwd & (r_c <= c_c))
    strict = (is_fwd & (r_c > c_c)) | (~is_fwd & (r_c < c_c))
    r_d = lax.broadcasted_iota(jnp.int32, (2 * CHUNK, LANES), 0)
    c_d = lax.broadcasted_iota(jnp.int32, (2 * CHUNK, LANES), 1)
    same_dir = (r_d // CHUNK) == (c_d // CHUNK)
    level_mask = lambda s_, r_, c_: ((r_ // (2 * s_)) == (c_ // (2 * s_))) & ((r_ // s_) != (c_ // s_))
    stack2 = lambda t: jnp.concatenate([t, t], axis=0)

    n_chunks = t_len // CHUNK
    rows = lambda c: slice(c * CHUNK, (c + 1) * CHUNK)
    pairs = [(c, h) for c in range(n_chunks) for h in range(GDN_HEADS)]
    qkk = {}
    for c, h in pairs:
        k16 = ks[h][rows(c)].astype(BF16)
        qk16 = jnp.concatenate([qs[h][rows(c)].astype(BF16), k16], axis=0)
        qkk[c, h] = _dot_nt(qk16, stack2(k16))

    bcast = lambda col: jnp.broadcast_to(col, (CHUNK, LANES))
    brow, grow, glast, a_mat, t_mat, kv16, qk_even = {}, {}, {}, {}, {}, {}, {}
    for key in pairs:
        c, h = key
        g_full = {}
        for d in range(N_DIR):
            j = d * GDN_HEADS + h
            r_last = c * CHUNK + (CHUNK - 1 if d == 0 else 0)
            g_full[d] = bcast(cs_col[d][rows(c), j:j + 1])
            brow[key, d] = beta_t[SUBLANES + j:SUBLANES + j + 1, rows(c)]
            glast[key, d] = cs_col[d][r_last:r_last + 1, j:j + 1]
            grow[key, d] = cs_row[d][j:j + 1, rows(c)]
        gcol2 = jnp.where(is_fwd, g_full[0], g_full[1])
        grow2 = jnp.concatenate([grow[key, 0], grow[key, 1]], axis=1)
        brow2 = jnp.concatenate([brow[key, 0], brow[key, 1]], axis=1)
        decay = jnp.exp2(jnp.where(incl, gcol2 - grow2, -jnp.inf))
        a_mat[key] = jnp.where(strict, qkk[key][CHUNK:] * decay, 0.0) * brow2
        t_mat[key] = eye - jnp.where(level_mask(1, r_c, c_c), a_mat[key], 0.0)
        qk = qkk[key][:CHUNK] * decay * brow2
        if c % 2 == 0:
            qk_even[h] = qk
        else:
            pair_rows = rows(c // 2)
            qkf_ref[0, h, pair_rows, :] = jnp.where(is_fwd, qk_even[h], pltpu.roll(qk, CHUNK, 1)).astype(qkf_ref.dtype)
            qkb_ref[0, h, pair_rows, :] = jnp.where(is_fwd, pltpu.roll(qk_even[h], CHUNK, 1), qk).astype(qkb_ref.dtype)
        eg = [jnp.exp2(g_full[d]) for d in range(N_DIR)]
        for d in range(N_DIR):
            qd_ref[0, d * GDN_HEADS + h, rows(c), :] = (qs[h][rows(c)] * eg[d]).astype(qd_ref.dtype)
        kv16[key] = jnp.concatenate(
            [jnp.concatenate([ks[h][rows(c)] * eg[d], vs[h][rows(c)]], axis=1) for d in range(N_DIR)],
            axis=0).astype(BF16)

    same_dir16 = jnp.where(same_dir, 1.0, 0.0).astype(BF16)
    block_diag = lambda t16: stack2(t16) * same_dir16
    a_bd = {key: block_diag(a_mat[key].astype(BF16)) for key in pairs}
    s = 2
    while s < CHUNK:
        lvl = level_mask(s, r_c, c_c)
        t16 = {key: t_mat[key].astype(BF16) for key in pairs}
        x_mat = {key: jnp.dot(t16[key], a_bd[key], preferred_element_type=F32) for key in pairs}
        y_mat = {key: jnp.dot(x_mat[key].astype(BF16), block_diag(t16[key]), preferred_element_type=F32)
                 for key in pairs}
        t_mat = {key: t_mat[key] - jnp.where(lvl, y_mat[key], 0.0) for key in pairs}
        s *= 2

    wu = {}
    for key in pairs:
        c, h = key
        t_sel = block_diag(t_mat[key].astype(BF16))
        wu[key] = jnp.dot(t_sel, kv16[key], preferred_element_type=F32)

    for c in range(n_chunks):
        egl_rows = []
        for d in range(N_DIR):
            for h in range(GDN_HEADS):
                key = (c, h)
                j = d * GDN_HEADS + h
                wu_d = wu[key][d * CHUNK:(d + 1) * CHUNK]
                w_ref[0, j, rows(c), :] = wu_d[:, :GDN_HEAD_DIM].astype(w_ref.dtype)
                u_ref[0, j, rows(c), :] = wu_d[:, GDN_HEAD_DIM:].astype(u_ref.dtype)
                egl_rows.append(jnp.broadcast_to(jnp.exp2(glast[key, d]), (1, LANES)))
        egl_ref[0, c] = jnp.concatenate(egl_rows, axis=0)
    for cp in range(n_chunks // 2):
        for d in range(N_DIR):
            for h in range(GDN_HEADS):
                fac = jnp.concatenate([brow[(c, h), d] * jnp.exp2(glast[(c, h), d] - grow[(c, h), d])
                                       for c in (2 * cp, 2 * cp + 1)], axis=1)
                kdt = kts[h][:, 2 * cp * CHUNK:(2 * cp + 2) * CHUNK] * fac
                kdt_ref[0, d * GDN_HEADS + h, cp] = kdt.astype(kdt_ref.dtype)


def _gdn_prep(yn, ab, gate_par):
    b, s, _ = ab.shape
    nt = s // PREP_T
    cpb = PREP_T // CHUNK
    nc = s // CHUNK
    chain = lambda last: pl.BlockSpec((1, N_CHAIN, PREP_T, last), lambda bi, i: (bi, 0, i, 0))
    return pl.pallas_call(
        _gprep_kernel,
        grid=(b, nt),
        in_specs=[pl.BlockSpec((3 * GDN_W // LANES, PREP_T, LANES), lambda bi, i: (0, bi * nt + i, 0)),
                  pl.BlockSpec((1, PREP_T, LANES), lambda bi, i: (bi, i, 0)),
                  pl.BlockSpec((SUBLANES, LANES), lambda bi, i: (0, 0))],
        out_specs=[chain(GDN_HEAD_DIM), chain(GDN_HEAD_DIM), chain(GDN_HEAD_DIM),
                   pl.BlockSpec((1, N_CHAIN, cpb // 2, GDN_HEAD_DIM, 2 * CHUNK), lambda bi, i: (bi, 0, i, 0, 0)),
                   pl.BlockSpec((1, GDN_HEADS, PREP_T // 2, LANES), lambda bi, i: (bi, 0, i, 0)),
                   pl.BlockSpec((1, GDN_HEADS, PREP_T // 2, LANES), lambda bi, i: (bi, 0, i, 0)),
                   pl.BlockSpec((1, cpb, N_CHAIN, LANES), lambda bi, i: (bi, i, 0, 0))],
        out_shape=[jax.ShapeDtypeStruct((b, N_CHAIN, s, GDN_HEAD_DIM), BF16),
                   jax.ShapeDtypeStruct((b, N_CHAIN, s, GDN_HEAD_DIM), BF16),
                   jax.ShapeDtypeStruct((b, N_CHAIN, s, GDN_HEAD_DIM), BF16),
                   jax.ShapeDtypeStruct((b, N_CHAIN, nc // 2, GDN_HEAD_DIM, 2 * CHUNK), BF16),
                   jax.ShapeDtypeStruct((b, GDN_HEADS, s // 2, LANES), BF16),
                   jax.ShapeDtypeStruct((b, GDN_HEADS, s // 2, LANES), BF16),
                   jax.ShapeDtypeStruct((b, nc, N_CHAIN, LANES), F32)],
        compiler_params=pltpu.CompilerParams(dimension_semantics=("arbitrary", "arbitrary"),
                                             vmem_limit_bytes=VMEM_LIMIT),
        name="gdn_prep",
    )(yn, ab, gate_par)


def _gscan_kernel(ef_ref, eb_ref, wf_ref, wb_ref, uf_ref, ub_ref, qf_ref, qb_ref, kf_ref, kb_ref,
                  pf_ref, pb_ref, of_ref, ob_ref, state_ref):
    n_batch = wf_ref.shape[0]

    @pl.when(pl.program_id(0) == 0)
    def _():
        state_ref[...] = jnp.zeros_like(state_ref)

    dirs = ((wf_ref, uf_ref, qf_ref, kf_ref, pf_ref, of_ref), (wb_ref, ub_ref, qb_ref, kb_ref, pb_ref, ob_ref))
    chains = [(bi, d, h) for bi in range(n_batch) for d in range(N_DIR) for h in range(GDN_HEADS)]
    slot = lambda bi, d, h: (bi * N_DIR + d) * GDN_HEADS + h
    st = {key: state_ref[slot(*key)] for key in chains}

    for sub in range(SCAN_CHUNKS):
        local = (sub, SCAN_CHUNKS - 1 - sub)
        rows = [slice(c * CHUNK, (c + 1) * CHUNK) for c in local]
        r = {}
        for key in chains:
            bi, d, h = key
            w_ref, _, q_ref = dirs[d][:3]
            wq = jnp.concatenate([w_ref[bi, h, rows[d]], q_ref[bi, h, rows[d]]], axis=0)
            r[key] = _dot(wq, st[key])
        intra = {}
        for key in chains:
            bi, d, h = key
            u_ref, k_ref, p_ref = dirs[d][1], dirs[d][3], dirs[d][4]
            v_new = (u_ref[bi, h, rows[d]].astype(F32) - r[key][:CHUNK]).astype(BF16)
            zeros = jnp.zeros_like(v_new)
            v_pad = jnp.concatenate([v_new, zeros] if local[d] % 2 == 0 else [zeros, v_new], axis=0)
            pair_rows = slice(local[d] // 2 * CHUNK, (local[d] // 2 + 1) * CHUNK)
            pk = jnp.concatenate([p_ref[bi, h, pair_rows], k_ref[bi, h, local[d] // 2]], axis=0)
            res = jnp.dot(pk, v_pad, preferred_element_type=F32)
            intra[key] = res[:CHUNK]
            j = d * GDN_HEADS + h
            egl = (ef_ref, eb_ref)[d][bi, local[d], j:j + 1, :]
            st[key] = st[key] * egl + res[CHUNK:]
        for bi in range(n_batch):
            for d in range(N_DIR):
                o_ref = dirs[d][5]
                o_ref[bi, rows[d], :] = jnp.concatenate(
                    [r[bi, d, h][CHUNK:] + intra[bi, d, h] for h in range(GDN_HEADS)], axis=-1).astype(o_ref.dtype)

    for key in chains:
        state_ref[slot(*key)] = st[key]


def _gdn_scan(egl, w, u, qd, kdt, qk_f, qk_b):
    b, _, s, _ = w.shape
    rows = SCAN_CHUNKS * CHUNK
    n_steps = s // rows
    fwd_i = lambda t: t
    bwd_i = lambda t: n_steps - 1 - t
    chain = lambda d, at, last: pl.BlockSpec((b, GDN_HEADS, rows, last), lambda t: (0, d, at(t), 0))
    kspec = lambda d, at: pl.BlockSpec((b, GDN_HEADS, SCAN_CHUNKS // 2, GDN_HEAD_DIM, 2 * CHUNK),
                                       lambda t: (0, d, at(t), 0, 0))
    pspec = lambda at: pl.BlockSpec((b, GDN_HEADS, rows // 2, LANES), lambda t: (0, 0, at(t), 0))
    espec = lambda at: pl.BlockSpec((b, SCAN_CHUNKS, N_CHAIN, LANES), lambda t: (0, at(t), 0, 0))
    dk = GDN_HEAD_DIM
    return pl.pallas_call(
        _gscan_kernel,
        grid=(n_steps,),
        in_specs=[espec(fwd_i), espec(bwd_i),
                  chain(0, fwd_i, dk), chain(1, bwd_i, dk), chain(0, fwd_i, dk), chain(1, bwd_i, dk),
                  chain(0, fwd_i, dk), chain(1, bwd_i, dk), kspec(0, fwd_i), kspec(1, bwd_i),
                  pspec(fwd_i), pspec(bwd_i)],
        out_specs=[pl.BlockSpec((b, rows, GDN_W), lambda t: (0, fwd_i(t), 0)),
                   pl.BlockSpec((b, rows, GDN_W), lambda t: (0, bwd_i(t), 0))],
        out_shape=[jax.ShapeDtypeStruct((b, s, GDN_W), BF16),
                   jax.ShapeDtypeStruct((b, s, GDN_W), BF16)],
        scratch_shapes=[pltpu.VMEM((b * N_CHAIN, GDN_HEAD_DIM, GDN_HEAD_DIM), F32)],
        compiler_params=pltpu.CompilerParams(dimension_semantics=("arbitrary",),
                                             vmem_limit_bytes=VMEM_LIMIT),
        name="gdn_scan",
    )(egl, egl, w, w, u, u, qd, qd, kdt, kdt, qk_f, qk_b)


def _ffn_kernel(x_ref, attn_ref, of_ref, ob_ref, z_ref, gnw_ref, wo_hbm, fnw_ref, w1_hbm, w2_hbm, onw_ref,
                o_ref, wo_ref, w1_ref, w2_ref, sem):
    @pl.when(pl.program_id(0) == 0)
    def _():
        copies = [pltpu.make_async_copy(src, dst, sem.at[k])
                  for k, (src, dst) in enumerate(((wo_hbm, wo_ref), (w1_hbm, w1_ref), (w2_hbm, w2_ref)))]
        for copy in copies:
            copy.start()
        for copy in copies:
            copy.wait()

    subs = [slice(r0, r0 + FFN_SUB) for r0 in range(0, FFN_TM, FFN_SUB)]
    rms = lambda t: t * lax.rsqrt(jnp.mean(t * t, axis=-1, keepdims=True) + EPS)
    hres, hn, act, acc = {}, {}, {}, {}
    for r in subs:
        o = of_ref[r, :].astype(F32) + ob_ref[r, :].astype(F32)
        heads = [rms(o[:, h * GDN_HEAD_DIM:(h + 1) * GDN_HEAD_DIM]) * gnw_ref[...] for h in range(GDN_HEADS)]
        z = z_ref[r, :]
        gdn = jnp.concatenate(heads, axis=-1) * (z * jax.nn.sigmoid(z))
        hres[r.start] = (x_ref[r, :] + jnp.dot(attn_ref[r, :], wo_ref[:ATTN_Q, :], preferred_element_type=F32)
                         + _dot(gdn, wo_ref[ATTN_Q:, :]))
    for r in subs:
        hn[r.start] = (rms(hres[r.start]) * fnw_ref[...]).astype(BF16)
        act[r.start] = jnp.dot(hn[r.start], w1_ref[...], preferred_element_type=F32)
    for r in subs:
        a = jnp.square(jnp.maximum(act[r.start], 0.0)).astype(BF16)
        acc[r.start] = hres[r.start] + jnp.dot(a, w2_ref[...], preferred_element_type=F32)
    for r in subs:
        o_ref[r, :] = rms(acc[r.start]) * onw_ref[...]


def _out_ffn(x2, attn, o_f, o_b, z, gnw, wo, fnw, w1, w2, onw):
    n = x2.shape[0]
    assert wo.shape == (D_MODEL, D_MODEL) and w1.shape == (D_MODEL, D_FF) and w2.shape == (D_FF, D_MODEL)
    row = lambda w: pl.BlockSpec((FFN_TM, w), lambda i: (i, 0))
    full = lambda a: pl.BlockSpec(a.shape, lambda i: (0, 0))
    hbm = pl.BlockSpec(memory_space=pl.ANY)
    return pl.pallas_call(
        _ffn_kernel,
        grid=(n // FFN_TM,),
        in_specs=[row(D_MODEL), row(ATTN_Q), row(GDN_W), row(GDN_W), row(GDN_W),
                  full(gnw), hbm, full(fnw), hbm, hbm, full(onw)],
        out_specs=row(D_MODEL),
        out_shape=jax.ShapeDtypeStruct((n, D_MODEL), F32),
        scratch_shapes=[pltpu.VMEM((D_MODEL, D_MODEL), BF16),
                        pltpu.VMEM((D_MODEL, D_FF), BF16),
                        pltpu.VMEM((D_FF, D_MODEL), BF16),
                        pltpu.SemaphoreType.DMA((3,))],
        compiler_params=pltpu.CompilerParams(dimension_semantics=("arbitrary",),
                                             vmem_limit_bytes=VMEM_LIMIT),
        name="out_ffn",
    )(x2, attn, o_f, o_b, z, gnw, wo, fnw, w1, w2, onw)


def _layer(h, band, norm_mix_w, w_in, layer, attn_sink, conv_w, gdn_a_log, gdn_dt_bias, gdn_norm_w,
           w_out, norm_ffn_w, w_ffn_in, w_ffn_out, out_norm_w):
    b, s, _ = h.shape
    n = b * s
    x2 = h.reshape(n, D_MODEL)
    q_a, k_a, v_a, yn, z_g, ab = _proj(x2, norm_mix_w.reshape(1, D_MODEL), w_in, conv_w, layer, s)

    attn, wo_bf, w1_bf, w2_bf = _attention(q_a, k_a, v_a, band, attn_sink, b, layer, (w_out, w_ffn_in, w_ffn_out))

    gate_par = jnp.zeros((SUBLANES, LANES), F32)
    gate_par = gate_par.at[0, :N_CHAIN].set(gdn_a_log.reshape(-1)).at[1, :N_CHAIN].set(gdn_dt_bias.reshape(-1))
    w_c, u_c, q_dec, k_dec_t, qk_f, qk_b, egl = _gdn_prep(yn, ab.reshape(b, s, LANES), gate_par)
    o_f, o_b = _gdn_scan(egl, w_c, u_c, q_dec, k_dec_t, qk_f, qk_b)

    out = _out_ffn(x2, attn, o_f.reshape(n, GDN_W), o_b.reshape(n, GDN_W), z_g,
                   gdn_norm_w.reshape(1, GDN_HEAD_DIM), wo_bf, norm_ffn_w.reshape(1, D_MODEL),
                   w1_bf, w2_bf, out_norm_w.reshape(1, D_MODEL))
    return out.reshape(b, s, D_MODEL)


def kernel(x, norm_mix_w, w_in, rel_bias, attn_sink, conv_w, gdn_a_log, gdn_dt_bias, gdn_norm_w, w_out,
           norm_ffn_w, w_ffn_in, w_ffn_out, norm_final_w):
    depth = w_in.shape[0]
    assert depth == 1, "the fused output kernel applies the final norm after the single trunk layer"
    rel = (np.arange(3 * BLOCK)[None, :] - BLOCK) - np.arange(BLOCK)[:, None]
    bucket = _t5_buckets(jnp.asarray(rel, dtype=jnp.int32))
    band = _bias_band(rel_bias.T, bucket.T)
    return _layer(x, band, norm_mix_w[0], w_in, 0, attn_sink[0], conv_w, gdn_a_log[0], gdn_dt_bias[0],
                  gdn_norm_w[0], w_out, norm_ffn_w[0], w_ffn_in, w_ffn_out, norm_final_w)
```

```python
import functools
import math

import jax
import jax.numpy as jnp
import numpy as np
from jax import lax
from jax.experimental import pallas as pl
from jax.experimental.pallas import tpu as pltpu

F32 = jnp.float32
BF16 = jnp.bfloat16

D_MODEL = 1024
ATTN_HEADS = 8
ATTN_KV_HEADS = 2
ATTN_HEAD_DIM = 64
ATTN_GROUP = ATTN_HEADS // ATTN_KV_HEADS
WINDOW = 128
BLOCK = 128
N_BUCKETS = 32
MAX_DISTANCE = 128
GDN_HEADS = 4
GDN_HEAD_DIM = 128
CONV_K = 5
CHUNK = 64
N_DIR = 2
N_CHAIN = N_DIR * GDN_HEADS
D_FF = 4 * D_MODEL
EPS = 1e-6
LOG2E = math.log2(math.e)
ATTN_Q = ATTN_HEADS * ATTN_HEAD_DIM
ATTN_KV = ATTN_KV_HEADS * ATTN_HEAD_DIM
GDN_W = GDN_HEADS * GDN_HEAD_DIM
LANES = 128
SUBLANES = 8
BF16_SUBLANES = 16
VMEM_LIMIT = 56 * 1024 * 1024

PROJ_TM = 1024
PROJ_SUB = 256
PROJ_CAST_ROWS = 256
PROJ_CAST_SLOTS = 6
ATTN_QB = 8
PREP_T = 1024
ROW_STRIDE = 4
SCAN_CHUNKS = 8
FFN_TM = 512
FFN_SUB = 256


def _dot(a, b):
    return jnp.dot(a.astype(BF16), b.astype(BF16), preferred_element_type=F32)


def _dot_nt(a, b):
    return lax.dot_general(a.astype(BF16), b.astype(BF16), (((1,), (1,)), ((), ())),
                           preferred_element_type=F32)


def _split3(x):
    hi = x.astype(BF16)
    r1 = x - hi.astype(F32)
    mid = r1.astype(BF16)
    lo = (r1 - mid.astype(F32)).astype(BF16)
    return hi, mid, lo


def _dot01_right(x, m01):
    hi, mid, lo = _split3(x)
    d = lambda p: jnp.dot(p, m01, preferred_element_type=F32)
    return d(hi) + d(mid) + d(lo)


def _conv_silu_norm(xe_ref, cw_ref, yn_ref, r0, n_out):
    halo = CONV_K // 2
    n_rows = n_out // ROW_STRIDE
    n_slab = 3 * GDN_W // LANES
    units = [(sb, ph) for sb in range(n_slab) for ph in range(ROW_STRIDE)]
    half_taps = [cw_ref[j] * 0.5 for j in range(CONV_K)]
    yv = {}
    for sb, ph in units:
        lanes = slice(sb * LANES, (sb + 1) * LANES)
        acc = None
        for j in range(CONV_K):
            win = xe_ref[sb, pl.ds(SUBLANES - halo + j + ph, n_rows, stride=ROW_STRIDE), :]
            term = half_taps[j][:, lanes] * win
            acc = term if acc is None else acc + term
        yv[sb, ph] = acc
    for key in units:
        yv[key] = yv[key] + yv[key] * jnp.tanh(yv[key])
    for sb, ph in units:
        if sb < 2 * GDN_HEADS:
            scale = lax.rsqrt(jnp.sum(yv[sb, ph] * yv[sb, ph], axis=-1, keepdims=True) + EPS)
            if sb < GDN_HEADS:
                scale = scale * (GDN_HEAD_DIM ** -0.5)
            yv[sb, ph] = yv[sb, ph] * scale
    for sb, ph in units:
        yn_ref[sb, pl.ds(r0 + ph, n_rows, stride=ROW_STRIDE), :] = yv[sb, ph]


def _proj_kernel(layer, tiles_per_seq, x_ref, xp_ref, xn_ref, nw_ref, wt_hbm, cw_ref,
                 qa_ref, ka_ref, va_ref, yn_ref, z_ref, ab_ref, wb_ref, stage_ref, sem, *xe_refs):
    d_in = wt_hbm.shape[1]
    d_main = d_in // LANES * LANES
    step = pl.program_id(0)

    @pl.when(step == 0)
    def _():
        starts = list(range(0, d_main, PROJ_CAST_ROWS)) + [d_main]
        sizes = [min(PROJ_CAST_ROWS, d_main - c0) for c0 in starts[:-1]] + [d_in - d_main]
        slot = lambda k: k % PROJ_CAST_SLOTS
        copies = [pltpu.make_async_copy(wt_hbm.at[layer, pl.ds(c0, nr)], stage_ref.at[slot(k), pl.ds(0, nr)],
                                        sem.at[slot(k)]) for k, (c0, nr) in enumerate(zip(starts, sizes))]
        lane = lax.broadcasted_iota(jnp.int32, (D_MODEL, PROJ_CAST_ROWS), 1)
        for k in range(min(PROJ_CAST_SLOTS - 1, len(copies))):
            copies[k].start()
        for k, (c0, nr) in enumerate(zip(starts, sizes)):
            if k + PROJ_CAST_SLOTS - 1 < len(copies):
                copies[k + PROJ_CAST_SLOTS - 1].start()
            copies[k].wait()
            cols = stage_ref[slot(k)].T
            if nr < PROJ_CAST_ROWS:
                cols = jnp.where(lane < nr, cols, 0.0)
            width = min(PROJ_CAST_ROWS, wb_ref.shape[1] - c0)
            wb_ref[:, c0:c0 + width] = cols[:, :width].astype(BF16)

    o_q, o_k, o_v, o_g, o_z, o_ab = 0, ATTN_Q, ATTN_Q + ATTN_KV, ATTN_Q + 2 * ATTN_KV, \
        ATTN_Q + 2 * ATTN_KV + 3 * GDN_W, ATTN_Q + 2 * ATTN_KV + 4 * GDN_W
    n_slab = 3 * GDN_W // LANES

    def normed(x):
        ms = jnp.mean(x * x, axis=-1, keepdims=True)
        return (x * lax.rsqrt(ms + EPS) * nw_ref[...]).astype(BF16)

    n_sub = PROJ_TM // PROJ_SUB

    def to_slabs(xe_ref, y, row0):
        for sb in range(n_slab):
            xe_ref[sb, row0:row0 + y.shape[0], :] = y[:, sb * LANES:(sb + 1) * LANES]

    def put(k, y):
        to_slabs(xe_refs[k], y, SUBLANES)
        if k > 0:
            to_slabs(xe_refs[k - 1], y[:SUBLANES], SUBLANES + PROJ_SUB)
        if k + 1 < n_sub:
            to_slabs(xe_refs[k + 1], y[PROJ_SUB - SUBLANES:], 0)

    i = step % tiles_per_seq
    outs = ((qa_ref, o_q, ATTN_Q, ATTN_HEAD_DIM ** -0.5 * LOG2E), (ka_ref, o_k, ATTN_KV, None), (va_ref, o_v, ATTN_KV, None),
            (z_ref, o_z, GDN_W, None), (ab_ref, o_ab, LANES, None))
    for k in range(n_sub):
        rows = slice(k * PROJ_SUB, (k + 1) * PROJ_SUB)
        if k == 0:
            xg = normed(jnp.concatenate([xp_ref[...], xn_ref[...], x_ref[rows, :]], axis=0))
            yg = jnp.dot(xg, wb_ref[:, o_g:o_z], preferred_element_type=F32)
            to_slabs(xe_refs[0], jnp.where(i > 0, yg[:SUBLANES], 0.0), 0)
            to_slabs(xe_refs[n_sub - 1], jnp.where(i < tiles_per_seq - 1, yg[SUBLANES:2 * SUBLANES], 0.0),
                     SUBLANES + PROJ_SUB)
            put(0, yg[2 * SUBLANES:])
            xn = xg[2 * SUBLANES:]
        else:
            xn = normed(x_ref[rows, :])
            put(k, jnp.dot(xn, wb_ref[:, o_g:o_z], preferred_element_type=F32))
            _conv_silu_norm(xe_refs[k - 1], cw_ref, yn_ref, (k - 1) * PROJ_SUB, PROJ_SUB)
        for ref, c0, width, scale in outs:
            y = jnp.dot(xn, wb_ref[:, c0:c0 + width], preferred_element_type=F32)
            if scale is not None:
                y = y * scale
            ref[rows, :] = y.astype(ref.dtype)
    _conv_silu_norm(xe_refs[n_sub - 1], cw_ref, yn_ref, PROJ_TM - PROJ_SUB, PROJ_SUB)


def _proj(x2, norm_w, w_in, conv_w, layer, seq_len):
    n = x2.shape[0]
    d_in = w_in.shape[2]
    d_pad = d_in // LANES * LANES + LANES
    assert d_pad == ATTN_Q + 2 * ATTN_KV + 4 * GDN_W + LANES and d_in - (d_pad - LANES) == 2 * N_CHAIN
    assert seq_len % PROJ_TM == 0
    w_t = jnp.swapaxes(w_in, 1, 2)
    nh8 = PROJ_TM // SUBLANES
    row = lambda w: pl.BlockSpec((PROJ_TM, w), lambda i: (i, 0))
    return pl.pallas_call(
        functools.partial(_proj_kernel, layer, seq_len // PROJ_TM),
        grid=(n // PROJ_TM,),
        in_specs=[row(D_MODEL),
                  pl.BlockSpec((SUBLANES, D_MODEL), lambda i: (jnp.maximum(i * nh8 - 1, 0), 0)),
                  pl.BlockSpec((SUBLANES, D_MODEL), lambda i: (jnp.minimum((i + 1) * nh8, n // SUBLANES - 1), 0)),
                  pl.BlockSpec((1, D_MODEL), lambda i: (0, 0)),
                  pl.BlockSpec(memory_space=pl.ANY),
                  pl.BlockSpec((CONV_K, 1, 3 * GDN_W), lambda i: (0, layer, 0))],
        out_specs=[row(ATTN_Q), row(ATTN_KV), row(ATTN_KV),
                   pl.BlockSpec((3 * GDN_W // LANES, PROJ_TM, LANES), lambda i: (0, i, 0)),
                   row(GDN_W), row(LANES)],
        out_shape=[jax.ShapeDtypeStruct((n, ATTN_Q), BF16),
                   jax.ShapeDtypeStruct((n, ATTN_KV), BF16),
                   jax.ShapeDtypeStruct((n, ATTN_KV), BF16),
                   jax.ShapeDtypeStruct((3 * GDN_W // LANES, n, LANES), F32),
                   jax.ShapeDtypeStruct((n, GDN_W), F32),
                   jax.ShapeDtypeStruct((n, LANES), F32)],
        scratch_shapes=[pltpu.VMEM((D_MODEL, d_pad), BF16),
                        pltpu.VMEM((PROJ_CAST_SLOTS, PROJ_CAST_ROWS, D_MODEL), F32),
                        pltpu.SemaphoreType.DMA((PROJ_CAST_SLOTS,))]
        + [pltpu.VMEM((3 * GDN_W // LANES, PROJ_SUB + 2 * SUBLANES, LANES), F32)] * (PROJ_TM // PROJ_SUB),
        compiler_params=pltpu.CompilerParams(dimension_semantics=("arbitrary",),
                                             vmem_limit_bytes=VMEM_LIMIT),
        name="proj",
    )(x2, x2, x2, norm_w, w_t, jnp.swapaxes(conv_w, 0, 1))


def _bias_kernel(relb_ref, bucket_ref, o_ref):
    bucket = bucket_ref[...]
    key = lax.broadcasted_iota(jnp.int32, (3 * BLOCK, BLOCK), 0)
    qry = lax.broadcasted_iota(jnp.int32, (3 * BLOCK, BLOCK), 1)
    in_window = jnp.abs(key - BLOCK - qry) <= WINDOW
    acc = [jnp.zeros((3 * BLOCK, BLOCK), F32) for _ in range(ATTN_HEADS)]
    for b in range(N_BUCKETS):
        hit = bucket == b
        acc = [jnp.where(hit, relb_ref[h, b], acc[h]) for h in range(ATTN_HEADS)]
    for h in range(ATTN_HEADS):
        o_ref[h // 2, :, (h % 2) * BLOCK:(h % 2 + 1) * BLOCK] = jnp.where(in_window, acc[h] * LOG2E, -1e30)


def _bias_band(rel_bias_t, bucket_t):
    shape = (ATTN_HEADS // 2, 3 * BLOCK, 2 * BLOCK)
    return pl.pallas_call(
        _bias_kernel,
        in_specs=[pl.BlockSpec(memory_space=pltpu.SMEM),
                  pl.BlockSpec((3 * BLOCK, BLOCK), lambda: (0, 0))],
        out_specs=pl.BlockSpec(shape, lambda: (0, 0, 0)),
        out_shape=jax.ShapeDtypeStruct(shape, F32),
        name="bias_band",
    )(rel_bias_t, bucket_t)


def _t5_buckets(rel):
    nb = N_BUCKETS // 2
    max_exact = nb // 2
    base = jnp.where(rel > 0, nb, 0)
    n = jnp.abs(rel)
    log_ratio = jnp.log(jnp.maximum(n, 1).astype(jnp.float32) / max_exact) / math.log(MAX_DISTANCE / max_exact)
    large = jnp.minimum(max_exact + (log_ratio * (nb - max_exact)).astype(jnp.int32), nb - 1)
    return base + jnp.where(n < max_exact, n, large)


def _attn_kernel(sink_ref, q_ref, kp_ref, kc_ref, kn_ref, vp_ref, vc_ref, vn_ref, bias_ref, wo_ref, w1_ref, w2_ref,
                 o_ref, wo_bf_ref, w1_bf_ref, w2_bf_ref):
    for src, dst in ((wo_ref, wo_bf_ref), (w1_ref, w1_bf_ref), (w2_ref, w2_bf_ref)):
        dst[...] = src[0].astype(dst.dtype)
    n = pl.program_id(1)
    last = pl.num_programs(1) - 1
    kband = jnp.concatenate([kp_ref[...], kc_ref[...], kn_ref[...]], axis=0)
    vband = jnp.concatenate([vp_ref[...], vc_ref[...], vn_ref[...]], axis=0)
    vband_t = vband.astype(F32).T.astype(BF16)
    key = lax.broadcasted_iota(jnp.int32, (3 * BLOCK, 1), 0)
    first_head = lax.broadcasted_iota(jnp.int32, (1, 2 * BLOCK), 1) < BLOCK
    head = lambda t, i: t[:, i * ATTN_HEAD_DIM:(i + 1) * ATTN_HEAD_DIM]
    n_pairs = ATTN_HEADS // 2
    kv_of = lambda pr: (2 * pr) // ATTN_GROUP
    units = [(j, pr) for j in range(ATTN_QB) for pr in range(n_pairs)]
    band_rows = lambda j: slice(j * BLOCK, (j + 3) * BLOCK)

    scores = {}
    for j, pr in units:
        qj = q_ref[j * BLOCK:(j + 1) * BLOCK, :]
        q2 = jnp.concatenate([head(qj, 2 * pr), head(qj, 2 * pr + 1)], axis=0)
        scores[j, pr] = _dot_nt(head(kband, kv_of(pr))[band_rows(j)], q2)
    probs, dens = {}, {}
    for j, pr in units:
        s = scores[j, pr] + bias_ref[pr]
        if j == 0:
            s = jnp.where((key < BLOCK) & (n == 0), -1e30, s)
        if j == ATTN_QB - 1:
            s = jnp.where((key >= 2 * BLOCK) & (n == last), -1e30, s)
        sink = jnp.where(first_head, sink_ref[2 * pr], sink_ref[2 * pr + 1]) * LOG2E
        m = jnp.maximum(jnp.max(s, axis=0, keepdims=True), sink)
        probs[j, pr] = jnp.exp2(s - m).astype(BF16)
        dens[j, pr] = jnp.exp2(sink - m)
    outs_t = {}
    ones_rows = jnp.ones((2 * SUBLANES, 3 * BLOCK), BF16)
    for j, pr in units:
        kv = kv_of(pr)
        v_t = vband_t[kv * ATTN_HEAD_DIM:(kv + 1) * ATTN_HEAD_DIM, band_rows(j)]
        pv = jnp.dot(jnp.concatenate([v_t, ones_rows], axis=0), probs[j, pr], preferred_element_type=F32)
        outs_t[j, pr] = pv[:ATTN_HEAD_DIM] / (pv[ATTN_HEAD_DIM:ATTN_HEAD_DIM + 1] + dens[j, pr])
    for j in range(ATTN_QB):
        o_t = jnp.concatenate([outs_t[j, pr][:, half * BLOCK:(half + 1) * BLOCK]
                               for pr in range(n_pairs) for half in range(2)], axis=0)
        o_ref[j * BLOCK:(j + 1) * BLOCK, :] = o_t.T.astype(o_ref.dtype)


def _attention(q_a, k_a, v_a, band, sink, batch, layer, weights):
    n_tok = q_a.shape[0]
    nb = n_tok // batch // BLOCK
    rows = ATTN_QB * BLOCK
    steps = nb // ATTN_QB
    kv_spec = lambda r, f: pl.BlockSpec((r, ATTN_KV), f)
    prev = lambda bi, n: (bi * nb + jnp.maximum(n * ATTN_QB - 1, 0), 0)
    cur = lambda bi, n: (bi * steps + n, 0)
    nxt = lambda bi, n: (bi * nb + jnp.minimum((n + 1) * ATTN_QB, nb - 1), 0)
    n_steps = batch * steps
    assert all(w.shape[1] % (n_steps * BF16_SUBLANES) == 0 for w in weights)
    w_rows = [w.shape[1] // n_steps for w in weights]
    w_in_specs = [pl.BlockSpec((1, r, w.shape[2]), lambda bi, n: (layer, bi * steps + n, 0))
                  for r, w in zip(w_rows, weights)]
    w_out_specs = [pl.BlockSpec((r, w.shape[2]), lambda bi, n: (bi * steps + n, 0)) for r, w in zip(w_rows, weights)]
    return pl.pallas_call(
        _attn_kernel,
        grid=(batch, steps),
        in_specs=[pl.BlockSpec(memory_space=pltpu.SMEM),
                  pl.BlockSpec((rows, ATTN_Q), cur),
                  kv_spec(BLOCK, prev), kv_spec(rows, cur), kv_spec(BLOCK, nxt),
                  kv_spec(BLOCK, prev), kv_spec(rows, cur), kv_spec(BLOCK, nxt),
                  pl.BlockSpec(band.shape, lambda bi, n: (0, 0, 0))] + w_in_specs,
        out_specs=[pl.BlockSpec((rows, ATTN_Q), cur)] + w_out_specs,
        out_shape=[jax.ShapeDtypeStruct((n_tok, ATTN_Q), BF16)]
                  + [jax.ShapeDtypeStruct(w.shape[1:], BF16) for w in weights],
        compiler_params=pltpu.CompilerParams(dimension_semantics=("arbitrary", "arbitrary"),
                                             vmem_limit_bytes=VMEM_LIMIT),
        name="attn",
    )(sink, q_a, k_a, k_a, k_a, v_a, v_a, v_a, band, *weights)


def _gprep_kernel(yn_ref, ab_ref, gp_ref, w_ref, u_ref, qd_ref, kdt_ref, qkf_ref, qkb_ref, egl_ref):
    t_len = PREP_T

    ab = ab_ref[0]
    sp_in = ab + gp_ref[1:2, :]
    softplus = jnp.maximum(sp_in, 0.0) + jnp.log1p(jnp.exp(-jnp.abs(sp_in)))
    g = (-jnp.exp(gp_ref[0:1, :]) * LOG2E) * softplus
    beta = jax.nn.sigmoid(ab)

    r_t = lax.broadcasted_iota(jnp.int32, (t_len, t_len), 0)
    c_t = lax.broadcasted_iota(jnp.int32, (t_len, t_len), 1)
    same = (r_t // CHUNK) == (c_t // CHUNK)
    lower = jnp.where(same & (r_t >= c_t), 1.0, 0.0).astype(BF16)
    upper = jnp.where(same & (r_t <= c_t), 1.0, 0.0).astype(BF16)
    g_t = g.T[:2 * SUBLANES]
    cs_row = (_dot01_right(g_t, upper), _dot01_right(g_t, lower))
    pad_rows = jnp.zeros((LANES - 2 * SUBLANES, t_len), F32)
    cs_col = tuple(jnp.concatenate([r, pad_rows], axis=0).T for r in cs_row)
    beta_t = beta.T[:3 * SUBLANES]

    qs = [yn_ref[h] for h in range(GDN_HEADS)]
    ks = [yn_ref[GDN_HEADS + h] for h in range(GDN_HEADS)]
    vs = [yn_ref[2 * GDN_HEADS + h] for h in range(GDN_HEADS)]
    kts = [kh.T for kh in ks]

    r_c = lax.broadcasted_iota(jnp.int32, (CHUNK, LANES), 0)
    lane = lax.broadcasted_iota(jnp.int32, (CHUNK, LANES), 1)
    is_fwd = lane < CHUNK
    c_c = lane % CHUNK
    eye = jnp.where(r_c == c_c, 1.0, 0.0).astype(F32)
    incl = (is_fwd & (r_c >= c_c)) | (~is_fwd & (r_c <= c_c))
    strict = (is_fwd & (r_c > c_c)) | (~is_fwd & (r_c < c_c))
    r_d = lax.broadcasted_iota(jnp.int32, (2 * CHUNK, LANES), 0)
    c_d = lax.broadcasted_iota(jnp.int32, (2 * CHUNK, LANES), 1)
    same_dir = (r_d // CHUNK) == (c_d // CHUNK)
    level_mask = lambda s_, r_, c_: ((r_ // (2 * s_)) == (c_ // (2 * s_))) & ((r_ // s_) != (c_ // s_))
    stack2 = lambda t: jnp.concatenate([t, t], axis=0)

    n_chunks = t_len // CHUNK
    rows = lambda c: slice(c * CHUNK, (c + 1) * CHUNK)
    pairs = [(c, h) for c in range(n_chunks) for h in range(GDN_HEADS)]
    qkk = {}
    for c, h in pairs:
        k16 = ks[h][rows(c)].astype(BF16)
        qk16 = jnp.concatenate([qs[h][rows(c)].astype(BF16), k16], axis=0)
        qkk[c, h] = _dot_nt(qk16, stack2(k16))

    bcast = lambda col: jnp.broadcast_to(col, (CHUNK, LANES))
    brow, grow, glast, a_mat, t_mat, kv16, qk_even = {}, {}, {}, {}, {}, {}, {}
    for key in pairs:
        c, h = key
        g_full = {}
        for d in range(N_DIR):
            j = d * GDN_HEADS + h
            r_last = c * CHUNK + (CHUNK - 1 if d == 0 else 0)
            g_full[d] = bcast(cs_col[d][rows(c), j:j + 1])
            brow[key, d] = beta_t[SUBLANES + j:SUBLANES + j + 1, rows(c)]
            glast[key, d] = cs_col[d][r_last:r_last + 1, j:j + 1]
            grow[key, d] = cs_row[d][j:j + 1, rows(c)]
        gcol2 = jnp.where(is_fwd, g_full[0], g_full[1])
        grow2 = jnp.concatenate([grow[key, 0], grow[key, 1]], axis=1)
        brow2 = jnp.concatenate([brow[key, 0], brow[key, 1]], axis=1)
        decay = jnp.exp2(jnp.where(incl, gcol2 - grow2, -jnp.inf))
        a_mat[key] = jnp.where(strict, qkk[key][CHUNK:] * decay, 0.0) * brow2
        t_mat[key] = eye - jnp.where(level_mask(1, r_c, c_c), a_mat[key], 0.0)
        qk = qkk[key][:CHUNK] * decay * brow2
        if c % 2 == 0:
            qk_even[h] = qk
        else:
            pair_rows = rows(c // 2)
            qkf_ref[0, h, pair_rows, :] = jnp.where(is_fwd, qk_even[h], pltpu.roll(qk, CHUNK, 1)).astype(qkf_ref.dtype)
            qkb_ref[0, h, pair_rows, :] = jnp.where(is_fwd, pltpu.roll(qk_even[h], CHUNK, 1), qk).astype(qkb_ref.dtype)
        eg = [jnp.exp2(g_full[d]) for d in range(N_DIR)]
        for d in range(N_DIR):
            qd_ref[0, d * GDN_HEADS + h, rows(c), :] = (qs[h][rows(c)] * eg[d]).astype(qd_ref.dtype)
        kv16[key] = jnp.concatenate(
            [jnp.concatenate([ks[h][rows(c)] * eg[d], vs[h][rows(c)]], axis=1) for d in range(N_DIR)],
            axis=0).astype(BF16)

    same_dir16 = jnp.where(same_dir, 1.0, 0.0).astype(BF16)
    block_diag = lambda t16: stack2(t16) * same_dir16
    a_bd = {key: block_diag(a_mat[key].astype(BF16)) for key in pairs}
    s = 2
    while s < CHUNK:
        lvl = level_mask(s, r_c, c_c)
        t16 = {key: t_mat[key].astype(BF16) for key in pairs}
        x_mat = {key: jnp.dot(t16[key], a_bd[key], preferred_element_type=F32) for key in pairs}
        y_mat = {key: jnp.dot(x_mat[key].astype(BF16), block_diag(t16[key]), preferred_element_type=F32)
                 for key in pairs}
        t_mat = {key: t_mat[key] - jnp.where(lvl, y_mat[key], 0.0) for key in pairs}
        s *= 2

    wu = {}
    for key in pairs:
        c, h = key
        t_sel = block_diag(t_mat[key].astype(BF16))
        wu[key] = jnp.dot(t_sel, kv16[key], preferred_element_type=F32)

    for c in range(n_chunks):
        egl_rows = []
        for d in range(N_DIR):
            for h in range(GDN_HEADS):
                key = (c, h)
                j = d * GDN_HEADS + h
                wu_d = wu[key][d * CHUNK:(d + 1) * CHUNK]
                w_ref[0, j, rows(c), :] = wu_d[:, :GDN_HEAD_DIM].astype(w_ref.dtype)
                u_ref[0, j, rows(c), :] = wu_d[:, GDN_HEAD_DIM:].astype(u_ref.dtype)
                egl_rows.append(jnp.broadcast_to(jnp.exp2(glast[key, d]), (1, LANES)))
        egl_ref[0, c] = jnp.concatenate(egl_rows, axis=0)
    for cp in range(n_chunks // 2):
        for d in range(N_DIR):
            for h in range(GDN_HEADS):
                fac = jnp.concatenate([brow[(c, h), d] * jnp.exp2(glast[(c, h), d] - grow[(c, h), d])
                                       for c in (2 * cp, 2 * cp + 1)], axis=1)
                kdt = kts[h][:, 2 * cp * CHUNK:(2 * cp + 2) * CHUNK] * fac
                kdt_ref[0, d * GDN_HEADS + h, cp] = kdt.astype(kdt_ref.dtype)


def _gdn_prep(yn, ab, gate_par):
    b, s, _ = ab.shape
    nt = s // PREP_T
    cpb = PREP_T // CHUNK
    nc = s // CHUNK
    chain = lambda last: pl.BlockSpec((1, N_CHAIN, PREP_T, last), lambda bi, i: (bi, 0, i, 0))
    return pl.pallas_call(
        _gprep_kernel,
        grid=(b, nt),
        in_specs=[pl.BlockSpec((3 * GDN_W // LANES, PREP_T, LANES), lambda bi, i: (0, bi * nt + i, 0)),
                  pl.BlockSpec((1, PREP_T, LANES), lambda bi, i: (bi, i, 0)),
                  pl.BlockSpec((SUBLANES, LANES), lambda bi, i: (0, 0))],
        out_specs=[chain(GDN_HEAD_DIM), chain(GDN_HEAD_DIM), chain(GDN_HEAD_DIM),
                   pl.BlockSpec((1, N_CHAIN, cpb // 2, GDN_HEAD_DIM, 2 * CHUNK), lambda bi, i: (bi, 0, i, 0, 0)),
                   pl.BlockSpec((1, GDN_HEADS, PREP_T // 2, LANES), lambda bi, i: (bi, 0, i, 0)),
                   pl.BlockSpec((1, GDN_HEADS, PREP_T // 2, LANES), lambda bi, i: (bi, 0, i, 0)),
                   pl.BlockSpec((1, cpb, N_CHAIN, LANES), lambda bi, i: (bi, i, 0, 0))],
        out_shape=[jax.ShapeDtypeStruct((b, N_CHAIN, s, GDN_HEAD_DIM), BF16),
                   jax.ShapeDtypeStruct((b, N_CHAIN, s, GDN_HEAD_DIM), BF16),
                   jax.ShapeDtypeStruct((b, N_CHAIN, s, GDN_HEAD_DIM), BF16),
                   jax.ShapeDtypeStruct((b, N_CHAIN, nc // 2, GDN_HEAD_DIM, 2 * CHUNK), BF16),
                   jax.ShapeDtypeStruct((b, GDN_HEADS, s // 2, LANES), BF16),
                   jax.ShapeDtypeStruct((b, GDN_HEADS, s // 2, LANES), BF16),
                   jax.ShapeDtypeStruct((b, nc, N_CHAIN, LANES), F32)],
        compiler_params=pltpu.CompilerParams(dimension_semantics=("arbitrary", "arbitrary"),
                                             vmem_limit_bytes=VMEM_LIMIT),
        name="gdn_prep",
    )(yn, ab, gate_par)


def _gscan_kernel(ef_ref, eb_ref, wf_ref, wb_ref, uf_ref, ub_ref, qf_ref, qb_ref, kf_ref, kb_ref,
                  pf_ref, pb_ref, of_ref, ob_ref, state_ref):
    n_batch = wf_ref.shape[0]

    @pl.when(pl.program_id(0) == 0)
    def _():
        state_ref[...] = jnp.zeros_like(state_ref)

    dirs = ((wf_ref, uf_ref, qf_ref, kf_ref, pf_ref, of_ref), (wb_ref, ub_ref, qb_ref, kb_ref, pb_ref, ob_ref))
    chains = [(bi, d, h) for bi in range(n_batch) for d in range(N_DIR) for h in range(GDN_HEADS)]
    slot = lambda bi, d, h: (bi * N_DIR + d) * GDN_HEADS + h
    st = {key: state_ref[slot(*key)] for key in chains}

    for sub in range(SCAN_CHUNKS):
        local = (sub, SCAN_CHUNKS - 1 - sub)
        rows = [slice(c * CHUNK, (c + 1) * CHUNK) for c in local]
        r = {}
        for key in chains:
            bi, d, h = key
            w_ref, _, q_ref = dirs[d][:3]
            wq = jnp.concatenate([w_ref[bi, h, rows[d]], q_ref[bi, h, rows[d]]], axis=0)
            r[key] = _dot(wq, st[key])
        intra = {}
        for key in chains:
            bi, d, h = key
            u_ref, k_ref, p_ref = dirs[d][1], dirs[d][3], dirs[d][4]
            v_new = (u_ref[bi, h, rows[d]].astype(F32) - r[key][:CHUNK]).astype(BF16)
            zeros = jnp.zeros_like(v_new)
            v_pad = jnp.concatenate([v_new, zeros] if local[d] % 2 == 0 else [zeros, v_new], axis=0)
            pair_rows = slice(local[d] // 2 * CHUNK, (local[d] // 2 + 1) * CHUNK)
            pk = jnp.concatenate([p_ref[bi, h, pair_rows], k_ref[bi, h, local[d] // 2]], axis=0)
            res = jnp.dot(pk, v_pad, preferred_element_type=F32)
            intra[key] = res[:CHUNK]
            j = d * GDN_HEADS + h
            egl = (ef_ref, eb_ref)[d][bi, local[d], j:j + 1, :]
            st[key] = st[key] * egl + res[CHUNK:]
        for bi in range(n_batch):
            for d in range(N_DIR):
                o_ref = dirs[d][5]
                o_ref[bi, rows[d], :] = jnp.concatenate(
                    [r[bi, d, h][CHUNK:] + intra[bi, d, h] for h in range(GDN_HEADS)], axis=-1).astype(o_ref.dtype)

    for key in chains:
        state_ref[slot(*key)] = st[key]


def _gdn_scan(egl, w, u, qd, kdt, qk_f, qk_b):
    b, _, s, _ = w.shape
    rows = SCAN_CHUNKS * CHUNK
    n_steps = s // rows
    fwd_i = lambda t: t
    bwd_i = lambda t: n_steps - 1 - t
    chain = lambda d, at, last: pl.BlockSpec((b, GDN_HEADS, rows, last), lambda t: (0, d, at(t), 0))
    kspec = lambda d, at: pl.BlockSpec((b, GDN_HEADS, SCAN_CHUNKS // 2, GDN_HEAD_DIM, 2 * CHUNK),
                                       lambda t: (0, d, at(t), 0, 0))
    pspec = lambda at: pl.BlockSpec((b, GDN_HEADS, rows // 2, LANES), lambda t: (0, 0, at(t), 0))
    espec = lambda at: pl.BlockSpec((b, SCAN_CHUNKS, N_CHAIN, LANES), lambda t: (0, at(t), 0, 0))
    dk = GDN_HEAD_DIM
    return pl.pallas_call(
        _gscan_kernel,
        grid=(n_steps,),
        in_specs=[espec(fwd_i), espec(bwd_i),
                  chain(0, fwd_i, dk), chain(1, bwd_i, dk), chain(0, fwd_i, dk), chain(1, bwd_i, dk),
                  chain(0, fwd_i, dk), chain(1, bwd_i, dk), kspec(0, fwd_i), kspec(1, bwd_i),
                  pspec(fwd_i), pspec(bwd_i)],
        out_specs=[pl.BlockSpec((b, rows, GDN_W), lambda t: (0, fwd_i(t), 0)),
                   pl.BlockSpec((b, rows, GDN_W), lambda t: (0, bwd_i(t), 0))],
        out_shape=[jax.ShapeDtypeStruct((b, s, GDN_W), BF16),
                   jax.ShapeDtypeStruct((b, s, GDN_W), BF16)],
        scratch_shapes=[pltpu.VMEM((b * N_CHAIN, GDN_HEAD_DIM, GDN_HEAD_DIM), F32)],
        compiler_params=pltpu.CompilerParams(dimension_semantics=("arbitrary",),
                                             vmem_limit_bytes=VMEM_LIMIT),
        name="gdn_scan",
    )(egl, egl, w, w, u, u, qd, qd, kdt, kdt, qk_f, qk_b)


def _ffn_kernel(x_ref, attn_ref, of_ref, ob_ref, z_ref, gnw_ref, wo_hbm, fnw_ref, w1_hbm, w2_hbm, onw_ref,
                o_ref, wo_ref, w1_ref, w2_ref, sem):
    @pl.when(pl.program_id(0) == 0)
    def _():
        copies = [pltpu.make_async_copy(src, dst, sem.at[k])
                  for k, (src, dst) in enumerate(((wo_hbm, wo_ref), (w1_hbm, w1_ref), (w2_hbm, w2_ref)))]
        for copy in copies:
            copy.start()
        for copy in copies:
            copy.wait()

    subs = [slice(r0, r0 + FFN_SUB) for r0 in range(0, FFN_TM, FFN_SUB)]
    rms = lambda t: t * lax.rsqrt(jnp.mean(t * t, axis=-1, keepdims=True) + EPS)
    hres, hn, act, acc = {}, {}, {}, {}
    for r in subs:
        o = of_ref[r, :].astype(F32) + ob_ref[r, :].astype(F32)
        heads = [rms(o[:, h * GDN_HEAD_DIM:(h + 1) * GDN_HEAD_DIM]) * gnw_ref[...] for h in range(GDN_HEADS)]
        z = z_ref[r, :]
        gdn = jnp.concatenate(heads, axis=-1) * (z * jax.nn.sigmoid(z))
        hres[r.start] = (x_ref[r, :] + jnp.dot(attn_ref[r, :], wo_ref[:ATTN_Q, :], preferred_element_type=F32)
                         + _dot(gdn, wo_ref[ATTN_Q:, :]))
    for r in subs:
        hn[r.start] = (rms(hres[r.start]) * fnw_ref[...]).astype(BF16)
        act[r.start] = jnp.dot(hn[r.start], w1_ref[...], preferred_element_type=F32)
    for r in subs:
        a = jnp.square(jnp.maximum(act[r.start], 0.0)).astype(BF16)
        acc[r.start] = hres[r.start] + jnp.dot(a, w2_ref[...], preferred_element_type=F32)
    for r in subs:
        o_ref[r, :] = rms(acc[r.start]) * onw_ref[...]


def _out_ffn(x2, attn, o_f, o_b, z, gnw, wo, fnw, w1, w2, onw):
    n = x2.shape[0]
    assert wo.shape == (D_MODEL, D_MODEL) and w1.shape == (D_MODEL, D_FF) and w2.shape == (D_FF, D_MODEL)
    row = lambda w: pl.BlockSpec((FFN_TM, w), lambda i: (i, 0))
    full = lambda a: pl.BlockSpec(a.shape, lambda i: (0, 0))
    hbm = pl.BlockSpec(memory_space=pl.ANY)
    return pl.pallas_call(
        _ffn_kernel,
        grid=(n // FFN_TM,),
        in_specs=[row(D_MODEL), row(ATTN_Q), row(GDN_W), row(GDN_W), row(GDN_W),
                  full(gnw), hbm, full(fnw), hbm, hbm, full(onw)],
        out_specs=row(D_MODEL),
        out_shape=jax.ShapeDtypeStruct((n, D_MODEL), F32),
        scratch_shapes=[pltpu.VMEM((D_MODEL, D_MODEL), BF16),
                        pltpu.VMEM((D_MODEL, D_FF), BF16),
                        pltpu.VMEM((D_FF, D_MODEL), BF16),
                        pltpu.SemaphoreType.DMA((3,))],
        compiler_params=pltpu.CompilerParams(dimension_semantics=("arbitrary",),
                                             vmem_limit_bytes=VMEM_LIMIT),
        name="out_ffn",
    )(x2, attn, o_f, o_b, z, gnw, wo, fnw, w1, w2, onw)


def _layer(h, band, norm_mix_w, w_in, layer, attn_sink, conv_w, gdn_a_log, gdn_dt_bias, gdn_norm_w,
           w_out, norm_ffn_w, w_ffn_in, w_ffn_out, out_norm_w):
    b, s, _ = h.shape
    n = b * s
    x2 = h.reshape(n, D_MODEL)
    q_a, k_a, v_a, yn, z_g, ab = _proj(x2, norm_mix_w.reshape(1, D_MODEL), w_in, conv_w, layer, s)

    attn, wo_bf, w1_bf, w2_bf = _attention(q_a, k_a, v_a, band, attn_sink, b, layer, (w_out, w_ffn_in, w_ffn_out))

    gate_par = jnp.zeros((SUBLANES, LANES), F32)
    gate_par = gate_par.at[0, :N_CHAIN].set(gdn_a_log.reshape(-1)).at[1, :N_CHAIN].set(gdn_dt_bias.reshape(-1))
    w_c, u_c, q_dec, k_dec_t, qk_f, qk_b, egl = _gdn_prep(yn, ab.reshape(b, s, LANES), gate_par)
    o_f, o_b = _gdn_scan(egl, w_c, u_c, q_dec, k_dec_t, qk_f, qk_b)

    out = _out_ffn(x2, attn, o_f.reshape(n, GDN_W), o_b.reshape(n, GDN_W), z_g,
                   gdn_norm_w.reshape(1, GDN_HEAD_DIM), wo_bf, norm_ffn_w.reshape(1, D_MODEL),
                   w1_bf, w2_bf, out_norm_w.reshape(1, D_MODEL))
    return out.reshape(b, s, D_MODEL)


def kernel(x, norm_mix_w, w_in, rel_bias, attn_sink, conv_w, gdn_a_log, gdn_dt_bias, gdn_norm_w, w_out,
           norm_ffn_w, w_ffn_in, w_ffn_out, norm_final_w):
    depth = w_in.shape[0]
    assert depth == 1, "the fused output kernel applies the final norm after the single trunk layer"
    rel = (np.arange(3 * BLOCK)[None, :] - BLOCK) - np.arange(BLOCK)[:, None]
    bucket = _t5_buckets(jnp.asarray(rel, dtype=jnp.int32))
    band = _bias_band(rel_bias.T, bucket.T)
    return _layer(x, band, norm_mix_w[0], w_in, 0, attn_sink[0], conv_w, gdn_a_log[0], gdn_dt_bias[0],
                  gdn_norm_w[0], w_out, norm_ffn_w[0], w_ffn_in, w_ffn_out, norm_final_w)
```

```python
import functools
import math

import jax
import jax.numpy as jnp
import numpy as np
from jax import lax
from jax.experimental import pallas as pl
from jax.experimental.pallas import tpu as pltpu

F32 = jnp.float32
BF16 = jnp.bfloat16

D_MODEL = 1024
ATTN_HEADS = 8
ATTN_KV_HEADS = 2
ATTN_HEAD_DIM = 64
ATTN_GROUP = ATTN_HEADS // ATTN_KV_HEADS
WINDOW = 128
BLOCK = 128
N_BUCKETS = 32
MAX_DISTANCE = 128
GDN_HEADS = 4
GDN_HEAD_DIM = 128
CONV_K = 5
CHUNK = 64
N_DIR = 2
N_CHAIN = N_DIR * GDN_HEADS
D_FF = 4 * D_MODEL
EPS = 1e-6
LOG2E = math.log2(math.e)
ATTN_Q = ATTN_HEADS * ATTN_HEAD_DIM
ATTN_KV = ATTN_KV_HEADS * ATTN_HEAD_DIM
GDN_W = GDN_HEADS * GDN_HEAD_DIM
LANES = 128
SUBLANES = 8
BF16_SUBLANES = 16
VMEM_LIMIT = 56 * 1024 * 1024

PROJ_TM = 1024
PROJ_SUB = 256
PROJ_CAST_ROWS = 256
PROJ_CAST_SLOTS = 6
ATTN_QB = 8
PREP_T = 1024
ROW_STRIDE = 4
SCAN_CHUNKS = 8
FFN_TM = 512
FFN_SUB = 256


def _dot(a, b):
    return jnp.dot(a.astype(BF16), b.astype(BF16), preferred_element_type=F32)


def _dot_nt(a, b):
    return lax.dot_general(a.astype(BF16), b.astype(BF16), (((1,), (1,)), ((), ())),
                           preferred_element_type=F32)


def _split3(x):
    hi = x.astype(BF16)
    r1 = x - hi.astype(F32)
    mid = r1.astype(BF16)
    lo = (r1 - mid.astype(F32)).astype(BF16)
    return hi, mid, lo


def _dot01_right(x, m01):
    hi, mid, lo = _split3(x)
    d = lambda p: jnp.dot(p, m01, preferred_element_type=F32)
    return d(hi) + d(mid) + d(lo)


def _conv_silu_norm(xe_ref, cw_ref, yn_ref, r0, n_out):
    halo = CONV_K // 2
    n_rows = n_out // ROW_STRIDE
    n_slab = 3 * GDN_W // LANES
    units = [(sb, ph) for sb in range(n_slab) for ph in range(ROW_STRIDE)]
    half_taps = [cw_ref[j] * 0.5 for j in range(CONV_K)]
    yv = {}
    for sb, ph in units:
        lanes = slice(sb * LANES, (sb + 1) * LANES)
        acc = None
        for j in range(CONV_K):
            win = xe_ref[sb, pl.ds(SUBLANES - halo + j + ph, n_rows, stride=ROW_STRIDE), :]
            term = half_taps[j][:, lanes] * win
            acc = term if acc is None else acc + term
        yv[sb, ph] = acc
    for key in units:
        yv[key] = yv[key] + yv[key] * jnp.tanh(yv[key])
    for sb, ph in units:
        if sb < 2 * GDN_HEADS:
            scale = lax.rsqrt(jnp.sum(yv[sb, ph] * yv[sb, ph], axis=-1, keepdims=True) + EPS)
            if sb < GDN_HEADS:
                scale = scale * (GDN_HEAD_DIM ** -0.5)
            yv[sb, ph] = yv[sb, ph] * scale
    for sb, ph in units:
        yn_ref[sb, pl.ds(r0 + ph, n_rows, stride=ROW_STRIDE), :] = yv[sb, ph]


def _proj_kernel(layer, tiles_per_seq, x_ref, xp_ref, xn_ref, nw_ref, wt_hbm, cw_ref,
                 qa_ref, ka_ref, va_ref, yn_ref, z_ref, ab_ref, wb_ref, stage_ref, sem, *xe_refs):
    d_in = wt_hbm.shape[1]
    d_main = d_in // LANES * LANES
    step = pl.program_id(0)

    @pl.when(step == 0)
    def _():
        starts = list(range(0, d_main, PROJ_CAST_ROWS)) + [d_main]
        sizes = [min(PROJ_CAST_ROWS, d_main - c0) for c0 in starts[:-1]] + [d_in - d_main]
        slot = lambda k: k % PROJ_CAST_SLOTS
        copies = [pltpu.make_async_copy(wt_hbm.at[layer, pl.ds(c0, nr)], stage_ref.at[slot(k), pl.ds(0, nr)],
                                        sem.at[slot(k)]) for k, (c0, nr) in enumerate(zip(starts, sizes))]
        lane = lax.broadcasted_iota(jnp.int32, (D_MODEL, PROJ_CAST_ROWS), 1)
        for k in range(min(PROJ_CAST_SLOTS - 1, len(copies))):
            copies[k].start()
        for k, (c0, nr) in enumerate(zip(starts, sizes)):
            if k + PROJ_CAST_SLOTS - 1 < len(copies):
                copies[k + PROJ_CAST_SLOTS - 1].start()
            copies[k].wait()
            cols = stage_ref[slot(k)].T
            if nr < PROJ_CAST_ROWS:
                cols = jnp.where(lane < nr, cols, 0.0)
            width = min(PROJ_CAST_ROWS, wb_ref.shape[1] - c0)
            wb_ref[:, c0:c0 + width] = cols[:, :width].astype(BF16)

    o_q, o_k, o_v, o_g, o_z, o_ab = 0, ATTN_Q, ATTN_Q + ATTN_KV, ATTN_Q + 2 * ATTN_KV, \
        ATTN_Q + 2 * ATTN_KV + 3 * GDN_W, ATTN_Q + 2 * ATTN_KV + 4 * GDN_W
    n_slab = 3 * GDN_W // LANES

    def normed(x):
        ms = jnp.mean(x * x, axis=-1, keepdims=True)
        return (x * lax.rsqrt(ms + EPS) * nw_ref[...]).astype(BF16)

    n_sub = PROJ_TM // PROJ_SUB

    def to_slabs(xe_ref, y, row0):
        for sb in range(n_slab):
            xe_ref[sb, row0:row0 + y.shape[0], :] = y[:, sb * LANES:(sb + 1) * LANES]

    def put(k, y):
        to_slabs(xe_refs[k], y, SUBLANES)
        if k > 0:
            to_slabs(xe_refs[k - 1], y[:SUBLANES], SUBLANES + PROJ_SUB)
        if k + 1 < n_sub:
            to_slabs(xe_refs[k + 1], y[PROJ_SUB - SUBLANES:], 0)

    i = step % tiles_per_seq
    outs = ((qa_ref, o_q, ATTN_Q, ATTN_HEAD_DIM ** -0.5 * LOG2E), (ka_ref, o_k, ATTN_KV, None), (va_ref, o_v, ATTN_KV, None),
            (z_ref, o_z, GDN_W, None), (ab_ref, o_ab, LANES, None))
    for k in range(n_sub):
        rows = slice(k * PROJ_SUB, (k + 1) * PROJ_SUB)
        if k == 0:
            xg = normed(jnp.concatenate([xp_ref[...], xn_ref[...], x_ref[rows, :]], axis=0))
            yg = jnp.dot(xg, wb_ref[:, o_g:o_z], preferred_element_type=F32)
            to_slabs(xe_refs[0], jnp.where(i > 0, yg[:SUBLANES], 0.0), 0)
            to_slabs(xe_refs[n_sub - 1], jnp.where(i < tiles_per_seq - 1, yg[SUBLANES:2 * SUBLANES], 0.0),
                     SUBLANES + PROJ_SUB)
            put(0, yg[2 * SUBLANES:])
            xn = xg[2 * SUBLANES:]
        else:
            xn = normed(x_ref[rows, :])
            put(k, jnp.dot(xn, wb_ref[:, o_g:o_z], preferred_element_type=F32))
            _conv_silu_norm(xe_refs[k - 1], cw_ref, yn_ref, (k - 1) * PROJ_SUB, PROJ_SUB)
        for ref, c0, width, scale in outs:
            y = jnp.dot(xn, wb_ref[:, c0:c0 + width], preferred_element_type=F32)
            if scale is not None:
                y = y * scale
            ref[rows, :] = y.astype(ref.dtype)
    _conv_silu_norm(xe_refs[n_sub - 1], cw_ref, yn_ref, PROJ_TM - PROJ_SUB, PROJ_SUB)


def _proj(x2, norm_w, w_in, conv_w, layer, seq_len):
    n = x2.shape[0]
    d_in = w_in.shape[2]
    d_pad = d_in // LANES * LANES + LANES
    assert d_pad == ATTN_Q + 2 * ATTN_KV + 4 * GDN_W + LANES and d_in - (d_pad - LANES) == 2 * N_CHAIN
    assert seq_len % PROJ_TM == 0
    w_t = jnp.swapaxes(w_in, 1, 2)
    nh8 = PROJ_TM // SUBLANES
    row = lambda w: pl.BlockSpec((PROJ_TM, w), lambda i: (i, 0))
    return pl.pallas_call(
        functools.partial(_proj_kernel, layer, seq_len // PROJ_TM),
        grid=(n // PROJ_TM,),
        in_specs=[row(D_MODEL),
                  pl.BlockSpec((SUBLANES, D_MODEL), lambda i: (jnp.maximum(i * nh8 - 1, 0), 0)),
                  pl.BlockSpec((SUBLANES, D_MODEL), lambda i: (jnp.minimum((i + 1) * nh8, n // SUBLANES - 1), 0)),
                  pl.BlockSpec((1, D_MODEL), lambda i: (0, 0)),
                  pl.BlockSpec(memory_space=pl.ANY),
                  pl.BlockSpec((CONV_K, 1, 3 * GDN_W), lambda i: (0, layer, 0))],
        out_specs=[row(ATTN_Q), row(ATTN_KV), row(ATTN_KV),
                   pl.BlockSpec((3 * GDN_W // LANES, PROJ_TM, LANES), lambda i: (0, i, 0)),
                   row(GDN_W), row(LANES)],
        out_shape=[jax.ShapeDtypeStruct((n, ATTN_Q), BF16),
                   jax.ShapeDtypeStruct((n, ATTN_KV), BF16),
                   jax.ShapeDtypeStruct((n, ATTN_KV), BF16),
                   jax.ShapeDtypeStruct((3 * GDN_W // LANES, n, LANES), F32),
                   jax.ShapeDtypeStruct((n, GDN_W), F32),
                   jax.ShapeDtypeStruct((n, LANES), F32)],
        scratch_shapes=[pltpu.VMEM((D_MODEL, d_pad), BF16),
                        pltpu.VMEM((PROJ_CAST_SLOTS, PROJ_CAST_ROWS, D_MODEL), F32),
                        pltpu.SemaphoreType.DMA((PROJ_CAST_SLOTS,))]
        + [pltpu.VMEM((3 * GDN_W // LANES, PROJ_SUB + 2 * SUBLANES, LANES), F32)] * (PROJ_TM // PROJ_SUB),
        compiler_params=pltpu.CompilerParams(dimension_semantics=("arbitrary",),
                                             vmem_limit_bytes=VMEM_LIMIT),
        name="proj",
    )(x2, x2, x2, norm_w, w_t, jnp.swapaxes(conv_w, 0, 1))


def _bias_kernel(relb_ref, bucket_ref, o_ref):
    bucket = bucket_ref[...]
    key = lax.broadcasted_iota(jnp.int32, (3 * BLOCK, BLOCK), 0)
    qry = lax.broadcasted_iota(jnp.int32, (3 * BLOCK, BLOCK), 1)
    in_window = jnp.abs(key - BLOCK - qry) <= WINDOW
    acc = [jnp.zeros((3 * BLOCK, BLOCK), F32) for _ in range(ATTN_HEADS)]
    for b in range(N_BUCKETS):
        hit = bucket == b
        acc = [jnp.where(hit, relb_ref[h, b], acc[h]) for h in range(ATTN_HEADS)]
    for h in range(ATTN_HEADS):
        o_ref[h // 2, :, (h % 2) * BLOCK:(h % 2 + 1) * BLOCK] = jnp.where(in_window, acc[h] * LOG2E, -1e30)


def _bias_band(rel_bias_t, bucket_t):
    shape = (ATTN_HEADS // 2, 3 * BLOCK, 2 * BLOCK)
    return pl.pallas_call(
        _bias_kernel,
        in_specs=[pl.BlockSpec(memory_space=pltpu.SMEM),
                  pl.BlockSpec((3 * BLOCK, BLOCK), lambda: (0, 0))],
        out_specs=pl.BlockSpec(shape, lambda: (0, 0, 0)),
        out_shape=jax.ShapeDtypeStruct(shape, F32),
        name="bias_band",
    )(rel_bias_t, bucket_t)


def _t5_buckets(rel):
    nb = N_BUCKETS // 2
    max_exact = nb // 2
    base = jnp.where(rel > 0, nb, 0)
    n = jnp.abs(rel)
    log_ratio = jnp.log(jnp.maximum(n, 1).astype(jnp.float32) / max_exact) / math.log(MAX_DISTANCE / max_exact)
    large = jnp.minimum(max_exact + (log_ratio * (nb - max_exact)).astype(jnp.int32), nb - 1)
    return base + jnp.where(n < max_exact, n, large)


def _attn_kernel(sink_ref, q_ref, kp_ref, kc_ref, kn_ref, vp_ref, vc_ref, vn_ref, bias_ref, wo_ref, w1_ref, w2_ref,
                 o_ref, wo_bf_ref, w1_bf_ref, w2_bf_ref):
    for src, dst in ((wo_ref, wo_bf_ref), (w1_ref, w1_bf_ref), (w2_ref, w2_bf_ref)):
        dst[...] = src[0].astype(dst.dtype)
    n = pl.program_id(1)
    last = pl.num_programs(1) - 1
    kband = jnp.concatenate([kp_ref[...], kc_ref[...], kn_ref[...]], axis=0)
    vband = jnp.concatenate([vp_ref[...], vc_ref[...], vn_ref[...]], axis=0)
    vband_t = vband.astype(F32).T.astype(BF16)
    key = lax.broadcasted_iota(jnp.int32, (3 * BLOCK, 1), 0)
    first_head = lax.broadcasted_iota(jnp.int32, (1, 2 * BLOCK), 1) < BLOCK
    head = lambda t, i: t[:, i * ATTN_HEAD_DIM:(i + 1) * ATTN_HEAD_DIM]
    n_pairs = ATTN_HEADS // 2
    kv_of = lambda pr: (2 * pr) // ATTN_GROUP
    units = [(j, pr) for j in range(ATTN_QB) for pr in range(n_pairs)]
    band_rows = lambda j: slice(j * BLOCK, (j + 3) * BLOCK)

    scores = {}
    for j, pr in units:
        qj = q_ref[j * BLOCK:(j + 1) * BLOCK, :]
        q2 = jnp.concatenate([head(qj, 2 * pr), head(qj, 2 * pr + 1)], axis=0)
        scores[j, pr] = _dot_nt(head(kband, kv_of(pr))[band_rows(j)], q2)
    probs, dens = {}, {}
    for j, pr in units:
        s = scores[j, pr] + bias_ref[pr]
        if j == 0:
            s = jnp.where((key < BLOCK) & (n == 0), -1e30, s)
        if j == ATTN_QB - 1:
            s = jnp.where((key >= 2 * BLOCK) & (n == last), -1e30, s)
        sink = jnp.where(first_head, sink_ref[2 * pr], sink_ref[2 * pr + 1]) * LOG2E
        m = jnp.maximum(jnp.max(s, axis=0, keepdims=True), sink)
        probs[j, pr] = jnp.exp2(s - m).astype(BF16)
        dens[j, pr] = jnp.exp2(sink - m)
    outs_t = {}
    ones_rows = jnp.ones((2 * SUBLANES, 3 * BLOCK), BF16)
    for j, pr in units:
        kv = kv_of(pr)
        v_t = vband_t[kv * ATTN_HEAD_DIM:(kv + 1) * ATTN_HEAD_DIM, band_rows(j)]
        pv = jnp.dot(jnp.concatenate([v_t, ones_rows], axis=0), probs[j, pr], preferred_element_type=F32)
        outs_t[j, pr] = pv[:ATTN_HEAD_DIM] / (pv[ATTN_HEAD_DIM:ATTN_HEAD_DIM + 1] + dens[j, pr])
    for j in range(ATTN_QB):
        o_t = jnp.concatenate([outs_t[j, pr][:, half * BLOCK:(half + 1) * BLOCK]
                               for pr in range(n_pairs) for half in range(2)], axis=0)
        o_ref[j * BLOCK:(j + 1) * BLOCK, :] = o_t.T.astype(o_ref.dtype)


def _attention(q_a, k_a, v_a, band, sink, batch, layer, weights):
    n_tok = q_a.shape[0]
    nb = n_tok // batch // BLOCK
    rows = ATTN_QB * BLOCK
    steps = nb // ATTN_QB
    kv_spec = lambda r, f: pl.BlockSpec((r, ATTN_KV), f)
    prev = lambda bi, n: (bi * nb + jnp.maximum(n * ATTN_QB - 1, 0), 0)
    cur = lambda bi, n: (bi * steps + n, 0)
    nxt = lambda bi, n: (bi * nb + jnp.minimum((n + 1) * ATTN_QB, nb - 1), 0)
    n_steps = batch * steps
    assert all(w.shape[1] % (n_steps * BF16_SUBLANES) == 0 for w in weights)
    w_rows = [w.shape[1] // n_steps for w in weights]
    w_in_specs = [pl.BlockSpec((1, r, w.shape[2]), lambda bi, n: (layer, bi * steps + n, 0))
                  for r, w in zip(w_rows, weights)]
    w_out_specs = [pl.BlockSpec((r, w.shape[2]), lambda bi, n: (bi * steps + n, 0)) for r, w in zip(w_rows, weights)]
    return pl.pallas_call(
        _attn_kernel,
        grid=(batch, steps),
        in_specs=[pl.BlockSpec(memory_space=pltpu.SMEM),
                  pl.BlockSpec((rows, ATTN_Q), cur),
                  kv_spec(BLOCK, prev), kv_spec(rows, cur), kv_spec(BLOCK, nxt),
                  kv_spec(BLOCK, prev), kv_spec(rows, cur), kv_spec(BLOCK, nxt),
                  pl.BlockSpec(band.shape, lambda bi, n: (0, 0, 0))] + w_in_specs,
        out_specs=[pl.BlockSpec((rows, ATTN_Q), cur)] + w_out_specs,
        out_shape=[jax.ShapeDtypeStruct((n_tok, ATTN_Q), BF16)]
                  + [jax.ShapeDtypeStruct(w.shape[1:], BF16) for w in weights],
        compiler_params=pltpu.CompilerParams(dimension_semantics=("arbitrary", "arbitrary"),
                                             vmem_limit_bytes=VMEM_LIMIT),
        name="attn",
    )(sink, q_a, k_a, k_a, k_a, v_a, v_a, v_a, band, *weights)


def _gprep_kernel(yn_ref, ab_ref, gp_ref, w_ref, u_ref, qd_ref, kdt_ref, qkf_ref, qkb_ref, egl_ref):
    t_len = PREP_T

    ab = ab_ref[0]
    sp_in = ab + gp_ref[1:2, :]
    softplus = jnp.maximum(sp_in, 0.0) + jnp.log1p(jnp.exp(-jnp.abs(sp_in)))
    g = (-jnp.exp(gp_ref[0:1, :]) * LOG2E) * softplus
    beta = jax.nn.sigmoid(ab)

    r_t = lax.broadcasted_iota(jnp.int32, (t_len, t_len), 0)
    c_t = lax.broadcasted_iota(jnp.int32, (t_len, t_len), 1)
    same = (r_t // CHUNK) == (c_t // CHUNK)
    lower = jnp.where(same & (r_t >= c_t), 1.0, 0.0).astype(BF16)
    upper = jnp.where(same & (r_t <= c_t), 1.0, 0.0).astype(BF16)
    g_t = g.T[:2 * SUBLANES]
    cs_row = (_dot01_right(g_t, upper), _dot01_right(g_t, lower))
    pad_rows = jnp.zeros((LANES - 2 * SUBLANES, t_len), F32)
    cs_col = tuple(jnp.concatenate([r, pad_rows], axis=0).T for r in cs_row)
    beta_t = beta.T[:3 * SUBLANES]

    q_at = lambda h, r: yn_ref[h, r, :]
    k_at = lambda h, r: yn_ref[GDN_HEADS + h, r, :]
    v_at = lambda h, r: yn_ref[2 * GDN_HEADS + h, r, :]

    r_c = lax.broadcasted_iota(jnp.int32, (CHUNK, LANES), 0)
    lane = lax.broadcasted_iota(jnp.int32, (CHUNK, LANES), 1)
    is_fwd = lane < CHUNK
    c_c = lane % CHUNK
    eye = jnp.where(r_c == c_c, 1.0, 0.0).astype(F32)
    incl = (is_fwd & (r_c >= c_c)) | (~is_fwd & (r_c <= c_c))
    strict = (is_fwd & (r_c > c_c)) | (~is_fwd & (r_c < c_c))
    r_d = lax.broadcasted_iota(jnp.int32, (2 * CHUNK, LANES), 0)
    c_d = lax.broadcasted_iota(jnp.int32, (2 * CHUNK, LANES), 1)
    same_dir = (r_d // CHUNK) == (c_d // CHUNK)
    level_mask = lambda s_, r_, c_: ((r_ // (2 * s_)) == (c_ // (2 * s_))) & ((r_ // s_) != (c_ // s_))
    stack2 = lambda t: jnp.concatenate([t, t], axis=0)

    n_chunks = t_len // CHUNK
    rows = lambda c: slice(c * CHUNK, (c + 1) * CHUNK)
    pairs = [(c, h) for c in range(n_chunks) for h in range(GDN_HEADS)]
    qkk = {}
    for c, h in pairs:
        k16 = k_at(h, rows(c)).astype(BF16)
        qk16 = jnp.concatenate([q_at(h, rows(c)).astype(BF16), k16], axis=0)
        qkk[c, h] = _dot_nt(qk16, stack2(k16))

    bcast = lambda col: jnp.broadcast_to(col, (CHUNK, LANES))
    brow, grow, glast, a_mat, t_mat, kv16, qk_even = {}, {}, {}, {}, {}, {}, {}
    for key in pairs:
        c, h = key
        g_full = {}
        for d in range(N_DIR):
            j = d * GDN_HEADS + h
            r_last = c * CHUNK + (CHUNK - 1 if d == 0 else 0)
            g_full[d] = bcast(cs_col[d][rows(c), j:j + 1])
            brow[key, d] = beta_t[SUBLANES + j:SUBLANES + j + 1, rows(c)]
            glast[key, d] = cs_col[d][r_last:r_last + 1, j:j + 1]
            grow[key, d] = cs_row[d][j:j + 1, rows(c)]
        gcol2 = jnp.where(is_fwd, g_full[0], g_full[1])
        grow2 = jnp.concatenate([grow[key, 0], grow[key, 1]], axis=1)
        brow2 = jnp.concatenate([brow[key, 0], brow[key, 1]], axis=1)
        decay = jnp.exp2(jnp.where(incl, gcol2 - grow2, -jnp.inf))
        a_mat[key] = jnp.where(strict, qkk[key][CHUNK:] * decay, 0.0) * brow2
        t_mat[key] = eye - jnp.where(level_mask(1, r_c, c_c), a_mat[key], 0.0)
        qk = qkk[key][:CHUNK] * decay * brow2
        if c % 2 == 0:
            qk_even[h] = qk
        else:
            pair_rows = rows(c // 2)
            qkf_ref[0, h, pair_rows, :] = jnp.where(is_fwd, qk_even[h], pltpu.roll(qk, CHUNK, 1)).astype(qkf_ref.dtype)
            qkb_ref[0, h, pair_rows, :] = jnp.where(is_fwd, pltpu.roll(qk_even[h], CHUNK, 1), qk).astype(qkb_ref.dtype)
        eg = [jnp.exp2(g_full[d]) for d in range(N_DIR)]
        for d in range(N_DIR):
            qd_ref[0, d * GDN_HEADS + h, rows(c), :] = (q_at(h, rows(c)) * eg[d]).astype(qd_ref.dtype)
        kv16[key] = jnp.concatenate(
            [jnp.concatenate([k_at(h, rows(c)) * eg[d], v_at(h, rows(c))], axis=1) for d in range(N_DIR)],
            axis=0).astype(BF16)

    same_dir16 = jnp.where(same_dir, 1.0, 0.0).astype(BF16)
    block_diag = lambda t16: stack2(t16) * same_dir16
    a_bd = {key: block_diag(a_mat[key].astype(BF16)) for key in pairs}
    s = 2
    while s < CHUNK:
        lvl = level_mask(s, r_c, c_c)
        t16 = {key: t_mat[key].astype(BF16) for key in pairs}
        x_mat = {key: jnp.dot(t16[key], a_bd[key], preferred_element_type=F32) for key in pairs}
        y_mat = {key: jnp.dot(x_mat[key].astype(BF16), block_diag(t16[key]), preferred_element_type=F32)
                 for key in pairs}
        t_mat = {key: t_mat[key] - jnp.where(lvl, y_mat[key], 0.0) for key in pairs}
        s *= 2

    wu = {}
    for key in pairs:
        c, h = key
        t_sel = block_diag(t_mat[key].astype(BF16))
        wu[key] = jnp.dot(t_sel, kv16[key], preferred_element_type=F32)

    for c in range(n_chunks):
        egl_rows = []
        for d in range(N_DIR):
            for h in range(GDN_HEADS):
                key = (c, h)
                j = d * GDN_HEADS + h
                wu_d = wu[key][d * CHUNK:(d + 1) * CHUNK]
                w_ref[0, j, rows(c), :] = wu_d[:, :GDN_HEAD_DIM].astype(w_ref.dtype)
                u_ref[0, j, rows(c), :] = wu_d[:, GDN_HEAD_DIM:].astype(u_ref.dtype)
                egl_rows.append(jnp.broadcast_to(jnp.exp2(glast[key, d]), (1, LANES)))
        egl_ref[0, c] = jnp.concatenate(egl_rows, axis=0)
    for cp in range(n_chunks // 2):
        for h in range(GDN_HEADS):
            k_t = k_at(h, slice(2 * cp * CHUNK, (2 * cp + 2) * CHUNK)).T
            for d in range(N_DIR):
                fac = jnp.concatenate([brow[(c, h), d] * jnp.exp2(glast[(c, h), d] - grow[(c, h), d])
                                       for c in (2 * cp, 2 * cp + 1)], axis=1)
                kdt_ref[0, d * GDN_HEADS + h, cp] = (k_t * fac).astype(kdt_ref.dtype)


def _gdn_prep(yn, ab, gate_par):
    b, s, _ = ab.shape
    nt = s // PREP_T
    cpb = PREP_T // CHUNK
    nc = s // CHUNK
    chain = lambda last: pl.BlockSpec((1, N_CHAIN, PREP_T, last), lambda bi, i: (bi, 0, i, 0))
    return pl.pallas_call(
        _gprep_kernel,
        grid=(b, nt),
        in_specs=[pl.BlockSpec((3 * GDN_W // LANES, PREP_T, LANES), lambda bi, i: (0, bi * nt + i, 0)),
                  pl.BlockSpec((1, PREP_T, LANES), lambda bi, i: (bi, i, 0)),
                  pl.BlockSpec((SUBLANES, LANES), lambda bi, i: (0, 0))],
        out_specs=[chain(GDN_HEAD_DIM), chain(GDN_HEAD_DIM), chain(GDN_HEAD_DIM),
                   pl.BlockSpec((1, N_CHAIN, cpb // 2, GDN_HEAD_DIM, 2 * CHUNK), lambda bi, i: (bi, 0, i, 0, 0)),
                   pl.BlockSpec((1, GDN_HEADS, PREP_T // 2, LANES), lambda bi, i: (bi, 0, i, 0)),
                   pl.BlockSpec((1, GDN_HEADS, PREP_T // 2, LANES), lambda bi, i: (bi, 0, i, 0)),
                   pl.BlockSpec((1, cpb, N_CHAIN, LANES), lambda bi, i: (bi, i, 0, 0))],
        out_shape=[jax.ShapeDtypeStruct((b, N_CHAIN, s, GDN_HEAD_DIM), BF16),
                   jax.ShapeDtypeStruct((b, N_CHAIN, s, GDN_HEAD_DIM), BF16),
                   jax.ShapeDtypeStruct((b, N_CHAIN, s, GDN_HEAD_DIM), BF16),
                   jax.ShapeDtypeStruct((b, N_CHAIN, nc // 2, GDN_HEAD_DIM, 2 * CHUNK), BF16),
                   jax.ShapeDtypeStruct((b, GDN_HEADS, s // 2, LANES), BF16),
                   jax.ShapeDtypeStruct((b, GDN_HEADS, s // 2, LANES), BF16),
                   jax.ShapeDtypeStruct((b, nc, N_CHAIN, LANES), F32)],
        compiler_params=pltpu.CompilerParams(dimension_semantics=("arbitrary", "arbitrary"),
                                             vmem_limit_bytes=VMEM_LIMIT),
        name="gdn_prep",
    )(yn, ab, gate_par)


def _gscan_kernel(ef_ref, eb_ref, wf_ref, wb_ref, uf_ref, ub_ref, qf_ref, qb_ref, kf_ref, kb_ref,
                  pf_ref, pb_ref, of_ref, ob_ref, state_ref):
    n_batch = wf_ref.shape[0]

    @pl.when(pl.program_id(0) == 0)
    def _():
        state_ref[...] = jnp.zeros_like(state_ref)

    dirs = ((wf_ref, uf_ref, qf_ref, kf_ref, pf_ref, of_ref), (wb_ref, ub_ref, qb_ref, kb_ref, pb_ref, ob_ref))
    chains = [(bi, d, h) for bi in range(n_batch) for d in range(N_DIR) for h in range(GDN_HEADS)]
    slot = lambda bi, d, h: (bi * N_DIR + d) * GDN_HEADS + h
    st = {key: state_ref[slot(*key)] for key in chains}

    for sub in range(SCAN_CHUNKS):
        local = (sub, SCAN_CHUNKS - 1 - sub)
        rows = [slice(c * CHUNK, (c + 1) * CHUNK) for c in local]
        r = {}
        for key in chains:
            bi, d, h = key
            w_ref, _, q_ref = dirs[d][:3]
            wq = jnp.concatenate([w_ref[bi, h, rows[d]], q_ref[bi, h, rows[d]]], axis=0)
            r[key] = _dot(wq, st[key])
        intra = {}
        for key in chains:
            bi, d, h = key
            u_ref, k_ref, p_ref = dirs[d][1], dirs[d][3], dirs[d][4]
            v_new = (u_ref[bi, h, rows[d]].astype(F32) - r[key][:CHUNK]).astype(BF16)
            zeros = jnp.zeros_like(v_new)
            v_pad = jnp.concatenate([v_new, zeros] if local[d] % 2 == 0 else [zeros, v_new], axis=0)
            pair_rows = slice(local[d] // 2 * CHUNK, (local[d] // 2 + 1) * CHUNK)
            pk = jnp.concatenate([p_ref[bi, h, pair_rows], k_ref[bi, h, local[d] // 2]], axis=0)
            res = jnp.dot(pk, v_pad, preferred_element_type=F32)
            intra[key] = res[:CHUNK]
            j = d * GDN_HEADS + h
            egl = (ef_ref, eb_ref)[d][bi, local[d], j:j + 1, :]
            st[key] = st[key] * egl + res[CHUNK:]
        for bi in range(n_batch):
            for d in range(N_DIR):
                o_ref = dirs[d][5]
                o_ref[bi, rows[d], :] = jnp.concatenate(
                    [r[bi, d, h][CHUNK:] + intra[bi, d, h] for h in range(GDN_HEADS)], axis=-1).astype(o_ref.dtype)

    for key in chains:
        state_ref[slot(*key)] = st[key]


def _gdn_scan(egl, w, u, qd, kdt, qk_f, qk_b):
    b, _, s, _ = w.shape
    rows = SCAN_CHUNKS * CHUNK
    n_steps = s // rows
    fwd_i = lambda t: t
    bwd_i = lambda t: n_steps - 1 - t
    chain = lambda d, at, last: pl.BlockSpec((b, GDN_HEADS, rows, last), lambda t: (0, d, at(t), 0))
    kspec = lambda d, at: pl.BlockSpec((b, GDN_HEADS, SCAN_CHUNKS // 2, GDN_HEAD_DIM, 2 * CHUNK),
                                       lambda t: (0, d, at(t), 0, 0))
    pspec = lambda at: pl.BlockSpec((b, GDN_HEADS, rows // 2, LANES), lambda t: (0, 0, at(t), 0))
    espec = lambda at: pl.BlockSpec((b, SCAN_CHUNKS, N_CHAIN, LANES), lambda t: (0, at(t), 0, 0))
    dk = GDN_HEAD_DIM
    return pl.pallas_call(
        _gscan_kernel,
        grid=(n_steps,),
        in_specs=[espec(fwd_i), espec(bwd_i),
                  chain(0, fwd_i, dk), chain(1, bwd_i, dk), chain(0, fwd_i, dk), chain(1, bwd_i, dk),
                  chain(0, fwd_i, dk), chain(1, bwd_i, dk), kspec(0, fwd_i), kspec(1, bwd_i),
                  pspec(fwd_i), pspec(bwd_i)],
        out_specs=[pl.BlockSpec((b, rows, GDN_W), lambda t: (0, fwd_i(t), 0)),
                   pl.BlockSpec((b, rows, GDN_W), lambda t: (0, bwd_i(t), 0))],
        out_shape=[jax.ShapeDtypeStruct((b, s, GDN_W), BF16),
                   jax.ShapeDtypeStruct((b, s, GDN_W), BF16)],
        scratch_shapes=[pltpu.VMEM((b * N_CHAIN, GDN_HEAD_DIM, GDN_HEAD_DIM), F32)],
        compiler_params=pltpu.CompilerParams(dimension_semantics=("arbitrary",),
                                             vmem_limit_bytes=VMEM_LIMIT),
        name="gdn_scan",
    )(egl, egl, w, w, u, u, qd, qd, kdt, kdt, qk_f, qk_b)


def _ffn_kernel(x_ref, attn_ref, of_ref, ob_ref, z_ref, gnw_ref, wo_hbm, fnw_ref, w1_hbm, w2_hbm, onw_ref,
                o_ref, wo_ref, w1_ref, w2_ref, sem):
    @pl.when(pl.program_id(0) == 0)
    def _():
        copies = [pltpu.make_async_copy(src, dst, sem.at[k])
                  for k, (src, dst) in enumerate(((wo_hbm, wo_ref), (w1_hbm, w1_ref), (w2_hbm, w2_ref)))]
        for copy in copies:
            copy.start()
        for copy in copies:
            copy.wait()

    subs = [slice(r0, r0 + FFN_SUB) for r0 in range(0, FFN_TM, FFN_SUB)]
    rms = lambda t: t * lax.rsqrt(jnp.mean(t * t, axis=-1, keepdims=True) + EPS)
    hres, hn, act, acc = {}, {}, {}, {}
    for r in subs:
        o = of_ref[r, :].astype(F32) + ob_ref[r, :].astype(F32)
        heads = [rms(o[:, h * GDN_HEAD_DIM:(h + 1) * GDN_HEAD_DIM]) * gnw_ref[...] for h in range(GDN_HEADS)]
        z = z_ref[r, :]
        gdn = jnp.concatenate(heads, axis=-1) * (z * jax.nn.sigmoid(z))
        hres[r.start] = (x_ref[r, :] + jnp.dot(attn_ref[r, :], wo_ref[:ATTN_Q, :], preferred_element_type=F32)
                         + _dot(gdn, wo_ref[ATTN_Q:, :]))
    for r in subs:
        hn[r.start] = (rms(hres[r.start]) * fnw_ref[...]).astype(BF16)
        act[r.start] = jnp.dot(hn[r.start], w1_ref[...], preferred_element_type=F32)
    for r in subs:
        a = jnp.square(jnp.maximum(act[r.start], 0.0)).astype(BF16)
        acc[r.start] = hres[r.start] + jnp.dot(a, w2_ref[...], preferred_element_type=F32)
    for r in subs:
        o_ref[r, :] = rms(acc[r.start]) * onw_ref[...]


def _out_ffn(x2, attn, o_f, o_b, z, gnw, wo, fnw, w1, w2, onw):
    n = x2.shape[0]
    assert wo.shape == (D_MODEL, D_MODEL) and w1.shape == (D_MODEL, D_FF) and w2.shape == (D_FF, D_MODEL)
    row = lambda w: pl.BlockSpec((FFN_TM, w), lambda i: (i, 0))
    full = lambda a: pl.BlockSpec(a.shape, lambda i: (0, 0))
    hbm = pl.BlockSpec(memory_space=pl.ANY)
    return pl.pallas_call(
        _ffn_kernel,
        grid=(n // FFN_TM,),
        in_specs=[row(D_MODEL), row(ATTN_Q), row(GDN_W), row(GDN_W), row(GDN_W),
                  full(gnw), hbm, full(fnw), hbm, hbm, full(onw)],
        out_specs=row(D_MODEL),
        out_shape=jax.ShapeDtypeStruct((n, D_MODEL), F32),
        scratch_shapes=[pltpu.VMEM((D_MODEL, D_MODEL), BF16),
                        pltpu.VMEM((D_MODEL, D_FF), BF16),
                        pltpu.VMEM((D_FF, D_MODEL), BF16),
                        pltpu.SemaphoreType.DMA((3,))],
        compiler_params=pltpu.CompilerParams(dimension_semantics=("arbitrary",),
                                             vmem_limit_bytes=VMEM_LIMIT),
        name="out_ffn",
    )(x2, attn, o_f, o_b, z, gnw, wo, fnw, w1, w2, onw)


def _layer(h, band, norm_mix_w, w_in, layer, attn_sink, conv_w, gdn_a_log, gdn_dt_bias, gdn_norm_w,
           w_out, norm_ffn_w, w_ffn_in, w_ffn_out, out_norm_w):
    b, s, _ = h.shape
    n = b * s
    x2 = h.reshape(n, D_MODEL)
    q_a, k_a, v_a, yn, z_g, ab = _proj(x2, norm_mix_w.reshape(1, D_MODEL), w_in, conv_w, layer, s)

    attn, wo_bf, w1_bf, w2_bf = _attention(q_a, k_a, v_a, band, attn_sink, b, layer, (w_out, w_ffn_in, w_ffn_out))

    gate_par = jnp.zeros((SUBLANES, LANES), F32)
    gate_par = gate_par.at[0, :N_CHAIN].set(gdn_a_log.reshape(-1)).at[1, :N_CHAIN].set(gdn_dt_bias.reshape(-1))
    w_c, u_c, q_dec, k_dec_t, qk_f, qk_b, egl = _gdn_prep(yn, ab.reshape(b, s, LANES), gate_par)
    o_f, o_b = _gdn_scan(egl, w_c, u_c, q_dec, k_dec_t, qk_f, qk_b)

    out = _out_ffn(x2, attn, o_f.reshape(n, GDN_W), o_b.reshape(n, GDN_W), z_g,
                   gdn_norm_w.reshape(1, GDN_HEAD_DIM), wo_bf, norm_ffn_w.reshape(1, D_MODEL),
                   w1_bf, w2_bf, out_norm_w.reshape(1, D_MODEL))
    return out.reshape(b, s, D_MODEL)


def kernel(x, norm_mix_w, w_in, rel_bias, attn_sink, conv_w, gdn_a_log, gdn_dt_bias, gdn_norm_w, w_out,
           norm_ffn_w, w_ffn_in, w_ffn_out, norm_final_w):
    depth = w_in.shape[0]
    assert depth == 1, "the fused output kernel applies the final norm after the single trunk layer"
    rel = (np.arange(3 * BLOCK)[None, :] - BLOCK) - np.arange(BLOCK)[:, None]
    bucket = _t5_buckets(jnp.asarray(rel, dtype=jnp.int32))
    band = _bias_band(rel_bias.T, bucket.T)
    return _layer(x, band, norm_mix_w[0], w_in, 0, attn_sink[0], conv_w, gdn_a_log[0], gdn_dt_bias[0],
                  gdn_norm_w[0], w_out, norm_ffn_w[0], w_ffn_in, w_ffn_out, norm_final_w)
```

```python
import functools
import math

import jax
import jax.numpy as jnp
import numpy as np
from jax import lax
from jax.experimental import pallas as pl
from jax.experimental.pallas import tpu as pltpu

F32 = jnp.float32
BF16 = jnp.bfloat16

D_MODEL = 1024
ATTN_HEADS = 8
ATTN_KV_HEADS = 2
ATTN_HEAD_DIM = 64
ATTN_GROUP = ATTN_HEADS // ATTN_KV_HEADS
WINDOW = 128
BLOCK = 128
N_BUCKETS = 32
MAX_DISTANCE = 128
GDN_HEADS = 4
GDN_HEAD_DIM = 128
CONV_K = 5
CHUNK = 64
N_DIR = 2
N_CHAIN = N_DIR * GDN_HEADS
D_FF = 4 * D_MODEL
EPS = 1e-6
LOG2E = math.log2(math.e)
ATTN_Q = ATTN_HEADS * ATTN_HEAD_DIM
ATTN_KV = ATTN_KV_HEADS * ATTN_HEAD_DIM
GDN_W = GDN_HEADS * GDN_HEAD_DIM
LANES = 128
SUBLANES = 8
BF16_SUBLANES = 16
VMEM_LIMIT = 56 * 1024 * 1024

PROJ_TM = 1024
PROJ_SUB = 256
PROJ_CAST_ROWS = 256
PROJ_CAST_SLOTS = 6
ATTN_QB = 8
PREP_T = 1024
ROW_STRIDE = 4
SCAN_CHUNKS = 8
FFN_TM = 512
FFN_SUB = 256


def _dot(a, b):
    return jnp.dot(a.astype(BF16), b.astype(BF16), preferred_element_type=F32)


def _dot_nt(a, b):
    return lax.dot_general(a.astype(BF16), b.astype(BF16), (((1,), (1,)), ((), ())),
                           preferred_element_type=F32)


def _split3(x):
    hi = x.astype(BF16)
    r1 = x - hi.astype(F32)
    mid = r1.astype(BF16)
    lo = (r1 - mid.astype(F32)).astype(BF16)
    return hi, mid, lo


def _dot01_right(x, m01):
    n = x.shape[0]
    parts = jnp.dot(jnp.concatenate(_split3(x), axis=0), m01, preferred_element_type=F32)
    return parts[:n] + parts[n:2 * n] + parts[2 * n:]


def _conv_silu_norm(xe_ref, cw_ref, yn_ref, r0, n_out):
    halo = CONV_K // 2
    n_rows = n_out // ROW_STRIDE
    n_slab = 3 * GDN_W // LANES
    units = [(sb, ph) for sb in range(n_slab) for ph in range(ROW_STRIDE)]
    half_taps = [cw_ref[j] * 0.5 for j in range(CONV_K)]
    yv = {}
    for sb, ph in units:
        lanes = slice(sb * LANES, (sb + 1) * LANES)
        acc = None
        for j in range(CONV_K):
            win = xe_ref[sb, pl.ds(SUBLANES - halo + j + ph, n_rows, stride=ROW_STRIDE), :]
            term = half_taps[j][:, lanes] * win
            acc = term if acc is None else acc + term
        yv[sb, ph] = acc
    for key in units:
        yv[key] = yv[key] + yv[key] * jnp.tanh(yv[key])
    for sb, ph in units:
        if sb < 2 * GDN_HEADS:
            scale = lax.rsqrt(jnp.sum(yv[sb, ph] * yv[sb, ph], axis=-1, keepdims=True) + EPS)
            if sb < GDN_HEADS:
                scale = scale * (GDN_HEAD_DIM ** -0.5)
            yv[sb, ph] = yv[sb, ph] * scale
    for sb, ph in units:
        yn_ref[sb, pl.ds(r0 + ph, n_rows, stride=ROW_STRIDE), :] = yv[sb, ph]


def _proj_kernel(layer, tiles_per_seq, x_ref, xp_ref, xn_ref, nw_ref, wt_hbm, cw_ref,
                 qa_ref, ka_ref, va_ref, yn_ref, z_ref, ab_ref, wb_ref, stage_ref, sem, *xe_refs):
    d_in = wt_hbm.shape[1]
    d_main = d_in // LANES * LANES
    step = pl.program_id(0)

    @pl.when(step == 0)
    def _():
        starts = list(range(0, d_main, PROJ_CAST_ROWS)) + [d_main]
        sizes = [min(PROJ_CAST_ROWS, d_main - c0) for c0 in starts[:-1]] + [d_in - d_main]
        slot = lambda k: k % PROJ_CAST_SLOTS
        copies = [pltpu.make_async_copy(wt_hbm.at[layer, pl.ds(c0, nr)], stage_ref.at[slot(k), pl.ds(0, nr)],
                                        sem.at[slot(k)]) for k, (c0, nr) in enumerate(zip(starts, sizes))]
        lane = lax.broadcasted_iota(jnp.int32, (D_MODEL, PROJ_CAST_ROWS), 1)
        for k in range(min(PROJ_CAST_SLOTS - 1, len(copies))):
            copies[k].start()
        for k, (c0, nr) in enumerate(zip(starts, sizes)):
            if k + PROJ_CAST_SLOTS - 1 < len(copies):
                copies[k + PROJ_CAST_SLOTS - 1].start()
            copies[k].wait()
            cols = stage_ref[slot(k)].T
            if nr < PROJ_CAST_ROWS:
                cols = jnp.where(lane < nr, cols, 0.0)
            width = min(PROJ_CAST_ROWS, wb_ref.shape[1] - c0)
            wb_ref[:, c0:c0 + width] = cols[:, :width].astype(BF16)

    o_q, o_k, o_v, o_g, o_z, o_ab = 0, ATTN_Q, ATTN_Q + ATTN_KV, ATTN_Q + 2 * ATTN_KV, \
        ATTN_Q + 2 * ATTN_KV + 3 * GDN_W, ATTN_Q + 2 * ATTN_KV + 4 * GDN_W
    n_slab = 3 * GDN_W // LANES

    def normed(x):
        ms = jnp.mean(x * x, axis=-1, keepdims=True)
        return (x * lax.rsqrt(ms + EPS) * nw_ref[...]).astype(BF16)

    n_sub = PROJ_TM // PROJ_SUB

    def to_slabs(xe_ref, y, row0):
        for sb in range(n_slab):
            xe_ref[sb, row0:row0 + y.shape[0], :] = y[:, sb * LANES:(sb + 1) * LANES]

    def put(k, y):
        to_slabs(xe_refs[k], y, SUBLANES)
        if k > 0:
            to_slabs(xe_refs[k - 1], y[:SUBLANES], SUBLANES + PROJ_SUB)
        if k + 1 < n_sub:
            to_slabs(xe_refs[k + 1], y[PROJ_SUB - SUBLANES:], 0)

    i = step % tiles_per_seq
    outs = ((qa_ref, o_q, ATTN_Q, ATTN_HEAD_DIM ** -0.5 * LOG2E), (ka_ref, o_k, ATTN_KV, None), (va_ref, o_v, ATTN_KV, None),
            (z_ref, o_z, GDN_W, None), (ab_ref, o_ab, LANES, None))
    for k in range(n_sub):
        rows = slice(k * PROJ_SUB, (k + 1) * PROJ_SUB)
        if k == 0:
            xg = normed(jnp.concatenate([xp_ref[...], xn_ref[...], x_ref[rows, :]], axis=0))
            yg = jnp.dot(xg, wb_ref[:, o_g:o_z], preferred_element_type=F32)
            to_slabs(xe_refs[0], jnp.where(i > 0, yg[:SUBLANES], 0.0), 0)
            to_slabs(xe_refs[n_sub - 1], jnp.where(i < tiles_per_seq - 1, yg[SUBLANES:2 * SUBLANES], 0.0),
                     SUBLANES + PROJ_SUB)
            put(0, yg[2 * SUBLANES:])
            xn = xg[2 * SUBLANES:]
        else:
            xn = normed(x_ref[rows, :])
            put(k, jnp.dot(xn, wb_ref[:, o_g:o_z], preferred_element_type=F32))
            _conv_silu_norm(xe_refs[k - 1], cw_ref, yn_ref, (k - 1) * PROJ_SUB, PROJ_SUB)
        for ref, c0, width, scale in outs:
            y = jnp.dot(xn, wb_ref[:, c0:c0 + width], preferred_element_type=F32)
            if scale is not None:
                y = y * scale
            ref[rows, :] = y.astype(ref.dtype)
    _conv_silu_norm(xe_refs[n_sub - 1], cw_ref, yn_ref, PROJ_TM - PROJ_SUB, PROJ_SUB)


def _proj(x2, norm_w, w_in, conv_w, layer, seq_len):
    n = x2.shape[0]
    d_in = w_in.shape[2]
    d_pad = d_in // LANES * LANES + LANES
    assert d_pad == ATTN_Q + 2 * ATTN_KV + 4 * GDN_W + LANES and d_in - (d_pad - LANES) == 2 * N_CHAIN
    assert seq_len % PROJ_TM == 0
    w_t = jnp.swapaxes(w_in, 1, 2)
    nh8 = PROJ_TM // SUBLANES
    row = lambda w: pl.BlockSpec((PROJ_TM, w), lambda i: (i, 0))
    return pl.pallas_call(
        functools.partial(_proj_kernel, layer, seq_len // PROJ_TM),
        grid=(n // PROJ_TM,),
        in_specs=[row(D_MODEL),
                  pl.BlockSpec((SUBLANES, D_MODEL), lambda i: (jnp.maximum(i * nh8 - 1, 0), 0)),
                  pl.BlockSpec((SUBLANES, D_MODEL), lambda i: (jnp.minimum((i + 1) * nh8, n // SUBLANES - 1), 0)),
                  pl.BlockSpec((1, D_MODEL), lambda i: (0, 0)),
                  pl.BlockSpec(memory_space=pl.ANY),
                  pl.BlockSpec((CONV_K, 1, 3 * GDN_W), lambda i: (0, layer, 0))],
        out_specs=[row(ATTN_Q), row(ATTN_KV), row(ATTN_KV),
                   pl.BlockSpec((3 * GDN_W // LANES, PROJ_TM, LANES), lambda i: (0, i, 0)),
                   row(GDN_W), row(LANES)],
        out_shape=[jax.ShapeDtypeStruct((n, ATTN_Q), BF16),
                   jax.ShapeDtypeStruct((n, ATTN_KV), BF16),
                   jax.ShapeDtypeStruct((n, ATTN_KV), BF16),
                   jax.ShapeDtypeStruct((3 * GDN_W // LANES, n, LANES), F32),
                   jax.ShapeDtypeStruct((n, GDN_W), F32),
                   jax.ShapeDtypeStruct((n, LANES), F32)],
        scratch_shapes=[pltpu.VMEM((D_MODEL, d_pad), BF16),
                        pltpu.VMEM((PROJ_CAST_SLOTS, PROJ_CAST_ROWS, D_MODEL), F32),
                        pltpu.SemaphoreType.DMA((PROJ_CAST_SLOTS,))]
        + [pltpu.VMEM((3 * GDN_W // LANES, PROJ_SUB + 2 * SUBLANES, LANES), F32)] * (PROJ_TM // PROJ_SUB),
        compiler_params=pltpu.CompilerParams(dimension_semantics=("arbitrary",),
                                             vmem_limit_bytes=VMEM_LIMIT),
        name="proj",
    )(x2, x2, x2, norm_w, w_t, jnp.swapaxes(conv_w, 0, 1))


def _bias_kernel(relb_ref, bucket_ref, o_ref):
    bucket = bucket_ref[...]
    key = lax.broadcasted_iota(jnp.int32, (3 * BLOCK, BLOCK), 0)
    qry = lax.broadcasted_iota(jnp.int32, (3 * BLOCK, BLOCK), 1)
    in_window = jnp.abs(key - BLOCK - qry) <= WINDOW
    acc = [jnp.zeros((3 * BLOCK, BLOCK), F32) for _ in range(ATTN_HEADS)]
    for b in range(N_BUCKETS):
        hit = bucket == b
        acc = [jnp.where(hit, relb_ref[h, b], acc[h]) for h in range(ATTN_HEADS)]
    for h in range(ATTN_HEADS):
        o_ref[h // 2, :, (h % 2) * BLOCK:(h % 2 + 1) * BLOCK] = jnp.where(in_window, acc[h] * LOG2E, -1e30)


def _bias_band(rel_bias_t, bucket_t):
    shape = (ATTN_HEADS // 2, 3 * BLOCK, 2 * BLOCK)
    return pl.pallas_call(
        _bias_kernel,
        in_specs=[pl.BlockSpec(memory_space=pltpu.SMEM),
                  pl.BlockSpec((3 * BLOCK, BLOCK), lambda: (0, 0))],
        out_specs=pl.BlockSpec(shape, lambda: (0, 0, 0)),
        out_shape=jax.ShapeDtypeStruct(shape, F32),
        name="bias_band",
    )(rel_bias_t, bucket_t)


def _t5_buckets(rel):
    nb = N_BUCKETS // 2
    max_exact = nb // 2
    base = jnp.where(rel > 0, nb, 0)
    n = jnp.abs(rel)
    log_ratio = jnp.log(jnp.maximum(n, 1).astype(jnp.float32) / max_exact) / math.log(MAX_DISTANCE / max_exact)
    large = jnp.minimum(max_exact + (log_ratio * (nb - max_exact)).astype(jnp.int32), nb - 1)
    return base + jnp.where(n < max_exact, n, large)


def _attn_kernel(sink_ref, q_ref, kp_ref, kc_ref, kn_ref, vp_ref, vc_ref, vn_ref, bias_ref, wo_ref, w1_ref, w2_ref,
                 o_ref, wo_bf_ref, w1_bf_ref, w2_bf_ref):
    for src, dst in ((wo_ref, wo_bf_ref), (w1_ref, w1_bf_ref), (w2_ref, w2_bf_ref)):
        dst[...] = src[0].astype(dst.dtype)
    n = pl.program_id(1)
    last = pl.num_programs(1) - 1
    kband = jnp.concatenate([kp_ref[...], kc_ref[...], kn_ref[...]], axis=0)
    vband = jnp.concatenate([vp_ref[...], vc_ref[...], vn_ref[...]], axis=0)
    vband_t = vband.astype(F32).T.astype(BF16)
    key = lax.broadcasted_iota(jnp.int32, (3 * BLOCK, 1), 0)
    first_head = lax.broadcasted_iota(jnp.int32, (1, 2 * BLOCK), 1) < BLOCK
    head = lambda t, i: t[:, i * ATTN_HEAD_DIM:(i + 1) * ATTN_HEAD_DIM]
    n_pairs = ATTN_HEADS // 2
    kv_of = lambda pr: (2 * pr) // ATTN_GROUP
    units = [(j, pr) for j in range(ATTN_QB) for pr in range(n_pairs)]
    band_rows = lambda j: slice(j * BLOCK, (j + 3) * BLOCK)

    scores = {}
    for j, pr in units:
        qj = q_ref[j * BLOCK:(j + 1) * BLOCK, :]
        q2 = jnp.concatenate([head(qj, 2 * pr), head(qj, 2 * pr + 1)], axis=0)
        scores[j, pr] = _dot_nt(head(kband, kv_of(pr))[band_rows(j)], q2)
    probs, dens = {}, {}
    for j, pr in units:
        s = scores[j, pr] + bias_ref[pr]
        if j == 0:
            s = jnp.where((key < BLOCK) & (n == 0), -1e30, s)
        if j == ATTN_QB - 1:
            s = jnp.where((key >= 2 * BLOCK) & (n == last), -1e30, s)
        sink = jnp.where(first_head, sink_ref[2 * pr], sink_ref[2 * pr + 1]) * LOG2E
        m = jnp.maximum(jnp.max(s, axis=0, keepdims=True), sink)
        probs[j, pr] = jnp.exp2(s - m).astype(BF16)
        dens[j, pr] = jnp.exp2(sink - m)
    outs_t = {}
    ones_rows = jnp.ones((2 * SUBLANES, 3 * BLOCK), BF16)
    for j, pr in units:
        kv = kv_of(pr)
        v_t = vband_t[kv * ATTN_HEAD_DIM:(kv + 1) * ATTN_HEAD_DIM, band_rows(j)]
        pv = jnp.dot(jnp.concatenate([v_t, ones_rows], axis=0), probs[j, pr], preferred_element_type=F32)
        outs_t[j, pr] = pv[:ATTN_HEAD_DIM] / (pv[ATTN_HEAD_DIM:ATTN_HEAD_DIM + 1] + dens[j, pr])
    for j in range(ATTN_QB):
        o_t = jnp.concatenate([outs_t[j, pr][:, half * BLOCK:(half + 1) * BLOCK]
                               for pr in range(n_pairs) for half in range(2)], axis=0)
        o_ref[j * BLOCK:(j + 1) * BLOCK, :] = o_t.T.astype(o_ref.dtype)


def _attention(q_a, k_a, v_a, band, sink, batch, layer, weights):
    n_tok = q_a.shape[0]
    nb = n_tok // batch // BLOCK
    rows = ATTN_QB * BLOCK
    steps = nb // ATTN_QB
    kv_spec = lambda r, f: pl.BlockSpec((r, ATTN_KV), f)
    prev = lambda bi, n: (bi * nb + jnp.maximum(n * ATTN_QB - 1, 0), 0)
    cur = lambda bi, n: (bi * steps + n, 0)
    nxt = lambda bi, n: (bi * nb + jnp.minimum((n + 1) * ATTN_QB, nb - 1), 0)
    n_steps = batch * steps
    assert all(w.shape[1] % (n_steps * BF16_SUBLANES) == 0 for w in weights)
    w_rows = [w.shape[1] // n_steps for w in weights]
    w_in_specs = [pl.BlockSpec((1, r, w.shape[2]), lambda bi, n: (layer, bi * steps + n, 0))
                  for r, w in zip(w_rows, weights)]
    w_out_specs = [pl.BlockSpec((r, w.shape[2]), lambda bi, n: (bi * steps + n, 0)) for r, w in zip(w_rows, weights)]
    return pl.pallas_call(
        _attn_kernel,
        grid=(batch, steps),
        in_specs=[pl.BlockSpec(memory_space=pltpu.SMEM),
                  pl.BlockSpec((rows, ATTN_Q), cur),
                  kv_spec(BLOCK, prev), kv_spec(rows, cur), kv_spec(BLOCK, nxt),
                  kv_spec(BLOCK, prev), kv_spec(rows, cur), kv_spec(BLOCK, nxt),
                  pl.BlockSpec(band.shape, lambda bi, n: (0, 0, 0))] + w_in_specs,
        out_specs=[pl.BlockSpec((rows, ATTN_Q), cur)] + w_out_specs,
        out_shape=[jax.ShapeDtypeStruct((n_tok, ATTN_Q), BF16)]
                  + [jax.ShapeDtypeStruct(w.shape[1:], BF16) for w in weights],
        compiler_params=pltpu.CompilerParams(dimension_semantics=("arbitrary", "arbitrary"),
                                             vmem_limit_bytes=VMEM_LIMIT),
        name="attn",
    )(sink, q_a, k_a, k_a, k_a, v_a, v_a, v_a, band, *weights)


def _gprep_kernel(yn_ref, ab_ref, gp_ref, w_ref, u_ref, qd_ref, kdt_ref, qkf_ref, qkb_ref, egl_ref):
    t_len = PREP_T

    ab = ab_ref[0]
    sp_in = ab + gp_ref[1:2, :]
    softplus = jnp.maximum(sp_in, 0.0) + jnp.log1p(jnp.exp(-jnp.abs(sp_in)))
    g = (-jnp.exp(gp_ref[0:1, :]) * LOG2E) * softplus
    beta = jax.nn.sigmoid(ab)

    r_t = lax.broadcasted_iota(jnp.int32, (LANES, LANES), 0)
    c_t = lax.broadcasted_iota(jnp.int32, (LANES, LANES), 1)
    same = (r_t // CHUNK) == (c_t // CHUNK)
    lower = jnp.where(same & (r_t >= c_t), 1.0, 0.0).astype(BF16)
    upper = jnp.where(same & (r_t <= c_t), 1.0, 0.0).astype(BF16)
    g_t = g.T[:2 * SUBLANES]
    n_blk = t_len // LANES
    g_blocks = jnp.concatenate([g_t[:, b * LANES:(b + 1) * LANES] for b in range(n_blk)], axis=0)
    unstack = lambda t: jnp.concatenate([t[b * 2 * SUBLANES:(b + 1) * 2 * SUBLANES] for b in range(n_blk)], axis=1)
    cs_row = (unstack(_dot01_right(g_blocks, upper)), unstack(_dot01_right(g_blocks, lower)))
    pad_rows = jnp.zeros((LANES - 2 * SUBLANES, t_len), F32)
    cs_col = tuple(jnp.concatenate([r, pad_rows], axis=0).T for r in cs_row)
    beta_t = beta.T[:3 * SUBLANES]

    q_at = lambda h, r: yn_ref[h, r, :]
    k_at = lambda h, r: yn_ref[GDN_HEADS + h, r, :]
    v_at = lambda h, r: yn_ref[2 * GDN_HEADS + h, r, :]

    r_c = lax.broadcasted_iota(jnp.int32, (CHUNK, LANES), 0)
    lane = lax.broadcasted_iota(jnp.int32, (CHUNK, LANES), 1)
    is_fwd = lane < CHUNK
    c_c = lane % CHUNK
    eye = jnp.where(r_c == c_c, 1.0, 0.0).astype(F32)
    incl = (is_fwd & (r_c >= c_c)) | (~is_fwd & (r_c <= c_c))
    strict = (is_fwd & (r_c > c_c)) | (~is_fwd & (r_c < c_c))
    r_d = lax.broadcasted_iota(jnp.int32, (2 * CHUNK, LANES), 0)
    c_d = lax.broadcasted_iota(jnp.int32, (2 * CHUNK, LANES), 1)
    same_dir = (r_d // CHUNK) == (c_d // CHUNK)
    level_mask = lambda s_, r_, c_: ((r_ // (2 * s_)) == (c_ // (2 * s_))) & ((r_ // s_) != (c_ // s_))
    stack2 = lambda t: jnp.concatenate([t, t], axis=0)

    n_chunks = t_len // CHUNK
    rows = lambda c: slice(c * CHUNK, (c + 1) * CHUNK)
    pairs = [(c, h) for c in range(n_chunks) for h in range(GDN_HEADS)]
    qkk = {}
    for c, h in pairs:
        k16 = k_at(h, rows(c)).astype(BF16)
        qk16 = jnp.concatenate([q_at(h, rows(c)).astype(BF16), k16], axis=0)
        qkk[c, h] = _dot_nt(qk16, stack2(k16))

    bcast = lambda col: jnp.broadcast_to(col, (CHUNK, LANES))
    brow, grow, glast, a_mat, t_mat, kv16, qk_even = {}, {}, {}, {}, {}, {}, {}
    for key in pairs:
        c, h = key
        g_full = {}
        for d in range(N_DIR):
            j = d * GDN_HEADS + h
            r_last = c * CHUNK + (CHUNK - 1 if d == 0 else 0)
            g_full[d] = bcast(cs_col[d][rows(c), j:j + 1])
            brow[key, d] = beta_t[SUBLANES + j:SUBLANES + j + 1, rows(c)]
            glast[key, d] = cs_col[d][r_last:r_last + 1, j:j + 1]
            grow[key, d] = cs_row[d][j:j + 1, rows(c)]
        gcol2 = jnp.where(is_fwd, g_full[0], g_full[1])
        grow2 = jnp.concatenate([grow[key, 0], grow[key, 1]], axis=1)
        brow2 = jnp.concatenate([brow[key, 0], brow[key, 1]], axis=1)
        decay = jnp.exp2(jnp.where(incl, gcol2 - grow2, -jnp.inf))
        a_mat[key] = jnp.where(strict, qkk[key][CHUNK:] * decay, 0.0) * brow2
        t_mat[key] = eye - jnp.where(level_mask(1, r_c, c_c), a_mat[key], 0.0)
        qk = qkk[key][:CHUNK] * decay * brow2
        if c % 2 == 0:
            qk_even[h] = qk
        else:
            pair_rows = rows(c // 2)
            qkf_ref[0, h, pair_rows, :] = jnp.where(is_fwd, qk_even[h], pltpu.roll(qk, CHUNK, 1)).astype(qkf_ref.dtype)
            qkb_ref[0, h, pair_rows, :] = jnp.where(is_fwd, pltpu.roll(qk_even[h], CHUNK, 1), qk).astype(qkb_ref.dtype)
        eg = [jnp.exp2(g_full[d]) for d in range(N_DIR)]
        for d in range(N_DIR):
            qd_ref[0, d * GDN_HEADS + h, rows(c), :] = (q_at(h, rows(c)) * eg[d]).astype(qd_ref.dtype)
        kv16[key] = jnp.concatenate(
            [jnp.concatenate([k_at(h, rows(c)) * eg[d], v_at(h, rows(c))], axis=1) for d in range(N_DIR)],
            axis=0).astype(BF16)

    same_dir16 = jnp.where(same_dir, 1.0, 0.0).astype(BF16)
    block_diag = lambda t16: stack2(t16) * same_dir16
    a_bd = {key: block_diag(a_mat[key].astype(BF16)) for key in pairs}
    s = 2
    while s < CHUNK:
        lvl = level_mask(s, r_c, c_c)
        t16 = {key: t_mat[key].astype(BF16) for key in pairs}
        x_mat = {key: jnp.dot(t16[key], a_bd[key], preferred_element_type=F32) for key in pairs}
        y_mat = {key: jnp.dot(x_mat[key].astype(BF16), block_diag(t16[key]), preferred_element_type=F32)
                 for key in pairs}
        t_mat = {key: t_mat[key] - jnp.where(lvl, y_mat[key], 0.0) for key in pairs}
        s *= 2

    wu = {}
    for key in pairs:
        c, h = key
        t_sel = block_diag(t_mat[key].astype(BF16))
        wu[key] = jnp.dot(t_sel, kv16[key], preferred_element_type=F32)

    for c in range(n_chunks):
        egl_rows = []
        for d in range(N_DIR):
            for h in range(GDN_HEADS):
                key = (c, h)
                j = d * GDN_HEADS + h
                wu_d = wu[key][d * CHUNK:(d + 1) * CHUNK]
                w_ref[0, j, rows(c), :] = wu_d[:, :GDN_HEAD_DIM].astype(w_ref.dtype)
                u_ref[0, j, rows(c), :] = wu_d[:, GDN_HEAD_DIM:].astype(u_ref.dtype)
                egl_rows.append(jnp.broadcast_to(jnp.exp2(glast[key, d]), (1, LANES)))
        egl_ref[0, c] = jnp.concatenate(egl_rows, axis=0)
    for cp in range(n_chunks // 2):
        for h in range(GDN_HEADS):
            k_t = k_at(h, slice(2 * cp * CHUNK, (2 * cp + 2) * CHUNK)).T
            for d in range(N_DIR):
                fac = jnp.concatenate([brow[(c, h), d] * jnp.exp2(glast[(c, h), d] - grow[(c, h), d])
                                       for c in (2 * cp, 2 * cp + 1)], axis=1)
                kdt_ref[0, d * GDN_HEADS + h, cp] = (k_t * fac).astype(kdt_ref.dtype)


def _gdn_prep(yn, ab, gate_par):
    b, s, _ = ab.shape
    nt = s // PREP_T
    cpb = PREP_T // CHUNK
    nc = s // CHUNK
    chain = lambda last: pl.BlockSpec((1, N_CHAIN, PREP_T, last), lambda bi, i: (bi, 0, i, 0))
    return pl.pallas_call(
        _gprep_kernel,
        grid=(b, nt),
        in_specs=[pl.BlockSpec((3 * GDN_W // LANES, PREP_T, LANES), lambda bi, i: (0, bi * nt + i, 0)),
                  pl.BlockSpec((1, PREP_T, LANES), lambda bi, i: (bi, i, 0)),
                  pl.BlockSpec((SUBLANES, LANES), lambda bi, i: (0, 0))],
        out_specs=[chain(GDN_HEAD_DIM), chain(GDN_HEAD_DIM), chain(GDN_HEAD_DIM),
                   pl.BlockSpec((1, N_CHAIN, cpb // 2, GDN_HEAD_DIM, 2 * CHUNK), lambda bi, i: (bi, 0, i, 0, 0)),
                   pl.BlockSpec((1, GDN_HEADS, PREP_T // 2, LANES), lambda bi, i: (bi, 0, i, 0)),
                   pl.BlockSpec((1, GDN_HEADS, PREP_T // 2, LANES), lambda bi, i: (bi, 0, i, 0)),
                   pl.BlockSpec((1, cpb, N_CHAIN, LANES), lambda bi, i: (bi, i, 0, 0))],
        out_shape=[jax.ShapeDtypeStruct((b, N_CHAIN, s, GDN_HEAD_DIM), BF16),
                   jax.ShapeDtypeStruct((b, N_CHAIN, s, GDN_HEAD_DIM), BF16),
                   jax.ShapeDtypeStruct((b, N_CHAIN, s, GDN_HEAD_DIM), BF16),
                   jax.ShapeDtypeStruct((b, N_CHAIN, nc // 2, GDN_HEAD_DIM, 2 * CHUNK), BF16),
                   jax.ShapeDtypeStruct((b, GDN_HEADS, s // 2, LANES), BF16),
                   jax.ShapeDtypeStruct((b, GDN_HEADS, s // 2, LANES), BF16),
                   jax.ShapeDtypeStruct((b, nc, N_CHAIN, LANES), F32)],
        compiler_params=pltpu.CompilerParams(dimension_semantics=("arbitrary", "arbitrary"),
                                             vmem_limit_bytes=VMEM_LIMIT),
        name="gdn_prep",
    )(yn, ab, gate_par)


def _gscan_kernel(ef_ref, eb_ref, wf_ref, wb_ref, uf_ref, ub_ref, qf_ref, qb_ref, kf_ref, kb_ref,
                  pf_ref, pb_ref, of_ref, ob_ref, state_ref):
    n_batch = wf_ref.shape[0]

    @pl.when(pl.program_id(0) == 0)
    def _():
        state_ref[...] = jnp.zeros_like(state_ref)

    dirs = ((wf_ref, uf_ref, qf_ref, kf_ref, pf_ref, of_ref), (wb_ref, ub_ref, qb_ref, kb_ref, pb_ref, ob_ref))
    chains = [(bi, d, h) for bi in range(n_batch) for d in range(N_DIR) for h in range(GDN_HEADS)]
    slot = lambda bi, d, h: (bi * N_DIR + d) * GDN_HEADS + h
    st = {key: state_ref[slot(*key)] for key in chains}

    for sub in range(SCAN_CHUNKS):
        local = (sub, SCAN_CHUNKS - 1 - sub)
        rows = [slice(c * CHUNK, (c + 1) * CHUNK) for c in local]
        r = {}
        for key in chains:
            bi, d, h = key
            w_ref, _, q_ref = dirs[d][:3]
            wq = jnp.concatenate([w_ref[bi, h, rows[d]], q_ref[bi, h, rows[d]]], axis=0)
            r[key] = _dot(wq, st[key])
        intra = {}
        for key in chains:
            bi, d, h = key
            u_ref, k_ref, p_ref = dirs[d][1], dirs[d][3], dirs[d][4]
            v_new = (u_ref[bi, h, rows[d]].astype(F32) - r[key][:CHUNK]).astype(BF16)
            zeros = jnp.zeros_like(v_new)
            v_pad = jnp.concatenate([v_new, zeros] if local[d] % 2 == 0 else [zeros, v_new], axis=0)
            pair_rows = slice(local[d] // 2 * CHUNK, (local[d] // 2 + 1) * CHUNK)
            pk = jnp.concatenate([p_ref[bi, h, pair_rows], k_ref[bi, h, local[d] // 2]], axis=0)
            res = jnp.dot(pk, v_pad, preferred_element_type=F32)
            intra[key] = res[:CHUNK]
            j = d * GDN_HEADS + h
            egl = (ef_ref, eb_ref)[d][bi, local[d], j:j + 1, :]
            st[key] = st[key] * egl + res[CHUNK:]
        for bi in range(n_batch):
            for d in range(N_DIR):
                o_ref = dirs[d][5]
                o_ref[bi, rows[d], :] = jnp.concatenate(
                    [r[bi, d, h][CHUNK:] + intra[bi, d, h] for h in range(GDN_HEADS)], axis=-1).astype(o_ref.dtype)

    for key in chains:
        state_ref[slot(*key)] = st[key]


def _gdn_scan(egl, w, u, qd, kdt, qk_f, qk_b):
    b, _, s, _ = w.shape
    rows = SCAN_CHUNKS * CHUNK
    n_steps = s // rows
    fwd_i = lambda t: t
    bwd_i = lambda t: n_steps - 1 - t
    chain = lambda d, at, last: pl.BlockSpec((b, GDN_HEADS, rows, last), lambda t: (0, d, at(t), 0))
    kspec = lambda d, at: pl.BlockSpec((b, GDN_HEADS, SCAN_CHUNKS // 2, GDN_HEAD_DIM, 2 * CHUNK),
                                       lambda t: (0, d, at(t), 0, 0))
    pspec = lambda at: pl.BlockSpec((b, GDN_HEADS, rows // 2, LANES), lambda t: (0, 0, at(t), 0))
    espec = lambda at: pl.BlockSpec((b, SCAN_CHUNKS, N_CHAIN, LANES), lambda t: (0, at(t), 0, 0))
    dk = GDN_HEAD_DIM
    return pl.pallas_call(
        _gscan_kernel,
        grid=(n_steps,),
        in_specs=[espec(fwd_i), espec(bwd_i),
                  chain(0, fwd_i, dk), chain(1, bwd_i, dk), chain(0, fwd_i, dk), chain(1, bwd_i, dk),
                  chain(0, fwd_i, dk), chain(1, bwd_i, dk), kspec(0, fwd_i), kspec(1, bwd_i),
                  pspec(fwd_i), pspec(bwd_i)],
        out_specs=[pl.BlockSpec((b, rows, GDN_W), lambda t: (0, fwd_i(t), 0)),
                   pl.BlockSpec((b, rows, GDN_W), lambda t: (0, bwd_i(t), 0))],
        out_shape=[jax.ShapeDtypeStruct((b, s, GDN_W), BF16),
                   jax.ShapeDtypeStruct((b, s, GDN_W), BF16)],
        scratch_shapes=[pltpu.VMEM((b * N_CHAIN, GDN_HEAD_DIM, GDN_HEAD_DIM), F32)],
        compiler_params=pltpu.CompilerParams(dimension_semantics=("arbitrary",),
                                             vmem_limit_bytes=VMEM_LIMIT),
        name="gdn_scan",
    )(egl, egl, w, w, u, u, qd, qd, kdt, kdt, qk_f, qk_b)


def _ffn_kernel(x_ref, attn_ref, of_ref, ob_ref, z_ref, gnw_ref, wo_hbm, fnw_ref, w1_hbm, w2_hbm, onw_ref,
                o_ref, wo_ref, w1_ref, w2_ref, sem):
    @pl.when(pl.program_id(0) == 0)
    def _():
        copies = [pltpu.make_async_copy(src, dst, sem.at[k])
                  for k, (src, dst) in enumerate(((wo_hbm, wo_ref), (w1_hbm, w1_ref), (w2_hbm, w2_ref)))]
        for copy in copies:
            copy.start()
        for copy in copies:
            copy.wait()

    subs = [slice(r0, r0 + FFN_SUB) for r0 in range(0, FFN_TM, FFN_SUB)]
    rms = lambda t: t * lax.rsqrt(jnp.mean(t * t, axis=-1, keepdims=True) + EPS)
    hres, hn, act, acc = {}, {}, {}, {}
    for r in subs:
        o = of_ref[r, :].astype(F32) + ob_ref[r, :].astype(F32)
        heads = [rms(o[:, h * GDN_HEAD_DIM:(h + 1) * GDN_HEAD_DIM]) * gnw_ref[...] for h in range(GDN_HEADS)]
        z = z_ref[r, :]
        gdn = jnp.concatenate(heads, axis=-1) * (z * jax.nn.sigmoid(z))
        hres[r.start] = (x_ref[r, :] + jnp.dot(attn_ref[r, :], wo_ref[:ATTN_Q, :], preferred_element_type=F32)
                         + _dot(gdn, wo_ref[ATTN_Q:, :]))
    for r in subs:
        hn[r.start] = (rms(hres[r.start]) * fnw_ref[...]).astype(BF16)
        act[r.start] = jnp.dot(hn[r.start], w1_ref[...], preferred_element_type=F32)
    for r in subs:
        a = jnp.square(jnp.maximum(act[r.start], 0.0)).astype(BF16)
        acc[r.start] = hres[r.start] + jnp.dot(a, w2_ref[...], preferred_element_type=F32)
    for r in subs:
        o_ref[r, :] = rms(acc[r.start]) * onw_ref[...]


def _out_ffn(x2, attn, o_f, o_b, z, gnw, wo, fnw, w1, w2, onw):
    n = x2.shape[0]
    assert wo.shape == (D_MODEL, D_MODEL) and w1.shape == (D_MODEL, D_FF) and w2.shape == (D_FF, D_MODEL)
    row = lambda w: pl.BlockSpec((FFN_TM, w), lambda i: (i, 0))
    full = lambda a: pl.BlockSpec(a.shape, lambda i: (0, 0))
    hbm = pl.BlockSpec(memory_space=pl.ANY)
    return pl.pallas_call(
        _ffn_kernel,
        grid=(n // FFN_TM,),
        in_specs=[row(D_MODEL), row(ATTN_Q), row(GDN_W), row(GDN_W), row(GDN_W),
                  full(gnw), hbm, full(fnw), hbm, hbm, full(onw)],
        out_specs=row(D_MODEL),
        out_shape=jax.ShapeDtypeStruct((n, D_MODEL), F32),
        scratch_shapes=[pltpu.VMEM((D_MODEL, D_MODEL), BF16),
                        pltpu.VMEM((D_MODEL, D_FF), BF16),
                        pltpu.VMEM((D_FF, D_MODEL), BF16),
                        pltpu.SemaphoreType.DMA((3,))],
        compiler_params=pltpu.CompilerParams(dimension_semantics=("arbitrary",),
                                             vmem_limit_bytes=VMEM_LIMIT),
        name="out_ffn",
    )(x2, attn, o_f, o_b, z, gnw, wo, fnw, w1, w2, onw)


def _layer(h, band, norm_mix_w, w_in, layer, attn_sink, conv_w, gdn_a_log, gdn_dt_bias, gdn_norm_w,
           w_out, norm_ffn_w, w_ffn_in, w_ffn_out, out_norm_w):
    b, s, _ = h.shape
    n = b * s
    x2 = h.reshape(n, D_MODEL)
    q_a, k_a, v_a, yn, z_g, ab = _proj(x2, norm_mix_w.reshape(1, D_MODEL), w_in, conv_w, layer, s)

    attn, wo_bf, w1_bf, w2_bf = _attention(q_a, k_a, v_a, band, attn_sink, b, layer, (w_out, w_ffn_in, w_ffn_out))

    gate_par = jnp.zeros((SUBLANES, LANES), F32)
    gate_par = gate_par.at[0, :N_CHAIN].set(gdn_a_log.reshape(-1)).at[1, :N_CHAIN].set(gdn_dt_bias.reshape(-1))
    w_c, u_c, q_dec, k_dec_t, qk_f, qk_b, egl = _gdn_prep(yn, ab.reshape(b, s, LANES), gate_par)
    o_f, o_b = _gdn_scan(egl, w_c, u_c, q_dec, k_dec_t, qk_f, qk_b)

    out = _out_ffn(x2, attn, o_f.reshape(n, GDN_W), o_b.reshape(n, GDN_W), z_g,
                   gdn_norm_w.reshape(1, GDN_HEAD_DIM), wo_bf, norm_ffn_w.reshape(1, D_MODEL),
                   w1_bf, w2_bf, out_norm_w.reshape(1, D_MODEL))
    return out.reshape(b, s, D_MODEL)


def kernel(x, norm_mix_w, w_in, rel_bias, attn_sink, conv_w, gdn_a_log, gdn_dt_bias, gdn_norm_w, w_out,
           norm_ffn_w, w_ffn_in, w_ffn_out, norm_final_w):
    depth = w_in.shape[0]
    assert depth == 1, "the fused output kernel applies the final norm after the single trunk layer"
    rel = (np.arange(3 * BLOCK)[None, :] - BLOCK) - np.arange(BLOCK)[:, None]
    bucket = _t5_buckets(jnp.asarray(rel, dtype=jnp.int32))
    band = _bias_band(rel_bias.T, bucket.T)
    return _layer(x, band, norm_mix_w[0], w_in, 0, attn_sink[0], conv_w, gdn_a_log[0], gdn_dt_bias[0],
                  gdn_norm_w[0], w_out, norm_ffn_w[0], w_ffn_in, w_ffn_out, norm_final_w)
```
